```python
import math
import jax, jax.numpy as jnp
from jax import lax
import numpy as np

D_MODEL = 1024
BATCH = 8
SEQ = 16384
DEPTH = 1

N_MEM = 256
D_MIX = D_MODEL
D_A = D_MIX // 2
D_B = D_MIX - D_A
CONV_A_W = 3
CONV_B_W = 31
D_IN_ALL = 3 * D_A + 2 * D_B
XA_HEADS = 4
XA_HEAD_DIM = D_MODEL // XA_HEADS
D_FF = int(math.ceil((8 * D_MODEL / 3) / 256) * 256)
RMS_EPS = 1e-6
LN_EPS = 1e-5

kernel_name = "hybrid_parallel_conv_groups_xattn_swiglu"


def rmsnorm(x, g):
    xf = x.astype(jnp.float32)
    y = xf * lax.rsqrt(jnp.mean(xf * xf, axis=-1, keepdims=True) + RMS_EPS)
    return (y * g.astype(jnp.float32)).astype(x.dtype)


def layernorm(x, g, b):
    xf = x.astype(jnp.float32)
    mu = jnp.mean(xf, axis=-1, keepdims=True)
    var = jnp.mean(jnp.square(xf - mu), axis=-1, keepdims=True)
    y = (xf - mu) * lax.rsqrt(var + LN_EPS)
    return (y * g.astype(jnp.float32) + b.astype(jnp.float32)).astype(x.dtype)


def causal_dwconv(u, w):
    k = w.shape[0]
    return lax.conv_general_dilated(
        u, w[:, None, :].astype(u.dtype),
        window_strides=(1,), padding=((k - 1, 0),),
        dimension_numbers=("NWC", "WIO", "NWC"),
        feature_group_count=u.shape[-1])


def _fwd_setup_inputs(seed: int = 0) -> dict:
    key = jax.random.key(seed)
    ks = jax.random.split(key, 24)
    f32 = jnp.float32

    def w(k, shape, fan_in):
        return jax.random.normal(k, shape, f32) * (fan_in ** -0.5)

    def gain(k, n):
        return jnp.ones((n,), f32) + 0.05 * jax.random.normal(k, (n,), f32)

    return {
        "x": jax.random.normal(ks[0], (BATCH, SEQ, D_MODEL), f32),
        "mem": jax.random.normal(ks[1], (BATCH, N_MEM, D_MODEL), f32),
        "mix_pre_g": gain(ks[2], D_MODEL),
        "w_mix_in": w(ks[3], (D_MODEL, D_IN_ALL), D_MODEL),
        "conv_a_w": w(ks[4], (CONV_A_W, D_A), CONV_A_W),
        "conv_b_w": w(ks[5], (CONV_B_W, D_B), CONV_B_W),
        "conv_b_b": 0.02 * jax.random.normal(ks[6], (D_B,), f32),
        "ln_b_g": gain(ks[7], D_B),
        "ln_b_b": 0.02 * jax.random.normal(ks[8], (D_B,), f32),
        "w_mix_out": w(ks[9], (D_MIX, D_MODEL), D_MIX),
        "mix_post_g": gain(ks[10], D_MODEL),
        "xa_pre_g": gain(ks[11], D_MODEL),
        "mem_norm_g": gain(ks[12], D_MODEL),
        "w_q": w(ks[13], (D_MODEL, XA_HEADS * XA_HEAD_DIM), D_MODEL),
        "w_k": w(ks[14], (D_MODEL, XA_HEADS * XA_HEAD_DIM), D_MODEL),
        "w_v": w(ks[15], (D_MODEL, XA_HEADS * XA_HEAD_DIM), D_MODEL),
        "w_o": w(ks[16], (XA_HEADS * XA_HEAD_DIM, D_MODEL), XA_HEADS * XA_HEAD_DIM),
        "xa_post_g": gain(ks[17], D_MODEL),
        "ffn_pre_g": gain(ks[18], D_MODEL),
        "w_gate": w(ks[19], (D_MODEL, D_FF), D_MODEL),
        "w_up": w(ks[20], (D_MODEL, D_FF), D_MODEL),
        "w_down": w(ks[21], (D_FF, D_MODEL), D_FF),
        "ffn_post_g": gain(ks[22], D_MODEL),
    }


def parallel_conv_mixer(h, w_mix_in, conv_a_w, conv_b_w, conv_b_b, ln_b_g, ln_b_b, w_mix_out):
    u = jnp.einsum("bsd,dc->bsc", h, w_mix_in.astype(h.dtype))
    b_a, c_a, v_a, glu_v, glu_g = jnp.split(
        u, [D_A, 2 * D_A, 3 * D_A, 3 * D_A + D_B], axis=-1)
    y_a = b_a * causal_dwconv(c_a * v_a, conv_a_w)
    z = glu_v * jax.nn.sigmoid(glu_g)
    z = causal_dwconv(z, conv_b_w) + conv_b_b.astype(z.dtype)
    y_b = jax.nn.silu(layernorm(z, ln_b_g, ln_b_b))
    y = jnp.concatenate([y_a, y_b], axis=-1)
    return jnp.einsum("bsc,cd->bsd", y, w_mix_out.astype(y.dtype))


def memory_cross_attention(h, mem_n, w_q, w_k, w_v, w_o):
    bsz, s, _ = h.shape
    m = mem_n.shape[1]
    q = jnp.einsum("bsd,de->bse", h, w_q.astype(h.dtype)).reshape(bsz, s, XA_HEADS, XA_HEAD_DIM)
    k = jnp.einsum("bmd,de->bme", mem_n, w_k.astype(h.dtype)).reshape(bsz, m, XA_HEADS, XA_HEAD_DIM)
    v = jnp.einsum("bmd,de->bme", mem_n, w_v.astype(h.dtype)).reshape(bsz, m, XA_HEADS, XA_HEAD_DIM)
    scores = jnp.einsum("bshd,bmhd->bhsm", q.astype(jnp.float32), k.astype(jnp.float32))
    p = jax.nn.softmax(scores * (XA_HEAD_DIM ** -0.5), axis=-1).astype(h.dtype)
    o = jnp.einsum("bhsm,bmhd->bshd", p, v).reshape(bsz, s, XA_HEADS * XA_HEAD_DIM)
    return jnp.einsum("bse,ed->bsd", o, w_o.astype(h.dtype))


def swiglu_ffn(h, w_gate, w_up, w_down):
    g = jnp.einsum("bsd,df->bsf", h, w_gate.astype(h.dtype))
    u = jnp.einsum("bsd,df->bsf", h, w_up.astype(h.dtype))
    return jnp.einsum("bsf,fd->bsd", jax.nn.silu(g) * u, w_down.astype(h.dtype))


def _fwd_reference(x, mem, mix_pre_g, w_mix_in, conv_a_w, conv_b_w, conv_b_b, ln_b_g, ln_b_b,
              w_mix_out, mix_post_g, xa_pre_g, mem_norm_g, w_q, w_k, w_v, w_o, xa_post_g,
              ffn_pre_g, w_gate, w_up, w_down, ffn_post_g):
    mem_n = rmsnorm(mem, mem_norm_g)
    for _ in range(DEPTH):
        x = x + rmsnorm(parallel_conv_mixer(rmsnorm(x, mix_pre_g), w_mix_in, conv_a_w,
                                            conv_b_w, conv_b_b, ln_b_g, ln_b_b, w_mix_out),
                        mix_post_g)
        x = x + rmsnorm(memory_cross_attention(rmsnorm(x, xa_pre_g), mem_n, w_q, w_k, w_v, w_o),
                        xa_post_g)
        x = x + rmsnorm(swiglu_ffn(rmsnorm(x, ffn_pre_g), w_gate, w_up, w_down), ffn_post_g)
    return x


import jax as _jax
import jax.numpy as _jnp

TWIN_FORMAT = 'train_step'
FWD_PARAMS = ['x', 'mem', 'mix_pre_g', 'w_mix_in', 'conv_a_w', 'conv_b_w', 'conv_b_b', 'ln_b_g', 'ln_b_b', 'w_mix_out', 'mix_post_g', 'xa_pre_g', 'mem_norm_g', 'w_q', 'w_k', 'w_v', 'w_o', 'xa_post_g', 'ffn_pre_g', 'w_gate', 'w_up', 'w_down', 'ffn_post_g']
TWIN_WEIGHTS = ['mix_pre_g', 'w_mix_in', 'conv_a_w', 'conv_b_w', 'conv_b_b', 'ln_b_g', 'ln_b_b', 'w_mix_out', 'mix_post_g', 'xa_pre_g', 'mem_norm_g', 'w_q', 'w_k', 'w_v', 'w_o', 'xa_post_g', 'ffn_pre_g', 'w_gate', 'w_up', 'w_down', 'ffn_post_g']
TWIN_DIFF_INPUT = 'x'
TWIN_INPUTS = ['x', 'mem', 'mix_pre_g', 'w_mix_in', 'conv_a_w', 'conv_b_w', 'conv_b_b', 'ln_b_g', 'ln_b_b', 'w_mix_out', 'mix_post_g', 'xa_pre_g', 'mem_norm_g', 'w_q', 'w_k', 'w_v', 'w_o', 'xa_post_g', 'ffn_pre_g', 'w_gate', 'w_up', 'w_down', 'ffn_post_g', 'loss_target', 'm_mix_pre_g', 'm_w_mix_in', 'm_conv_a_w', 'm_conv_b_w', 'm_conv_b_b', 'm_ln_b_g', 'm_ln_b_b', 'm_w_mix_out', 'm_mix_post_g', 'm_xa_pre_g', 'm_mem_norm_g', 'm_w_q', 'm_w_k', 'm_w_v', 'm_w_o', 'm_xa_post_g', 'm_ffn_pre_g', 'm_w_gate', 'm_w_up', 'm_w_down', 'm_ffn_post_g', 'v_mix_pre_g', 'v_w_mix_in', 'v_conv_a_w', 'v_conv_b_w', 'v_conv_b_b', 'v_ln_b_g', 'v_ln_b_b', 'v_w_mix_out', 'v_mix_post_g', 'v_xa_pre_g', 'v_mem_norm_g', 'v_w_q', 'v_w_k', 'v_w_v', 'v_w_o', 'v_xa_post_g', 'v_ffn_pre_g', 'v_w_gate', 'v_w_up', 'v_w_down', 'v_ffn_post_g']
TWIN_OUTPUTS = ['loss', 'grad_x', 'grad_mix_pre_g', 'grad_w_mix_in', 'grad_conv_a_w', 'grad_conv_b_w', 'grad_conv_b_b', 'grad_ln_b_g', 'grad_ln_b_b', 'grad_w_mix_out', 'grad_mix_post_g', 'grad_xa_pre_g', 'grad_mem_norm_g', 'grad_w_q', 'grad_w_k', 'grad_w_v', 'grad_w_o', 'grad_xa_post_g', 'grad_ffn_pre_g', 'grad_w_gate', 'grad_w_up', 'grad_w_down', 'grad_ffn_post_g', 'delta_mix_pre_g', 'delta_w_mix_in', 'delta_conv_a_w', 'delta_conv_b_w', 'delta_conv_b_b', 'delta_ln_b_g', 'delta_ln_b_b', 'delta_w_mix_out', 'delta_mix_post_g', 'delta_xa_pre_g', 'delta_mem_norm_g', 'delta_w_q', 'delta_w_k', 'delta_w_v', 'delta_w_o', 'delta_xa_post_g', 'delta_ffn_pre_g', 'delta_w_gate', 'delta_w_up', 'delta_w_down', 'delta_ffn_post_g', 'new_m_mix_pre_g', 'new_m_w_mix_in', 'new_m_conv_a_w', 'new_m_conv_b_w', 'new_m_conv_b_b', 'new_m_ln_b_g', 'new_m_ln_b_b', 'new_m_w_mix_out', 'new_m_mix_post_g', 'new_m_xa_pre_g', 'new_m_mem_norm_g', 'new_m_w_q', 'new_m_w_k', 'new_m_w_v', 'new_m_w_o', 'new_m_xa_post_g', 'new_m_ffn_pre_g', 'new_m_w_gate', 'new_m_w_up', 'new_m_w_down', 'new_m_ffn_post_g', 'new_v_mix_pre_g', 'new_v_w_mix_in', 'new_v_conv_a_w', 'new_v_conv_b_w', 'new_v_conv_b_b', 'new_v_ln_b_g', 'new_v_ln_b_b', 'new_v_w_mix_out', 'new_v_mix_post_g', 'new_v_xa_pre_g', 'new_v_mem_norm_g', 'new_v_w_q', 'new_v_w_k', 'new_v_w_v', 'new_v_w_o', 'new_v_xa_post_g', 'new_v_ffn_pre_g', 'new_v_w_gate', 'new_v_w_up', 'new_v_w_down', 'new_v_ffn_post_g']
TWIN_LEAF_KINDS = {'loss': 'loss', 'grad_x': 'grad_x', 'grad_mix_pre_g': 'grad_w', 'grad_w_mix_in': 'grad_w', 'grad_conv_a_w': 'grad_w', 'grad_conv_b_w': 'grad_w', 'grad_conv_b_b': 'grad_w', 'grad_ln_b_g': 'grad_w', 'grad_ln_b_b': 'grad_w', 'grad_w_mix_out': 'grad_w', 'grad_mix_post_g': 'grad_w', 'grad_xa_pre_g': 'grad_w', 'grad_mem_norm_g': 'grad_w', 'grad_w_q': 'grad_w', 'grad_w_k': 'grad_w', 'grad_w_v': 'grad_w', 'grad_w_o': 'grad_w', 'grad_xa_post_g': 'grad_w', 'grad_ffn_pre_g': 'grad_w', 'grad_w_gate': 'grad_w', 'grad_w_up': 'grad_w', 'grad_w_down': 'grad_w', 'grad_ffn_post_g': 'grad_w', 'delta_mix_pre_g': 'delta_w', 'delta_w_mix_in': 'delta_w', 'delta_conv_a_w': 'delta_w', 'delta_conv_b_w': 'delta_w', 'delta_conv_b_b': 'delta_w', 'delta_ln_b_g': 'delta_w', 'delta_ln_b_b': 'delta_w', 'delta_w_mix_out': 'delta_w', 'delta_mix_post_g': 'delta_w', 'delta_xa_pre_g': 'delta_w', 'delta_mem_norm_g': 'delta_w', 'delta_w_q': 'delta_w', 'delta_w_k': 'delta_w', 'delta_w_v': 'delta_w', 'delta_w_o': 'delta_w', 'delta_xa_post_g': 'delta_w', 'delta_ffn_pre_g': 'delta_w', 'delta_w_gate': 'delta_w', 'delta_w_up': 'delta_w', 'delta_w_down': 'delta_w', 'delta_ffn_post_g': 'delta_w', 'new_m_mix_pre_g': 'new_m', 'new_m_w_mix_in': 'new_m', 'new_m_conv_a_w': 'new_m', 'new_m_conv_b_w': 'new_m', 'new_m_conv_b_b': 'new_m', 'new_m_ln_b_g': 'new_m', 'new_m_ln_b_b': 'new_m', 'new_m_w_mix_out': 'new_m', 'new_m_mix_post_g': 'new_m', 'new_m_xa_pre_g': 'new_m', 'new_m_mem_norm_g': 'new_m', 'new_m_w_q': 'new_m', 'new_m_w_k': 'new_m', 'new_m_w_v': 'new_m', 'new_m_w_o': 'new_m', 'new_m_xa_post_g': 'new_m', 'new_m_ffn_pre_g': 'new_m', 'new_m_w_gate': 'new_m', 'new_m_w_up': 'new_m', 'new_m_w_down': 'new_m', 'new_m_ffn_post_g': 'new_m', 'new_v_mix_pre_g': 'new_v', 'new_v_w_mix_in': 'new_v', 'new_v_conv_a_w': 'new_v', 'new_v_conv_b_w': 'new_v', 'new_v_conv_b_b': 'new_v', 'new_v_ln_b_g': 'new_v', 'new_v_ln_b_b': 'new_v', 'new_v_w_mix_out': 'new_v', 'new_v_mix_post_g': 'new_v', 'new_v_xa_pre_g': 'new_v', 'new_v_mem_norm_g': 'new_v', 'new_v_w_q': 'new_v', 'new_v_w_k': 'new_v', 'new_v_w_v': 'new_v', 'new_v_w_o': 'new_v', 'new_v_xa_post_g': 'new_v', 'new_v_ffn_pre_g': 'new_v', 'new_v_w_gate': 'new_v', 'new_v_w_up': 'new_v', 'new_v_w_down': 'new_v', 'new_v_ffn_post_g': 'new_v'}


def _forward(args):
    return _fwd_reference(*[args[k] for k in FWD_PARAMS])


def _output_shape():
    def fwd():
        inp = _fwd_setup_inputs(0)
        return _fwd_reference(*[inp[k] for k in FWD_PARAMS])
    out = _jax.eval_shape(fwd)
    return out.shape, out.dtype

N_MICROBATCH = 1
ADAM_LR = 0.001
ADAM_B1 = 0.9
ADAM_B2 = 0.999
ADAM_EPS = 1e-08
ADAM_WD = 0.01
ADAM_STEP = 10
PER_EXAMPLE_BATCH_AXIS = {'x': 0, 'mem': 0, 'loss_target': 0}
SHARED_INPUTS = []
_WEIGHT_DTYPES = {'mix_pre_g': _jnp.float32, 'w_mix_in': _jnp.float32, 'conv_a_w': _jnp.float32, 'conv_b_w': _jnp.float32, 'conv_b_b': _jnp.float32, 'ln_b_g': _jnp.float32, 'ln_b_b': _jnp.float32, 'w_mix_out': _jnp.float32, 'mix_post_g': _jnp.float32, 'xa_pre_g': _jnp.float32, 'mem_norm_g': _jnp.float32, 'w_q': _jnp.float32, 'w_k': _jnp.float32, 'w_v': _jnp.float32, 'w_o': _jnp.float32, 'xa_post_g': _jnp.float32, 'ffn_pre_g': _jnp.float32, 'w_gate': _jnp.float32, 'w_up': _jnp.float32, 'w_down': _jnp.float32, 'ffn_post_g': _jnp.float32}
MOMENT_SCALE = {'mix_pre_g': 1.953352e+00, 'w_mix_in': 1.143556e+00, 'conv_a_w': 1.590669e+00, 'conv_b_w': 1.385385e+00, 'conv_b_b': 2.927624e+01, 'ln_b_g': 1.136529e+01, 'ln_b_b': 1.824680e+01, 'w_mix_out': 4.039391e+00, 'mix_post_g': 1.284164e+02, 'xa_pre_g': 1.831751e+00, 'mem_norm_g': 8.399042e+00, 'w_q': 1.732554e+00, 'w_k': 1.748525e+00, 'w_v': 6.917166e+00, 'w_o': 7.090727e+00, 'xa_post_g': 1.298235e+02, 'ffn_pre_g': 4.725067e+00, 'w_gate': 1.399661e+00, 'w_up': 2.559402e+00, 'w_down': 4.324510e+00, 'ffn_post_g': 1.271392e+02}


def _to_microbatches(a, axis):
    t = _jnp.moveaxis(a, axis, 0)
    t = t.reshape((N_MICROBATCH, t.shape[0] // N_MICROBATCH) + t.shape[1:])
    return _jnp.moveaxis(t, 1, axis + 1)


def setup_inputs(seed: int = 0) -> dict:
    inp = _fwd_setup_inputs(seed)
    key = _jax.random.fold_in(_jax.random.key(seed), 7919)
    shape, _ = _output_shape()
    out = dict(inp)
    out["loss_target"] = _jax.random.normal(_jax.random.fold_in(key, 0), shape, _jnp.float32)
    for i, name in enumerate(TWIN_WEIGHTS):
        w = inp[name].astype(_jnp.float32)
        if MOMENT_SCALE is None:
            s = _jnp.sqrt(_jnp.mean(_jnp.square(w)) + 1e-30)
        else:
            s = MOMENT_SCALE[name]
        km, kv = _jax.random.split(_jax.random.fold_in(key, i + 1))
        out[name] = w
        out["m_" + name] = s * _jax.random.normal(km, w.shape, _jnp.float32)
        out["v_" + name] = (s * s) * _jax.random.uniform(kv, w.shape, _jnp.float32, 0.5, 1.5)
    if N_MICROBATCH > 1:
        for name, axis in PER_EXAMPLE_BATCH_AXIS.items():
            out[name] = _to_microbatches(out[name], axis)
    return {'x': out['x'], 'mem': out['mem'], 'mix_pre_g': out['mix_pre_g'], 'w_mix_in': out['w_mix_in'], 'conv_a_w': out['conv_a_w'], 'conv_b_w': out['conv_b_w'], 'conv_b_b': out['conv_b_b'], 'ln_b_g': out['ln_b_g'], 'ln_b_b': out['ln_b_b'], 'w_mix_out': out['w_mix_out'], 'mix_post_g': out['mix_post_g'], 'xa_pre_g': out['xa_pre_g'], 'mem_norm_g': out['mem_norm_g'], 'w_q': out['w_q'], 'w_k': out['w_k'], 'w_v': out['w_v'], 'w_o': out['w_o'], 'xa_post_g': out['xa_post_g'], 'ffn_pre_g': out['ffn_pre_g'], 'w_gate': out['w_gate'], 'w_up': out['w_up'], 'w_down': out['w_down'], 'ffn_post_g': out['ffn_post_g'], 'loss_target': out['loss_target'], 'm_mix_pre_g': out['m_mix_pre_g'], 'm_w_mix_in': out['m_w_mix_in'], 'm_conv_a_w': out['m_conv_a_w'], 'm_conv_b_w': out['m_conv_b_w'], 'm_conv_b_b': out['m_conv_b_b'], 'm_ln_b_g': out['m_ln_b_g'], 'm_ln_b_b': out['m_ln_b_b'], 'm_w_mix_out': out['m_w_mix_out'], 'm_mix_post_g': out['m_mix_post_g'], 'm_xa_pre_g': out['m_xa_pre_g'], 'm_mem_norm_g': out['m_mem_norm_g'], 'm_w_q': out['m_w_q'], 'm_w_k': out['m_w_k'], 'm_w_v': out['m_w_v'], 'm_w_o': out['m_w_o'], 'm_xa_post_g': out['m_xa_post_g'], 'm_ffn_pre_g': out['m_ffn_pre_g'], 'm_w_gate': out['m_w_gate'], 'm_w_up': out['m_w_up'], 'm_w_down': out['m_w_down'], 'm_ffn_post_g': out['m_ffn_post_g'], 'v_mix_pre_g': out['v_mix_pre_g'], 'v_w_mix_in': out['v_w_mix_in'], 'v_conv_a_w': out['v_conv_a_w'], 'v_conv_b_w': out['v_conv_b_w'], 'v_conv_b_b': out['v_conv_b_b'], 'v_ln_b_g': out['v_ln_b_g'], 'v_ln_b_b': out['v_ln_b_b'], 'v_w_mix_out': out['v_w_mix_out'], 'v_mix_post_g': out['v_mix_post_g'], 'v_xa_pre_g': out['v_xa_pre_g'], 'v_mem_norm_g': out['v_mem_norm_g'], 'v_w_q': out['v_w_q'], 'v_w_k': out['v_w_k'], 'v_w_v': out['v_w_v'], 'v_w_o': out['v_w_o'], 'v_xa_post_g': out['v_xa_post_g'], 'v_ffn_pre_g': out['v_ffn_pre_g'], 'v_w_gate': out['v_w_gate'], 'v_w_up': out['v_w_up'], 'v_w_down': out['v_w_down'], 'v_ffn_post_g': out['v_ffn_post_g']}


def _loss(weights, diff, rest, loss_target):
    with _jax.named_scope("forward"):
        args = {**rest, TWIN_DIFF_INPUT: diff, **{k: w.astype(_WEIGHT_DTYPES[k]) for k, w in weights.items()}}
        y = _forward(args)
    with _jax.named_scope("loss_head"):
        err = _jnp.square(y.astype(_jnp.float32) - loss_target)
        return 0.5 * _jnp.sum(_jnp.mean(err, axis=-1)) if err.ndim else 0.5 * err


def _adamw(w, g, m, v):
    m = ADAM_B1 * m + (1.0 - ADAM_B1) * g
    v = ADAM_B2 * v + (1.0 - ADAM_B2) * _jnp.square(g)
    m_hat = m / (1.0 - ADAM_B1 ** ADAM_STEP)
    v_hat = v / (1.0 - ADAM_B2 ** ADAM_STEP)
    delta = -ADAM_LR * (m_hat / (_jnp.sqrt(v_hat) + ADAM_EPS) + ADAM_WD * w)
    return delta, m, v


def reference(x, mem, mix_pre_g, w_mix_in, conv_a_w, conv_b_w, conv_b_b, ln_b_g, ln_b_b, w_mix_out, mix_post_g, xa_pre_g, mem_norm_g, w_q, w_k, w_v, w_o, xa_post_g, ffn_pre_g, w_gate, w_up, w_down, ffn_post_g, loss_target, m_mix_pre_g, m_w_mix_in, m_conv_a_w, m_conv_b_w, m_conv_b_b, m_ln_b_g, m_ln_b_b, m_w_mix_out, m_mix_post_g, m_xa_pre_g, m_mem_norm_g, m_w_q, m_w_k, m_w_v, m_w_o, m_xa_post_g, m_ffn_pre_g, m_w_gate, m_w_up, m_w_down, m_ffn_post_g, v_mix_pre_g, v_w_mix_in, v_conv_a_w, v_conv_b_w, v_conv_b_b, v_ln_b_g, v_ln_b_b, v_w_mix_out, v_mix_post_g, v_xa_pre_g, v_mem_norm_g, v_w_q, v_w_k, v_w_v, v_w_o, v_xa_post_g, v_ffn_pre_g, v_w_gate, v_w_up, v_w_down, v_ffn_post_g):
    given = dict(x=x, mem=mem, mix_pre_g=mix_pre_g, w_mix_in=w_mix_in, conv_a_w=conv_a_w, conv_b_w=conv_b_w, conv_b_b=conv_b_b, ln_b_g=ln_b_g, ln_b_b=ln_b_b, w_mix_out=w_mix_out, mix_post_g=mix_post_g, xa_pre_g=xa_pre_g, mem_norm_g=mem_norm_g, w_q=w_q, w_k=w_k, w_v=w_v, w_o=w_o, xa_post_g=xa_post_g, ffn_pre_g=ffn_pre_g, w_gate=w_gate, w_up=w_up, w_down=w_down, ffn_post_g=ffn_post_g, loss_target=loss_target, m_mix_pre_g=m_mix_pre_g, m_w_mix_in=m_w_mix_in, m_conv_a_w=m_conv_a_w, m_conv_b_w=m_conv_b_w, m_conv_b_b=m_conv_b_b, m_ln_b_g=m_ln_b_g, m_ln_b_b=m_ln_b_b, m_w_mix_out=m_w_mix_out, m_mix_post_g=m_mix_post_g, m_xa_pre_g=m_xa_pre_g, m_mem_norm_g=m_mem_norm_g, m_w_q=m_w_q, m_w_k=m_w_k, m_w_v=m_w_v, m_w_o=m_w_o, m_xa_post_g=m_xa_post_g, m_ffn_pre_g=m_ffn_pre_g, m_w_gate=m_w_gate, m_w_up=m_w_up, m_w_down=m_w_down, m_ffn_post_g=m_ffn_post_g, v_mix_pre_g=v_mix_pre_g, v_w_mix_in=v_w_mix_in, v_conv_a_w=v_conv_a_w, v_conv_b_w=v_conv_b_w, v_conv_b_b=v_conv_b_b, v_ln_b_g=v_ln_b_g, v_ln_b_b=v_ln_b_b, v_w_mix_out=v_w_mix_out, v_mix_post_g=v_mix_post_g, v_xa_pre_g=v_xa_pre_g, v_mem_norm_g=v_mem_norm_g, v_w_q=v_w_q, v_w_k=v_w_k, v_w_v=v_w_v, v_w_o=v_w_o, v_xa_post_g=v_xa_post_g, v_ffn_pre_g=v_ffn_pre_g, v_w_gate=v_w_gate, v_w_up=v_w_up, v_w_down=v_w_down, v_ffn_post_g=v_ffn_post_g)
    weights = {n: given[n] for n in TWIN_WEIGHTS}
    shared = {n: given[n] for n in SHARED_INPUTS}
    per_example = {n: given[n] for n in ['x', 'mem']}
    grad_fn = _jax.value_and_grad(_loss, argnums=(0, 1))

    def one_microbatch(ex, loss_target):
        ex = dict(ex)
        diff = ex.pop(TWIN_DIFF_INPUT)
        return grad_fn(weights, diff, {**shared, **ex}, loss_target)

    if N_MICROBATCH == 1:
        loss, (grad_w, grad_x) = one_microbatch(per_example, given["loss_target"])
    else:
        def body(carry, xs):
            loss_sum, grad_sum = carry
            l_k, (gw_k, gx_k) = one_microbatch(xs[0], xs[1])
            with _jax.named_scope("update"):
                return (loss_sum + l_k, _jax.tree.map(_jnp.add, grad_sum, gw_k)), gx_k

        init = (_jnp.zeros((), _jnp.float32), _jax.tree.map(_jnp.zeros_like, weights))
        (loss, grad_w), grad_x = _jax.lax.scan(body, init, (per_example, given["loss_target"]))
    with _jax.named_scope("update"):
        delta_w, new_m, new_v = {}, {}, {}
        for n in TWIN_WEIGHTS:
            delta_w[n], new_m[n], new_v[n] = _adamw(weights[n], grad_w[n], given["m_" + n], given["v_" + n])
    return (loss, grad_x, *[grad_w[n] for n in TWIN_WEIGHTS], *[delta_w[n] for n in TWIN_WEIGHTS],
            *[new_m[n] for n in TWIN_WEIGHTS], *[new_v[n] for n in TWIN_WEIGHTS])
```

```python
import functools

import jax
import jax.numpy as jnp
from jax import lax
from jax.experimental import pallas as pl
from jax.experimental.pallas import tpu as pltpu

F32 = jnp.float32
BF16 = jnp.bfloat16

D_MODEL = 1024
D_A = 512
D_B = 512
D_IN_ALL = 2560
CONV_A_W = 3
CONV_B_W = 31
XA_HEADS = 4
XA_HEAD_DIM = 256
D_FF = 2816
N_DEV = 8
RMS_EPS = 1e-6
LN_EPS = 1e-5
ADAM_LR = 0.001
ADAM_B1 = 0.9
ADAM_B2 = 0.999
ADAM_EPS = 1e-08
ADAM_WD = 0.01
ADAM_STEP = 10

VMEM_LIMIT_BYTES = 56 * 1024 * 1024
SUBLANES = 8
LANES = 128
HALO_B = 32
HALO_A = 8
CONV_CHUNK = 32

MESH = pl.DeviceIdType.MESH


def _params(n_grid_axes=1):
    return pltpu.CompilerParams(dimension_semantics=("arbitrary",) * n_grid_axes, vmem_limit_bytes=VMEM_LIMIT_BYTES)


def _sds(shape, dtype):
    return jax.ShapeDtypeStruct(shape, dtype)


def _tile(rows, cols):
    return pl.BlockSpec((rows, cols), lambda i: (i, 0))


def _rtile(rows, cols, n):
    return pl.BlockSpec((rows, cols), lambda i: (n - 1 - i, 0))


def _whole(shape):
    zeros = (0,) * len(shape)
    return pl.BlockSpec(shape, lambda i: zeros)


def _resident(shape):
    zeros = (0,) * len(shape)
    return pl.BlockSpec(shape, lambda i: zeros, pipeline_mode=pl.Buffered(1))


def _dot(a, b):
    return jnp.dot(a, b, preferred_element_type=F32)


def _dot_nt(a, b):
    return lax.dot_general(a, b, (((1,), (1,)), ((), ())), preferred_element_type=F32)


def _dot_tn(a, b):
    return lax.dot_general(a, b, (((0,), (0,)), ((), ())), preferred_element_type=F32)


def _sigmoid(x):
    return 1.0 / (1.0 + jnp.exp(-x))


def _rms_fwd(x, g):
    r = lax.rsqrt(jnp.mean(x * x, axis=-1, keepdims=True) + RMS_EPS)
    return x * r * g


def _rms_bwd(dy, xin, g):
    r = lax.rsqrt(jnp.mean(xin * xin, axis=-1, keepdims=True) + RMS_EPS)
    n = xin * r
    dg = jnp.sum(dy * n, axis=0, keepdims=True)
    dn = dy * g
    dx = r * (dn - n * jnp.mean(dn * n, axis=-1, keepdims=True))
    return dx, dg


def _accumulate(ref, value):
    @pl.when(pl.program_id(0) == 0)
    def _():
        ref[...] = jnp.zeros(ref.shape, ref.dtype)

    ref[...] += value


def _norm_matmul(x, g, w, tm, name):
    t, d = x.shape
    n = w.shape[1]

    def body(x_ref, g_ref, w_ref, h_ref, o_ref):
        h = _rms_fwd(x_ref[...], g_ref[...]).astype(BF16)
        h_ref[...] = h
        o_ref[...] = _dot(h, w_ref[...]).astype(BF16)

    return pl.pallas_call(
        body, name=name, grid=(t // tm,),
        in_specs=[_tile(tm, d), _whole((1, d)), _resident((d, n))],
        out_specs=[_tile(tm, d), _tile(tm, n)],
        out_shape=[_sds((t, d), BF16), _sds((t, n), BF16)],
        compiler_params=_params(),
    )(x, g, w)


def _conv_fwd(u, wa, wb, bb, lg, lb, tc, name):
    t = u.shape[0]
    n_chunks = tc // CONV_CHUNK

    def body(u_ref, wa_ref, wb_ref, bb_ref, lg_ref, lb_ref, y_ref, zc_ref, ca_ref, zbuf, cvbuf):
        @pl.when(pl.program_id(0) == 0)
        def _():
            zbuf[0:HALO_B, :] = jnp.zeros((HALO_B, D_B), F32)
            cvbuf[0:HALO_A, :] = jnp.zeros((HALO_A, D_A), F32)

        c_a = u_ref[:, D_A:2 * D_A].astype(F32)
        v_a = u_ref[:, 2 * D_A:3 * D_A].astype(F32)
        cvbuf[HALO_A:HALO_A + tc, :] = c_a * v_a
        glu_v = u_ref[:, 3 * D_A:3 * D_A + D_B].astype(F32)
        glu_g = u_ref[:, 3 * D_A + D_B:3 * D_A + 2 * D_B].astype(F32)
        zbuf[HALO_B:HALO_B + tc, :] = glu_v * _sigmoid(glu_g)

        for c in range(n_chunks):
            r0 = c * CONV_CHUNK
            acc = jnp.zeros((CONV_CHUNK, D_A), F32)
            for k in range(CONV_A_W):
                acc = acc + wa_ref[k:k + 1, :] * cvbuf[r0 + HALO_A - (CONV_A_W - 1) + k:r0 + HALO_A - (CONV_A_W - 1) + k + CONV_CHUNK, :]
            ca_ref[r0:r0 + CONV_CHUNK, :] = acc.astype(BF16)
            b_a = u_ref[r0:r0 + CONV_CHUNK, 0:D_A].astype(F32)
            y_ref[r0:r0 + CONV_CHUNK, 0:D_A] = (b_a * acc).astype(BF16)

            accb = jnp.zeros((CONV_CHUNK, D_B), F32)
            for k in range(CONV_B_W):
                off = r0 + HALO_B - (CONV_B_W - 1) + k
                accb = accb + wb_ref[k:k + 1, :] * zbuf[off:off + CONV_CHUNK, :]
            zc = accb + bb_ref[...]
            zc_ref[r0:r0 + CONV_CHUNK, :] = zc.astype(BF16)
            mu = jnp.mean(zc, axis=-1, keepdims=True)
            xc = zc - mu
            var = jnp.mean(xc * xc, axis=-1, keepdims=True)
            ln = xc * lax.rsqrt(var + LN_EPS) * lg_ref[...] + lb_ref[...]
            y_ref[r0:r0 + CONV_CHUNK, D_A:D_A + D_B] = (ln * _sigmoid(ln)).astype(BF16)

        zbuf[0:HALO_B, :] = zbuf[tc:tc + HALO_B, :]
        cvbuf[0:HALO_A, :] = cvbuf[tc:tc + HALO_A, :]

    return pl.pallas_call(
        body, name=name, grid=(t // tc,),
        in_specs=[_tile(tc, D_IN_ALL), _whole((CONV_A_W, D_A)), _whole((CONV_B_W, D_B)), _whole((1, D_B)),
                  _whole((1, D_B)), _whole((1, D_B))],
        out_specs=[_tile(tc, D_A + D_B), _tile(tc, D_B), _tile(tc, D_A)],
        out_shape=[_sds((t, D_A + D_B), BF16), _sds((t, D_B), BF16), _sds((t, D_A), BF16)],
        scratch_shapes=[pltpu.VMEM((HALO_B + tc, D_B), F32), pltpu.VMEM((HALO_A + tc, D_A), F32)],
        compiler_params=_params(),
    )(u, wa, wb, bb, lg, lb)


def _proj_norm_res(a, w, xres, g, tm, name):
    t, k = a.shape
    d = w.shape[1]

    def body(a_ref, w_ref, x_ref, g_ref, xo_ref, s_ref):
        s = _dot(a_ref[...], w_ref[...])
        s_ref[...] = s.astype(BF16)
        xo_ref[...] = x_ref[...] + _rms_fwd(s, g_ref[...])

    return pl.pallas_call(
        body, name=name, grid=(t // tm,),
        in_specs=[_tile(tm, k), _resident((k, d)), _tile(tm, d), _whole((1, d))],
        out_specs=[_tile(tm, d), _tile(tm, d)],
        out_shape=[_sds((t, d), F32), _sds((t, d), BF16)],
        compiler_params=_params(),
    )(a, w, xres, g)


def _mem_fwd(mem, g, wk, wv, name):
    m, d = mem.shape

    def body(mem_ref, g_ref, wk_ref, wv_ref, n_ref, k_ref, v_ref):
        n = _rms_fwd(mem_ref[...], g_ref[...]).astype(BF16)
        n_ref[...] = n
        k_ref[...] = _dot(n, wk_ref[...]).astype(BF16)
        v_ref[...] = _dot(n, wv_ref[...]).astype(BF16)

    return pl.pallas_call(
        body, name=name, grid=(1,),
        in_specs=[_whole((m, d)), _whole((1, d)), _whole((d, d)), _whole((d, d))],
        out_specs=[_whole((m, d))] * 3,
        out_shape=[_sds((m, d), BF16)] * 3,
        compiler_params=_params(),
    )(mem, g, wk, wv)


def _softmax_rows(s):
    e = jnp.exp(s - jnp.max(s, axis=-1, keepdims=True))
    return e / jnp.sum(e, axis=-1, keepdims=True)


def _attn_fwd(x1, g_pre, wq, k, v, wo, g_post, tm, name):
    t, d = x1.shape
    m = k.shape[0]
    scale = XA_HEAD_DIM ** -0.5

    def body(x_ref, gp_ref, wq_ref, k_ref, v_ref, wo_ref, go_ref, x2_ref, h_ref, q_ref, o_ref, a_ref):
        x = x_ref[...]
        h = _rms_fwd(x, gp_ref[...]).astype(BF16)
        h_ref[...] = h
        q_ref[...] = _dot(h, wq_ref[...]).astype(BF16)
        for hd in range(XA_HEADS):
            cols = slice(hd * XA_HEAD_DIM, (hd + 1) * XA_HEAD_DIM)
            p = _softmax_rows(_dot_nt(q_ref[:, cols], k_ref[:, cols]) * scale)
            o_ref[:, cols] = _dot(p.astype(BF16), v_ref[:, cols]).astype(BF16)
        a = _dot(o_ref[...], wo_ref[...])
        a_ref[...] = a.astype(BF16)
        x2_ref[...] = x + _rms_fwd(a, go_ref[...])

    return pl.pallas_call(
        body, name=name, grid=(t // tm,),
        in_specs=[_tile(tm, d), _whole((1, d)), _resident((d, d)), _whole((m, d)), _whole((m, d)), _resident((d, d)),
                  _whole((1, d))],
        out_specs=[_tile(tm, d)] * 5,
        out_shape=[_sds((t, d), F32)] + [_sds((t, d), BF16)] * 4,
        compiler_params=_params(),
    )(x1, g_pre, wq, k, v, wo, g_post)


def _ffn_fwd(x2, g_pre, wg, wu, wd, g_post, target, tm, name):
    t, d = x2.shape
    f = wg.shape[1]

    def body(x_ref, gp_ref, wg_ref, wu_ref, wd_ref, go_ref, tgt_ref, h_ref, gt_ref, up_ref, f_ref, sq_ref):
        x = x_ref[...]
        h = _rms_fwd(x, gp_ref[...]).astype(BF16)
        h_ref[...] = h
        gt = _dot(h, wg_ref[...])
        up = _dot(h, wu_ref[...])
        gt_ref[...] = gt.astype(BF16)
        up_ref[...] = up.astype(BF16)
        hd = (gt * _sigmoid(gt) * up).astype(BF16)
        ff = _dot(hd, wd_ref[...])
        f_ref[...] = ff.astype(BF16)
        err = x + _rms_fwd(ff, go_ref[...]) - tgt_ref[...]
        _accumulate(sq_ref, jnp.sum(err * err, axis=0, keepdims=True))

    return pl.pallas_call(
        body, name=name, grid=(t // tm,),
        in_specs=[_tile(tm, d), _whole((1, d)), _resident((d, f)), _resident((d, f)), _resident((f, d)), _whole((1, d)),
                  _tile(tm, d)],
        out_specs=[_tile(tm, d), _tile(tm, f), _tile(tm, f), _tile(tm, d), _whole((1, d))],
        out_shape=[_sds((t, d), BF16), _sds((t, f), BF16), _sds((t, f), BF16), _sds((t, d), BF16), _sds((1, d), F32)],
        compiler_params=_params(),
    )(x2, g_pre, wg, wu, wd, g_post, target)


def _ffn_bwd(x2, f, target, gt, up, g_pre, wg, wu, wd, g_post, tm, name):
    t, d = x2.shape
    ff = wg.shape[1]

    def body(x_ref, f_ref, tgt_ref, gt_ref, up_ref, gp_ref, wg_ref, wu_ref, wd_ref, go_ref,
             dx_ref, df_ref, hd_ref, dgt_ref, dup_ref, dgo_ref, dgp_ref):
        x = x_ref[...]
        fo = f_ref[...].astype(F32)
        dx3 = (x + _rms_fwd(fo, go_ref[...]) - tgt_ref[...]) * (1.0 / d)
        df, dgo = _rms_bwd(dx3, fo, go_ref[...])
        _accumulate(dgo_ref, dgo)
        df = df.astype(BF16)
        df_ref[...] = df
        dhd = _dot_nt(df, wd_ref[...])
        gt = gt_ref[...].astype(F32)
        up = up_ref[...].astype(F32)
        sg = _sigmoid(gt)
        si = gt * sg
        hd_ref[...] = (si * up).astype(BF16)
        dup = (dhd * si).astype(BF16)
        dgt = (dhd * up * (sg * (1.0 + gt * (1.0 - sg)))).astype(BF16)
        dup_ref[...] = dup
        dgt_ref[...] = dgt
        dh = _dot_nt(dgt, wg_ref[...]) + _dot_nt(dup, wu_ref[...])
        dxn, dgp = _rms_bwd(dh, x, gp_ref[...])
        _accumulate(dgp_ref, dgp)
        dx_ref[...] = dx3 + dxn

    return pl.pallas_call(
        body, name=name, grid=(t // tm,),
        in_specs=[_tile(tm, d), _tile(tm, d), _tile(tm, d), _tile(tm, ff), _tile(tm, ff), _whole((1, d)),
                  _resident((d, ff)), _resident((d, ff)), _resident((ff, d)), _whole((1, d))],
        out_specs=[_tile(tm, d), _tile(tm, d), _tile(tm, ff), _tile(tm, ff), _tile(tm, ff), _whole((1, d)), _whole((1, d))],
        out_shape=[_sds((t, d), F32), _sds((t, d), BF16), _sds((t, ff), BF16), _sds((t, ff), BF16), _sds((t, ff), BF16),
                   _sds((1, d), F32), _sds((1, d), F32)],
        compiler_params=_params(),
    )(x2, f, target, gt, up, g_pre, wg, wu, wd, g_post)


def _attn_bwd(dx2, a, x1, q, k, v, g_pre, wq, wo, g_post, tm, name):
    t, d = x1.shape
    m = k.shape[0]
    scale = XA_HEAD_DIM ** -0.5

    def body(dx2_ref, a_ref, x_ref, q_ref, k_ref, v_ref, gp_ref, wq_ref, wo_ref, go_ref,
             dx1_ref, da_ref, dq_ref, dk_ref, dv_ref, dgo_ref, dgp_ref, do_buf):
        dx2 = dx2_ref[...]
        da, dgo = _rms_bwd(dx2, a_ref[...].astype(F32), go_ref[...])
        _accumulate(dgo_ref, dgo)
        da = da.astype(BF16)
        da_ref[...] = da
        do_buf[...] = _dot_nt(da, wo_ref[...]).astype(BF16)
        for hd in range(XA_HEADS):
            cols = slice(hd * XA_HEAD_DIM, (hd + 1) * XA_HEAD_DIM)
            qh = q_ref[:, cols]
            p = _softmax_rows(_dot_nt(qh, k_ref[:, cols]) * scale)
            do_h = do_buf[:, cols]
            dp = _dot_nt(do_h, v_ref[:, cols])
            _accumulate(dv_ref.at[:, cols], _dot_tn(p.astype(BF16), do_h))
            ds = (p * (dp - jnp.sum(dp * p, axis=-1, keepdims=True)) * scale).astype(BF16)
            dq_ref[:, cols] = _dot(ds, k_ref[:, cols]).astype(BF16)
            _accumulate(dk_ref.at[:, cols], _dot_tn(ds, qh))
        dh = _dot_nt(dq_ref[...], wq_ref[...])
        dxn, dgp = _rms_bwd(dh, x_ref[...], gp_ref[...])
        _accumulate(dgp_ref, dgp)
        dx1_ref[...] = dx2 + dxn

    return pl.pallas_call(
        body, name=name, grid=(t // tm,),
        in_specs=[_tile(tm, d), _tile(tm, d), _tile(tm, d), _tile(tm, d), _whole((m, d)), _whole((m, d)), _whole((1, d)),
                  _resident((d, d)), _resident((d, d)), _whole((1, d))],
        out_specs=[_tile(tm, d), _tile(tm, d), _tile(tm, d), _whole((m, d)), _whole((m, d)), _whole((1, d)), _whole((1, d))],
        out_shape=[_sds((t, d), F32), _sds((t, d), BF16), _sds((t, d), BF16), _sds((m, d), F32), _sds((m, d), F32),
                   _sds((1, d), F32), _sds((1, d), F32)],
        scratch_shapes=[pltpu.VMEM((tm, d), BF16)],
        compiler_params=_params(),
    )(dx2, a, x1, q, k, v, g_pre, wq, wo, g_post)


def _mem_bwd(mem, mem_n, dk, dv, g, wk, wv, name):
    m, d = mem.shape

    def body(mem_ref, n_ref, dk_ref, dv_ref, g_ref, wk_ref, wv_ref, dwk_ref, dwv_ref, dg_ref):
        dk = dk_ref[...].astype(BF16)
        dv = dv_ref[...].astype(BF16)
        n = n_ref[...]
        dwk_ref[...] = _dot_tn(n, dk)
        dwv_ref[...] = _dot_tn(n, dv)
        dn = _dot_nt(dk, wk_ref[...]) + _dot_nt(dv, wv_ref[...])
        _, dg = _rms_bwd(dn, mem_ref[...], g_ref[...])
        dg_ref[...] = dg

    return pl.pallas_call(
        body, name=name, grid=(1,),
        in_specs=[_whole((m, d)), _whole((m, d)), _whole((m, d)), _whole((m, d)), _whole((1, d)), _whole((d, d)),
                  _whole((d, d))],
        out_specs=[_whole((d, d)), _whole((d, d)), _whole((1, d))],
        out_shape=[_sds((d, d), F32), _sds((d, d), F32), _sds((1, d), F32)],
        compiler_params=_params(),
    )(mem, mem_n, dk, dv, g, wk, wv)


def _proj_bwd(dxo, s, w, g, tm, name):
    t, d = dxo.shape
    k = w.shape[0]

    def body(dx_ref, s_ref, w_ref, g_ref, ds_ref, da_ref, dg_ref):
        ds, dg = _rms_bwd(dx_ref[...], s_ref[...].astype(F32), g_ref[...])
        _accumulate(dg_ref, dg)
        ds = ds.astype(BF16)
        ds_ref[...] = ds
        da_ref[...] = _dot_nt(ds, w_ref[...]).astype(BF16)

    return pl.pallas_call(
        body, name=name, grid=(t // tm,),
        in_specs=[_tile(tm, d), _tile(tm, d), _resident((k, d)), _whole((1, d))],
        out_specs=[_tile(tm, d), _tile(tm, k), _whole((1, d))],
        out_shape=[_sds((t, d), BF16), _sds((t, k), BF16), _sds((1, d), F32)],
        compiler_params=_params(),
    )(dxo, s, w, g)


def _conv_bwd(dy, u, zc, ca, wa, wb, lg, lb, tc, name):
    t = u.shape[0]
    n_tiles = t // tc
    n_chunks = tc // CONV_CHUNK

    def body(dy_ref, u_ref, zc_ref, ca_ref, wa_ref, wb_ref, lg_ref, lb_ref,
             du_ref, dwa_ref, dwb_ref, dbb_ref, dlg_ref, dlb_ref, ebuf, eabuf, zbuf, cvbuf, wacc, aacc, vacc):
        step = pl.program_id(0)

        @pl.when(step == 0)
        def _():
            ebuf[tc:tc + HALO_B, :] = jnp.zeros((HALO_B, D_B), F32)
            eabuf[tc:tc + HALO_A, :] = jnp.zeros((HALO_A, D_A), F32)
            wacc[...] = jnp.zeros_like(wacc)
            aacc[...] = jnp.zeros_like(aacc)
            vacc[...] = jnp.zeros_like(vacc)

        dbb = jnp.zeros((SUBLANES, D_B), F32)
        dlg = jnp.zeros((SUBLANES, D_B), F32)
        dlb = jnp.zeros((SUBLANES, D_B), F32)
        for c in range(n_chunks):
            rows = slice(c * CONV_CHUNK, (c + 1) * CONV_CHUNK)
            dy_a = dy_ref[rows, 0:D_A].astype(F32)
            b_a = u_ref[rows, 0:D_A].astype(F32)
            du_ref[rows, 0:D_A] = (dy_a * ca_ref[rows, :].astype(F32)).astype(BF16)
            eabuf[rows, :] = dy_a * b_a
            cvbuf[rows, :] = u_ref[rows, D_A:2 * D_A].astype(F32) * u_ref[rows, 2 * D_A:3 * D_A].astype(F32)
            glu_g = u_ref[rows, 3 * D_A + D_B:3 * D_A + 2 * D_B].astype(F32)
            zbuf[rows, :] = u_ref[rows, 3 * D_A:3 * D_A + D_B].astype(F32) * _sigmoid(glu_g)

            zcv = zc_ref[rows, :].astype(F32)
            mu = jnp.mean(zcv, axis=-1, keepdims=True)
            xc = zcv - mu
            rstd = lax.rsqrt(jnp.mean(xc * xc, axis=-1, keepdims=True) + LN_EPS)
            xhat = xc * rstd
            ln = xhat * lg_ref[...] + lb_ref[...]
            sg = _sigmoid(ln)
            dln = dy_ref[rows, D_A:D_A + D_B].astype(F32) * (sg * (1.0 + ln * (1.0 - sg)))
            dlg = dlg + jnp.sum((dln * xhat).reshape(CONV_CHUNK // SUBLANES, SUBLANES, D_B), axis=0)
            dlb = dlb + jnp.sum(dln.reshape(CONV_CHUNK // SUBLANES, SUBLANES, D_B), axis=0)
            dxh = dln * lg_ref[...]
            dzc = rstd * (dxh - jnp.mean(dxh, axis=-1, keepdims=True) - xhat * jnp.mean(dxh * xhat, axis=-1, keepdims=True))
            dbb = dbb + jnp.sum(dzc.reshape(CONV_CHUNK // SUBLANES, SUBLANES, D_B), axis=0)
            ebuf[rows, :] = dzc
        vacc[0] += dbb
        vacc[1] += dlg
        vacc[2] += dlb

        for c in range(n_chunks):
            r0 = c * CONV_CHUNK
            rows = slice(r0, r0 + CONV_CHUNK)
            cv = cvbuf[rows, :]
            dcv = jnp.zeros((CONV_CHUNK, D_A), F32)
            for k in range(CONV_A_W):
                off = r0 + (CONV_A_W - 1) - k
                e = eabuf[off:off + CONV_CHUNK, :]
                dcv = dcv + wa_ref[k:k + 1, :] * e
                aacc[k] += jnp.sum((cv * e).reshape(CONV_CHUNK // SUBLANES, SUBLANES, D_A), axis=0)
            du_ref[rows, D_A:2 * D_A] = (dcv * u_ref[rows, 2 * D_A:3 * D_A].astype(F32)).astype(BF16)
            du_ref[rows, 2 * D_A:3 * D_A] = (dcv * u_ref[rows, D_A:2 * D_A].astype(F32)).astype(BF16)

            z = zbuf[rows, :]
            dz = jnp.zeros((CONV_CHUNK, D_B), F32)
            for k in range(CONV_B_W):
                off = r0 + (CONV_B_W - 1) - k
                e = ebuf[off:off + CONV_CHUNK, :]
                dz = dz + wb_ref[k:k + 1, :] * e
                wacc[k] += jnp.sum((z * e).reshape(CONV_CHUNK // SUBLANES, SUBLANES, D_B), axis=0)
            glu_v = u_ref[rows, 3 * D_A:3 * D_A + D_B].astype(F32)
            sgg = _sigmoid(u_ref[rows, 3 * D_A + D_B:3 * D_A + 2 * D_B].astype(F32))
            du_ref[rows, 3 * D_A:3 * D_A + D_B] = (dz * sgg).astype(BF16)
            du_ref[rows, 3 * D_A + D_B:3 * D_A + 2 * D_B] = (dz * glu_v * sgg * (1.0 - sgg)).astype(BF16)

        ebuf[tc:tc + HALO_B, :] = ebuf[0:HALO_B, :]
        eabuf[tc:tc + HALO_A, :] = eabuf[0:HALO_A, :]

        @pl.when(step == n_tiles - 1)
        def _():
            for k in range(CONV_B_W):
                dwb_ref[k:k + 1, :] = jnp.sum(wacc[k], axis=0, keepdims=True)
            for k in range(CONV_A_W):
                dwa_ref[k:k + 1, :] = jnp.sum(aacc[k], axis=0, keepdims=True)
            dbb_ref[...] = jnp.sum(vacc[0], axis=0, keepdims=True)
            dlg_ref[...] = jnp.sum(vacc[1], axis=0, keepdims=True)
            dlb_ref[...] = jnp.sum(vacc[2], axis=0, keepdims=True)

    return pl.pallas_call(
        body, name=name, grid=(n_tiles,),
        in_specs=[_rtile(tc, D_A + D_B, n_tiles), _rtile(tc, D_IN_ALL, n_tiles), _rtile(tc, D_B, n_tiles),
                  _rtile(tc, D_A, n_tiles), _whole((CONV_A_W, D_A)), _whole((CONV_B_W, D_B)), _whole((1, D_B)),
                  _whole((1, D_B))],
        out_specs=[_rtile(tc, D_IN_ALL, n_tiles), _whole((CONV_A_W, D_A)), _whole((CONV_B_W, D_B)), _whole((1, D_B)),
                   _whole((1, D_B)), _whole((1, D_B))],
        out_shape=[_sds((t, D_IN_ALL), BF16), _sds((CONV_A_W, D_A), F32), _sds((CONV_B_W, D_B), F32), _sds((1, D_B), F32),
                   _sds((1, D_B), F32), _sds((1, D_B), F32)],
        scratch_shapes=[pltpu.VMEM((tc + HALO_B, D_B), F32), pltpu.VMEM((tc + HALO_A, D_A), F32),
                        pltpu.VMEM((tc, D_B), F32), pltpu.VMEM((tc, D_A), F32),
                        pltpu.VMEM((CONV_B_W, SUBLANES, D_B), F32), pltpu.VMEM((CONV_A_W, SUBLANES, D_A), F32),
                        pltpu.VMEM((3, SUBLANES, D_B), F32)],
        compiler_params=_params(),
    )(dy, u, zc, ca, wa, wb, lg, lb)


def _in_bwd(du, w, x, dx1, g, tm, name):
    t, d = x.shape
    n = w.shape[1]

    def body(du_ref, w_ref, x_ref, dx1_ref, g_ref, dx_ref, dg_ref):
        dh = _dot_nt(du_ref[...], w_ref[...])
        dxn, dg = _rms_bwd(dh, x_ref[...], g_ref[...])
        _accumulate(dg_ref, dg)
        dx_ref[...] = dx1_ref[...] + dxn

    return pl.pallas_call(
        body, name=name, grid=(t // tm,),
        in_specs=[_tile(tm, n), _resident((d, n)), _tile(tm, d), _tile(tm, d), _whole((1, d))],
        out_specs=[_tile(tm, d), _whole((1, d))],
        out_shape=[_sds((t, d), F32), _sds((1, d), F32)],
        compiler_params=_params(),
    )(du, w, x, dx1, g)


def _wgrad(a, b, tk, bm, bn, name):
    t, m = a.shape
    n = b.shape[1]

    def body(a_ref, b_ref, o_ref):
        @pl.when(pl.program_id(2) == 0)
        def _():
            o_ref[...] = jnp.zeros_like(o_ref)

        o_ref[...] += _dot_tn(a_ref[...], b_ref[...])

    return pl.pallas_call(
        body, name=name, grid=(m // bm, n // bn, t // tk),
        in_specs=[pl.BlockSpec((tk, bm), lambda i, j, k: (k, i)), pl.BlockSpec((tk, bn), lambda i, j, k: (k, j))],
        out_specs=pl.BlockSpec((bm, bn), lambda i, j, k: (i, j)),
        out_shape=_sds((m, n), F32),
        compiler_params=_params(3),
    )(a, b)


def _adamw(parts, w, m, v, tr, name):
    r, c = w.shape
    counts = [n for _, n in parts]

    def body(*refs):
        p_refs = refs[:len(parts)]
        w_ref, m_ref, v_ref, g_ref, d_ref, nm_ref, nv_ref = refs[len(parts):]
        g = None
        for p_ref, n in zip(p_refs, counts):
            for j in range(n):
                g = p_ref[j].astype(F32) if g is None else g + p_ref[j].astype(F32)
        g_ref[...] = g
        nm = ADAM_B1 * m_ref[...] + (1.0 - ADAM_B1) * g
        nv = ADAM_B2 * v_ref[...] + (1.0 - ADAM_B2) * (g * g)
        nm_ref[...] = nm
        nv_ref[...] = nv
        m_hat = nm / (1.0 - ADAM_B1 ** ADAM_STEP)
        v_hat = nv / (1.0 - ADAM_B2 ** ADAM_STEP)
        d_ref[...] = -ADAM_LR * (m_hat / (jnp.sqrt(v_hat) + ADAM_EPS) + ADAM_WD * w_ref[...])

    return pl.pallas_call(
        body, name=name, grid=(r // tr,),
        in_specs=[pl.BlockSpec((n, tr, c), lambda i: (0, i, 0)) for n in counts] + [_tile(tr, c)] * 3,
        out_specs=[_tile(tr, c)] * 4,
        out_shape=[_sds((r, c), F32)] * 4,
        compiler_params=_params(),
    )(*[p for p, _ in parts], w, m, v)


def _sum_parts(parts, name):
    n_parts, r, c = parts.shape

    def body(p_ref, o_ref):
        acc = p_ref[0]
        for j in range(1, n_parts):
            acc = acc + p_ref[j]
        o_ref[...] = acc

    return pl.pallas_call(
        body, name=name, grid=(1,),
        in_specs=[_whole((n_parts, r, c))], out_specs=_whole((r, c)), out_shape=_sds((r, c), F32),
        compiler_params=_params(),
    )(parts)


def _pair_sum(where, g, recv, tr, name):
    _, n_chips, r, c = g.shape

    def body(where_ref, g_ref, r_ref, o_ref):
        del where_ref
        o_ref[...] = (g_ref[...].astype(F32) + r_ref[...].astype(F32)).astype(BF16)

    grid_spec = pltpu.PrefetchScalarGridSpec(
        num_scalar_prefetch=1, grid=(n_chips, r // tr),
        in_specs=[pl.BlockSpec((None, None, tr, c), lambda k, i, s: (s[0], s[1] ^ k, i, 0)),
                  pl.BlockSpec((None, tr, c), lambda k, i, s: (s[1] ^ k, i, 0))],
        out_specs=pl.BlockSpec((None, tr, c), lambda k, i, s: (k, i, 0)))
    return pl.pallas_call(
        body, name=name, grid_spec=grid_spec, out_shape=_sds((n_chips, r, c), BF16), compiler_params=_params(2),
    )(where, g, recv)


def _place():
    return lax.axis_index("x"), lax.axis_index("y"), lax.axis_index("c")


def _all_gather(block, name):
    r, c = block.shape

    def body(x_ref, out_ref, send_sems, recv_sems, local_sem):
        x, y, cc = _place()
        me, sibling = (x, y, cc), (x, y, 1 - cc)
        chips = [(1 - x, y), (x, 1 - y), (1 - x, 1 - y)]

        def slot(px, py, pc):
            return out_ref.at[4 * px + 2 * py + pc]

        def copy(k, owner, to, src=None):
            return pltpu.make_async_remote_copy(
                src_ref=slot(*owner) if src is None else src, dst_ref=slot(*owner),
                send_sem=send_sems.at[k], recv_sem=recv_sems.at[k], device_id=to, device_id_type=MESH)

        mine = pltpu.make_async_copy(x_ref, slot(*me), local_sem)
        mine.start()
        first = [copy(0, me, sibling, src=x_ref)]
        first += [copy(1 + j, me, (*chip, cc), src=x_ref) for j, chip in enumerate(chips)]
        for cp in first:
            cp.start()
        passed = [copy(4 + j, (*chip, cc), sibling) for j, chip in enumerate(chips)]
        for j, chip in enumerate(chips):
            copy(1 + j, (*chip, cc), me).wait_recv()
            passed[j].start()
        copy(0, sibling, me).wait_recv()
        for j, chip in enumerate(chips):
            copy(4 + j, (*chip, 1 - cc), me).wait_recv()
        for cp in first + passed:
            cp.wait_send()
        mine.wait()

    return pl.pallas_call(
        body, name=name,
        in_specs=[pl.BlockSpec(memory_space=pl.ANY)], out_specs=pl.BlockSpec(memory_space=pl.ANY),
        out_shape=_sds((N_DEV, r, c), block.dtype),
        scratch_shapes=[pltpu.SemaphoreType.DMA((7,)), pltpu.SemaphoreType.DMA((7,)), pltpu.SemaphoreType.DMA],
    )(block)


def _sibling_exchange(g, name):
    _, n_chips, r, c = g.shape

    def body(g_ref, out_ref, send_sem, recv_sem):
        x, y, cc = _place()
        cp = pltpu.make_async_remote_copy(
            src_ref=g_ref.at[1 - cc], dst_ref=out_ref, send_sem=send_sem, recv_sem=recv_sem,
            device_id=(x, y, 1 - cc), device_id_type=MESH)
        cp.start()
        cp.wait()

    return pl.pallas_call(
        body, name=name,
        in_specs=[pl.BlockSpec(memory_space=pl.ANY)], out_specs=pl.BlockSpec(memory_space=pl.ANY),
        out_shape=_sds((n_chips, r, c), g.dtype),
        scratch_shapes=[pltpu.SemaphoreType.DMA, pltpu.SemaphoreType.DMA],
    )(g)


def _chip_exchange(p, name):
    _, r, c = p.shape

    def body(p_ref, out_ref, send_sems, recv_sems):
        x, y, cc = _place()
        peers = [(x, 1 - y, cc), (1 - x, y, cc), (1 - x, 1 - y, cc)]
        copies = [
            pltpu.make_async_remote_copy(
                src_ref=p_ref.at[1 + j], dst_ref=out_ref.at[j], send_sem=send_sems.at[j], recv_sem=recv_sems.at[j],
                device_id=peer, device_id_type=MESH)
            for j, peer in enumerate(peers)]
        for cp in copies:
            cp.start()
        for cp in copies:
            cp.wait()

    return pl.pallas_call(
        body, name=name,
        in_specs=[pl.BlockSpec(memory_space=pl.ANY)], out_specs=pl.BlockSpec(memory_space=pl.ANY),
        out_shape=_sds((3, r, c), p.dtype),
        scratch_shapes=[pltpu.SemaphoreType.DMA((3,)), pltpu.SemaphoreType.DMA((3,))],
    )(p)


def _row(v):
    return v.reshape(1, -1)


def _local_grads(x, mem, target, gains, conv, big, tm=512, tm_ffn=256, tc=256, tk=1024):
    g = {k: _row(v) for k, v in gains.items()}
    wa, wb = conv["conv_a_w"], conv["conv_b_w"]
    bb, lg, lb = _row(conv["conv_b_b"]), _row(conv["ln_b_g"]), _row(conv["ln_b_b"])

    mem_n, k, v = _mem_fwd(mem, g["mem_norm_g"], big["w_k"], big["w_v"], "mem_fwd")
    h1, u = _norm_matmul(x, g["mix_pre_g"], big["w_mix_in"], tm, "mix_in_fwd")
    ycat, zc, ca = _conv_fwd(u, wa, wb, bb, lg, lb, tc, "conv_fwd")
    x1, s1 = _proj_norm_res(ycat, big["w_mix_out"], x, g["mix_post_g"], tm, "mix_out_fwd")
    x2, h2, q, o, a = _attn_fwd(x1, g["xa_pre_g"], big["w_q"], k, v, big["w_o"], g["xa_post_g"], tm, "attn_fwd")
    h3, gt, up, f, sq = _ffn_fwd(x2, g["ffn_pre_g"], big["w_gate"], big["w_up"], big["w_down"], g["ffn_post_g"], target,
                                 tm_ffn, "ffn_fwd")

    dx2, df, hd, dgt, dup, d_ffn_post, d_ffn_pre = _ffn_bwd(
        x2, f, target, gt, up, g["ffn_pre_g"], big["w_gate"], big["w_up"], big["w_down"], g["ffn_post_g"], tm_ffn, "ffn_bwd")
    d_w_down = _wgrad(hd, df, tk, D_FF // 2, D_MODEL, "wgrad_down")
    d_w_gate = _wgrad(h3, dgt, tk, D_MODEL, D_FF // 2, "wgrad_gate")
    d_w_up = _wgrad(h3, dup, tk, D_MODEL, D_FF // 2, "wgrad_up")

    dx1, da, dq, dk, dv, d_xa_post, d_xa_pre = _attn_bwd(
        dx2, a, x1, q, k, v, g["xa_pre_g"], big["w_q"], big["w_o"], g["xa_post_g"], tm, "attn_bwd")
    d_w_o = _wgrad(o, da, tk, D_MODEL, D_MODEL, "wgrad_o")
    d_w_q = _wgrad(h2, dq, tk, D_MODEL, D_MODEL, "wgrad_q")
    d_w_k, d_w_v, d_mem_norm = _mem_bwd(mem, mem_n, dk, dv, g["mem_norm_g"], big["w_k"], big["w_v"], "mem_bwd")

    ds1, dycat, d_mix_post = _proj_bwd(dx1, s1, big["w_mix_out"], g["mix_post_g"], tm, "mix_out_bwd")
    d_w_mix_out = _wgrad(ycat, ds1, tk, D_MODEL, D_MODEL, "wgrad_mix_out")
    du, d_conv_a, d_conv_b, d_conv_bb, d_ln_g, d_ln_b = _conv_bwd(dycat, u, zc, ca, wa, wb, lg, lb, tc, "conv_bwd")
    d_w_mix_in = _wgrad(h1, du, tk, D_MODEL, D_IN_ALL // 2, "wgrad_mix_in")
    dx, d_mix_pre = _in_bwd(du, big["w_mix_in"], x, dx1, g["mix_pre_g"], tm, "mix_in_bwd")

    big_grads = dict(w_mix_in=d_w_mix_in, w_mix_out=d_w_mix_out, w_q=d_w_q, w_k=d_w_k, w_v=d_w_v, w_o=d_w_o,
                     w_gate=d_w_gate, w_up=d_w_up, w_down=d_w_down)
    small_grads = dict(mix_pre_g=d_mix_pre, conv_a_w=d_conv_a, conv_b_w=d_conv_b, conv_b_b=d_conv_bb, ln_b_g=d_ln_g,
                       ln_b_b=d_ln_b, mix_post_g=d_mix_post, xa_pre_g=d_xa_pre, mem_norm_g=d_mem_norm, xa_post_g=d_xa_post,
                       ffn_pre_g=d_ffn_pre, ffn_post_g=d_ffn_post)
    return sq, dx, big_grads, small_grads


WEIGHTS = ("mix_pre_g", "w_mix_in", "conv_a_w", "conv_b_w", "conv_b_b", "ln_b_g", "ln_b_b", "w_mix_out", "mix_post_g",
           "xa_pre_g", "mem_norm_g", "w_q", "w_k", "w_v", "w_o", "xa_post_g", "ffn_pre_g", "w_gate", "w_up", "w_down",
           "ffn_post_g")
LARGE = ("w_mix_in", "w_mix_out", "w_q", "w_k", "w_v", "w_o", "w_gate", "w_up", "w_down")
COLUMN_SHARDED = ("w_mix_in", "w_gate", "w_up")
GAINS = ("mix_pre_g", "mix_post_g", "xa_pre_g", "mem_norm_g", "xa_post_g", "ffn_pre_g", "ffn_post_g")
CHANNEL_VECTORS = ("conv_b_b", "ln_b_g", "ln_b_b")
CONV_TAPS = ("conv_a_w", "conv_b_w")
SMALL = GAINS + CHANNEL_VECTORS + CONV_TAPS
CONV_COLS_PER_DEVICE = D_A // N_DEV
ADAM_ROWS_PER_STEP = 288


def _shard_rows(name, shard):
    return shard.reshape(-1, D_MODEL) if name in COLUMN_SHARDED else shard


def _pack_shards(shards):
    return jnp.concatenate([_shard_rows(n, shards[n]) for n in LARGE], axis=0)


def _row_ranges(shards):
    ranges, off = {}, 0
    for n in LARGE:
        rows = _shard_rows(n, shards[n]).shape[0]
        ranges[n] = (off, rows)
        off += rows
    return ranges


def _unpack_gathered(gathered, ranges):
    whole = {}
    for n, (off, rows) in ranges.items():
        blk = gathered[:, off:off + rows, :]
        if n in COLUMN_SHARDED:
            whole[n] = blk.reshape(N_DEV, D_MODEL, rows).transpose(1, 0, 2).reshape(D_MODEL, N_DEV * rows)
        else:
            whole[n] = blk.reshape(N_DEV * rows, D_MODEL)
    return whole


def _pack_grads(grads, ranges):
    parts = []
    for n, (_, rows) in ranges.items():
        g = grads[n].astype(BF16)
        if n in COLUMN_SHARDED:
            g = g.reshape(D_MODEL, 4, 2, rows).transpose(2, 1, 0, 3).reshape(2, 4, rows, D_MODEL)
        else:
            g = g.reshape(4, 2, rows, D_MODEL).transpose(1, 0, 2, 3)
        parts.append(g)
    return jnp.concatenate(parts, axis=2)


def _unpack_shards(packed, ranges, shards):
    return {n: packed[off:off + rows, :].reshape(shards[n].shape) for n, (off, rows) in ranges.items()}


def _lane_rows(v):
    flat = v.reshape(-1)
    tile = SUBLANES * LANES
    flat = jnp.pad(flat, (0, (-flat.shape[0]) % tile))
    return flat.reshape(-1, LANES)


def _pack_small(values, names):
    return jnp.concatenate([_lane_rows(values[n]) for n in names], axis=0)


def _unpack_small(packed, like, names):
    out, off = {}, 0
    for n in names:
        size = like[n].size
        rows = _lane_rows(like[n]).shape[0]
        out[n] = packed[off:off + rows, :].reshape(-1)[:size].reshape(like[n].shape)
        off += rows
    return out


def kernel(x, mem, mix_pre_g, w_mix_in, conv_a_w, conv_b_w, conv_b_b, ln_b_g, ln_b_b, w_mix_out, mix_post_g, xa_pre_g, mem_norm_g, w_q, w_k, w_v, w_o, xa_post_g, ffn_pre_g, w_gate, w_up, w_down, ffn_post_g, loss_target, m_mix_pre_g, m_w_mix_in, m_conv_a_w, m_conv_b_w, m_conv_b_b, m_ln_b_g, m_ln_b_b, m_w_mix_out, m_mix_post_g, m_xa_pre_g, m_mem_norm_g, m_w_q, m_w_k, m_w_v, m_w_o, m_xa_post_g, m_ffn_pre_g, m_w_gate, m_w_up, m_w_down, m_ffn_post_g, v_mix_pre_g, v_w_mix_in, v_conv_a_w, v_conv_b_w, v_conv_b_b, v_ln_b_g, v_ln_b_b, v_w_mix_out, v_mix_post_g, v_xa_pre_g, v_mem_norm_g, v_w_q, v_w_k, v_w_v, v_w_o, v_xa_post_g, v_ffn_pre_g, v_w_gate, v_w_up, v_w_down, v_ffn_post_g):
    given = dict(locals())
    w = {n: given[n] for n in WEIGHTS}
    m = {n: given["m_" + n] for n in WEIGHTS}
    v = {n: given["v_" + n] for n in WEIGHTS}
    px, py, pc = _place()
    device = 4 * px + 2 * py + pc

    ranges = _row_ranges(w)
    gathered = _all_gather(_pack_shards(w).astype(BF16), "gather_weights")
    whole = _unpack_gathered(gathered, ranges)

    taps = _all_gather(_pack_small(w, CONV_TAPS), "gather_taps")
    conv = {n: w[n] for n in CHANNEL_VECTORS}
    off = 0
    for n in CONV_TAPS:
        k, cols = w[n].shape
        rows = _lane_rows(w[n]).shape[0]
        blk = taps[:, off:off + rows, :].reshape(N_DEV, -1)[:, :k * cols].reshape(N_DEV, k, cols)
        conv[n] = blk.transpose(1, 0, 2).reshape(k, N_DEV * cols)
        off += rows
    sq, dx, large_grads, small_grads = _local_grads(
        x[0], mem[0], loss_target[0], {n: w[n] for n in GAINS}, conv, whole)

    packed = _pack_grads(large_grads, ranges)
    from_sibling = _sibling_exchange(packed, "reduce_pair")
    where = jnp.stack([pc, 2 * px + py]).astype(jnp.int32)
    chip_sums = _pair_sum(where, packed, from_sibling, ADAM_ROWS_PER_STEP, "pair_sum")
    from_chips = _chip_exchange(chip_sums, "reduce_chips")
    g_l, d_l, nm_l, nv_l = _adamw([(chip_sums, 1), (from_chips, 3)], _pack_shards(w), _pack_shards(m), _pack_shards(v),
                                  ADAM_ROWS_PER_STEP, "adamw_large")
    grad, delta, new_m, new_v = (_unpack_shards(p, ranges, w) for p in (g_l, d_l, nm_l, nv_l))

    small_grads = dict(small_grads, loss=sq)
    names = SMALL + ("loss",)
    partial = _pack_small(small_grads, names)
    total = _unpack_small(_sum_parts(_all_gather(partial, "gather_small"), "sum_small"), small_grads, names)
    loss = jnp.sum(total.pop("loss")) * (0.5 / D_MODEL)
    for n in CONV_TAPS:
        total[n] = lax.dynamic_slice_in_dim(total[n], device * CONV_COLS_PER_DEVICE, CONV_COLS_PER_DEVICE, axis=1)
    total = {n: total[n].reshape(w[n].shape) for n in SMALL}
    packed_small = [_pack_small(values, SMALL) for values in (total, w, m, v)]
    g_s, d_s, nm_s, nv_s = _adamw([(packed_small[0][None], 1)], *packed_small[1:], packed_small[0].shape[0], "adamw_small")
    for out, p in ((grad, g_s), (delta, d_s), (new_m, nm_s), (new_v, nv_s)):
        out.update(_unpack_small(p, w, SMALL))

    return (loss, dx[None], *[grad[n] for n in WEIGHTS], *[delta[n] for n in WEIGHTS], *[new_m[n] for n in WEIGHTS],
            *[new_v[n] for n in WEIGHTS])
```

```python
import functools

import jax
import jax.numpy as jnp
from jax import lax
from jax.experimental import pallas as pl
from jax.experimental.pallas import tpu as pltpu

F32 = jnp.float32
BF16 = jnp.bfloat16

D_MODEL = 1024
D_A = 512
D_B = 512
D_IN_ALL = 2560
CONV_A_W = 3
CONV_B_W = 31
XA_HEADS = 4
XA_HEAD_DIM = 256
D_FF = 2816
N_DEV = 8
RMS_EPS = 1e-6
LN_EPS = 1e-5
ADAM_LR = 0.001
ADAM_B1 = 0.9
ADAM_B2 = 0.999
ADAM_EPS = 1e-08
ADAM_WD = 0.01
ADAM_STEP = 10

VMEM_LIMIT_BYTES = 56 * 1024 * 1024
SUBLANES = 8
LANES = 128
HALO_B = 32
HALO_A = 8
CONV_CHUNK = 32

MESH = pl.DeviceIdType.MESH


def _params(n_grid_axes=1):
    return pltpu.CompilerParams(dimension_semantics=("arbitrary",) * n_grid_axes, vmem_limit_bytes=VMEM_LIMIT_BYTES)


def _sds(shape, dtype):
    return jax.ShapeDtypeStruct(shape, dtype)


def _tile(rows, cols):
    return pl.BlockSpec((rows, cols), lambda i: (i, 0))


def _rtile(rows, cols, n):
    return pl.BlockSpec((rows, cols), lambda i: (n - 1 - i, 0))


def _whole(shape):
    zeros = (0,) * len(shape)
    return pl.BlockSpec(shape, lambda i: zeros)


def _resident(shape):
    zeros = (0,) * len(shape)
    return pl.BlockSpec(shape, lambda i: zeros, pipeline_mode=pl.Buffered(1))


def _dot(a, b):
    return jnp.dot(a, b, preferred_element_type=F32)


def _dot_nt(a, b):
    return lax.dot_general(a, b, (((1,), (1,)), ((), ())), preferred_element_type=F32)


def _dot_tn(a, b):
    return lax.dot_general(a, b, (((0,), (0,)), ((), ())), preferred_element_type=F32)


def _sigmoid(x):
    return 1.0 / (1.0 + jnp.exp(-x))


def _rms_fwd(x, g):
    r = lax.rsqrt(jnp.mean(x * x, axis=-1, keepdims=True) + RMS_EPS)
    return x * r * g


def _rms_bwd(dy, xin, g):
    r = lax.rsqrt(jnp.mean(xin * xin, axis=-1, keepdims=True) + RMS_EPS)
    n = xin * r
    dg = jnp.sum(dy * n, axis=0, keepdims=True)
    dn = dy * g
    dx = r * (dn - n * jnp.mean(dn * n, axis=-1, keepdims=True))
    return dx, dg


def _zero_at_first_step(*refs):
    @pl.when(pl.program_id(0) == 0)
    def _():
        for ref in refs:
            ref[...] = jnp.zeros(ref.shape, ref.dtype)


def _norm_matmul(x, g, w, tm, name):
    t, d = x.shape
    n = w.shape[1]

    def body(x_ref, g_ref, w_ref, h_ref, o_ref):
        h = _rms_fwd(x_ref[...], g_ref[...]).astype(BF16)
        h_ref[...] = h
        o_ref[...] = _dot(h, w_ref[...]).astype(BF16)

    return pl.pallas_call(
        body, name=name, grid=(t // tm,),
        in_specs=[_tile(tm, d), _whole((1, d)), _resident((d, n))],
        out_specs=[_tile(tm, d), _tile(tm, n)],
        out_shape=[_sds((t, d), BF16), _sds((t, n), BF16)],
        compiler_params=_params(),
    )(x, g, w)


def _conv_fwd(u, wa, wb, bb, lg, lb, tc, name):
    t = u.shape[0]
    n_chunks = tc // CONV_CHUNK

    def body(u_ref, wa_ref, wb_ref, bb_ref, lg_ref, lb_ref, y_ref, zc_ref, ca_ref, zbuf, cvbuf, zcbuf):
        @pl.when(pl.program_id(0) == 0)
        def _():
            zbuf[:, 0:HALO_B, :] = jnp.zeros((D_B // LANES, HALO_B, LANES), F32)
            cvbuf[:, 0:HALO_A, :] = jnp.zeros((D_A // LANES, HALO_A, LANES), F32)

        for lb_i in range(D_A // LANES):
            lanes = slice(lb_i * LANES, (lb_i + 1) * LANES)
            c_a = u_ref[:, D_A + lb_i * LANES:D_A + (lb_i + 1) * LANES].astype(F32)
            v_a = u_ref[:, 2 * D_A + lb_i * LANES:2 * D_A + (lb_i + 1) * LANES].astype(F32)
            cvbuf[lb_i, HALO_A:HALO_A + tc, :] = c_a * v_a
            glu_v = u_ref[:, 3 * D_A + lb_i * LANES:3 * D_A + (lb_i + 1) * LANES].astype(F32)
            glu_g = u_ref[:, 3 * D_A + D_B + lb_i * LANES:3 * D_A + D_B + (lb_i + 1) * LANES].astype(F32)
            zbuf[lb_i, HALO_B:HALO_B + tc, :] = glu_v * _sigmoid(glu_g)

            for c in range(n_chunks):
                r0 = c * CONV_CHUNK
                rows = slice(r0, r0 + CONV_CHUNK)
                acc = jnp.zeros((CONV_CHUNK, LANES), F32)
                for k in range(CONV_A_W):
                    off = r0 + HALO_A - (CONV_A_W - 1) + k
                    acc = acc + wa_ref[k:k + 1, lanes] * cvbuf[lb_i, off:off + CONV_CHUNK, :]
                ca_ref[rows, lanes] = acc.astype(BF16)
                y_ref[rows, lanes] = (u_ref[rows, lanes].astype(F32) * acc).astype(BF16)

                accb = jnp.zeros((CONV_CHUNK, LANES), F32)
                for k in range(CONV_B_W):
                    off = r0 + HALO_B - (CONV_B_W - 1) + k
                    accb = accb + wb_ref[k:k + 1, lanes] * zbuf[lb_i, off:off + CONV_CHUNK, :]
                zcbuf[rows, lanes] = accb + bb_ref[:, lanes]

            zbuf[lb_i, 0:HALO_B, :] = zbuf[lb_i, tc:tc + HALO_B, :]
            cvbuf[lb_i, 0:HALO_A, :] = cvbuf[lb_i, tc:tc + HALO_A, :]

        for c in range(n_chunks):
            rows = slice(c * CONV_CHUNK, (c + 1) * CONV_CHUNK)
            zc = zcbuf[rows, :]
            zc_ref[rows, :] = zc.astype(BF16)
            mu = jnp.mean(zc, axis=-1, keepdims=True)
            xc = zc - mu
            var = jnp.mean(xc * xc, axis=-1, keepdims=True)
            ln = xc * lax.rsqrt(var + LN_EPS) * lg_ref[...] + lb_ref[...]
            y_ref[rows, D_A:D_A + D_B] = (ln * _sigmoid(ln)).astype(BF16)

    return pl.pallas_call(
        body, name=name, grid=(t // tc,),
        in_specs=[_tile(tc, D_IN_ALL), _whole((CONV_A_W, D_A)), _whole((CONV_B_W, D_B)), _whole((1, D_B)),
                  _whole((1, D_B)), _whole((1, D_B))],
        out_specs=[_tile(tc, D_A + D_B), _tile(tc, D_B), _tile(tc, D_A)],
        out_shape=[_sds((t, D_A + D_B), BF16), _sds((t, D_B), BF16), _sds((t, D_A), BF16)],
        scratch_shapes=[pltpu.VMEM((D_B // LANES, HALO_B + tc, LANES), F32),
                        pltpu.VMEM((D_A // LANES, HALO_A + tc, LANES), F32), pltpu.VMEM((tc, D_B), F32)],
        compiler_params=_params(),
    )(u, wa, wb, bb, lg, lb)


def _proj_norm_res(a, w, xres, g, tm, name):
    t, k = a.shape
    d = w.shape[1]

    def body(a_ref, w_ref, x_ref, g_ref, xo_ref, s_ref):
        s = _dot(a_ref[...], w_ref[...])
        s_ref[...] = s.astype(BF16)
        xo_ref[...] = x_ref[...] + _rms_fwd(s, g_ref[...])

    return pl.pallas_call(
        body, name=name, grid=(t // tm,),
        in_specs=[_tile(tm, k), _resident((k, d)), _tile(tm, d), _whole((1, d))],
        out_specs=[_tile(tm, d), _tile(tm, d)],
        out_shape=[_sds((t, d), F32), _sds((t, d), BF16)],
        compiler_params=_params(),
    )(a, w, xres, g)


def _mem_fwd(mem, g, wk, wv, name):
    m, d = mem.shape

    def body(mem_ref, g_ref, wk_ref, wv_ref, n_ref, k_ref, v_ref):
        n = _rms_fwd(mem_ref[...], g_ref[...]).astype(BF16)
        n_ref[...] = n
        k_ref[...] = _dot(n, wk_ref[...]).astype(BF16)
        v_ref[...] = _dot(n, wv_ref[...]).astype(BF16)

    return pl.pallas_call(
        body, name=name, grid=(1,),
        in_specs=[_whole((m, d)), _whole((1, d)), _whole((d, d)), _whole((d, d))],
        out_specs=[_whole((m, d))] * 3,
        out_shape=[_sds((m, d), BF16)] * 3,
        compiler_params=_params(),
    )(mem, g, wk, wv)


def _softmax_rows(s):
    e = jnp.exp(s - jnp.max(s, axis=-1, keepdims=True))
    return e / jnp.sum(e, axis=-1, keepdims=True)


def _attn_fwd(x1, g_pre, wq, k, v, wo, g_post, tm, name):
    t, d = x1.shape
    m = k.shape[0]
    scale = XA_HEAD_DIM ** -0.5

    def body(x_ref, gp_ref, wq_ref, k_ref, v_ref, wo_ref, go_ref, x2_ref, h_ref, q_ref, o_ref, a_ref):
        x = x_ref[...]
        h = _rms_fwd(x, gp_ref[...]).astype(BF16)
        h_ref[...] = h
        q_ref[...] = _dot(h, wq_ref[...]).astype(BF16)
        for hd in range(XA_HEADS):
            cols = slice(hd * XA_HEAD_DIM, (hd + 1) * XA_HEAD_DIM)
            p = _softmax_rows(_dot_nt(q_ref[:, cols], k_ref[:, cols]) * scale)
            o_ref[:, cols] = _dot(p.astype(BF16), v_ref[:, cols]).astype(BF16)
        a = _dot(o_ref[...], wo_ref[...])
        a_ref[...] = a.astype(BF16)
        x2_ref[...] = x + _rms_fwd(a, go_ref[...])

    return pl.pallas_call(
        body, name=name, grid=(t // tm,),
        in_specs=[_tile(tm, d), _whole((1, d)), _resident((d, d)), _whole((m, d)), _whole((m, d)), _resident((d, d)),
                  _whole((1, d))],
        out_specs=[_tile(tm, d)] * 5,
        out_shape=[_sds((t, d), F32)] + [_sds((t, d), BF16)] * 4,
        compiler_params=_params(),
    )(x1, g_pre, wq, k, v, wo, g_post)


def _ffn_fwd(x2, g_pre, wg, wu, wd, g_post, target, tm, name):
    t, d = x2.shape
    f = wg.shape[1]

    def body(x_ref, gp_ref, wg_ref, wu_ref, wd_ref, go_ref, tgt_ref, h_ref, gt_ref, up_ref, f_ref, sq_ref):
        _zero_at_first_step(sq_ref)
        x = x_ref[...]
        h = _rms_fwd(x, gp_ref[...]).astype(BF16)
        h_ref[...] = h
        gt = _dot(h, wg_ref[...])
        up = _dot(h, wu_ref[...])
        gt_ref[...] = gt.astype(BF16)
        up_ref[...] = up.astype(BF16)
        hd = (gt * _sigmoid(gt) * up).astype(BF16)
        ff = _dot(hd, wd_ref[...])
        f_ref[...] = ff.astype(BF16)
        err = x + _rms_fwd(ff, go_ref[...]) - tgt_ref[...]
        sq_ref[...] += jnp.sum(err * err, axis=0, keepdims=True)

    return pl.pallas_call(
        body, name=name, grid=(t // tm,),
        in_specs=[_tile(tm, d), _whole((1, d)), _resident((d, f)), _resident((d, f)), _resident((f, d)), _whole((1, d)),
                  _tile(tm, d)],
        out_specs=[_tile(tm, d), _tile(tm, f), _tile(tm, f), _tile(tm, d), _whole((1, d))],
        out_shape=[_sds((t, d), BF16), _sds((t, f), BF16), _sds((t, f), BF16), _sds((t, d), BF16), _sds((1, d), F32)],
        compiler_params=_params(),
    )(x2, g_pre, wg, wu, wd, g_post, target)


def _ffn_bwd(x2, f, target, gt, up, g_pre, wg, wu, wd, g_post, tm, name):
    t, d = x2.shape
    ff = wg.shape[1]

    def body(x_ref, f_ref, tgt_ref, gt_ref, up_ref, gp_ref, wg_ref, wu_ref, wd_ref, go_ref,
             dx_ref, df_ref, hd_ref, dgt_ref, dup_ref, dgo_ref, dgp_ref):
        _zero_at_first_step(dgo_ref, dgp_ref)
        x = x_ref[...]
        fo = f_ref[...].astype(F32)
        dx3 = (x + _rms_fwd(fo, go_ref[...]) - tgt_ref[...]) * (1.0 / d)
        df, dgo = _rms_bwd(dx3, fo, go_ref[...])
        dgo_ref[...] += dgo
        df = df.astype(BF16)
        df_ref[...] = df
        dhd = _dot_nt(df, wd_ref[...])
        gt = gt_ref[...].astype(F32)
        up = up_ref[...].astype(F32)
        sg = _sigmoid(gt)
        si = gt * sg
        hd_ref[...] = (si * up).astype(BF16)
        dup = (dhd * si).astype(BF16)
        dgt = (dhd * up * (sg * (1.0 + gt * (1.0 - sg)))).astype(BF16)
        dup_ref[...] = dup
        dgt_ref[...] = dgt
        dh = _dot_nt(dgt, wg_ref[...]) + _dot_nt(dup, wu_ref[...])
        dxn, dgp = _rms_bwd(dh, x, gp_ref[...])
        dgp_ref[...] += dgp
        dx_ref[...] = dx3 + dxn

    return pl.pallas_call(
        body, name=name, grid=(t // tm,),
        in_specs=[_tile(tm, d), _tile(tm, d), _tile(tm, d), _tile(tm, ff), _tile(tm, ff), _whole((1, d)),
                  _resident((d, ff)), _resident((d, ff)), _resident((ff, d)), _whole((1, d))],
        out_specs=[_tile(tm, d), _tile(tm, d), _tile(tm, ff), _tile(tm, ff), _tile(tm, ff), _whole((1, d)), _whole((1, d))],
        out_shape=[_sds((t, d), F32), _sds((t, d), BF16), _sds((t, ff), BF16), _sds((t, ff), BF16), _sds((t, ff), BF16),
                   _sds((1, d), F32), _sds((1, d), F32)],
        compiler_params=_params(),
    )(x2, f, target, gt, up, g_pre, wg, wu, wd, g_post)


def _attn_bwd(dx2, a, x1, q, k, v, g_pre, wq, wo, g_post, tm, name):
    t, d = x1.shape
    m = k.shape[0]
    scale = XA_HEAD_DIM ** -0.5

    def body(dx2_ref, a_ref, x_ref, q_ref, k_ref, v_ref, gp_ref, wq_ref, wo_ref, go_ref,
             dx1_ref, da_ref, dq_ref, dk_ref, dv_ref, dgo_ref, dgp_ref, do_buf):
        _zero_at_first_step(dgo_ref, dgp_ref, dk_ref, dv_ref)
        dx2 = dx2_ref[...]
        da, dgo = _rms_bwd(dx2, a_ref[...].astype(F32), go_ref[...])
        dgo_ref[...] += dgo
        da = da.astype(BF16)
        da_ref[...] = da
        do_buf[...] = _dot_nt(da, wo_ref[...]).astype(BF16)
        for hd in range(XA_HEADS):
            cols = slice(hd * XA_HEAD_DIM, (hd + 1) * XA_HEAD_DIM)
            qh = q_ref[:, cols]
            p = _softmax_rows(_dot_nt(qh, k_ref[:, cols]) * scale)
            do_h = do_buf[:, cols]
            dp = _dot_nt(do_h, v_ref[:, cols])
            dv_ref[:, cols] += _dot_tn(p.astype(BF16), do_h)
            ds = (p * (dp - jnp.sum(dp * p, axis=-1, keepdims=True)) * scale).astype(BF16)
            dq_ref[:, cols] = _dot(ds, k_ref[:, cols]).astype(BF16)
            dk_ref[:, cols] += _dot_tn(ds, qh)
        dh = _dot_nt(dq_ref[...], wq_ref[...])
        dxn, dgp = _rms_bwd(dh, x_ref[...], gp_ref[...])
        dgp_ref[...] += dgp
        dx1_ref[...] = dx2 + dxn

    return pl.pallas_call(
        body, name=name, grid=(t // tm,),
        in_specs=[_tile(tm, d), _tile(tm, d), _tile(tm, d), _tile(tm, d), _whole((m, d)), _whole((m, d)), _whole((1, d)),
                  _resident((d, d)), _resident((d, d)), _whole((1, d))],
        out_specs=[_tile(tm, d), _tile(tm, d), _tile(tm, d), _whole((m, d)), _whole((m, d)), _whole((1, d)), _whole((1, d))],
        out_shape=[_sds((t, d), F32), _sds((t, d), BF16), _sds((t, d), BF16), _sds((m, d), F32), _sds((m, d), F32),
                   _sds((1, d), F32), _sds((1, d), F32)],
        scratch_shapes=[pltpu.VMEM((tm, d), BF16)],
        compiler_params=_params(),
    )(dx2, a, x1, q, k, v, g_pre, wq, wo, g_post)


def _mem_bwd(mem, mem_n, dk, dv, g, wk, wv, name):
    m, d = mem.shape

    def body(mem_ref, n_ref, dk_ref, dv_ref, g_ref, wk_ref, wv_ref, dwk_ref, dwv_ref, dg_ref):
        dk = dk_ref[...].astype(BF16)
        dv = dv_ref[...].astype(BF16)
        n = n_ref[...]
        dwk_ref[...] = _dot_tn(n, dk)
        dwv_ref[...] = _dot_tn(n, dv)
        dn = _dot_nt(dk, wk_ref[...]) + _dot_nt(dv, wv_ref[...])
        _, dg = _rms_bwd(dn, mem_ref[...], g_ref[...])
        dg_ref[...] = dg

    return pl.pallas_call(
        body, name=name, grid=(1,),
        in_specs=[_whole((m, d)), _whole((m, d)), _whole((m, d)), _whole((m, d)), _whole((1, d)), _whole((d, d)),
                  _whole((d, d))],
        out_specs=[_whole((d, d)), _whole((d, d)), _whole((1, d))],
        out_shape=[_sds((d, d), F32), _sds((d, d), F32), _sds((1, d), F32)],
        compiler_params=_params(),
    )(mem, mem_n, dk, dv, g, wk, wv)


def _proj_bwd(dxo, s, w, g, tm, name):
    t, d = dxo.shape
    k = w.shape[0]

    def body(dx_ref, s_ref, w_ref, g_ref, ds_ref, da_ref, dg_ref):
        _zero_at_first_step(dg_ref)
        ds, dg = _rms_bwd(dx_ref[...], s_ref[...].astype(F32), g_ref[...])
        dg_ref[...] += dg
        ds = ds.astype(BF16)
        ds_ref[...] = ds
        da_ref[...] = _dot_nt(ds, w_ref[...]).astype(BF16)

    return pl.pallas_call(
        body, name=name, grid=(t // tm,),
        in_specs=[_tile(tm, d), _tile(tm, d), _resident((k, d)), _whole((1, d))],
        out_specs=[_tile(tm, d), _tile(tm, k), _whole((1, d))],
        out_shape=[_sds((t, d), BF16), _sds((t, k), BF16), _sds((1, d), F32)],
        compiler_params=_params(),
    )(dxo, s, w, g)


def _conv_bwd(dy, u, zc, ca, wa, wb, lg, lb, tc, name):
    t = u.shape[0]
    n_tiles = t // tc
    n_chunks = tc // CONV_CHUNK

    def body(dy_ref, u_ref, zc_ref, ca_ref, wa_ref, wb_ref, lg_ref, lb_ref,
             du_ref, dwa_ref, dwb_ref, dbb_ref, dlg_ref, dlb_ref, ebuf, eabuf, zbuf, cvbuf, wacc, aacc, vacc):
        step = pl.program_id(0)

        @pl.when(step == 0)
        def _():
            ebuf[:, tc:tc + HALO_B, :] = jnp.zeros((D_B // LANES, HALO_B, LANES), F32)
            eabuf[:, tc:tc + HALO_A, :] = jnp.zeros((D_A // LANES, HALO_A, LANES), F32)
            wacc[...] = jnp.zeros_like(wacc)
            aacc[...] = jnp.zeros_like(aacc)
            vacc[...] = jnp.zeros_like(vacc)

        dbb = jnp.zeros((SUBLANES, D_B), F32)
        dlg = jnp.zeros((SUBLANES, D_B), F32)
        dlb = jnp.zeros((SUBLANES, D_B), F32)
        for c in range(n_chunks):
            rows = slice(c * CONV_CHUNK, (c + 1) * CONV_CHUNK)
            dy_a = dy_ref[rows, 0:D_A].astype(F32)
            b_a = u_ref[rows, 0:D_A].astype(F32)
            du_ref[rows, 0:D_A] = (dy_a * ca_ref[rows, :].astype(F32)).astype(BF16)
            dca = dy_a * b_a
            cv = u_ref[rows, D_A:2 * D_A].astype(F32) * u_ref[rows, 2 * D_A:3 * D_A].astype(F32)
            glu_g = u_ref[rows, 3 * D_A + D_B:3 * D_A + 2 * D_B].astype(F32)
            z = u_ref[rows, 3 * D_A:3 * D_A + D_B].astype(F32) * _sigmoid(glu_g)
            for lb_i in range(D_A // LANES):
                lanes = slice(lb_i * LANES, (lb_i + 1) * LANES)
                eabuf[lb_i, rows, :] = dca[:, lanes]
                cvbuf[lb_i, rows, :] = cv[:, lanes]
                zbuf[lb_i, rows, :] = z[:, lanes]

            zcv = zc_ref[rows, :].astype(F32)
            mu = jnp.mean(zcv, axis=-1, keepdims=True)
            xc = zcv - mu
            rstd = lax.rsqrt(jnp.mean(xc * xc, axis=-1, keepdims=True) + LN_EPS)
            xhat = xc * rstd
            ln = xhat * lg_ref[...] + lb_ref[...]
            sg = _sigmoid(ln)
            dln = dy_ref[rows, D_A:D_A + D_B].astype(F32) * (sg * (1.0 + ln * (1.0 - sg)))
            dlg = dlg + jnp.sum((dln * xhat).reshape(CONV_CHUNK // SUBLANES, SUBLANES, D_B), axis=0)
            dlb = dlb + jnp.sum(dln.reshape(CONV_CHUNK // SUBLANES, SUBLANES, D_B), axis=0)
            dxh = dln * lg_ref[...]
            dzc = rstd * (dxh - jnp.mean(dxh, axis=-1, keepdims=True) - xhat * jnp.mean(dxh * xhat, axis=-1, keepdims=True))
            dbb = dbb + jnp.sum(dzc.reshape(CONV_CHUNK // SUBLANES, SUBLANES, D_B), axis=0)
            for lb_i in range(D_B // LANES):
                ebuf[lb_i, rows, :] = dzc[:, lb_i * LANES:(lb_i + 1) * LANES]
        vacc[0] += dbb
        vacc[1] += dlg
        vacc[2] += dlb

        for lb_i in range(D_A // LANES):
            lanes = slice(lb_i * LANES, (lb_i + 1) * LANES)

            def cols(first):
                return slice(first + lb_i * LANES, first + (lb_i + 1) * LANES)

            for c in range(n_chunks):
                r0 = c * CONV_CHUNK
                rows = slice(r0, r0 + CONV_CHUNK)
                cv = cvbuf[lb_i, rows, :]
                dcv = jnp.zeros((CONV_CHUNK, LANES), F32)
                for k in range(CONV_A_W):
                    off = r0 + (CONV_A_W - 1) - k
                    e = eabuf[lb_i, off:off + CONV_CHUNK, :]
                    dcv = dcv + wa_ref[k:k + 1, lanes] * e
                    aacc[k, :, lanes] += jnp.sum((cv * e).reshape(CONV_CHUNK // SUBLANES, SUBLANES, LANES), axis=0)
                du_ref[rows, cols(D_A)] = (dcv * u_ref[rows, cols(2 * D_A)].astype(F32)).astype(BF16)
                du_ref[rows, cols(2 * D_A)] = (dcv * u_ref[rows, cols(D_A)].astype(F32)).astype(BF16)

                z = zbuf[lb_i, rows, :]
                dz = jnp.zeros((CONV_CHUNK, LANES), F32)
                for k in range(CONV_B_W):
                    off = r0 + (CONV_B_W - 1) - k
                    e = ebuf[lb_i, off:off + CONV_CHUNK, :]
                    dz = dz + wb_ref[k:k + 1, lanes] * e
                    wacc[k, :, lanes] += jnp.sum((z * e).reshape(CONV_CHUNK // SUBLANES, SUBLANES, LANES), axis=0)
                glu_v = u_ref[rows, cols(3 * D_A)].astype(F32)
                sgg = _sigmoid(u_ref[rows, cols(3 * D_A + D_B)].astype(F32))
                du_ref[rows, cols(3 * D_A)] = (dz * sgg).astype(BF16)
                du_ref[rows, cols(3 * D_A + D_B)] = (dz * glu_v * sgg * (1.0 - sgg)).astype(BF16)

            ebuf[lb_i, tc:tc + HALO_B, :] = ebuf[lb_i, 0:HALO_B, :]
            eabuf[lb_i, tc:tc + HALO_A, :] = eabuf[lb_i, 0:HALO_A, :]

        @pl.when(step == n_tiles - 1)
        def _():
            for k in range(CONV_B_W):
                dwb_ref[k:k + 1, :] = jnp.sum(wacc[k], axis=0, keepdims=True)
            for k in range(CONV_A_W):
                dwa_ref[k:k + 1, :] = jnp.sum(aacc[k], axis=0, keepdims=True)
            dbb_ref[...] = jnp.sum(vacc[0], axis=0, keepdims=True)
            dlg_ref[...] = jnp.sum(vacc[1], axis=0, keepdims=True)
            dlb_ref[...] = jnp.sum(vacc[2], axis=0, keepdims=True)

    return pl.pallas_call(
        body, name=name, grid=(n_tiles,),
        in_specs=[_rtile(tc, D_A + D_B, n_tiles), _rtile(tc, D_IN_ALL, n_tiles), _rtile(tc, D_B, n_tiles),
                  _rtile(tc, D_A, n_tiles), _whole((CONV_A_W, D_A)), _whole((CONV_B_W, D_B)), _whole((1, D_B)),
                  _whole((1, D_B))],
        out_specs=[_rtile(tc, D_IN_ALL, n_tiles), _whole((CONV_A_W, D_A)), _whole((CONV_B_W, D_B)), _whole((1, D_B)),
                   _whole((1, D_B)), _whole((1, D_B))],
        out_shape=[_sds((t, D_IN_ALL), BF16), _sds((CONV_A_W, D_A), F32), _sds((CONV_B_W, D_B), F32), _sds((1, D_B), F32),
                   _sds((1, D_B), F32), _sds((1, D_B), F32)],
        scratch_shapes=[pltpu.VMEM((D_B // LANES, tc + HALO_B, LANES), F32), pltpu.VMEM((D_A // LANES, tc + HALO_A, LANES), F32),
                        pltpu.VMEM((D_B // LANES, tc, LANES), F32), pltpu.VMEM((D_A // LANES, tc, LANES), F32),
                        pltpu.VMEM((CONV_B_W, SUBLANES, D_B), F32), pltpu.VMEM((CONV_A_W, SUBLANES, D_A), F32),
                        pltpu.VMEM((3, SUBLANES, D_B), F32)],
        compiler_params=_params(),
    )(dy, u, zc, ca, wa, wb, lg, lb)


def _in_bwd(du, w, x, dx1, g, tm, name):
    t, d = x.shape
    n = w.shape[1]

    def body(du_ref, w_ref, x_ref, dx1_ref, g_ref, dx_ref, dg_ref):
        _zero_at_first_step(dg_ref)
        dh = _dot_nt(du_ref[...], w_ref[...])
        dxn, dg = _rms_bwd(dh, x_ref[...], g_ref[...])
        dg_ref[...] += dg
        dx_ref[...] = dx1_ref[...] + dxn

    return pl.pallas_call(
        body, name=name, grid=(t // tm,),
        in_specs=[_tile(tm, n), _resident((d, n)), _tile(tm, d), _tile(tm, d), _whole((1, d))],
        out_specs=[_tile(tm, d), _whole((1, d))],
        out_shape=[_sds((t, d), F32), _sds((1, d), F32)],
        compiler_params=_params(),
    )(du, w, x, dx1, g)


def _wgrad(a, b, tk, bm, bn, name):
    t, m = a.shape
    n = b.shape[1]

    def body(a_ref, b_ref, o_ref):
        @pl.when(pl.program_id(2) == 0)
        def _():
            o_ref[...] = jnp.zeros_like(o_ref)

        o_ref[...] += _dot_tn(a_ref[...], b_ref[...])

    return pl.pallas_call(
        body, name=name, grid=(m // bm, n // bn, t // tk),
        in_specs=[pl.BlockSpec((tk, bm), lambda i, j, k: (k, i)), pl.BlockSpec((tk, bn), lambda i, j, k: (k, j))],
        out_specs=pl.BlockSpec((bm, bn), lambda i, j, k: (i, j)),
        out_shape=_sds((m, n), F32),
        compiler_params=_params(3),
    )(a, b)


def _adamw(parts, w, m, v, tr, name):
    r, c = w.shape
    counts = [n for _, n in parts]

    def body(*refs):
        p_refs = refs[:len(parts)]
        w_ref, m_ref, v_ref, g_ref, d_ref, nm_ref, nv_ref = refs[len(parts):]
        g = None
        for p_ref, n in zip(p_refs, counts):
            for j in range(n):
                g = p_ref[j].astype(F32) if g is None else g + p_ref[j].astype(F32)
        g_ref[...] = g
        nm = ADAM_B1 * m_ref[...] + (1.0 - ADAM_B1) * g
        nv = ADAM_B2 * v_ref[...] + (1.0 - ADAM_B2) * (g * g)
        nm_ref[...] = nm
        nv_ref[...] = nv
        m_hat = nm / (1.0 - ADAM_B1 ** ADAM_STEP)
        v_hat = nv / (1.0 - ADAM_B2 ** ADAM_STEP)
        d_ref[...] = -ADAM_LR * (m_hat / (jnp.sqrt(v_hat) + ADAM_EPS) + ADAM_WD * w_ref[...])

    return pl.pallas_call(
        body, name=name, grid=(r // tr,),
        in_specs=[pl.BlockSpec((n, tr, c), lambda i: (0, i, 0)) for n in counts] + [_tile(tr, c)] * 3,
        out_specs=[_tile(tr, c)] * 4,
        out_shape=[_sds((r, c), F32)] * 4,
        compiler_params=_params(),
    )(*[p for p, _ in parts], w, m, v)


def _sum_parts(parts, name):
    n_parts, r, c = parts.shape

    def body(p_ref, o_ref):
        acc = p_ref[0]
        for j in range(1, n_parts):
            acc = acc + p_ref[j]
        o_ref[...] = acc

    return pl.pallas_call(
        body, name=name, grid=(1,),
        in_specs=[_whole((n_parts, r, c))], out_specs=_whole((r, c)), out_shape=_sds((r, c), F32),
        compiler_params=_params(),
    )(parts)


def _pair_sum(where, g, recv, tr, name):
    _, n_chips, r, c = g.shape

    def body(where_ref, g_ref, r_ref, o_ref):
        del where_ref
        o_ref[...] = (g_ref[...].astype(F32) + r_ref[...].astype(F32)).astype(BF16)

    grid_spec = pltpu.PrefetchScalarGridSpec(
        num_scalar_prefetch=1, grid=(n_chips, r // tr),
        in_specs=[pl.BlockSpec((None, None, tr, c), lambda k, i, s: (s[0], s[1] ^ k, i, 0)),
                  pl.BlockSpec((None, tr, c), lambda k, i, s: (s[1] ^ k, i, 0))],
        out_specs=pl.BlockSpec((None, tr, c), lambda k, i, s: (k, i, 0)))
    return pl.pallas_call(
        body, name=name, grid_spec=grid_spec, out_shape=_sds((n_chips, r, c), BF16), compiler_params=_params(2),
    )(where, g, recv)


def _place():
    return lax.axis_index("x"), lax.axis_index("y"), lax.axis_index("c")


def _all_gather(block, name):
    r, c = block.shape

    def body(x_ref, out_ref, send_sems, recv_sems, local_sem):
        x, y, cc = _place()
        me, sibling = (x, y, cc), (x, y, 1 - cc)
        chips = [(1 - x, y), (x, 1 - y), (1 - x, 1 - y)]

        def slot(px, py, pc):
            return out_ref.at[4 * px + 2 * py + pc]

        def copy(k, owner, to, src=None):
            return pltpu.make_async_remote_copy(
                src_ref=slot(*owner) if src is None else src, dst_ref=slot(*owner),
                send_sem=send_sems.at[k], recv_sem=recv_sems.at[k], device_id=to, device_id_type=MESH)

        mine = pltpu.make_async_copy(x_ref, slot(*me), local_sem)
        mine.start()
        first = [copy(0, me, sibling, src=x_ref)]
        first += [copy(1 + j, me, (*chip, cc), src=x_ref) for j, chip in enumerate(chips)]
        for cp in first:
            cp.start()
        passed = [copy(4 + j, (*chip, cc), sibling) for j, chip in enumerate(chips)]
        for j, chip in enumerate(chips):
            copy(1 + j, (*chip, cc), me).wait_recv()
            passed[j].start()
        copy(0, sibling, me).wait_recv()
        for j, chip in enumerate(chips):
            copy(4 + j, (*chip, 1 - cc), me).wait_recv()
        for cp in first + passed:
            cp.wait_send()
        mine.wait()

    return pl.pallas_call(
        body, name=name,
        in_specs=[pl.BlockSpec(memory_space=pl.ANY)], out_specs=pl.BlockSpec(memory_space=pl.ANY),
        out_shape=_sds((N_DEV, r, c), block.dtype),
        scratch_shapes=[pltpu.SemaphoreType.DMA((7,)), pltpu.SemaphoreType.DMA((7,)), pltpu.SemaphoreType.DMA],
    )(block)


def _sibling_exchange(g, name):
    _, n_chips, r, c = g.shape

    def body(g_ref, out_ref, send_sem, recv_sem):
        x, y, cc = _place()
        cp = pltpu.make_async_remote_copy(
            src_ref=g_ref.at[1 - cc], dst_ref=out_ref, send_sem=send_sem, recv_sem=recv_sem,
            device_id=(x, y, 1 - cc), device_id_type=MESH)
        cp.start()
        cp.wait()

    return pl.pallas_call(
        body, name=name,
        in_specs=[pl.BlockSpec(memory_space=pl.ANY)], out_specs=pl.BlockSpec(memory_space=pl.ANY),
        out_shape=_sds((n_chips, r, c), g.dtype),
        scratch_shapes=[pltpu.SemaphoreType.DMA, pltpu.SemaphoreType.DMA],
    )(g)


def _chip_exchange(p, name):
    _, r, c = p.shape

    def body(p_ref, out_ref, send_sems, recv_sems):
        x, y, cc = _place()
        peers = [(x, 1 - y, cc), (1 - x, y, cc), (1 - x, 1 - y, cc)]
        copies = [
            pltpu.make_async_remote_copy(
                src_ref=p_ref.at[1 + j], dst_ref=out_ref.at[j], send_sem=send_sems.at[j], recv_sem=recv_sems.at[j],
                device_id=peer, device_id_type=MESH)
            for j, peer in enumerate(peers)]
        for cp in copies:
            cp.start()
        for cp in copies:
            cp.wait()

    return pl.pallas_call(
        body, name=name,
        in_specs=[pl.BlockSpec(memory_space=pl.ANY)], out_specs=pl.BlockSpec(memory_space=pl.ANY),
        out_shape=_sds((3, r, c), p.dtype),
        scratch_shapes=[pltpu.SemaphoreType.DMA((3,)), pltpu.SemaphoreType.DMA((3,))],
    )(p)


def _row(v):
    return v.reshape(1, -1)


def _local_grads(x, mem, target, gains, conv, big, tm=512, tm_ffn=256, tc=256, tk=2048):
    g = {k: _row(v) for k, v in gains.items()}
    wa, wb = conv["conv_a_w"], conv["conv_b_w"]
    bb, lg, lb = _row(conv["conv_b_b"]), _row(conv["ln_b_g"]), _row(conv["ln_b_b"])

    mem_n, k, v = _mem_fwd(mem, g["mem_norm_g"], big["w_k"], big["w_v"], "mem_fwd")
    h1, u = _norm_matmul(x, g["mix_pre_g"], big["w_mix_in"], tm, "mix_in_fwd")
    ycat, zc, ca = _conv_fwd(u, wa, wb, bb, lg, lb, tc, "conv_fwd")
    x1, s1 = _proj_norm_res(ycat, big["w_mix_out"], x, g["mix_post_g"], tm, "mix_out_fwd")
    x2, h2, q, o, a = _attn_fwd(x1, g["xa_pre_g"], big["w_q"], k, v, big["w_o"], g["xa_post_g"], tm, "attn_fwd")
    h3, gt, up, f, sq = _ffn_fwd(x2, g["ffn_pre_g"], big["w_gate"], big["w_up"], big["w_down"], g["ffn_post_g"], target,
                                 tm_ffn, "ffn_fwd")

    dx2, df, hd, dgt, dup, d_ffn_post, d_ffn_pre = _ffn_bwd(
        x2, f, target, gt, up, g["ffn_pre_g"], big["w_gate"], big["w_up"], big["w_down"], g["ffn_post_g"], tm_ffn, "ffn_bwd")
    d_w_down = _wgrad(hd, df, tk, D_FF // 2, D_MODEL, "wgrad_down")
    d_w_gate = _wgrad(h3, dgt, tk, D_MODEL, D_FF // 2, "wgrad_gate")
    d_w_up = _wgrad(h3, dup, tk, D_MODEL, D_FF // 2, "wgrad_up")

    dx1, da, dq, dk, dv, d_xa_post, d_xa_pre = _attn_bwd(
        dx2, a, x1, q, k, v, g["xa_pre_g"], big["w_q"], big["w_o"], g["xa_post_g"], tm, "attn_bwd")
    d_w_o = _wgrad(o, da, tk, D_MODEL, D_MODEL, "wgrad_o")
    d_w_q = _wgrad(h2, dq, tk, D_MODEL, D_MODEL, "wgrad_q")
    d_w_k, d_w_v, d_mem_norm = _mem_bwd(mem, mem_n, dk, dv, g["mem_norm_g"], big["w_k"], big["w_v"], "mem_bwd")

    ds1, dycat, d_mix_post = _proj_bwd(dx1, s1, big["w_mix_out"], g["mix_post_g"], tm, "mix_out_bwd")
    d_w_mix_out = _wgrad(ycat, ds1, tk, D_MODEL, D_MODEL, "wgrad_mix_out")
    du, d_conv_a, d_conv_b, d_conv_bb, d_ln_g, d_ln_b = _conv_bwd(dycat, u, zc, ca, wa, wb, lg, lb, tc, "conv_bwd")
    d_w_mix_in = _wgrad(h1, du, tk, D_MODEL, D_IN_ALL // 2, "wgrad_mix_in")
    dx, d_mix_pre = _in_bwd(du, big["w_mix_in"], x, dx1, g["mix_pre_g"], tm, "mix_in_bwd")

    big_grads = dict(w_mix_in=d_w_mix_in, w_mix_out=d_w_mix_out, w_q=d_w_q, w_k=d_w_k, w_v=d_w_v, w_o=d_w_o,
                     w_gate=d_w_gate, w_up=d_w_up, w_down=d_w_down)
    small_grads = dict(mix_pre_g=d_mix_pre, conv_a_w=d_conv_a, conv_b_w=d_conv_b, conv_b_b=d_conv_bb, ln_b_g=d_ln_g,
                       ln_b_b=d_ln_b, mix_post_g=d_mix_post, xa_pre_g=d_xa_pre, mem_norm_g=d_mem_norm, xa_post_g=d_xa_post,
                       ffn_pre_g=d_ffn_pre, ffn_post_g=d_ffn_post)
    return sq, dx, big_grads, small_grads


WEIGHTS = ("mix_pre_g", "w_mix_in", "conv_a_w", "conv_b_w", "conv_b_b", "ln_b_g", "ln_b_b", "w_mix_out", "mix_post_g",
           "xa_pre_g", "mem_norm_g", "w_q", "w_k", "w_v", "w_o", "xa_post_g", "ffn_pre_g", "w_gate", "w_up", "w_down",
           "ffn_post_g")
LARGE = ("w_mix_in", "w_mix_out", "w_q", "w_k", "w_v", "w_o", "w_gate", "w_up", "w_down")
COLUMN_SHARDED = ("w_mix_in", "w_gate", "w_up")
GAINS = ("mix_pre_g", "mix_post_g", "xa_pre_g", "mem_norm_g", "xa_post_g", "ffn_pre_g", "ffn_post_g")
CHANNEL_VECTORS = ("conv_b_b", "ln_b_g", "ln_b_b")
CONV_TAPS = ("conv_a_w", "conv_b_w")
SMALL = GAINS + CHANNEL_VECTORS + CONV_TAPS
CONV_COLS_PER_DEVICE = D_A // N_DEV
ADAM_ROWS_PER_STEP = 288


def _shard_rows(name, shard):
    return shard.reshape(-1, D_MODEL) if name in COLUMN_SHARDED else shard


def _pack_shards(shards):
    return jnp.concatenate([_shard_rows(n, shards[n]) for n in LARGE], axis=0)


def _row_ranges(shards):
    ranges, off = {}, 0
    for n in LARGE:
        rows = _shard_rows(n, shards[n]).shape[0]
        ranges[n] = (off, rows)
        off += rows
    return ranges


def _unpack_gathered(gathered, ranges):
    whole = {}
    for n, (off, rows) in ranges.items():
        blk = gathered[:, off:off + rows, :]
        if n in COLUMN_SHARDED:
            whole[n] = blk.reshape(N_DEV, D_MODEL, rows).transpose(1, 0, 2).reshape(D_MODEL, N_DEV * rows)
        else:
            whole[n] = blk.reshape(N_DEV * rows, D_MODEL)
    return whole


def _pack_grads(grads, ranges):
    parts = []
    for n, (_, rows) in ranges.items():
        g = grads[n].astype(BF16)
        if n in COLUMN_SHARDED:
            g = g.reshape(D_MODEL, 4, 2, rows).transpose(2, 1, 0, 3).reshape(2, 4, rows, D_MODEL)
        else:
            g = g.reshape(4, 2, rows, D_MODEL).transpose(1, 0, 2, 3)
        parts.append(g)
    return jnp.concatenate(parts, axis=2)


def _unpack_shards(packed, ranges, shards):
    return {n: packed[off:off + rows, :].reshape(shards[n].shape) for n, (off, rows) in ranges.items()}


def _lane_rows(v):
    flat = v.reshape(-1)
    tile = SUBLANES * LANES
    flat = jnp.pad(flat, (0, (-flat.shape[0]) % tile))
    return flat.reshape(-1, LANES)


def _pack_small(values, names):
    return jnp.concatenate([_lane_rows(values[n]) for n in names], axis=0)


def _unpack_small(packed, like, names):
    out, off = {}, 0
    for n in names:
        size = like[n].size
        rows = _lane_rows(like[n]).shape[0]
        out[n] = packed[off:off + rows, :].reshape(-1)[:size].reshape(like[n].shape)
        off += rows
    return out


def kernel(x, mem, mix_pre_g, w_mix_in, conv_a_w, conv_b_w, conv_b_b, ln_b_g, ln_b_b, w_mix_out, mix_post_g, xa_pre_g, mem_norm_g, w_q, w_k, w_v, w_o, xa_post_g, ffn_pre_g, w_gate, w_up, w_down, ffn_post_g, loss_target, m_mix_pre_g, m_w_mix_in, m_conv_a_w, m_conv_b_w, m_conv_b_b, m_ln_b_g, m_ln_b_b, m_w_mix_out, m_mix_post_g, m_xa_pre_g, m_mem_norm_g, m_w_q, m_w_k, m_w_v, m_w_o, m_xa_post_g, m_ffn_pre_g, m_w_gate, m_w_up, m_w_down, m_ffn_post_g, v_mix_pre_g, v_w_mix_in, v_conv_a_w, v_conv_b_w, v_conv_b_b, v_ln_b_g, v_ln_b_b, v_w_mix_out, v_mix_post_g, v_xa_pre_g, v_mem_norm_g, v_w_q, v_w_k, v_w_v, v_w_o, v_xa_post_g, v_ffn_pre_g, v_w_gate, v_w_up, v_w_down, v_ffn_post_g):
    given = dict(locals())
    w = {n: given[n] for n in WEIGHTS}
    m = {n: given["m_" + n] for n in WEIGHTS}
    v = {n: given["v_" + n] for n in WEIGHTS}
    px, py, pc = _place()
    device = 4 * px + 2 * py + pc

    ranges = _row_ranges(w)
    gathered = _all_gather(_pack_shards(w).astype(BF16), "gather_weights")
    whole = _unpack_gathered(gathered, ranges)

    taps = _all_gather(_pack_small(w, CONV_TAPS), "gather_taps")
    conv = {n: w[n] for n in CHANNEL_VECTORS}
    off = 0
    for n in CONV_TAPS:
        k, cols = w[n].shape
        rows = _lane_rows(w[n]).shape[0]
        blk = taps[:, off:off + rows, :].reshape(N_DEV, -1)[:, :k * cols].reshape(N_DEV, k, cols)
        conv[n] = blk.transpose(1, 0, 2).reshape(k, N_DEV * cols)
        off += rows
    sq, dx, large_grads, small_grads = _local_grads(
        x[0], mem[0], loss_target[0], {n: w[n] for n in GAINS}, conv, whole)

    packed = _pack_grads(large_grads, ranges)
    from_sibling = _sibling_exchange(packed, "reduce_pair")
    where = jnp.stack([pc, 2 * px + py]).astype(jnp.int32)
    chip_sums = _pair_sum(where, packed, from_sibling, ADAM_ROWS_PER_STEP, "pair_sum")
    from_chips = _chip_exchange(chip_sums, "reduce_chips")
    g_l, d_l, nm_l, nv_l = _adamw([(chip_sums, 1), (from_chips, 3)], _pack_shards(w), _pack_shards(m), _pack_shards(v),
                                  ADAM_ROWS_PER_STEP, "adamw_large")
    grad, delta, new_m, new_v = (_unpack_shards(p, ranges, w) for p in (g_l, d_l, nm_l, nv_l))

    small_grads = dict(small_grads, loss=sq)
    names = SMALL + ("loss",)
    partial = _pack_small(small_grads, names)
    total = _unpack_small(_sum_parts(_all_gather(partial, "gather_small"), "sum_small"), small_grads, names)
    loss = jnp.sum(total.pop("loss")) * (0.5 / D_MODEL)
    for n in CONV_TAPS:
        total[n] = lax.dynamic_slice_in_dim(total[n], device * CONV_COLS_PER_DEVICE, CONV_COLS_PER_DEVICE, axis=1)
    total = {n: total[n].reshape(w[n].shape) for n in SMALL}
    packed_small = [_pack_small(values, SMALL) for values in (total, w, m, v)]
    g_s, d_s, nm_s, nv_s = _adamw([(packed_small[0][None], 1)], *packed_small[1:], packed_small[0].shape[0], "adamw_small")
    for out, p in ((grad, g_s), (delta, d_s), (new_m, nm_s), (new_v, nv_s)):
        out.update(_unpack_small(p, w, SMALL))

    return (loss, dx[None], *[grad[n] for n in WEIGHTS], *[delta[n] for n in WEIGHTS], *[new_m[n] for n in WEIGHTS],
            *[new_v[n] for n in WEIGHTS])
```

```python
import functools

import jax
import jax.numpy as jnp
from jax import lax
from jax.experimental import pallas as pl
from jax.experimental.pallas import tpu as pltpu

F32 = jnp.float32
BF16 = jnp.bfloat16

D_MODEL = 1024
D_A = 512
D_B = 512
D_IN_ALL = 2560
CONV_A_W = 3
CONV_B_W = 31
XA_HEADS = 4
XA_HEAD_DIM = 256
D_FF = 2816
N_DEV = 8
RMS_EPS = 1e-6
LN_EPS = 1e-5
ADAM_LR = 0.001
ADAM_B1 = 0.9
ADAM_B2 = 0.999
ADAM_EPS = 1e-08
ADAM_WD = 0.01
ADAM_STEP = 10

VMEM_LIMIT_BYTES = 56 * 1024 * 1024
SUBLANES = 8
LANES = 128
HALO_B = 32
HALO_A = 8
CONV_CHUNK = 32

MESH = pl.DeviceIdType.MESH


def _params(n_grid_axes=1):
    return pltpu.CompilerParams(dimension_semantics=("arbitrary",) * n_grid_axes, vmem_limit_bytes=VMEM_LIMIT_BYTES)


def _sds(shape, dtype):
    return jax.ShapeDtypeStruct(shape, dtype)


def _tile(rows, cols):
    return pl.BlockSpec((rows, cols), lambda i: (i, 0))


def _rtile(rows, cols, n):
    return pl.BlockSpec((rows, cols), lambda i: (n - 1 - i, 0))


def _whole(shape):
    zeros = (0,) * len(shape)
    return pl.BlockSpec(shape, lambda i: zeros)


def _resident(shape):
    zeros = (0,) * len(shape)
    return pl.BlockSpec(shape, lambda i: zeros, pipeline_mode=pl.Buffered(1))


def _dot(a, b):
    return jnp.dot(a, b, preferred_element_type=F32)


def _dot_nt(a, b):
    return lax.dot_general(a, b, (((1,), (1,)), ((), ())), preferred_element_type=F32)


def _dot_tn(a, b):
    return lax.dot_general(a, b, (((0,), (0,)), ((), ())), preferred_element_type=F32)


def _sigmoid(x):
    return 1.0 / (1.0 + jnp.exp(-x))


def _rms_fwd(x, g):
    r = lax.rsqrt(jnp.mean(x * x, axis=-1, keepdims=True) + RMS_EPS)
    return x * r * g


def _rms_bwd(dy, xin, g):
    r = lax.rsqrt(jnp.mean(xin * xin, axis=-1, keepdims=True) + RMS_EPS)
    n = xin * r
    dg = jnp.sum(dy * n, axis=0, keepdims=True)
    dn = dy * g
    dx = r * (dn - n * jnp.mean(dn * n, axis=-1, keepdims=True))
    return dx, dg


def _zero_at_first_step(*refs):
    @pl.when(pl.program_id(0) == 0)
    def _():
        for ref in refs:
            ref[...] = jnp.zeros(ref.shape, ref.dtype)


def _place():
    return lax.axis_index("x"), lax.axis_index("y"), lax.axis_index("c")


def _device_index():
    x, y, c = _place()
    return 4 * x + 2 * y + c


class _Gather:
    def __init__(self, arrays):
        self.arrays = list(arrays)
        self.out_shape = [_sds((N_DEV, *a.shape), a.dtype) for a in self.arrays]
        n = len(self.arrays)
        self.scratch_shapes = [pltpu.SemaphoreType.DMA((n, 7)), pltpu.SemaphoreType.DMA((n, 7)),
                               pltpu.SemaphoreType.DMA((n,))]

    def forward_step(self, n_steps):
        return max(0, n_steps - 9)

    def bind(self, srcs, dsts, send_sems, recv_sems, local_sems):
        x, y, cc = _place()
        me, sibling = (x, y, cc), (x, y, 1 - cc)
        chips = [(1 - x, y), (x, 1 - y), (1 - x, 1 - y)]

        def copy(a, k, owner, to, src=None):
            slot = dsts[a].at[4 * owner[0] + 2 * owner[1] + owner[2]]
            return pltpu.make_async_remote_copy(
                src_ref=slot if src is None else src, dst_ref=slot, send_sem=send_sems.at[a, k],
                recv_sem=recv_sems.at[a, k], device_id=to, device_id_type=MESH)

        def first(a):
            return [copy(a, 0, me, sibling, src=srcs[a])] + [
                copy(a, 1 + j, me, (*chip, cc), src=srcs[a]) for j, chip in enumerate(chips)]

        def passed(a, j):
            return copy(a, 4 + j, (*chips[j], cc), sibling)

        def mine(a):
            return pltpu.make_async_copy(srcs[a], dsts[a].at[4 * x + 2 * y + cc], local_sems.at[a])

        def start():
            for a in range(len(srcs)):
                mine(a).start()
                for cp in first(a):
                    cp.start()

        def forward():
            for a in range(len(srcs)):
                for j, chip in enumerate(chips):
                    copy(a, 1 + j, (*chip, cc), me).wait_recv()
                    passed(a, j).start()

        def finish():
            for a in range(len(srcs)):
                copy(a, 0, sibling, me).wait_recv()
                for j, chip in enumerate(chips):
                    copy(a, 4 + j, (*chip, 1 - cc), me).wait_recv()
                for cp in first(a) + [passed(a, j) for j in range(len(chips))]:
                    cp.wait_send()
                mine(a).wait()

        return start, forward, finish


class _Exchange:
    def __init__(self, arrays, scatter):
        self.arrays = list(arrays)
        self.scatter = list(scatter)
        self.out_shape = [_sds(a.shape if s else (N_DEV, *a.shape), a.dtype) for a, s in zip(self.arrays, self.scatter)]
        n = len(self.arrays)
        self.scratch_shapes = [pltpu.SemaphoreType.DMA((n, 7)), pltpu.SemaphoreType.DMA((n, 7)),
                               pltpu.SemaphoreType.DMA((n,))]

    def forward_step(self, n_steps):
        return n_steps - 1

    def bind(self, srcs, dsts, send_sems, recv_sems, local_sems):
        me = _device_index()

        def copies(a):
            out = []
            for k in range(1, N_DEV):
                p = me ^ k
                out.append(pltpu.make_async_remote_copy(
                    src_ref=srcs[a].at[p] if self.scatter[a] else srcs[a], dst_ref=dsts[a].at[me],
                    send_sem=send_sems.at[a, k - 1], recv_sem=recv_sems.at[a, k - 1],
                    device_id=(p >> 2, (p >> 1) & 1, p & 1), device_id_type=MESH))
            return out

        def mine(a):
            return pltpu.make_async_copy(srcs[a].at[me] if self.scatter[a] else srcs[a], dsts[a].at[me], local_sems.at[a])

        def start():
            for a in range(len(srcs)):
                mine(a).start()
                for cp in copies(a):
                    cp.start()

        def forward():
            pass

        def finish():
            for a in range(len(srcs)):
                for cp in copies(a):
                    cp.wait()
                mine(a).wait()

        return start, forward, finish


def _hosted_call(core, comm, name, n_steps, in_specs, out_specs, out_shape, scratch_shapes, operands):
    n_in, n_out, n_scr, n_arr = len(in_specs), len(out_specs), len(scratch_shapes), len(comm.arrays)
    any_spec = pl.BlockSpec(memory_space=pl.ANY)

    def body(*refs):
        ins, refs = refs[:n_in], refs[n_in:]
        srcs, refs = refs[:n_arr], refs[n_arr:]
        outs, refs = refs[:n_out], refs[n_out:]
        dsts, refs = refs[:n_arr], refs[n_arr:]
        scratch, sems = refs[:n_scr], refs[n_scr:]
        start, forward, finish = comm.bind(srcs, dsts, *sems)
        step = pl.program_id(0)
        pl.when(step == 0)(start)
        core(*ins, *outs, *scratch)
        pl.when(step == comm.forward_step(n_steps))(forward)
        pl.when(step == n_steps - 1)(finish)

    results = pl.pallas_call(
        body, name=name, grid=(n_steps,),
        in_specs=list(in_specs) + [any_spec] * n_arr,
        out_specs=list(out_specs) + [any_spec] * n_arr,
        out_shape=list(out_shape) + comm.out_shape,
        scratch_shapes=list(scratch_shapes) + comm.scratch_shapes,
        compiler_params=_params(),
    )(*operands, *comm.arrays)
    return results[:n_out], results[n_out:]


def _comm_call(comm, name):
    return _hosted_call(lambda: None, comm, name, 1, [], [], [], [], [])[1]


def _norm_matmul(x, g, w, tm, name, comm):
    t, d = x.shape
    n = w.shape[1]

    def core(x_ref, g_ref, w_ref, h_ref, o_ref):
        h = _rms_fwd(x_ref[...], g_ref[...]).astype(BF16)
        h_ref[...] = h
        o_ref[...] = _dot(h, w_ref[...]).astype(BF16)

    return _hosted_call(
        core, comm, name, t // tm,
        in_specs=[_tile(tm, d), _whole((1, d)), _resident((d, n))],
        out_specs=[_tile(tm, d), _tile(tm, n)],
        out_shape=[_sds((t, d), BF16), _sds((t, n), BF16)],
        scratch_shapes=[], operands=(x, g, w))


def _conv_fwd(u, wa, wb, bb, lg, lb, tc, name, comm):
    t = u.shape[0]
    n_chunks = tc // CONV_CHUNK

    def core(u_ref, wa_ref, wb_ref, bb_ref, lg_ref, lb_ref, y_ref, zc_ref, ca_ref, zbuf, cvbuf, zcbuf):
        @pl.when(pl.program_id(0) == 0)
        def _():
            zbuf[:, 0:HALO_B, :] = jnp.zeros((D_B // LANES, HALO_B, LANES), F32)
            cvbuf[:, 0:HALO_A, :] = jnp.zeros((D_A // LANES, HALO_A, LANES), F32)

        for lb_i in range(D_A // LANES):
            lanes = slice(lb_i * LANES, (lb_i + 1) * LANES)
            c_a = u_ref[:, D_A + lb_i * LANES:D_A + (lb_i + 1) * LANES].astype(F32)
            v_a = u_ref[:, 2 * D_A + lb_i * LANES:2 * D_A + (lb_i + 1) * LANES].astype(F32)
            cvbuf[lb_i, HALO_A:HALO_A + tc, :] = c_a * v_a
            glu_v = u_ref[:, 3 * D_A + lb_i * LANES:3 * D_A + (lb_i + 1) * LANES].astype(F32)
            glu_g = u_ref[:, 3 * D_A + D_B + lb_i * LANES:3 * D_A + D_B + (lb_i + 1) * LANES].astype(F32)
            zbuf[lb_i, HALO_B:HALO_B + tc, :] = glu_v * _sigmoid(glu_g)

            for c in range(n_chunks):
                r0 = c * CONV_CHUNK
                rows = slice(r0, r0 + CONV_CHUNK)
                acc = jnp.zeros((CONV_CHUNK, LANES), F32)
                for k in range(CONV_A_W):
                    off = r0 + HALO_A - (CONV_A_W - 1) + k
                    acc = acc + wa_ref[k:k + 1, lanes] * cvbuf[lb_i, off:off + CONV_CHUNK, :]
                ca_ref[rows, lanes] = acc.astype(BF16)
                y_ref[rows, lanes] = (u_ref[rows, lanes].astype(F32) * acc).astype(BF16)

                accb = jnp.zeros((CONV_CHUNK, LANES), F32)
                for k in range(CONV_B_W):
                    off = r0 + HALO_B - (CONV_B_W - 1) + k
                    accb = accb + wb_ref[k:k + 1, lanes] * zbuf[lb_i, off:off + CONV_CHUNK, :]
                zcbuf[rows, lanes] = accb + bb_ref[:, lanes]

            zbuf[lb_i, 0:HALO_B, :] = zbuf[lb_i, tc:tc + HALO_B, :]
            cvbuf[lb_i, 0:HALO_A, :] = cvbuf[lb_i, tc:tc + HALO_A, :]

        for c in range(n_chunks):
            rows = slice(c * CONV_CHUNK, (c + 1) * CONV_CHUNK)
            zc = zcbuf[rows, :]
            zc_ref[rows, :] = zc.astype(BF16)
            mu = jnp.mean(zc, axis=-1, keepdims=True)
            xc = zc - mu
            var = jnp.mean(xc * xc, axis=-1, keepdims=True)
            ln = xc * lax.rsqrt(var + LN_EPS) * lg_ref[...] + lb_ref[...]
            y_ref[rows, D_A:D_A + D_B] = (ln * _sigmoid(ln)).astype(BF16)

    return _hosted_call(
        core, comm, name, t // tc,
        in_specs=[_tile(tc, D_IN_ALL), _whole((CONV_A_W, D_A)), _whole((CONV_B_W, D_B)), _whole((1, D_B)),
                  _whole((1, D_B)), _whole((1, D_B))],
        out_specs=[_tile(tc, D_A + D_B), _tile(tc, D_B), _tile(tc, D_A)],
        out_shape=[_sds((t, D_A + D_B), BF16), _sds((t, D_B), BF16), _sds((t, D_A), BF16)],
        scratch_shapes=[pltpu.VMEM((D_B // LANES, HALO_B + tc, LANES), F32),
                        pltpu.VMEM((D_A // LANES, HALO_A + tc, LANES), F32), pltpu.VMEM((tc, D_B), F32)],
        operands=(u, wa, wb, bb, lg, lb))


def _proj_norm_res(a, w, xres, g, tm, name):
    t, k = a.shape
    d = w.shape[1]

    def body(a_ref, w_ref, x_ref, g_ref, xo_ref, s_ref):
        s = _dot(a_ref[...], w_ref[...])
        s_ref[...] = s.astype(BF16)
        xo_ref[...] = x_ref[...] + _rms_fwd(s, g_ref[...])

    return pl.pallas_call(
        body, name=name, grid=(t // tm,),
        in_specs=[_tile(tm, k), _resident((k, d)), _tile(tm, d), _whole((1, d))],
        out_specs=[_tile(tm, d), _tile(tm, d)],
        out_shape=[_sds((t, d), F32), _sds((t, d), BF16)],
        compiler_params=_params(),
    )(a, w, xres, g)


def _mem_fwd(mem, g, wk, wv, name):
    m, d = mem.shape

    def body(mem_ref, g_ref, wk_ref, wv_ref, n_ref, k_ref, v_ref):
        n = _rms_fwd(mem_ref[...], g_ref[...]).astype(BF16)
        n_ref[...] = n
        k_ref[...] = _dot(n, wk_ref[...]).astype(BF16)
        v_ref[...] = _dot(n, wv_ref[...]).astype(BF16)

    return pl.pallas_call(
        body, name=name, grid=(1,),
        in_specs=[_whole((m, d)), _whole((1, d)), _whole((d, d)), _whole((d, d))],
        out_specs=[_whole((m, d))] * 3,
        out_shape=[_sds((m, d), BF16)] * 3,
        compiler_params=_params(),
    )(mem, g, wk, wv)


def _softmax_rows(s):
    e = jnp.exp(s - jnp.max(s, axis=-1, keepdims=True))
    return e / jnp.sum(e, axis=-1, keepdims=True)


def _attn_fwd(x1, g_pre, wq, k, v, wo, g_post, tm, name):
    t, d = x1.shape
    m = k.shape[0]
    scale = XA_HEAD_DIM ** -0.5

    def body(x_ref, gp_ref, wq_ref, k_ref, v_ref, wo_ref, go_ref, x2_ref, h_ref, q_ref, o_ref, a_ref):
        x = x_ref[...]
        h = _rms_fwd(x, gp_ref[...]).astype(BF16)
        h_ref[...] = h
        q_ref[...] = _dot(h, wq_ref[...]).astype(BF16)
        for hd in range(XA_HEADS):
            cols = slice(hd * XA_HEAD_DIM, (hd + 1) * XA_HEAD_DIM)
            p = _softmax_rows(_dot_nt(q_ref[:, cols], k_ref[:, cols]) * scale)
            o_ref[:, cols] = _dot(p.astype(BF16), v_ref[:, cols]).astype(BF16)
        a = _dot(o_ref[...], wo_ref[...])
        a_ref[...] = a.astype(BF16)
        x2_ref[...] = x + _rms_fwd(a, go_ref[...])

    return pl.pallas_call(
        body, name=name, grid=(t // tm,),
        in_specs=[_tile(tm, d), _whole((1, d)), _resident((d, d)), _whole((m, d)), _whole((m, d)), _resident((d, d)),
                  _whole((1, d))],
        out_specs=[_tile(tm, d)] * 5,
        out_shape=[_sds((t, d), F32)] + [_sds((t, d), BF16)] * 4,
        compiler_params=_params(),
    )(x1, g_pre, wq, k, v, wo, g_post)


def _ffn_fwd(x2, g_pre, wg, wu, wd, g_post, target, tm, name):
    t, d = x2.shape
    f = wg.shape[1]

    def body(x_ref, gp_ref, wg_ref, wu_ref, wd_ref, go_ref, tgt_ref, h_ref, gt_ref, up_ref, f_ref, sq_ref):
        _zero_at_first_step(sq_ref)
        x = x_ref[...]
        h = _rms_fwd(x, gp_ref[...]).astype(BF16)
        h_ref[...] = h
        gt = _dot(h, wg_ref[...])
        up = _dot(h, wu_ref[...])
        gt_ref[...] = gt.astype(BF16)
        up_ref[...] = up.astype(BF16)
        hd = (gt * _sigmoid(gt) * up).astype(BF16)
        ff = _dot(hd, wd_ref[...])
        f_ref[...] = ff.astype(BF16)
        err = x + _rms_fwd(ff, go_ref[...]) - tgt_ref[...]
        sq_ref[...] += jnp.sum(err * err, axis=0, keepdims=True)

    return pl.pallas_call(
        body, name=name, grid=(t // tm,),
        in_specs=[_tile(tm, d), _whole((1, d)), _resident((d, f)), _resident((d, f)), _resident((f, d)), _whole((1, d)),
                  _tile(tm, d)],
        out_specs=[_tile(tm, d), _tile(tm, f), _tile(tm, f), _tile(tm, d), _whole((1, d))],
        out_shape=[_sds((t, d), BF16), _sds((t, f), BF16), _sds((t, f), BF16), _sds((t, d), BF16), _sds((1, d), F32)],
        compiler_params=_params(),
    )(x2, g_pre, wg, wu, wd, g_post, target)


def _ffn_bwd(x2, f, target, gt, up, g_pre, wg, wu, wd, g_post, tm, name):
    t, d = x2.shape
    ff = wg.shape[1]

    def body(x_ref, f_ref, tgt_ref, gt_ref, up_ref, gp_ref, wg_ref, wu_ref, wd_ref, go_ref,
             dx_ref, df_ref, hd_ref, dgt_ref, dup_ref, dgo_ref, dgp_ref):
        _zero_at_first_step(dgo_ref, dgp_ref)
        x = x_ref[...]
        fo = f_ref[...].astype(F32)
        dx3 = (x + _rms_fwd(fo, go_ref[...]) - tgt_ref[...]) * (1.0 / d)
        df, dgo = _rms_bwd(dx3, fo, go_ref[...])
        dgo_ref[...] += dgo
        df = df.astype(BF16)
        df_ref[...] = df
        dhd = _dot_nt(df, wd_ref[...])
        gt = gt_ref[...].astype(F32)
        up = up_ref[...].astype(F32)
        sg = _sigmoid(gt)
        si = gt * sg
        hd_ref[...] = (si * up).astype(BF16)
        dup = (dhd * si).astype(BF16)
        dgt = (dhd * up * (sg * (1.0 + gt * (1.0 - sg)))).astype(BF16)
        dup_ref[...] = dup
        dgt_ref[...] = dgt
        dh = _dot_nt(dgt, wg_ref[...]) + _dot_nt(dup, wu_ref[...])
        dxn, dgp = _rms_bwd(dh, x, gp_ref[...])
        dgp_ref[...] += dgp
        dx_ref[...] = dx3 + dxn

    return pl.pallas_call(
        body, name=name, grid=(t // tm,),
        in_specs=[_tile(tm, d), _tile(tm, d), _tile(tm, d), _tile(tm, ff), _tile(tm, ff), _whole((1, d)),
                  _resident((d, ff)), _resident((d, ff)), _resident((ff, d)), _whole((1, d))],
        out_specs=[_tile(tm, d), _tile(tm, d), _tile(tm, ff), _tile(tm, ff), _tile(tm, ff), _whole((1, d)), _whole((1, d))],
        out_shape=[_sds((t, d), F32), _sds((t, d), BF16), _sds((t, ff), BF16), _sds((t, ff), BF16), _sds((t, ff), BF16),
                   _sds((1, d), F32), _sds((1, d), F32)],
        compiler_params=_params(),
    )(x2, f, target, gt, up, g_pre, wg, wu, wd, g_post)


def _attn_bwd(dx2, a, x1, q, k, v, g_pre, wq, wo, g_post, tm, name, comm):
    t, d = x1.shape
    m = k.shape[0]
    scale = XA_HEAD_DIM ** -0.5

    def core(dx2_ref, a_ref, x_ref, q_ref, k_ref, v_ref, gp_ref, wq_ref, wo_ref, go_ref,
             dx1_ref, da_ref, dq_ref, dk_ref, dv_ref, dgo_ref, dgp_ref, do_buf):
        _zero_at_first_step(dgo_ref, dgp_ref, dk_ref, dv_ref)
        dx2 = dx2_ref[...]
        da, dgo = _rms_bwd(dx2, a_ref[...].astype(F32), go_ref[...])
        dgo_ref[...] += dgo
        da = da.astype(BF16)
        da_ref[...] = da
        do_buf[...] = _dot_nt(da, wo_ref[...]).astype(BF16)
        for hd in range(XA_HEADS):
            cols = slice(hd * XA_HEAD_DIM, (hd + 1) * XA_HEAD_DIM)
            qh = q_ref[:, cols]
            p = _softmax_rows(_dot_nt(qh, k_ref[:, cols]) * scale)
            do_h = do_buf[:, cols]
            dp = _dot_nt(do_h, v_ref[:, cols])
            dv_ref[:, cols] += _dot_tn(p.astype(BF16), do_h)
            ds = (p * (dp - jnp.sum(dp * p, axis=-1, keepdims=True)) * scale).astype(BF16)
            dq_ref[:, cols] = _dot(ds, k_ref[:, cols]).astype(BF16)
            dk_ref[:, cols] += _dot_tn(ds, qh)
        dh = _dot_nt(dq_ref[...], wq_ref[...])
        dxn, dgp = _rms_bwd(dh, x_ref[...], gp_ref[...])
        dgp_ref[...] += dgp
        dx1_ref[...] = dx2 + dxn

    return _hosted_call(
        core, comm, name, t // tm,
        in_specs=[_tile(tm, d), _tile(tm, d), _tile(tm, d), _tile(tm, d), _whole((m, d)), _whole((m, d)), _whole((1, d)),
                  _resident((d, d)), _resident((d, d)), _whole((1, d))],
        out_specs=[_tile(tm, d), _tile(tm, d), _tile(tm, d), _whole((m, d)), _whole((m, d)), _whole((1, d)), _whole((1, d))],
        out_shape=[_sds((t, d), F32), _sds((t, d), BF16), _sds((t, d), BF16), _sds((m, d), F32), _sds((m, d), F32),
                   _sds((1, d), F32), _sds((1, d), F32)],
        scratch_shapes=[pltpu.VMEM((tm, d), BF16)],
        operands=(dx2, a, x1, q, k, v, g_pre, wq, wo, g_post))


def _mem_bwd(mem, mem_n, dk, dv, g, wk, wv, name):
    m, d = mem.shape

    def body(mem_ref, n_ref, dk_ref, dv_ref, g_ref, wk_ref, wv_ref, dwk_ref, dwv_ref, dg_ref):
        dk = dk_ref[...].astype(BF16)
        dv = dv_ref[...].astype(BF16)
        n = n_ref[...]
        dwk_ref[...] = _dot_tn(n, dk).astype(BF16)
        dwv_ref[...] = _dot_tn(n, dv).astype(BF16)
        dn = _dot_nt(dk, wk_ref[...]) + _dot_nt(dv, wv_ref[...])
        _, dg = _rms_bwd(dn, mem_ref[...], g_ref[...])
        dg_ref[...] = dg

    return pl.pallas_call(
        body, name=name, grid=(1,),
        in_specs=[_whole((m, d)), _whole((m, d)), _whole((m, d)), _whole((m, d)), _whole((1, d)), _whole((d, d)),
                  _whole((d, d))],
        out_specs=[_whole((d, d)), _whole((d, d)), _whole((1, d))],
        out_shape=[_sds((d, d), BF16), _sds((d, d), BF16), _sds((1, d), F32)],
        compiler_params=_params(),
    )(mem, mem_n, dk, dv, g, wk, wv)


def _proj_bwd(dxo, s, w, g, tm, name):
    t, d = dxo.shape
    k = w.shape[0]

    def body(dx_ref, s_ref, w_ref, g_ref, ds_ref, da_ref, dg_ref):
        _zero_at_first_step(dg_ref)
        ds, dg = _rms_bwd(dx_ref[...], s_ref[...].astype(F32), g_ref[...])
        dg_ref[...] += dg
        ds = ds.astype(BF16)
        ds_ref[...] = ds
        da_ref[...] = _dot_nt(ds, w_ref[...]).astype(BF16)

    return pl.pallas_call(
        body, name=name, grid=(t // tm,),
        in_specs=[_tile(tm, d), _tile(tm, d), _resident((k, d)), _whole((1, d))],
        out_specs=[_tile(tm, d), _tile(tm, k), _whole((1, d))],
        out_shape=[_sds((t, d), BF16), _sds((t, k), BF16), _sds((1, d), F32)],
        compiler_params=_params(),
    )(dxo, s, w, g)


def _conv_bwd(dy, u, zc, ca, wa, wb, lg, lb, tc, name, comm):
    t = u.shape[0]
    n_tiles = t // tc
    n_chunks = tc // CONV_CHUNK

    def core(dy_ref, u_ref, zc_ref, ca_ref, wa_ref, wb_ref, lg_ref, lb_ref,
             du_ref, dwa_ref, dwb_ref, dbb_ref, dlg_ref, dlb_ref, ebuf, eabuf, zbuf, cvbuf, wacc, aacc, vacc):
        step = pl.program_id(0)

        @pl.when(step == 0)
        def _():
            ebuf[:, tc:tc + HALO_B, :] = jnp.zeros((D_B // LANES, HALO_B, LANES), F32)
            eabuf[:, tc:tc + HALO_A, :] = jnp.zeros((D_A // LANES, HALO_A, LANES), F32)
            wacc[...] = jnp.zeros_like(wacc)
            aacc[...] = jnp.zeros_like(aacc)
            vacc[...] = jnp.zeros_like(vacc)

        dbb = jnp.zeros((SUBLANES, D_B), F32)
        dlg = jnp.zeros((SUBLANES, D_B), F32)
        dlb = jnp.zeros((SUBLANES, D_B), F32)
        for c in range(n_chunks):
            rows = slice(c * CONV_CHUNK, (c + 1) * CONV_CHUNK)
            dy_a = dy_ref[rows, 0:D_A].astype(F32)
            b_a = u_ref[rows, 0:D_A].astype(F32)
            du_ref[rows, 0:D_A] = (dy_a * ca_ref[rows, :].astype(F32)).astype(BF16)
            dca = dy_a * b_a
            cv = u_ref[rows, D_A:2 * D_A].astype(F32) * u_ref[rows, 2 * D_A:3 * D_A].astype(F32)
            glu_g = u_ref[rows, 3 * D_A + D_B:3 * D_A + 2 * D_B].astype(F32)
            z = u_ref[rows, 3 * D_A:3 * D_A + D_B].astype(F32) * _sigmoid(glu_g)
            for lb_i in range(D_A // LANES):
                lanes = slice(lb_i * LANES, (lb_i + 1) * LANES)
                eabuf[lb_i, rows, :] = dca[:, lanes]
                cvbuf[lb_i, rows, :] = cv[:, lanes]
                zbuf[lb_i, rows, :] = z[:, lanes]

            zcv = zc_ref[rows, :].astype(F32)
            mu = jnp.mean(zcv, axis=-1, keepdims=True)
            xc = zcv - mu
            rstd = lax.rsqrt(jnp.mean(xc * xc, axis=-1, keepdims=True) + LN_EPS)
            xhat = xc * rstd
            ln = xhat * lg_ref[...] + lb_ref[...]
            sg = _sigmoid(ln)
            dln = dy_ref[rows, D_A:D_A + D_B].astype(F32) * (sg * (1.0 + ln * (1.0 - sg)))
            dlg = dlg + jnp.sum((dln * xhat).reshape(CONV_CHUNK // SUBLANES, SUBLANES, D_B), axis=0)
            dlb = dlb + jnp.sum(dln.reshape(CONV_CHUNK // SUBLANES, SUBLANES, D_B), axis=0)
            dxh = dln * lg_ref[...]
            dzc = rstd * (dxh - jnp.mean(dxh, axis=-1, keepdims=True) - xhat * jnp.mean(dxh * xhat, axis=-1, keepdims=True))
            dbb = dbb + jnp.sum(dzc.reshape(CONV_CHUNK // SUBLANES, SUBLANES, D_B), axis=0)
            for lb_i in range(D_B // LANES):
                ebuf[lb_i, rows, :] = dzc[:, lb_i * LANES:(lb_i + 1) * LANES]
        vacc[0] += dbb
        vacc[1] += dlg
        vacc[2] += dlb

        for lb_i in range(D_A // LANES):
            lanes = slice(lb_i * LANES, (lb_i + 1) * LANES)

            def cols(first):
                return slice(first + lb_i * LANES, first + (lb_i + 1) * LANES)

            for c in range(n_chunks):
                r0 = c * CONV_CHUNK
                rows = slice(r0, r0 + CONV_CHUNK)
                cv = cvbuf[lb_i, rows, :]
                dcv = jnp.zeros((CONV_CHUNK, LANES), F32)
                for k in range(CONV_A_W):
                    off = r0 + (CONV_A_W - 1) - k
                    e = eabuf[lb_i, off:off + CONV_CHUNK, :]
                    dcv = dcv + wa_ref[k:k + 1, lanes] * e
                    aacc[k, :, lanes] += jnp.sum((cv * e).reshape(CONV_CHUNK // SUBLANES, SUBLANES, LANES), axis=0)
                du_ref[rows, cols(D_A)] = (dcv * u_ref[rows, cols(2 * D_A)].astype(F32)).astype(BF16)
                du_ref[rows, cols(2 * D_A)] = (dcv * u_ref[rows, cols(D_A)].astype(F32)).astype(BF16)

                z = zbuf[lb_i, rows, :]
                dz = jnp.zeros((CONV_CHUNK, LANES), F32)
                for k in range(CONV_B_W):
                    off = r0 + (CONV_B_W - 1) - k
                    e = ebuf[lb_i, off:off + CONV_CHUNK, :]
                    dz = dz + wb_ref[k:k + 1, lanes] * e
                    wacc[k, :, lanes] += jnp.sum((z * e).reshape(CONV_CHUNK // SUBLANES, SUBLANES, LANES), axis=0)
                glu_v = u_ref[rows, cols(3 * D_A)].astype(F32)
                sgg = _sigmoid(u_ref[rows, cols(3 * D_A + D_B)].astype(F32))
                du_ref[rows, cols(3 * D_A)] = (dz * sgg).astype(BF16)
                du_ref[rows, cols(3 * D_A + D_B)] = (dz * glu_v * sgg * (1.0 - sgg)).astype(BF16)

            ebuf[lb_i, tc:tc + HALO_B, :] = ebuf[lb_i, 0:HALO_B, :]
            eabuf[lb_i, tc:tc + HALO_A, :] = eabuf[lb_i, 0:HALO_A, :]

        @pl.when(step == n_tiles - 1)
        def _():
            for k in range(CONV_B_W):
                dwb_ref[k:k + 1, :] = jnp.sum(wacc[k], axis=0, keepdims=True)
            for k in range(CONV_A_W):
                dwa_ref[k:k + 1, :] = jnp.sum(aacc[k], axis=0, keepdims=True)
            dbb_ref[...] = jnp.sum(vacc[0], axis=0, keepdims=True)
            dlg_ref[...] = jnp.sum(vacc[1], axis=0, keepdims=True)
            dlb_ref[...] = jnp.sum(vacc[2], axis=0, keepdims=True)

    return _hosted_call(
        core, comm, name, n_tiles,
        in_specs=[_rtile(tc, D_A + D_B, n_tiles), _rtile(tc, D_IN_ALL, n_tiles), _rtile(tc, D_B, n_tiles),
                  _rtile(tc, D_A, n_tiles), _whole((CONV_A_W, D_A)), _whole((CONV_B_W, D_B)), _whole((1, D_B)),
                  _whole((1, D_B))],
        out_specs=[_rtile(tc, D_IN_ALL, n_tiles), _whole((CONV_A_W, D_A)), _whole((CONV_B_W, D_B)), _whole((1, D_B)),
                   _whole((1, D_B)), _whole((1, D_B))],
        out_shape=[_sds((t, D_IN_ALL), BF16), _sds((CONV_A_W, D_A), F32), _sds((CONV_B_W, D_B), F32), _sds((1, D_B), F32),
                   _sds((1, D_B), F32), _sds((1, D_B), F32)],
        scratch_shapes=[pltpu.VMEM((D_B // LANES, tc + HALO_B, LANES), F32), pltpu.VMEM((D_A // LANES, tc + HALO_A, LANES), F32),
                        pltpu.VMEM((D_B // LANES, tc, LANES), F32), pltpu.VMEM((D_A // LANES, tc, LANES), F32),
                        pltpu.VMEM((CONV_B_W, SUBLANES, D_B), F32), pltpu.VMEM((CONV_A_W, SUBLANES, D_A), F32),
                        pltpu.VMEM((3, SUBLANES, D_B), F32)],
        operands=(dy, u, zc, ca, wa, wb, lg, lb))


def _in_bwd(du, w, x, dx1, g, tm, name):
    t, d = x.shape
    n = w.shape[1]

    def body(du_ref, w_ref, x_ref, dx1_ref, g_ref, dx_ref, dg_ref):
        _zero_at_first_step(dg_ref)
        dh = _dot_nt(du_ref[...], w_ref[...])
        dxn, dg = _rms_bwd(dh, x_ref[...], g_ref[...])
        dg_ref[...] += dg
        dx_ref[...] = dx1_ref[...] + dxn

    return pl.pallas_call(
        body, name=name, grid=(t // tm,),
        in_specs=[_tile(tm, n), _resident((d, n)), _tile(tm, d), _tile(tm, d), _whole((1, d))],
        out_specs=[_tile(tm, d), _whole((1, d))],
        out_shape=[_sds((t, d), F32), _sds((1, d), F32)],
        compiler_params=_params(),
    )(du, w, x, dx1, g)


def _wgrad(a, b, tk, bm, bn, name):
    t, m = a.shape
    n = b.shape[1]
    n_k = t // tk

    def body(a_ref, b_ref, o_ref, acc):
        @pl.when(pl.program_id(2) == 0)
        def _():
            acc[...] = jnp.zeros_like(acc)

        acc[...] += _dot_tn(a_ref[...], b_ref[...])

        @pl.when(pl.program_id(2) == n_k - 1)
        def _():
            o_ref[...] = acc[...].astype(BF16)

    return pl.pallas_call(
        body, name=name, grid=(m // bm, n // bn, n_k),
        in_specs=[pl.BlockSpec((tk, bm), lambda i, j, k: (k, i)), pl.BlockSpec((tk, bn), lambda i, j, k: (k, j))],
        out_specs=pl.BlockSpec((bm, bn), lambda i, j, k: (i, j)),
        out_shape=_sds((m, n), BF16),
        scratch_shapes=[pltpu.VMEM((bm, bn), F32)],
        compiler_params=_params(3),
    )(a, b)


def _adamw(parts, w, m, v, tr, name):
    n_parts, r, c = parts.shape

    def body(p_ref, w_ref, m_ref, v_ref, g_ref, d_ref, nm_ref, nv_ref):
        g = p_ref[0].astype(F32)
        for j in range(1, n_parts):
            g = g + p_ref[j].astype(F32)
        g_ref[...] = g
        nm = ADAM_B1 * m_ref[...] + (1.0 - ADAM_B1) * g
        nv = ADAM_B2 * v_ref[...] + (1.0 - ADAM_B2) * (g * g)
        nm_ref[...] = nm
        nv_ref[...] = nv
        m_hat = nm / (1.0 - ADAM_B1 ** ADAM_STEP)
        v_hat = nv / (1.0 - ADAM_B2 ** ADAM_STEP)
        d_ref[...] = -ADAM_LR * (m_hat / (jnp.sqrt(v_hat) + ADAM_EPS) + ADAM_WD * w_ref[...])

    return pl.pallas_call(
        body, name=name, grid=(r // tr,),
        in_specs=[pl.BlockSpec((n_parts, tr, c), lambda i: (0, i, 0))] + [_tile(tr, c)] * 3,
        out_specs=[_tile(tr, c)] * 4,
        out_shape=[_sds((r, c), F32)] * 4,
        compiler_params=_params(),
    )(parts, w, m, v)


def _sum_parts(parts, name):
    n_parts, r, c = parts.shape

    def body(p_ref, o_ref):
        acc = p_ref[0]
        for j in range(1, n_parts):
            acc = acc + p_ref[j]
        o_ref[...] = acc

    return pl.pallas_call(
        body, name=name, grid=(1,),
        in_specs=[_whole((n_parts, r, c))], out_specs=_whole((r, c)), out_shape=_sds((r, c), F32),
        compiler_params=_params(),
    )(parts)


def _row(v):
    return v.reshape(1, -1)


def _by_owner_columns(g):
    rows, cols = g.shape
    return g.reshape(rows, N_DEV, cols // N_DEV).transpose(1, 0, 2)


def _from_owner_columns(g):
    n_dev, rows, cols = g.shape
    return g.transpose(1, 0, 2).reshape(rows, n_dev * cols)


def _by_owner_rows(g):
    return g.reshape(N_DEV, g.shape[0] // N_DEV, g.shape[1])


WEIGHTS = ("mix_pre_g", "w_mix_in", "conv_a_w", "conv_b_w", "conv_b_b", "ln_b_g", "ln_b_b", "w_mix_out", "mix_post_g",
           "xa_pre_g", "mem_norm_g", "w_q", "w_k", "w_v", "w_o", "xa_post_g", "ffn_pre_g", "w_gate", "w_up", "w_down",
           "ffn_post_g")
LARGE = ("w_mix_in", "w_mix_out", "w_q", "w_k", "w_v", "w_o", "w_gate", "w_up", "w_down")
GAINS = ("mix_pre_g", "mix_post_g", "xa_pre_g", "mem_norm_g", "xa_post_g", "ffn_pre_g", "ffn_post_g")
CHANNEL_VECTORS = ("conv_b_b", "ln_b_g", "ln_b_b")
CONV_TAPS = ("conv_a_w", "conv_b_w")
SMALL = GAINS + CHANNEL_VECTORS + CONV_TAPS
CONV_COLS_PER_DEVICE = D_A // N_DEV
TOKEN_TILE = 512
FFN_TOKEN_TILE = 256
CONV_TOKEN_TILE = 256
WGRAD_TOKEN_TILE = 2048
ADAM_ROWS_PER_STEP = 256


def _lane_rows(v):
    flat = v.reshape(-1)
    tile = SUBLANES * LANES
    flat = jnp.pad(flat, (0, (-flat.shape[0]) % tile))
    return flat.reshape(-1, LANES)


def _pack_small(values, names):
    return jnp.concatenate([_lane_rows(values[n]) for n in names], axis=0)


def _unpack_small(packed, like, names):
    out, off = {}, 0
    for n in names:
        size = like[n].size
        rows = _lane_rows(like[n]).shape[0]
        out[n] = packed[off:off + rows, :].reshape(-1)[:size].reshape(like[n].shape)
        off += rows
    return out


def kernel(x, mem, mix_pre_g, w_mix_in, conv_a_w, conv_b_w, conv_b_b, ln_b_g, ln_b_b, w_mix_out, mix_post_g, xa_pre_g, mem_norm_g, w_q, w_k, w_v, w_o, xa_post_g, ffn_pre_g, w_gate, w_up, w_down, ffn_post_g, loss_target, m_mix_pre_g, m_w_mix_in, m_conv_a_w, m_conv_b_w, m_conv_b_b, m_ln_b_g, m_ln_b_b, m_w_mix_out, m_mix_post_g, m_xa_pre_g, m_mem_norm_g, m_w_q, m_w_k, m_w_v, m_w_o, m_xa_post_g, m_ffn_pre_g, m_w_gate, m_w_up, m_w_down, m_ffn_post_g, v_mix_pre_g, v_w_mix_in, v_conv_a_w, v_conv_b_w, v_conv_b_b, v_ln_b_g, v_ln_b_b, v_w_mix_out, v_mix_post_g, v_xa_pre_g, v_mem_norm_g, v_w_q, v_w_k, v_w_v, v_w_o, v_xa_post_g, v_ffn_pre_g, v_w_gate, v_w_up, v_w_down, v_ffn_post_g):
    given = dict(locals())
    w = {n: given[n] for n in WEIGHTS}
    m = {n: given["m_" + n] for n in WEIGHTS}
    v = {n: given["v_" + n] for n in WEIGHTS}
    xs, mems, target = x[0], mem[0], loss_target[0]
    t = xs.shape[0]
    tm, tm_ffn, tc, tk = min(TOKEN_TILE, t), min(FFN_TOKEN_TILE, t), min(CONV_TOKEN_TILE, t), min(WGRAD_TOKEN_TILE, t)
    g = {n: _row(w[n]) for n in GAINS}
    bb, lg, lb = (_row(w[n]) for n in CHANNEL_VECTORS)

    def shard_bf16(*names):
        return [w[n].astype(BF16) for n in names]

    g_mix_in, g_mix_out, g_taps = _comm_call(
        _Gather(shard_bf16("w_mix_in", "w_mix_out") + [_pack_small(w, CONV_TAPS)]), "gather_mixer")
    w_mix_in, w_mix_out = _from_owner_columns(g_mix_in), g_mix_out.reshape(D_MODEL, D_MODEL)
    taps, off = {}, 0
    for n in CONV_TAPS:
        k, cols = w[n].shape
        rows = _lane_rows(w[n]).shape[0]
        blk = g_taps[:, off:off + rows, :].reshape(N_DEV, -1)[:, :k * cols].reshape(N_DEV, k, cols)
        taps[n] = blk.transpose(1, 0, 2).reshape(k, N_DEV * cols)
        off += rows
    wa, wb = taps["conv_a_w"], taps["conv_b_w"]

    (h1, u), gathered = _norm_matmul(xs, g["mix_pre_g"], w_mix_in, tm, "mix_in_fwd",
                                     _Gather(shard_bf16("w_q", "w_k", "w_v", "w_o")))
    w_q, w_k, w_v, w_o = (a.reshape(D_MODEL, D_MODEL) for a in gathered)
    (ycat, zc, ca), (g_gate, g_up, g_down) = _conv_fwd(u, wa, wb, bb, lg, lb, tc, "conv_fwd",
                                                       _Gather(shard_bf16("w_gate", "w_up", "w_down")))
    w_gate, w_up, w_down = _from_owner_columns(g_gate), _from_owner_columns(g_up), g_down.reshape(D_FF, D_MODEL)
    mem_n, kk, vv = _mem_fwd(mems, g["mem_norm_g"], w_k, w_v, "mem_fwd")
    x1, s1 = _proj_norm_res(ycat, w_mix_out, xs, g["mix_post_g"], tm, "mix_out_fwd")
    x2, h2, q, o, a = _attn_fwd(x1, g["xa_pre_g"], w_q, kk, vv, w_o, g["xa_post_g"], tm, "attn_fwd")
    h3, gt, up, f, sq = _ffn_fwd(x2, g["ffn_pre_g"], w_gate, w_up, w_down, g["ffn_post_g"], target, tm_ffn, "ffn_fwd")

    dx2, df, hd, dgt, dup, d_ffn_post, d_ffn_pre = _ffn_bwd(
        x2, f, target, gt, up, g["ffn_pre_g"], w_gate, w_up, w_down, g["ffn_post_g"], tm_ffn, "ffn_bwd")
    d_w_down = _wgrad(hd, df, tk, D_FF // 2, D_MODEL, "wgrad_down")
    d_w_gate = _wgrad(h3, dgt, tk, D_MODEL, D_FF // 2, "wgrad_gate")
    d_w_up = _wgrad(h3, dup, tk, D_MODEL, D_FF // 2, "wgrad_up")
    ffn_slabs = [_by_owner_columns(d_w_gate), _by_owner_columns(d_w_up), _by_owner_rows(d_w_down)]

    (dx1, da, dq, dk, dv, d_xa_post, d_xa_pre), from_ffn = _attn_bwd(
        dx2, a, x1, q, kk, vv, g["xa_pre_g"], w_q, w_o, g["xa_post_g"], tm, "attn_bwd",
        _Exchange(ffn_slabs, [True] * 3))
    d_w_o = _wgrad(o, da, tk, D_MODEL, D_MODEL, "wgrad_o")
    d_w_q = _wgrad(h2, dq, tk, D_MODEL, D_MODEL, "wgrad_q")
    d_w_k, d_w_v, d_mem_norm = _mem_bwd(mems, mem_n, dk, dv, g["mem_norm_g"], w_k, w_v, "mem_bwd")
    ds1, dycat, d_mix_post = _proj_bwd(dx1, s1, w_mix_out, g["mix_post_g"], tm, "mix_out_bwd")
    d_w_mix_out = _wgrad(ycat, ds1, tk, D_MODEL, D_MODEL, "wgrad_mix_out")
    attn_slabs = [_by_owner_rows(d) for d in (d_w_mix_out, d_w_q, d_w_k, d_w_v, d_w_o)]

    (du, d_conv_a, d_conv_b, d_conv_bb, d_ln_g, d_ln_b), from_attn = _conv_bwd(
        dycat, u, zc, ca, wa, wb, lg, lb, tc, "conv_bwd", _Exchange(attn_slabs, [True] * 5))
    d_w_mix_in = _wgrad(h1, du, tk, D_MODEL, D_IN_ALL // 2, "wgrad_mix_in")
    dx, d_mix_pre = _in_bwd(du, w_mix_in, xs, dx1, g["mix_pre_g"], tm, "mix_in_bwd")

    small_grads = dict(mix_pre_g=d_mix_pre, conv_a_w=d_conv_a, conv_b_w=d_conv_b, conv_b_b=d_conv_bb, ln_b_g=d_ln_g,
                       ln_b_b=d_ln_b, mix_post_g=d_mix_post, xa_pre_g=d_xa_pre, mem_norm_g=d_mem_norm, xa_post_g=d_xa_post,
                       ffn_pre_g=d_ffn_pre, ffn_post_g=d_ffn_post, loss=sq)
    names = SMALL + ("loss",)
    from_mix_in, all_small = _comm_call(
        _Exchange([_by_owner_columns(d_w_mix_in), _pack_small(small_grads, names)], [True, False]), "reduce_tail")

    received = dict(zip(("w_gate", "w_up", "w_down"), from_ffn))
    received.update(zip(("w_mix_out", "w_q", "w_k", "w_v", "w_o"), from_attn))
    received["w_mix_in"] = from_mix_in
    grad, delta, new_m, new_v = {}, {}, {}, {}
    for n in LARGE:
        rows = w[n].shape[0]
        tr = ADAM_ROWS_PER_STEP if rows % ADAM_ROWS_PER_STEP == 0 else rows
        grad[n], delta[n], new_m[n], new_v[n] = _adamw(received[n], w[n], m[n], v[n], tr, "adamw_" + n)

    total = _unpack_small(_sum_parts(all_small, "sum_small"), small_grads, names)
    loss = jnp.sum(total.pop("loss")) * (0.5 / D_MODEL)
    first_col = _device_index() * CONV_COLS_PER_DEVICE
    for n in CONV_TAPS:
        total[n] = lax.dynamic_slice_in_dim(total[n], first_col, CONV_COLS_PER_DEVICE, axis=1)
    total = {n: total[n].reshape(w[n].shape) for n in SMALL}
    packed_small = [_pack_small(values, SMALL) for values in (total, w, m, v)]
    g_s, d_s, nm_s, nv_s = _adamw(packed_small[0][None], *packed_small[1:], packed_small[0].shape[0], "adamw_small")
    for out, p in ((grad, g_s), (delta, d_s), (new_m, nm_s), (new_v, nv_s)):
        out.update(_unpack_small(p, w, SMALL))

    return (loss, dx[None], *[grad[n] for n in WEIGHTS], *[delta[n] for n in WEIGHTS], *[new_m[n] for n in WEIGHTS],
            *[new_v[n] for n in WEIGHTS])
```

```python
import functools

import jax
import jax.numpy as jnp
from jax import lax
from jax.experimental import pallas as pl
from jax.experimental.pallas import tpu as pltpu

F32 = jnp.float32
BF16 = jnp.bfloat16

D_MODEL = 1024
D_A = 512
D_B = 512
D_IN_ALL = 2560
CONV_A_W = 3
CONV_B_W = 31
XA_HEADS = 4
XA_HEAD_DIM = 256
D_FF = 2816
N_DEV = 8
RMS_EPS = 1e-6
LN_EPS = 1e-5
ADAM_LR = 0.001
ADAM_B1 = 0.9
ADAM_B2 = 0.999
ADAM_EPS = 1e-08
ADAM_WD = 0.01
ADAM_STEP = 10

VMEM_LIMIT_BYTES = 56 * 1024 * 1024
SUBLANES = 8
LANES = 128
HALO_B = 32
HALO_A = 8
CONV_CHUNK = 32

MESH = pl.DeviceIdType.MESH


def _params(n_grid_axes=1):
    return pltpu.CompilerParams(dimension_semantics=("arbitrary",) * n_grid_axes, vmem_limit_bytes=VMEM_LIMIT_BYTES)


def _sds(shape, dtype):
    return jax.ShapeDtypeStruct(shape, dtype)


def _tile(rows, cols):
    return pl.BlockSpec((rows, cols), lambda i: (i, 0))


def _rtile(rows, cols, n):
    return pl.BlockSpec((rows, cols), lambda i: (n - 1 - i, 0))


def _whole(shape):
    zeros = (0,) * len(shape)
    return pl.BlockSpec(shape, lambda i: zeros)


def _resident(shape):
    zeros = (0,) * len(shape)
    return pl.BlockSpec(shape, lambda i: zeros, pipeline_mode=pl.Buffered(1))


def _dot(a, b):
    return jnp.dot(a, b, preferred_element_type=F32)


def _dot_nt(a, b):
    return lax.dot_general(a, b, (((1,), (1,)), ((), ())), preferred_element_type=F32)


def _dot_tn(a, b):
    return lax.dot_general(a, b, (((0,), (0,)), ((), ())), preferred_element_type=F32)


def _sigmoid(x):
    return 1.0 / (1.0 + jnp.exp(-x))


def _rms_fwd(x, g):
    r = lax.rsqrt(jnp.mean(x * x, axis=-1, keepdims=True) + RMS_EPS)
    return x * r * g


def _rms_bwd(dy, xin, g):
    r = lax.rsqrt(jnp.mean(xin * xin, axis=-1, keepdims=True) + RMS_EPS)
    n = xin * r
    dg = jnp.sum(dy * n, axis=0, keepdims=True)
    dn = dy * g
    dx = r * (dn - n * jnp.mean(dn * n, axis=-1, keepdims=True))
    return dx, dg


def _zero_at_first_step(*refs):
    @pl.when(pl.program_id(0) == 0)
    def _():
        for ref in refs:
            ref[...] = jnp.zeros(ref.shape, ref.dtype)


def _place():
    return lax.axis_index("x"), lax.axis_index("y"), lax.axis_index("c")


def _device_index():
    x, y, c = _place()
    return 4 * x + 2 * y + c


class _Gather:
    def __init__(self, arrays):
        self.arrays = list(arrays)
        self.out_shape = [_sds((N_DEV, *a.shape), a.dtype) for a in self.arrays]
        n = len(self.arrays)
        self.scratch_shapes = [pltpu.SemaphoreType.DMA((n, 7)), pltpu.SemaphoreType.DMA((n, 7)),
                               pltpu.SemaphoreType.DMA((n,))]

    def forward_step(self, n_steps):
        return max(0, n_steps - 9)

    def bind(self, srcs, dsts, send_sems, recv_sems, local_sems):
        x, y, cc = _place()
        me, sibling = (x, y, cc), (x, y, 1 - cc)
        chips = [(1 - x, y), (x, 1 - y), (1 - x, 1 - y)]

        def copy(a, k, owner, to, src=None):
            slot = dsts[a].at[4 * owner[0] + 2 * owner[1] + owner[2]]
            return pltpu.make_async_remote_copy(
                src_ref=slot if src is None else src, dst_ref=slot, send_sem=send_sems.at[a, k],
                recv_sem=recv_sems.at[a, k], device_id=to, device_id_type=MESH)

        def first(a):
            return [copy(a, 0, me, sibling, src=srcs[a])] + [
                copy(a, 1 + j, me, (*chip, cc), src=srcs[a]) for j, chip in enumerate(chips)]

        def passed(a, j):
            return copy(a, 4 + j, (*chips[j], cc), sibling)

        def mine(a):
            return pltpu.make_async_copy(srcs[a], dsts[a].at[4 * x + 2 * y + cc], local_sems.at[a])

        def start():
            for a in range(len(srcs)):
                mine(a).start()
                for cp in first(a):
                    cp.start()

        def forward():
            for a in range(len(srcs)):
                for j, chip in enumerate(chips):
                    copy(a, 1 + j, (*chip, cc), me).wait_recv()
                    passed(a, j).start()

        def finish():
            for a in range(len(srcs)):
                copy(a, 0, sibling, me).wait_recv()
                for j, chip in enumerate(chips):
                    copy(a, 4 + j, (*chip, 1 - cc), me).wait_recv()
                for cp in first(a) + [passed(a, j) for j in range(len(chips))]:
                    cp.wait_send()
                mine(a).wait()

        return start, forward, finish


class _Exchange:
    def __init__(self, arrays, scatter):
        self.arrays = list(arrays)
        self.scatter = list(scatter)
        self.out_shape = [_sds(a.shape if s else (N_DEV, *a.shape), a.dtype) for a, s in zip(self.arrays, self.scatter)]
        n = len(self.arrays)
        self.scratch_shapes = [pltpu.SemaphoreType.DMA((n, 7)), pltpu.SemaphoreType.DMA((n, 7)),
                               pltpu.SemaphoreType.DMA((n,))]

    def forward_step(self, n_steps):
        return n_steps - 1

    def bind(self, srcs, dsts, send_sems, recv_sems, local_sems):
        me = _device_index()

        def copies(a):
            out = []
            for k in range(1, N_DEV):
                p = me ^ k
                out.append(pltpu.make_async_remote_copy(
                    src_ref=srcs[a].at[p] if self.scatter[a] else srcs[a], dst_ref=dsts[a].at[me],
                    send_sem=send_sems.at[a, k - 1], recv_sem=recv_sems.at[a, k - 1],
                    device_id=(p >> 2, (p >> 1) & 1, p & 1), device_id_type=MESH))
            return out

        def mine(a):
            return pltpu.make_async_copy(srcs[a].at[me] if self.scatter[a] else srcs[a], dsts[a].at[me], local_sems.at[a])

        def start():
            for a in range(len(srcs)):
                mine(a).start()
                for cp in copies(a):
                    cp.start()

        def forward():
            pass

        def finish():
            for a in range(len(srcs)):
                for cp in copies(a):
                    cp.wait()
                mine(a).wait()

        return start, forward, finish


def _hosted_call(core, comm, name, grid, in_specs, out_specs, out_shape, scratch_shapes, operands):
    grid = (grid,) if isinstance(grid, int) else tuple(grid)
    n_steps = 1
    for extent in grid:
        n_steps *= extent
    n_in, n_out, n_scr, n_arr = len(in_specs), len(out_specs), len(scratch_shapes), len(comm.arrays)
    any_spec = pl.BlockSpec(memory_space=pl.ANY)

    def body(*refs):
        ins, refs = refs[:n_in], refs[n_in:]
        srcs, refs = refs[:n_arr], refs[n_arr:]
        outs, refs = refs[:n_out], refs[n_out:]
        dsts, refs = refs[:n_arr], refs[n_arr:]
        scratch, sems = refs[:n_scr], refs[n_scr:]
        start, forward, finish = comm.bind(srcs, dsts, *sems)
        step = pl.program_id(0)
        for axis in range(1, len(grid)):
            step = step * grid[axis] + pl.program_id(axis)
        pl.when(step == 0)(start)
        core(*ins, *outs, *scratch)
        pl.when(step == comm.forward_step(n_steps))(forward)
        pl.when(step == n_steps - 1)(finish)

    results = pl.pallas_call(
        body, name=name, grid=grid,
        in_specs=list(in_specs) + [any_spec] * n_arr,
        out_specs=list(out_specs) + [any_spec] * n_arr,
        out_shape=list(out_shape) + comm.out_shape,
        scratch_shapes=list(scratch_shapes) + comm.scratch_shapes,
        compiler_params=_params(len(grid)),
    )(*operands, *comm.arrays)
    return results[:n_out], results[n_out:]


def _comm_call(comm, name):
    return _hosted_call(lambda: None, comm, name, 1, [], [], [], [], [])[1]


def _norm_matmul(x, g, w, tm, name, comm):
    t, d = x.shape
    n = w.shape[1]

    def core(x_ref, g_ref, w_ref, h_ref, o_ref):
        h = _rms_fwd(x_ref[...], g_ref[...]).astype(BF16)
        h_ref[...] = h
        o_ref[...] = _dot(h, w_ref[...]).astype(BF16)

    return _hosted_call(
        core, comm, name, t // tm,
        in_specs=[_tile(tm, d), _whole((1, d)), _resident((d, n))],
        out_specs=[_tile(tm, d), _tile(tm, n)],
        out_shape=[_sds((t, d), BF16), _sds((t, n), BF16)],
        scratch_shapes=[], operands=(x, g, w))


def _conv_fwd(u, wa, wb, bb, lg, lb, tc, name, comm):
    t = u.shape[0]
    n_chunks = tc // CONV_CHUNK

    def core(u_ref, wa_ref, wb_ref, bb_ref, lg_ref, lb_ref, y_ref, zc_ref, ca_ref, zbuf, cvbuf, zcbuf):
        @pl.when(pl.program_id(0) == 0)
        def _():
            zbuf[:, 0:HALO_B, :] = jnp.zeros((D_B // LANES, HALO_B, LANES), F32)
            cvbuf[:, 0:HALO_A, :] = jnp.zeros((D_A // LANES, HALO_A, LANES), F32)

        for lb_i in range(D_A // LANES):
            lanes = slice(lb_i * LANES, (lb_i + 1) * LANES)
            c_a = u_ref[:, D_A + lb_i * LANES:D_A + (lb_i + 1) * LANES].astype(F32)
            v_a = u_ref[:, 2 * D_A + lb_i * LANES:2 * D_A + (lb_i + 1) * LANES].astype(F32)
            cvbuf[lb_i, HALO_A:HALO_A + tc, :] = c_a * v_a
            glu_v = u_ref[:, 3 * D_A + lb_i * LANES:3 * D_A + (lb_i + 1) * LANES].astype(F32)
            glu_g = u_ref[:, 3 * D_A + D_B + lb_i * LANES:3 * D_A + D_B + (lb_i + 1) * LANES].astype(F32)
            zbuf[lb_i, HALO_B:HALO_B + tc, :] = glu_v * _sigmoid(glu_g)

            for c in range(n_chunks):
                r0 = c * CONV_CHUNK
                rows = slice(r0, r0 + CONV_CHUNK)
                acc = jnp.zeros((CONV_CHUNK, LANES), F32)
                for k in range(CONV_A_W):
                    off = r0 + HALO_A - (CONV_A_W - 1) + k
                    acc = acc + wa_ref[k:k + 1, lanes] * cvbuf[lb_i, off:off + CONV_CHUNK, :]
                ca_ref[rows, lanes] = acc.astype(BF16)
                y_ref[rows, lanes] = (u_ref[rows, lanes].astype(F32) * acc).astype(BF16)

                accb = jnp.zeros((CONV_CHUNK, LANES), F32)
                for k in range(CONV_B_W):
                    off = r0 + HALO_B - (CONV_B_W - 1) + k
                    accb = accb + wb_ref[k:k + 1, lanes] * zbuf[lb_i, off:off + CONV_CHUNK, :]
                zcbuf[rows, lanes] = accb + bb_ref[:, lanes]

            zbuf[lb_i, 0:HALO_B, :] = zbuf[lb_i, tc:tc + HALO_B, :]
            cvbuf[lb_i, 0:HALO_A, :] = cvbuf[lb_i, tc:tc + HALO_A, :]

        for c in range(n_chunks):
            rows = slice(c * CONV_CHUNK, (c + 1) * CONV_CHUNK)
            zc = zcbuf[rows, :]
            zc_ref[rows, :] = zc.astype(BF16)
            mu = jnp.mean(zc, axis=-1, keepdims=True)
            xc = zc - mu
            var = jnp.mean(xc * xc, axis=-1, keepdims=True)
            ln = xc * lax.rsqrt(var + LN_EPS) * lg_ref[...] + lb_ref[...]
            y_ref[rows, D_A:D_A + D_B] = (ln * _sigmoid(ln)).astype(BF16)

    return _hosted_call(
        core, comm, name, t // tc,
        in_specs=[_tile(tc, D_IN_ALL), _whole((CONV_A_W, D_A)), _whole((CONV_B_W, D_B)), _whole((1, D_B)),
                  _whole((1, D_B)), _whole((1, D_B))],
        out_specs=[_tile(tc, D_A + D_B), _tile(tc, D_B), _tile(tc, D_A)],
        out_shape=[_sds((t, D_A + D_B), BF16), _sds((t, D_B), BF16), _sds((t, D_A), BF16)],
        scratch_shapes=[pltpu.VMEM((D_B // LANES, HALO_B + tc, LANES), F32),
                        pltpu.VMEM((D_A // LANES, HALO_A + tc, LANES), F32), pltpu.VMEM((tc, D_B), F32)],
        operands=(u, wa, wb, bb, lg, lb))


def _proj_norm_res(a, w, xres, g, tm, name):
    t, k = a.shape
    d = w.shape[1]

    def body(a_ref, w_ref, x_ref, g_ref, xo_ref, s_ref):
        s = _dot(a_ref[...], w_ref[...])
        s_ref[...] = s.astype(BF16)
        xo_ref[...] = x_ref[...] + _rms_fwd(s, g_ref[...])

    return pl.pallas_call(
        body, name=name, grid=(t // tm,),
        in_specs=[_tile(tm, k), _resident((k, d)), _tile(tm, d), _whole((1, d))],
        out_specs=[_tile(tm, d), _tile(tm, d)],
        out_shape=[_sds((t, d), F32), _sds((t, d), BF16)],
        compiler_params=_params(),
    )(a, w, xres, g)


def _mem_fwd(mem, g, wk, wv, name):
    m, d = mem.shape

    def body(mem_ref, g_ref, wk_ref, wv_ref, n_ref, k_ref, v_ref):
        n = _rms_fwd(mem_ref[...], g_ref[...]).astype(BF16)
        n_ref[...] = n
        k_ref[...] = _dot(n, wk_ref[...]).astype(BF16)
        v_ref[...] = _dot(n, wv_ref[...]).astype(BF16)

    return pl.pallas_call(
        body, name=name, grid=(1,),
        in_specs=[_whole((m, d)), _whole((1, d)), _whole((d, d)), _whole((d, d))],
        out_specs=[_whole((m, d))] * 3,
        out_shape=[_sds((m, d), BF16)] * 3,
        compiler_params=_params(),
    )(mem, g, wk, wv)


def _softmax_rows(s):
    e = jnp.exp(s - jnp.max(s, axis=-1, keepdims=True))
    return e / jnp.sum(e, axis=-1, keepdims=True)


def _attn_fwd(x1, g_pre, wq, k, v, wo, g_post, tm, name):
    t, d = x1.shape
    m = k.shape[0]
    scale = XA_HEAD_DIM ** -0.5

    def body(x_ref, gp_ref, wq_ref, k_ref, v_ref, wo_ref, go_ref, x2_ref, h_ref, q_ref, o_ref, a_ref):
        x = x_ref[...]
        h = _rms_fwd(x, gp_ref[...]).astype(BF16)
        h_ref[...] = h
        q_ref[...] = _dot(h, wq_ref[...]).astype(BF16)
        for hd in range(XA_HEADS):
            cols = slice(hd * XA_HEAD_DIM, (hd + 1) * XA_HEAD_DIM)
            p = _softmax_rows(_dot_nt(q_ref[:, cols], k_ref[:, cols]) * scale)
            o_ref[:, cols] = _dot(p.astype(BF16), v_ref[:, cols]).astype(BF16)
        a = _dot(o_ref[...], wo_ref[...])
        a_ref[...] = a.astype(BF16)
        x2_ref[...] = x + _rms_fwd(a, go_ref[...])

    return pl.pallas_call(
        body, name=name, grid=(t // tm,),
        in_specs=[_tile(tm, d), _whole((1, d)), _resident((d, d)), _whole((m, d)), _whole((m, d)), _resident((d, d)),
                  _whole((1, d))],
        out_specs=[_tile(tm, d)] * 5,
        out_shape=[_sds((t, d), F32)] + [_sds((t, d), BF16)] * 4,
        compiler_params=_params(),
    )(x1, g_pre, wq, k, v, wo, g_post)


def _ffn_fwd(x2, g_pre, wg, wu, wd, g_post, target, tm, name):
    t, d = x2.shape
    f = wg.shape[1]

    def body(x_ref, gp_ref, wg_ref, wu_ref, wd_ref, go_ref, tgt_ref, h_ref, gt_ref, up_ref, f_ref, sq_ref):
        _zero_at_first_step(sq_ref)
        x = x_ref[...]
        h = _rms_fwd(x, gp_ref[...]).astype(BF16)
        h_ref[...] = h
        gt = _dot(h, wg_ref[...])
        up = _dot(h, wu_ref[...])
        gt_ref[...] = gt.astype(BF16)
        up_ref[...] = up.astype(BF16)
        hd = (gt * _sigmoid(gt) * up).astype(BF16)
        ff = _dot(hd, wd_ref[...])
        f_ref[...] = ff.astype(BF16)
        err = x + _rms_fwd(ff, go_ref[...]) - tgt_ref[...]
        sq_ref[...] += jnp.sum(err * err, axis=0, keepdims=True)

    return pl.pallas_call(
        body, name=name, grid=(t // tm,),
        in_specs=[_tile(tm, d), _whole((1, d)), _resident((d, f)), _resident((d, f)), _resident((f, d)), _whole((1, d)),
                  _tile(tm, d)],
        out_specs=[_tile(tm, d), _tile(tm, f), _tile(tm, f), _tile(tm, d), _whole((1, d))],
        out_shape=[_sds((t, d), BF16), _sds((t, f), BF16), _sds((t, f), BF16), _sds((t, d), BF16), _sds((1, d), F32)],
        compiler_params=_params(),
    )(x2, g_pre, wg, wu, wd, g_post, target)


def _ffn_bwd(x2, f, target, gt, up, g_pre, wg, wu, wd, g_post, tm, name):
    t, d = x2.shape
    ff = wg.shape[1]

    def body(x_ref, f_ref, tgt_ref, gt_ref, up_ref, gp_ref, wg_ref, wu_ref, wd_ref, go_ref,
             dx_ref, df_ref, hd_ref, dgt_ref, dup_ref, dgo_ref, dgp_ref):
        _zero_at_first_step(dgo_ref, dgp_ref)
        x = x_ref[...]
        fo = f_ref[...].astype(F32)
        dx3 = (x + _rms_fwd(fo, go_ref[...]) - tgt_ref[...]) * (1.0 / d)
        df, dgo = _rms_bwd(dx3, fo, go_ref[...])
        dgo_ref[...] += dgo
        df = df.astype(BF16)
        df_ref[...] = df
        dhd = _dot_nt(df, wd_ref[...])
        gt = gt_ref[...].astype(F32)
        up = up_ref[...].astype(F32)
        sg = _sigmoid(gt)
        si = gt * sg
        hd_ref[...] = (si * up).astype(BF16)
        dup = (dhd * si).astype(BF16)
        dgt = (dhd * up * (sg * (1.0 + gt * (1.0 - sg)))).astype(BF16)
        dup_ref[...] = dup
        dgt_ref[...] = dgt
        dh = _dot_nt(dgt, wg_ref[...]) + _dot_nt(dup, wu_ref[...])
        dxn, dgp = _rms_bwd(dh, x, gp_ref[...])
        dgp_ref[...] += dgp
        dx_ref[...] = dx3 + dxn

    return pl.pallas_call(
        body, name=name, grid=(t // tm,),
        in_specs=[_tile(tm, d), _tile(tm, d), _tile(tm, d), _tile(tm, ff), _tile(tm, ff), _whole((1, d)),
                  _resident((d, ff)), _resident((d, ff)), _resident((ff, d)), _whole((1, d))],
        out_specs=[_tile(tm, d), _tile(tm, d), _tile(tm, ff), _tile(tm, ff), _tile(tm, ff), _whole((1, d)), _whole((1, d))],
        out_shape=[_sds((t, d), F32), _sds((t, d), BF16), _sds((t, ff), BF16), _sds((t, ff), BF16), _sds((t, ff), BF16),
                   _sds((1, d), F32), _sds((1, d), F32)],
        compiler_params=_params(),
    )(x2, f, target, gt, up, g_pre, wg, wu, wd, g_post)


def _attn_bwd(dx2, a, x1, q, k, v, g_pre, wq, wo, g_post, tm, name, comm):
    t, d = x1.shape
    m = k.shape[0]
    scale = XA_HEAD_DIM ** -0.5

    def core(dx2_ref, a_ref, x_ref, q_ref, k_ref, v_ref, gp_ref, wq_ref, wo_ref, go_ref,
             dx1_ref, da_ref, dq_ref, dk_ref, dv_ref, dgo_ref, dgp_ref, do_buf):
        _zero_at_first_step(dgo_ref, dgp_ref, dk_ref, dv_ref)
        dx2 = dx2_ref[...]
        da, dgo = _rms_bwd(dx2, a_ref[...].astype(F32), go_ref[...])
        dgo_ref[...] += dgo
        da = da.astype(BF16)
        da_ref[...] = da
        do_buf[...] = _dot_nt(da, wo_ref[...]).astype(BF16)
        for hd in range(XA_HEADS):
            cols = slice(hd * XA_HEAD_DIM, (hd + 1) * XA_HEAD_DIM)
            qh = q_ref[:, cols]
            p = _softmax_rows(_dot_nt(qh, k_ref[:, cols]) * scale)
            do_h = do_buf[:, cols]
            dp = _dot_nt(do_h, v_ref[:, cols])
            dv_ref[:, cols] += _dot_tn(p.astype(BF16), do_h)
            ds = (p * (dp - jnp.sum(dp * p, axis=-1, keepdims=True)) * scale).astype(BF16)
            dq_ref[:, cols] = _dot(ds, k_ref[:, cols]).astype(BF16)
            dk_ref[:, cols] += _dot_tn(ds, qh)
        dh = _dot_nt(dq_ref[...], wq_ref[...])
        dxn, dgp = _rms_bwd(dh, x_ref[...], gp_ref[...])
        dgp_ref[...] += dgp
        dx1_ref[...] = dx2 + dxn

    return _hosted_call(
        core, comm, name, t // tm,
        in_specs=[_tile(tm, d), _tile(tm, d), _tile(tm, d), _tile(tm, d), _whole((m, d)), _whole((m, d)), _whole((1, d)),
                  _resident((d, d)), _resident((d, d)), _whole((1, d))],
        out_specs=[_tile(tm, d), _tile(tm, d), _tile(tm, d), _whole((m, d)), _whole((m, d)), _whole((1, d)), _whole((1, d))],
        out_shape=[_sds((t, d), F32), _sds((t, d), BF16), _sds((t, d), BF16), _sds((m, d), F32), _sds((m, d), F32),
                   _sds((1, d), F32), _sds((1, d), F32)],
        scratch_shapes=[pltpu.VMEM((tm, d), BF16)],
        operands=(dx2, a, x1, q, k, v, g_pre, wq, wo, g_post))


def _mem_bwd(mem, mem_n, dk, dv, g, wk, wv, name):
    m, d = mem.shape

    def body(mem_ref, n_ref, dk_ref, dv_ref, g_ref, wk_ref, wv_ref, dwk_ref, dwv_ref, dg_ref):
        dk = dk_ref[...].astype(BF16)
        dv = dv_ref[...].astype(BF16)
        n = n_ref[...]
        dwk_ref[...] = _dot_tn(n, dk).astype(BF16)
        dwv_ref[...] = _dot_tn(n, dv).astype(BF16)
        dn = _dot_nt(dk, wk_ref[...]) + _dot_nt(dv, wv_ref[...])
        _, dg = _rms_bwd(dn, mem_ref[...], g_ref[...])
        dg_ref[...] = dg

    return pl.pallas_call(
        body, name=name, grid=(1,),
        in_specs=[_whole((m, d)), _whole((m, d)), _whole((m, d)), _whole((m, d)), _whole((1, d)), _whole((d, d)),
                  _whole((d, d))],
        out_specs=[_whole((d, d)), _whole((d, d)), _whole((1, d))],
        out_shape=[_sds((d, d), BF16), _sds((d, d), BF16), _sds((1, d), F32)],
        compiler_params=_params(),
    )(mem, mem_n, dk, dv, g, wk, wv)


def _proj_bwd(dxo, s, w, g, tm, name):
    t, d = dxo.shape
    k = w.shape[0]

    def body(dx_ref, s_ref, w_ref, g_ref, ds_ref, da_ref, dg_ref):
        _zero_at_first_step(dg_ref)
        ds, dg = _rms_bwd(dx_ref[...], s_ref[...].astype(F32), g_ref[...])
        dg_ref[...] += dg
        ds = ds.astype(BF16)
        ds_ref[...] = ds
        da_ref[...] = _dot_nt(ds, w_ref[...]).astype(BF16)

    return pl.pallas_call(
        body, name=name, grid=(t // tm,),
        in_specs=[_tile(tm, d), _tile(tm, d), _resident((k, d)), _whole((1, d))],
        out_specs=[_tile(tm, d), _tile(tm, k), _whole((1, d))],
        out_shape=[_sds((t, d), BF16), _sds((t, k), BF16), _sds((1, d), F32)],
        compiler_params=_params(),
    )(dxo, s, w, g)


def _conv_bwd(dy, u, zc, ca, wa, wb, lg, lb, tc, name, comm):
    t = u.shape[0]
    n_tiles = t // tc
    n_chunks = tc // CONV_CHUNK

    def core(dy_ref, u_ref, zc_ref, ca_ref, wa_ref, wb_ref, lg_ref, lb_ref,
             du_ref, dwa_ref, dwb_ref, dbb_ref, dlg_ref, dlb_ref, ebuf, eabuf, zbuf, cvbuf, wacc, aacc, vacc):
        step = pl.program_id(0)

        @pl.when(step == 0)
        def _():
            ebuf[:, tc:tc + HALO_B, :] = jnp.zeros((D_B // LANES, HALO_B, LANES), F32)
            eabuf[:, tc:tc + HALO_A, :] = jnp.zeros((D_A // LANES, HALO_A, LANES), F32)
            wacc[...] = jnp.zeros_like(wacc)
            aacc[...] = jnp.zeros_like(aacc)
            vacc[...] = jnp.zeros_like(vacc)

        dbb = jnp.zeros((SUBLANES, D_B), F32)
        dlg = jnp.zeros((SUBLANES, D_B), F32)
        dlb = jnp.zeros((SUBLANES, D_B), F32)
        for c in range(n_chunks):
            rows = slice(c * CONV_CHUNK, (c + 1) * CONV_CHUNK)
            dy_a = dy_ref[rows, 0:D_A].astype(F32)
            b_a = u_ref[rows, 0:D_A].astype(F32)
            du_ref[rows, 0:D_A] = (dy_a * ca_ref[rows, :].astype(F32)).astype(BF16)
            dca = dy_a * b_a
            cv = u_ref[rows, D_A:2 * D_A].astype(F32) * u_ref[rows, 2 * D_A:3 * D_A].astype(F32)
            glu_g = u_ref[rows, 3 * D_A + D_B:3 * D_A + 2 * D_B].astype(F32)
            z = u_ref[rows, 3 * D_A:3 * D_A + D_B].astype(F32) * _sigmoid(glu_g)
            for lb_i in range(D_A // LANES):
                lanes = slice(lb_i * LANES, (lb_i + 1) * LANES)
                eabuf[lb_i, rows, :] = dca[:, lanes]
                cvbuf[lb_i, rows, :] = cv[:, lanes]
                zbuf[lb_i, rows, :] = z[:, lanes]

            zcv = zc_ref[rows, :].astype(F32)
            mu = jnp.mean(zcv, axis=-1, keepdims=True)
            xc = zcv - mu
            rstd = lax.rsqrt(jnp.mean(xc * xc, axis=-1, keepdims=True) + LN_EPS)
            xhat = xc * rstd
            ln = xhat * lg_ref[...] + lb_ref[...]
            sg = _sigmoid(ln)
            dln = dy_ref[rows, D_A:D_A + D_B].astype(F32) * (sg * (1.0 + ln * (1.0 - sg)))
            dlg = dlg + jnp.sum((dln * xhat).reshape(CONV_CHUNK // SUBLANES, SUBLANES, D_B), axis=0)
            dlb = dlb + jnp.sum(dln.reshape(CONV_CHUNK // SUBLANES, SUBLANES, D_B), axis=0)
            dxh = dln * lg_ref[...]
            dzc = rstd * (dxh - jnp.mean(dxh, axis=-1, keepdims=True) - xhat * jnp.mean(dxh * xhat, axis=-1, keepdims=True))
            dbb = dbb + jnp.sum(dzc.reshape(CONV_CHUNK // SUBLANES, SUBLANES, D_B), axis=0)
            for lb_i in range(D_B // LANES):
                ebuf[lb_i, rows, :] = dzc[:, lb_i * LANES:(lb_i + 1) * LANES]
        vacc[0] += dbb
        vacc[1] += dlg
        vacc[2] += dlb

        for lb_i in range(D_A // LANES):
            lanes = slice(lb_i * LANES, (lb_i + 1) * LANES)

            def cols(first):
                return slice(first + lb_i * LANES, first + (lb_i + 1) * LANES)

            for c in range(n_chunks):
                r0 = c * CONV_CHUNK
                rows = slice(r0, r0 + CONV_CHUNK)
                cv = cvbuf[lb_i, rows, :]
                dcv = jnp.zeros((CONV_CHUNK, LANES), F32)
                for k in range(CONV_A_W):
                    off = r0 + (CONV_A_W - 1) - k
                    e = eabuf[lb_i, off:off + CONV_CHUNK, :]
                    dcv = dcv + wa_ref[k:k + 1, lanes] * e
                    aacc[k, :, lanes] += jnp.sum((cv * e).reshape(CONV_CHUNK // SUBLANES, SUBLANES, LANES), axis=0)
                du_ref[rows, cols(D_A)] = (dcv * u_ref[rows, cols(2 * D_A)].astype(F32)).astype(BF16)
                du_ref[rows, cols(2 * D_A)] = (dcv * u_ref[rows, cols(D_A)].astype(F32)).astype(BF16)

                z = zbuf[lb_i, rows, :]
                dz = jnp.zeros((CONV_CHUNK, LANES), F32)
                for k in range(CONV_B_W):
                    off = r0 + (CONV_B_W - 1) - k
                    e = ebuf[lb_i, off:off + CONV_CHUNK, :]
                    dz = dz + wb_ref[k:k + 1, lanes] * e
                    wacc[k, :, lanes] += jnp.sum((z * e).reshape(CONV_CHUNK // SUBLANES, SUBLANES, LANES), axis=0)
                glu_v = u_ref[rows, cols(3 * D_A)].astype(F32)
                sgg = _sigmoid(u_ref[rows, cols(3 * D_A + D_B)].astype(F32))
                du_ref[rows, cols(3 * D_A)] = (dz * sgg).astype(BF16)
                du_ref[rows, cols(3 * D_A + D_B)] = (dz * glu_v * sgg * (1.0 - sgg)).astype(BF16)

            ebuf[lb_i, tc:tc + HALO_B, :] = ebuf[lb_i, 0:HALO_B, :]
            eabuf[lb_i, tc:tc + HALO_A, :] = eabuf[lb_i, 0:HALO_A, :]

        @pl.when(step == n_tiles - 1)
        def _():
            for k in range(CONV_B_W):
                dwb_ref[k:k + 1, :] = jnp.sum(wacc[k], axis=0, keepdims=True)
            for k in range(CONV_A_W):
                dwa_ref[k:k + 1, :] = jnp.sum(aacc[k], axis=0, keepdims=True)
            dbb_ref[...] = jnp.sum(vacc[0], axis=0, keepdims=True)
            dlg_ref[...] = jnp.sum(vacc[1], axis=0, keepdims=True)
            dlb_ref[...] = jnp.sum(vacc[2], axis=0, keepdims=True)

    return _hosted_call(
        core, comm, name, n_tiles,
        in_specs=[_rtile(tc, D_A + D_B, n_tiles), _rtile(tc, D_IN_ALL, n_tiles), _rtile(tc, D_B, n_tiles),
                  _rtile(tc, D_A, n_tiles), _whole((CONV_A_W, D_A)), _whole((CONV_B_W, D_B)), _whole((1, D_B)),
                  _whole((1, D_B))],
        out_specs=[_rtile(tc, D_IN_ALL, n_tiles), _whole((CONV_A_W, D_A)), _whole((CONV_B_W, D_B)), _whole((1, D_B)),
                   _whole((1, D_B)), _whole((1, D_B))],
        out_shape=[_sds((t, D_IN_ALL), BF16), _sds((CONV_A_W, D_A), F32), _sds((CONV_B_W, D_B), F32), _sds((1, D_B), F32),
                   _sds((1, D_B), F32), _sds((1, D_B), F32)],
        scratch_shapes=[pltpu.VMEM((D_B // LANES, tc + HALO_B, LANES), F32), pltpu.VMEM((D_A // LANES, tc + HALO_A, LANES), F32),
                        pltpu.VMEM((D_B // LANES, tc, LANES), F32), pltpu.VMEM((D_A // LANES, tc, LANES), F32),
                        pltpu.VMEM((CONV_B_W, SUBLANES, D_B), F32), pltpu.VMEM((CONV_A_W, SUBLANES, D_A), F32),
                        pltpu.VMEM((3, SUBLANES, D_B), F32)],
        operands=(dy, u, zc, ca, wa, wb, lg, lb))


def _in_bwd(du, w, x, dx1, g, tm, name):
    t, d = x.shape
    n = w.shape[1]

    def body(du_ref, w_ref, x_ref, dx1_ref, g_ref, dx_ref, dg_ref):
        _zero_at_first_step(dg_ref)
        dh = _dot_nt(du_ref[...], w_ref[...])
        dxn, dg = _rms_bwd(dh, x_ref[...], g_ref[...])
        dg_ref[...] += dg
        dx_ref[...] = dx1_ref[...] + dxn

    return pl.pallas_call(
        body, name=name, grid=(t // tm,),
        in_specs=[_tile(tm, n), _resident((d, n)), _tile(tm, d), _tile(tm, d), _whole((1, d))],
        out_specs=[_tile(tm, d), _whole((1, d))],
        out_shape=[_sds((t, d), F32), _sds((1, d), F32)],
        compiler_params=_params(),
    )(du, w, x, dx1, g)


def _wgrad(a, b, tk, bm, bn, name, comm=None, rows=None):
    t = a.shape[0]
    first, m = (0, a.shape[1]) if rows is None else rows
    first_block = first // bm
    n = b.shape[1]
    n_k = t // tk

    def body(a_ref, b_ref, o_ref, acc):
        @pl.when(pl.program_id(2) == 0)
        def _():
            acc[...] = jnp.zeros_like(acc)

        acc[...] += _dot_tn(a_ref[...], b_ref[...])

        @pl.when(pl.program_id(2) == n_k - 1)
        def _():
            o_ref[...] = acc[...].astype(BF16)

    call = dict(
        grid=(m // bm, n // bn, n_k),
        in_specs=[pl.BlockSpec((tk, bm), lambda i, j, k: (k, first_block + i)),
                  pl.BlockSpec((tk, bn), lambda i, j, k: (k, j))],
        out_specs=[pl.BlockSpec((bm, bn), lambda i, j, k: (i, j))],
        out_shape=[_sds((m, n), BF16)],
        scratch_shapes=[pltpu.VMEM((bm, bn), F32)])
    if comm is None:
        return pl.pallas_call(body, name=name, compiler_params=_params(3), **call)(a, b)[0]
    (out,), received = _hosted_call(body, comm, name, operands=(a, b), **call)
    return out, received


def _adamw(parts, w, m, v, tr, name):
    n_parts, r, c = parts.shape

    def body(p_ref, w_ref, m_ref, v_ref, g_ref, d_ref, nm_ref, nv_ref):
        g = p_ref[0].astype(F32)
        for j in range(1, n_parts):
            g = g + p_ref[j].astype(F32)
        g_ref[...] = g
        nm = ADAM_B1 * m_ref[...] + (1.0 - ADAM_B1) * g
        nv = ADAM_B2 * v_ref[...] + (1.0 - ADAM_B2) * (g * g)
        nm_ref[...] = nm
        nv_ref[...] = nv
        m_hat = nm / (1.0 - ADAM_B1 ** ADAM_STEP)
        v_hat = nv / (1.0 - ADAM_B2 ** ADAM_STEP)
        d_ref[...] = -ADAM_LR * (m_hat / (jnp.sqrt(v_hat) + ADAM_EPS) + ADAM_WD * w_ref[...])

    return pl.pallas_call(
        body, name=name, grid=(r // tr,),
        in_specs=[pl.BlockSpec((n_parts, tr, c), lambda i: (0, i, 0))] + [_tile(tr, c)] * 3,
        out_specs=[_tile(tr, c)] * 4,
        out_shape=[_sds((r, c), F32)] * 4,
        compiler_params=_params(),
    )(parts, w, m, v)


def _sum_parts(parts, name):
    n_parts, r, c = parts.shape

    def body(p_ref, o_ref):
        acc = p_ref[0]
        for j in range(1, n_parts):
            acc = acc + p_ref[j]
        o_ref[...] = acc

    return pl.pallas_call(
        body, name=name, grid=(1,),
        in_specs=[_whole((n_parts, r, c))], out_specs=_whole((r, c)), out_shape=_sds((r, c), F32),
        compiler_params=_params(),
    )(parts)


def _row(v):
    return v.reshape(1, -1)


def _by_owner_columns(g):
    rows, cols = g.shape
    return g.reshape(rows, N_DEV, cols // N_DEV).transpose(1, 0, 2)


def _from_owner_columns(g):
    n_dev, rows, cols = g.shape
    return g.transpose(1, 0, 2).reshape(rows, n_dev * cols)


def _by_owner_rows(g):
    return g.reshape(N_DEV, g.shape[0] // N_DEV, g.shape[1])


WEIGHTS = ("mix_pre_g", "w_mix_in", "conv_a_w", "conv_b_w", "conv_b_b", "ln_b_g", "ln_b_b", "w_mix_out", "mix_post_g",
           "xa_pre_g", "mem_norm_g", "w_q", "w_k", "w_v", "w_o", "xa_post_g", "ffn_pre_g", "w_gate", "w_up", "w_down",
           "ffn_post_g")
LARGE = ("w_mix_in", "w_mix_out", "w_q", "w_k", "w_v", "w_o", "w_gate", "w_up", "w_down")
GAINS = ("mix_pre_g", "mix_post_g", "xa_pre_g", "mem_norm_g", "xa_post_g", "ffn_pre_g", "ffn_post_g")
CHANNEL_VECTORS = ("conv_b_b", "ln_b_g", "ln_b_b")
CONV_TAPS = ("conv_a_w", "conv_b_w")
SMALL = GAINS + CHANNEL_VECTORS + CONV_TAPS
CONV_COLS_PER_DEVICE = D_A // N_DEV
TOKEN_TILE = 512
FFN_TOKEN_TILE = 256
CONV_TOKEN_TILE = 256
WGRAD_TOKEN_TILE = 2048
ADAM_ROWS_PER_STEP = 256


def _lane_rows(v):
    flat = v.reshape(-1)
    tile = SUBLANES * LANES
    flat = jnp.pad(flat, (0, (-flat.shape[0]) % tile))
    return flat.reshape(-1, LANES)


def _pack_small(values, names):
    return jnp.concatenate([_lane_rows(values[n]) for n in names], axis=0)


def _unpack_small(packed, like, names):
    out, off = {}, 0
    for n in names:
        size = like[n].size
        rows = _lane_rows(like[n]).shape[0]
        out[n] = packed[off:off + rows, :].reshape(-1)[:size].reshape(like[n].shape)
        off += rows
    return out


def kernel(x, mem, mix_pre_g, w_mix_in, conv_a_w, conv_b_w, conv_b_b, ln_b_g, ln_b_b, w_mix_out, mix_post_g, xa_pre_g, mem_norm_g, w_q, w_k, w_v, w_o, xa_post_g, ffn_pre_g, w_gate, w_up, w_down, ffn_post_g, loss_target, m_mix_pre_g, m_w_mix_in, m_conv_a_w, m_conv_b_w, m_conv_b_b, m_ln_b_g, m_ln_b_b, m_w_mix_out, m_mix_post_g, m_xa_pre_g, m_mem_norm_g, m_w_q, m_w_k, m_w_v, m_w_o, m_xa_post_g, m_ffn_pre_g, m_w_gate, m_w_up, m_w_down, m_ffn_post_g, v_mix_pre_g, v_w_mix_in, v_conv_a_w, v_conv_b_w, v_conv_b_b, v_ln_b_g, v_ln_b_b, v_w_mix_out, v_mix_post_g, v_xa_pre_g, v_mem_norm_g, v_w_q, v_w_k, v_w_v, v_w_o, v_xa_post_g, v_ffn_pre_g, v_w_gate, v_w_up, v_w_down, v_ffn_post_g):
    given = dict(locals())
    w = {n: given[n] for n in WEIGHTS}
    m = {n: given["m_" + n] for n in WEIGHTS}
    v = {n: given["v_" + n] for n in WEIGHTS}
    xs, mems, target = x[0], mem[0], loss_target[0]
    t = xs.shape[0]
    tm, tm_ffn, tc, tk = min(TOKEN_TILE, t), min(FFN_TOKEN_TILE, t), min(CONV_TOKEN_TILE, t), min(WGRAD_TOKEN_TILE, t)
    g = {n: _row(w[n]) for n in GAINS}
    bb, lg, lb = (_row(w[n]) for n in CHANNEL_VECTORS)

    def shard_bf16(*names):
        return [w[n].astype(BF16) for n in names]

    g_mix_in, g_taps = _comm_call(_Gather(shard_bf16("w_mix_in") + [_pack_small(w, CONV_TAPS)]), "gather_mixer")
    w_mix_in = _from_owner_columns(g_mix_in)
    taps, off = {}, 0
    for n in CONV_TAPS:
        k, cols = w[n].shape
        rows = _lane_rows(w[n]).shape[0]
        blk = g_taps[:, off:off + rows, :].reshape(N_DEV, -1)[:, :k * cols].reshape(N_DEV, k, cols)
        taps[n] = blk.transpose(1, 0, 2).reshape(k, N_DEV * cols)
        off += rows
    wa, wb = taps["conv_a_w"], taps["conv_b_w"]

    (h1, u), gathered = _norm_matmul(xs, g["mix_pre_g"], w_mix_in, tm, "mix_in_fwd",
                                     _Gather(shard_bf16("w_mix_out", "w_q", "w_k", "w_v", "w_o")))
    w_mix_out, w_q, w_k, w_v, w_o = (a.reshape(D_MODEL, D_MODEL) for a in gathered)
    (ycat, zc, ca), (g_gate, g_up, g_down) = _conv_fwd(u, wa, wb, bb, lg, lb, tc, "conv_fwd",
                                                       _Gather(shard_bf16("w_gate", "w_up", "w_down")))
    w_gate, w_up, w_down = _from_owner_columns(g_gate), _from_owner_columns(g_up), g_down.reshape(D_FF, D_MODEL)
    mem_n, kk, vv = _mem_fwd(mems, g["mem_norm_g"], w_k, w_v, "mem_fwd")
    x1, s1 = _proj_norm_res(ycat, w_mix_out, xs, g["mix_post_g"], tm, "mix_out_fwd")
    x2, h2, q, o, a = _attn_fwd(x1, g["xa_pre_g"], w_q, kk, vv, w_o, g["xa_post_g"], tm, "attn_fwd")
    h3, gt, up, f, sq = _ffn_fwd(x2, g["ffn_pre_g"], w_gate, w_up, w_down, g["ffn_post_g"], target, tm_ffn, "ffn_fwd")

    dx2, df, hd, dgt, dup, d_ffn_post, d_ffn_pre = _ffn_bwd(
        x2, f, target, gt, up, g["ffn_pre_g"], w_gate, w_up, w_down, g["ffn_post_g"], tm_ffn, "ffn_bwd")
    d_w_down = _wgrad(hd, df, tk, D_FF // 2, D_MODEL, "wgrad_down")
    d_w_gate = _wgrad(h3, dgt, tk, D_MODEL, D_FF // 2, "wgrad_gate")
    d_w_up = _wgrad(h3, dup, tk, D_MODEL, D_FF // 2, "wgrad_up")
    ffn_slabs = [_by_owner_columns(d_w_gate), _by_owner_columns(d_w_up), _by_owner_rows(d_w_down)]

    (dx1, da, dq, dk, dv, d_xa_post, d_xa_pre), from_ffn = _attn_bwd(
        dx2, a, x1, q, kk, vv, g["xa_pre_g"], w_q, w_o, g["xa_post_g"], tm, "attn_bwd",
        _Exchange(ffn_slabs, [True] * 3))
    d_w_o = _wgrad(o, da, tk, D_MODEL, D_MODEL, "wgrad_o")
    d_w_q = _wgrad(h2, dq, tk, D_MODEL, D_MODEL, "wgrad_q")
    d_w_k, d_w_v, d_mem_norm = _mem_bwd(mems, mem_n, dk, dv, g["mem_norm_g"], w_k, w_v, "mem_bwd")
    ds1, dycat, d_mix_post = _proj_bwd(dx1, s1, w_mix_out, g["mix_post_g"], tm, "mix_out_bwd")
    d_w_mix_out = _wgrad(ycat, ds1, tk, D_MODEL, D_MODEL, "wgrad_mix_out")
    attn_slabs = [_by_owner_rows(d) for d in (d_w_mix_out, d_w_q, d_w_k, d_w_v, d_w_o)]

    (du, d_conv_a, d_conv_b, d_conv_bb, d_ln_g, d_ln_b), from_attn = _conv_bwd(
        dycat, u, zc, ca, wa, wb, lg, lb, tc, "conv_bwd", _Exchange(attn_slabs, [True] * 5))
    half = D_MODEL // 2
    d_in_lo = _wgrad(h1, du, tk, half, D_IN_ALL // 2, "wgrad_mix_in_lo", rows=(0, half))
    d_in_hi, (from_in_lo,) = _wgrad(h1, du, tk, half, D_IN_ALL // 2, "wgrad_mix_in_hi",
                                    _Exchange([_by_owner_columns(d_in_lo)], [True]), rows=(half, half))
    dx, d_mix_pre = _in_bwd(du, w_mix_in, xs, dx1, g["mix_pre_g"], tm, "mix_in_bwd")

    small_grads = dict(mix_pre_g=d_mix_pre, conv_a_w=d_conv_a, conv_b_w=d_conv_b, conv_b_b=d_conv_bb, ln_b_g=d_ln_g,
                       ln_b_b=d_ln_b, mix_post_g=d_mix_post, xa_pre_g=d_xa_pre, mem_norm_g=d_mem_norm, xa_post_g=d_xa_post,
                       ffn_pre_g=d_ffn_pre, ffn_post_g=d_ffn_post, loss=sq)
    names = SMALL + ("loss",)
    from_in_hi, all_small = _comm_call(
        _Exchange([_by_owner_columns(d_in_hi), _pack_small(small_grads, names)], [True, False]), "reduce_tail")

    received = dict(zip(("w_gate", "w_up", "w_down"), from_ffn))
    received.update(zip(("w_mix_out", "w_q", "w_k", "w_v", "w_o"), from_attn))
    received["w_mix_in"] = jnp.concatenate([from_in_lo, from_in_hi], axis=1)
    grad, delta, new_m, new_v = {}, {}, {}, {}
    for n in LARGE:
        rows = w[n].shape[0]
        tr = ADAM_ROWS_PER_STEP if rows % ADAM_ROWS_PER_STEP == 0 else rows
        grad[n], delta[n], new_m[n], new_v[n] = _adamw(received[n], w[n], m[n], v[n], tr, "adamw_" + n)

    total = _unpack_small(_sum_parts(all_small, "sum_small"), small_grads, names)
    loss = jnp.sum(total.pop("loss")) * (0.5 / D_MODEL)
    first_col = _device_index() * CONV_COLS_PER_DEVICE
    for n in CONV_TAPS:
        total[n] = lax.dynamic_slice_in_dim(total[n], first_col, CONV_COLS_PER_DEVICE, axis=1)
    total = {n: total[n].reshape(w[n].shape) for n in SMALL}
    packed_small = [_pack_small(values, SMALL) for values in (total, w, m, v)]
    g_s, d_s, nm_s, nv_s = _adamw(packed_small[0][None], *packed_small[1:], packed_small[0].shape[0], "adamw_small")
    for out, p in ((grad, g_s), (delta, d_s), (new_m, nm_s), (new_v, nv_s)):
        out.update(_unpack_small(p, w, SMALL))

    return (loss, dx[None], *[grad[n] for n in WEIGHTS], *[delta[n] for n in WEIGHTS], *[new_m[n] for n in WEIGHTS],
            *[new_v[n] for n in WEIGHTS])
```

```python
import functools

import jax
import jax.numpy as jnp
from jax import lax
from jax.experimental import pallas as pl
from jax.experimental.pallas import tpu as pltpu

F32 = jnp.float32
BF16 = jnp.bfloat16

D_MODEL = 1024
D_A = 512
D_B = 512
D_IN_ALL = 2560
CONV_A_W = 3
CONV_B_W = 31
XA_HEADS = 4
XA_HEAD_DIM = 256
D_FF = 2816
N_DEV = 8
RMS_EPS = 1e-6
LN_EPS = 1e-5
ADAM_LR = 0.001
ADAM_B1 = 0.9
ADAM_B2 = 0.999
ADAM_EPS = 1e-08
ADAM_WD = 0.01
ADAM_STEP = 10

VMEM_LIMIT_BYTES = 56 * 1024 * 1024
SUBLANES = 8
LANES = 128
HALO_B = 32
HALO_A = 8
CONV_CHUNK = 32

MESH = pl.DeviceIdType.MESH


def _params(n_grid_axes=1):
    return pltpu.CompilerParams(dimension_semantics=("arbitrary",) * n_grid_axes, vmem_limit_bytes=VMEM_LIMIT_BYTES)


def _sds(shape, dtype):
    return jax.ShapeDtypeStruct(shape, dtype)


def _tile(rows, cols):
    return pl.BlockSpec((rows, cols), lambda i: (i, 0))


def _rtile(rows, cols, n):
    return pl.BlockSpec((rows, cols), lambda i: (n - 1 - i, 0))


def _whole(shape):
    zeros = (0,) * len(shape)
    return pl.BlockSpec(shape, lambda i: zeros)


def _resident(shape):
    zeros = (0,) * len(shape)
    return pl.BlockSpec(shape, lambda i: zeros, pipeline_mode=pl.Buffered(1))


def _dot(a, b):
    return jnp.dot(a, b, preferred_element_type=F32)


def _dot_nt(a, b):
    return lax.dot_general(a, b, (((1,), (1,)), ((), ())), preferred_element_type=F32)


def _dot_tn(a, b):
    return lax.dot_general(a, b, (((0,), (0,)), ((), ())), preferred_element_type=F32)


def _sigmoid(x):
    return 1.0 / (1.0 + jnp.exp(-x))


def _rms_fwd(x, g):
    r = lax.rsqrt(jnp.mean(x * x, axis=-1, keepdims=True) + RMS_EPS)
    return x * r * g


def _rms_bwd(dy, xin, g):
    r = lax.rsqrt(jnp.mean(xin * xin, axis=-1, keepdims=True) + RMS_EPS)
    n = xin * r
    dg = jnp.sum(dy * n, axis=0, keepdims=True)
    dn = dy * g
    dx = r * (dn - n * jnp.mean(dn * n, axis=-1, keepdims=True))
    return dx, dg


def _zero_at_first_step(*refs):
    @pl.when(pl.program_id(0) == 0)
    def _():
        for ref in refs:
            ref[...] = jnp.zeros(ref.shape, ref.dtype)


def _place():
    return lax.axis_index("x"), lax.axis_index("y"), lax.axis_index("c")


def _device_index():
    x, y, c = _place()
    return 4 * x + 2 * y + c


class _Gather:
    def __init__(self, arrays):
        self.arrays = list(arrays)
        self.out_shape = [_sds((N_DEV, *a.shape), a.dtype) for a in self.arrays]
        n = len(self.arrays)
        self.scratch_shapes = [pltpu.SemaphoreType.DMA((n, 7)), pltpu.SemaphoreType.DMA((n, 7)),
                               pltpu.SemaphoreType.DMA((n,))]

    def forward_step(self, n_steps):
        return max(0, n_steps - 9)

    def bind(self, srcs, dsts, send_sems, recv_sems, local_sems):
        x, y, cc = _place()
        me, sibling = (x, y, cc), (x, y, 1 - cc)
        chips = [(1 - x, y), (x, 1 - y), (1 - x, 1 - y)]

        def copy(a, k, owner, to, src=None):
            slot = dsts[a].at[4 * owner[0] + 2 * owner[1] + owner[2]]
            return pltpu.make_async_remote_copy(
                src_ref=slot if src is None else src, dst_ref=slot, send_sem=send_sems.at[a, k],
                recv_sem=recv_sems.at[a, k], device_id=to, device_id_type=MESH)

        def first(a):
            return [copy(a, 0, me, sibling, src=srcs[a])] + [
                copy(a, 1 + j, me, (*chip, cc), src=srcs[a]) for j, chip in enumerate(chips)]

        def passed(a, j):
            return copy(a, 4 + j, (*chips[j], cc), sibling)

        def mine(a):
            return pltpu.make_async_copy(srcs[a], dsts[a].at[4 * x + 2 * y + cc], local_sems.at[a])

        def start():
            for a in range(len(srcs)):
                mine(a).start()
                for cp in first(a):
                    cp.start()

        def forward():
            for a in range(len(srcs)):
                for j, chip in enumerate(chips):
                    copy(a, 1 + j, (*chip, cc), me).wait_recv()
                    passed(a, j).start()

        def finish():
            for a in range(len(srcs)):
                copy(a, 0, sibling, me).wait_recv()
                for j, chip in enumerate(chips):
                    copy(a, 4 + j, (*chip, 1 - cc), me).wait_recv()
                for cp in first(a) + [passed(a, j) for j in range(len(chips))]:
                    cp.wait_send()
                mine(a).wait()

        return start, forward, finish


class _Exchange:
    def __init__(self, arrays, scatter):
        self.arrays = list(arrays)
        self.scatter = list(scatter)
        self.out_shape = [_sds(a.shape if s else (N_DEV, *a.shape), a.dtype) for a, s in zip(self.arrays, self.scatter)]
        n = len(self.arrays)
        self.scratch_shapes = [pltpu.SemaphoreType.DMA((n, 7)), pltpu.SemaphoreType.DMA((n, 7)),
                               pltpu.SemaphoreType.DMA((n,))]

    def forward_step(self, n_steps):
        return n_steps - 1

    def bind(self, srcs, dsts, send_sems, recv_sems, local_sems):
        me = _device_index()

        def copies(a):
            out = []
            for k in range(1, N_DEV):
                p = me ^ k
                out.append(pltpu.make_async_remote_copy(
                    src_ref=srcs[a].at[p] if self.scatter[a] else srcs[a], dst_ref=dsts[a].at[me],
                    send_sem=send_sems.at[a, k - 1], recv_sem=recv_sems.at[a, k - 1],
                    device_id=(p >> 2, (p >> 1) & 1, p & 1), device_id_type=MESH))
            return out

        def mine(a):
            return pltpu.make_async_copy(srcs[a].at[me] if self.scatter[a] else srcs[a], dsts[a].at[me], local_sems.at[a])

        def start():
            for a in range(len(srcs)):
                mine(a).start()
                for cp in copies(a):
                    cp.start()

        def forward():
            pass

        def finish():
            for a in range(len(srcs)):
                for cp in copies(a):
                    cp.wait()
                mine(a).wait()

        return start, forward, finish


def _hosted_call(core, comm, name, grid, in_specs, out_specs, out_shape, scratch_shapes, operands):
    grid = (grid,) if isinstance(grid, int) else tuple(grid)
    n_steps = 1
    for extent in grid:
        n_steps *= extent
    n_in, n_out, n_scr, n_arr = len(in_specs), len(out_specs), len(scratch_shapes), len(comm.arrays)
    any_spec = pl.BlockSpec(memory_space=pl.ANY)

    def body(*refs):
        ins, refs = refs[:n_in], refs[n_in:]
        srcs, refs = refs[:n_arr], refs[n_arr:]
        outs, refs = refs[:n_out], refs[n_out:]
        dsts, refs = refs[:n_arr], refs[n_arr:]
        scratch, sems = refs[:n_scr], refs[n_scr:]
        start, forward, finish = comm.bind(srcs, dsts, *sems)
        step = pl.program_id(0)
        for axis in range(1, len(grid)):
            step = step * grid[axis] + pl.program_id(axis)
        pl.when(step == 0)(start)
        core(*ins, *outs, *scratch)
        pl.when(step == comm.forward_step(n_steps))(forward)
        pl.when(step == n_steps - 1)(finish)

    results = pl.pallas_call(
        body, name=name, grid=grid,
        in_specs=list(in_specs) + [any_spec] * n_arr,
        out_specs=list(out_specs) + [any_spec] * n_arr,
        out_shape=list(out_shape) + comm.out_shape,
        scratch_shapes=list(scratch_shapes) + comm.scratch_shapes,
        compiler_params=_params(len(grid)),
    )(*operands, *comm.arrays)
    return results[:n_out], results[n_out:]


def _comm_call(comm, name):
    return _hosted_call(lambda: None, comm, name, 1, [], [], [], [], [])[1]


def _norm_matmul(x, g, w, tm, name, comm):
    t, d = x.shape
    n = w.shape[1]

    def core(x_ref, g_ref, w_ref, h_ref, o_ref):
        h = _rms_fwd(x_ref[...], g_ref[...]).astype(BF16)
        h_ref[...] = h
        o_ref[...] = _dot(h, w_ref[...]).astype(BF16)

    return _hosted_call(
        core, comm, name, t // tm,
        in_specs=[_tile(tm, d), _whole((1, d)), _resident((d, n))],
        out_specs=[_tile(tm, d), _tile(tm, n)],
        out_shape=[_sds((t, d), BF16), _sds((t, n), BF16)],
        scratch_shapes=[], operands=(x, g, w))


def _conv_fwd(u, wa, wb, bb, lg, lb, tc, name, comm):
    t = u.shape[0]
    n_chunks = tc // CONV_CHUNK

    def core(u_ref, wa_ref, wb_ref, bb_ref, lg_ref, lb_ref, y_ref, zc_ref, ca_ref, zbuf, cvbuf, zcbuf):
        @pl.when(pl.program_id(0) == 0)
        def _():
            zbuf[:, 0:HALO_B, :] = jnp.zeros((D_B // LANES, HALO_B, LANES), F32)
            cvbuf[:, 0:HALO_A, :] = jnp.zeros((D_A // LANES, HALO_A, LANES), F32)

        for lb_i in range(D_A // LANES):
            lanes = slice(lb_i * LANES, (lb_i + 1) * LANES)
            c_a = u_ref[:, D_A + lb_i * LANES:D_A + (lb_i + 1) * LANES].astype(F32)
            v_a = u_ref[:, 2 * D_A + lb_i * LANES:2 * D_A + (lb_i + 1) * LANES].astype(F32)
            cvbuf[lb_i, HALO_A:HALO_A + tc, :] = c_a * v_a
            glu_v = u_ref[:, 3 * D_A + lb_i * LANES:3 * D_A + (lb_i + 1) * LANES].astype(F32)
            glu_g = u_ref[:, 3 * D_A + D_B + lb_i * LANES:3 * D_A + D_B + (lb_i + 1) * LANES].astype(F32)
            zbuf[lb_i, HALO_B:HALO_B + tc, :] = glu_v * _sigmoid(glu_g)

            for c in range(n_chunks):
                r0 = c * CONV_CHUNK
                rows = slice(r0, r0 + CONV_CHUNK)
                acc = jnp.zeros((CONV_CHUNK, LANES), F32)
                for k in range(CONV_A_W):
                    off = r0 + HALO_A - (CONV_A_W - 1) + k
                    acc = acc + wa_ref[k:k + 1, lanes] * cvbuf[lb_i, off:off + CONV_CHUNK, :]
                ca_ref[rows, lanes] = acc.astype(BF16)
                y_ref[rows, lanes] = (u_ref[rows, lanes].astype(F32) * acc).astype(BF16)

                accb = jnp.zeros((CONV_CHUNK, LANES), F32)
                for k in range(CONV_B_W):
                    off = r0 + HALO_B - (CONV_B_W - 1) + k
                    accb = accb + wb_ref[k:k + 1, lanes] * zbuf[lb_i, off:off + CONV_CHUNK, :]
                zcbuf[rows, lanes] = accb + bb_ref[:, lanes]

            zbuf[lb_i, 0:HALO_B, :] = zbuf[lb_i, tc:tc + HALO_B, :]
            cvbuf[lb_i, 0:HALO_A, :] = cvbuf[lb_i, tc:tc + HALO_A, :]

        for c in range(n_chunks):
            rows = slice(c * CONV_CHUNK, (c + 1) * CONV_CHUNK)
            zc = zcbuf[rows, :]
            zc_ref[rows, :] = zc.astype(BF16)
            mu = jnp.mean(zc, axis=-1, keepdims=True)
            xc = zc - mu
            var = jnp.mean(xc * xc, axis=-1, keepdims=True)
            ln = xc * lax.rsqrt(var + LN_EPS) * lg_ref[...] + lb_ref[...]
            y_ref[rows, D_A:D_A + D_B] = (ln * _sigmoid(ln)).astype(BF16)

    return _hosted_call(
        core, comm, name, t // tc,
        in_specs=[_tile(tc, D_IN_ALL), _whole((CONV_A_W, D_A)), _whole((CONV_B_W, D_B)), _whole((1, D_B)),
                  _whole((1, D_B)), _whole((1, D_B))],
        out_specs=[_tile(tc, D_A + D_B), _tile(tc, D_B), _tile(tc, D_A)],
        out_shape=[_sds((t, D_A + D_B), BF16), _sds((t, D_B), BF16), _sds((t, D_A), BF16)],
        scratch_shapes=[pltpu.VMEM((D_B // LANES, HALO_B + tc, LANES), F32),
                        pltpu.VMEM((D_A // LANES, HALO_A + tc, LANES), F32), pltpu.VMEM((tc, D_B), F32)],
        operands=(u, wa, wb, bb, lg, lb))


def _proj_norm_res(a, w, xres, g, tm, name):
    t, k = a.shape
    d = w.shape[1]

    def body(a_ref, w_ref, x_ref, g_ref, xo_ref, s_ref):
        s = _dot(a_ref[...], w_ref[...])
        s_ref[...] = s.astype(BF16)
        xo_ref[...] = x_ref[...] + _rms_fwd(s, g_ref[...])

    return pl.pallas_call(
        body, name=name, grid=(t // tm,),
        in_specs=[_tile(tm, k), _resident((k, d)), _tile(tm, d), _whole((1, d))],
        out_specs=[_tile(tm, d), _tile(tm, d)],
        out_shape=[_sds((t, d), F32), _sds((t, d), BF16)],
        compiler_params=_params(),
    )(a, w, xres, g)


def _mem_fwd(mem, g, wk, wv, name):
    m, d = mem.shape

    def body(mem_ref, g_ref, wk_ref, wv_ref, n_ref, k_ref, v_ref):
        n = _rms_fwd(mem_ref[...], g_ref[...]).astype(BF16)
        n_ref[...] = n
        k_ref[...] = _dot(n, wk_ref[...]).astype(BF16)
        v_ref[...] = _dot(n, wv_ref[...]).astype(BF16)

    return pl.pallas_call(
        body, name=name, grid=(1,),
        in_specs=[_whole((m, d)), _whole((1, d)), _whole((d, d)), _whole((d, d))],
        out_specs=[_whole((m, d))] * 3,
        out_shape=[_sds((m, d), BF16)] * 3,
        compiler_params=_params(),
    )(mem, g, wk, wv)


def _softmax_rows(s):
    e = jnp.exp(s - jnp.max(s, axis=-1, keepdims=True))
    return e / jnp.sum(e, axis=-1, keepdims=True)


def _attn_fwd(x1, g_pre, wq, k, v, wo, g_post, tm, name):
    t, d = x1.shape
    m = k.shape[0]
    scale = XA_HEAD_DIM ** -0.5

    def body(x_ref, gp_ref, wq_ref, k_ref, v_ref, wo_ref, go_ref, x2_ref, h_ref, q_ref, o_ref, a_ref):
        x = x_ref[...]
        h = _rms_fwd(x, gp_ref[...]).astype(BF16)
        h_ref[...] = h
        q_ref[...] = _dot(h, wq_ref[...]).astype(BF16)
        for hd in range(XA_HEADS):
            cols = slice(hd * XA_HEAD_DIM, (hd + 1) * XA_HEAD_DIM)
            p = _softmax_rows(_dot_nt(q_ref[:, cols], k_ref[:, cols]) * scale)
            o_ref[:, cols] = _dot(p.astype(BF16), v_ref[:, cols]).astype(BF16)
        a = _dot(o_ref[...], wo_ref[...])
        a_ref[...] = a.astype(BF16)
        x2_ref[...] = x + _rms_fwd(a, go_ref[...])

    return pl.pallas_call(
        body, name=name, grid=(t // tm,),
        in_specs=[_tile(tm, d), _whole((1, d)), _resident((d, d)), _whole((m, d)), _whole((m, d)), _resident((d, d)),
                  _whole((1, d))],
        out_specs=[_tile(tm, d)] * 5,
        out_shape=[_sds((t, d), F32)] + [_sds((t, d), BF16)] * 4,
        compiler_params=_params(),
    )(x1, g_pre, wq, k, v, wo, g_post)


def _ffn_fwd(x2, g_pre, wg_t, wu_t, wd, g_post, target, tm, name):
    t, d = x2.shape
    f = wg_t.shape[0]

    def body(x_ref, gp_ref, wg_ref, wu_ref, wd_ref, go_ref, tgt_ref, h_ref, gt_ref, up_ref, f_ref, sq_ref):
        _zero_at_first_step(sq_ref)
        x = x_ref[...]
        h = _rms_fwd(x, gp_ref[...]).astype(BF16)
        h_ref[...] = h
        gt = _dot_nt(h, wg_ref[...])
        up = _dot_nt(h, wu_ref[...])
        gt_ref[...] = gt.astype(BF16)
        up_ref[...] = up.astype(BF16)
        hd = (gt * _sigmoid(gt) * up).astype(BF16)
        ff = _dot(hd, wd_ref[...])
        f_ref[...] = ff.astype(BF16)
        err = x + _rms_fwd(ff, go_ref[...]) - tgt_ref[...]
        sq_ref[...] += jnp.sum(err * err, axis=0, keepdims=True)

    return pl.pallas_call(
        body, name=name, grid=(t // tm,),
        in_specs=[_tile(tm, d), _whole((1, d)), _resident((f, d)), _resident((f, d)), _resident((f, d)), _whole((1, d)),
                  _tile(tm, d)],
        out_specs=[_tile(tm, d), _tile(tm, f), _tile(tm, f), _tile(tm, d), _whole((1, d))],
        out_shape=[_sds((t, d), BF16), _sds((t, f), BF16), _sds((t, f), BF16), _sds((t, d), BF16), _sds((1, d), F32)],
        compiler_params=_params(),
    )(x2, g_pre, wg_t, wu_t, wd, g_post, target)


def _ffn_bwd(x2, f, target, gt, up, g_pre, wg_t, wu_t, wd, g_post, tm, name):
    t, d = x2.shape
    ff = wg_t.shape[0]

    def body(x_ref, f_ref, tgt_ref, gt_ref, up_ref, gp_ref, wg_ref, wu_ref, wd_ref, go_ref,
             dx_ref, df_ref, hd_ref, dgt_ref, dup_ref, dgo_ref, dgp_ref):
        _zero_at_first_step(dgo_ref, dgp_ref)
        x = x_ref[...]
        fo = f_ref[...].astype(F32)
        dx3 = (x + _rms_fwd(fo, go_ref[...]) - tgt_ref[...]) * (1.0 / d)
        df, dgo = _rms_bwd(dx3, fo, go_ref[...])
        dgo_ref[...] += dgo
        df = df.astype(BF16)
        df_ref[...] = df
        dhd = _dot_nt(df, wd_ref[...])
        gt = gt_ref[...].astype(F32)
        up = up_ref[...].astype(F32)
        sg = _sigmoid(gt)
        si = gt * sg
        hd_ref[...] = (si * up).astype(BF16)
        dup = (dhd * si).astype(BF16)
        dgt = (dhd * up * (sg * (1.0 + gt * (1.0 - sg)))).astype(BF16)
        dup_ref[...] = dup
        dgt_ref[...] = dgt
        dh = _dot(dgt, wg_ref[...]) + _dot(dup, wu_ref[...])
        dxn, dgp = _rms_bwd(dh, x, gp_ref[...])
        dgp_ref[...] += dgp
        dx_ref[...] = dx3 + dxn

    return pl.pallas_call(
        body, name=name, grid=(t // tm,),
        in_specs=[_tile(tm, d), _tile(tm, d), _tile(tm, d), _tile(tm, ff), _tile(tm, ff), _whole((1, d)),
                  _resident((ff, d)), _resident((ff, d)), _resident((ff, d)), _whole((1, d))],
        out_specs=[_tile(tm, d), _tile(tm, d), _tile(tm, ff), _tile(tm, ff), _tile(tm, ff), _whole((1, d)), _whole((1, d))],
        out_shape=[_sds((t, d), F32), _sds((t, d), BF16), _sds((t, ff), BF16), _sds((t, ff), BF16), _sds((t, ff), BF16),
                   _sds((1, d), F32), _sds((1, d), F32)],
        compiler_params=_params(),
    )(x2, f, target, gt, up, g_pre, wg_t, wu_t, wd, g_post)


def _attn_bwd(dx2, a, x1, q, k, v, g_pre, wq, wo, g_post, tm, name, comm):
    t, d = x1.shape
    m = k.shape[0]
    scale = XA_HEAD_DIM ** -0.5

    def core(dx2_ref, a_ref, x_ref, q_ref, k_ref, v_ref, gp_ref, wq_ref, wo_ref, go_ref,
             dx1_ref, da_ref, dq_ref, dk_ref, dv_ref, dgo_ref, dgp_ref, do_buf):
        _zero_at_first_step(dgo_ref, dgp_ref, dk_ref, dv_ref)
        dx2 = dx2_ref[...]
        da, dgo = _rms_bwd(dx2, a_ref[...].astype(F32), go_ref[...])
        dgo_ref[...] += dgo
        da = da.astype(BF16)
        da_ref[...] = da
        do_buf[...] = _dot_nt(da, wo_ref[...]).astype(BF16)
        for hd in range(XA_HEADS):
            cols = slice(hd * XA_HEAD_DIM, (hd + 1) * XA_HEAD_DIM)
            qh = q_ref[:, cols]
            p = _softmax_rows(_dot_nt(qh, k_ref[:, cols]) * scale)
            do_h = do_buf[:, cols]
            dp = _dot_nt(do_h, v_ref[:, cols])
            dv_ref[:, cols] += _dot_tn(p.astype(BF16), do_h)
            ds = (p * (dp - jnp.sum(dp * p, axis=-1, keepdims=True)) * scale).astype(BF16)
            dq_ref[:, cols] = _dot(ds, k_ref[:, cols]).astype(BF16)
            dk_ref[:, cols] += _dot_tn(ds, qh)
        dh = _dot_nt(dq_ref[...], wq_ref[...])
        dxn, dgp = _rms_bwd(dh, x_ref[...], gp_ref[...])
        dgp_ref[...] += dgp
        dx1_ref[...] = dx2 + dxn

    return _hosted_call(
        core, comm, name, t // tm,
        in_specs=[_tile(tm, d), _tile(tm, d), _tile(tm, d), _tile(tm, d), _whole((m, d)), _whole((m, d)), _whole((1, d)),
                  _resident((d, d)), _resident((d, d)), _whole((1, d))],
        out_specs=[_tile(tm, d), _tile(tm, d), _tile(tm, d), _whole((m, d)), _whole((m, d)), _whole((1, d)), _whole((1, d))],
        out_shape=[_sds((t, d), F32), _sds((t, d), BF16), _sds((t, d), BF16), _sds((m, d), F32), _sds((m, d), F32),
                   _sds((1, d), F32), _sds((1, d), F32)],
        scratch_shapes=[pltpu.VMEM((tm, d), BF16)],
        operands=(dx2, a, x1, q, k, v, g_pre, wq, wo, g_post))


def _mem_bwd(mem, mem_n, dk, dv, g, wk, wv, name):
    m, d = mem.shape

    def body(mem_ref, n_ref, dk_ref, dv_ref, g_ref, wk_ref, wv_ref, dwk_ref, dwv_ref, dg_ref):
        dk = dk_ref[...].astype(BF16)
        dv = dv_ref[...].astype(BF16)
        n = n_ref[...]
        dwk_ref[...] = _dot_tn(n, dk).astype(BF16)
        dwv_ref[...] = _dot_tn(n, dv).astype(BF16)
        dn = _dot_nt(dk, wk_ref[...]) + _dot_nt(dv, wv_ref[...])
        _, dg = _rms_bwd(dn, mem_ref[...], g_ref[...])
        dg_ref[...] = dg

    return pl.pallas_call(
        body, name=name, grid=(1,),
        in_specs=[_whole((m, d)), _whole((m, d)), _whole((m, d)), _whole((m, d)), _whole((1, d)), _whole((d, d)),
                  _whole((d, d))],
        out_specs=[_whole((d, d)), _whole((d, d)), _whole((1, d))],
        out_shape=[_sds((d, d), BF16), _sds((d, d), BF16), _sds((1, d), F32)],
        compiler_params=_params(),
    )(mem, mem_n, dk, dv, g, wk, wv)


def _proj_bwd(dxo, s, w, g, tm, name):
    t, d = dxo.shape
    k = w.shape[0]

    def body(dx_ref, s_ref, w_ref, g_ref, ds_ref, da_ref, dg_ref):
        _zero_at_first_step(dg_ref)
        ds, dg = _rms_bwd(dx_ref[...], s_ref[...].astype(F32), g_ref[...])
        dg_ref[...] += dg
        ds = ds.astype(BF16)
        ds_ref[...] = ds
        da_ref[...] = _dot_nt(ds, w_ref[...]).astype(BF16)

    return pl.pallas_call(
        body, name=name, grid=(t // tm,),
        in_specs=[_tile(tm, d), _tile(tm, d), _resident((k, d)), _whole((1, d))],
        out_specs=[_tile(tm, d), _tile(tm, k), _whole((1, d))],
        out_shape=[_sds((t, d), BF16), _sds((t, k), BF16), _sds((1, d), F32)],
        compiler_params=_params(),
    )(dxo, s, w, g)


def _conv_bwd(dy, u, zc, ca, wa, wb, lg, lb, tc, name, comm):
    t = u.shape[0]
    n_tiles = t // tc
    n_chunks = tc // CONV_CHUNK

    def core(dy_ref, u_ref, zc_ref, ca_ref, wa_ref, wb_ref, lg_ref, lb_ref,
             du_ref, dwa_ref, dwb_ref, dbb_ref, dlg_ref, dlb_ref, ebuf, eabuf, zbuf, cvbuf, wacc, aacc, vacc):
        step = pl.program_id(0)

        @pl.when(step == 0)
        def _():
            ebuf[:, tc:tc + HALO_B, :] = jnp.zeros((D_B // LANES, HALO_B, LANES), F32)
            eabuf[:, tc:tc + HALO_A, :] = jnp.zeros((D_A // LANES, HALO_A, LANES), F32)
            wacc[...] = jnp.zeros_like(wacc)
            aacc[...] = jnp.zeros_like(aacc)
            vacc[...] = jnp.zeros_like(vacc)

        dbb = jnp.zeros((SUBLANES, D_B), F32)
        dlg = jnp.zeros((SUBLANES, D_B), F32)
        dlb = jnp.zeros((SUBLANES, D_B), F32)
        for c in range(n_chunks):
            rows = slice(c * CONV_CHUNK, (c + 1) * CONV_CHUNK)
            dy_a = dy_ref[rows, 0:D_A].astype(F32)
            b_a = u_ref[rows, 0:D_A].astype(F32)
            du_ref[rows, 0:D_A] = (dy_a * ca_ref[rows, :].astype(F32)).astype(BF16)
            dca = dy_a * b_a
            cv = u_ref[rows, D_A:2 * D_A].astype(F32) * u_ref[rows, 2 * D_A:3 * D_A].astype(F32)
            glu_g = u_ref[rows, 3 * D_A + D_B:3 * D_A + 2 * D_B].astype(F32)
            z = u_ref[rows, 3 * D_A:3 * D_A + D_B].astype(F32) * _sigmoid(glu_g)
            for lb_i in range(D_A // LANES):
                lanes = slice(lb_i * LANES, (lb_i + 1) * LANES)
                eabuf[lb_i, rows, :] = dca[:, lanes]
                cvbuf[lb_i, rows, :] = cv[:, lanes]
                zbuf[lb_i, rows, :] = z[:, lanes]

            zcv = zc_ref[rows, :].astype(F32)
            mu = jnp.mean(zcv, axis=-1, keepdims=True)
            xc = zcv - mu
            rstd = lax.rsqrt(jnp.mean(xc * xc, axis=-1, keepdims=True) + LN_EPS)
            xhat = xc * rstd
            ln = xhat * lg_ref[...] + lb_ref[...]
            sg = _sigmoid(ln)
            dln = dy_ref[rows, D_A:D_A + D_B].astype(F32) * (sg * (1.0 + ln * (1.0 - sg)))
            dlg = dlg + jnp.sum((dln * xhat).reshape(CONV_CHUNK // SUBLANES, SUBLANES, D_B), axis=0)
            dlb = dlb + jnp.sum(dln.reshape(CONV_CHUNK // SUBLANES, SUBLANES, D_B), axis=0)
            dxh = dln * lg_ref[...]
            dzc = rstd * (dxh - jnp.mean(dxh, axis=-1, keepdims=True) - xhat * jnp.mean(dxh * xhat, axis=-1, keepdims=True))
            dbb = dbb + jnp.sum(dzc.reshape(CONV_CHUNK // SUBLANES, SUBLANES, D_B), axis=0)
            for lb_i in range(D_B // LANES):
                ebuf[lb_i, rows, :] = dzc[:, lb_i * LANES:(lb_i + 1) * LANES]
        vacc[0] += dbb
        vacc[1] += dlg
        vacc[2] += dlb

        for lb_i in range(D_A // LANES):
            lanes = slice(lb_i * LANES, (lb_i + 1) * LANES)

            def cols(first):
                return slice(first + lb_i * LANES, first + (lb_i + 1) * LANES)

            for c in range(n_chunks):
                r0 = c * CONV_CHUNK
                rows = slice(r0, r0 + CONV_CHUNK)
                cv = cvbuf[lb_i, rows, :]
                dcv = jnp.zeros((CONV_CHUNK, LANES), F32)
                for k in range(CONV_A_W):
                    off = r0 + (CONV_A_W - 1) - k
                    e = eabuf[lb_i, off:off + CONV_CHUNK, :]
                    dcv = dcv + wa_ref[k:k + 1, lanes] * e
                    aacc[k, :, lanes] += jnp.sum((cv * e).reshape(CONV_CHUNK // SUBLANES, SUBLANES, LANES), axis=0)
                du_ref[rows, cols(D_A)] = (dcv * u_ref[rows, cols(2 * D_A)].astype(F32)).astype(BF16)
                du_ref[rows, cols(2 * D_A)] = (dcv * u_ref[rows, cols(D_A)].astype(F32)).astype(BF16)

                z = zbuf[lb_i, rows, :]
                dz = jnp.zeros((CONV_CHUNK, LANES), F32)
                for k in range(CONV_B_W):
                    off = r0 + (CONV_B_W - 1) - k
                    e = ebuf[lb_i, off:off + CONV_CHUNK, :]
                    dz = dz + wb_ref[k:k + 1, lanes] * e
                    wacc[k, :, lanes] += jnp.sum((z * e).reshape(CONV_CHUNK // SUBLANES, SUBLANES, LANES), axis=0)
                glu_v = u_ref[rows, cols(3 * D_A)].astype(F32)
                sgg = _sigmoid(u_ref[rows, cols(3 * D_A + D_B)].astype(F32))
                du_ref[rows, cols(3 * D_A)] = (dz * sgg).astype(BF16)
                du_ref[rows, cols(3 * D_A + D_B)] = (dz * glu_v * sgg * (1.0 - sgg)).astype(BF16)

            ebuf[lb_i, tc:tc + HALO_B, :] = ebuf[lb_i, 0:HALO_B, :]
            eabuf[lb_i, tc:tc + HALO_A, :] = eabuf[lb_i, 0:HALO_A, :]

        @pl.when(step == n_tiles - 1)
        def _():
            for k in range(CONV_B_W):
                dwb_ref[k:k + 1, :] = jnp.sum(wacc[k], axis=0, keepdims=True)
            for k in range(CONV_A_W):
                dwa_ref[k:k + 1, :] = jnp.sum(aacc[k], axis=0, keepdims=True)
            dbb_ref[...] = jnp.sum(vacc[0], axis=0, keepdims=True)
            dlg_ref[...] = jnp.sum(vacc[1], axis=0, keepdims=True)
            dlb_ref[...] = jnp.sum(vacc[2], axis=0, keepdims=True)

    return _hosted_call(
        core, comm, name, n_tiles,
        in_specs=[_rtile(tc, D_A + D_B, n_tiles), _rtile(tc, D_IN_ALL, n_tiles), _rtile(tc, D_B, n_tiles),
                  _rtile(tc, D_A, n_tiles), _whole((CONV_A_W, D_A)), _whole((CONV_B_W, D_B)), _whole((1, D_B)),
                  _whole((1, D_B))],
        out_specs=[_rtile(tc, D_IN_ALL, n_tiles), _whole((CONV_A_W, D_A)), _whole((CONV_B_W, D_B)), _whole((1, D_B)),
                   _whole((1, D_B)), _whole((1, D_B))],
        out_shape=[_sds((t, D_IN_ALL), BF16), _sds((CONV_A_W, D_A), F32), _sds((CONV_B_W, D_B), F32), _sds((1, D_B), F32),
                   _sds((1, D_B), F32), _sds((1, D_B), F32)],
        scratch_shapes=[pltpu.VMEM((D_B // LANES, tc + HALO_B, LANES), F32), pltpu.VMEM((D_A // LANES, tc + HALO_A, LANES), F32),
                        pltpu.VMEM((D_B // LANES, tc, LANES), F32), pltpu.VMEM((D_A // LANES, tc, LANES), F32),
                        pltpu.VMEM((CONV_B_W, SUBLANES, D_B), F32), pltpu.VMEM((CONV_A_W, SUBLANES, D_A), F32),
                        pltpu.VMEM((3, SUBLANES, D_B), F32)],
        operands=(dy, u, zc, ca, wa, wb, lg, lb))


def _in_bwd(du, w_t, x, dx1, g, tm, name):
    t, d = x.shape
    n = w_t.shape[0]

    def body(du_ref, w_ref, x_ref, dx1_ref, g_ref, dx_ref, dg_ref):
        _zero_at_first_step(dg_ref)
        dh = _dot(du_ref[...], w_ref[...])
        dxn, dg = _rms_bwd(dh, x_ref[...], g_ref[...])
        dg_ref[...] += dg
        dx_ref[...] = dx1_ref[...] + dxn

    return pl.pallas_call(
        body, name=name, grid=(t // tm,),
        in_specs=[_tile(tm, n), _resident((n, d)), _tile(tm, d), _tile(tm, d), _whole((1, d))],
        out_specs=[_tile(tm, d), _whole((1, d))],
        out_shape=[_sds((t, d), F32), _sds((1, d), F32)],
        compiler_params=_params(),
    )(du, w_t, x, dx1, g)


def _wgrad(a, b, tk, bm, bn, name, comm=None, cols=None):
    t, m = a.shape
    first, n = (0, b.shape[1]) if cols is None else cols
    first_block = first // bn
    n_k = t // tk

    def body(a_ref, b_ref, o_ref, acc):
        @pl.when(pl.program_id(2) == 0)
        def _():
            acc[...] = jnp.zeros_like(acc)

        acc[...] += _dot_tn(a_ref[...], b_ref[...])

        @pl.when(pl.program_id(2) == n_k - 1)
        def _():
            o_ref[...] = acc[...].astype(BF16)

    call = dict(
        grid=(m // bm, n // bn, n_k),
        in_specs=[pl.BlockSpec((tk, bm), lambda i, j, k: (k, i)),
                  pl.BlockSpec((tk, bn), lambda i, j, k: (k, first_block + j))],
        out_specs=[pl.BlockSpec((bm, bn), lambda i, j, k: (i, j))],
        out_shape=[_sds((m, n), BF16)],
        scratch_shapes=[pltpu.VMEM((bm, bn), F32)])
    if comm is None:
        return pl.pallas_call(body, name=name, compiler_params=_params(3), **call)(a, b)[0]
    (out,), received = _hosted_call(body, comm, name, operands=(a, b), **call)
    return out, received


def _adamw(parts, w, m, v, tr, name):
    n_parts, r, c = parts.shape

    def body(p_ref, w_ref, m_ref, v_ref, g_ref, d_ref, nm_ref, nv_ref):
        g = p_ref[0].astype(F32)
        for j in range(1, n_parts):
            g = g + p_ref[j].astype(F32)
        g_ref[...] = g
        nm = ADAM_B1 * m_ref[...] + (1.0 - ADAM_B1) * g
        nv = ADAM_B2 * v_ref[...] + (1.0 - ADAM_B2) * (g * g)
        nm_ref[...] = nm
        nv_ref[...] = nv
        m_hat = nm / (1.0 - ADAM_B1 ** ADAM_STEP)
        v_hat = nv / (1.0 - ADAM_B2 ** ADAM_STEP)
        d_ref[...] = -ADAM_LR * (m_hat / (jnp.sqrt(v_hat) + ADAM_EPS) + ADAM_WD * w_ref[...])

    return pl.pallas_call(
        body, name=name, grid=(r // tr,),
        in_specs=[pl.BlockSpec((n_parts, tr, c), lambda i: (0, i, 0))] + [_tile(tr, c)] * 3,
        out_specs=[_tile(tr, c)] * 4,
        out_shape=[_sds((r, c), F32)] * 4,
        compiler_params=_params(),
    )(parts, w, m, v)


def _sum_parts(parts, name):
    n_parts, r, c = parts.shape

    def body(p_ref, o_ref):
        acc = p_ref[0]
        for j in range(1, n_parts):
            acc = acc + p_ref[j]
        o_ref[...] = acc

    return pl.pallas_call(
        body, name=name, grid=(1,),
        in_specs=[_whole((n_parts, r, c))], out_specs=_whole((r, c)), out_shape=_sds((r, c), F32),
        compiler_params=_params(),
    )(parts)


def _row(v):
    return v.reshape(1, -1)


def _by_owner_rows(g):
    return g.reshape(N_DEV, g.shape[0] // N_DEV, g.shape[1])


WEIGHTS = ("mix_pre_g", "w_mix_in", "conv_a_w", "conv_b_w", "conv_b_b", "ln_b_g", "ln_b_b", "w_mix_out", "mix_post_g",
           "xa_pre_g", "mem_norm_g", "w_q", "w_k", "w_v", "w_o", "xa_post_g", "ffn_pre_g", "w_gate", "w_up", "w_down",
           "ffn_post_g")
LARGE = ("w_mix_in", "w_mix_out", "w_q", "w_k", "w_v", "w_o", "w_gate", "w_up", "w_down")
COLUMN_SHARDED = ("w_mix_in", "w_gate", "w_up")
GAINS = ("mix_pre_g", "mix_post_g", "xa_pre_g", "mem_norm_g", "xa_post_g", "ffn_pre_g", "ffn_post_g")
CHANNEL_VECTORS = ("conv_b_b", "ln_b_g", "ln_b_b")
CONV_TAPS = ("conv_a_w", "conv_b_w")
SMALL = GAINS + CHANNEL_VECTORS + CONV_TAPS
CONV_COLS_PER_DEVICE = D_A // N_DEV
TOKEN_TILE = 512
FFN_TOKEN_TILE = 256
CONV_TOKEN_TILE = 256
WGRAD_TOKEN_TILE = 2048
ADAM_ROWS_PER_STEP = 256


def _lane_rows(v):
    flat = v.reshape(-1)
    tile = SUBLANES * LANES
    flat = jnp.pad(flat, (0, (-flat.shape[0]) % tile))
    return flat.reshape(-1, LANES)


def _pack_small(values, names):
    return jnp.concatenate([_lane_rows(values[n]) for n in names], axis=0)


def _unpack_small(packed, like, names):
    out, off = {}, 0
    for n in names:
        size = like[n].size
        rows = _lane_rows(like[n]).shape[0]
        out[n] = packed[off:off + rows, :].reshape(-1)[:size].reshape(like[n].shape)
        off += rows
    return out


def kernel(x, mem, mix_pre_g, w_mix_in, conv_a_w, conv_b_w, conv_b_b, ln_b_g, ln_b_b, w_mix_out, mix_post_g, xa_pre_g, mem_norm_g, w_q, w_k, w_v, w_o, xa_post_g, ffn_pre_g, w_gate, w_up, w_down, ffn_post_g, loss_target, m_mix_pre_g, m_w_mix_in, m_conv_a_w, m_conv_b_w, m_conv_b_b, m_ln_b_g, m_ln_b_b, m_w_mix_out, m_mix_post_g, m_xa_pre_g, m_mem_norm_g, m_w_q, m_w_k, m_w_v, m_w_o, m_xa_post_g, m_ffn_pre_g, m_w_gate, m_w_up, m_w_down, m_ffn_post_g, v_mix_pre_g, v_w_mix_in, v_conv_a_w, v_conv_b_w, v_conv_b_b, v_ln_b_g, v_ln_b_b, v_w_mix_out, v_mix_post_g, v_xa_pre_g, v_mem_norm_g, v_w_q, v_w_k, v_w_v, v_w_o, v_xa_post_g, v_ffn_pre_g, v_w_gate, v_w_up, v_w_down, v_ffn_post_g):
    given = dict(locals())
    w = {n: given[n] for n in WEIGHTS}
    m = {n: given["m_" + n] for n in WEIGHTS}
    v = {n: given["v_" + n] for n in WEIGHTS}
    xs, mems, target = x[0], mem[0], loss_target[0]
    t = xs.shape[0]
    tm, tm_ffn, tc, tk = min(TOKEN_TILE, t), min(FFN_TOKEN_TILE, t), min(CONV_TOKEN_TILE, t), min(WGRAD_TOKEN_TILE, t)
    g = {n: _row(w[n]) for n in GAINS}
    bb, lg, lb = (_row(w[n]) for n in CHANNEL_VECTORS)

    def shard_bf16(*names):
        return [w[n].astype(BF16) for n in names]

    def shard_bf16_t(*names):
        return [w[n].T.astype(BF16) for n in names]

    g_mix_in, g_taps = _comm_call(_Gather(shard_bf16_t("w_mix_in") + [_pack_small(w, CONV_TAPS)]), "gather_mixer")
    w_mix_in_t = g_mix_in.reshape(D_IN_ALL, D_MODEL)
    taps, off = {}, 0
    for n in CONV_TAPS:
        k, cols = w[n].shape
        rows = _lane_rows(w[n]).shape[0]
        blk = g_taps[:, off:off + rows, :].reshape(N_DEV, -1)[:, :k * cols].reshape(N_DEV, k, cols)
        taps[n] = blk.transpose(1, 0, 2).reshape(k, N_DEV * cols)
        off += rows
    wa, wb = taps["conv_a_w"], taps["conv_b_w"]

    (h1, u), gathered = _norm_matmul(xs, g["mix_pre_g"], w_mix_in_t.T, tm, "mix_in_fwd",
                                     _Gather(shard_bf16("w_mix_out", "w_q", "w_k", "w_v", "w_o")))
    w_mix_out, w_q, w_k, w_v, w_o = (a.reshape(D_MODEL, D_MODEL) for a in gathered)
    (ycat, zc, ca), (g_gate, g_up, g_down) = _conv_fwd(u, wa, wb, bb, lg, lb, tc, "conv_fwd",
                                                       _Gather(shard_bf16_t("w_gate", "w_up") + shard_bf16("w_down")))
    w_gate_t, w_up_t, w_down = (a.reshape(D_FF, D_MODEL) for a in (g_gate, g_up, g_down))
    mem_n, kk, vv = _mem_fwd(mems, g["mem_norm_g"], w_k, w_v, "mem_fwd")
    x1, s1 = _proj_norm_res(ycat, w_mix_out, xs, g["mix_post_g"], tm, "mix_out_fwd")
    x2, h2, q, o, a = _attn_fwd(x1, g["xa_pre_g"], w_q, kk, vv, w_o, g["xa_post_g"], tm, "attn_fwd")
    h3, gt, up, f, sq = _ffn_fwd(x2, g["ffn_pre_g"], w_gate_t, w_up_t, w_down, g["ffn_post_g"], target, tm_ffn, "ffn_fwd")

    dx2, df, hd, dgt, dup, d_ffn_post, d_ffn_pre = _ffn_bwd(
        x2, f, target, gt, up, g["ffn_pre_g"], w_gate_t, w_up_t, w_down, g["ffn_post_g"], tm_ffn, "ffn_bwd")
    d_w_down = _wgrad(hd, df, tk, D_FF // 2, D_MODEL, "wgrad_down")
    d_w_gate_t = _wgrad(dgt, h3, tk, D_FF // 2, D_MODEL, "wgrad_gate")
    d_w_up_t = _wgrad(dup, h3, tk, D_FF // 2, D_MODEL, "wgrad_up")
    ffn_slabs = [_by_owner_rows(d) for d in (d_w_gate_t, d_w_up_t, d_w_down)]

    (dx1, da, dq, dk, dv, d_xa_post, d_xa_pre), from_ffn = _attn_bwd(
        dx2, a, x1, q, kk, vv, g["xa_pre_g"], w_q, w_o, g["xa_post_g"], tm, "attn_bwd",
        _Exchange(ffn_slabs, [True] * 3))
    d_w_o = _wgrad(o, da, tk, D_MODEL, D_MODEL, "wgrad_o")
    d_w_q = _wgrad(h2, dq, tk, D_MODEL, D_MODEL, "wgrad_q")
    d_w_k, d_w_v, d_mem_norm = _mem_bwd(mems, mem_n, dk, dv, g["mem_norm_g"], w_k, w_v, "mem_bwd")
    ds1, dycat, d_mix_post = _proj_bwd(dx1, s1, w_mix_out, g["mix_post_g"], tm, "mix_out_bwd")
    d_w_mix_out = _wgrad(ycat, ds1, tk, D_MODEL, D_MODEL, "wgrad_mix_out")
    attn_slabs = [_by_owner_rows(d) for d in (d_w_mix_out, d_w_q, d_w_k, d_w_v, d_w_o)]

    (du, d_conv_a, d_conv_b, d_conv_bb, d_ln_g, d_ln_b), from_attn = _conv_bwd(
        dycat, u, zc, ca, wa, wb, lg, lb, tc, "conv_bwd", _Exchange(attn_slabs, [True] * 5))
    half = D_MODEL // 2
    d_in_lo = _wgrad(du, h1, tk, D_IN_ALL // 2, half, "wgrad_mix_in_lo", cols=(0, half))
    d_in_hi, (from_in_lo,) = _wgrad(du, h1, tk, D_IN_ALL // 2, half, "wgrad_mix_in_hi",
                                    _Exchange([_by_owner_rows(d_in_lo)], [True]), cols=(half, half))
    dx, d_mix_pre = _in_bwd(du, w_mix_in_t, xs, dx1, g["mix_pre_g"], tm, "mix_in_bwd")

    small_grads = dict(mix_pre_g=d_mix_pre, conv_a_w=d_conv_a, conv_b_w=d_conv_b, conv_b_b=d_conv_bb, ln_b_g=d_ln_g,
                       ln_b_b=d_ln_b, mix_post_g=d_mix_post, xa_pre_g=d_xa_pre, mem_norm_g=d_mem_norm, xa_post_g=d_xa_post,
                       ffn_pre_g=d_ffn_pre, ffn_post_g=d_ffn_post, loss=sq)
    names = SMALL + ("loss",)
    from_in_hi, all_small = _comm_call(
        _Exchange([_by_owner_rows(d_in_hi), _pack_small(small_grads, names)], [True, False]), "reduce_tail")

    received = dict(zip(("w_gate", "w_up", "w_down"), from_ffn))
    received.update(zip(("w_mix_out", "w_q", "w_k", "w_v", "w_o"), from_attn))
    received["w_mix_in"] = jnp.concatenate([from_in_lo, from_in_hi], axis=2)
    grad, delta, new_m, new_v = {}, {}, {}, {}
    for n in LARGE:
        operands = [a[n].T if n in COLUMN_SHARDED else a[n] for a in (w, m, v)]
        rows = operands[0].shape[0]
        tr = ADAM_ROWS_PER_STEP if rows % ADAM_ROWS_PER_STEP == 0 else rows
        results = _adamw(received[n], *operands, tr, "adamw_" + n)
        grad[n], delta[n], new_m[n], new_v[n] = [r.T if n in COLUMN_SHARDED else r for r in results]

    total = _unpack_small(_sum_parts(all_small, "sum_small"), small_grads, names)
    loss = jnp.sum(total.pop("loss")) * (0.5 / D_MODEL)
    first_col = _device_index() * CONV_COLS_PER_DEVICE
    for n in CONV_TAPS:
        total[n] = lax.dynamic_slice_in_dim(total[n], first_col, CONV_COLS_PER_DEVICE, axis=1)
    total = {n: total[n].reshape(w[n].shape) for n in SMALL}
    packed_small = [_pack_small(values, SMALL) for values in (total, w, m, v)]
    g_s, d_s, nm_s, nv_s = _adamw(packed_small[0][None], *packed_small[1:], packed_small[0].shape[0], "adamw_small")
    for out, p in ((grad, g_s), (delta, d_s), (new_m, nm_s), (new_v, nv_s)):
        out.update(_unpack_small(p, w, SMALL))

    return (loss, dx[None], *[grad[n] for n in WEIGHTS], *[delta[n] for n in WEIGHTS], *[new_m[n] for n in WEIGHTS],
            *[new_v[n] for n in WEIGHTS])
```

```python
import functools

import jax
import jax.numpy as jnp
from jax import lax
from jax.experimental import pallas as pl
from jax.experimental.pallas import tpu as pltpu

F32 = jnp.float32
BF16 = jnp.bfloat16

D_MODEL = 1024
D_A = 512
D_B = 512
D_IN_ALL = 2560
CONV_A_W = 3
CONV_B_W = 31
XA_HEADS = 4
XA_HEAD_DIM = 256
D_FF = 2816
N_DEV = 8
RMS_EPS = 1e-6
LN_EPS = 1e-5
ADAM_LR = 0.001
ADAM_B1 = 0.9
ADAM_B2 = 0.999
ADAM_EPS = 1e-08
ADAM_WD = 0.01
ADAM_STEP = 10

VMEM_LIMIT_BYTES = 56 * 1024 * 1024
SUBLANES = 8
LANES = 128
HALO_B = 32
HALO_A = 8
CONV_CHUNK = 32

MESH = pl.DeviceIdType.MESH


def _params(n_grid_axes=1):
    return pltpu.CompilerParams(dimension_semantics=("arbitrary",) * n_grid_axes, vmem_limit_bytes=VMEM_LIMIT_BYTES)


def _sds(shape, dtype):
    return jax.ShapeDtypeStruct(shape, dtype)


def _tile(rows, cols):
    return pl.BlockSpec((rows, cols), lambda i: (i, 0))


def _rtile(rows, cols, n):
    return pl.BlockSpec((rows, cols), lambda i: (n - 1 - i, 0))


def _whole(shape):
    zeros = (0,) * len(shape)
    return pl.BlockSpec(shape, lambda i: zeros)


def _resident(shape):
    zeros = (0,) * len(shape)
    return pl.BlockSpec(shape, lambda i: zeros, pipeline_mode=pl.Buffered(1))


def _dot(a, b):
    return jnp.dot(a, b, preferred_element_type=F32)


def _dot_nt(a, b):
    return lax.dot_general(a, b, (((1,), (1,)), ((), ())), preferred_element_type=F32)


def _dot_tn(a, b):
    return lax.dot_general(a, b, (((0,), (0,)), ((), ())), preferred_element_type=F32)


def _sigmoid(x):
    return 1.0 / (1.0 + jnp.exp(-x))


def _rms_fwd(x, g):
    r = lax.rsqrt(jnp.mean(x * x, axis=-1, keepdims=True) + RMS_EPS)
    return x * r * g


def _rms_bwd(dy, xin, g):
    r = lax.rsqrt(jnp.mean(xin * xin, axis=-1, keepdims=True) + RMS_EPS)
    n = xin * r
    dg = jnp.sum(dy * n, axis=0, keepdims=True)
    dn = dy * g
    dx = r * (dn - n * jnp.mean(dn * n, axis=-1, keepdims=True))
    return dx, dg


def _zero_at_first_step(*refs):
    @pl.when(pl.program_id(0) == 0)
    def _():
        for ref in refs:
            ref[...] = jnp.zeros(ref.shape, ref.dtype)


def _place():
    return lax.axis_index("x"), lax.axis_index("y"), lax.axis_index("c")


def _device_index():
    x, y, c = _place()
    return 4 * x + 2 * y + c


class _Gather:
    def __init__(self, arrays):
        self.arrays = list(arrays)
        self.out_shape = [_sds((N_DEV, *a.shape), a.dtype) for a in self.arrays]
        n = len(self.arrays)
        self.scratch_shapes = [pltpu.SemaphoreType.DMA((n, 7)), pltpu.SemaphoreType.DMA((n, 7)),
                               pltpu.SemaphoreType.DMA((n,))]

    def forward_step(self, n_steps):
        return max(0, n_steps - 9)

    def bind(self, srcs, dsts, send_sems, recv_sems, local_sems):
        x, y, cc = _place()
        me, sibling = (x, y, cc), (x, y, 1 - cc)
        chips = [(1 - x, y), (x, 1 - y), (1 - x, 1 - y)]

        def copy(a, k, owner, to, src=None):
            slot = dsts[a].at[4 * owner[0] + 2 * owner[1] + owner[2]]
            return pltpu.make_async_remote_copy(
                src_ref=slot if src is None else src, dst_ref=slot, send_sem=send_sems.at[a, k],
                recv_sem=recv_sems.at[a, k], device_id=to, device_id_type=MESH)

        def first(a):
            return [copy(a, 0, me, sibling, src=srcs[a])] + [
                copy(a, 1 + j, me, (*chip, cc), src=srcs[a]) for j, chip in enumerate(chips)]

        def passed(a, j):
            return copy(a, 4 + j, (*chips[j], cc), sibling)

        def mine(a):
            return pltpu.make_async_copy(srcs[a], dsts[a].at[4 * x + 2 * y + cc], local_sems.at[a])

        def start():
            for a in range(len(srcs)):
                mine(a).start()
                for cp in first(a):
                    cp.start()

        def forward():
            for a in range(len(srcs)):
                for j, chip in enumerate(chips):
                    copy(a, 1 + j, (*chip, cc), me).wait_recv()
                    passed(a, j).start()

        def finish():
            for a in range(len(srcs)):
                copy(a, 0, sibling, me).wait_recv()
                for j, chip in enumerate(chips):
                    copy(a, 4 + j, (*chip, 1 - cc), me).wait_recv()
                for cp in first(a) + [passed(a, j) for j in range(len(chips))]:
                    cp.wait_send()
                mine(a).wait()

        return start, forward, finish


class _Exchange:
    def __init__(self, arrays, scatter):
        self.arrays = list(arrays)
        self.scatter = list(scatter)
        self.out_shape = [_sds(a.shape if s else (N_DEV, *a.shape), a.dtype) for a, s in zip(self.arrays, self.scatter)]
        n = len(self.arrays)
        self.scratch_shapes = [pltpu.SemaphoreType.DMA((n, 7)), pltpu.SemaphoreType.DMA((n, 7)),
                               pltpu.SemaphoreType.DMA((n,))]

    def forward_step(self, n_steps):
        return n_steps - 1

    def bind(self, srcs, dsts, send_sems, recv_sems, local_sems):
        me = _device_index()

        def copies(a):
            out = []
            for k in range(1, N_DEV):
                p = me ^ k
                out.append(pltpu.make_async_remote_copy(
                    src_ref=srcs[a].at[p] if self.scatter[a] else srcs[a], dst_ref=dsts[a].at[me],
                    send_sem=send_sems.at[a, k - 1], recv_sem=recv_sems.at[a, k - 1],
                    device_id=(p >> 2, (p >> 1) & 1, p & 1), device_id_type=MESH))
            return out

        def mine(a):
            return pltpu.make_async_copy(srcs[a].at[me] if self.scatter[a] else srcs[a], dsts[a].at[me], local_sems.at[a])

        def start():
            for a in range(len(srcs)):
                mine(a).start()
                for cp in copies(a):
                    cp.start()

        def forward():
            pass

        def finish():
            for a in range(len(srcs)):
                for cp in copies(a):
                    cp.wait()
                mine(a).wait()

        return start, forward, finish


def _hosted_call(core, comm, name, grid, in_specs, out_specs, out_shape, scratch_shapes, operands):
    grid = (grid,) if isinstance(grid, int) else tuple(grid)
    n_steps = 1
    for extent in grid:
        n_steps *= extent
    n_in, n_out, n_scr, n_arr = len(in_specs), len(out_specs), len(scratch_shapes), len(comm.arrays)
    any_spec = pl.BlockSpec(memory_space=pl.ANY)

    def body(*refs):
        ins, refs = refs[:n_in], refs[n_in:]
        srcs, refs = refs[:n_arr], refs[n_arr:]
        outs, refs = refs[:n_out], refs[n_out:]
        dsts, refs = refs[:n_arr], refs[n_arr:]
        scratch, sems = refs[:n_scr], refs[n_scr:]
        start, forward, finish = comm.bind(srcs, dsts, *sems)
        step = pl.program_id(0)
        for axis in range(1, len(grid)):
            step = step * grid[axis] + pl.program_id(axis)
        pl.when(step == 0)(start)
        core(*ins, *outs, *scratch)
        pl.when(step == comm.forward_step(n_steps))(forward)
        pl.when(step == n_steps - 1)(finish)

    results = pl.pallas_call(
        body, name=name, grid=grid,
        in_specs=list(in_specs) + [any_spec] * n_arr,
        out_specs=list(out_specs) + [any_spec] * n_arr,
        out_shape=list(out_shape) + comm.out_shape,
        scratch_shapes=list(scratch_shapes) + comm.scratch_shapes,
        compiler_params=_params(len(grid)),
    )(*operands, *comm.arrays)
    return results[:n_out], results[n_out:]


def _comm_call(comm, name):
    return _hosted_call(lambda: None, comm, name, 1, [], [], [], [], [])[1]


def _norm_matmul(x, g, w, tm, name, comm):
    t, d = x.shape
    n = w.shape[1]

    def core(x_ref, g_ref, w_ref, h_ref, o_ref):
        h = _rms_fwd(x_ref[...], g_ref[...]).astype(BF16)
        h_ref[...] = h
        o_ref[...] = _dot(h, w_ref[...]).astype(BF16)

    return _hosted_call(
        core, comm, name, t // tm,
        in_specs=[_tile(tm, d), _whole((1, d)), _resident((d, n))],
        out_specs=[_tile(tm, d), _tile(tm, n)],
        out_shape=[_sds((t, d), BF16), _sds((t, n), BF16)],
        scratch_shapes=[], operands=(x, g, w))


def _conv_fwd(u, wa, wb, bb, lg, lb, xres, w_out, g_post, tc, name, comm):
    t = u.shape[0]
    d = w_out.shape[1]
    n_chunks = tc // CONV_CHUNK
    piece = 2 * LANES

    def core(u_ref, wa_ref, wb_ref, bb_ref, lg_ref, lb_ref, x_ref, wo_ref, go_ref,
             y_ref, zc_ref, ca_ref, x1_ref, s_ref, zbuf, cvbuf, zcbuf):
        @pl.when(pl.program_id(0) == 0)
        def _():
            zbuf[:, 0:HALO_B, :] = jnp.zeros((D_B // LANES, HALO_B, LANES), F32)
            cvbuf[:, 0:HALO_A, :] = jnp.zeros((D_A // LANES, HALO_A, LANES), F32)

        s = None
        for lb_i in range(D_A // LANES):
            if lb_i > 0 and lb_i % 2 == 0:
                cols = slice((lb_i - 2) * LANES, lb_i * LANES)
                part = _dot(y_ref[:, cols], wo_ref[cols, :])
                s = part if s is None else s + part
            lanes = slice(lb_i * LANES, (lb_i + 1) * LANES)
            c_a = u_ref[:, D_A + lb_i * LANES:D_A + (lb_i + 1) * LANES].astype(F32)
            v_a = u_ref[:, 2 * D_A + lb_i * LANES:2 * D_A + (lb_i + 1) * LANES].astype(F32)
            cvbuf[lb_i, HALO_A:HALO_A + tc, :] = c_a * v_a
            glu_v = u_ref[:, 3 * D_A + lb_i * LANES:3 * D_A + (lb_i + 1) * LANES].astype(F32)
            glu_g = u_ref[:, 3 * D_A + D_B + lb_i * LANES:3 * D_A + D_B + (lb_i + 1) * LANES].astype(F32)
            zbuf[lb_i, HALO_B:HALO_B + tc, :] = glu_v * _sigmoid(glu_g)

            for c in range(n_chunks):
                r0 = c * CONV_CHUNK
                rows = slice(r0, r0 + CONV_CHUNK)
                acc = jnp.zeros((CONV_CHUNK, LANES), F32)
                for k in range(CONV_A_W):
                    off = r0 + HALO_A - (CONV_A_W - 1) + k
                    acc = acc + wa_ref[k:k + 1, lanes] * cvbuf[lb_i, off:off + CONV_CHUNK, :]
                ca_ref[rows, lanes] = acc.astype(BF16)
                y_ref[rows, lanes] = (u_ref[rows, lanes].astype(F32) * acc).astype(BF16)

                accb = jnp.zeros((CONV_CHUNK, LANES), F32)
                for k in range(CONV_B_W):
                    off = r0 + HALO_B - (CONV_B_W - 1) + k
                    accb = accb + wb_ref[k:k + 1, lanes] * zbuf[lb_i, off:off + CONV_CHUNK, :]
                zcbuf[rows, lanes] = accb + bb_ref[:, lanes]

            zbuf[lb_i, 0:HALO_B, :] = zbuf[lb_i, tc:tc + HALO_B, :]
            cvbuf[lb_i, 0:HALO_A, :] = cvbuf[lb_i, tc:tc + HALO_A, :]

        cols = slice(D_A - piece, D_A)
        s = s + _dot(y_ref[:, cols], wo_ref[cols, :])

        for c in range(n_chunks):
            rows = slice(c * CONV_CHUNK, (c + 1) * CONV_CHUNK)
            zc = zcbuf[rows, :]
            zc_ref[rows, :] = zc.astype(BF16)
            mu = jnp.mean(zc, axis=-1, keepdims=True)
            xc = zc - mu
            var = jnp.mean(xc * xc, axis=-1, keepdims=True)
            ln = xc * lax.rsqrt(var + LN_EPS) * lg_ref[...] + lb_ref[...]
            y_ref[rows, D_A:D_A + D_B] = (ln * _sigmoid(ln)).astype(BF16)

        s = s + _dot(y_ref[:, D_A:D_A + D_B], wo_ref[D_A:D_A + D_B, :])
        s_ref[...] = s.astype(BF16)
        x1_ref[...] = x_ref[...] + _rms_fwd(s, go_ref[...])

    return _hosted_call(
        core, comm, name, t // tc,
        in_specs=[_tile(tc, D_IN_ALL), _whole((CONV_A_W, D_A)), _whole((CONV_B_W, D_B)), _whole((1, D_B)),
                  _whole((1, D_B)), _whole((1, D_B)), _tile(tc, d), _resident((D_A + D_B, d)), _whole((1, d))],
        out_specs=[_tile(tc, D_A + D_B), _tile(tc, D_B), _tile(tc, D_A), _tile(tc, d), _tile(tc, d)],
        out_shape=[_sds((t, D_A + D_B), BF16), _sds((t, D_B), BF16), _sds((t, D_A), BF16), _sds((t, d), F32),
                   _sds((t, d), BF16)],
        scratch_shapes=[pltpu.VMEM((D_B // LANES, HALO_B + tc, LANES), F32),
                        pltpu.VMEM((D_A // LANES, HALO_A + tc, LANES), F32), pltpu.VMEM((tc, D_B), F32)],
        operands=(u, wa, wb, bb, lg, lb, xres, w_out, g_post))


def _mem_fwd(mem, g, wk, wv, name):
    m, d = mem.shape

    def body(mem_ref, g_ref, wk_ref, wv_ref, n_ref, k_ref, v_ref):
        n = _rms_fwd(mem_ref[...], g_ref[...]).astype(BF16)
        n_ref[...] = n
        k_ref[...] = _dot(n, wk_ref[...]).astype(BF16)
        v_ref[...] = _dot(n, wv_ref[...]).astype(BF16)

    return pl.pallas_call(
        body, name=name, grid=(1,),
        in_specs=[_whole((m, d)), _whole((1, d)), _whole((d, d)), _whole((d, d))],
        out_specs=[_whole((m, d))] * 3,
        out_shape=[_sds((m, d), BF16)] * 3,
        compiler_params=_params(),
    )(mem, g, wk, wv)


def _softmax_rows(s):
    e = jnp.exp(s - jnp.max(s, axis=-1, keepdims=True))
    return e / jnp.sum(e, axis=-1, keepdims=True)


def _attn_fwd(x1, g_pre, wq, k, v, wo, g_post, tm, name):
    t, d = x1.shape
    m = k.shape[0]
    scale = XA_HEAD_DIM ** -0.5

    def body(x_ref, gp_ref, wq_ref, k_ref, v_ref, wo_ref, go_ref, x2_ref, h_ref, q_ref, o_ref, a_ref):
        x = x_ref[...]
        h = _rms_fwd(x, gp_ref[...]).astype(BF16)
        h_ref[...] = h
        q_ref[...] = _dot(h, wq_ref[...]).astype(BF16)
        for hd in range(XA_HEADS):
            cols = slice(hd * XA_HEAD_DIM, (hd + 1) * XA_HEAD_DIM)
            p = _softmax_rows(_dot_nt(q_ref[:, cols], k_ref[:, cols]) * scale)
            o_ref[:, cols] = _dot(p.astype(BF16), v_ref[:, cols]).astype(BF16)
        a = _dot(o_ref[...], wo_ref[...])
        a_ref[...] = a.astype(BF16)
        x2_ref[...] = x + _rms_fwd(a, go_ref[...])

    return pl.pallas_call(
        body, name=name, grid=(t // tm,),
        in_specs=[_tile(tm, d), _whole((1, d)), _resident((d, d)), _whole((m, d)), _whole((m, d)), _resident((d, d)),
                  _whole((1, d))],
        out_specs=[_tile(tm, d)] * 5,
        out_shape=[_sds((t, d), F32)] + [_sds((t, d), BF16)] * 4,
        compiler_params=_params(),
    )(x1, g_pre, wq, k, v, wo, g_post)


def _ffn_fwd(x2, g_pre, wg_t, wu_t, wd, g_post, target, tm, name):
    t, d = x2.shape
    f = wg_t.shape[0]

    def body(x_ref, gp_ref, wg_ref, wu_ref, wd_ref, go_ref, tgt_ref, h_ref, gt_ref, up_ref, f_ref, sq_ref):
        _zero_at_first_step(sq_ref)
        x = x_ref[...]
        h = _rms_fwd(x, gp_ref[...]).astype(BF16)
        h_ref[...] = h
        gt = _dot_nt(h, wg_ref[...])
        up = _dot_nt(h, wu_ref[...])
        gt_ref[...] = gt.astype(BF16)
        up_ref[...] = up.astype(BF16)
        hd = (gt * _sigmoid(gt) * up).astype(BF16)
        ff = _dot(hd, wd_ref[...])
        f_ref[...] = ff.astype(BF16)
        err = x + _rms_fwd(ff, go_ref[...]) - tgt_ref[...]
        sq_ref[...] += jnp.sum(err * err, axis=0, keepdims=True)

    return pl.pallas_call(
        body, name=name, grid=(t // tm,),
        in_specs=[_tile(tm, d), _whole((1, d)), _resident((f, d)), _resident((f, d)), _resident((f, d)), _whole((1, d)),
                  _tile(tm, d)],
        out_specs=[_tile(tm, d), _tile(tm, f), _tile(tm, f), _tile(tm, d), _whole((1, d))],
        out_shape=[_sds((t, d), BF16), _sds((t, f), BF16), _sds((t, f), BF16), _sds((t, d), BF16), _sds((1, d), F32)],
        compiler_params=_params(),
    )(x2, g_pre, wg_t, wu_t, wd, g_post, target)


def _ffn_bwd(x2, f, target, gt, up, g_pre, wg_t, wu_t, wd, g_post, tm, name):
    t, d = x2.shape
    ff = wg_t.shape[0]

    def body(x_ref, f_ref, tgt_ref, gt_ref, up_ref, gp_ref, wg_ref, wu_ref, wd_ref, go_ref,
             dx_ref, df_ref, hd_ref, dgt_ref, dup_ref, dgo_ref, dgp_ref):
        _zero_at_first_step(dgo_ref, dgp_ref)
        x = x_ref[...]
        fo = f_ref[...].astype(F32)
        dx3 = (x + _rms_fwd(fo, go_ref[...]) - tgt_ref[...]) * (1.0 / d)
        df, dgo = _rms_bwd(dx3, fo, go_ref[...])
        dgo_ref[...] += dgo
        df = df.astype(BF16)
        df_ref[...] = df
        dhd = _dot_nt(df, wd_ref[...])
        gt = gt_ref[...].astype(F32)
        up = up_ref[...].astype(F32)
        sg = _sigmoid(gt)
        si = gt * sg
        hd_ref[...] = (si * up).astype(BF16)
        dup = (dhd * si).astype(BF16)
        dgt = (dhd * up * (sg * (1.0 + gt * (1.0 - sg)))).astype(BF16)
        dup_ref[...] = dup
        dgt_ref[...] = dgt
        dh = _dot(dgt, wg_ref[...]) + _dot(dup, wu_ref[...])
        dxn, dgp = _rms_bwd(dh, x, gp_ref[...])
        dgp_ref[...] += dgp
        dx_ref[...] = dx3 + dxn

    return pl.pallas_call(
        body, name=name, grid=(t // tm,),
        in_specs=[_tile(tm, d), _tile(tm, d), _tile(tm, d), _tile(tm, ff), _tile(tm, ff), _whole((1, d)),
                  _resident((ff, d)), _resident((ff, d)), _resident((ff, d)), _whole((1, d))],
        out_specs=[_tile(tm, d), _tile(tm, d), _tile(tm, ff), _tile(tm, ff), _tile(tm, ff), _whole((1, d)), _whole((1, d))],
        out_shape=[_sds((t, d), F32), _sds((t, d), BF16), _sds((t, ff), BF16), _sds((t, ff), BF16), _sds((t, ff), BF16),
                   _sds((1, d), F32), _sds((1, d), F32)],
        compiler_params=_params(),
    )(x2, f, target, gt, up, g_pre, wg_t, wu_t, wd, g_post)


def _attn_bwd(dx2, a, x1, q, k, v, g_pre, wq, wo, g_post, tm, name, comm):
    t, d = x1.shape
    m = k.shape[0]
    scale = XA_HEAD_DIM ** -0.5

    def core(dx2_ref, a_ref, x_ref, q_ref, k_ref, v_ref, gp_ref, wq_ref, wo_ref, go_ref,
             dx1_ref, da_ref, dq_ref, dk_ref, dv_ref, dgo_ref, dgp_ref, do_buf):
        _zero_at_first_step(dgo_ref, dgp_ref, dk_ref, dv_ref)
        dx2 = dx2_ref[...]
        da, dgo = _rms_bwd(dx2, a_ref[...].astype(F32), go_ref[...])
        dgo_ref[...] += dgo
        da = da.astype(BF16)
        da_ref[...] = da
        do_buf[...] = _dot_nt(da, wo_ref[...]).astype(BF16)
        for hd in range(XA_HEADS):
            cols = slice(hd * XA_HEAD_DIM, (hd + 1) * XA_HEAD_DIM)
            qh = q_ref[:, cols]
            p = _softmax_rows(_dot_nt(qh, k_ref[:, cols]) * scale)
            do_h = do_buf[:, cols]
            dp = _dot_nt(do_h, v_ref[:, cols])
            dv_ref[:, cols] += _dot_tn(p.astype(BF16), do_h)
            ds = (p * (dp - jnp.sum(dp * p, axis=-1, keepdims=True)) * scale).astype(BF16)
            dq_ref[:, cols] = _dot(ds, k_ref[:, cols]).astype(BF16)
            dk_ref[:, cols] += _dot_tn(ds, qh)
        dh = _dot_nt(dq_ref[...], wq_ref[...])
        dxn, dgp = _rms_bwd(dh, x_ref[...], gp_ref[...])
        dgp_ref[...] += dgp
        dx1_ref[...] = dx2 + dxn

    return _hosted_call(
        core, comm, name, t // tm,
        in_specs=[_tile(tm, d), _tile(tm, d), _tile(tm, d), _tile(tm, d), _whole((m, d)), _whole((m, d)), _whole((1, d)),
                  _resident((d, d)), _resident((d, d)), _whole((1, d))],
        out_specs=[_tile(tm, d), _tile(tm, d), _tile(tm, d), _whole((m, d)), _whole((m, d)), _whole((1, d)), _whole((1, d))],
        out_shape=[_sds((t, d), F32), _sds((t, d), BF16), _sds((t, d), BF16), _sds((m, d), F32), _sds((m, d), F32),
                   _sds((1, d), F32), _sds((1, d), F32)],
        scratch_shapes=[pltpu.VMEM((tm, d), BF16)],
        operands=(dx2, a, x1, q, k, v, g_pre, wq, wo, g_post))


def _mem_bwd(mem, mem_n, dk, dv, g, wk, wv, name):
    m, d = mem.shape

    def body(mem_ref, n_ref, dk_ref, dv_ref, g_ref, wk_ref, wv_ref, dwk_ref, dwv_ref, dg_ref):
        dk = dk_ref[...].astype(BF16)
        dv = dv_ref[...].astype(BF16)
        n = n_ref[...]
        dwk_ref[...] = _dot_tn(n, dk).astype(BF16)
        dwv_ref[...] = _dot_tn(n, dv).astype(BF16)
        dn = _dot_nt(dk, wk_ref[...]) + _dot_nt(dv, wv_ref[...])
        _, dg = _rms_bwd(dn, mem_ref[...], g_ref[...])
        dg_ref[...] = dg

    return pl.pallas_call(
        body, name=name, grid=(1,),
        in_specs=[_whole((m, d)), _whole((m, d)), _whole((m, d)), _whole((m, d)), _whole((1, d)), _whole((d, d)),
                  _whole((d, d))],
        out_specs=[_whole((d, d)), _whole((d, d)), _whole((1, d))],
        out_shape=[_sds((d, d), BF16), _sds((d, d), BF16), _sds((1, d), F32)],
        compiler_params=_params(),
    )(mem, mem_n, dk, dv, g, wk, wv)


def _proj_bwd(dxo, s, w, g, tm, name):
    t, d = dxo.shape
    k = w.shape[0]

    def body(dx_ref, s_ref, w_ref, g_ref, ds_ref, da_ref, dg_ref):
        _zero_at_first_step(dg_ref)
        ds, dg = _rms_bwd(dx_ref[...], s_ref[...].astype(F32), g_ref[...])
        dg_ref[...] += dg
        ds = ds.astype(BF16)
        ds_ref[...] = ds
        da_ref[...] = _dot_nt(ds, w_ref[...]).astype(BF16)

    return pl.pallas_call(
        body, name=name, grid=(t // tm,),
        in_specs=[_tile(tm, d), _tile(tm, d), _resident((k, d)), _whole((1, d))],
        out_specs=[_tile(tm, d), _tile(tm, k), _whole((1, d))],
        out_shape=[_sds((t, d), BF16), _sds((t, k), BF16), _sds((1, d), F32)],
        compiler_params=_params(),
    )(dxo, s, w, g)


def _conv_bwd(dy, u, zc, ca, wa, wb, lg, lb, tc, name, comm):
    t = u.shape[0]
    n_tiles = t // tc
    n_chunks = tc // CONV_CHUNK

    def core(dy_ref, u_ref, zc_ref, ca_ref, wa_ref, wb_ref, lg_ref, lb_ref,
             du_ref, dwa_ref, dwb_ref, dbb_ref, dlg_ref, dlb_ref, ebuf, eabuf, zbuf, cvbuf, wacc, aacc, vacc):
        step = pl.program_id(0)

        @pl.when(step == 0)
        def _():
            ebuf[:, tc:tc + HALO_B, :] = jnp.zeros((D_B // LANES, HALO_B, LANES), F32)
            eabuf[:, tc:tc + HALO_A, :] = jnp.zeros((D_A // LANES, HALO_A, LANES), F32)
            wacc[...] = jnp.zeros_like(wacc)
            aacc[...] = jnp.zeros_like(aacc)
            vacc[...] = jnp.zeros_like(vacc)

        dbb = jnp.zeros((SUBLANES, D_B), F32)
        dlg = jnp.zeros((SUBLANES, D_B), F32)
        dlb = jnp.zeros((SUBLANES, D_B), F32)
        for c in range(n_chunks):
            rows = slice(c * CONV_CHUNK, (c + 1) * CONV_CHUNK)
            dy_a = dy_ref[rows, 0:D_A].astype(F32)
            b_a = u_ref[rows, 0:D_A].astype(F32)
            du_ref[rows, 0:D_A] = (dy_a * ca_ref[rows, :].astype(F32)).astype(BF16)
            dca = dy_a * b_a
            cv = u_ref[rows, D_A:2 * D_A].astype(F32) * u_ref[rows, 2 * D_A:3 * D_A].astype(F32)
            glu_g = u_ref[rows, 3 * D_A + D_B:3 * D_A + 2 * D_B].astype(F32)
            z = u_ref[rows, 3 * D_A:3 * D_A + D_B].astype(F32) * _sigmoid(glu_g)
            for lb_i in range(D_A // LANES):
                lanes = slice(lb_i * LANES, (lb_i + 1) * LANES)
                eabuf[lb_i, rows, :] = dca[:, lanes]
                cvbuf[lb_i, rows, :] = cv[:, lanes]
                zbuf[lb_i, rows, :] = z[:, lanes]

            zcv = zc_ref[rows, :].astype(F32)
            mu = jnp.mean(zcv, axis=-1, keepdims=True)
            xc = zcv - mu
            rstd = lax.rsqrt(jnp.mean(xc * xc, axis=-1, keepdims=True) + LN_EPS)
            xhat = xc * rstd
            ln = xhat * lg_ref[...] + lb_ref[...]
            sg = _sigmoid(ln)
            dln = dy_ref[rows, D_A:D_A + D_B].astype(F32) * (sg * (1.0 + ln * (1.0 - sg)))
            dlg = dlg + jnp.sum((dln * xhat).reshape(CONV_CHUNK // SUBLANES, SUBLANES, D_B), axis=0)
            dlb = dlb + jnp.sum(dln.reshape(CONV_CHUNK // SUBLANES, SUBLANES, D_B), axis=0)
            dxh = dln * lg_ref[...]
            dzc = rstd * (dxh - jnp.mean(dxh, axis=-1, keepdims=True) - xhat * jnp.mean(dxh * xhat, axis=-1, keepdims=True))
            dbb = dbb + jnp.sum(dzc.reshape(CONV_CHUNK // SUBLANES, SUBLANES, D_B), axis=0)
            for lb_i in range(D_B // LANES):
                ebuf[lb_i, rows, :] = dzc[:, lb_i * LANES:(lb_i + 1) * LANES]
        vacc[0] += dbb
        vacc[1] += dlg
        vacc[2] += dlb

        for lb_i in range(D_A // LANES):
            lanes = slice(lb_i * LANES, (lb_i + 1) * LANES)

            def cols(first):
                return slice(first + lb_i * LANES, first + (lb_i + 1) * LANES)

            for c in range(n_chunks):
                r0 = c * CONV_CHUNK
                rows = slice(r0, r0 + CONV_CHUNK)
                cv = cvbuf[lb_i, rows, :]
                dcv = jnp.zeros((CONV_CHUNK, LANES), F32)
                for k in range(CONV_A_W):
                    off = r0 + (CONV_A_W - 1) - k
                    e = eabuf[lb_i, off:off + CONV_CHUNK, :]
                    dcv = dcv + wa_ref[k:k + 1, lanes] * e
                    aacc[k, :, lanes] += jnp.sum((cv * e).reshape(CONV_CHUNK // SUBLANES, SUBLANES, LANES), axis=0)
                du_ref[rows, cols(D_A)] = (dcv * u_ref[rows, cols(2 * D_A)].astype(F32)).astype(BF16)
                du_ref[rows, cols(2 * D_A)] = (dcv * u_ref[rows, cols(D_A)].astype(F32)).astype(BF16)

                z = zbuf[lb_i, rows, :]
                dz = jnp.zeros((CONV_CHUNK, LANES), F32)
                for k in range(CONV_B_W):
                    off = r0 + (CONV_B_W - 1) - k
                    e = ebuf[lb_i, off:off + CONV_CHUNK, :]
                    dz = dz + wb_ref[k:k + 1, lanes] * e
                    wacc[k, :, lanes] += jnp.sum((z * e).reshape(CONV_CHUNK // SUBLANES, SUBLANES, LANES), axis=0)
                glu_v = u_ref[rows, cols(3 * D_A)].astype(F32)
                sgg = _sigmoid(u_ref[rows, cols(3 * D_A + D_B)].astype(F32))
                du_ref[rows, cols(3 * D_A)] = (dz * sgg).astype(BF16)
                du_ref[rows, cols(3 * D_A + D_B)] = (dz * glu_v * sgg * (1.0 - sgg)).astype(BF16)

            ebuf[lb_i, tc:tc + HALO_B, :] = ebuf[lb_i, 0:HALO_B, :]
            eabuf[lb_i, tc:tc + HALO_A, :] = eabuf[lb_i, 0:HALO_A, :]

        @pl.when(step == n_tiles - 1)
        def _():
            for k in range(CONV_B_W):
                dwb_ref[k:k + 1, :] = jnp.sum(wacc[k], axis=0, keepdims=True)
            for k in range(CONV_A_W):
                dwa_ref[k:k + 1, :] = jnp.sum(aacc[k], axis=0, keepdims=True)
            dbb_ref[...] = jnp.sum(vacc[0], axis=0, keepdims=True)
            dlg_ref[...] = jnp.sum(vacc[1], axis=0, keepdims=True)
            dlb_ref[...] = jnp.sum(vacc[2], axis=0, keepdims=True)

    return _hosted_call(
        core, comm, name, n_tiles,
        in_specs=[_rtile(tc, D_A + D_B, n_tiles), _rtile(tc, D_IN_ALL, n_tiles), _rtile(tc, D_B, n_tiles),
                  _rtile(tc, D_A, n_tiles), _whole((CONV_A_W, D_A)), _whole((CONV_B_W, D_B)), _whole((1, D_B)),
                  _whole((1, D_B))],
        out_specs=[_rtile(tc, D_IN_ALL, n_tiles), _whole((CONV_A_W, D_A)), _whole((CONV_B_W, D_B)), _whole((1, D_B)),
                   _whole((1, D_B)), _whole((1, D_B))],
        out_shape=[_sds((t, D_IN_ALL), BF16), _sds((CONV_A_W, D_A), F32), _sds((CONV_B_W, D_B), F32), _sds((1, D_B), F32),
                   _sds((1, D_B), F32), _sds((1, D_B), F32)],
        scratch_shapes=[pltpu.VMEM((D_B // LANES, tc + HALO_B, LANES), F32), pltpu.VMEM((D_A // LANES, tc + HALO_A, LANES), F32),
                        pltpu.VMEM((D_B // LANES, tc, LANES), F32), pltpu.VMEM((D_A // LANES, tc, LANES), F32),
                        pltpu.VMEM((CONV_B_W, SUBLANES, D_B), F32), pltpu.VMEM((CONV_A_W, SUBLANES, D_A), F32),
                        pltpu.VMEM((3, SUBLANES, D_B), F32)],
        operands=(dy, u, zc, ca, wa, wb, lg, lb))


def _in_bwd(du, w_t, x, dx1, g, tm, name):
    t, d = x.shape
    n = w_t.shape[0]

    def body(du_ref, w_ref, x_ref, dx1_ref, g_ref, dx_ref, dg_ref):
        _zero_at_first_step(dg_ref)
        dh = _dot(du_ref[...], w_ref[...])
        dxn, dg = _rms_bwd(dh, x_ref[...], g_ref[...])
        dg_ref[...] += dg
        dx_ref[...] = dx1_ref[...] + dxn

    return pl.pallas_call(
        body, name=name, grid=(t // tm,),
        in_specs=[_tile(tm, n), _resident((n, d)), _tile(tm, d), _tile(tm, d), _whole((1, d))],
        out_specs=[_tile(tm, d), _whole((1, d))],
        out_shape=[_sds((t, d), F32), _sds((1, d), F32)],
        compiler_params=_params(),
    )(du, w_t, x, dx1, g)


def _wgrad(a, b, tk, bm, bn, name, comm=None, cols=None):
    t, m = a.shape
    first, n = (0, b.shape[1]) if cols is None else cols
    first_block = first // bn
    n_k = t // tk

    def body(a_ref, b_ref, o_ref, acc):
        @pl.when(pl.program_id(2) == 0)
        def _():
            acc[...] = jnp.zeros_like(acc)

        acc[...] += _dot_tn(a_ref[...], b_ref[...])

        @pl.when(pl.program_id(2) == n_k - 1)
        def _():
            o_ref[...] = acc[...].astype(BF16)

    call = dict(
        grid=(m // bm, n // bn, n_k),
        in_specs=[pl.BlockSpec((tk, bm), lambda i, j, k: (k, i)),
                  pl.BlockSpec((tk, bn), lambda i, j, k: (k, first_block + j))],
        out_specs=[pl.BlockSpec((bm, bn), lambda i, j, k: (i, j))],
        out_shape=[_sds((m, n), BF16)],
        scratch_shapes=[pltpu.VMEM((bm, bn), F32)])
    if comm is None:
        return pl.pallas_call(body, name=name, compiler_params=_params(3), **call)(a, b)[0]
    (out,), received = _hosted_call(body, comm, name, operands=(a, b), **call)
    return out, received


def _adamw(items, tr, name):
    n_parts, r, c = items[0][0].shape
    n_items = len(items)

    def body(*refs):
        ins, outs = refs[:4 * n_items], refs[4 * n_items:]
        for i in range(n_items):
            p_ref, w_ref, m_ref, v_ref = ins[4 * i:4 * i + 4]
            g_ref, d_ref, nm_ref, nv_ref = outs[4 * i:4 * i + 4]
            g = p_ref[0].astype(F32)
            for j in range(1, n_parts):
                g = g + p_ref[j].astype(F32)
            g_ref[...] = g
            nm = ADAM_B1 * m_ref[...] + (1.0 - ADAM_B1) * g
            nv = ADAM_B2 * v_ref[...] + (1.0 - ADAM_B2) * (g * g)
            nm_ref[...] = nm
            nv_ref[...] = nv
            m_hat = nm / (1.0 - ADAM_B1 ** ADAM_STEP)
            v_hat = nv / (1.0 - ADAM_B2 ** ADAM_STEP)
            d_ref[...] = -ADAM_LR * (m_hat / (jnp.sqrt(v_hat) + ADAM_EPS) + ADAM_WD * w_ref[...])

    results = pl.pallas_call(
        body, name=name, grid=(r // tr,),
        in_specs=([pl.BlockSpec((n_parts, tr, c), lambda i: (0, i, 0))] + [_tile(tr, c)] * 3) * n_items,
        out_specs=[_tile(tr, c)] * (4 * n_items),
        out_shape=[_sds((r, c), F32)] * (4 * n_items),
        compiler_params=_params(),
    )(*[a for item in items for a in item])
    return [results[4 * i:4 * i + 4] for i in range(n_items)]


def _sum_parts(parts, name):
    n_parts, r, c = parts.shape

    def body(p_ref, o_ref):
        acc = p_ref[0]
        for j in range(1, n_parts):
            acc = acc + p_ref[j]
        o_ref[...] = acc

    return pl.pallas_call(
        body, name=name, grid=(1,),
        in_specs=[_whole((n_parts, r, c))], out_specs=_whole((r, c)), out_shape=_sds((r, c), F32),
        compiler_params=_params(),
    )(parts)


def _row(v):
    return v.reshape(1, -1)


def _by_owner_rows(g):
    return g.reshape(N_DEV, g.shape[0] // N_DEV, g.shape[1])


WEIGHTS = ("mix_pre_g", "w_mix_in", "conv_a_w", "conv_b_w", "conv_b_b", "ln_b_g", "ln_b_b", "w_mix_out", "mix_post_g",
           "xa_pre_g", "mem_norm_g", "w_q", "w_k", "w_v", "w_o", "xa_post_g", "ffn_pre_g", "w_gate", "w_up", "w_down",
           "ffn_post_g")
LARGE = ("w_mix_in", "w_mix_out", "w_q", "w_k", "w_v", "w_o", "w_gate", "w_up", "w_down")
COLUMN_SHARDED = ("w_mix_in", "w_gate", "w_up")
GAINS = ("mix_pre_g", "mix_post_g", "xa_pre_g", "mem_norm_g", "xa_post_g", "ffn_pre_g", "ffn_post_g")
CHANNEL_VECTORS = ("conv_b_b", "ln_b_g", "ln_b_b")
CONV_TAPS = ("conv_a_w", "conv_b_w")
SMALL = GAINS + CHANNEL_VECTORS + CONV_TAPS
CONV_COLS_PER_DEVICE = D_A // N_DEV
TOKEN_TILE = 512
FFN_FWD_TOKEN_TILE = 512
FFN_TOKEN_TILE = 256
CONV_TOKEN_TILE = 256
WGRAD_TOKEN_TILE = 2048
ADAM_ROWS_PER_STEP = 64


def _lane_rows(v):
    flat = v.reshape(-1)
    tile = SUBLANES * LANES
    flat = jnp.pad(flat, (0, (-flat.shape[0]) % tile))
    return flat.reshape(-1, LANES)


def _pack_small(values, names):
    return jnp.concatenate([_lane_rows(values[n]) for n in names], axis=0)


def _unpack_small(packed, like, names):
    out, off = {}, 0
    for n in names:
        size = like[n].size
        rows = _lane_rows(like[n]).shape[0]
        out[n] = packed[off:off + rows, :].reshape(-1)[:size].reshape(like[n].shape)
        off += rows
    return out


def kernel(x, mem, mix_pre_g, w_mix_in, conv_a_w, conv_b_w, conv_b_b, ln_b_g, ln_b_b, w_mix_out, mix_post_g, xa_pre_g, mem_norm_g, w_q, w_k, w_v, w_o, xa_post_g, ffn_pre_g, w_gate, w_up, w_down, ffn_post_g, loss_target, m_mix_pre_g, m_w_mix_in, m_conv_a_w, m_conv_b_w, m_conv_b_b, m_ln_b_g, m_ln_b_b, m_w_mix_out, m_mix_post_g, m_xa_pre_g, m_mem_norm_g, m_w_q, m_w_k, m_w_v, m_w_o, m_xa_post_g, m_ffn_pre_g, m_w_gate, m_w_up, m_w_down, m_ffn_post_g, v_mix_pre_g, v_w_mix_in, v_conv_a_w, v_conv_b_w, v_conv_b_b, v_ln_b_g, v_ln_b_b, v_w_mix_out, v_mix_post_g, v_xa_pre_g, v_mem_norm_g, v_w_q, v_w_k, v_w_v, v_w_o, v_xa_post_g, v_ffn_pre_g, v_w_gate, v_w_up, v_w_down, v_ffn_post_g):
    given = dict(locals())
    w = {n: given[n] for n in WEIGHTS}
    m = {n: given["m_" + n] for n in WEIGHTS}
    v = {n: given["v_" + n] for n in WEIGHTS}
    xs, mems, target = x[0], mem[0], loss_target[0]
    t = xs.shape[0]
    tm, tm_ffn, tc, tk = min(TOKEN_TILE, t), min(FFN_TOKEN_TILE, t), min(CONV_TOKEN_TILE, t), min(WGRAD_TOKEN_TILE, t)
    g = {n: _row(w[n]) for n in GAINS}
    bb, lg, lb = (_row(w[n]) for n in CHANNEL_VECTORS)

    def shard_bf16(*names):
        return [w[n].astype(BF16) for n in names]

    def shard_bf16_t(*names):
        return [w[n].T.astype(BF16) for n in names]

    g_mix_in, g_taps = _comm_call(_Gather(shard_bf16_t("w_mix_in") + [_pack_small(w, CONV_TAPS)]), "gather_mixer")
    w_mix_in_t = g_mix_in.reshape(D_IN_ALL, D_MODEL)
    taps, off = {}, 0
    for n in CONV_TAPS:
        k, cols = w[n].shape
        rows = _lane_rows(w[n]).shape[0]
        blk = g_taps[:, off:off + rows, :].reshape(N_DEV, -1)[:, :k * cols].reshape(N_DEV, k, cols)
        taps[n] = blk.transpose(1, 0, 2).reshape(k, N_DEV * cols)
        off += rows
    wa, wb = taps["conv_a_w"], taps["conv_b_w"]

    (h1, u), gathered = _norm_matmul(xs, g["mix_pre_g"], w_mix_in_t.T, tm, "mix_in_fwd",
                                     _Gather(shard_bf16("w_mix_out", "w_q", "w_k", "w_v", "w_o")))
    w_mix_out, w_q, w_k, w_v, w_o = (a.reshape(D_MODEL, D_MODEL) for a in gathered)
    (ycat, zc, ca, x1, s1), (g_gate, g_up, g_down) = _conv_fwd(
        u, wa, wb, bb, lg, lb, xs, w_mix_out, g["mix_post_g"], tc, "conv_fwd",
        _Gather(shard_bf16_t("w_gate", "w_up") + shard_bf16("w_down")))
    w_gate_t, w_up_t, w_down = (a.reshape(D_FF, D_MODEL) for a in (g_gate, g_up, g_down))
    mem_n, kk, vv = _mem_fwd(mems, g["mem_norm_g"], w_k, w_v, "mem_fwd")
    x2, h2, q, o, a = _attn_fwd(x1, g["xa_pre_g"], w_q, kk, vv, w_o, g["xa_post_g"], tm, "attn_fwd")
    h3, gt, up, f, sq = _ffn_fwd(x2, g["ffn_pre_g"], w_gate_t, w_up_t, w_down, g["ffn_post_g"], target,
                                 min(FFN_FWD_TOKEN_TILE, t), "ffn_fwd")

    dx2, df, hd, dgt, dup, d_ffn_post, d_ffn_pre = _ffn_bwd(
        x2, f, target, gt, up, g["ffn_pre_g"], w_gate_t, w_up_t, w_down, g["ffn_post_g"], tm_ffn, "ffn_bwd")
    d_w_down = _wgrad(hd, df, tk, D_FF // 2, D_MODEL, "wgrad_down")
    d_w_gate_t = _wgrad(dgt, h3, tk, D_FF // 2, D_MODEL, "wgrad_gate")
    d_w_up_t = _wgrad(dup, h3, tk, D_FF // 2, D_MODEL, "wgrad_up")
    ffn_slabs = [_by_owner_rows(d) for d in (d_w_gate_t, d_w_up_t, d_w_down)]

    (dx1, da, dq, dk, dv, d_xa_post, d_xa_pre), from_ffn = _attn_bwd(
        dx2, a, x1, q, kk, vv, g["xa_pre_g"], w_q, w_o, g["xa_post_g"], tm, "attn_bwd",
        _Exchange(ffn_slabs, [True] * 3))
    d_w_o = _wgrad(o, da, tk, D_MODEL, D_MODEL, "wgrad_o")
    d_w_q = _wgrad(h2, dq, tk, D_MODEL, D_MODEL, "wgrad_q")
    d_w_k, d_w_v, d_mem_norm = _mem_bwd(mems, mem_n, dk, dv, g["mem_norm_g"], w_k, w_v, "mem_bwd")
    ds1, dycat, d_mix_post = _proj_bwd(dx1, s1, w_mix_out, g["mix_post_g"], tm, "mix_out_bwd")
    d_w_mix_out = _wgrad(ycat, ds1, tk, D_MODEL, D_MODEL, "wgrad_mix_out")
    attn_slabs = [_by_owner_rows(d) for d in (d_w_mix_out, d_w_q, d_w_k, d_w_v, d_w_o)]

    (du, d_conv_a, d_conv_b, d_conv_bb, d_ln_g, d_ln_b), from_attn = _conv_bwd(
        dycat, u, zc, ca, wa, wb, lg, lb, tc, "conv_bwd", _Exchange(attn_slabs, [True] * 5))
    small_grads = dict(conv_a_w=d_conv_a, conv_b_w=d_conv_b, conv_b_b=d_conv_bb, ln_b_g=d_ln_g, ln_b_b=d_ln_b,
                       mix_post_g=d_mix_post, xa_pre_g=d_xa_pre, mem_norm_g=d_mem_norm, xa_post_g=d_xa_post,
                       ffn_pre_g=d_ffn_pre, ffn_post_g=d_ffn_post, loss=sq)
    names = tuple(n for n in SMALL if n != "mix_pre_g") + ("loss", "mix_pre_g")
    half = D_MODEL // 2
    d_in_lo = _wgrad(du, h1, tk, D_IN_ALL // 2, half, "wgrad_mix_in_lo", cols=(0, half))
    d_in_hi, (from_in_lo, early_small) = _wgrad(
        du, h1, tk, D_IN_ALL // 2, half, "wgrad_mix_in_hi",
        _Exchange([_by_owner_rows(d_in_lo), _pack_small(small_grads, names[:-1])], [True, False]), cols=(half, half))
    dx, d_mix_pre = _in_bwd(du, w_mix_in_t, xs, dx1, g["mix_pre_g"], tm, "mix_in_bwd")
    small_grads["mix_pre_g"] = d_mix_pre
    from_in_hi, late_small = _comm_call(
        _Exchange([_by_owner_rows(d_in_hi), _lane_rows(d_mix_pre)], [True, False]), "reduce_tail")
    all_small = jnp.concatenate([early_small, late_small], axis=1)

    received = dict(zip(("w_gate", "w_up", "w_down"), from_ffn))
    received.update(zip(("w_mix_out", "w_q", "w_k", "w_v", "w_o"), from_attn))
    received["w_mix_in"] = jnp.concatenate([from_in_lo, from_in_hi], axis=2)
    grad, delta, new_m, new_v = {}, {}, {}, {}
    for group in (("w_mix_in",), ("w_gate",), ("w_up",), ("w_down",), ("w_mix_out", "w_q", "w_k", "w_v", "w_o")):
        items = [[received[n]] + [a[n].T if n in COLUMN_SHARDED else a[n] for a in (w, m, v)] for n in group]
        rows = items[0][1].shape[0]
        results = _adamw(items, min(rows, ADAM_ROWS_PER_STEP) if len(group) > 1 else rows, "adamw_" + group[0])
        for n, result in zip(group, results):
            grad[n], delta[n], new_m[n], new_v[n] = [r.T if n in COLUMN_SHARDED else r for r in result]

    total = _unpack_small(_sum_parts(all_small, "sum_small"), small_grads, names)
    loss = jnp.sum(total.pop("loss")) * (0.5 / D_MODEL)
    first_col = _device_index() * CONV_COLS_PER_DEVICE
    for n in CONV_TAPS:
        total[n] = lax.dynamic_slice_in_dim(total[n], first_col, CONV_COLS_PER_DEVICE, axis=1)
    total = {n: total[n].reshape(w[n].shape) for n in SMALL}
    packed_small = [_pack_small(values, SMALL) for values in (total, w, m, v)]
    ((g_s, d_s, nm_s, nv_s),) = _adamw([[packed_small[0][None]] + packed_small[1:]], packed_small[0].shape[0],
                                       "adamw_small")
    for out, p in ((grad, g_s), (delta, d_s), (new_m, nm_s), (new_v, nv_s)):
        out.update(_unpack_small(p, w, SMALL))

    return (loss, dx[None], *[grad[n] for n in WEIGHTS], *[delta[n] for n in WEIGHTS], *[new_m[n] for n in WEIGHTS],
            *[new_v[n] for n in WEIGHTS])
```

```python
import functools

import jax
import jax.numpy as jnp
from jax import lax
from jax.experimental import pallas as pl
from jax.experimental.pallas import tpu as pltpu

F32 = jnp.float32
BF16 = jnp.bfloat16

D_MODEL = 1024
D_A = 512
D_B = 512
D_IN_ALL = 2560
CONV_A_W = 3
CONV_B_W = 31
XA_HEADS = 4
XA_HEAD_DIM = 256
D_FF = 2816
N_DEV = 8
RMS_EPS = 1e-6
LN_EPS = 1e-5
ADAM_LR = 0.001
ADAM_B1 = 0.9
ADAM_B2 = 0.999
ADAM_EPS = 1e-08
ADAM_WD = 0.01
ADAM_STEP = 10

VMEM_LIMIT_BYTES = 56 * 1024 * 1024
SUBLANES = 8
LANES = 128
HALO_B = 32
HALO_A = 8
CONV_CHUNK = 32

MESH = pl.DeviceIdType.MESH


def _params(n_grid_axes=1):
    return pltpu.CompilerParams(dimension_semantics=("arbitrary",) * n_grid_axes, vmem_limit_bytes=VMEM_LIMIT_BYTES)


def _sds(shape, dtype):
    return jax.ShapeDtypeStruct(shape, dtype)


def _tile(rows, cols):
    return pl.BlockSpec((rows, cols), lambda i: (i, 0))


def _rtile(rows, cols, n):
    return pl.BlockSpec((rows, cols), lambda i: (n - 1 - i, 0))


def _whole(shape):
    zeros = (0,) * len(shape)
    return pl.BlockSpec(shape, lambda i: zeros)


def _resident(shape):
    zeros = (0,) * len(shape)
    return pl.BlockSpec(shape, lambda i: zeros, pipeline_mode=pl.Buffered(1))


def _dot(a, b):
    return jnp.dot(a, b, preferred_element_type=F32)


def _dot_nt(a, b):
    return lax.dot_general(a, b, (((1,), (1,)), ((), ())), preferred_element_type=F32)


def _dot_tn(a, b):
    return lax.dot_general(a, b, (((0,), (0,)), ((), ())), preferred_element_type=F32)


def _sigmoid(x):
    return 1.0 / (1.0 + jnp.exp(-x))


def _rms_fwd(x, g):
    r = lax.rsqrt(jnp.mean(x * x, axis=-1, keepdims=True) + RMS_EPS)
    return x * r * g


def _rms_bwd(dy, xin, g):
    r = lax.rsqrt(jnp.mean(xin * xin, axis=-1, keepdims=True) + RMS_EPS)
    n = xin * r
    dg = jnp.sum(dy * n, axis=0, keepdims=True)
    dn = dy * g
    dx = r * (dn - n * jnp.mean(dn * n, axis=-1, keepdims=True))
    return dx, dg


def _zero_at_first_step(*refs):
    @pl.when(pl.program_id(0) == 0)
    def _():
        for ref in refs:
            ref[...] = jnp.zeros(ref.shape, ref.dtype)


def _place():
    return lax.axis_index("x"), lax.axis_index("y"), lax.axis_index("c")


def _device_index():
    x, y, c = _place()
    return 4 * x + 2 * y + c


class _Gather:
    def __init__(self, arrays):
        self.arrays = list(arrays)
        self.out_shape = [_sds((N_DEV, *a.shape), a.dtype) for a in self.arrays]
        n = len(self.arrays)
        self.scratch_shapes = [pltpu.SemaphoreType.DMA((n, 7)), pltpu.SemaphoreType.DMA((n, 7)),
                               pltpu.SemaphoreType.DMA((n,))]

    def forward_step(self, n_steps):
        return max(0, n_steps - 9)

    def bind(self, srcs, dsts, send_sems, recv_sems, local_sems):
        x, y, cc = _place()
        me, sibling = (x, y, cc), (x, y, 1 - cc)
        chips = [(1 - x, y), (x, 1 - y), (1 - x, 1 - y)]

        def copy(a, k, owner, to, src=None):
            slot = dsts[a].at[4 * owner[0] + 2 * owner[1] + owner[2]]
            return pltpu.make_async_remote_copy(
                src_ref=slot if src is None else src, dst_ref=slot, send_sem=send_sems.at[a, k],
                recv_sem=recv_sems.at[a, k], device_id=to, device_id_type=MESH)

        def first(a):
            return [copy(a, 0, me, sibling, src=srcs[a])] + [
                copy(a, 1 + j, me, (*chip, cc), src=srcs[a]) for j, chip in enumerate(chips)]

        def passed(a, j):
            return copy(a, 4 + j, (*chips[j], cc), sibling)

        def mine(a):
            return pltpu.make_async_copy(srcs[a], dsts[a].at[4 * x + 2 * y + cc], local_sems.at[a])

        def start():
            for a in range(len(srcs)):
                mine(a).start()
                for cp in first(a):
                    cp.start()

        def forward():
            for a in range(len(srcs)):
                for j, chip in enumerate(chips):
                    copy(a, 1 + j, (*chip, cc), me).wait_recv()
                    passed(a, j).start()

        def finish():
            for a in range(len(srcs)):
                copy(a, 0, sibling, me).wait_recv()
                for j, chip in enumerate(chips):
                    copy(a, 4 + j, (*chip, 1 - cc), me).wait_recv()
                for cp in first(a) + [passed(a, j) for j in range(len(chips))]:
                    cp.wait_send()
                mine(a).wait()

        return start, forward, finish


class _Exchange:
    def __init__(self, arrays, scatter):
        self.arrays = list(arrays)
        self.scatter = list(scatter)
        self.out_shape = [_sds(a.shape if s else (N_DEV, *a.shape), a.dtype) for a, s in zip(self.arrays, self.scatter)]
        n = len(self.arrays)
        self.scratch_shapes = [pltpu.SemaphoreType.DMA((n, 7)), pltpu.SemaphoreType.DMA((n, 7)),
                               pltpu.SemaphoreType.DMA((n,))]

    def forward_step(self, n_steps):
        return n_steps - 1

    def bind(self, srcs, dsts, send_sems, recv_sems, local_sems):
        me = _device_index()

        def copies(a):
            out = []
            for k in range(1, N_DEV):
                p = me ^ k
                out.append(pltpu.make_async_remote_copy(
                    src_ref=srcs[a].at[p] if self.scatter[a] else srcs[a], dst_ref=dsts[a].at[me],
                    send_sem=send_sems.at[a, k - 1], recv_sem=recv_sems.at[a, k - 1],
                    device_id=(p >> 2, (p >> 1) & 1, p & 1), device_id_type=MESH))
            return out

        def mine(a):
            return pltpu.make_async_copy(srcs[a].at[me] if self.scatter[a] else srcs[a], dsts[a].at[me], local_sems.at[a])

        def start():
            for a in range(len(srcs)):
                mine(a).start()
                for cp in copies(a):
                    cp.start()

        def forward():
            pass

        def finish():
            for a in range(len(srcs)):
                for cp in copies(a):
                    cp.wait()
                mine(a).wait()

        return start, forward, finish


def _hosted_call(core, comm, name, grid, in_specs, out_specs, out_shape, scratch_shapes, operands):
    grid = (grid,) if isinstance(grid, int) else tuple(grid)
    n_steps = 1
    for extent in grid:
        n_steps *= extent
    n_in, n_out, n_scr, n_arr = len(in_specs), len(out_specs), len(scratch_shapes), len(comm.arrays)
    any_spec = pl.BlockSpec(memory_space=pl.ANY)

    def body(*refs):
        ins, refs = refs[:n_in], refs[n_in:]
        srcs, refs = refs[:n_arr], refs[n_arr:]
        outs, refs = refs[:n_out], refs[n_out:]
        dsts, refs = refs[:n_arr], refs[n_arr:]
        scratch, sems = refs[:n_scr], refs[n_scr:]
        start, forward, finish = comm.bind(srcs, dsts, *sems)
        step = pl.program_id(0)
        for axis in range(1, len(grid)):
            step = step * grid[axis] + pl.program_id(axis)
        pl.when(step == 0)(start)
        core(*ins, *outs, *scratch)
        pl.when(step == comm.forward_step(n_steps))(forward)
        pl.when(step == n_steps - 1)(finish)

    results = pl.pallas_call(
        body, name=name, grid=grid,
        in_specs=list(in_specs) + [any_spec] * n_arr,
        out_specs=list(out_specs) + [any_spec] * n_arr,
        out_shape=list(out_shape) + comm.out_shape,
        scratch_shapes=list(scratch_shapes) + comm.scratch_shapes,
        compiler_params=_params(len(grid)),
    )(*operands, *comm.arrays)
    return results[:n_out], results[n_out:]


def _comm_call(comm, name):
    return _hosted_call(lambda: None, comm, name, 1, [], [], [], [], [])[1]


def _norm_matmul(x, g, w, tm, name, comm):
    t, d = x.shape
    n = w.shape[1]

    def core(x_ref, g_ref, w_ref, h_ref, o_ref):
        h = _rms_fwd(x_ref[...], g_ref[...]).astype(BF16)
        h_ref[...] = h
        o_ref[...] = _dot(h, w_ref[...]).astype(BF16)

    return _hosted_call(
        core, comm, name, t // tm,
        in_specs=[_tile(tm, d), _whole((1, d)), _resident((d, n))],
        out_specs=[_tile(tm, d), _tile(tm, n)],
        out_shape=[_sds((t, d), BF16), _sds((t, n), BF16)],
        scratch_shapes=[], operands=(x, g, w))


def _conv_fwd(u, wa, wb, bb, lg, lb, xres, w_out, g_post, tc, name, comm):
    t = u.shape[0]
    d = w_out.shape[1]
    n_chunks = tc // CONV_CHUNK
    piece = 2 * LANES

    def core(u_ref, wa_ref, wb_ref, bb_ref, lg_ref, lb_ref, x_ref, wo_ref, go_ref,
             y_ref, zc_ref, ca_ref, x1_ref, s_ref, zbuf, cvbuf, zcbuf):
        @pl.when(pl.program_id(0) == 0)
        def _():
            zbuf[:, 0:HALO_B, :] = jnp.zeros((D_B // LANES, HALO_B, LANES), F32)
            cvbuf[:, 0:HALO_A, :] = jnp.zeros((D_A // LANES, HALO_A, LANES), F32)

        s = None
        for lb_i in range(D_A // LANES):
            if lb_i > 0 and lb_i % 2 == 0:
                cols = slice((lb_i - 2) * LANES, lb_i * LANES)
                part = _dot(y_ref[:, cols], wo_ref[cols, :])
                s = part if s is None else s + part
            lanes = slice(lb_i * LANES, (lb_i + 1) * LANES)
            c_a = u_ref[:, D_A + lb_i * LANES:D_A + (lb_i + 1) * LANES].astype(F32)
            v_a = u_ref[:, 2 * D_A + lb_i * LANES:2 * D_A + (lb_i + 1) * LANES].astype(F32)
            cvbuf[lb_i, HALO_A:HALO_A + tc, :] = c_a * v_a
            glu_v = u_ref[:, 3 * D_A + lb_i * LANES:3 * D_A + (lb_i + 1) * LANES].astype(F32)
            glu_g = u_ref[:, 3 * D_A + D_B + lb_i * LANES:3 * D_A + D_B + (lb_i + 1) * LANES].astype(F32)
            zbuf[lb_i, HALO_B:HALO_B + tc, :] = glu_v * _sigmoid(glu_g)

            for c in range(n_chunks):
                r0 = c * CONV_CHUNK
                rows = slice(r0, r0 + CONV_CHUNK)
                acc = jnp.zeros((CONV_CHUNK, LANES), F32)
                for k in range(CONV_A_W):
                    off = r0 + HALO_A - (CONV_A_W - 1) + k
                    acc = acc + wa_ref[k:k + 1, lanes] * cvbuf[lb_i, off:off + CONV_CHUNK, :]
                ca_ref[rows, lanes] = acc.astype(BF16)
                y_ref[rows, lanes] = (u_ref[rows, lanes].astype(F32) * acc).astype(BF16)

                accb = jnp.zeros((CONV_CHUNK, LANES), F32)
                for k in range(CONV_B_W):
                    off = r0 + HALO_B - (CONV_B_W - 1) + k
                    accb = accb + wb_ref[k:k + 1, lanes] * zbuf[lb_i, off:off + CONV_CHUNK, :]
                zcbuf[rows, lanes] = accb + bb_ref[:, lanes]

            zbuf[lb_i, 0:HALO_B, :] = zbuf[lb_i, tc:tc + HALO_B, :]
            cvbuf[lb_i, 0:HALO_A, :] = cvbuf[lb_i, tc:tc + HALO_A, :]

        cols = slice(D_A - piece, D_A)
        s = s + _dot(y_ref[:, cols], wo_ref[cols, :])

        for c in range(n_chunks):
            rows = slice(c * CONV_CHUNK, (c + 1) * CONV_CHUNK)
            zc = zcbuf[rows, :]
            zc_ref[rows, :] = zc.astype(BF16)
            mu = jnp.mean(zc, axis=-1, keepdims=True)
            xc = zc - mu
            var = jnp.mean(xc * xc, axis=-1, keepdims=True)
            ln = xc * lax.rsqrt(var + LN_EPS) * lg_ref[...] + lb_ref[...]
            y_ref[rows, D_A:D_A + D_B] = (ln * _sigmoid(ln)).astype(BF16)

        s = s + _dot(y_ref[:, D_A:D_A + D_B], wo_ref[D_A:D_A + D_B, :])
        s_ref[...] = s.astype(BF16)
        x1_ref[...] = x_ref[...] + _rms_fwd(s, go_ref[...])

    return _hosted_call(
        core, comm, name, t // tc,
        in_specs=[_tile(tc, D_IN_ALL), _whole((CONV_A_W, D_A)), _whole((CONV_B_W, D_B)), _whole((1, D_B)),
                  _whole((1, D_B)), _whole((1, D_B)), _tile(tc, d), _resident((D_A + D_B, d)), _whole((1, d))],
        out_specs=[_tile(tc, D_A + D_B), _tile(tc, D_B), _tile(tc, D_A), _tile(tc, d), _tile(tc, d)],
        out_shape=[_sds((t, D_A + D_B), BF16), _sds((t, D_B), BF16), _sds((t, D_A), BF16), _sds((t, d), F32),
                   _sds((t, d), BF16)],
        scratch_shapes=[pltpu.VMEM((D_B // LANES, HALO_B + tc, LANES), F32),
                        pltpu.VMEM((D_A // LANES, HALO_A + tc, LANES), F32), pltpu.VMEM((tc, D_B), F32)],
        operands=(u, wa, wb, bb, lg, lb, xres, w_out, g_post))


def _mem_fwd(mem, g, wk, wv, name):
    m, d = mem.shape

    def body(mem_ref, g_ref, wk_ref, wv_ref, n_ref, k_ref, v_ref):
        n = _rms_fwd(mem_ref[...], g_ref[...]).astype(BF16)
        n_ref[...] = n
        k_ref[...] = _dot(n, wk_ref[...]).astype(BF16)
        v_ref[...] = _dot(n, wv_ref[...]).astype(BF16)

    return pl.pallas_call(
        body, name=name, grid=(1,),
        in_specs=[_whole((m, d)), _whole((1, d)), _whole((d, d)), _whole((d, d))],
        out_specs=[_whole((m, d))] * 3,
        out_shape=[_sds((m, d), BF16)] * 3,
        compiler_params=_params(),
    )(mem, g, wk, wv)


def _softmax_rows(s):
    e = jnp.exp(s - jnp.max(s, axis=-1, keepdims=True))
    return e / jnp.sum(e, axis=-1, keepdims=True)


def _attn_fwd(x1, g_pre, wq, k, v, wo, g_post, tm, name):
    t, d = x1.shape
    m = k.shape[0]
    scale = XA_HEAD_DIM ** -0.5

    def body(x_ref, gp_ref, wq_ref, k_ref, v_ref, wo_ref, go_ref, x2_ref, h_ref, q_ref, o_ref, a_ref):
        x = x_ref[...]
        h = _rms_fwd(x, gp_ref[...]).astype(BF16)
        h_ref[...] = h
        q_ref[...] = _dot(h, wq_ref[...]).astype(BF16)
        for hd in range(XA_HEADS):
            cols = slice(hd * XA_HEAD_DIM, (hd + 1) * XA_HEAD_DIM)
            p = _softmax_rows(_dot_nt(q_ref[:, cols], k_ref[:, cols]) * scale)
            o_ref[:, cols] = _dot(p.astype(BF16), v_ref[:, cols]).astype(BF16)
        a = _dot(o_ref[...], wo_ref[...])
        a_ref[...] = a.astype(BF16)
        x2_ref[...] = x + _rms_fwd(a, go_ref[...])

    return pl.pallas_call(
        body, name=name, grid=(t // tm,),
        in_specs=[_tile(tm, d), _whole((1, d)), _resident((d, d)), _whole((m, d)), _whole((m, d)), _resident((d, d)),
                  _whole((1, d))],
        out_specs=[_tile(tm, d)] * 5,
        out_shape=[_sds((t, d), F32)] + [_sds((t, d), BF16)] * 4,
        compiler_params=_params(),
    )(x1, g_pre, wq, k, v, wo, g_post)


def _ffn_fwd(x2, g_pre, wg_t, wu_t, wd, g_post, target, tm, name):
    t, d = x2.shape
    f = wg_t.shape[0]

    def body(x_ref, gp_ref, wg_ref, wu_ref, wd_ref, go_ref, tgt_ref, h_ref, gt_ref, up_ref, f_ref, sq_ref):
        _zero_at_first_step(sq_ref)
        x = x_ref[...]
        h = _rms_fwd(x, gp_ref[...]).astype(BF16)
        h_ref[...] = h
        gt = _dot_nt(h, wg_ref[...])
        up = _dot_nt(h, wu_ref[...])
        gt_ref[...] = gt.astype(BF16)
        up_ref[...] = up.astype(BF16)
        hd = (gt * _sigmoid(gt) * up).astype(BF16)
        ff = _dot(hd, wd_ref[...])
        f_ref[...] = ff.astype(BF16)
        err = x + _rms_fwd(ff, go_ref[...]) - tgt_ref[...]
        sq_ref[...] += jnp.sum(err * err, axis=0, keepdims=True)

    return pl.pallas_call(
        body, name=name, grid=(t // tm,),
        in_specs=[_tile(tm, d), _whole((1, d)), _resident((f, d)), _resident((f, d)), _resident((f, d)), _whole((1, d)),
                  _tile(tm, d)],
        out_specs=[_tile(tm, d), _tile(tm, f), _tile(tm, f), _tile(tm, d), _whole((1, d))],
        out_shape=[_sds((t, d), BF16), _sds((t, f), BF16), _sds((t, f), BF16), _sds((t, d), BF16), _sds((1, d), F32)],
        compiler_params=_params(),
    )(x2, g_pre, wg_t, wu_t, wd, g_post, target)


def _ffn_bwd(x2, f, target, gt, up, g_pre, wg_t, wu_t, wd, g_post, tm, name):
    t, d = x2.shape
    ff = wg_t.shape[0]

    def body(x_ref, f_ref, tgt_ref, gt_ref, up_ref, gp_ref, wg_ref, wu_ref, wd_ref, go_ref,
             dx_ref, df_ref, hd_ref, dgt_ref, dup_ref, dgo_ref, dgp_ref):
        _zero_at_first_step(dgo_ref, dgp_ref)
        x = x_ref[...]
        fo = f_ref[...].astype(F32)
        dx3 = (x + _rms_fwd(fo, go_ref[...]) - tgt_ref[...]) * (1.0 / d)
        df, dgo = _rms_bwd(dx3, fo, go_ref[...])
        dgo_ref[...] += dgo
        df = df.astype(BF16)
        df_ref[...] = df
        dhd = _dot_nt(df, wd_ref[...])
        gt = gt_ref[...].astype(F32)
        up = up_ref[...].astype(F32)
        sg = _sigmoid(gt)
        si = gt * sg
        hd_ref[...] = (si * up).astype(BF16)
        dup = (dhd * si).astype(BF16)
        dgt = (dhd * up * (sg * (1.0 + gt * (1.0 - sg)))).astype(BF16)
        dup_ref[...] = dup
        dgt_ref[...] = dgt
        dh = _dot(dgt, wg_ref[...]) + _dot(dup, wu_ref[...])
        dxn, dgp = _rms_bwd(dh, x, gp_ref[...])
        dgp_ref[...] += dgp
        dx_ref[...] = dx3 + dxn

    return pl.pallas_call(
        body, name=name, grid=(t // tm,),
        in_specs=[_tile(tm, d), _tile(tm, d), _tile(tm, d), _tile(tm, ff), _tile(tm, ff), _whole((1, d)),
                  _resident((ff, d)), _resident((ff, d)), _resident((ff, d)), _whole((1, d))],
        out_specs=[_tile(tm, d), _tile(tm, d), _tile(tm, ff), _tile(tm, ff), _tile(tm, ff), _whole((1, d)), _whole((1, d))],
        out_shape=[_sds((t, d), F32), _sds((t, d), BF16), _sds((t, ff), BF16), _sds((t, ff), BF16), _sds((t, ff), BF16),
                   _sds((1, d), F32), _sds((1, d), F32)],
        compiler_params=_params(),
    )(x2, f, target, gt, up, g_pre, wg_t, wu_t, wd, g_post)


def _attn_bwd(dx2, a, x1, q, k, v, g_pre, wq, wo, g_post, tm, name, comm):
    t, d = x1.shape
    m = k.shape[0]
    scale = XA_HEAD_DIM ** -0.5

    def core(dx2_ref, a_ref, x_ref, q_ref, k_ref, v_ref, gp_ref, wq_ref, wo_ref, go_ref,
             dx1_ref, da_ref, dq_ref, dk_ref, dv_ref, dgo_ref, dgp_ref, do_buf):
        _zero_at_first_step(dgo_ref, dgp_ref, dk_ref, dv_ref)
        dx2 = dx2_ref[...]
        da, dgo = _rms_bwd(dx2, a_ref[...].astype(F32), go_ref[...])
        dgo_ref[...] += dgo
        da = da.astype(BF16)
        da_ref[...] = da
        do_buf[...] = _dot_nt(da, wo_ref[...]).astype(BF16)
        for hd in range(XA_HEADS):
            cols = slice(hd * XA_HEAD_DIM, (hd + 1) * XA_HEAD_DIM)
            qh = q_ref[:, cols]
            p = _softmax_rows(_dot_nt(qh, k_ref[:, cols]) * scale)
            do_h = do_buf[:, cols]
            dp = _dot_nt(do_h, v_ref[:, cols])
            dv_ref[:, cols] += _dot_tn(p.astype(BF16), do_h)
            ds = (p * (dp - jnp.sum(dp * p, axis=-1, keepdims=True)) * scale).astype(BF16)
            dq_ref[:, cols] = _dot(ds, k_ref[:, cols]).astype(BF16)
            dk_ref[:, cols] += _dot_tn(ds, qh)
        dh = _dot_nt(dq_ref[...], wq_ref[...])
        dxn, dgp = _rms_bwd(dh, x_ref[...], gp_ref[...])
        dgp_ref[...] += dgp
        dx1_ref[...] = dx2 + dxn

    return _hosted_call(
        core, comm, name, t // tm,
        in_specs=[_tile(tm, d), _tile(tm, d), _tile(tm, d), _tile(tm, d), _whole((m, d)), _whole((m, d)), _whole((1, d)),
                  _resident((d, d)), _resident((d, d)), _whole((1, d))],
        out_specs=[_tile(tm, d), _tile(tm, d), _tile(tm, d), _whole((m, d)), _whole((m, d)), _whole((1, d)), _whole((1, d))],
        out_shape=[_sds((t, d), F32), _sds((t, d), BF16), _sds((t, d), BF16), _sds((m, d), F32), _sds((m, d), F32),
                   _sds((1, d), F32), _sds((1, d), F32)],
        scratch_shapes=[pltpu.VMEM((tm, d), BF16)],
        operands=(dx2, a, x1, q, k, v, g_pre, wq, wo, g_post))


def _mem_bwd(mem, mem_n, dk, dv, g, wk, wv, name):
    m, d = mem.shape

    def body(mem_ref, n_ref, dk_ref, dv_ref, g_ref, wk_ref, wv_ref, dwk_ref, dwv_ref, dg_ref):
        dk = dk_ref[...].astype(BF16)
        dv = dv_ref[...].astype(BF16)
        n = n_ref[...]
        dwk_ref[...] = _dot_tn(n, dk).astype(BF16)
        dwv_ref[...] = _dot_tn(n, dv).astype(BF16)
        dn = _dot_nt(dk, wk_ref[...]) + _dot_nt(dv, wv_ref[...])
        _, dg = _rms_bwd(dn, mem_ref[...], g_ref[...])
        dg_ref[...] = dg

    return pl.pallas_call(
        body, name=name, grid=(1,),
        in_specs=[_whole((m, d)), _whole((m, d)), _whole((m, d)), _whole((m, d)), _whole((1, d)), _whole((d, d)),
                  _whole((d, d))],
        out_specs=[_whole((d, d)), _whole((d, d)), _whole((1, d))],
        out_shape=[_sds((d, d), BF16), _sds((d, d), BF16), _sds((1, d), F32)],
        compiler_params=_params(),
    )(mem, mem_n, dk, dv, g, wk, wv)


def _proj_bwd(dxo, s, w, g, tm, name, comm):
    t, d = dxo.shape
    k = w.shape[0]

    def core(dx_ref, s_ref, w_ref, g_ref, ds_ref, da_ref, dg_ref):
        _zero_at_first_step(dg_ref)
        ds, dg = _rms_bwd(dx_ref[...], s_ref[...].astype(F32), g_ref[...])
        dg_ref[...] += dg
        ds = ds.astype(BF16)
        ds_ref[...] = ds
        da_ref[...] = _dot_nt(ds, w_ref[...]).astype(BF16)

    return _hosted_call(
        core, comm, name, t // tm,
        in_specs=[_tile(tm, d), _tile(tm, d), _resident((k, d)), _whole((1, d))],
        out_specs=[_tile(tm, d), _tile(tm, k), _whole((1, d))],
        out_shape=[_sds((t, d), BF16), _sds((t, k), BF16), _sds((1, d), F32)],
        scratch_shapes=[], operands=(dxo, s, w, g))


def _du_pieces():
    pair = 2 * LANES
    first_pass = [(0, D_A)]
    per_pair = [[(base + p * pair, pair) for base in (D_A, 2 * D_A, 3 * D_A, 3 * D_A + D_B)] for p in range(2)]
    return [first_pass] + per_pair


def _rows_by_du_piece(w_in_t):
    return jnp.concatenate([w_in_t[c0:c0 + n] for piece in _du_pieces() for c0, n in piece], axis=0)


def _conv_bwd(dy, u, zc, ca, wa, wb, lg, lb, w_in_pieces, x, dx1, g_pre, tc, name):
    t = u.shape[0]
    d = x.shape[1]
    n_tiles = t // tc
    n_chunks = tc // CONV_CHUNK
    pieces = _du_pieces()

    def core(dy_ref, u_ref, zc_ref, ca_ref, wa_ref, wb_ref, lg_ref, lb_ref, wp_ref, x_ref, dx1_ref, gp_ref,
             du_ref, dwa_ref, dwb_ref, dbb_ref, dlg_ref, dlb_ref, dx_ref, dgp_ref,
             ebuf, eabuf, zbuf, cvbuf, wacc, aacc, vacc):
        step = pl.program_id(0)

        @pl.when(step == 0)
        def _():
            ebuf[:, tc:tc + HALO_B, :] = jnp.zeros((D_B // LANES, HALO_B, LANES), F32)
            eabuf[:, tc:tc + HALO_A, :] = jnp.zeros((D_A // LANES, HALO_A, LANES), F32)
            wacc[...] = jnp.zeros_like(wacc)
            aacc[...] = jnp.zeros_like(aacc)
            vacc[...] = jnp.zeros_like(vacc)
            dgp_ref[...] = jnp.zeros_like(dgp_ref)

        def projected(index):
            first_row = sum(n for piece in pieces[:index] for _, n in piece)
            cols = [du_ref[:, c0:c0 + n] for c0, n in pieces[index]]
            lhs = cols[0] if len(cols) == 1 else jnp.concatenate(cols, axis=1)
            return _dot(lhs, wp_ref[first_row:first_row + lhs.shape[1], :])

        dbb = jnp.zeros((SUBLANES, D_B), F32)
        dlg = jnp.zeros((SUBLANES, D_B), F32)
        dlb = jnp.zeros((SUBLANES, D_B), F32)
        for c in range(n_chunks):
            rows = slice(c * CONV_CHUNK, (c + 1) * CONV_CHUNK)
            dy_a = dy_ref[rows, 0:D_A].astype(F32)
            b_a = u_ref[rows, 0:D_A].astype(F32)
            du_ref[rows, 0:D_A] = (dy_a * ca_ref[rows, :].astype(F32)).astype(BF16)
            dca = dy_a * b_a
            cv = u_ref[rows, D_A:2 * D_A].astype(F32) * u_ref[rows, 2 * D_A:3 * D_A].astype(F32)
            glu_g = u_ref[rows, 3 * D_A + D_B:3 * D_A + 2 * D_B].astype(F32)
            z = u_ref[rows, 3 * D_A:3 * D_A + D_B].astype(F32) * _sigmoid(glu_g)
            for lb_i in range(D_A // LANES):
                lanes = slice(lb_i * LANES, (lb_i + 1) * LANES)
                eabuf[lb_i, rows, :] = dca[:, lanes]
                cvbuf[lb_i, rows, :] = cv[:, lanes]
                zbuf[lb_i, rows, :] = z[:, lanes]

            zcv = zc_ref[rows, :].astype(F32)
            mu = jnp.mean(zcv, axis=-1, keepdims=True)
            xc = zcv - mu
            rstd = lax.rsqrt(jnp.mean(xc * xc, axis=-1, keepdims=True) + LN_EPS)
            xhat = xc * rstd
            ln = xhat * lg_ref[...] + lb_ref[...]
            sg = _sigmoid(ln)
            dln = dy_ref[rows, D_A:D_A + D_B].astype(F32) * (sg * (1.0 + ln * (1.0 - sg)))
            dlg = dlg + jnp.sum((dln * xhat).reshape(CONV_CHUNK // SUBLANES, SUBLANES, D_B), axis=0)
            dlb = dlb + jnp.sum(dln.reshape(CONV_CHUNK // SUBLANES, SUBLANES, D_B), axis=0)
            dxh = dln * lg_ref[...]
            dzc = rstd * (dxh - jnp.mean(dxh, axis=-1, keepdims=True) - xhat * jnp.mean(dxh * xhat, axis=-1, keepdims=True))
            dbb = dbb + jnp.sum(dzc.reshape(CONV_CHUNK // SUBLANES, SUBLANES, D_B), axis=0)
            for lb_i in range(D_B // LANES):
                ebuf[lb_i, rows, :] = dzc[:, lb_i * LANES:(lb_i + 1) * LANES]
        vacc[0] += dbb
        vacc[1] += dlg
        vacc[2] += dlb
        dh = projected(0)

        for lb_i in range(D_A // LANES):
            lanes = slice(lb_i * LANES, (lb_i + 1) * LANES)
            if lb_i > 0 and lb_i % 2 == 0:
                dh = dh + projected(lb_i // 2)

            def cols(first):
                return slice(first + lb_i * LANES, first + (lb_i + 1) * LANES)

            for c in range(n_chunks):
                r0 = c * CONV_CHUNK
                rows = slice(r0, r0 + CONV_CHUNK)
                cv = cvbuf[lb_i, rows, :]
                dcv = jnp.zeros((CONV_CHUNK, LANES), F32)
                for k in range(CONV_A_W):
                    off = r0 + (CONV_A_W - 1) - k
                    e = eabuf[lb_i, off:off + CONV_CHUNK, :]
                    dcv = dcv + wa_ref[k:k + 1, lanes] * e
                    aacc[k, :, lanes] += jnp.sum((cv * e).reshape(CONV_CHUNK // SUBLANES, SUBLANES, LANES), axis=0)
                du_ref[rows, cols(D_A)] = (dcv * u_ref[rows, cols(2 * D_A)].astype(F32)).astype(BF16)
                du_ref[rows, cols(2 * D_A)] = (dcv * u_ref[rows, cols(D_A)].astype(F32)).astype(BF16)

                z = zbuf[lb_i, rows, :]
                dz = jnp.zeros((CONV_CHUNK, LANES), F32)
                for k in range(CONV_B_W):
                    off = r0 + (CONV_B_W - 1) - k
                    e = ebuf[lb_i, off:off + CONV_CHUNK, :]
                    dz = dz + wb_ref[k:k + 1, lanes] * e
                    wacc[k, :, lanes] += jnp.sum((z * e).reshape(CONV_CHUNK // SUBLANES, SUBLANES, LANES), axis=0)
                glu_v = u_ref[rows, cols(3 * D_A)].astype(F32)
                sgg = _sigmoid(u_ref[rows, cols(3 * D_A + D_B)].astype(F32))
                du_ref[rows, cols(3 * D_A)] = (dz * sgg).astype(BF16)
                du_ref[rows, cols(3 * D_A + D_B)] = (dz * glu_v * sgg * (1.0 - sgg)).astype(BF16)

            ebuf[lb_i, tc:tc + HALO_B, :] = ebuf[lb_i, 0:HALO_B, :]
            eabuf[lb_i, tc:tc + HALO_A, :] = eabuf[lb_i, 0:HALO_A, :]

        dh = dh + projected(len(pieces) - 1)
        dxn, dgp = _rms_bwd(dh, x_ref[...], gp_ref[...])
        dgp_ref[...] += dgp
        dx_ref[...] = dx1_ref[...] + dxn

        @pl.when(step == n_tiles - 1)
        def _():
            for k in range(CONV_B_W):
                dwb_ref[k:k + 1, :] = jnp.sum(wacc[k], axis=0, keepdims=True)
            for k in range(CONV_A_W):
                dwa_ref[k:k + 1, :] = jnp.sum(aacc[k], axis=0, keepdims=True)
            dbb_ref[...] = jnp.sum(vacc[0], axis=0, keepdims=True)
            dlg_ref[...] = jnp.sum(vacc[1], axis=0, keepdims=True)
            dlb_ref[...] = jnp.sum(vacc[2], axis=0, keepdims=True)

    return pl.pallas_call(
        core, name=name, grid=(n_tiles,), compiler_params=_params(),
        in_specs=[_rtile(tc, D_A + D_B, n_tiles), _rtile(tc, D_IN_ALL, n_tiles), _rtile(tc, D_B, n_tiles),
                  _rtile(tc, D_A, n_tiles), _whole((CONV_A_W, D_A)), _whole((CONV_B_W, D_B)), _whole((1, D_B)),
                  _whole((1, D_B)), _resident((D_IN_ALL, d)), _rtile(tc, d, n_tiles), _rtile(tc, d, n_tiles),
                  _whole((1, d))],
        out_specs=[_rtile(tc, D_IN_ALL, n_tiles), _whole((CONV_A_W, D_A)), _whole((CONV_B_W, D_B)), _whole((1, D_B)),
                   _whole((1, D_B)), _whole((1, D_B)), _rtile(tc, d, n_tiles), _whole((1, d))],
        out_shape=[_sds((t, D_IN_ALL), BF16), _sds((CONV_A_W, D_A), F32), _sds((CONV_B_W, D_B), F32), _sds((1, D_B), F32),
                   _sds((1, D_B), F32), _sds((1, D_B), F32), _sds((t, d), F32), _sds((1, d), F32)],
        scratch_shapes=[pltpu.VMEM((D_B // LANES, tc + HALO_B, LANES), F32), pltpu.VMEM((D_A // LANES, tc + HALO_A, LANES), F32),
                        pltpu.VMEM((D_B // LANES, tc, LANES), F32), pltpu.VMEM((D_A // LANES, tc, LANES), F32),
                        pltpu.VMEM((CONV_B_W, SUBLANES, D_B), F32), pltpu.VMEM((CONV_A_W, SUBLANES, D_A), F32),
                        pltpu.VMEM((3, SUBLANES, D_B), F32)],
    )(dy, u, zc, ca, wa, wb, lg, lb, w_in_pieces, x, dx1, g_pre)


def _wgrad(a, b, tk, bm, bn, name, comm=None, cols=None):
    t, m = a.shape
    first, n = (0, b.shape[1]) if cols is None else cols
    first_block = first // bn
    n_k = t // tk

    def body(a_ref, b_ref, o_ref, acc):
        @pl.when(pl.program_id(2) == 0)
        def _():
            acc[...] = jnp.zeros_like(acc)

        acc[...] += _dot_tn(a_ref[...], b_ref[...])

        @pl.when(pl.program_id(2) == n_k - 1)
        def _():
            o_ref[...] = acc[...].astype(BF16)

    call = dict(
        grid=(m // bm, n // bn, n_k),
        in_specs=[pl.BlockSpec((tk, bm), lambda i, j, k: (k, i)),
                  pl.BlockSpec((tk, bn), lambda i, j, k: (k, first_block + j))],
        out_specs=[pl.BlockSpec((bm, bn), lambda i, j, k: (i, j))],
        out_shape=[_sds((m, n), BF16)],
        scratch_shapes=[pltpu.VMEM((bm, bn), F32)])
    if comm is None:
        return pl.pallas_call(body, name=name, compiler_params=_params(3), **call)(a, b)[0]
    (out,), received = _hosted_call(body, comm, name, operands=(a, b), **call)
    return out, received


def _adamw(items, tr, name):
    n_parts, r, c = items[0][0].shape
    n_items = len(items)

    def body(*refs):
        ins, outs = refs[:4 * n_items], refs[4 * n_items:]
        for i in range(n_items):
            p_ref, w_ref, m_ref, v_ref = ins[4 * i:4 * i + 4]
            g_ref, d_ref, nm_ref, nv_ref = outs[4 * i:4 * i + 4]
            g = p_ref[0].astype(F32)
            for j in range(1, n_parts):
                g = g + p_ref[j].astype(F32)
            g_ref[...] = g
            nm = ADAM_B1 * m_ref[...] + (1.0 - ADAM_B1) * g
            nv = ADAM_B2 * v_ref[...] + (1.0 - ADAM_B2) * (g * g)
            nm_ref[...] = nm
            nv_ref[...] = nv
            m_hat = nm / (1.0 - ADAM_B1 ** ADAM_STEP)
            v_hat = nv / (1.0 - ADAM_B2 ** ADAM_STEP)
            d_ref[...] = -ADAM_LR * (m_hat / (jnp.sqrt(v_hat) + ADAM_EPS) + ADAM_WD * w_ref[...])

    results = pl.pallas_call(
        body, name=name, grid=(r // tr,),
        in_specs=([pl.BlockSpec((n_parts, tr, c), lambda i: (0, i, 0))] + [_tile(tr, c)] * 3) * n_items,
        out_specs=[_tile(tr, c)] * (4 * n_items),
        out_shape=[_sds((r, c), F32)] * (4 * n_items),
        compiler_params=_params(),
    )(*[a for item in items for a in item])
    return [results[4 * i:4 * i + 4] for i in range(n_items)]


def _sum_parts(parts, name):
    n_parts, r, c = parts.shape

    def body(p_ref, o_ref):
        acc = p_ref[0]
        for j in range(1, n_parts):
            acc = acc + p_ref[j]
        o_ref[...] = acc

    return pl.pallas_call(
        body, name=name, grid=(1,),
        in_specs=[_whole((n_parts, r, c))], out_specs=_whole((r, c)), out_shape=_sds((r, c), F32),
        compiler_params=_params(),
    )(parts)


def _row(v):
    return v.reshape(1, -1)


def _by_owner_rows(g):
    return g.reshape(N_DEV, g.shape[0] // N_DEV, g.shape[1])


WEIGHTS = ("mix_pre_g", "w_mix_in", "conv_a_w", "conv_b_w", "conv_b_b", "ln_b_g", "ln_b_b", "w_mix_out", "mix_post_g",
           "xa_pre_g", "mem_norm_g", "w_q", "w_k", "w_v", "w_o", "xa_post_g", "ffn_pre_g", "w_gate", "w_up", "w_down",
           "ffn_post_g")
LARGE = ("w_mix_in", "w_mix_out", "w_q", "w_k", "w_v", "w_o", "w_gate", "w_up", "w_down")
COLUMN_SHARDED = ("w_mix_in", "w_gate", "w_up")
GAINS = ("mix_pre_g", "mix_post_g", "xa_pre_g", "mem_norm_g", "xa_post_g", "ffn_pre_g", "ffn_post_g")
CHANNEL_VECTORS = ("conv_b_b", "ln_b_g", "ln_b_b")
CONV_TAPS = ("conv_a_w", "conv_b_w")
SMALL = GAINS + CHANNEL_VECTORS + CONV_TAPS
CONV_COLS_PER_DEVICE = D_A // N_DEV
TOKEN_TILE = 512
FFN_FWD_TOKEN_TILE = 512
FFN_TOKEN_TILE = 256
CONV_TOKEN_TILE = 256
WGRAD_TOKEN_TILE = 2048
ADAM_ROWS_PER_STEP = 64


def _lane_rows(v):
    flat = v.reshape(-1)
    tile = SUBLANES * LANES
    flat = jnp.pad(flat, (0, (-flat.shape[0]) % tile))
    return flat.reshape(-1, LANES)


def _pack_small(values, names):
    return jnp.concatenate([_lane_rows(values[n]) for n in names], axis=0)


def _unpack_small(packed, like, names):
    out, off = {}, 0
    for n in names:
        size = like[n].size
        rows = _lane_rows(like[n]).shape[0]
        out[n] = packed[off:off + rows, :].reshape(-1)[:size].reshape(like[n].shape)
        off += rows
    return out


def kernel(x, mem, mix_pre_g, w_mix_in, conv_a_w, conv_b_w, conv_b_b, ln_b_g, ln_b_b, w_mix_out, mix_post_g, xa_pre_g, mem_norm_g, w_q, w_k, w_v, w_o, xa_post_g, ffn_pre_g, w_gate, w_up, w_down, ffn_post_g, loss_target, m_mix_pre_g, m_w_mix_in, m_conv_a_w, m_conv_b_w, m_conv_b_b, m_ln_b_g, m_ln_b_b, m_w_mix_out, m_mix_post_g, m_xa_pre_g, m_mem_norm_g, m_w_q, m_w_k, m_w_v, m_w_o, m_xa_post_g, m_ffn_pre_g, m_w_gate, m_w_up, m_w_down, m_ffn_post_g, v_mix_pre_g, v_w_mix_in, v_conv_a_w, v_conv_b_w, v_conv_b_b, v_ln_b_g, v_ln_b_b, v_w_mix_out, v_mix_post_g, v_xa_pre_g, v_mem_norm_g, v_w_q, v_w_k, v_w_v, v_w_o, v_xa_post_g, v_ffn_pre_g, v_w_gate, v_w_up, v_w_down, v_ffn_post_g):
    given = dict(locals())
    w = {n: given[n] for n in WEIGHTS}
    m = {n: given["m_" + n] for n in WEIGHTS}
    v = {n: given["v_" + n] for n in WEIGHTS}
    xs, mems, target = x[0], mem[0], loss_target[0]
    t = xs.shape[0]
    tm, tm_ffn, tc, tk = min(TOKEN_TILE, t), min(FFN_TOKEN_TILE, t), min(CONV_TOKEN_TILE, t), min(WGRAD_TOKEN_TILE, t)
    g = {n: _row(w[n]) for n in GAINS}
    bb, lg, lb = (_row(w[n]) for n in CHANNEL_VECTORS)

    def shard_bf16(*names):
        return [w[n].astype(BF16) for n in names]

    def shard_bf16_t(*names):
        return [w[n].T.astype(BF16) for n in names]

    g_mix_in, g_taps = _comm_call(_Gather(shard_bf16_t("w_mix_in") + [_pack_small(w, CONV_TAPS)]), "gather_mixer")
    w_mix_in_t = g_mix_in.reshape(D_IN_ALL, D_MODEL)
    taps, off = {}, 0
    for n in CONV_TAPS:
        k, cols = w[n].shape
        rows = _lane_rows(w[n]).shape[0]
        blk = g_taps[:, off:off + rows, :].reshape(N_DEV, -1)[:, :k * cols].reshape(N_DEV, k, cols)
        taps[n] = blk.transpose(1, 0, 2).reshape(k, N_DEV * cols)
        off += rows
    wa, wb = taps["conv_a_w"], taps["conv_b_w"]

    (h1, u), gathered = _norm_matmul(xs, g["mix_pre_g"], w_mix_in_t.T, tm, "mix_in_fwd",
                                     _Gather(shard_bf16("w_mix_out", "w_q", "w_k", "w_v", "w_o")))
    w_mix_out, w_q, w_k, w_v, w_o = (a.reshape(D_MODEL, D_MODEL) for a in gathered)
    (ycat, zc, ca, x1, s1), (g_gate, g_up, g_down) = _conv_fwd(
        u, wa, wb, bb, lg, lb, xs, w_mix_out, g["mix_post_g"], tc, "conv_fwd",
        _Gather(shard_bf16_t("w_gate", "w_up") + shard_bf16("w_down")))
    w_gate_t, w_up_t, w_down = (a.reshape(D_FF, D_MODEL) for a in (g_gate, g_up, g_down))
    mem_n, kk, vv = _mem_fwd(mems, g["mem_norm_g"], w_k, w_v, "mem_fwd")
    x2, h2, q, o, a = _attn_fwd(x1, g["xa_pre_g"], w_q, kk, vv, w_o, g["xa_post_g"], tm, "attn_fwd")
    h3, gt, up, f, sq = _ffn_fwd(x2, g["ffn_pre_g"], w_gate_t, w_up_t, w_down, g["ffn_post_g"], target,
                                 min(FFN_FWD_TOKEN_TILE, t), "ffn_fwd")

    dx2, df, hd, dgt, dup, d_ffn_post, d_ffn_pre = _ffn_bwd(
        x2, f, target, gt, up, g["ffn_pre_g"], w_gate_t, w_up_t, w_down, g["ffn_post_g"], tm_ffn, "ffn_bwd")
    d_w_down = _wgrad(hd, df, tk, D_FF // 2, D_MODEL, "wgrad_down")
    d_w_gate_t = _wgrad(dgt, h3, tk, D_FF // 2, D_MODEL, "wgrad_gate")
    d_w_up_t = _wgrad(dup, h3, tk, D_FF // 2, D_MODEL, "wgrad_up")
    ffn_slabs = [_by_owner_rows(d) for d in (d_w_gate_t, d_w_up_t, d_w_down)]

    (dx1, da, dq, dk, dv, d_xa_post, d_xa_pre), from_ffn = _attn_bwd(
        dx2, a, x1, q, kk, vv, g["xa_pre_g"], w_q, w_o, g["xa_post_g"], tm, "attn_bwd",
        _Exchange(ffn_slabs, [True] * 3))
    d_w_o = _wgrad(o, da, tk, D_MODEL, D_MODEL, "wgrad_o")
    d_w_q = _wgrad(h2, dq, tk, D_MODEL, D_MODEL, "wgrad_q")
    d_w_k, d_w_v, d_mem_norm = _mem_bwd(mems, mem_n, dk, dv, g["mem_norm_g"], w_k, w_v, "mem_bwd")
    attn_slabs = [_by_owner_rows(d) for d in (d_w_q, d_w_k, d_w_v, d_w_o)]
    (ds1, dycat, d_mix_post), from_attn = _proj_bwd(dx1, s1, w_mix_out, g["mix_post_g"], tm, "mix_out_bwd",
                                                    _Exchange(attn_slabs, [True] * 4))
    d_w_mix_out = _wgrad(ycat, ds1, tk, D_MODEL, D_MODEL, "wgrad_mix_out")

    du, d_conv_a, d_conv_b, d_conv_bb, d_ln_g, d_ln_b, dx, d_mix_pre = _conv_bwd(
        dycat, u, zc, ca, wa, wb, lg, lb, _rows_by_du_piece(w_mix_in_t), xs, dx1, g["mix_pre_g"], tc, "conv_bwd")
    small_grads = dict(mix_pre_g=d_mix_pre, conv_a_w=d_conv_a, conv_b_w=d_conv_b, conv_b_b=d_conv_bb, ln_b_g=d_ln_g,
                       ln_b_b=d_ln_b, mix_post_g=d_mix_post, xa_pre_g=d_xa_pre, mem_norm_g=d_mem_norm, xa_post_g=d_xa_post,
                       ffn_pre_g=d_ffn_pre, ffn_post_g=d_ffn_post, loss=sq)
    names = SMALL + ("loss",)
    half = D_MODEL // 2
    d_in_lo, (from_mix_out,) = _wgrad(du, h1, tk, D_IN_ALL // 2, half, "wgrad_mix_in_lo",
                                      _Exchange([_by_owner_rows(d_w_mix_out)], [True]), cols=(0, half))
    d_in_hi, (from_in_lo, all_small) = _wgrad(
        du, h1, tk, D_IN_ALL // 2, half, "wgrad_mix_in_hi",
        _Exchange([_by_owner_rows(d_in_lo), _pack_small(small_grads, names)], [True, False]), cols=(half, half))
    (from_in_hi,) = _comm_call(_Exchange([_by_owner_rows(d_in_hi)], [True]), "reduce_tail")

    received = dict(zip(("w_gate", "w_up", "w_down"), from_ffn))
    received.update(zip(("w_q", "w_k", "w_v", "w_o"), from_attn))
    received["w_mix_out"] = from_mix_out
    received["w_mix_in"] = jnp.concatenate([from_in_lo, from_in_hi], axis=2)
    grad, delta, new_m, new_v = {}, {}, {}, {}
    for group in (("w_mix_in",), ("w_gate",), ("w_up",), ("w_down",), ("w_mix_out", "w_q", "w_k", "w_v", "w_o")):
        items = [[received[n]] + [a[n].T if n in COLUMN_SHARDED else a[n] for a in (w, m, v)] for n in group]
        rows = items[0][1].shape[0]
        results = _adamw(items, min(rows, ADAM_ROWS_PER_STEP) if len(group) > 1 else rows, "adamw_" + group[0])
        for n, result in zip(group, results):
            grad[n], delta[n], new_m[n], new_v[n] = [r.T if n in COLUMN_SHARDED else r for r in result]

    total = _unpack_small(_sum_parts(all_small, "sum_small"), small_grads, names)
    loss = jnp.sum(total.pop("loss")) * (0.5 / D_MODEL)
    first_col = _device_index() * CONV_COLS_PER_DEVICE
    for n in CONV_TAPS:
        total[n] = lax.dynamic_slice_in_dim(total[n], first_col, CONV_COLS_PER_DEVICE, axis=1)
    total = {n: total[n].reshape(w[n].shape) for n in SMALL}
    packed_small = [_pack_small(values, SMALL) for values in (total, w, m, v)]
    ((g_s, d_s, nm_s, nv_s),) = _adamw([[packed_small[0][None]] + packed_small[1:]], packed_small[0].shape[0],
                                       "adamw_small")
    for out, p in ((grad, g_s), (delta, d_s), (new_m, nm_s), (new_v, nv_s)):
        out.update(_unpack_small(p, w, SMALL))

    return (loss, dx[None], *[grad[n] for n in WEIGHTS], *[delta[n] for n in WEIGHTS], *[new_m[n] for n in WEIGHTS],
            *[new_v[n] for n in WEIGHTS])
```

```python
import functools

import jax
import jax.numpy as jnp
from jax import lax
from jax.experimental import pallas as pl
from jax.experimental.pallas import tpu as pltpu

F32 = jnp.float32
BF16 = jnp.bfloat16

D_MODEL = 1024
D_A = 512
D_B = 512
D_IN_ALL = 2560
CONV_A_W = 3
CONV_B_W = 31
XA_HEADS = 4
XA_HEAD_DIM = 256
D_FF = 2816
N_DEV = 8
RMS_EPS = 1e-6
LN_EPS = 1e-5
ADAM_LR = 0.001
ADAM_B1 = 0.9
ADAM_B2 = 0.999
ADAM_EPS = 1e-08
ADAM_WD = 0.01
ADAM_STEP = 10

VMEM_LIMIT_BYTES = 56 * 1024 * 1024
SUBLANES = 8
LANES = 128
HALO_B = 32
HALO_A = 8
CONV_CHUNK = 32

MESH = pl.DeviceIdType.MESH


def _params(n_grid_axes=1):
    return pltpu.CompilerParams(dimension_semantics=("arbitrary",) * n_grid_axes, vmem_limit_bytes=VMEM_LIMIT_BYTES)


def _sds(shape, dtype):
    return jax.ShapeDtypeStruct(shape, dtype)


def _tile(rows, cols):
    return pl.BlockSpec((rows, cols), lambda i: (i, 0))


def _rtile(rows, cols, n):
    return pl.BlockSpec((rows, cols), lambda i: (n - 1 - i, 0))


def _whole(shape):
    zeros = (0,) * len(shape)
    return pl.BlockSpec(shape, lambda i: zeros)


def _resident(shape):
    zeros = (0,) * len(shape)
    return pl.BlockSpec(shape, lambda i: zeros, pipeline_mode=pl.Buffered(1))


def _dot(a, b):
    return jnp.dot(a, b, preferred_element_type=F32)


def _dot_nt(a, b):
    return lax.dot_general(a, b, (((1,), (1,)), ((), ())), preferred_element_type=F32)


def _dot_tn(a, b):
    return lax.dot_general(a, b, (((0,), (0,)), ((), ())), preferred_element_type=F32)


def _sigmoid(x):
    return 1.0 / (1.0 + jnp.exp(-x))


def _rms_fwd(x, g):
    r = lax.rsqrt(jnp.mean(x * x, axis=-1, keepdims=True) + RMS_EPS)
    return x * r * g


def _rms_bwd(dy, xin, g):
    r = lax.rsqrt(jnp.mean(xin * xin, axis=-1, keepdims=True) + RMS_EPS)
    n = xin * r
    dg = jnp.sum(dy * n, axis=0, keepdims=True)
    dn = dy * g
    dx = r * (dn - n * jnp.mean(dn * n, axis=-1, keepdims=True))
    return dx, dg


def _zero_at_first_step(*refs):
    @pl.when(pl.program_id(0) == 0)
    def _():
        for ref in refs:
            ref[...] = jnp.zeros(ref.shape, ref.dtype)


def _place():
    return lax.axis_index("x"), lax.axis_index("y"), lax.axis_index("c")


def _device_index():
    x, y, c = _place()
    return 4 * x + 2 * y + c


class _Gather:
    def __init__(self, arrays):
        self.arrays = list(arrays)
        self.out_shape = [_sds((N_DEV, *a.shape), a.dtype) for a in self.arrays]
        n = len(self.arrays)
        self.scratch_shapes = [pltpu.SemaphoreType.DMA((n, 7)), pltpu.SemaphoreType.DMA((n, 7)),
                               pltpu.SemaphoreType.DMA((n,))]

    def forward_step(self, n_steps):
        return max(0, n_steps - 9)

    def bind(self, srcs, dsts, send_sems, recv_sems, local_sems):
        x, y, cc = _place()
        me, sibling = (x, y, cc), (x, y, 1 - cc)
        chips = [(1 - x, y), (x, 1 - y), (1 - x, 1 - y)]

        def copy(a, k, owner, to, src=None):
            slot = dsts[a].at[4 * owner[0] + 2 * owner[1] + owner[2]]
            return pltpu.make_async_remote_copy(
                src_ref=slot if src is None else src, dst_ref=slot, send_sem=send_sems.at[a, k],
                recv_sem=recv_sems.at[a, k], device_id=to, device_id_type=MESH)

        def first(a):
            return [copy(a, 0, me, sibling, src=srcs[a])] + [
                copy(a, 1 + j, me, (*chip, cc), src=srcs[a]) for j, chip in enumerate(chips)]

        def passed(a, j):
            return copy(a, 4 + j, (*chips[j], cc), sibling)

        def mine(a):
            return pltpu.make_async_copy(srcs[a], dsts[a].at[4 * x + 2 * y + cc], local_sems.at[a])

        def start():
            for a in range(len(srcs)):
                mine(a).start()
                for cp in first(a):
                    cp.start()

        def forward():
            for a in range(len(srcs)):
                for j, chip in enumerate(chips):
                    copy(a, 1 + j, (*chip, cc), me).wait_recv()
                    passed(a, j).start()

        def finish():
            for a in range(len(srcs)):
                copy(a, 0, sibling, me).wait_recv()
                for j, chip in enumerate(chips):
                    copy(a, 4 + j, (*chip, 1 - cc), me).wait_recv()
                for cp in first(a) + [passed(a, j) for j in range(len(chips))]:
                    cp.wait_send()
                mine(a).wait()

        return start, forward, finish


class _Exchange:
    def __init__(self, arrays, scatter):
        self.arrays = list(arrays)
        self.scatter = list(scatter)
        self.out_shape = [_sds(a.shape if s else (N_DEV, *a.shape), a.dtype) for a, s in zip(self.arrays, self.scatter)]
        n = len(self.arrays)
        self.scratch_shapes = [pltpu.SemaphoreType.DMA((n, 7)), pltpu.SemaphoreType.DMA((n, 7)),
                               pltpu.SemaphoreType.DMA((n,))]

    def forward_step(self, n_steps):
        return n_steps - 1

    def bind(self, srcs, dsts, send_sems, recv_sems, local_sems):
        me = _device_index()

        def copies(a):
            out = []
            for k in range(1, N_DEV):
                p = me ^ k
                out.append(pltpu.make_async_remote_copy(
                    src_ref=srcs[a].at[p] if self.scatter[a] else srcs[a], dst_ref=dsts[a].at[me],
                    send_sem=send_sems.at[a, k - 1], recv_sem=recv_sems.at[a, k - 1],
                    device_id=(p >> 2, (p >> 1) & 1, p & 1), device_id_type=MESH))
            return out

        def mine(a):
            return pltpu.make_async_copy(srcs[a].at[me] if self.scatter[a] else srcs[a], dsts[a].at[me], local_sems.at[a])

        def start():
            for a in range(len(srcs)):
                mine(a).start()
                for cp in copies(a):
                    cp.start()

        def forward():
            pass

        def finish():
            for a in range(len(srcs)):
                for cp in copies(a):
                    cp.wait()
                mine(a).wait()

        return start, forward, finish


def _hosted_call(core, comm, name, grid, in_specs, out_specs, out_shape, scratch_shapes, operands):
    grid = (grid,) if isinstance(grid, int) else tuple(grid)
    n_steps = 1
    for extent in grid:
        n_steps *= extent
    n_in, n_out, n_scr, n_arr = len(in_specs), len(out_specs), len(scratch_shapes), len(comm.arrays)
    any_spec = pl.BlockSpec(memory_space=pl.ANY)

    def body(*refs):
        ins, refs = refs[:n_in], refs[n_in:]
        srcs, refs = refs[:n_arr], refs[n_arr:]
        outs, refs = refs[:n_out], refs[n_out:]
        dsts, refs = refs[:n_arr], refs[n_arr:]
        scratch, sems = refs[:n_scr], refs[n_scr:]
        start, forward, finish = comm.bind(srcs, dsts, *sems)
        step = pl.program_id(0)
        for axis in range(1, len(grid)):
            step = step * grid[axis] + pl.program_id(axis)
        pl.when(step == 0)(start)
        core(*ins, *outs, *scratch)
        pl.when(step == comm.forward_step(n_steps))(forward)
        pl.when(step == n_steps - 1)(finish)

    results = pl.pallas_call(
        body, name=name, grid=grid,
        in_specs=list(in_specs) + [any_spec] * n_arr,
        out_specs=list(out_specs) + [any_spec] * n_arr,
        out_shape=list(out_shape) + comm.out_shape,
        scratch_shapes=list(scratch_shapes) + comm.scratch_shapes,
        compiler_params=_params(len(grid)),
    )(*operands, *comm.arrays)
    return results[:n_out], results[n_out:]


def _comm_call(comm, name):
    return _hosted_call(lambda: None, comm, name, 1, [], [], [], [], [])[1]


def _norm_matmul(x, g, w, tm, name, comm):
    t, d = x.shape
    n = w.shape[1]

    def core(x_ref, g_ref, w_ref, h_ref, o_ref):
        h = _rms_fwd(x_ref[...], g_ref[...]).astype(BF16)
        h_ref[...] = h
        o_ref[...] = _dot(h, w_ref[...]).astype(BF16)

    return _hosted_call(
        core, comm, name, t // tm,
        in_specs=[_tile(tm, d), _whole((1, d)), _resident((d, n))],
        out_specs=[_tile(tm, d), _tile(tm, n)],
        out_shape=[_sds((t, d), BF16), _sds((t, n), BF16)],
        scratch_shapes=[], operands=(x, g, w))


def _conv_fwd(u, wa, wb, bb, lg, lb, xres, w_out, g_post, tc, name, comm):
    t = u.shape[0]
    d = w_out.shape[1]
    n_chunks = tc // CONV_CHUNK
    piece = 2 * LANES

    def core(u_ref, wa_ref, wb_ref, bb_ref, lg_ref, lb_ref, x_ref, wo_ref, go_ref,
             y_ref, zc_ref, ca_ref, x1_ref, s_ref, zbuf, cvbuf, zcbuf):
        @pl.when(pl.program_id(0) == 0)
        def _():
            zbuf[:, 0:HALO_B, :] = jnp.zeros((D_B // LANES, HALO_B, LANES), F32)
            cvbuf[:, 0:HALO_A, :] = jnp.zeros((D_A // LANES, HALO_A, LANES), F32)

        s = None
        for lb_i in range(D_A // LANES):
            if lb_i > 0 and lb_i % 2 == 0:
                cols = slice((lb_i - 2) * LANES, lb_i * LANES)
                part = _dot(y_ref[:, cols], wo_ref[cols, :])
                s = part if s is None else s + part
            lanes = slice(lb_i * LANES, (lb_i + 1) * LANES)
            c_a = u_ref[:, D_A + lb_i * LANES:D_A + (lb_i + 1) * LANES].astype(F32)
            v_a = u_ref[:, 2 * D_A + lb_i * LANES:2 * D_A + (lb_i + 1) * LANES].astype(F32)
            cvbuf[lb_i, HALO_A:HALO_A + tc, :] = c_a * v_a
            glu_v = u_ref[:, 3 * D_A + lb_i * LANES:3 * D_A + (lb_i + 1) * LANES].astype(F32)
            glu_g = u_ref[:, 3 * D_A + D_B + lb_i * LANES:3 * D_A + D_B + (lb_i + 1) * LANES].astype(F32)
            zbuf[lb_i, HALO_B:HALO_B + tc, :] = glu_v * _sigmoid(glu_g)

            for c in range(n_chunks):
                r0 = c * CONV_CHUNK
                rows = slice(r0, r0 + CONV_CHUNK)
                acc = jnp.zeros((CONV_CHUNK, LANES), F32)
                for k in range(CONV_A_W):
                    off = r0 + HALO_A - (CONV_A_W - 1) + k
                    acc = acc + wa_ref[k:k + 1, lanes] * cvbuf[lb_i, off:off + CONV_CHUNK, :]
                ca_ref[rows, lanes] = acc.astype(BF16)
                y_ref[rows, lanes] = (u_ref[rows, lanes].astype(F32) * acc).astype(BF16)

                accb = jnp.zeros((CONV_CHUNK, LANES), F32)
                for k in range(CONV_B_W):
                    off = r0 + HALO_B - (CONV_B_W - 1) + k
                    accb = accb + wb_ref[k:k + 1, lanes] * zbuf[lb_i, off:off + CONV_CHUNK, :]
                zcbuf[rows, lanes] = accb + bb_ref[:, lanes]

            zbuf[lb_i, 0:HALO_B, :] = zbuf[lb_i, tc:tc + HALO_B, :]
            cvbuf[lb_i, 0:HALO_A, :] = cvbuf[lb_i, tc:tc + HALO_A, :]

        cols = slice(D_A - piece, D_A)
        s = s + _dot(y_ref[:, cols], wo_ref[cols, :])

        for c in range(n_chunks):
            rows = slice(c * CONV_CHUNK, (c + 1) * CONV_CHUNK)
            zc = zcbuf[rows, :]
            zc_ref[rows, :] = zc.astype(BF16)
            mu = jnp.mean(zc, axis=-1, keepdims=True)
            xc = zc - mu
            var = jnp.mean(xc * xc, axis=-1, keepdims=True)
            ln = xc * lax.rsqrt(var + LN_EPS) * lg_ref[...] + lb_ref[...]
            y_ref[rows, D_A:D_A + D_B] = (ln * _sigmoid(ln)).astype(BF16)

        s = s + _dot(y_ref[:, D_A:D_A + D_B], wo_ref[D_A:D_A + D_B, :])
        s_ref[...] = s.astype(BF16)
        x1_ref[...] = x_ref[...] + _rms_fwd(s, go_ref[...])

    return _hosted_call(
        core, comm, name, t // tc,
        in_specs=[_tile(tc, D_IN_ALL), _whole((CONV_A_W, D_A)), _whole((CONV_B_W, D_B)), _whole((1, D_B)),
                  _whole((1, D_B)), _whole((1, D_B)), _tile(tc, d), _resident((D_A + D_B, d)), _whole((1, d))],
        out_specs=[_tile(tc, D_A + D_B), _tile(tc, D_B), _tile(tc, D_A), _tile(tc, d), _tile(tc, d)],
        out_shape=[_sds((t, D_A + D_B), BF16), _sds((t, D_B), BF16), _sds((t, D_A), BF16), _sds((t, d), F32),
                   _sds((t, d), BF16)],
        scratch_shapes=[pltpu.VMEM((D_B // LANES, HALO_B + tc, LANES), F32),
                        pltpu.VMEM((D_A // LANES, HALO_A + tc, LANES), F32), pltpu.VMEM((tc, D_B), F32)],
        operands=(u, wa, wb, bb, lg, lb, xres, w_out, g_post))


def _mem_fwd(mem, g, wk, wv, name):
    m, d = mem.shape

    def body(mem_ref, g_ref, wk_ref, wv_ref, n_ref, k_ref, v_ref):
        n = _rms_fwd(mem_ref[...], g_ref[...]).astype(BF16)
        n_ref[...] = n
        k_ref[...] = _dot(n, wk_ref[...]).astype(BF16)
        v_ref[...] = _dot(n, wv_ref[...]).astype(BF16)

    return pl.pallas_call(
        body, name=name, grid=(1,),
        in_specs=[_whole((m, d)), _whole((1, d)), _whole((d, d)), _whole((d, d))],
        out_specs=[_whole((m, d))] * 3,
        out_shape=[_sds((m, d), BF16)] * 3,
        compiler_params=_params(),
    )(mem, g, wk, wv)


def _softmax_rows(s):
    e = jnp.exp(s - jnp.max(s, axis=-1, keepdims=True))
    return e / jnp.sum(e, axis=-1, keepdims=True)


def _attn_fwd(x1, g_pre, wq, k, v, wo, g_post, tm, name):
    t, d = x1.shape
    m = k.shape[0]
    scale = XA_HEAD_DIM ** -0.5

    def body(x_ref, gp_ref, wq_ref, k_ref, v_ref, wo_ref, go_ref, x2_ref, h_ref, q_ref, o_ref, a_ref):
        x = x_ref[...]
        h = _rms_fwd(x, gp_ref[...]).astype(BF16)
        h_ref[...] = h
        q_ref[...] = _dot(h, wq_ref[...]).astype(BF16)
        for hd in range(XA_HEADS):
            cols = slice(hd * XA_HEAD_DIM, (hd + 1) * XA_HEAD_DIM)
            p = _softmax_rows(_dot_nt(q_ref[:, cols], k_ref[:, cols]) * scale)
            o_ref[:, cols] = _dot(p.astype(BF16), v_ref[:, cols]).astype(BF16)
        a = _dot(o_ref[...], wo_ref[...])
        a_ref[...] = a.astype(BF16)
        x2_ref[...] = x + _rms_fwd(a, go_ref[...])

    return pl.pallas_call(
        body, name=name, grid=(t // tm,),
        in_specs=[_tile(tm, d), _whole((1, d)), _resident((d, d)), _whole((m, d)), _whole((m, d)), _resident((d, d)),
                  _whole((1, d))],
        out_specs=[_tile(tm, d)] * 5,
        out_shape=[_sds((t, d), F32)] + [_sds((t, d), BF16)] * 4,
        compiler_params=_params(),
    )(x1, g_pre, wq, k, v, wo, g_post)


def _ffn_fwd(x2, g_pre, wg_t, wu_t, wd, g_post, target, tm, name):
    t, d = x2.shape
    f = wg_t.shape[0]

    def body(x_ref, gp_ref, wg_ref, wu_ref, wd_ref, go_ref, tgt_ref, h_ref, gt_ref, up_ref, f_ref, sq_ref):
        _zero_at_first_step(sq_ref)
        x = x_ref[...]
        h = _rms_fwd(x, gp_ref[...]).astype(BF16)
        h_ref[...] = h
        gt = _dot_nt(h, wg_ref[...])
        up = _dot_nt(h, wu_ref[...])
        gt_ref[...] = gt.astype(BF16)
        up_ref[...] = up.astype(BF16)
        hd = (gt * _sigmoid(gt) * up).astype(BF16)
        ff = _dot(hd, wd_ref[...])
        f_ref[...] = ff.astype(BF16)
        err = x + _rms_fwd(ff, go_ref[...]) - tgt_ref[...]
        sq_ref[...] += jnp.sum(err * err, axis=0, keepdims=True)

    return pl.pallas_call(
        body, name=name, grid=(t // tm,),
        in_specs=[_tile(tm, d), _whole((1, d)), _resident((f, d)), _resident((f, d)), _resident((f, d)), _whole((1, d)),
                  _tile(tm, d)],
        out_specs=[_tile(tm, d), _tile(tm, f), _tile(tm, f), _tile(tm, d), _whole((1, d))],
        out_shape=[_sds((t, d), BF16), _sds((t, f), BF16), _sds((t, f), BF16), _sds((t, d), BF16), _sds((1, d), F32)],
        compiler_params=_params(),
    )(x2, g_pre, wg_t, wu_t, wd, g_post, target)


def _ffn_bwd(x2, f, target, gt, up, g_pre, wg_t, wu_t, wd, g_post, tm, name):
    t, d = x2.shape
    ff = wg_t.shape[0]

    def body(x_ref, f_ref, tgt_ref, gt_ref, up_ref, gp_ref, wg_ref, wu_ref, wd_ref, go_ref,
             dx_ref, df_ref, hd_ref, dgt_ref, dup_ref, dgo_ref, dgp_ref):
        _zero_at_first_step(dgo_ref, dgp_ref)
        x = x_ref[...]
        fo = f_ref[...].astype(F32)
        dx3 = (x + _rms_fwd(fo, go_ref[...]) - tgt_ref[...]) * (1.0 / d)
        df, dgo = _rms_bwd(dx3, fo, go_ref[...])
        dgo_ref[...] += dgo
        df = df.astype(BF16)
        df_ref[...] = df
        dhd = _dot_nt(df, wd_ref[...])
        gt = gt_ref[...].astype(F32)
        up = up_ref[...].astype(F32)
        sg = _sigmoid(gt)
        si = gt * sg
        hd_ref[...] = (si * up).astype(BF16)
        dup = (dhd * si).astype(BF16)
        dgt = (dhd * up * (sg * (1.0 + gt * (1.0 - sg)))).astype(BF16)
        dup_ref[...] = dup
        dgt_ref[...] = dgt
        dh = _dot(dgt, wg_ref[...]) + _dot(dup, wu_ref[...])
        dxn, dgp = _rms_bwd(dh, x, gp_ref[...])
        dgp_ref[...] += dgp
        dx_ref[...] = dx3 + dxn

    return pl.pallas_call(
        body, name=name, grid=(t // tm,),
        in_specs=[_tile(tm, d), _tile(tm, d), _tile(tm, d), _tile(tm, ff), _tile(tm, ff), _whole((1, d)),
                  _resident((ff, d)), _resident((ff, d)), _resident((ff, d)), _whole((1, d))],
        out_specs=[_tile(tm, d), _tile(tm, d), _tile(tm, ff), _tile(tm, ff), _tile(tm, ff), _whole((1, d)), _whole((1, d))],
        out_shape=[_sds((t, d), F32), _sds((t, d), BF16), _sds((t, ff), BF16), _sds((t, ff), BF16), _sds((t, ff), BF16),
                   _sds((1, d), F32), _sds((1, d), F32)],
        compiler_params=_params(),
    )(x2, f, target, gt, up, g_pre, wg_t, wu_t, wd, g_post)


def _attn_bwd(dx2, a, x1, q, k, v, g_pre, wq, wo, g_post, tm, name, comm):
    t, d = x1.shape
    m = k.shape[0]
    scale = XA_HEAD_DIM ** -0.5

    def core(dx2_ref, a_ref, x_ref, q_ref, k_ref, v_ref, gp_ref, wq_ref, wo_ref, go_ref,
             dx1_ref, da_ref, dq_ref, dk_ref, dv_ref, dgo_ref, dgp_ref, do_buf):
        _zero_at_first_step(dgo_ref, dgp_ref, dk_ref, dv_ref)
        dx2 = dx2_ref[...]
        da, dgo = _rms_bwd(dx2, a_ref[...].astype(F32), go_ref[...])
        dgo_ref[...] += dgo
        da = da.astype(BF16)
        da_ref[...] = da
        do_buf[...] = _dot_nt(da, wo_ref[...]).astype(BF16)
        for hd in range(XA_HEADS):
            cols = slice(hd * XA_HEAD_DIM, (hd + 1) * XA_HEAD_DIM)
            qh = q_ref[:, cols]
            p = _softmax_rows(_dot_nt(qh, k_ref[:, cols]) * scale)
            do_h = do_buf[:, cols]
            dp = _dot_nt(do_h, v_ref[:, cols])
            dv_ref[:, cols] += _dot_tn(p.astype(BF16), do_h)
            ds = (p * (dp - jnp.sum(dp * p, axis=-1, keepdims=True)) * scale).astype(BF16)
            dq_ref[:, cols] = _dot(ds, k_ref[:, cols]).astype(BF16)
            dk_ref[:, cols] += _dot_tn(ds, qh)
        dh = _dot_nt(dq_ref[...], wq_ref[...])
        dxn, dgp = _rms_bwd(dh, x_ref[...], gp_ref[...])
        dgp_ref[...] += dgp
        dx1_ref[...] = dx2 + dxn

    return _hosted_call(
        core, comm, name, t // tm,
        in_specs=[_tile(tm, d), _tile(tm, d), _tile(tm, d), _tile(tm, d), _whole((m, d)), _whole((m, d)), _whole((1, d)),
                  _resident((d, d)), _resident((d, d)), _whole((1, d))],
        out_specs=[_tile(tm, d), _tile(tm, d), _tile(tm, d), _whole((m, d)), _whole((m, d)), _whole((1, d)), _whole((1, d))],
        out_shape=[_sds((t, d), F32), _sds((t, d), BF16), _sds((t, d), BF16), _sds((m, d), F32), _sds((m, d), F32),
                   _sds((1, d), F32), _sds((1, d), F32)],
        scratch_shapes=[pltpu.VMEM((tm, d), BF16)],
        operands=(dx2, a, x1, q, k, v, g_pre, wq, wo, g_post))


def _mem_bwd(mem, mem_n, dk, dv, g, wk, wv, name):
    m, d = mem.shape

    def body(mem_ref, n_ref, dk_ref, dv_ref, g_ref, wk_ref, wv_ref, dwk_ref, dwv_ref, dg_ref):
        dk = dk_ref[...].astype(BF16)
        dv = dv_ref[...].astype(BF16)
        n = n_ref[...]
        dwk_ref[...] = _dot_tn(n, dk).astype(BF16)
        dwv_ref[...] = _dot_tn(n, dv).astype(BF16)
        dn = _dot_nt(dk, wk_ref[...]) + _dot_nt(dv, wv_ref[...])
        _, dg = _rms_bwd(dn, mem_ref[...], g_ref[...])
        dg_ref[...] = dg

    return pl.pallas_call(
        body, name=name, grid=(1,),
        in_specs=[_whole((m, d)), _whole((m, d)), _whole((m, d)), _whole((m, d)), _whole((1, d)), _whole((d, d)),
                  _whole((d, d))],
        out_specs=[_whole((d, d)), _whole((d, d)), _whole((1, d))],
        out_shape=[_sds((d, d), BF16), _sds((d, d), BF16), _sds((1, d), F32)],
        compiler_params=_params(),
    )(mem, mem_n, dk, dv, g, wk, wv)


def _proj_bwd(dxo, s, w, g, tm, name):
    t, d = dxo.shape
    k = w.shape[0]

    def body(dx_ref, s_ref, w_ref, g_ref, ds_ref, da_ref, dg_ref):
        _zero_at_first_step(dg_ref)
        ds, dg = _rms_bwd(dx_ref[...], s_ref[...].astype(F32), g_ref[...])
        dg_ref[...] += dg
        ds = ds.astype(BF16)
        ds_ref[...] = ds
        da_ref[...] = _dot_nt(ds, w_ref[...]).astype(BF16)

    return pl.pallas_call(
        body, name=name, grid=(t // tm,),
        in_specs=[_tile(tm, d), _tile(tm, d), _resident((k, d)), _whole((1, d))],
        out_specs=[_tile(tm, d), _tile(tm, k), _whole((1, d))],
        out_shape=[_sds((t, d), BF16), _sds((t, k), BF16), _sds((1, d), F32)],
        compiler_params=_params(),
    )(dxo, s, w, g)


def _conv_bwd(dy, u, zc, ca, wa, wb, lg, lb, tc, name, comm):
    t = u.shape[0]
    n_tiles = t // tc
    n_chunks = tc // CONV_CHUNK

    def core(dy_ref, u_ref, zc_ref, ca_ref, wa_ref, wb_ref, lg_ref, lb_ref,
             du_ref, dwa_ref, dwb_ref, dbb_ref, dlg_ref, dlb_ref, ebuf, eabuf, zbuf, cvbuf, wacc, aacc, vacc):
        step = pl.program_id(0)

        @pl.when(step == 0)
        def _():
            ebuf[:, tc:tc + HALO_B, :] = jnp.zeros((D_B // LANES, HALO_B, LANES), F32)
            eabuf[:, tc:tc + HALO_A, :] = jnp.zeros((D_A // LANES, HALO_A, LANES), F32)
            wacc[...] = jnp.zeros_like(wacc)
            aacc[...] = jnp.zeros_like(aacc)
            vacc[...] = jnp.zeros_like(vacc)

        dbb = jnp.zeros((SUBLANES, D_B), F32)
        dlg = jnp.zeros((SUBLANES, D_B), F32)
        dlb = jnp.zeros((SUBLANES, D_B), F32)
        for c in range(n_chunks):
            rows = slice(c * CONV_CHUNK, (c + 1) * CONV_CHUNK)
            dy_a = dy_ref[rows, 0:D_A].astype(F32)
            b_a = u_ref[rows, 0:D_A].astype(F32)
            du_ref[rows, 0:D_A] = (dy_a * ca_ref[rows, :].astype(F32)).astype(BF16)
            dca = dy_a * b_a
            cv = u_ref[rows, D_A:2 * D_A].astype(F32) * u_ref[rows, 2 * D_A:3 * D_A].astype(F32)
            glu_g = u_ref[rows, 3 * D_A + D_B:3 * D_A + 2 * D_B].astype(F32)
            z = u_ref[rows, 3 * D_A:3 * D_A + D_B].astype(F32) * _sigmoid(glu_g)
            for lb_i in range(D_A // LANES):
                lanes = slice(lb_i * LANES, (lb_i + 1) * LANES)
                eabuf[lb_i, rows, :] = dca[:, lanes]
                cvbuf[lb_i, rows, :] = cv[:, lanes]
                zbuf[lb_i, rows, :] = z[:, lanes]

            zcv = zc_ref[rows, :].astype(F32)
            mu = jnp.mean(zcv, axis=-1, keepdims=True)
            xc = zcv - mu
            rstd = lax.rsqrt(jnp.mean(xc * xc, axis=-1, keepdims=True) + LN_EPS)
            xhat = xc * rstd
            ln = xhat * lg_ref[...] + lb_ref[...]
            sg = _sigmoid(ln)
            dln = dy_ref[rows, D_A:D_A + D_B].astype(F32) * (sg * (1.0 + ln * (1.0 - sg)))
            dlg = dlg + jnp.sum((dln * xhat).reshape(CONV_CHUNK // SUBLANES, SUBLANES, D_B), axis=0)
            dlb = dlb + jnp.sum(dln.reshape(CONV_CHUNK // SUBLANES, SUBLANES, D_B), axis=0)
            dxh = dln * lg_ref[...]
            dzc = rstd * (dxh - jnp.mean(dxh, axis=-1, keepdims=True) - xhat * jnp.mean(dxh * xhat, axis=-1, keepdims=True))
            dbb = dbb + jnp.sum(dzc.reshape(CONV_CHUNK // SUBLANES, SUBLANES, D_B), axis=0)
            for lb_i in range(D_B // LANES):
                ebuf[lb_i, rows, :] = dzc[:, lb_i * LANES:(lb_i + 1) * LANES]
        vacc[0] += dbb
        vacc[1] += dlg
        vacc[2] += dlb

        for lb_i in range(D_A // LANES):
            lanes = slice(lb_i * LANES, (lb_i + 1) * LANES)

            def cols(first):
                return slice(first + lb_i * LANES, first + (lb_i + 1) * LANES)

            for c in range(n_chunks):
                r0 = c * CONV_CHUNK
                rows = slice(r0, r0 + CONV_CHUNK)
                cv = cvbuf[lb_i, rows, :]
                dcv = jnp.zeros((CONV_CHUNK, LANES), F32)
                for k in range(CONV_A_W):
                    off = r0 + (CONV_A_W - 1) - k
                    e = eabuf[lb_i, off:off + CONV_CHUNK, :]
                    dcv = dcv + wa_ref[k:k + 1, lanes] * e
                    aacc[k, :, lanes] += jnp.sum((cv * e).reshape(CONV_CHUNK // SUBLANES, SUBLANES, LANES), axis=0)
                du_ref[rows, cols(D_A)] = (dcv * u_ref[rows, cols(2 * D_A)].astype(F32)).astype(BF16)
                du_ref[rows, cols(2 * D_A)] = (dcv * u_ref[rows, cols(D_A)].astype(F32)).astype(BF16)

                z = zbuf[lb_i, rows, :]
                dz = jnp.zeros((CONV_CHUNK, LANES), F32)
                for k in range(CONV_B_W):
                    off = r0 + (CONV_B_W - 1) - k
                    e = ebuf[lb_i, off:off + CONV_CHUNK, :]
                    dz = dz + wb_ref[k:k + 1, lanes] * e
                    wacc[k, :, lanes] += jnp.sum((z * e).reshape(CONV_CHUNK // SUBLANES, SUBLANES, LANES), axis=0)
                glu_v = u_ref[rows, cols(3 * D_A)].astype(F32)
                sgg = _sigmoid(u_ref[rows, cols(3 * D_A + D_B)].astype(F32))
                du_ref[rows, cols(3 * D_A)] = (dz * sgg).astype(BF16)
                du_ref[rows, cols(3 * D_A + D_B)] = (dz * glu_v * sgg * (1.0 - sgg)).astype(BF16)

            ebuf[lb_i, tc:tc + HALO_B, :] = ebuf[lb_i, 0:HALO_B, :]
            eabuf[lb_i, tc:tc + HALO_A, :] = eabuf[lb_i, 0:HALO_A, :]

        @pl.when(step == n_tiles - 1)
        def _():
            for k in range(CONV_B_W):
                dwb_ref[k:k + 1, :] = jnp.sum(wacc[k], axis=0, keepdims=True)
            for k in range(CONV_A_W):
                dwa_ref[k:k + 1, :] = jnp.sum(aacc[k], axis=0, keepdims=True)
            dbb_ref[...] = jnp.sum(vacc[0], axis=0, keepdims=True)
            dlg_ref[...] = jnp.sum(vacc[1], axis=0, keepdims=True)
            dlb_ref[...] = jnp.sum(vacc[2], axis=0, keepdims=True)

    return _hosted_call(
        core, comm, name, n_tiles,
        in_specs=[_rtile(tc, D_A + D_B, n_tiles), _rtile(tc, D_IN_ALL, n_tiles), _rtile(tc, D_B, n_tiles),
                  _rtile(tc, D_A, n_tiles), _whole((CONV_A_W, D_A)), _whole((CONV_B_W, D_B)), _whole((1, D_B)),
                  _whole((1, D_B))],
        out_specs=[_rtile(tc, D_IN_ALL, n_tiles), _whole((CONV_A_W, D_A)), _whole((CONV_B_W, D_B)), _whole((1, D_B)),
                   _whole((1, D_B)), _whole((1, D_B))],
        out_shape=[_sds((t, D_IN_ALL), BF16), _sds((CONV_A_W, D_A), F32), _sds((CONV_B_W, D_B), F32), _sds((1, D_B), F32),
                   _sds((1, D_B), F32), _sds((1, D_B), F32)],
        scratch_shapes=[pltpu.VMEM((D_B // LANES, tc + HALO_B, LANES), F32), pltpu.VMEM((D_A // LANES, tc + HALO_A, LANES), F32),
                        pltpu.VMEM((D_B // LANES, tc, LANES), F32), pltpu.VMEM((D_A // LANES, tc, LANES), F32),
                        pltpu.VMEM((CONV_B_W, SUBLANES, D_B), F32), pltpu.VMEM((CONV_A_W, SUBLANES, D_A), F32),
                        pltpu.VMEM((3, SUBLANES, D_B), F32)],
        operands=(dy, u, zc, ca, wa, wb, lg, lb))


def _in_bwd(du, w_t, x, dx1, g, tm, name):
    t, d = x.shape
    n = w_t.shape[0]

    def body(du_ref, w_ref, x_ref, dx1_ref, g_ref, dx_ref, dg_ref):
        _zero_at_first_step(dg_ref)
        dh = _dot(du_ref[...], w_ref[...])
        dxn, dg = _rms_bwd(dh, x_ref[...], g_ref[...])
        dg_ref[...] += dg
        dx_ref[...] = dx1_ref[...] + dxn

    return pl.pallas_call(
        body, name=name, grid=(t // tm,),
        in_specs=[_tile(tm, n), _resident((n, d)), _tile(tm, d), _tile(tm, d), _whole((1, d))],
        out_specs=[_tile(tm, d), _whole((1, d))],
        out_shape=[_sds((t, d), F32), _sds((1, d), F32)],
        compiler_params=_params(),
    )(du, w_t, x, dx1, g)


def _wgrad(a, b, tk, bm, bn, name, comm=None, cols=None):
    t, m = a.shape
    first, n = (0, b.shape[1]) if cols is None else cols
    first_block = first // bn
    n_k = t // tk

    def body(a_ref, b_ref, o_ref, acc):
        @pl.when(pl.program_id(2) == 0)
        def _():
            acc[...] = jnp.zeros_like(acc)

        acc[...] += _dot_tn(a_ref[...], b_ref[...])

        @pl.when(pl.program_id(2) == n_k - 1)
        def _():
            o_ref[...] = acc[...].astype(BF16)

    call = dict(
        grid=(m // bm, n // bn, n_k),
        in_specs=[pl.BlockSpec((tk, bm), lambda i, j, k: (k, i)),
                  pl.BlockSpec((tk, bn), lambda i, j, k: (k, first_block + j))],
        out_specs=[pl.BlockSpec((bm, bn), lambda i, j, k: (i, j))],
        out_shape=[_sds((m, n), BF16)],
        scratch_shapes=[pltpu.VMEM((bm, bn), F32)])
    if comm is None:
        return pl.pallas_call(body, name=name, compiler_params=_params(3), **call)(a, b)[0]
    (out,), received = _hosted_call(body, comm, name, operands=(a, b), **call)
    return out, received


def _adamw(items, tr, name):
    n_parts, r, c = items[0][0].shape
    n_items = len(items)

    def body(*refs):
        ins, outs = refs[:4 * n_items], refs[4 * n_items:]
        for i in range(n_items):
            p_ref, w_ref, m_ref, v_ref = ins[4 * i:4 * i + 4]
            g_ref, d_ref, nm_ref, nv_ref = outs[4 * i:4 * i + 4]
            g = p_ref[0].astype(F32)
            for j in range(1, n_parts):
                g = g + p_ref[j].astype(F32)
            g_ref[...] = g
            nm = ADAM_B1 * m_ref[...] + (1.0 - ADAM_B1) * g
            nv = ADAM_B2 * v_ref[...] + (1.0 - ADAM_B2) * (g * g)
            nm_ref[...] = nm
            nv_ref[...] = nv
            m_hat = nm / (1.0 - ADAM_B1 ** ADAM_STEP)
            v_hat = nv / (1.0 - ADAM_B2 ** ADAM_STEP)
            d_ref[...] = -ADAM_LR * (m_hat / (jnp.sqrt(v_hat) + ADAM_EPS) + ADAM_WD * w_ref[...])

    results = pl.pallas_call(
        body, name=name, grid=(r // tr,),
        in_specs=([pl.BlockSpec((n_parts, tr, c), lambda i: (0, i, 0))] + [_tile(tr, c)] * 3) * n_items,
        out_specs=[_tile(tr, c)] * (4 * n_items),
        out_shape=[_sds((r, c), F32)] * (4 * n_items),
        compiler_params=_params(),
    )(*[a for item in items for a in item])
    return [results[4 * i:4 * i + 4] for i in range(n_items)]


def _sum_parts(parts, name):
    n_parts, r, c = parts.shape

    def body(p_ref, o_ref):
        acc = p_ref[0]
        for j in range(1, n_parts):
            acc = acc + p_ref[j]
        o_ref[...] = acc

    return pl.pallas_call(
        body, name=name, grid=(1,),
        in_specs=[_whole((n_parts, r, c))], out_specs=_whole((r, c)), out_shape=_sds((r, c), F32),
        compiler_params=_params(),
    )(parts)


def _row(v):
    return v.reshape(1, -1)


def _by_owner_rows(g):
    return g.reshape(N_DEV, g.shape[0] // N_DEV, g.shape[1])


WEIGHTS = ("mix_pre_g", "w_mix_in", "conv_a_w", "conv_b_w", "conv_b_b", "ln_b_g", "ln_b_b", "w_mix_out", "mix_post_g",
           "xa_pre_g", "mem_norm_g", "w_q", "w_k", "w_v", "w_o", "xa_post_g", "ffn_pre_g", "w_gate", "w_up", "w_down",
           "ffn_post_g")
LARGE = ("w_mix_in", "w_mix_out", "w_q", "w_k", "w_v", "w_o", "w_gate", "w_up", "w_down")
COLUMN_SHARDED = ("w_mix_in", "w_gate", "w_up")
GAINS = ("mix_pre_g", "mix_post_g", "xa_pre_g", "mem_norm_g", "xa_post_g", "ffn_pre_g", "ffn_post_g")
CHANNEL_VECTORS = ("conv_b_b", "ln_b_g", "ln_b_b")
CONV_TAPS = ("conv_a_w", "conv_b_w")
SMALL = GAINS + CHANNEL_VECTORS + CONV_TAPS
CONV_COLS_PER_DEVICE = D_A // N_DEV
TOKEN_TILE = 512
FFN_FWD_TOKEN_TILE = 512
FFN_TOKEN_TILE = 256
CONV_TOKEN_TILE = 256
WGRAD_TOKEN_TILE = 2048
ADAM_ROWS_PER_STEP = 64


def _lane_rows(v):
    flat = v.reshape(-1)
    tile = SUBLANES * LANES
    flat = jnp.pad(flat, (0, (-flat.shape[0]) % tile))
    return flat.reshape(-1, LANES)


def _pack_small(values, names):
    return jnp.concatenate([_lane_rows(values[n]) for n in names], axis=0)


def _unpack_small(packed, like, names):
    out, off = {}, 0
    for n in names:
        size = like[n].size
        rows = _lane_rows(like[n]).shape[0]
        out[n] = packed[off:off + rows, :].reshape(-1)[:size].reshape(like[n].shape)
        off += rows
    return out


def kernel(x, mem, mix_pre_g, w_mix_in, conv_a_w, conv_b_w, conv_b_b, ln_b_g, ln_b_b, w_mix_out, mix_post_g, xa_pre_g, mem_norm_g, w_q, w_k, w_v, w_o, xa_post_g, ffn_pre_g, w_gate, w_up, w_down, ffn_post_g, loss_target, m_mix_pre_g, m_w_mix_in, m_conv_a_w, m_conv_b_w, m_conv_b_b, m_ln_b_g, m_ln_b_b, m_w_mix_out, m_mix_post_g, m_xa_pre_g, m_mem_norm_g, m_w_q, m_w_k, m_w_v, m_w_o, m_xa_post_g, m_ffn_pre_g, m_w_gate, m_w_up, m_w_down, m_ffn_post_g, v_mix_pre_g, v_w_mix_in, v_conv_a_w, v_conv_b_w, v_conv_b_b, v_ln_b_g, v_ln_b_b, v_w_mix_out, v_mix_post_g, v_xa_pre_g, v_mem_norm_g, v_w_q, v_w_k, v_w_v, v_w_o, v_xa_post_g, v_ffn_pre_g, v_w_gate, v_w_up, v_w_down, v_ffn_post_g):
    given = dict(locals())
    w = {n: given[n] for n in WEIGHTS}
    m = {n: given["m_" + n] for n in WEIGHTS}
    v = {n: given["v_" + n] for n in WEIGHTS}
    xs, mems, target = x[0], mem[0], loss_target[0]
    t = xs.shape[0]
    tm, tm_ffn, tc, tk = min(TOKEN_TILE, t), min(FFN_TOKEN_TILE, t), min(CONV_TOKEN_TILE, t), min(WGRAD_TOKEN_TILE, t)
    g = {n: _row(w[n]) for n in GAINS}
    bb, lg, lb = (_row(w[n]) for n in CHANNEL_VECTORS)

    def shard_bf16(*names):
        return [w[n].astype(BF16) for n in names]

    def shard_bf16_t(*names):
        return [w[n].T.astype(BF16) for n in names]

    g_mix_in, g_taps = _comm_call(_Gather(shard_bf16_t("w_mix_in") + [_pack_small(w, CONV_TAPS)]), "gather_mixer")
    w_mix_in_t = g_mix_in.reshape(D_IN_ALL, D_MODEL)
    taps, off = {}, 0
    for n in CONV_TAPS:
        k, cols = w[n].shape
        rows = _lane_rows(w[n]).shape[0]
        blk = g_taps[:, off:off + rows, :].reshape(N_DEV, -1)[:, :k * cols].reshape(N_DEV, k, cols)
        taps[n] = blk.transpose(1, 0, 2).reshape(k, N_DEV * cols)
        off += rows
    wa, wb = taps["conv_a_w"], taps["conv_b_w"]

    (h1, u), gathered = _norm_matmul(xs, g["mix_pre_g"], w_mix_in_t.T, tm, "mix_in_fwd",
                                     _Gather(shard_bf16("w_mix_out", "w_q", "w_k", "w_v", "w_o")))
    w_mix_out, w_q, w_k, w_v, w_o = (a.reshape(D_MODEL, D_MODEL) for a in gathered)
    (ycat, zc, ca, x1, s1), (g_gate, g_up, g_down) = _conv_fwd(
        u, wa, wb, bb, lg, lb, xs, w_mix_out, g["mix_post_g"], tc, "conv_fwd",
        _Gather(shard_bf16_t("w_gate", "w_up") + shard_bf16("w_down")))
    w_gate_t, w_up_t, w_down = (a.reshape(D_FF, D_MODEL) for a in (g_gate, g_up, g_down))
    mem_n, kk, vv = _mem_fwd(mems, g["mem_norm_g"], w_k, w_v, "mem_fwd")
    x2, h2, q, o, a = _attn_fwd(x1, g["xa_pre_g"], w_q, kk, vv, w_o, g["xa_post_g"], tm, "attn_fwd")
    h3, gt, up, f, sq = _ffn_fwd(x2, g["ffn_pre_g"], w_gate_t, w_up_t, w_down, g["ffn_post_g"], target,
                                 min(FFN_FWD_TOKEN_TILE, t), "ffn_fwd")

    dx2, df, hd, dgt, dup, d_ffn_post, d_ffn_pre = _ffn_bwd(
        x2, f, target, gt, up, g["ffn_pre_g"], w_gate_t, w_up_t, w_down, g["ffn_post_g"], tm_ffn, "ffn_bwd")
    d_w_down = _wgrad(hd, df, tk, D_FF // 2, D_MODEL, "wgrad_down")
    d_w_gate_t = _wgrad(dgt, h3, tk, D_FF // 2, D_MODEL, "wgrad_gate")
    d_w_up_t = _wgrad(dup, h3, tk, D_FF // 2, D_MODEL, "wgrad_up")
    ffn_slabs = [_by_owner_rows(d) for d in (d_w_gate_t, d_w_up_t, d_w_down)]

    (dx1, da, dq, dk, dv, d_xa_post, d_xa_pre), from_ffn = _attn_bwd(
        dx2, a, x1, q, kk, vv, g["xa_pre_g"], w_q, w_o, g["xa_post_g"], tm, "attn_bwd",
        _Exchange(ffn_slabs, [True] * 3))
    d_w_o = _wgrad(o, da, tk, D_MODEL, D_MODEL, "wgrad_o")
    d_w_q = _wgrad(h2, dq, tk, D_MODEL, D_MODEL, "wgrad_q")
    d_w_k, d_w_v, d_mem_norm = _mem_bwd(mems, mem_n, dk, dv, g["mem_norm_g"], w_k, w_v, "mem_bwd")
    ds1, dycat, d_mix_post = _proj_bwd(dx1, s1, w_mix_out, g["mix_post_g"], tm, "mix_out_bwd")
    d_w_mix_out = _wgrad(ycat, ds1, tk, D_MODEL, D_MODEL, "wgrad_mix_out")
    attn_slabs = [_by_owner_rows(d) for d in (d_w_mix_out, d_w_q, d_w_k, d_w_v, d_w_o)]

    (du, d_conv_a, d_conv_b, d_conv_bb, d_ln_g, d_ln_b), from_attn = _conv_bwd(
        dycat, u, zc, ca, wa, wb, lg, lb, tc, "conv_bwd", _Exchange(attn_slabs, [True] * 5))
    dx, d_mix_pre = _in_bwd(du, w_mix_in_t, xs, dx1, g["mix_pre_g"], tm, "mix_in_bwd")
    small_grads = dict(mix_pre_g=d_mix_pre, conv_a_w=d_conv_a, conv_b_w=d_conv_b, conv_b_b=d_conv_bb, ln_b_g=d_ln_g,
                       ln_b_b=d_ln_b, mix_post_g=d_mix_post, xa_pre_g=d_xa_pre, mem_norm_g=d_mem_norm, xa_post_g=d_xa_post,
                       ffn_pre_g=d_ffn_pre, ffn_post_g=d_ffn_post, loss=sq)
    names = SMALL + ("loss",)
    half = D_MODEL // 2
    d_in_lo = _wgrad(du, h1, tk, D_IN_ALL // 2, half, "wgrad_mix_in_lo", cols=(0, half))
    d_in_hi, (from_in_lo, all_small) = _wgrad(
        du, h1, tk, D_IN_ALL // 2, half, "wgrad_mix_in_hi",
        _Exchange([_by_owner_rows(d_in_lo), _pack_small(small_grads, names)], [True, False]), cols=(half, half))
    (from_in_hi,) = _comm_call(_Exchange([_by_owner_rows(d_in_hi)], [True]), "reduce_tail")

    received = dict(zip(("w_gate", "w_up", "w_down"), from_ffn))
    received.update(zip(("w_mix_out", "w_q", "w_k", "w_v", "w_o"), from_attn))
    received["w_mix_in"] = jnp.concatenate([from_in_lo, from_in_hi], axis=2)
    grad, delta, new_m, new_v = {}, {}, {}, {}
    for group in (("w_mix_in",), ("w_gate",), ("w_up",), ("w_down",), ("w_mix_out", "w_q", "w_k", "w_v", "w_o")):
        items = [[received[n]] + [a[n].T if n in COLUMN_SHARDED else a[n] for a in (w, m, v)] for n in group]
        rows = items[0][1].shape[0]
        results = _adamw(items, min(rows, ADAM_ROWS_PER_STEP) if len(group) > 1 else rows, "adamw_" + group[0])
        for n, result in zip(group, results):
            grad[n], delta[n], new_m[n], new_v[n] = [r.T if n in COLUMN_SHARDED else r for r in result]

    total = _unpack_small(_sum_parts(all_small, "sum_small"), small_grads, names)
    loss = jnp.sum(total.pop("loss")) * (0.5 / D_MODEL)
    first_col = _device_index() * CONV_COLS_PER_DEVICE
    for n in CONV_TAPS:
        total[n] = lax.dynamic_slice_in_dim(total[n], first_col, CONV_COLS_PER_DEVICE, axis=1)
    total = {n: total[n].reshape(w[n].shape) for n in SMALL}
    packed_small = [_pack_small(values, SMALL) for values in (total, w, m, v)]
    ((g_s, d_s, nm_s, nv_s),) = _adamw([[packed_small[0][None]] + packed_small[1:]], packed_small[0].shape[0],
                                       "adamw_small")
    for out, p in ((grad, g_s), (delta, d_s), (new_m, nm_s), (new_v, nv_s)):
        out.update(_unpack_small(p, w, SMALL))

    return (loss, dx[None], *[grad[n] for n in WEIGHTS], *[delta[n] for n in WEIGHTS], *[new_m[n] for n in WEIGHTS],
            *[new_v[n] for n in WEIGHTS])
```

```python
import functools

import jax
import jax.numpy as jnp
from jax import lax
from jax.experimental import pallas as pl
from jax.experimental.pallas import tpu as pltpu

F32 = jnp.float32
BF16 = jnp.bfloat16

D_MODEL = 1024
D_A = 512
D_B = 512
D_IN_ALL = 2560
CONV_A_W = 3
CONV_B_W = 31
XA_HEADS = 4
XA_HEAD_DIM = 256
D_FF = 2816
N_DEV = 8
RMS_EPS = 1e-6
LN_EPS = 1e-5
ADAM_LR = 0.001
ADAM_B1 = 0.9
ADAM_B2 = 0.999
ADAM_EPS = 1e-08
ADAM_WD = 0.01
ADAM_STEP = 10

VMEM_LIMIT_BYTES = 56 * 1024 * 1024
SUBLANES = 8
LANES = 128
HALO_B = 32
HALO_A = 8
CONV_CHUNK = 32
GATHER_FORWARD_STEPS_BEFORE_END = 8

MESH = pl.DeviceIdType.MESH


def _params(n_grid_axes=1):
    return pltpu.CompilerParams(dimension_semantics=("arbitrary",) * n_grid_axes, vmem_limit_bytes=VMEM_LIMIT_BYTES)


def _sds(shape, dtype):
    return jax.ShapeDtypeStruct(shape, dtype)


def _tile(rows, cols):
    return pl.BlockSpec((rows, cols), lambda i: (i, 0))


def _rtile(rows, cols, n):
    return pl.BlockSpec((rows, cols), lambda i: (n - 1 - i, 0))


def _whole(shape):
    zeros = (0,) * len(shape)
    return pl.BlockSpec(shape, lambda i: zeros)


def _resident(shape):
    zeros = (0,) * len(shape)
    return pl.BlockSpec(shape, lambda i: zeros, pipeline_mode=pl.Buffered(1))


def _dot(a, b):
    return jnp.dot(a, b, preferred_element_type=F32)


def _dot_nt(a, b):
    return lax.dot_general(a, b, (((1,), (1,)), ((), ())), preferred_element_type=F32)


def _dot_tn(a, b):
    return lax.dot_general(a, b, (((0,), (0,)), ((), ())), preferred_element_type=F32)


def _sigmoid(x):
    return 1.0 / (1.0 + jnp.exp(-x))


def _rms_fwd(x, g):
    r = lax.rsqrt(jnp.mean(x * x, axis=-1, keepdims=True) + RMS_EPS)
    return x * r * g


def _rms_bwd(dy, xin, g):
    r = lax.rsqrt(jnp.mean(xin * xin, axis=-1, keepdims=True) + RMS_EPS)
    n = xin * r
    dg = jnp.sum(dy * n, axis=0, keepdims=True)
    dn = dy * g
    dx = r * (dn - n * jnp.mean(dn * n, axis=-1, keepdims=True))
    return dx, dg


def _zero_at_first_step(*refs):
    @pl.when(pl.program_id(0) == 0)
    def _():
        for ref in refs:
            ref[...] = jnp.zeros(ref.shape, ref.dtype)


def _place():
    return lax.axis_index("x"), lax.axis_index("y"), lax.axis_index("c")


def _device_index():
    x, y, c = _place()
    return 4 * x + 2 * y + c


class _Gather:
    def __init__(self, arrays):
        self.arrays = list(arrays)
        self.out_shape = [_sds((N_DEV, *a.shape), a.dtype) for a in self.arrays]
        n = len(self.arrays)
        self.scratch_shapes = [pltpu.SemaphoreType.DMA((n, 7)), pltpu.SemaphoreType.DMA((n, 7)),
                               pltpu.SemaphoreType.DMA((n,))]

    def forward_step(self, n_steps):
        return max(0, n_steps - 1 - GATHER_FORWARD_STEPS_BEFORE_END)

    def bind(self, srcs, dsts, send_sems, recv_sems, local_sems):
        x, y, cc = _place()
        me, sibling = (x, y, cc), (x, y, 1 - cc)
        chips = [(1 - x, y), (x, 1 - y), (1 - x, 1 - y)]

        def copy(a, k, owner, to, src=None):
            slot = dsts[a].at[4 * owner[0] + 2 * owner[1] + owner[2]]
            return pltpu.make_async_remote_copy(
                src_ref=slot if src is None else src, dst_ref=slot, send_sem=send_sems.at[a, k],
                recv_sem=recv_sems.at[a, k], device_id=to, device_id_type=MESH)

        def first(a):
            return [copy(a, 0, me, sibling, src=srcs[a])] + [
                copy(a, 1 + j, me, (*chip, cc), src=srcs[a]) for j, chip in enumerate(chips)]

        def passed(a, j):
            return copy(a, 4 + j, (*chips[j], cc), sibling)

        def mine(a):
            return pltpu.make_async_copy(srcs[a], dsts[a].at[4 * x + 2 * y + cc], local_sems.at[a])

        def start():
            for a in range(len(srcs)):
                mine(a).start()
                for cp in first(a):
                    cp.start()

        def forward():
            for a in range(len(srcs)):
                for j, chip in enumerate(chips):
                    copy(a, 1 + j, (*chip, cc), me).wait_recv()
                    passed(a, j).start()

        def finish():
            for a in range(len(srcs)):
                copy(a, 0, sibling, me).wait_recv()
                for j, chip in enumerate(chips):
                    copy(a, 4 + j, (*chip, 1 - cc), me).wait_recv()
                for cp in first(a) + [passed(a, j) for j in range(len(chips))]:
                    cp.wait_send()
                mine(a).wait()

        return start, forward, finish


class _Exchange:
    def __init__(self, arrays, scatter):
        self.arrays = list(arrays)
        self.scatter = list(scatter)
        self.out_shape = [_sds(a.shape if s else (N_DEV, *a.shape), a.dtype) for a, s in zip(self.arrays, self.scatter)]
        n = len(self.arrays)
        self.scratch_shapes = [pltpu.SemaphoreType.DMA((n, 7)), pltpu.SemaphoreType.DMA((n, 7)),
                               pltpu.SemaphoreType.DMA((n,))]

    def forward_step(self, n_steps):
        return n_steps - 1

    def bind(self, srcs, dsts, send_sems, recv_sems, local_sems):
        me = _device_index()

        def copies(a):
            out = []
            for k in range(1, N_DEV):
                p = me ^ k
                out.append(pltpu.make_async_remote_copy(
                    src_ref=srcs[a].at[p] if self.scatter[a] else srcs[a], dst_ref=dsts[a].at[me],
                    send_sem=send_sems.at[a, k - 1], recv_sem=recv_sems.at[a, k - 1],
                    device_id=(p >> 2, (p >> 1) & 1, p & 1), device_id_type=MESH))
            return out

        def mine(a):
            return pltpu.make_async_copy(srcs[a].at[me] if self.scatter[a] else srcs[a], dsts[a].at[me], local_sems.at[a])

        def start():
            for a in range(len(srcs)):
                mine(a).start()
                for cp in copies(a):
                    cp.start()

        def forward():
            pass

        def finish():
            for a in range(len(srcs)):
                for cp in copies(a):
                    cp.wait()
                mine(a).wait()

        return start, forward, finish


def _hosted_call(core, comm, name, grid, in_specs, out_specs, out_shape, scratch_shapes, operands):
    grid = (grid,) if isinstance(grid, int) else tuple(grid)
    n_steps = 1
    for extent in grid:
        n_steps *= extent
    n_in, n_out, n_scr, n_arr = len(in_specs), len(out_specs), len(scratch_shapes), len(comm.arrays)
    any_spec = pl.BlockSpec(memory_space=pl.ANY)

    def body(*refs):
        ins, refs = refs[:n_in], refs[n_in:]
        srcs, refs = refs[:n_arr], refs[n_arr:]
        outs, refs = refs[:n_out], refs[n_out:]
        dsts, refs = refs[:n_arr], refs[n_arr:]
        scratch, sems = refs[:n_scr], refs[n_scr:]
        start, forward, finish = comm.bind(srcs, dsts, *sems)
        step = pl.program_id(0)
        for axis in range(1, len(grid)):
            step = step * grid[axis] + pl.program_id(axis)
        pl.when(step == 0)(start)
        core(*ins, *outs, *scratch)
        pl.when(step == comm.forward_step(n_steps))(forward)
        pl.when(step == n_steps - 1)(finish)

    results = pl.pallas_call(
        body, name=name, grid=grid,
        in_specs=list(in_specs) + [any_spec] * n_arr,
        out_specs=list(out_specs) + [any_spec] * n_arr,
        out_shape=list(out_shape) + comm.out_shape,
        scratch_shapes=list(scratch_shapes) + comm.scratch_shapes,
        compiler_params=_params(len(grid)),
    )(*operands, *comm.arrays)
    return results[:n_out], results[n_out:]


def _comm_call(comm, name):
    return _hosted_call(lambda: None, comm, name, 1, [], [], [], [], [])[1]


def _norm_matmul(x, g, w, tm, name, comm):
    t, d = x.shape
    n = w.shape[1]

    def core(x_ref, g_ref, w_ref, h_ref, o_ref):
        h = _rms_fwd(x_ref[...], g_ref[...]).astype(BF16)
        h_ref[...] = h
        o_ref[...] = _dot(h, w_ref[...]).astype(BF16)

    return _hosted_call(
        core, comm, name, t // tm,
        in_specs=[_tile(tm, d), _whole((1, d)), _resident((d, n))],
        out_specs=[_tile(tm, d), _tile(tm, n)],
        out_shape=[_sds((t, d), BF16), _sds((t, n), BF16)],
        scratch_shapes=[], operands=(x, g, w))


def _conv_fwd(u, wa, wb, bb, lg, lb, xres, w_out, g_post, tc, name, comm):
    t = u.shape[0]
    d = w_out.shape[1]
    n_chunks = tc // CONV_CHUNK
    piece = 2 * LANES

    def core(u_ref, wa_ref, wb_ref, bb_ref, lg_ref, lb_ref, x_ref, wo_ref, go_ref,
             y_ref, zc_ref, ca_ref, x1_ref, s_ref, zbuf, cvbuf, zcbuf):
        @pl.when(pl.program_id(0) == 0)
        def _():
            zbuf[:, 0:HALO_B, :] = jnp.zeros((D_B // LANES, HALO_B, LANES), F32)
            cvbuf[:, 0:HALO_A, :] = jnp.zeros((D_A // LANES, HALO_A, LANES), F32)

        s = None
        for lb_i in range(D_A // LANES):
            if lb_i > 0 and lb_i % 2 == 0:
                cols = slice((lb_i - 2) * LANES, lb_i * LANES)
                part = _dot(y_ref[:, cols], wo_ref[cols, :])
                s = part if s is None else s + part
            lanes = slice(lb_i * LANES, (lb_i + 1) * LANES)
            c_a = u_ref[:, D_A + lb_i * LANES:D_A + (lb_i + 1) * LANES].astype(F32)
            v_a = u_ref[:, 2 * D_A + lb_i * LANES:2 * D_A + (lb_i + 1) * LANES].astype(F32)
            cvbuf[lb_i, HALO_A:HALO_A + tc, :] = c_a * v_a
            glu_v = u_ref[:, 3 * D_A + lb_i * LANES:3 * D_A + (lb_i + 1) * LANES].astype(F32)
            glu_g = u_ref[:, 3 * D_A + D_B + lb_i * LANES:3 * D_A + D_B + (lb_i + 1) * LANES].astype(F32)
            zbuf[lb_i, HALO_B:HALO_B + tc, :] = glu_v * _sigmoid(glu_g)

            for c in range(n_chunks):
                r0 = c * CONV_CHUNK
                rows = slice(r0, r0 + CONV_CHUNK)
                acc = jnp.zeros((CONV_CHUNK, LANES), F32)
                for k in range(CONV_A_W):
                    off = r0 + HALO_A - (CONV_A_W - 1) + k
                    acc = acc + wa_ref[k:k + 1, lanes] * cvbuf[lb_i, off:off + CONV_CHUNK, :]
                ca_ref[rows, lanes] = acc.astype(BF16)
                y_ref[rows, lanes] = (u_ref[rows, lanes].astype(F32) * acc).astype(BF16)

                accb = jnp.zeros((CONV_CHUNK, LANES), F32)
                for k in range(CONV_B_W):
                    off = r0 + HALO_B - (CONV_B_W - 1) + k
                    accb = accb + wb_ref[k:k + 1, lanes] * zbuf[lb_i, off:off + CONV_CHUNK, :]
                zcbuf[rows, lanes] = accb + bb_ref[:, lanes]

            zbuf[lb_i, 0:HALO_B, :] = zbuf[lb_i, tc:tc + HALO_B, :]
            cvbuf[lb_i, 0:HALO_A, :] = cvbuf[lb_i, tc:tc + HALO_A, :]

        cols = slice(D_A - piece, D_A)
        s = s + _dot(y_ref[:, cols], wo_ref[cols, :])

        for c in range(n_chunks):
            rows = slice(c * CONV_CHUNK, (c + 1) * CONV_CHUNK)
            zc = zcbuf[rows, :]
            zc_ref[rows, :] = zc.astype(BF16)
            mu = jnp.mean(zc, axis=-1, keepdims=True)
            xc = zc - mu
            var = jnp.mean(xc * xc, axis=-1, keepdims=True)
            ln = xc * lax.rsqrt(var + LN_EPS) * lg_ref[...] + lb_ref[...]
            y_ref[rows, D_A:D_A + D_B] = (ln * _sigmoid(ln)).astype(BF16)

        s = s + _dot(y_ref[:, D_A:D_A + D_B], wo_ref[D_A:D_A + D_B, :])
        s_ref[...] = s.astype(BF16)
        x1_ref[...] = x_ref[...] + _rms_fwd(s, go_ref[...])

    return _hosted_call(
        core, comm, name, t // tc,
        in_specs=[_tile(tc, D_IN_ALL), _whole((CONV_A_W, D_A)), _whole((CONV_B_W, D_B)), _whole((1, D_B)),
                  _whole((1, D_B)), _whole((1, D_B)), _tile(tc, d), _resident((D_A + D_B, d)), _whole((1, d))],
        out_specs=[_tile(tc, D_A + D_B), _tile(tc, D_B), _tile(tc, D_A), _tile(tc, d), _tile(tc, d)],
        out_shape=[_sds((t, D_A + D_B), BF16), _sds((t, D_B), BF16), _sds((t, D_A), BF16), _sds((t, d), F32),
                   _sds((t, d), BF16)],
        scratch_shapes=[pltpu.VMEM((D_B // LANES, HALO_B + tc, LANES), F32),
                        pltpu.VMEM((D_A // LANES, HALO_A + tc, LANES), F32), pltpu.VMEM((tc, D_B), F32)],
        operands=(u, wa, wb, bb, lg, lb, xres, w_out, g_post))


def _mem_fwd(mem, g, wk, wv, name):
    m, d = mem.shape

    def body(mem_ref, g_ref, wk_ref, wv_ref, n_ref, k_ref, v_ref):
        n = _rms_fwd(mem_ref[...], g_ref[...]).astype(BF16)
        n_ref[...] = n
        k_ref[...] = _dot(n, wk_ref[...]).astype(BF16)
        v_ref[...] = _dot(n, wv_ref[...]).astype(BF16)

    return pl.pallas_call(
        body, name=name, grid=(1,),
        in_specs=[_whole((m, d)), _whole((1, d)), _whole((d, d)), _whole((d, d))],
        out_specs=[_whole((m, d))] * 3,
        out_shape=[_sds((m, d), BF16)] * 3,
        compiler_params=_params(),
    )(mem, g, wk, wv)


def _softmax_rows(s):
    e = jnp.exp(s - jnp.max(s, axis=-1, keepdims=True))
    return e / jnp.sum(e, axis=-1, keepdims=True)


def _attn_fwd(x1, g_pre, wq, k, v, wo, g_post, tm, name):
    t, d = x1.shape
    m = k.shape[0]
    scale = XA_HEAD_DIM ** -0.5

    def body(x_ref, gp_ref, wq_ref, k_ref, v_ref, wo_ref, go_ref, x2_ref, h_ref, q_ref, o_ref, a_ref):
        x = x_ref[...]
        h = _rms_fwd(x, gp_ref[...]).astype(BF16)
        h_ref[...] = h
        q_ref[...] = _dot(h, wq_ref[...]).astype(BF16)
        for hd in range(XA_HEADS):
            cols = slice(hd * XA_HEAD_DIM, (hd + 1) * XA_HEAD_DIM)
            p = _softmax_rows(_dot_nt(q_ref[:, cols], k_ref[:, cols]) * scale)
            o_ref[:, cols] = _dot(p.astype(BF16), v_ref[:, cols]).astype(BF16)
        a = _dot(o_ref[...], wo_ref[...])
        a_ref[...] = a.astype(BF16)
        x2_ref[...] = x + _rms_fwd(a, go_ref[...])

    return pl.pallas_call(
        body, name=name, grid=(t // tm,),
        in_specs=[_tile(tm, d), _whole((1, d)), _resident((d, d)), _whole((m, d)), _whole((m, d)), _resident((d, d)),
                  _whole((1, d))],
        out_specs=[_tile(tm, d)] * 5,
        out_shape=[_sds((t, d), F32)] + [_sds((t, d), BF16)] * 4,
        compiler_params=_params(),
    )(x1, g_pre, wq, k, v, wo, g_post)


def _ffn_fwd(x2, g_pre, wg_t, wu_t, wd, g_post, target, tm, name):
    t, d = x2.shape
    f = wg_t.shape[0]

    def body(x_ref, gp_ref, wg_ref, wu_ref, wd_ref, go_ref, tgt_ref, h_ref, gt_ref, up_ref, f_ref, sq_ref):
        _zero_at_first_step(sq_ref)
        x = x_ref[...]
        h = _rms_fwd(x, gp_ref[...]).astype(BF16)
        h_ref[...] = h
        gt = _dot_nt(h, wg_ref[...])
        up = _dot_nt(h, wu_ref[...])
        gt_ref[...] = gt.astype(BF16)
        up_ref[...] = up.astype(BF16)
        hd = (gt * _sigmoid(gt) * up).astype(BF16)
        ff = _dot(hd, wd_ref[...])
        f_ref[...] = ff.astype(BF16)
        err = x + _rms_fwd(ff, go_ref[...]) - tgt_ref[...]
        sq_ref[...] += jnp.sum(err * err, axis=0, keepdims=True)

    return pl.pallas_call(
        body, name=name, grid=(t // tm,),
        in_specs=[_tile(tm, d), _whole((1, d)), _resident((f, d)), _resident((f, d)), _resident((f, d)), _whole((1, d)),
                  _tile(tm, d)],
        out_specs=[_tile(tm, d), _tile(tm, f), _tile(tm, f), _tile(tm, d), _whole((1, d))],
        out_shape=[_sds((t, d), BF16), _sds((t, f), BF16), _sds((t, f), BF16), _sds((t, d), BF16), _sds((1, d), F32)],
        compiler_params=_params(),
    )(x2, g_pre, wg_t, wu_t, wd, g_post, target)


def _ffn_bwd(x2, f, target, gt, up, g_pre, wg_t, wu_t, wd, g_post, tm, name):
    t, d = x2.shape
    ff = wg_t.shape[0]

    def body(x_ref, f_ref, tgt_ref, gt_ref, up_ref, gp_ref, wg_ref, wu_ref, wd_ref, go_ref,
             dx_ref, df_ref, hd_ref, dgt_ref, dup_ref, dgo_ref, dgp_ref):
        _zero_at_first_step(dgo_ref, dgp_ref)
        x = x_ref[...]
        fo = f_ref[...].astype(F32)
        dx3 = (x + _rms_fwd(fo, go_ref[...]) - tgt_ref[...]) * (1.0 / d)
        df, dgo = _rms_bwd(dx3, fo, go_ref[...])
        dgo_ref[...] += dgo
        df = df.astype(BF16)
        df_ref[...] = df
        dhd = _dot_nt(df, wd_ref[...])
        gt = gt_ref[...].astype(F32)
        up = up_ref[...].astype(F32)
        sg = _sigmoid(gt)
        si = gt * sg
        hd_ref[...] = (si * up).astype(BF16)
        dup = (dhd * si).astype(BF16)
        dgt = (dhd * up * (sg * (1.0 + gt * (1.0 - sg)))).astype(BF16)
        dup_ref[...] = dup
        dgt_ref[...] = dgt
        dh = _dot(dgt, wg_ref[...]) + _dot(dup, wu_ref[...])
        dxn, dgp = _rms_bwd(dh, x, gp_ref[...])
        dgp_ref[...] += dgp
        dx_ref[...] = dx3 + dxn

    return pl.pallas_call(
        body, name=name, grid=(t // tm,),
        in_specs=[_tile(tm, d), _tile(tm, d), _tile(tm, d), _tile(tm, ff), _tile(tm, ff), _whole((1, d)),
                  _resident((ff, d)), _resident((ff, d)), _resident((ff, d)), _whole((1, d))],
        out_specs=[_tile(tm, d), _tile(tm, d), _tile(tm, ff), _tile(tm, ff), _tile(tm, ff), _whole((1, d)), _whole((1, d))],
        out_shape=[_sds((t, d), F32), _sds((t, d), BF16), _sds((t, ff), BF16), _sds((t, ff), BF16), _sds((t, ff), BF16),
                   _sds((1, d), F32), _sds((1, d), F32)],
        compiler_params=_params(),
    )(x2, f, target, gt, up, g_pre, wg_t, wu_t, wd, g_post)


def _attn_bwd(dx2, a, x1, q, k, v, g_pre, wq, wo, g_post, tm, name, comm):
    t, d = x1.shape
    m = k.shape[0]
    scale = XA_HEAD_DIM ** -0.5

    def core(dx2_ref, a_ref, x_ref, q_ref, k_ref, v_ref, gp_ref, wq_ref, wo_ref, go_ref,
             dx1_ref, da_ref, dq_ref, dk_ref, dv_ref, dgo_ref, dgp_ref, do_buf):
        _zero_at_first_step(dgo_ref, dgp_ref, dk_ref, dv_ref)
        dx2 = dx2_ref[...]
        da, dgo = _rms_bwd(dx2, a_ref[...].astype(F32), go_ref[...])
        dgo_ref[...] += dgo
        da = da.astype(BF16)
        da_ref[...] = da
        do_buf[...] = _dot_nt(da, wo_ref[...]).astype(BF16)
        for hd in range(XA_HEADS):
            cols = slice(hd * XA_HEAD_DIM, (hd + 1) * XA_HEAD_DIM)
            qh = q_ref[:, cols]
            p = _softmax_rows(_dot_nt(qh, k_ref[:, cols]) * scale)
            do_h = do_buf[:, cols]
            dp = _dot_nt(do_h, v_ref[:, cols])
            dv_ref[:, cols] += _dot_tn(p.astype(BF16), do_h)
            ds = (p * (dp - jnp.sum(dp * p, axis=-1, keepdims=True)) * scale).astype(BF16)
            dq_ref[:, cols] = _dot(ds, k_ref[:, cols]).astype(BF16)
            dk_ref[:, cols] += _dot_tn(ds, qh)
        dh = _dot_nt(dq_ref[...], wq_ref[...])
        dxn, dgp = _rms_bwd(dh, x_ref[...], gp_ref[...])
        dgp_ref[...] += dgp
        dx1_ref[...] = dx2 + dxn

    return _hosted_call(
        core, comm, name, t // tm,
        in_specs=[_tile(tm, d), _tile(tm, d), _tile(tm, d), _tile(tm, d), _whole((m, d)), _whole((m, d)), _whole((1, d)),
                  _resident((d, d)), _resident((d, d)), _whole((1, d))],
        out_specs=[_tile(tm, d), _tile(tm, d), _tile(tm, d), _whole((m, d)), _whole((m, d)), _whole((1, d)), _whole((1, d))],
        out_shape=[_sds((t, d), F32), _sds((t, d), BF16), _sds((t, d), BF16), _sds((m, d), F32), _sds((m, d), F32),
                   _sds((1, d), F32), _sds((1, d), F32)],
        scratch_shapes=[pltpu.VMEM((tm, d), BF16)],
        operands=(dx2, a, x1, q, k, v, g_pre, wq, wo, g_post))


def _mem_bwd(mem, mem_n, dk, dv, g, wk, wv, name):
    m, d = mem.shape

    def body(mem_ref, n_ref, dk_ref, dv_ref, g_ref, wk_ref, wv_ref, dwk_ref, dwv_ref, dg_ref):
        dk = dk_ref[...].astype(BF16)
        dv = dv_ref[...].astype(BF16)
        n = n_ref[...]
        dwk_ref[...] = _dot_tn(n, dk).astype(BF16)
        dwv_ref[...] = _dot_tn(n, dv).astype(BF16)
        dn = _dot_nt(dk, wk_ref[...]) + _dot_nt(dv, wv_ref[...])
        _, dg = _rms_bwd(dn, mem_ref[...], g_ref[...])
        dg_ref[...] = dg

    return pl.pallas_call(
        body, name=name, grid=(1,),
        in_specs=[_whole((m, d)), _whole((m, d)), _whole((m, d)), _whole((m, d)), _whole((1, d)), _whole((d, d)),
                  _whole((d, d))],
        out_specs=[_whole((d, d)), _whole((d, d)), _whole((1, d))],
        out_shape=[_sds((d, d), BF16), _sds((d, d), BF16), _sds((1, d), F32)],
        compiler_params=_params(),
    )(mem, mem_n, dk, dv, g, wk, wv)


def _proj_bwd(dxo, s, w, g, tm, name):
    t, d = dxo.shape
    k = w.shape[0]

    def body(dx_ref, s_ref, w_ref, g_ref, ds_ref, da_ref, dg_ref):
        _zero_at_first_step(dg_ref)
        ds, dg = _rms_bwd(dx_ref[...], s_ref[...].astype(F32), g_ref[...])
        dg_ref[...] += dg
        ds = ds.astype(BF16)
        ds_ref[...] = ds
        da_ref[...] = _dot_nt(ds, w_ref[...]).astype(BF16)

    return pl.pallas_call(
        body, name=name, grid=(t // tm,),
        in_specs=[_tile(tm, d), _tile(tm, d), _resident((k, d)), _whole((1, d))],
        out_specs=[_tile(tm, d), _tile(tm, k), _whole((1, d))],
        out_shape=[_sds((t, d), BF16), _sds((t, k), BF16), _sds((1, d), F32)],
        compiler_params=_params(),
    )(dxo, s, w, g)


def _conv_bwd(dy, u, zc, ca, wa, wb, lg, lb, tc, name, comm):
    t = u.shape[0]
    n_tiles = t // tc
    n_chunks = tc // CONV_CHUNK

    def core(dy_ref, u_ref, zc_ref, ca_ref, wa_ref, wb_ref, lg_ref, lb_ref,
             du_ref, dwa_ref, dwb_ref, dbb_ref, dlg_ref, dlb_ref, ebuf, eabuf, zbuf, cvbuf, gatebuf, wacc, aacc, vacc):
        step = pl.program_id(0)

        @pl.when(step == 0)
        def _():
            ebuf[:, tc:tc + HALO_B, :] = jnp.zeros((D_B // LANES, HALO_B, LANES), F32)
            eabuf[:, tc:tc + HALO_A, :] = jnp.zeros((D_A // LANES, HALO_A, LANES), F32)
            wacc[...] = jnp.zeros_like(wacc)
            aacc[...] = jnp.zeros_like(aacc)
            vacc[...] = jnp.zeros_like(vacc)

        dbb = jnp.zeros((SUBLANES, D_B), F32)
        dlg = jnp.zeros((SUBLANES, D_B), F32)
        dlb = jnp.zeros((SUBLANES, D_B), F32)
        for c in range(n_chunks):
            rows = slice(c * CONV_CHUNK, (c + 1) * CONV_CHUNK)
            dy_a = dy_ref[rows, 0:D_A].astype(F32)
            b_a = u_ref[rows, 0:D_A].astype(F32)
            du_ref[rows, 0:D_A] = (dy_a * ca_ref[rows, :].astype(F32)).astype(BF16)
            dca = dy_a * b_a
            cv = u_ref[rows, D_A:2 * D_A].astype(F32) * u_ref[rows, 2 * D_A:3 * D_A].astype(F32)
            gate = _sigmoid(u_ref[rows, 3 * D_A + D_B:3 * D_A + 2 * D_B].astype(F32))
            z = u_ref[rows, 3 * D_A:3 * D_A + D_B].astype(F32) * gate
            for lb_i in range(D_A // LANES):
                lanes = slice(lb_i * LANES, (lb_i + 1) * LANES)
                eabuf[lb_i, rows, :] = dca[:, lanes]
                cvbuf[lb_i, rows, :] = cv[:, lanes]
                zbuf[lb_i, rows, :] = z[:, lanes]
                gatebuf[lb_i, rows, :] = gate[:, lanes]

            zcv = zc_ref[rows, :].astype(F32)
            mu = jnp.mean(zcv, axis=-1, keepdims=True)
            xc = zcv - mu
            rstd = lax.rsqrt(jnp.mean(xc * xc, axis=-1, keepdims=True) + LN_EPS)
            xhat = xc * rstd
            ln = xhat * lg_ref[...] + lb_ref[...]
            sg = _sigmoid(ln)
            dln = dy_ref[rows, D_A:D_A + D_B].astype(F32) * (sg * (1.0 + ln * (1.0 - sg)))
            dlg = dlg + jnp.sum((dln * xhat).reshape(CONV_CHUNK // SUBLANES, SUBLANES, D_B), axis=0)
            dlb = dlb + jnp.sum(dln.reshape(CONV_CHUNK // SUBLANES, SUBLANES, D_B), axis=0)
            dxh = dln * lg_ref[...]
            dzc = rstd * (dxh - jnp.mean(dxh, axis=-1, keepdims=True) - xhat * jnp.mean(dxh * xhat, axis=-1, keepdims=True))
            dbb = dbb + jnp.sum(dzc.reshape(CONV_CHUNK // SUBLANES, SUBLANES, D_B), axis=0)
            for lb_i in range(D_B // LANES):
                ebuf[lb_i, rows, :] = dzc[:, lb_i * LANES:(lb_i + 1) * LANES]
        vacc[0] += dbb
        vacc[1] += dlg
        vacc[2] += dlb

        for lb_i in range(D_A // LANES):
            lanes = slice(lb_i * LANES, (lb_i + 1) * LANES)

            def cols(first):
                return slice(first + lb_i * LANES, first + (lb_i + 1) * LANES)

            for c in range(n_chunks):
                r0 = c * CONV_CHUNK
                rows = slice(r0, r0 + CONV_CHUNK)
                cv = cvbuf[lb_i, rows, :]
                dcv = jnp.zeros((CONV_CHUNK, LANES), F32)
                for k in range(CONV_A_W):
                    off = r0 + (CONV_A_W - 1) - k
                    e = eabuf[lb_i, off:off + CONV_CHUNK, :]
                    dcv = dcv + wa_ref[k:k + 1, lanes] * e
                    aacc[k, :, lanes] += jnp.sum((cv * e).reshape(CONV_CHUNK // SUBLANES, SUBLANES, LANES), axis=0)
                du_ref[rows, cols(D_A)] = (dcv * u_ref[rows, cols(2 * D_A)].astype(F32)).astype(BF16)
                du_ref[rows, cols(2 * D_A)] = (dcv * u_ref[rows, cols(D_A)].astype(F32)).astype(BF16)

                z = zbuf[lb_i, rows, :]
                dz = jnp.zeros((CONV_CHUNK, LANES), F32)
                for k in range(CONV_B_W):
                    off = r0 + (CONV_B_W - 1) - k
                    e = ebuf[lb_i, off:off + CONV_CHUNK, :]
                    dz = dz + wb_ref[k:k + 1, lanes] * e
                    wacc[k, :, lanes] += jnp.sum((z * e).reshape(CONV_CHUNK // SUBLANES, SUBLANES, LANES), axis=0)
                glu_v = u_ref[rows, cols(3 * D_A)].astype(F32)
                sgg = gatebuf[lb_i, rows, :]
                du_ref[rows, cols(3 * D_A)] = (dz * sgg).astype(BF16)
                du_ref[rows, cols(3 * D_A + D_B)] = (dz * glu_v * sgg * (1.0 - sgg)).astype(BF16)

            ebuf[lb_i, tc:tc + HALO_B, :] = ebuf[lb_i, 0:HALO_B, :]
            eabuf[lb_i, tc:tc + HALO_A, :] = eabuf[lb_i, 0:HALO_A, :]

        @pl.when(step == n_tiles - 1)
        def _():
            for k in range(CONV_B_W):
                dwb_ref[k:k + 1, :] = jnp.sum(wacc[k], axis=0, keepdims=True)
            for k in range(CONV_A_W):
                dwa_ref[k:k + 1, :] = jnp.sum(aacc[k], axis=0, keepdims=True)
            dbb_ref[...] = jnp.sum(vacc[0], axis=0, keepdims=True)
            dlg_ref[...] = jnp.sum(vacc[1], axis=0, keepdims=True)
            dlb_ref[...] = jnp.sum(vacc[2], axis=0, keepdims=True)

    return _hosted_call(
        core, comm, name, n_tiles,
        in_specs=[_rtile(tc, D_A + D_B, n_tiles), _rtile(tc, D_IN_ALL, n_tiles), _rtile(tc, D_B, n_tiles),
                  _rtile(tc, D_A, n_tiles), _whole((CONV_A_W, D_A)), _whole((CONV_B_W, D_B)), _whole((1, D_B)),
                  _whole((1, D_B))],
        out_specs=[_rtile(tc, D_IN_ALL, n_tiles), _whole((CONV_A_W, D_A)), _whole((CONV_B_W, D_B)), _whole((1, D_B)),
                   _whole((1, D_B)), _whole((1, D_B))],
        out_shape=[_sds((t, D_IN_ALL), BF16), _sds((CONV_A_W, D_A), F32), _sds((CONV_B_W, D_B), F32), _sds((1, D_B), F32),
                   _sds((1, D_B), F32), _sds((1, D_B), F32)],
        scratch_shapes=[pltpu.VMEM((D_B // LANES, tc + HALO_B, LANES), F32), pltpu.VMEM((D_A // LANES, tc + HALO_A, LANES), F32),
                        pltpu.VMEM((D_B // LANES, tc, LANES), F32), pltpu.VMEM((D_A // LANES, tc, LANES), F32),
                        pltpu.VMEM((D_B // LANES, tc, LANES), F32),
                        pltpu.VMEM((CONV_B_W, SUBLANES, D_B), F32), pltpu.VMEM((CONV_A_W, SUBLANES, D_A), F32),
                        pltpu.VMEM((3, SUBLANES, D_B), F32)],
        operands=(dy, u, zc, ca, wa, wb, lg, lb))


def _in_bwd(du, w_t, x, dx1, g, tm, name):
    t, d = x.shape
    n = w_t.shape[0]

    def body(du_ref, w_ref, x_ref, dx1_ref, g_ref, dx_ref, dg_ref):
        _zero_at_first_step(dg_ref)
        dh = _dot(du_ref[...], w_ref[...])
        dxn, dg = _rms_bwd(dh, x_ref[...], g_ref[...])
        dg_ref[...] += dg
        dx_ref[...] = dx1_ref[...] + dxn

    return pl.pallas_call(
        body, name=name, grid=(t // tm,),
        in_specs=[_tile(tm, n), _resident((n, d)), _tile(tm, d), _tile(tm, d), _whole((1, d))],
        out_specs=[_tile(tm, d), _whole((1, d))],
        out_shape=[_sds((t, d), F32), _sds((1, d), F32)],
        compiler_params=_params(),
    )(du, w_t, x, dx1, g)


def _wgrad(a, b, tk, bm, bn, name, comm=None, cols=None):
    t, m = a.shape
    first, n = (0, b.shape[1]) if cols is None else cols
    first_block = first // bn
    n_k = t // tk

    def body(a_ref, b_ref, o_ref, acc):
        @pl.when(pl.program_id(2) == 0)
        def _():
            acc[...] = jnp.zeros_like(acc)

        acc[...] += _dot_tn(a_ref[...], b_ref[...])

        @pl.when(pl.program_id(2) == n_k - 1)
        def _():
            o_ref[...] = acc[...].astype(BF16)

    call = dict(
        grid=(m // bm, n // bn, n_k),
        in_specs=[pl.BlockSpec((tk, bm), lambda i, j, k: (k, i)),
                  pl.BlockSpec((tk, bn), lambda i, j, k: (k, first_block + j))],
        out_specs=[pl.BlockSpec((bm, bn), lambda i, j, k: (i, j))],
        out_shape=[_sds((m, n), BF16)],
        scratch_shapes=[pltpu.VMEM((bm, bn), F32)])
    if comm is None:
        return pl.pallas_call(body, name=name, compiler_params=_params(3), **call)(a, b)[0]
    (out,), received = _hosted_call(body, comm, name, operands=(a, b), **call)
    return out, received


def _adamw(items, tr, name):
    n_parts, r, c = items[0][0].shape
    n_items = len(items)

    def body(*refs):
        ins, outs = refs[:4 * n_items], refs[4 * n_items:]
        for i in range(n_items):
            p_ref, w_ref, m_ref, v_ref = ins[4 * i:4 * i + 4]
            g_ref, d_ref, nm_ref, nv_ref = outs[4 * i:4 * i + 4]
            g = p_ref[0].astype(F32)
            for j in range(1, n_parts):
                g = g + p_ref[j].astype(F32)
            g_ref[...] = g
            nm = ADAM_B1 * m_ref[...] + (1.0 - ADAM_B1) * g
            nv = ADAM_B2 * v_ref[...] + (1.0 - ADAM_B2) * (g * g)
            nm_ref[...] = nm
            nv_ref[...] = nv
            m_hat = nm / (1.0 - ADAM_B1 ** ADAM_STEP)
            v_hat = nv / (1.0 - ADAM_B2 ** ADAM_STEP)
            d_ref[...] = -ADAM_LR * (m_hat / (jnp.sqrt(v_hat) + ADAM_EPS) + ADAM_WD * w_ref[...])

    results = pl.pallas_call(
        body, name=name, grid=(r // tr,),
        in_specs=([pl.BlockSpec((n_parts, tr, c), lambda i: (0, i, 0))] + [_tile(tr, c)] * 3) * n_items,
        out_specs=[_tile(tr, c)] * (4 * n_items),
        out_shape=[_sds((r, c), F32)] * (4 * n_items),
        compiler_params=_params(),
    )(*[a for item in items for a in item])
    return [results[4 * i:4 * i + 4] for i in range(n_items)]


def _sum_parts(parts, name):
    n_parts, r, c = parts.shape

    def body(p_ref, o_ref):
        acc = p_ref[0]
        for j in range(1, n_parts):
            acc = acc + p_ref[j]
        o_ref[...] = acc

    return pl.pallas_call(
        body, name=name, grid=(1,),
        in_specs=[_whole((n_parts, r, c))], out_specs=_whole((r, c)), out_shape=_sds((r, c), F32),
        compiler_params=_params(),
    )(parts)


def _row(v):
    return v.reshape(1, -1)


def _by_owner_rows(g):
    return g.reshape(N_DEV, g.shape[0] // N_DEV, g.shape[1])


WEIGHTS = ("mix_pre_g", "w_mix_in", "conv_a_w", "conv_b_w", "conv_b_b", "ln_b_g", "ln_b_b", "w_mix_out", "mix_post_g",
           "xa_pre_g", "mem_norm_g", "w_q", "w_k", "w_v", "w_o", "xa_post_g", "ffn_pre_g", "w_gate", "w_up", "w_down",
           "ffn_post_g")
LARGE = ("w_mix_in", "w_mix_out", "w_q", "w_k", "w_v", "w_o", "w_gate", "w_up", "w_down")
COLUMN_SHARDED = ("w_mix_in", "w_gate", "w_up")
GAINS = ("mix_pre_g", "mix_post_g", "xa_pre_g", "mem_norm_g", "xa_post_g", "ffn_pre_g", "ffn_post_g")
CHANNEL_VECTORS = ("conv_b_b", "ln_b_g", "ln_b_b")
CONV_TAPS = ("conv_a_w", "conv_b_w")
SMALL = GAINS + CHANNEL_VECTORS + CONV_TAPS
CONV_COLS_PER_DEVICE = D_A // N_DEV
TOKEN_TILE = 512
FFN_FWD_TOKEN_TILE = 512
FFN_TOKEN_TILE = 256
CONV_TOKEN_TILE = 256
WGRAD_TOKEN_TILE = 2048
ADAM_ROWS_PER_STEP = 64


def _lane_rows(v):
    flat = v.reshape(-1)
    tile = SUBLANES * LANES
    flat = jnp.pad(flat, (0, (-flat.shape[0]) % tile))
    return flat.reshape(-1, LANES)


def _pack_small(values, names):
    return jnp.concatenate([_lane_rows(values[n]) for n in names], axis=0)


def _unpack_small(packed, like, names):
    out, off = {}, 0
    for n in names:
        size = like[n].size
        rows = _lane_rows(like[n]).shape[0]
        out[n] = packed[off:off + rows, :].reshape(-1)[:size].reshape(like[n].shape)
        off += rows
    return out


def kernel(x, mem, mix_pre_g, w_mix_in, conv_a_w, conv_b_w, conv_b_b, ln_b_g, ln_b_b, w_mix_out, mix_post_g, xa_pre_g, mem_norm_g, w_q, w_k, w_v, w_o, xa_post_g, ffn_pre_g, w_gate, w_up, w_down, ffn_post_g, loss_target, m_mix_pre_g, m_w_mix_in, m_conv_a_w, m_conv_b_w, m_conv_b_b, m_ln_b_g, m_ln_b_b, m_w_mix_out, m_mix_post_g, m_xa_pre_g, m_mem_norm_g, m_w_q, m_w_k, m_w_v, m_w_o, m_xa_post_g, m_ffn_pre_g, m_w_gate, m_w_up, m_w_down, m_ffn_post_g, v_mix_pre_g, v_w_mix_in, v_conv_a_w, v_conv_b_w, v_conv_b_b, v_ln_b_g, v_ln_b_b, v_w_mix_out, v_mix_post_g, v_xa_pre_g, v_mem_norm_g, v_w_q, v_w_k, v_w_v, v_w_o, v_xa_post_g, v_ffn_pre_g, v_w_gate, v_w_up, v_w_down, v_ffn_post_g):
    given = dict(locals())
    w = {n: given[n] for n in WEIGHTS}
    m = {n: given["m_" + n] for n in WEIGHTS}
    v = {n: given["v_" + n] for n in WEIGHTS}
    xs, mems, target = x[0], mem[0], loss_target[0]
    t = xs.shape[0]
    tm, tm_ffn, tc, tk = min(TOKEN_TILE, t), min(FFN_TOKEN_TILE, t), min(CONV_TOKEN_TILE, t), min(WGRAD_TOKEN_TILE, t)
    g = {n: _row(w[n]) for n in GAINS}
    bb, lg, lb = (_row(w[n]) for n in CHANNEL_VECTORS)

    def shard_bf16(*names):
        return [w[n].astype(BF16) for n in names]

    def shard_bf16_t(*names):
        return [w[n].T.astype(BF16) for n in names]

    g_mix_in, g_taps = _comm_call(_Gather(shard_bf16_t("w_mix_in") + [_pack_small(w, CONV_TAPS)]), "gather_mixer")
    w_mix_in_t = g_mix_in.reshape(D_IN_ALL, D_MODEL)
    taps, off = {}, 0
    for n in CONV_TAPS:
        k, cols = w[n].shape
        rows = _lane_rows(w[n]).shape[0]
        blk = g_taps[:, off:off + rows, :].reshape(N_DEV, -1)[:, :k * cols].reshape(N_DEV, k, cols)
        taps[n] = blk.transpose(1, 0, 2).reshape(k, N_DEV * cols)
        off += rows
    wa, wb = taps["conv_a_w"], taps["conv_b_w"]

    (h1, u), gathered = _norm_matmul(xs, g["mix_pre_g"], w_mix_in_t.T, tm, "mix_in_fwd",
                                     _Gather(shard_bf16("w_mix_out", "w_q", "w_k", "w_v", "w_o")))
    w_mix_out, w_q, w_k, w_v, w_o = (a.reshape(D_MODEL, D_MODEL) for a in gathered)
    (ycat, zc, ca, x1, s1), (g_gate, g_up, g_down) = _conv_fwd(
        u, wa, wb, bb, lg, lb, xs, w_mix_out, g["mix_post_g"], tc, "conv_fwd",
        _Gather(shard_bf16_t("w_gate", "w_up") + shard_bf16("w_down")))
    w_gate_t, w_up_t, w_down = (a.reshape(D_FF, D_MODEL) for a in (g_gate, g_up, g_down))
    mem_n, kk, vv = _mem_fwd(mems, g["mem_norm_g"], w_k, w_v, "mem_fwd")
    x2, h2, q, o, a = _attn_fwd(x1, g["xa_pre_g"], w_q, kk, vv, w_o, g["xa_post_g"], tm, "attn_fwd")
    h3, gt, up, f, sq = _ffn_fwd(x2, g["ffn_pre_g"], w_gate_t, w_up_t, w_down, g["ffn_post_g"], target,
                                 min(FFN_FWD_TOKEN_TILE, t), "ffn_fwd")

    dx2, df, hd, dgt, dup, d_ffn_post, d_ffn_pre = _ffn_bwd(
        x2, f, target, gt, up, g["ffn_pre_g"], w_gate_t, w_up_t, w_down, g["ffn_post_g"], tm_ffn, "ffn_bwd")
    d_w_down = _wgrad(hd, df, tk, D_FF // 2, D_MODEL, "wgrad_down")
    d_w_gate_t = _wgrad(dgt, h3, tk, D_FF // 2, D_MODEL, "wgrad_gate")
    d_w_up_t = _wgrad(dup, h3, tk, D_FF // 2, D_MODEL, "wgrad_up")
    ffn_slabs = [_by_owner_rows(d) for d in (d_w_gate_t, d_w_up_t, d_w_down)]

    (dx1, da, dq, dk, dv, d_xa_post, d_xa_pre), from_ffn = _attn_bwd(
        dx2, a, x1, q, kk, vv, g["xa_pre_g"], w_q, w_o, g["xa_post_g"], tm, "attn_bwd",
        _Exchange(ffn_slabs, [True] * 3))
    d_w_o = _wgrad(o, da, tk, D_MODEL, D_MODEL, "wgrad_o")
    d_w_q = _wgrad(h2, dq, tk, D_MODEL, D_MODEL, "wgrad_q")
    d_w_k, d_w_v, d_mem_norm = _mem_bwd(mems, mem_n, dk, dv, g["mem_norm_g"], w_k, w_v, "mem_bwd")
    ds1, dycat, d_mix_post = _proj_bwd(dx1, s1, w_mix_out, g["mix_post_g"], tm, "mix_out_bwd")
    d_w_mix_out = _wgrad(ycat, ds1, tk, D_MODEL, D_MODEL, "wgrad_mix_out")
    attn_slabs = [_by_owner_rows(d) for d in (d_w_mix_out, d_w_q, d_w_k, d_w_v, d_w_o)]

    (du, d_conv_a, d_conv_b, d_conv_bb, d_ln_g, d_ln_b), from_attn = _conv_bwd(
        dycat, u, zc, ca, wa, wb, lg, lb, tc, "conv_bwd", _Exchange(attn_slabs, [True] * 5))
    small_grads = dict(conv_a_w=d_conv_a, conv_b_w=d_conv_b, conv_b_b=d_conv_bb, ln_b_g=d_ln_g, ln_b_b=d_ln_b,
                       mix_post_g=d_mix_post, xa_pre_g=d_xa_pre, mem_norm_g=d_mem_norm, xa_post_g=d_xa_post,
                       ffn_pre_g=d_ffn_pre, ffn_post_g=d_ffn_post, loss=sq)
    names = tuple(n for n in SMALL if n != "mix_pre_g") + ("loss", "mix_pre_g")
    half = D_MODEL // 2
    d_in_lo = _wgrad(du, h1, tk, D_IN_ALL // 2, half, "wgrad_mix_in_lo", cols=(0, half))
    d_in_hi, (from_in_lo, early_small) = _wgrad(
        du, h1, tk, D_IN_ALL // 2, half, "wgrad_mix_in_hi",
        _Exchange([_by_owner_rows(d_in_lo), _pack_small(small_grads, names[:-1])], [True, False]), cols=(half, half))
    dx, d_mix_pre = _in_bwd(du, w_mix_in_t, xs, dx1, g["mix_pre_g"], tm, "mix_in_bwd")
    small_grads["mix_pre_g"] = d_mix_pre
    from_in_hi, late_small = _comm_call(
        _Exchange([_by_owner_rows(d_in_hi), _lane_rows(d_mix_pre)], [True, False]), "reduce_tail")
    all_small = jnp.concatenate([early_small, late_small], axis=1)

    received = dict(zip(("w_gate", "w_up", "w_down"), from_ffn))
    received.update(zip(("w_mix_out", "w_q", "w_k", "w_v", "w_o"), from_attn))
    received["w_mix_in"] = jnp.concatenate([from_in_lo, from_in_hi], axis=2)
    grad, delta, new_m, new_v = {}, {}, {}, {}
    for group in (("w_mix_in",), ("w_gate",), ("w_up",), ("w_down",), ("w_mix_out", "w_q", "w_k", "w_v", "w_o")):
        items = [[received[n]] + [a[n].T if n in COLUMN_SHARDED else a[n] for a in (w, m, v)] for n in group]
        rows = items[0][1].shape[0]
        results = _adamw(items, min(rows, ADAM_ROWS_PER_STEP) if len(group) > 1 else rows, "adamw_" + group[0])
        for n, result in zip(group, results):
            grad[n], delta[n], new_m[n], new_v[n] = [r.T if n in COLUMN_SHARDED else r for r in result]

    total = _unpack_small(_sum_parts(all_small, "sum_small"), small_grads, names)
    loss = jnp.sum(total.pop("loss")) * (0.5 / D_MODEL)
    first_col = _device_index() * CONV_COLS_PER_DEVICE
    for n in CONV_TAPS:
        total[n] = lax.dynamic_slice_in_dim(total[n], first_col, CONV_COLS_PER_DEVICE, axis=1)
    total = {n: total[n].reshape(w[n].shape) for n in SMALL}
    packed_small = [_pack_small(values, SMALL) for values in (total, w, m, v)]
    ((g_s, d_s, nm_s, nv_s),) = _adamw([[packed_small[0][None]] + packed_small[1:]], packed_small[0].shape[0],
                                       "adamw_small")
    for out, p in ((grad, g_s), (delta, d_s), (new_m, nm_s), (new_v, nv_s)):
        out.update(_unpack_small(p, w, SMALL))

    return (loss, dx[None], *[grad[n] for n in WEIGHTS], *[delta[n] for n in WEIGHTS], *[new_m[n] for n in WEIGHTS],
            *[new_v[n] for n in WEIGHTS])
```

```python
import jax
import jax.numpy as jnp
from jax import lax
from jax.experimental import pallas as pl
from jax.experimental.pallas import tpu as pltpu

F32 = jnp.float32
BF16 = jnp.bfloat16

D_MODEL = 1024
D_A = 512
D_B = 512
D_IN_ALL = 2560
CONV_A_W = 3
CONV_B_W = 31
XA_HEADS = 4
XA_HEAD_DIM = 256
D_FF = 2816
N_DEV = 8
RMS_EPS = 1e-6
LN_EPS = 1e-5
ADAM_LR = 0.001
ADAM_B1 = 0.9
ADAM_B2 = 0.999
ADAM_EPS = 1e-08
ADAM_WD = 0.01
ADAM_STEP = 10

VMEM_LIMIT_BYTES = 56 * 1024 * 1024
SUBLANES = 8
LANES = 128
HALO_B = 32
HALO_A = 8
CONV_FWD_CHUNK = 16
CONV_CHUNK = 32
GATHER_FORWARD_STEPS_BEFORE_END = 8

MESH = pl.DeviceIdType.MESH


def _params(n_grid_axes=1):
    return pltpu.CompilerParams(dimension_semantics=("arbitrary",) * n_grid_axes, vmem_limit_bytes=VMEM_LIMIT_BYTES)


def _sds(shape, dtype):
    return jax.ShapeDtypeStruct(shape, dtype)


def _tile(rows, cols):
    return pl.BlockSpec((rows, cols), lambda i: (i, 0))


def _rtile(rows, cols, n):
    return pl.BlockSpec((rows, cols), lambda i: (n - 1 - i, 0))


def _whole(shape):
    zeros = (0,) * len(shape)
    return pl.BlockSpec(shape, lambda i: zeros)


def _resident(shape):
    zeros = (0,) * len(shape)
    return pl.BlockSpec(shape, lambda i: zeros, pipeline_mode=pl.Buffered(1))


def _dot(a, b):
    return jnp.dot(a, b, preferred_element_type=F32)


def _dot_nt(a, b):
    return lax.dot_general(a, b, (((1,), (1,)), ((), ())), preferred_element_type=F32)


def _dot_tn(a, b):
    return lax.dot_general(a, b, (((0,), (0,)), ((), ())), preferred_element_type=F32)


def _sigmoid(x):
    return 1.0 / (1.0 + jnp.exp(-x))


def _rms_fwd(x, g):
    r = lax.rsqrt(jnp.mean(x * x, axis=-1, keepdims=True) + RMS_EPS)
    return x * r * g


def _rms_bwd(dy, xin, g):
    r = lax.rsqrt(jnp.mean(xin * xin, axis=-1, keepdims=True) + RMS_EPS)
    n = xin * r
    dg = jnp.sum(dy * n, axis=0, keepdims=True)
    dn = dy * g
    dx = r * (dn - n * jnp.mean(dn * n, axis=-1, keepdims=True))
    return dx, dg


def _zero_at_first_step(*refs):
    @pl.when(pl.program_id(0) == 0)
    def _():
        for ref in refs:
            ref[...] = jnp.zeros(ref.shape, ref.dtype)


def _place():
    return lax.axis_index("x"), lax.axis_index("y"), lax.axis_index("c")


def _device_index():
    x, y, c = _place()
    return 4 * x + 2 * y + c


class _Gather:
    def __init__(self, arrays):
        self.arrays = list(arrays)
        self.out_shape = [_sds((N_DEV, *a.shape), a.dtype) for a in self.arrays]
        n = len(self.arrays)
        self.scratch_shapes = [pltpu.SemaphoreType.DMA((n, 7)), pltpu.SemaphoreType.DMA((n, 7)),
                               pltpu.SemaphoreType.DMA((n,))]

    def forward_step(self, n_steps):
        return max(0, n_steps - 1 - GATHER_FORWARD_STEPS_BEFORE_END)

    def bind(self, srcs, dsts, send_sems, recv_sems, local_sems):
        x, y, cc = _place()
        me, sibling = (x, y, cc), (x, y, 1 - cc)
        chips = [(1 - x, y), (x, 1 - y), (1 - x, 1 - y)]

        def copy(a, k, owner, to, src=None):
            slot = dsts[a].at[4 * owner[0] + 2 * owner[1] + owner[2]]
            return pltpu.make_async_remote_copy(
                src_ref=slot if src is None else src, dst_ref=slot, send_sem=send_sems.at[a, k],
                recv_sem=recv_sems.at[a, k], device_id=to, device_id_type=MESH)

        def first(a):
            return [copy(a, 0, me, sibling, src=srcs[a])] + [
                copy(a, 1 + j, me, (*chip, cc), src=srcs[a]) for j, chip in enumerate(chips)]

        def passed(a, j):
            return copy(a, 4 + j, (*chips[j], cc), sibling)

        def mine(a):
            return pltpu.make_async_copy(srcs[a], dsts[a].at[4 * x + 2 * y + cc], local_sems.at[a])

        def start():
            for a in range(len(srcs)):
                mine(a).start()
                for cp in first(a):
                    cp.start()

        def forward():
            for a in range(len(srcs)):
                for j, chip in enumerate(chips):
                    copy(a, 1 + j, (*chip, cc), me).wait_recv()
                    passed(a, j).start()

        def finish():
            for a in range(len(srcs)):
                copy(a, 0, sibling, me).wait_recv()
                for j, chip in enumerate(chips):
                    copy(a, 4 + j, (*chip, 1 - cc), me).wait_recv()
                for cp in first(a) + [passed(a, j) for j in range(len(chips))]:
                    cp.wait_send()
                mine(a).wait()

        return start, forward, finish


class _Exchange:
    def __init__(self, arrays, scatter):
        self.arrays = list(arrays)
        self.scatter = list(scatter)
        self.out_shape = [_sds(a.shape if s else (N_DEV, *a.shape), a.dtype) for a, s in zip(self.arrays, self.scatter)]
        n = len(self.arrays)
        self.scratch_shapes = [pltpu.SemaphoreType.DMA((n, 7)), pltpu.SemaphoreType.DMA((n, 7)),
                               pltpu.SemaphoreType.DMA((n,))]

    def forward_step(self, n_steps):
        return n_steps - 1

    def bind(self, srcs, dsts, send_sems, recv_sems, local_sems):
        me = _device_index()

        def copies(a):
            out = []
            for k in range(1, N_DEV):
                p = me ^ k
                out.append(pltpu.make_async_remote_copy(
                    src_ref=srcs[a].at[p] if self.scatter[a] else srcs[a], dst_ref=dsts[a].at[me],
                    send_sem=send_sems.at[a, k - 1], recv_sem=recv_sems.at[a, k - 1],
                    device_id=(p >> 2, (p >> 1) & 1, p & 1), device_id_type=MESH))
            return out

        def mine(a):
            return pltpu.make_async_copy(srcs[a].at[me] if self.scatter[a] else srcs[a], dsts[a].at[me], local_sems.at[a])

        def start():
            for a in range(len(srcs)):
                mine(a).start()
                for cp in copies(a):
                    cp.start()

        def forward():
            pass

        def finish():
            for a in range(len(srcs)):
                for cp in copies(a):
                    cp.wait()
                mine(a).wait()

        return start, forward, finish


def _hosted_call(core, comm, name, grid, in_specs, out_specs, out_shape, scratch_shapes, operands):
    grid = (grid,) if isinstance(grid, int) else tuple(grid)
    n_steps = 1
    for extent in grid:
        n_steps *= extent
    n_in, n_out, n_scr, n_arr = len(in_specs), len(out_specs), len(scratch_shapes), len(comm.arrays)
    any_spec = pl.BlockSpec(memory_space=pl.ANY)

    def body(*refs):
        ins, refs = refs[:n_in], refs[n_in:]
        srcs, refs = refs[:n_arr], refs[n_arr:]
        outs, refs = refs[:n_out], refs[n_out:]
        dsts, refs = refs[:n_arr], refs[n_arr:]
        scratch, sems = refs[:n_scr], refs[n_scr:]
        start, forward, finish = comm.bind(srcs, dsts, *sems)
        step = pl.program_id(0)
        for axis in range(1, len(grid)):
            step = step * grid[axis] + pl.program_id(axis)
        pl.when(step == 0)(start)
        core(*ins, *outs, *scratch)
        pl.when(step == comm.forward_step(n_steps))(forward)
        pl.when(step == n_steps - 1)(finish)

    results = pl.pallas_call(
        body, name=name, grid=grid,
        in_specs=list(in_specs) + [any_spec] * n_arr,
        out_specs=list(out_specs) + [any_spec] * n_arr,
        out_shape=list(out_shape) + comm.out_shape,
        scratch_shapes=list(scratch_shapes) + comm.scratch_shapes,
        compiler_params=_params(len(grid)),
    )(*operands, *comm.arrays)
    return results[:n_out], results[n_out:]


def _comm_call(comm, name):
    return _hosted_call(lambda: None, comm, name, 1, [], [], [], [], [])[1]


def _norm_matmul(x, g, w, tm, name, comm):
    t, d = x.shape
    n = w.shape[1]

    def core(x_ref, g_ref, w_ref, h_ref, o_ref):
        h = _rms_fwd(x_ref[...], g_ref[...]).astype(BF16)
        h_ref[...] = h
        o_ref[...] = _dot(h, w_ref[...]).astype(BF16)

    return _hosted_call(
        core, comm, name, t // tm,
        in_specs=[_tile(tm, d), _whole((1, d)), _resident((d, n))],
        out_specs=[_tile(tm, d), _tile(tm, n)],
        out_shape=[_sds((t, d), BF16), _sds((t, n), BF16)],
        scratch_shapes=[], operands=(x, g, w))


def _conv_fwd(u, wa, wb, bb, lg, lb, xres, w_out, g_post, tc, name, comm):
    t = u.shape[0]
    d = w_out.shape[1]
    chunk = CONV_FWD_CHUNK
    n_chunks = tc // chunk
    piece = 2 * LANES

    def core(u_ref, wa_ref, wb_ref, bb_ref, lg_ref, lb_ref, x_ref, wo_ref, go_ref,
             y_ref, zc_ref, ca_ref, x1_ref, s_ref, zbuf, cvbuf, zcbuf):
        @pl.when(pl.program_id(0) == 0)
        def _():
            zbuf[:, 0:HALO_B, :] = jnp.zeros((D_B // LANES, HALO_B, LANES), F32)
            cvbuf[:, 0:HALO_A, :] = jnp.zeros((D_A // LANES, HALO_A, LANES), F32)

        s = None
        for lb_i in range(D_A // LANES):
            if lb_i > 0 and lb_i % 2 == 0:
                cols = slice((lb_i - 2) * LANES, lb_i * LANES)
                part = _dot(y_ref[:, cols], wo_ref[cols, :])
                s = part if s is None else s + part
            lanes = slice(lb_i * LANES, (lb_i + 1) * LANES)
            c_a = u_ref[:, D_A + lb_i * LANES:D_A + (lb_i + 1) * LANES].astype(F32)
            v_a = u_ref[:, 2 * D_A + lb_i * LANES:2 * D_A + (lb_i + 1) * LANES].astype(F32)
            cvbuf[lb_i, HALO_A:HALO_A + tc, :] = c_a * v_a
            glu_v = u_ref[:, 3 * D_A + lb_i * LANES:3 * D_A + (lb_i + 1) * LANES].astype(F32)
            glu_g = u_ref[:, 3 * D_A + D_B + lb_i * LANES:3 * D_A + D_B + (lb_i + 1) * LANES].astype(F32)
            zbuf[lb_i, HALO_B:HALO_B + tc, :] = glu_v * _sigmoid(glu_g)

            for c in range(n_chunks):
                r0 = c * chunk
                rows = slice(r0, r0 + chunk)
                acc = jnp.zeros((chunk, LANES), F32)
                for k in range(CONV_A_W):
                    off = r0 + HALO_A - (CONV_A_W - 1) + k
                    acc = acc + wa_ref[k:k + 1, lanes] * cvbuf[lb_i, off:off + chunk, :]
                ca_ref[rows, lanes] = acc.astype(BF16)
                y_ref[rows, lanes] = (u_ref[rows, lanes].astype(F32) * acc).astype(BF16)

                accb = jnp.zeros((chunk, LANES), F32)
                for k in range(CONV_B_W):
                    off = r0 + HALO_B - (CONV_B_W - 1) + k
                    accb = accb + wb_ref[k:k + 1, lanes] * zbuf[lb_i, off:off + chunk, :]
                zcbuf[rows, lanes] = accb + bb_ref[:, lanes]

            zbuf[lb_i, 0:HALO_B, :] = zbuf[lb_i, tc:tc + HALO_B, :]
            cvbuf[lb_i, 0:HALO_A, :] = cvbuf[lb_i, tc:tc + HALO_A, :]

        cols = slice(D_A - piece, D_A)
        s = s + _dot(y_ref[:, cols], wo_ref[cols, :])

        for c in range(n_chunks):
            rows = slice(c * chunk, (c + 1) * chunk)
            zc = zcbuf[rows, :]
            zc_ref[rows, :] = zc.astype(BF16)
            mu = jnp.mean(zc, axis=-1, keepdims=True)
            xc = zc - mu
            var = jnp.mean(xc * xc, axis=-1, keepdims=True)
            ln = xc * lax.rsqrt(var + LN_EPS) * lg_ref[...] + lb_ref[...]
            y_ref[rows, D_A:D_A + D_B] = (ln * _sigmoid(ln)).astype(BF16)

        s = s + _dot(y_ref[:, D_A:D_A + D_B], wo_ref[D_A:D_A + D_B, :])
        s_ref[...] = s.astype(BF16)
        x1_ref[...] = x_ref[...] + _rms_fwd(s, go_ref[...])

    return _hosted_call(
        core, comm, name, t // tc,
        in_specs=[_tile(tc, D_IN_ALL), _whole((CONV_A_W, D_A)), _whole((CONV_B_W, D_B)), _whole((1, D_B)),
                  _whole((1, D_B)), _whole((1, D_B)), _tile(tc, d), _resident((D_A + D_B, d)), _whole((1, d))],
        out_specs=[_tile(tc, D_A + D_B), _tile(tc, D_B), _tile(tc, D_A), _tile(tc, d), _tile(tc, d)],
        out_shape=[_sds((t, D_A + D_B), BF16), _sds((t, D_B), BF16), _sds((t, D_A), BF16), _sds((t, d), F32),
                   _sds((t, d), BF16)],
        scratch_shapes=[pltpu.VMEM((D_B // LANES, HALO_B + tc, LANES), F32),
                        pltpu.VMEM((D_A // LANES, HALO_A + tc, LANES), F32), pltpu.VMEM((tc, D_B), F32)],
        operands=(u, wa, wb, bb, lg, lb, xres, w_out, g_post))


def _mem_fwd(mem, g, wk, wv, name):
    m, d = mem.shape

    def body(mem_ref, g_ref, wk_ref, wv_ref, n_ref, k_ref, v_ref):
        n = _rms_fwd(mem_ref[...], g_ref[...]).astype(BF16)
        n_ref[...] = n
        k_ref[...] = _dot(n, wk_ref[...]).astype(BF16)
        v_ref[...] = _dot(n, wv_ref[...]).astype(BF16)

    return pl.pallas_call(
        body, name=name, grid=(1,),
        in_specs=[_whole((m, d)), _whole((1, d)), _whole((d, d)), _whole((d, d))],
        out_specs=[_whole((m, d))] * 3,
        out_shape=[_sds((m, d), BF16)] * 3,
        compiler_params=_params(),
    )(mem, g, wk, wv)


def _softmax_rows(s):
    e = jnp.exp(s - jnp.max(s, axis=-1, keepdims=True))
    return e / jnp.sum(e, axis=-1, keepdims=True)


def _attn_fwd(x1, g_pre, wq, k, v, wo, g_post, tm, name):
    t, d = x1.shape
    m = k.shape[0]
    scale = XA_HEAD_DIM ** -0.5

    def body(x_ref, gp_ref, wq_ref, k_ref, v_ref, wo_ref, go_ref, x2_ref, h_ref, q_ref, o_ref, a_ref):
        x = x_ref[...]
        h = _rms_fwd(x, gp_ref[...]).astype(BF16)
        h_ref[...] = h
        q_ref[...] = _dot(h, wq_ref[...]).astype(BF16)
        for hd in range(XA_HEADS):
            cols = slice(hd * XA_HEAD_DIM, (hd + 1) * XA_HEAD_DIM)
            p = _softmax_rows(_dot_nt(q_ref[:, cols], k_ref[:, cols]) * scale)
            o_ref[:, cols] = _dot(p.astype(BF16), v_ref[:, cols]).astype(BF16)
        a = _dot(o_ref[...], wo_ref[...])
        a_ref[...] = a.astype(BF16)
        x2_ref[...] = x + _rms_fwd(a, go_ref[...])

    return pl.pallas_call(
        body, name=name, grid=(t // tm,),
        in_specs=[_tile(tm, d), _whole((1, d)), _resident((d, d)), _whole((m, d)), _whole((m, d)), _resident((d, d)),
                  _whole((1, d))],
        out_specs=[_tile(tm, d)] * 5,
        out_shape=[_sds((t, d), F32)] + [_sds((t, d), BF16)] * 4,
        compiler_params=_params(),
    )(x1, g_pre, wq, k, v, wo, g_post)


def _ffn_fwd(x2, g_pre, wg_t, wu_t, wd, g_post, target, tm, name):
    t, d = x2.shape
    f = wg_t.shape[0]

    def body(x_ref, gp_ref, wg_ref, wu_ref, wd_ref, go_ref, tgt_ref, h_ref, gt_ref, up_ref, f_ref, sq_ref):
        _zero_at_first_step(sq_ref)
        x = x_ref[...]
        h = _rms_fwd(x, gp_ref[...]).astype(BF16)
        h_ref[...] = h
        gt = _dot_nt(h, wg_ref[...])
        up = _dot_nt(h, wu_ref[...])
        gt_ref[...] = gt.astype(BF16)
        up_ref[...] = up.astype(BF16)
        hd = (gt * _sigmoid(gt) * up).astype(BF16)
        ff = _dot(hd, wd_ref[...])
        f_ref[...] = ff.astype(BF16)
        err = x + _rms_fwd(ff, go_ref[...]) - tgt_ref[...]
        sq_ref[...] += jnp.sum(err * err, axis=0, keepdims=True)

    return pl.pallas_call(
        body, name=name, grid=(t // tm,),
        in_specs=[_tile(tm, d), _whole((1, d)), _resident((f, d)), _resident((f, d)), _resident((f, d)), _whole((1, d)),
                  _tile(tm, d)],
        out_specs=[_tile(tm, d), _tile(tm, f), _tile(tm, f), _tile(tm, d), _whole((1, d))],
        out_shape=[_sds((t, d), BF16), _sds((t, f), BF16), _sds((t, f), BF16), _sds((t, d), BF16), _sds((1, d), F32)],
        compiler_params=_params(),
    )(x2, g_pre, wg_t, wu_t, wd, g_post, target)


def _ffn_bwd(x2, f, target, gt, up, g_pre, wg_t, wu_t, wd, g_post, tm, name):
    t, d = x2.shape
    ff = wg_t.shape[0]

    def body(x_ref, f_ref, tgt_ref, gt_ref, up_ref, gp_ref, wg_ref, wu_ref, wd_ref, go_ref,
             dx_ref, df_ref, hd_ref, dgt_ref, dup_ref, dgo_ref, dgp_ref):
        _zero_at_first_step(dgo_ref, dgp_ref)
        x = x_ref[...]
        fo = f_ref[...].astype(F32)
        dx3 = (x + _rms_fwd(fo, go_ref[...]) - tgt_ref[...]) * (1.0 / d)
        df, dgo = _rms_bwd(dx3, fo, go_ref[...])
        dgo_ref[...] += dgo
        df = df.astype(BF16)
        df_ref[...] = df
        dhd = _dot_nt(df, wd_ref[...])
        gt = gt_ref[...].astype(F32)
        up = up_ref[...].astype(F32)
        sg = _sigmoid(gt)
        si = gt * sg
        hd_ref[...] = (si * up).astype(BF16)
        dup = (dhd * si).astype(BF16)
        dgt = (dhd * up * (sg * (1.0 + gt * (1.0 - sg)))).astype(BF16)
        dup_ref[...] = dup
        dgt_ref[...] = dgt
        dh = _dot(dgt, wg_ref[...]) + _dot(dup, wu_ref[...])
        dxn, dgp = _rms_bwd(dh, x, gp_ref[...])
        dgp_ref[...] += dgp
        dx_ref[...] = dx3 + dxn

    return pl.pallas_call(
        body, name=name, grid=(t // tm,),
        in_specs=[_tile(tm, d), _tile(tm, d), _tile(tm, d), _tile(tm, ff), _tile(tm, ff), _whole((1, d)),
                  _resident((ff, d)), _resident((ff, d)), _resident((ff, d)), _whole((1, d))],
        out_specs=[_tile(tm, d), _tile(tm, d), _tile(tm, ff), _tile(tm, ff), _tile(tm, ff), _whole((1, d)), _whole((1, d))],
        out_shape=[_sds((t, d), F32), _sds((t, d), BF16), _sds((t, ff), BF16), _sds((t, ff), BF16), _sds((t, ff), BF16),
                   _sds((1, d), F32), _sds((1, d), F32)],
        compiler_params=_params(),
    )(x2, f, target, gt, up, g_pre, wg_t, wu_t, wd, g_post)


def _attn_bwd(dx2, a, x1, q, k, v, g_pre, wq, wo, g_post, tm, name, comm):
    t, d = x1.shape
    m = k.shape[0]
    scale = XA_HEAD_DIM ** -0.5

    def core(dx2_ref, a_ref, x_ref, q_ref, k_ref, v_ref, gp_ref, wq_ref, wo_ref, go_ref,
             dx1_ref, da_ref, dq_ref, dk_ref, dv_ref, dgo_ref, dgp_ref, do_buf):
        _zero_at_first_step(dgo_ref, dgp_ref, dk_ref, dv_ref)
        dx2 = dx2_ref[...]
        da, dgo = _rms_bwd(dx2, a_ref[...].astype(F32), go_ref[...])
        dgo_ref[...] += dgo
        da = da.astype(BF16)
        da_ref[...] = da
        do_buf[...] = _dot_nt(da, wo_ref[...]).astype(BF16)
        for hd in range(XA_HEADS):
            cols = slice(hd * XA_HEAD_DIM, (hd + 1) * XA_HEAD_DIM)
            qh = q_ref[:, cols]
            p = _softmax_rows(_dot_nt(qh, k_ref[:, cols]) * scale)
            do_h = do_buf[:, cols]
            dp = _dot_nt(do_h, v_ref[:, cols])
            dv_ref[:, cols] += _dot_tn(p.astype(BF16), do_h)
            ds = (p * (dp - jnp.sum(dp * p, axis=-1, keepdims=True)) * scale).astype(BF16)
            dq_ref[:, cols] = _dot(ds, k_ref[:, cols]).astype(BF16)
            dk_ref[:, cols] += _dot_tn(ds, qh)
        dh = _dot_nt(dq_ref[...], wq_ref[...])
        dxn, dgp = _rms_bwd(dh, x_ref[...], gp_ref[...])
        dgp_ref[...] += dgp
        dx1_ref[...] = dx2 + dxn

    return _hosted_call(
        core, comm, name, t // tm,
        in_specs=[_tile(tm, d), _tile(tm, d), _tile(tm, d), _tile(tm, d), _whole((m, d)), _whole((m, d)), _whole((1, d)),
                  _resident((d, d)), _resident((d, d)), _whole((1, d))],
        out_specs=[_tile(tm, d), _tile(tm, d), _tile(tm, d), _whole((m, d)), _whole((m, d)), _whole((1, d)), _whole((1, d))],
        out_shape=[_sds((t, d), F32), _sds((t, d), BF16), _sds((t, d), BF16), _sds((m, d), F32), _sds((m, d), F32),
                   _sds((1, d), F32), _sds((1, d), F32)],
        scratch_shapes=[pltpu.VMEM((tm, d), BF16)],
        operands=(dx2, a, x1, q, k, v, g_pre, wq, wo, g_post))


def _mem_bwd(mem, mem_n, dk, dv, g, wk, wv, name):
    m, d = mem.shape

    def body(mem_ref, n_ref, dk_ref, dv_ref, g_ref, wk_ref, wv_ref, dwk_ref, dwv_ref, dg_ref):
        dk = dk_ref[...].astype(BF16)
        dv = dv_ref[...].astype(BF16)
        n = n_ref[...]
        dwk_ref[...] = _dot_tn(n, dk).astype(BF16)
        dwv_ref[...] = _dot_tn(n, dv).astype(BF16)
        dn = _dot_nt(dk, wk_ref[...]) + _dot_nt(dv, wv_ref[...])
        _, dg = _rms_bwd(dn, mem_ref[...], g_ref[...])
        dg_ref[...] = dg

    return pl.pallas_call(
        body, name=name, grid=(1,),
        in_specs=[_whole((m, d)), _whole((m, d)), _whole((m, d)), _whole((m, d)), _whole((1, d)), _whole((d, d)),
                  _whole((d, d))],
        out_specs=[_whole((d, d)), _whole((d, d)), _whole((1, d))],
        out_shape=[_sds((d, d), BF16), _sds((d, d), BF16), _sds((1, d), F32)],
        compiler_params=_params(),
    )(mem, mem_n, dk, dv, g, wk, wv)


def _proj_bwd(dxo, s, w, g, tm, name):
    t, d = dxo.shape
    k = w.shape[0]

    def body(dx_ref, s_ref, w_ref, g_ref, ds_ref, da_ref, dg_ref):
        _zero_at_first_step(dg_ref)
        ds, dg = _rms_bwd(dx_ref[...], s_ref[...].astype(F32), g_ref[...])
        dg_ref[...] += dg
        ds = ds.astype(BF16)
        ds_ref[...] = ds
        da_ref[...] = _dot_nt(ds, w_ref[...]).astype(BF16)

    return pl.pallas_call(
        body, name=name, grid=(t // tm,),
        in_specs=[_tile(tm, d), _tile(tm, d), _resident((k, d)), _whole((1, d))],
        out_specs=[_tile(tm, d), _tile(tm, k), _whole((1, d))],
        out_shape=[_sds((t, d), BF16), _sds((t, k), BF16), _sds((1, d), F32)],
        compiler_params=_params(),
    )(dxo, s, w, g)


def _conv_bwd(dy, u, zc, ca, wa, wb, lg, lb, tc, name, comm):
    t = u.shape[0]
    n_tiles = t // tc
    n_chunks = tc // CONV_CHUNK

    def core(dy_ref, u_ref, zc_ref, ca_ref, wa_ref, wb_ref, lg_ref, lb_ref,
             du_ref, dwa_ref, dwb_ref, dbb_ref, dlg_ref, dlb_ref, ebuf, eabuf, zbuf, cvbuf, gatebuf, wacc, aacc, vacc):
        step = pl.program_id(0)

        @pl.when(step == 0)
        def _():
            ebuf[:, tc:tc + HALO_B, :] = jnp.zeros((D_B // LANES, HALO_B, LANES), F32)
            eabuf[:, tc:tc + HALO_A, :] = jnp.zeros((D_A // LANES, HALO_A, LANES), F32)
            wacc[...] = jnp.zeros_like(wacc)
            aacc[...] = jnp.zeros_like(aacc)
            vacc[...] = jnp.zeros_like(vacc)

        dbb = jnp.zeros((SUBLANES, D_B), F32)
        dlg = jnp.zeros((SUBLANES, D_B), F32)
        dlb = jnp.zeros((SUBLANES, D_B), F32)
        for c in range(n_chunks):
            rows = slice(c * CONV_CHUNK, (c + 1) * CONV_CHUNK)
            dy_a = dy_ref[rows, 0:D_A].astype(F32)
            b_a = u_ref[rows, 0:D_A].astype(F32)
            du_ref[rows, 0:D_A] = (dy_a * ca_ref[rows, :].astype(F32)).astype(BF16)
            dca = dy_a * b_a
            cv = u_ref[rows, D_A:2 * D_A].astype(F32) * u_ref[rows, 2 * D_A:3 * D_A].astype(F32)
            gate = _sigmoid(u_ref[rows, 3 * D_A + D_B:3 * D_A + 2 * D_B].astype(F32))
            z = u_ref[rows, 3 * D_A:3 * D_A + D_B].astype(F32) * gate
            for lb_i in range(D_A // LANES):
                lanes = slice(lb_i * LANES, (lb_i + 1) * LANES)
                eabuf[lb_i, rows, :] = dca[:, lanes]
                cvbuf[lb_i, rows, :] = cv[:, lanes]
                zbuf[lb_i, rows, :] = z[:, lanes]
                gatebuf[lb_i, rows, :] = gate[:, lanes]

            zcv = zc_ref[rows, :].astype(F32)
            mu = jnp.mean(zcv, axis=-1, keepdims=True)
            xc = zcv - mu
            rstd = lax.rsqrt(jnp.mean(xc * xc, axis=-1, keepdims=True) + LN_EPS)
            xhat = xc * rstd
            ln = xhat * lg_ref[...] + lb_ref[...]
            sg = _sigmoid(ln)
            dln = dy_ref[rows, D_A:D_A + D_B].astype(F32) * (sg * (1.0 + ln * (1.0 - sg)))
            dlg = dlg + jnp.sum((dln * xhat).reshape(CONV_CHUNK // SUBLANES, SUBLANES, D_B), axis=0)
            dlb = dlb + jnp.sum(dln.reshape(CONV_CHUNK // SUBLANES, SUBLANES, D_B), axis=0)
            dxh = dln * lg_ref[...]
            dzc = rstd * (dxh - jnp.mean(dxh, axis=-1, keepdims=True) - xhat * jnp.mean(dxh * xhat, axis=-1, keepdims=True))
            dbb = dbb + jnp.sum(dzc.reshape(CONV_CHUNK // SUBLANES, SUBLANES, D_B), axis=0)
            for lb_i in range(D_B // LANES):
                ebuf[lb_i, rows, :] = dzc[:, lb_i * LANES:(lb_i + 1) * LANES]
        vacc[0] += dbb
        vacc[1] += dlg
        vacc[2] += dlb

        for lb_i in range(D_A // LANES):
            lanes = slice(lb_i * LANES, (lb_i + 1) * LANES)

            def cols(first):
                return slice(first + lb_i * LANES, first + (lb_i + 1) * LANES)

            for c in range(n_chunks):
                r0 = c * CONV_CHUNK
                rows = slice(r0, r0 + CONV_CHUNK)
                cv = cvbuf[lb_i, rows, :]
                dcv = jnp.zeros((CONV_CHUNK, LANES), F32)
                for k in range(CONV_A_W):
                    off = r0 + (CONV_A_W - 1) - k
                    e = eabuf[lb_i, off:off + CONV_CHUNK, :]
                    dcv = dcv + wa_ref[k:k + 1, lanes] * e
                    aacc[k, :, lanes] += jnp.sum((cv * e).reshape(CONV_CHUNK // SUBLANES, SUBLANES, LANES), axis=0)
                du_ref[rows, cols(D_A)] = (dcv * u_ref[rows, cols(2 * D_A)].astype(F32)).astype(BF16)
                du_ref[rows, cols(2 * D_A)] = (dcv * u_ref[rows, cols(D_A)].astype(F32)).astype(BF16)

                z = zbuf[lb_i, rows, :]
                dz = jnp.zeros((CONV_CHUNK, LANES), F32)
                for k in range(CONV_B_W):
                    off = r0 + (CONV_B_W - 1) - k
                    e = ebuf[lb_i, off:off + CONV_CHUNK, :]
                    dz = dz + wb_ref[k:k + 1, lanes] * e
                    wacc[k, :, lanes] += jnp.sum((z * e).reshape(CONV_CHUNK // SUBLANES, SUBLANES, LANES), axis=0)
                glu_v = u_ref[rows, cols(3 * D_A)].astype(F32)
                sgg = gatebuf[lb_i, rows, :]
                du_ref[rows, cols(3 * D_A)] = (dz * sgg).astype(BF16)
                du_ref[rows, cols(3 * D_A + D_B)] = (dz * glu_v * sgg * (1.0 - sgg)).astype(BF16)

            ebuf[lb_i, tc:tc + HALO_B, :] = ebuf[lb_i, 0:HALO_B, :]
            eabuf[lb_i, tc:tc + HALO_A, :] = eabuf[lb_i, 0:HALO_A, :]

        @pl.when(step == n_tiles - 1)
        def _():
            for k in range(CONV_B_W):
                dwb_ref[k:k + 1, :] = jnp.sum(wacc[k], axis=0, keepdims=True)
            for k in range(CONV_A_W):
                dwa_ref[k:k + 1, :] = jnp.sum(aacc[k], axis=0, keepdims=True)
            dbb_ref[...] = jnp.sum(vacc[0], axis=0, keepdims=True)
            dlg_ref[...] = jnp.sum(vacc[1], axis=0, keepdims=True)
            dlb_ref[...] = jnp.sum(vacc[2], axis=0, keepdims=True)

    return _hosted_call(
        core, comm, name, n_tiles,
        in_specs=[_rtile(tc, D_A + D_B, n_tiles), _rtile(tc, D_IN_ALL, n_tiles), _rtile(tc, D_B, n_tiles),
                  _rtile(tc, D_A, n_tiles), _whole((CONV_A_W, D_A)), _whole((CONV_B_W, D_B)), _whole((1, D_B)),
                  _whole((1, D_B))],
        out_specs=[_rtile(tc, D_IN_ALL, n_tiles), _whole((CONV_A_W, D_A)), _whole((CONV_B_W, D_B)), _whole((1, D_B)),
                   _whole((1, D_B)), _whole((1, D_B))],
        out_shape=[_sds((t, D_IN_ALL), BF16), _sds((CONV_A_W, D_A), F32), _sds((CONV_B_W, D_B), F32), _sds((1, D_B), F32),
                   _sds((1, D_B), F32), _sds((1, D_B), F32)],
        scratch_shapes=[pltpu.VMEM((D_B // LANES, tc + HALO_B, LANES), F32), pltpu.VMEM((D_A // LANES, tc + HALO_A, LANES), F32),
                        pltpu.VMEM((D_B // LANES, tc, LANES), F32), pltpu.VMEM((D_A // LANES, tc, LANES), F32),
                        pltpu.VMEM((D_B // LANES, tc, LANES), F32),
                        pltpu.VMEM((CONV_B_W, SUBLANES, D_B), F32), pltpu.VMEM((CONV_A_W, SUBLANES, D_A), F32),
                        pltpu.VMEM((3, SUBLANES, D_B), F32)],
        operands=(dy, u, zc, ca, wa, wb, lg, lb))


def _in_bwd(du, w_t, x, dx1, g, tm, name):
    t, d = x.shape
    n = w_t.shape[0]

    def body(du_ref, w_ref, x_ref, dx1_ref, g_ref, dx_ref, dg_ref):
        _zero_at_first_step(dg_ref)
        dh = _dot(du_ref[...], w_ref[...])
        dxn, dg = _rms_bwd(dh, x_ref[...], g_ref[...])
        dg_ref[...] += dg
        dx_ref[...] = dx1_ref[...] + dxn

    return pl.pallas_call(
        body, name=name, grid=(t // tm,),
        in_specs=[_tile(tm, n), _resident((n, d)), _tile(tm, d), _tile(tm, d), _whole((1, d))],
        out_specs=[_tile(tm, d), _whole((1, d))],
        out_shape=[_sds((t, d), F32), _sds((1, d), F32)],
        compiler_params=_params(),
    )(du, w_t, x, dx1, g)


def _wgrad(a, b, tk, bm, bn, name, comm=None, cols=None):
    t, m = a.shape
    first, n = (0, b.shape[1]) if cols is None else cols
    first_block = first // bn
    n_k = t // tk

    def body(a_ref, b_ref, o_ref, acc):
        @pl.when(pl.program_id(2) == 0)
        def _():
            acc[...] = jnp.zeros_like(acc)

        acc[...] += _dot_tn(a_ref[...], b_ref[...])

        @pl.when(pl.program_id(2) == n_k - 1)
        def _():
            o_ref[...] = acc[...].astype(BF16)

    call = dict(
        grid=(m // bm, n // bn, n_k),
        in_specs=[pl.BlockSpec((tk, bm), lambda i, j, k: (k, i)),
                  pl.BlockSpec((tk, bn), lambda i, j, k: (k, first_block + j))],
        out_specs=[pl.BlockSpec((bm, bn), lambda i, j, k: (i, j))],
        out_shape=[_sds((m, n), BF16)],
        scratch_shapes=[pltpu.VMEM((bm, bn), F32)])
    if comm is None:
        return pl.pallas_call(body, name=name, compiler_params=_params(3), **call)(a, b)[0]
    (out,), received = _hosted_call(body, comm, name, operands=(a, b), **call)
    return out, received


def _adamw(items, tr, name):
    n_parts, r, c = items[0][0].shape
    n_items = len(items)

    def body(*refs):
        ins, outs = refs[:4 * n_items], refs[4 * n_items:]
        for i in range(n_items):
            p_ref, w_ref, m_ref, v_ref = ins[4 * i:4 * i + 4]
            g_ref, d_ref, nm_ref, nv_ref = outs[4 * i:4 * i + 4]
            g = p_ref[0].astype(F32)
            for j in range(1, n_parts):
                g = g + p_ref[j].astype(F32)
            g_ref[...] = g
            nm = ADAM_B1 * m_ref[...] + (1.0 - ADAM_B1) * g
            nv = ADAM_B2 * v_ref[...] + (1.0 - ADAM_B2) * (g * g)
            nm_ref[...] = nm
            nv_ref[...] = nv
            m_hat = nm / (1.0 - ADAM_B1 ** ADAM_STEP)
            v_hat = nv / (1.0 - ADAM_B2 ** ADAM_STEP)
            d_ref[...] = -ADAM_LR * (m_hat / (jnp.sqrt(v_hat) + ADAM_EPS) + ADAM_WD * w_ref[...])

    results = pl.pallas_call(
        body, name=name, grid=(r // tr,),
        in_specs=([pl.BlockSpec((n_parts, tr, c), lambda i: (0, i, 0))] + [_tile(tr, c)] * 3) * n_items,
        out_specs=[_tile(tr, c)] * (4 * n_items),
        out_shape=[_sds((r, c), F32)] * (4 * n_items),
        compiler_params=_params(),
    )(*[a for item in items for a in item])
    return [results[4 * i:4 * i + 4] for i in range(n_items)]


def _sum_parts(parts, name):
    n_parts, r, c = parts.shape

    def body(p_ref, o_ref):
        acc = p_ref[0]
        for j in range(1, n_parts):
            acc = acc + p_ref[j]
        o_ref[...] = acc

    return pl.pallas_call(
        body, name=name, grid=(1,),
        in_specs=[_whole((n_parts, r, c))], out_specs=_whole((r, c)), out_shape=_sds((r, c), F32),
        compiler_params=_params(),
    )(parts)


def _row(v):
    return v.reshape(1, -1)


def _by_owner_rows(g):
    return g.reshape(N_DEV, g.shape[0] // N_DEV, g.shape[1])


WEIGHTS = ("mix_pre_g", "w_mix_in", "conv_a_w", "conv_b_w", "conv_b_b", "ln_b_g", "ln_b_b", "w_mix_out", "mix_post_g",
           "xa_pre_g", "mem_norm_g", "w_q", "w_k", "w_v", "w_o", "xa_post_g", "ffn_pre_g", "w_gate", "w_up", "w_down",
           "ffn_post_g")
LARGE = ("w_mix_in", "w_mix_out", "w_q", "w_k", "w_v", "w_o", "w_gate", "w_up", "w_down")
COLUMN_SHARDED = ("w_mix_in", "w_gate", "w_up")
GAINS = ("mix_pre_g", "mix_post_g", "xa_pre_g", "mem_norm_g", "xa_post_g", "ffn_pre_g", "ffn_post_g")
CHANNEL_VECTORS = ("conv_b_b", "ln_b_g", "ln_b_b")
CONV_TAPS = ("conv_a_w", "conv_b_w")
SMALL = GAINS + CHANNEL_VECTORS + CONV_TAPS
CONV_COLS_PER_DEVICE = D_A // N_DEV
TOKEN_TILE = 512
FFN_FWD_TOKEN_TILE = 512
FFN_TOKEN_TILE = 256
CONV_TOKEN_TILE = 256
WGRAD_TOKEN_TILE = 2048
ADAM_ROWS_PER_STEP = 64


def _lane_rows(v):
    flat = v.reshape(-1)
    tile = SUBLANES * LANES
    flat = jnp.pad(flat, (0, (-flat.shape[0]) % tile))
    return flat.reshape(-1, LANES)


def _pack_small(values, names):
    return jnp.concatenate([_lane_rows(values[n]) for n in names], axis=0)


def _unpack_small(packed, like, names):
    out, off = {}, 0
    for n in names:
        size = like[n].size
        rows = _lane_rows(like[n]).shape[0]
        out[n] = packed[off:off + rows, :].reshape(-1)[:size].reshape(like[n].shape)
        off += rows
    return out


def kernel(x, mem, mix_pre_g, w_mix_in, conv_a_w, conv_b_w, conv_b_b, ln_b_g, ln_b_b, w_mix_out, mix_post_g, xa_pre_g, mem_norm_g, w_q, w_k, w_v, w_o, xa_post_g, ffn_pre_g, w_gate, w_up, w_down, ffn_post_g, loss_target, m_mix_pre_g, m_w_mix_in, m_conv_a_w, m_conv_b_w, m_conv_b_b, m_ln_b_g, m_ln_b_b, m_w_mix_out, m_mix_post_g, m_xa_pre_g, m_mem_norm_g, m_w_q, m_w_k, m_w_v, m_w_o, m_xa_post_g, m_ffn_pre_g, m_w_gate, m_w_up, m_w_down, m_ffn_post_g, v_mix_pre_g, v_w_mix_in, v_conv_a_w, v_conv_b_w, v_conv_b_b, v_ln_b_g, v_ln_b_b, v_w_mix_out, v_mix_post_g, v_xa_pre_g, v_mem_norm_g, v_w_q, v_w_k, v_w_v, v_w_o, v_xa_post_g, v_ffn_pre_g, v_w_gate, v_w_up, v_w_down, v_ffn_post_g):
    given = dict(locals())
    w = {n: given[n] for n in WEIGHTS}
    m = {n: given["m_" + n] for n in WEIGHTS}
    v = {n: given["v_" + n] for n in WEIGHTS}
    xs, mems, target = x[0], mem[0], loss_target[0]
    t = xs.shape[0]
    tm, tm_ffn, tc, tk = min(TOKEN_TILE, t), min(FFN_TOKEN_TILE, t), min(CONV_TOKEN_TILE, t), min(WGRAD_TOKEN_TILE, t)
    g = {n: _row(w[n]) for n in GAINS}
    bb, lg, lb = (_row(w[n]) for n in CHANNEL_VECTORS)

    def shard_bf16(*names):
        return [w[n].astype(BF16) for n in names]

    def shard_bf16_t(*names):
        return [w[n].T.astype(BF16) for n in names]

    g_mix_in, g_taps = _comm_call(_Gather(shard_bf16_t("w_mix_in") + [_pack_small(w, CONV_TAPS)]), "gather_mixer")
    w_mix_in_t = g_mix_in.reshape(D_IN_ALL, D_MODEL)
    taps, off = {}, 0
    for n in CONV_TAPS:
        k, cols = w[n].shape
        rows = _lane_rows(w[n]).shape[0]
        blk = g_taps[:, off:off + rows, :].reshape(N_DEV, -1)[:, :k * cols].reshape(N_DEV, k, cols)
        taps[n] = blk.transpose(1, 0, 2).reshape(k, N_DEV * cols)
        off += rows
    wa, wb = taps["conv_a_w"], taps["conv_b_w"]

    (h1, u), gathered = _norm_matmul(xs, g["mix_pre_g"], w_mix_in_t.T, tm, "mix_in_fwd",
                                     _Gather(shard_bf16("w_mix_out", "w_q", "w_k", "w_v", "w_o")))
    w_mix_out, w_q, w_k, w_v, w_o = (a.reshape(D_MODEL, D_MODEL) for a in gathered)
    (ycat, zc, ca, x1, s1), (g_gate, g_up, g_down) = _conv_fwd(
        u, wa, wb, bb, lg, lb, xs, w_mix_out, g["mix_post_g"], tc, "conv_fwd",
        _Gather(shard_bf16_t("w_gate", "w_up") + shard_bf16("w_down")))
    w_gate_t, w_up_t, w_down = (a.reshape(D_FF, D_MODEL) for a in (g_gate, g_up, g_down))
    mem_n, kk, vv = _mem_fwd(mems, g["mem_norm_g"], w_k, w_v, "mem_fwd")
    x2, h2, q, o, a = _attn_fwd(x1, g["xa_pre_g"], w_q, kk, vv, w_o, g["xa_post_g"], tm, "attn_fwd")
    h3, gt, up, f, sq = _ffn_fwd(x2, g["ffn_pre_g"], w_gate_t, w_up_t, w_down, g["ffn_post_g"], target,
                                 min(FFN_FWD_TOKEN_TILE, t), "ffn_fwd")

    dx2, df, hd, dgt, dup, d_ffn_post, d_ffn_pre = _ffn_bwd(
        x2, f, target, gt, up, g["ffn_pre_g"], w_gate_t, w_up_t, w_down, g["ffn_post_g"], tm_ffn, "ffn_bwd")
    d_w_down = _wgrad(hd, df, tk, D_FF // 2, D_MODEL, "wgrad_down")
    d_w_gate_t = _wgrad(dgt, h3, tk, D_FF // 2, D_MODEL, "wgrad_gate")
    d_w_up_t = _wgrad(dup, h3, tk, D_FF // 2, D_MODEL, "wgrad_up")
    ffn_slabs = [_by_owner_rows(d) for d in (d_w_gate_t, d_w_up_t, d_w_down)]

    (dx1, da, dq, dk, dv, d_xa_post, d_xa_pre), from_ffn = _attn_bwd(
        dx2, a, x1, q, kk, vv, g["xa_pre_g"], w_q, w_o, g["xa_post_g"], tm, "attn_bwd",
        _Exchange(ffn_slabs, [True] * 3))
    d_w_o = _wgrad(o, da, tk, D_MODEL, D_MODEL, "wgrad_o")
    d_w_q = _wgrad(h2, dq, tk, D_MODEL, D_MODEL, "wgrad_q")
    d_w_k, d_w_v, d_mem_norm = _mem_bwd(mems, mem_n, dk, dv, g["mem_norm_g"], w_k, w_v, "mem_bwd")
    ds1, dycat, d_mix_post = _proj_bwd(dx1, s1, w_mix_out, g["mix_post_g"], tm, "mix_out_bwd")
    d_w_mix_out = _wgrad(ycat, ds1, tk, D_MODEL, D_MODEL, "wgrad_mix_out")
    attn_slabs = [_by_owner_rows(d) for d in (d_w_mix_out, d_w_q, d_w_k, d_w_v, d_w_o)]

    (du, d_conv_a, d_conv_b, d_conv_bb, d_ln_g, d_ln_b), from_attn = _conv_bwd(
        dycat, u, zc, ca, wa, wb, lg, lb, tc, "conv_bwd", _Exchange(attn_slabs, [True] * 5))
    small_grads = dict(conv_a_w=d_conv_a, conv_b_w=d_conv_b, conv_b_b=d_conv_bb, ln_b_g=d_ln_g, ln_b_b=d_ln_b,
                       mix_post_g=d_mix_post, xa_pre_g=d_xa_pre, mem_norm_g=d_mem_norm, xa_post_g=d_xa_post,
                       ffn_pre_g=d_ffn_pre, ffn_post_g=d_ffn_post, loss=sq)
    names = tuple(n for n in SMALL if n != "mix_pre_g") + ("loss", "mix_pre_g")
    half = D_MODEL // 2
    d_in_lo = _wgrad(du, h1, tk, D_IN_ALL // 2, half, "wgrad_mix_in_lo", cols=(0, half))
    d_in_hi, (from_in_lo, early_small) = _wgrad(
        du, h1, tk, D_IN_ALL // 2, half, "wgrad_mix_in_hi",
        _Exchange([_by_owner_rows(d_in_lo), _pack_small(small_grads, names[:-1])], [True, False]), cols=(half, half))
    dx, d_mix_pre = _in_bwd(du, w_mix_in_t, xs, dx1, g["mix_pre_g"], tm, "mix_in_bwd")
    small_grads["mix_pre_g"] = d_mix_pre
    from_in_hi, late_small = _comm_call(
        _Exchange([_by_owner_rows(d_in_hi), _lane_rows(d_mix_pre)], [True, False]), "reduce_tail")
    all_small = jnp.concatenate([early_small, late_small], axis=1)

    received = dict(zip(("w_gate", "w_up", "w_down"), from_ffn))
    received.update(zip(("w_mix_out", "w_q", "w_k", "w_v", "w_o"), from_attn))
    received["w_mix_in"] = jnp.concatenate([from_in_lo, from_in_hi], axis=2)
    grad, delta, new_m, new_v = {}, {}, {}, {}
    for group in (("w_mix_in",), ("w_gate",), ("w_up",), ("w_down",), ("w_mix_out", "w_q", "w_k", "w_v", "w_o")):
        items = [[received[n]] + [a[n].T if n in COLUMN_SHARDED else a[n] for a in (w, m, v)] for n in group]
        rows = items[0][1].shape[0]
        results = _adamw(items, min(rows, ADAM_ROWS_PER_STEP) if len(group) > 1 else rows, "adamw_" + group[0])
        for n, result in zip(group, results):
            grad[n], delta[n], new_m[n], new_v[n] = [r.T if n in COLUMN_SHARDED else r for r in result]

    total = _unpack_small(_sum_parts(all_small, "sum_small"), small_grads, names)
    loss = jnp.sum(total.pop("loss")) * (0.5 / D_MODEL)
    first_col = _device_index() * CONV_COLS_PER_DEVICE
    for n in CONV_TAPS:
        total[n] = lax.dynamic_slice_in_dim(total[n], first_col, CONV_COLS_PER_DEVICE, axis=1)
    total = {n: total[n].reshape(w[n].shape) for n in SMALL}
    packed_small = [_pack_small(values, SMALL) for values in (total, w, m, v)]
    ((g_s, d_s, nm_s, nv_s),) = _adamw([[packed_small[0][None]] + packed_small[1:]], packed_small[0].shape[0],
                                       "adamw_small")
    for out, p in ((grad, g_s), (delta, d_s), (new_m, nm_s), (new_v, nv_s)):
        out.update(_unpack_small(p, w, SMALL))

    return (loss, dx[None], *[grad[n] for n in WEIGHTS], *[delta[n] for n in WEIGHTS], *[new_m[n] for n in WEIGHTS],
            *[new_v[n] for n in WEIGHTS])
```

```python
import jax
import jax.numpy as jnp
from jax import lax
from jax.experimental import pallas as pl
from jax.experimental.pallas import tpu as pltpu

F32 = jnp.float32
BF16 = jnp.bfloat16

D_MODEL = 1024
D_A = 512
D_B = 512
D_IN_ALL = 2560
CONV_A_W = 3
CONV_B_W = 31
XA_HEADS = 4
XA_HEAD_DIM = 256
D_FF = 2816
N_DEV = 8
RMS_EPS = 1e-6
LN_EPS = 1e-5
ADAM_LR = 0.001
ADAM_B1 = 0.9
ADAM_B2 = 0.999
ADAM_EPS = 1e-08
ADAM_WD = 0.01
ADAM_STEP = 10

VMEM_LIMIT_BYTES = 56 * 1024 * 1024
SUBLANES = 8
LANES = 128
HALO_B = 32
HALO_A = 8
CONV_FWD_CHUNK = 16
CONV_CHUNK = 32
GATHER_FORWARD_STEPS_BEFORE_END = 8

MESH = pl.DeviceIdType.MESH


def _params(n_grid_axes=1):
    return pltpu.CompilerParams(dimension_semantics=("arbitrary",) * n_grid_axes, vmem_limit_bytes=VMEM_LIMIT_BYTES)


def _sds(shape, dtype):
    return jax.ShapeDtypeStruct(shape, dtype)


def _tile(rows, cols):
    return pl.BlockSpec((rows, cols), lambda i: (i, 0))


def _rtile(rows, cols, n):
    return pl.BlockSpec((rows, cols), lambda i: (n - 1 - i, 0))


def _whole(shape):
    zeros = (0,) * len(shape)
    return pl.BlockSpec(shape, lambda i: zeros)


def _resident(shape):
    zeros = (0,) * len(shape)
    return pl.BlockSpec(shape, lambda i: zeros, pipeline_mode=pl.Buffered(1))


def _dot(a, b):
    return jnp.dot(a, b, preferred_element_type=F32)


def _dot_nt(a, b):
    return lax.dot_general(a, b, (((1,), (1,)), ((), ())), preferred_element_type=F32)


def _dot_tn(a, b):
    return lax.dot_general(a, b, (((0,), (0,)), ((), ())), preferred_element_type=F32)


def _sigmoid(x):
    return 1.0 / (1.0 + jnp.exp(-x))


def _rms_fwd(x, g):
    r = lax.rsqrt(jnp.mean(x * x, axis=-1, keepdims=True) + RMS_EPS)
    return x * r * g


def _rms_bwd(dy, xin, g):
    r = lax.rsqrt(jnp.mean(xin * xin, axis=-1, keepdims=True) + RMS_EPS)
    n = xin * r
    dg = jnp.sum(dy * n, axis=0, keepdims=True)
    dn = dy * g
    dx = r * (dn - n * jnp.mean(dn * n, axis=-1, keepdims=True))
    return dx, dg


def _zero_at_first_step(*refs):
    @pl.when(pl.program_id(0) == 0)
    def _():
        for ref in refs:
            ref[...] = jnp.zeros(ref.shape, ref.dtype)


def _place():
    return lax.axis_index("x"), lax.axis_index("y"), lax.axis_index("c")


def _device_index():
    x, y, c = _place()
    return 4 * x + 2 * y + c


class _Gather:
    def __init__(self, arrays):
        self.arrays = list(arrays)
        self.out_shape = [_sds((N_DEV, *a.shape), a.dtype) for a in self.arrays]
        n = len(self.arrays)
        self.scratch_shapes = [pltpu.SemaphoreType.DMA((n, 7)), pltpu.SemaphoreType.DMA((n, 7)),
                               pltpu.SemaphoreType.DMA((n,))]

    def forward_step(self, n_steps):
        return max(0, n_steps - 1 - GATHER_FORWARD_STEPS_BEFORE_END)

    def bind(self, srcs, dsts, send_sems, recv_sems, local_sems):
        x, y, cc = _place()
        me, sibling = (x, y, cc), (x, y, 1 - cc)
        chips = [(1 - x, y), (x, 1 - y), (1 - x, 1 - y)]

        def copy(a, k, owner, to, src=None):
            slot = dsts[a].at[4 * owner[0] + 2 * owner[1] + owner[2]]
            return pltpu.make_async_remote_copy(
                src_ref=slot if src is None else src, dst_ref=slot, send_sem=send_sems.at[a, k],
                recv_sem=recv_sems.at[a, k], device_id=to, device_id_type=MESH)

        def first(a):
            return [copy(a, 0, me, sibling, src=srcs[a])] + [
                copy(a, 1 + j, me, (*chip, cc), src=srcs[a]) for j, chip in enumerate(chips)]

        def passed(a, j):
            return copy(a, 4 + j, (*chips[j], cc), sibling)

        def mine(a):
            return pltpu.make_async_copy(srcs[a], dsts[a].at[4 * x + 2 * y + cc], local_sems.at[a])

        def start():
            for a in range(len(srcs)):
                mine(a).start()
                for cp in first(a):
                    cp.start()

        def forward():
            for a in range(len(srcs)):
                for j, chip in enumerate(chips):
                    copy(a, 1 + j, (*chip, cc), me).wait_recv()
                    passed(a, j).start()

        def finish():
            for a in range(len(srcs)):
                copy(a, 0, sibling, me).wait_recv()
                for j, chip in enumerate(chips):
                    copy(a, 4 + j, (*chip, 1 - cc), me).wait_recv()
                for cp in first(a) + [passed(a, j) for j in range(len(chips))]:
                    cp.wait_send()
                mine(a).wait()

        return start, forward, finish


class _Exchange:
    def __init__(self, arrays, scatter):
        self.arrays = list(arrays)
        self.scatter = list(scatter)
        self.out_shape = [_sds(a.shape if s else (N_DEV, *a.shape), a.dtype) for a, s in zip(self.arrays, self.scatter)]
        n = len(self.arrays)
        self.scratch_shapes = [pltpu.SemaphoreType.DMA((n, 7)), pltpu.SemaphoreType.DMA((n, 7)),
                               pltpu.SemaphoreType.DMA((n,))]

    def forward_step(self, n_steps):
        return n_steps - 1

    def bind(self, srcs, dsts, send_sems, recv_sems, local_sems):
        me = _device_index()

        def copies(a):
            out = []
            for k in range(1, N_DEV):
                p = me ^ k
                out.append(pltpu.make_async_remote_copy(
                    src_ref=srcs[a].at[p] if self.scatter[a] else srcs[a], dst_ref=dsts[a].at[me],
                    send_sem=send_sems.at[a, k - 1], recv_sem=recv_sems.at[a, k - 1],
                    device_id=(p >> 2, (p >> 1) & 1, p & 1), device_id_type=MESH))
            return out

        def mine(a):
            return pltpu.make_async_copy(srcs[a].at[me] if self.scatter[a] else srcs[a], dsts[a].at[me], local_sems.at[a])

        def start():
            for a in range(len(srcs)):
                mine(a).start()
                for cp in copies(a):
                    cp.start()

        def forward():
            pass

        def finish():
            for a in range(len(srcs)):
                for cp in copies(a):
                    cp.wait()
                mine(a).wait()

        return start, forward, finish


def _hosted_call(core, comm, name, grid, in_specs, out_specs, out_shape, scratch_shapes, operands):
    grid = (grid,) if isinstance(grid, int) else tuple(grid)
    n_steps = 1
    for extent in grid:
        n_steps *= extent
    n_in, n_out, n_scr, n_arr = len(in_specs), len(out_specs), len(scratch_shapes), len(comm.arrays)
    any_spec = pl.BlockSpec(memory_space=pl.ANY)

    def body(*refs):
        ins, refs = refs[:n_in], refs[n_in:]
        srcs, refs = refs[:n_arr], refs[n_arr:]
        outs, refs = refs[:n_out], refs[n_out:]
        dsts, refs = refs[:n_arr], refs[n_arr:]
        scratch, sems = refs[:n_scr], refs[n_scr:]
        start, forward, finish = comm.bind(srcs, dsts, *sems)
        step = pl.program_id(0)
        for axis in range(1, len(grid)):
            step = step * grid[axis] + pl.program_id(axis)
        pl.when(step == 0)(start)
        core(*ins, *outs, *scratch)
        pl.when(step == comm.forward_step(n_steps))(forward)
        pl.when(step == n_steps - 1)(finish)

    results = pl.pallas_call(
        body, name=name, grid=grid,
        in_specs=list(in_specs) + [any_spec] * n_arr,
        out_specs=list(out_specs) + [any_spec] * n_arr,
        out_shape=list(out_shape) + comm.out_shape,
        scratch_shapes=list(scratch_shapes) + comm.scratch_shapes,
        compiler_params=_params(len(grid)),
    )(*operands, *comm.arrays)
    return results[:n_out], results[n_out:]


def _comm_call(comm, name):
    return _hosted_call(lambda: None, comm, name, 1, [], [], [], [], [])[1]


def _norm_matmul(x, g, w, tm, name, comm):
    t, d = x.shape
    n = w.shape[1]

    def core(x_ref, g_ref, w_ref, h_ref, o_ref):
        h = _rms_fwd(x_ref[...], g_ref[...]).astype(BF16)
        h_ref[...] = h
        o_ref[...] = _dot(h, w_ref[...]).astype(BF16)

    return _hosted_call(
        core, comm, name, t // tm,
        in_specs=[_tile(tm, d), _whole((1, d)), _resident((d, n))],
        out_specs=[_tile(tm, d), _tile(tm, n)],
        out_shape=[_sds((t, d), BF16), _sds((t, n), BF16)],
        scratch_shapes=[], operands=(x, g, w))


def _conv_fwd(u, wa, wb, bb, lg, lb, xres, w_out, g_post, tc, name, comm):
    t = u.shape[0]
    d = w_out.shape[1]
    chunk = CONV_FWD_CHUNK
    n_chunks = tc // chunk
    piece = 2 * LANES

    def core(u_ref, wa_ref, wb_ref, bb_ref, lg_ref, lb_ref, x_ref, wo_ref, go_ref,
             y_ref, zc_ref, ca_ref, x1_ref, s_ref, zbuf, cvbuf, zcbuf):
        @pl.when(pl.program_id(0) == 0)
        def _():
            zbuf[:, 0:HALO_B, :] = jnp.zeros((D_B // LANES, HALO_B, LANES), F32)
            cvbuf[:, 0:HALO_A, :] = jnp.zeros((D_A // LANES, HALO_A, LANES), F32)

        s = None
        for lb_i in range(D_A // LANES):
            if lb_i > 0 and lb_i % 2 == 0:
                cols = slice((lb_i - 2) * LANES, lb_i * LANES)
                part = _dot(y_ref[:, cols], wo_ref[cols, :])
                s = part if s is None else s + part
            lanes = slice(lb_i * LANES, (lb_i + 1) * LANES)
            c_a = u_ref[:, D_A + lb_i * LANES:D_A + (lb_i + 1) * LANES].astype(F32)
            v_a = u_ref[:, 2 * D_A + lb_i * LANES:2 * D_A + (lb_i + 1) * LANES].astype(F32)
            cvbuf[lb_i, HALO_A:HALO_A + tc, :] = c_a * v_a
            glu_v = u_ref[:, 3 * D_A + lb_i * LANES:3 * D_A + (lb_i + 1) * LANES].astype(F32)
            glu_g = u_ref[:, 3 * D_A + D_B + lb_i * LANES:3 * D_A + D_B + (lb_i + 1) * LANES].astype(F32)
            zbuf[lb_i, HALO_B:HALO_B + tc, :] = glu_v * _sigmoid(glu_g)

            for c in range(n_chunks):
                r0 = c * chunk
                rows = slice(r0, r0 + chunk)
                acc = jnp.zeros((chunk, LANES), F32)
                for k in range(CONV_A_W):
                    off = r0 + HALO_A - (CONV_A_W - 1) + k
                    acc = acc + wa_ref[k:k + 1, lanes] * cvbuf[lb_i, off:off + chunk, :]
                ca_ref[rows, lanes] = acc.astype(BF16)
                y_ref[rows, lanes] = (u_ref[rows, lanes].astype(F32) * acc).astype(BF16)

                accb = jnp.zeros((chunk, LANES), F32)
                for k in range(CONV_B_W):
                    off = r0 + HALO_B - (CONV_B_W - 1) + k
                    accb = accb + wb_ref[k:k + 1, lanes] * zbuf[lb_i, off:off + chunk, :]
                zcbuf[rows, lanes] = accb + bb_ref[:, lanes]

            zbuf[lb_i, 0:HALO_B, :] = zbuf[lb_i, tc:tc + HALO_B, :]
            cvbuf[lb_i, 0:HALO_A, :] = cvbuf[lb_i, tc:tc + HALO_A, :]

        cols = slice(D_A - piece, D_A)
        s = s + _dot(y_ref[:, cols], wo_ref[cols, :])

        for c in range(n_chunks):
            rows = slice(c * chunk, (c + 1) * chunk)
            zc = zcbuf[rows, :]
            zc_ref[rows, :] = zc.astype(BF16)
            mu = jnp.mean(zc, axis=-1, keepdims=True)
            xc = zc - mu
            var = jnp.mean(xc * xc, axis=-1, keepdims=True)
            ln = xc * lax.rsqrt(var + LN_EPS) * lg_ref[...] + lb_ref[...]
            y_ref[rows, D_A:D_A + D_B] = (ln * _sigmoid(ln)).astype(BF16)

        s = s + _dot(y_ref[:, D_A:D_A + D_B], wo_ref[D_A:D_A + D_B, :])
        s_ref[...] = s.astype(BF16)
        x1_ref[...] = x_ref[...] + _rms_fwd(s, go_ref[...])

    return _hosted_call(
        core, comm, name, t // tc,
        in_specs=[_tile(tc, D_IN_ALL), _whole((CONV_A_W, D_A)), _whole((CONV_B_W, D_B)), _whole((1, D_B)),
                  _whole((1, D_B)), _whole((1, D_B)), _tile(tc, d), _resident((D_A + D_B, d)), _whole((1, d))],
        out_specs=[_tile(tc, D_A + D_B), _tile(tc, D_B), _tile(tc, D_A), _tile(tc, d), _tile(tc, d)],
        out_shape=[_sds((t, D_A + D_B), BF16), _sds((t, D_B), BF16), _sds((t, D_A), BF16), _sds((t, d), F32),
                   _sds((t, d), BF16)],
        scratch_shapes=[pltpu.VMEM((D_B // LANES, HALO_B + tc, LANES), F32),
                        pltpu.VMEM((D_A // LANES, HALO_A + tc, LANES), F32), pltpu.VMEM((tc, D_B), F32)],
        operands=(u, wa, wb, bb, lg, lb, xres, w_out, g_post))


def _mem_fwd(mem, g, wk, wv, name):
    m, d = mem.shape

    def body(mem_ref, g_ref, wk_ref, wv_ref, n_ref, k_ref, v_ref):
        n = _rms_fwd(mem_ref[...], g_ref[...]).astype(BF16)
        n_ref[...] = n
        k_ref[...] = _dot(n, wk_ref[...]).astype(BF16)
        v_ref[...] = _dot(n, wv_ref[...]).astype(BF16)

    return pl.pallas_call(
        body, name=name, grid=(1,),
        in_specs=[_whole((m, d)), _whole((1, d)), _whole((d, d)), _whole((d, d))],
        out_specs=[_whole((m, d))] * 3,
        out_shape=[_sds((m, d), BF16)] * 3,
        compiler_params=_params(),
    )(mem, g, wk, wv)


def _softmax_rows(s):
    e = jnp.exp(s - jnp.max(s, axis=-1, keepdims=True))
    return e / jnp.sum(e, axis=-1, keepdims=True)


def _attn_fwd(x1, g_pre, wq, k, v, wo, g_post, tm, name):
    t, d = x1.shape
    m = k.shape[0]
    scale = XA_HEAD_DIM ** -0.5

    def body(x_ref, gp_ref, wq_ref, k_ref, v_ref, wo_ref, go_ref, x2_ref, h_ref, q_ref, o_ref, a_ref):
        x = x_ref[...]
        h = _rms_fwd(x, gp_ref[...]).astype(BF16)
        h_ref[...] = h
        q_ref[...] = _dot(h, wq_ref[...]).astype(BF16)
        for hd in range(XA_HEADS):
            cols = slice(hd * XA_HEAD_DIM, (hd + 1) * XA_HEAD_DIM)
            p = _softmax_rows(_dot_nt(q_ref[:, cols], k_ref[:, cols]) * scale)
            o_ref[:, cols] = _dot(p.astype(BF16), v_ref[:, cols]).astype(BF16)
        a = _dot(o_ref[...], wo_ref[...])
        a_ref[...] = a.astype(BF16)
        x2_ref[...] = x + _rms_fwd(a, go_ref[...])

    return pl.pallas_call(
        body, name=name, grid=(t // tm,),
        in_specs=[_tile(tm, d), _whole((1, d)), _resident((d, d)), _whole((m, d)), _whole((m, d)), _resident((d, d)),
                  _whole((1, d))],
        out_specs=[_tile(tm, d)] * 5,
        out_shape=[_sds((t, d), F32)] + [_sds((t, d), BF16)] * 4,
        compiler_params=_params(),
    )(x1, g_pre, wq, k, v, wo, g_post)


def _ffn_fwd(x2, g_pre, wg_t, wu_t, wd, g_post, target, tm, name):
    t, d = x2.shape
    f = wg_t.shape[0]

    def body(x_ref, gp_ref, wg_ref, wu_ref, wd_ref, go_ref, tgt_ref, h_ref, gt_ref, up_ref, f_ref, sq_ref):
        _zero_at_first_step(sq_ref)
        x = x_ref[...]
        h = _rms_fwd(x, gp_ref[...]).astype(BF16)
        h_ref[...] = h
        gt = _dot_nt(h, wg_ref[...])
        up = _dot_nt(h, wu_ref[...])
        gt_ref[...] = gt.astype(BF16)
        up_ref[...] = up.astype(BF16)
        hd = (gt * _sigmoid(gt) * up).astype(BF16)
        ff = _dot(hd, wd_ref[...])
        f_ref[...] = ff.astype(BF16)
        err = x + _rms_fwd(ff, go_ref[...]) - tgt_ref[...]
        sq_ref[...] += jnp.sum(err * err, axis=0, keepdims=True)

    return pl.pallas_call(
        body, name=name, grid=(t // tm,),
        in_specs=[_tile(tm, d), _whole((1, d)), _resident((f, d)), _resident((f, d)), _resident((f, d)), _whole((1, d)),
                  _tile(tm, d)],
        out_specs=[_tile(tm, d), _tile(tm, f), _tile(tm, f), _tile(tm, d), _whole((1, d))],
        out_shape=[_sds((t, d), BF16), _sds((t, f), BF16), _sds((t, f), BF16), _sds((t, d), BF16), _sds((1, d), F32)],
        compiler_params=_params(),
    )(x2, g_pre, wg_t, wu_t, wd, g_post, target)


def _ffn_bwd(x2, f, target, gt, up, g_pre, wg_t, wu_t, wd, g_post, tm, name):
    t, d = x2.shape
    ff = wg_t.shape[0]

    def body(x_ref, f_ref, tgt_ref, gt_ref, up_ref, gp_ref, wg_ref, wu_ref, wd_ref, go_ref,
             dx_ref, df_ref, hd_ref, dgt_ref, dup_ref, dgo_ref, dgp_ref):
        _zero_at_first_step(dgo_ref, dgp_ref)
        x = x_ref[...]
        fo = f_ref[...].astype(F32)
        dx3 = (x + _rms_fwd(fo, go_ref[...]) - tgt_ref[...]) * (1.0 / d)
        df, dgo = _rms_bwd(dx3, fo, go_ref[...])
        dgo_ref[...] += dgo
        df = df.astype(BF16)
        df_ref[...] = df
        dhd = _dot_nt(df, wd_ref[...])
        gt = gt_ref[...].astype(F32)
        up = up_ref[...].astype(F32)
        sg = _sigmoid(gt)
        si = gt * sg
        hd_ref[...] = (si * up).astype(BF16)
        dup = (dhd * si).astype(BF16)
        dgt = (dhd * up * (sg * (1.0 + gt * (1.0 - sg)))).astype(BF16)
        dup_ref[...] = dup
        dgt_ref[...] = dgt
        dh = _dot(dgt, wg_ref[...]) + _dot(dup, wu_ref[...])
        dxn, dgp = _rms_bwd(dh, x, gp_ref[...])
        dgp_ref[...] += dgp
        dx_ref[...] = dx3 + dxn

    return pl.pallas_call(
        body, name=name, grid=(t // tm,),
        in_specs=[_tile(tm, d), _tile(tm, d), _tile(tm, d), _tile(tm, ff), _tile(tm, ff), _whole((1, d)),
                  _resident((ff, d)), _resident((ff, d)), _resident((ff, d)), _whole((1, d))],
        out_specs=[_tile(tm, d), _tile(tm, d), _tile(tm, ff), _tile(tm, ff), _tile(tm, ff), _whole((1, d)), _whole((1, d))],
        out_shape=[_sds((t, d), F32), _sds((t, d), BF16), _sds((t, ff), BF16), _sds((t, ff), BF16), _sds((t, ff), BF16),
                   _sds((1, d), F32), _sds((1, d), F32)],
        compiler_params=_params(),
    )(x2, f, target, gt, up, g_pre, wg_t, wu_t, wd, g_post)


def _attn_bwd(dx2, a, x1, q, k, v, g_pre, wq, wo, g_post, s_mix, w_mix_out, g_mix_post, tm, name, comm):
    t, d = x1.shape
    m = k.shape[0]
    kdim = w_mix_out.shape[0]
    scale = XA_HEAD_DIM ** -0.5

    def core(dx2_ref, a_ref, x_ref, q_ref, k_ref, v_ref, gp_ref, wq_ref, wo_ref, go_ref, s_ref, wm_ref, gm_ref,
             dx1_ref, da_ref, dq_ref, dk_ref, dv_ref, dgo_ref, dgp_ref, ds_ref, dy_ref, dgm_ref, do_buf):
        _zero_at_first_step(dgo_ref, dgp_ref, dk_ref, dv_ref, dgm_ref)
        dx2 = dx2_ref[...]
        da, dgo = _rms_bwd(dx2, a_ref[...].astype(F32), go_ref[...])
        dgo_ref[...] += dgo
        da = da.astype(BF16)
        da_ref[...] = da
        do_buf[...] = _dot_nt(da, wo_ref[...]).astype(BF16)
        for hd in range(XA_HEADS):
            cols = slice(hd * XA_HEAD_DIM, (hd + 1) * XA_HEAD_DIM)
            qh = q_ref[:, cols]
            p = _softmax_rows(_dot_nt(qh, k_ref[:, cols]) * scale)
            do_h = do_buf[:, cols]
            dp = _dot_nt(do_h, v_ref[:, cols])
            dv_ref[:, cols] += _dot_tn(p.astype(BF16), do_h)
            ds = (p * (dp - jnp.sum(dp * p, axis=-1, keepdims=True)) * scale).astype(BF16)
            dq_ref[:, cols] = _dot(ds, k_ref[:, cols]).astype(BF16)
            dk_ref[:, cols] += _dot_tn(ds, qh)
        dh = _dot_nt(dq_ref[...], wq_ref[...])
        dxn, dgp = _rms_bwd(dh, x_ref[...], gp_ref[...])
        dgp_ref[...] += dgp
        dx1 = dx2 + dxn
        dx1_ref[...] = dx1
        ds, dgm = _rms_bwd(dx1, s_ref[...].astype(F32), gm_ref[...])
        dgm_ref[...] += dgm
        ds = ds.astype(BF16)
        ds_ref[...] = ds
        dy_ref[...] = _dot_nt(ds, wm_ref[...]).astype(BF16)

    return _hosted_call(
        core, comm, name, t // tm,
        in_specs=[_tile(tm, d), _tile(tm, d), _tile(tm, d), _tile(tm, d), _whole((m, d)), _whole((m, d)), _whole((1, d)),
                  _resident((d, d)), _resident((d, d)), _whole((1, d)), _tile(tm, d), _resident((kdim, d)), _whole((1, d))],
        out_specs=[_tile(tm, d), _tile(tm, d), _tile(tm, d), _whole((m, d)), _whole((m, d)), _whole((1, d)), _whole((1, d)),
                   _tile(tm, d), _tile(tm, kdim), _whole((1, d))],
        out_shape=[_sds((t, d), F32), _sds((t, d), BF16), _sds((t, d), BF16), _sds((m, d), F32), _sds((m, d), F32),
                   _sds((1, d), F32), _sds((1, d), F32), _sds((t, d), BF16), _sds((t, kdim), BF16), _sds((1, d), F32)],
        scratch_shapes=[pltpu.VMEM((tm, d), BF16)],
        operands=(dx2, a, x1, q, k, v, g_pre, wq, wo, g_post, s_mix, w_mix_out, g_mix_post))


def _mem_bwd(mem, mem_n, dk, dv, g, wk, wv, name):
    m, d = mem.shape

    def body(mem_ref, n_ref, dk_ref, dv_ref, g_ref, wk_ref, wv_ref, dwk_ref, dwv_ref, dg_ref):
        dk = dk_ref[...].astype(BF16)
        dv = dv_ref[...].astype(BF16)
        n = n_ref[...]
        dwk_ref[...] = _dot_tn(n, dk).astype(BF16)
        dwv_ref[...] = _dot_tn(n, dv).astype(BF16)
        dn = _dot_nt(dk, wk_ref[...]) + _dot_nt(dv, wv_ref[...])
        _, dg = _rms_bwd(dn, mem_ref[...], g_ref[...])
        dg_ref[...] = dg

    return pl.pallas_call(
        body, name=name, grid=(1,),
        in_specs=[_whole((m, d)), _whole((m, d)), _whole((m, d)), _whole((m, d)), _whole((1, d)), _whole((d, d)),
                  _whole((d, d))],
        out_specs=[_whole((d, d)), _whole((d, d)), _whole((1, d))],
        out_shape=[_sds((d, d), BF16), _sds((d, d), BF16), _sds((1, d), F32)],
        compiler_params=_params(),
    )(mem, mem_n, dk, dv, g, wk, wv)


def _conv_bwd(dy, u, zc, ca, wa, wb, lg, lb, tc, name, comm):
    t = u.shape[0]
    n_tiles = t // tc
    n_chunks = tc // CONV_CHUNK

    def core(dy_ref, u_ref, zc_ref, ca_ref, wa_ref, wb_ref, lg_ref, lb_ref,
             du_ref, dwa_ref, dwb_ref, dbb_ref, dlg_ref, dlb_ref, ebuf, eabuf, zbuf, cvbuf, gatebuf, wacc, aacc, vacc):
        step = pl.program_id(0)

        @pl.when(step == 0)
        def _():
            ebuf[:, tc:tc + HALO_B, :] = jnp.zeros((D_B // LANES, HALO_B, LANES), F32)
            eabuf[:, tc:tc + HALO_A, :] = jnp.zeros((D_A // LANES, HALO_A, LANES), F32)
            wacc[...] = jnp.zeros_like(wacc)
            aacc[...] = jnp.zeros_like(aacc)
            vacc[...] = jnp.zeros_like(vacc)

        dbb = jnp.zeros((SUBLANES, D_B), F32)
        dlg = jnp.zeros((SUBLANES, D_B), F32)
        dlb = jnp.zeros((SUBLANES, D_B), F32)
        for c in range(n_chunks):
            rows = slice(c * CONV_CHUNK, (c + 1) * CONV_CHUNK)
            dy_a = dy_ref[rows, 0:D_A].astype(F32)
            b_a = u_ref[rows, 0:D_A].astype(F32)
            du_ref[rows, 0:D_A] = (dy_a * ca_ref[rows, :].astype(F32)).astype(BF16)
            dca = dy_a * b_a
            cv = u_ref[rows, D_A:2 * D_A].astype(F32) * u_ref[rows, 2 * D_A:3 * D_A].astype(F32)
            gate = _sigmoid(u_ref[rows, 3 * D_A + D_B:3 * D_A + 2 * D_B].astype(F32))
            z = u_ref[rows, 3 * D_A:3 * D_A + D_B].astype(F32) * gate
            for lb_i in range(D_A // LANES):
                lanes = slice(lb_i * LANES, (lb_i + 1) * LANES)
                eabuf[lb_i, rows, :] = dca[:, lanes]
                cvbuf[lb_i, rows, :] = cv[:, lanes]
                zbuf[lb_i, rows, :] = z[:, lanes]
                gatebuf[lb_i, rows, :] = gate[:, lanes]

            zcv = zc_ref[rows, :].astype(F32)
            mu = jnp.mean(zcv, axis=-1, keepdims=True)
            xc = zcv - mu
            rstd = lax.rsqrt(jnp.mean(xc * xc, axis=-1, keepdims=True) + LN_EPS)
            xhat = xc * rstd
            ln = xhat * lg_ref[...] + lb_ref[...]
            sg = _sigmoid(ln)
            dln = dy_ref[rows, D_A:D_A + D_B].astype(F32) * (sg * (1.0 + ln * (1.0 - sg)))
            dlg = dlg + jnp.sum((dln * xhat).reshape(CONV_CHUNK // SUBLANES, SUBLANES, D_B), axis=0)
            dlb = dlb + jnp.sum(dln.reshape(CONV_CHUNK // SUBLANES, SUBLANES, D_B), axis=0)
            dxh = dln * lg_ref[...]
            dzc = rstd * (dxh - jnp.mean(dxh, axis=-1, keepdims=True) - xhat * jnp.mean(dxh * xhat, axis=-1, keepdims=True))
            dbb = dbb + jnp.sum(dzc.reshape(CONV_CHUNK // SUBLANES, SUBLANES, D_B), axis=0)
            for lb_i in range(D_B // LANES):
                ebuf[lb_i, rows, :] = dzc[:, lb_i * LANES:(lb_i + 1) * LANES]
        vacc[0] += dbb
        vacc[1] += dlg
        vacc[2] += dlb

        for lb_i in range(D_A // LANES):
            lanes = slice(lb_i * LANES, (lb_i + 1) * LANES)

            def cols(first):
                return slice(first + lb_i * LANES, first + (lb_i + 1) * LANES)

            for c in range(n_chunks):
                r0 = c * CONV_CHUNK
                rows = slice(r0, r0 + CONV_CHUNK)
                cv = cvbuf[lb_i, rows, :]
                dcv = jnp.zeros((CONV_CHUNK, LANES), F32)
                for k in range(CONV_A_W):
                    off = r0 + (CONV_A_W - 1) - k
                    e = eabuf[lb_i, off:off + CONV_CHUNK, :]
                    dcv = dcv + wa_ref[k:k + 1, lanes] * e
                    aacc[k, :, lanes] += jnp.sum((cv * e).reshape(CONV_CHUNK // SUBLANES, SUBLANES, LANES), axis=0)
                du_ref[rows, cols(D_A)] = (dcv * u_ref[rows, cols(2 * D_A)].astype(F32)).astype(BF16)
                du_ref[rows, cols(2 * D_A)] = (dcv * u_ref[rows, cols(D_A)].astype(F32)).astype(BF16)

                z = zbuf[lb_i, rows, :]
                dz = jnp.zeros((CONV_CHUNK, LANES), F32)
                for k in range(CONV_B_W):
                    off = r0 + (CONV_B_W - 1) - k
                    e = ebuf[lb_i, off:off + CONV_CHUNK, :]
                    dz = dz + wb_ref[k:k + 1, lanes] * e
                    wacc[k, :, lanes] += jnp.sum((z * e).reshape(CONV_CHUNK // SUBLANES, SUBLANES, LANES), axis=0)
                glu_v = u_ref[rows, cols(3 * D_A)].astype(F32)
                sgg = gatebuf[lb_i, rows, :]
                du_ref[rows, cols(3 * D_A)] = (dz * sgg).astype(BF16)
                du_ref[rows, cols(3 * D_A + D_B)] = (dz * glu_v * sgg * (1.0 - sgg)).astype(BF16)

            ebuf[lb_i, tc:tc + HALO_B, :] = ebuf[lb_i, 0:HALO_B, :]
            eabuf[lb_i, tc:tc + HALO_A, :] = eabuf[lb_i, 0:HALO_A, :]

        @pl.when(step == n_tiles - 1)
        def _():
            for k in range(CONV_B_W):
                dwb_ref[k:k + 1, :] = jnp.sum(wacc[k], axis=0, keepdims=True)
            for k in range(CONV_A_W):
                dwa_ref[k:k + 1, :] = jnp.sum(aacc[k], axis=0, keepdims=True)
            dbb_ref[...] = jnp.sum(vacc[0], axis=0, keepdims=True)
            dlg_ref[...] = jnp.sum(vacc[1], axis=0, keepdims=True)
            dlb_ref[...] = jnp.sum(vacc[2], axis=0, keepdims=True)

    return _hosted_call(
        core, comm, name, n_tiles,
        in_specs=[_rtile(tc, D_A + D_B, n_tiles), _rtile(tc, D_IN_ALL, n_tiles), _rtile(tc, D_B, n_tiles),
                  _rtile(tc, D_A, n_tiles), _whole((CONV_A_W, D_A)), _whole((CONV_B_W, D_B)), _whole((1, D_B)),
                  _whole((1, D_B))],
        out_specs=[_rtile(tc, D_IN_ALL, n_tiles), _whole((CONV_A_W, D_A)), _whole((CONV_B_W, D_B)), _whole((1, D_B)),
                   _whole((1, D_B)), _whole((1, D_B))],
        out_shape=[_sds((t, D_IN_ALL), BF16), _sds((CONV_A_W, D_A), F32), _sds((CONV_B_W, D_B), F32), _sds((1, D_B), F32),
                   _sds((1, D_B), F32), _sds((1, D_B), F32)],
        scratch_shapes=[pltpu.VMEM((D_B // LANES, tc + HALO_B, LANES), F32), pltpu.VMEM((D_A // LANES, tc + HALO_A, LANES), F32),
                        pltpu.VMEM((D_B // LANES, tc, LANES), F32), pltpu.VMEM((D_A // LANES, tc, LANES), F32),
                        pltpu.VMEM((D_B // LANES, tc, LANES), F32),
                        pltpu.VMEM((CONV_B_W, SUBLANES, D_B), F32), pltpu.VMEM((CONV_A_W, SUBLANES, D_A), F32),
                        pltpu.VMEM((3, SUBLANES, D_B), F32)],
        operands=(dy, u, zc, ca, wa, wb, lg, lb))


def _in_bwd(du, w_t, x, dx1, g, tm, name):
    t, d = x.shape
    n = w_t.shape[0]

    def body(du_ref, w_ref, x_ref, dx1_ref, g_ref, dx_ref, dg_ref):
        _zero_at_first_step(dg_ref)
        dh = _dot(du_ref[...], w_ref[...])
        dxn, dg = _rms_bwd(dh, x_ref[...], g_ref[...])
        dg_ref[...] += dg
        dx_ref[...] = dx1_ref[...] + dxn

    return pl.pallas_call(
        body, name=name, grid=(t // tm,),
        in_specs=[_tile(tm, n), _resident((n, d)), _tile(tm, d), _tile(tm, d), _whole((1, d))],
        out_specs=[_tile(tm, d), _whole((1, d))],
        out_shape=[_sds((t, d), F32), _sds((1, d), F32)],
        compiler_params=_params(),
    )(du, w_t, x, dx1, g)


def _wgrad(a, b, tk, bm, bn, name, comm=None, cols=None):
    t, m = a.shape
    first, n = (0, b.shape[1]) if cols is None else cols
    first_block = first // bn
    n_k = t // tk

    def body(a_ref, b_ref, o_ref, acc):
        @pl.when(pl.program_id(2) == 0)
        def _():
            acc[...] = jnp.zeros_like(acc)

        acc[...] += _dot_tn(a_ref[...], b_ref[...])

        @pl.when(pl.program_id(2) == n_k - 1)
        def _():
            o_ref[...] = acc[...].astype(BF16)

    call = dict(
        grid=(m // bm, n // bn, n_k),
        in_specs=[pl.BlockSpec((tk, bm), lambda i, j, k: (k, i)),
                  pl.BlockSpec((tk, bn), lambda i, j, k: (k, first_block + j))],
        out_specs=[pl.BlockSpec((bm, bn), lambda i, j, k: (i, j))],
        out_shape=[_sds((m, n), BF16)],
        scratch_shapes=[pltpu.VMEM((bm, bn), F32)])
    if comm is None:
        return pl.pallas_call(body, name=name, compiler_params=_params(3), **call)(a, b)[0]
    (out,), received = _hosted_call(body, comm, name, operands=(a, b), **call)
    return out, received


def _adamw(items, tr, name):
    n_parts, r, c = items[0][0].shape
    n_items = len(items)

    def body(*refs):
        ins, outs = refs[:4 * n_items], refs[4 * n_items:]
        for i in range(n_items):
            p_ref, w_ref, m_ref, v_ref = ins[4 * i:4 * i + 4]
            g_ref, d_ref, nm_ref, nv_ref = outs[4 * i:4 * i + 4]
            g = p_ref[0].astype(F32)
            for j in range(1, n_parts):
                g = g + p_ref[j].astype(F32)
            g_ref[...] = g
            nm = ADAM_B1 * m_ref[...] + (1.0 - ADAM_B1) * g
            nv = ADAM_B2 * v_ref[...] + (1.0 - ADAM_B2) * (g * g)
            nm_ref[...] = nm
            nv_ref[...] = nv
            m_hat = nm / (1.0 - ADAM_B1 ** ADAM_STEP)
            v_hat = nv / (1.0 - ADAM_B2 ** ADAM_STEP)
            d_ref[...] = -ADAM_LR * (m_hat / (jnp.sqrt(v_hat) + ADAM_EPS) + ADAM_WD * w_ref[...])

    results = pl.pallas_call(
        body, name=name, grid=(r // tr,),
        in_specs=([pl.BlockSpec((n_parts, tr, c), lambda i: (0, i, 0))] + [_tile(tr, c)] * 3) * n_items,
        out_specs=[_tile(tr, c)] * (4 * n_items),
        out_shape=[_sds((r, c), F32)] * (4 * n_items),
        compiler_params=_params(),
    )(*[a for item in items for a in item])
    return [results[4 * i:4 * i + 4] for i in range(n_items)]


def _sum_parts(parts, name):
    n_parts, r, c = parts.shape

    def body(p_ref, o_ref):
        acc = p_ref[0]
        for j in range(1, n_parts):
            acc = acc + p_ref[j]
        o_ref[...] = acc

    return pl.pallas_call(
        body, name=name, grid=(1,),
        in_specs=[_whole((n_parts, r, c))], out_specs=_whole((r, c)), out_shape=_sds((r, c), F32),
        compiler_params=_params(),
    )(parts)


def _row(v):
    return v.reshape(1, -1)


def _by_owner_rows(g):
    return g.reshape(N_DEV, g.shape[0] // N_DEV, g.shape[1])


WEIGHTS = ("mix_pre_g", "w_mix_in", "conv_a_w", "conv_b_w", "conv_b_b", "ln_b_g", "ln_b_b", "w_mix_out", "mix_post_g",
           "xa_pre_g", "mem_norm_g", "w_q", "w_k", "w_v", "w_o", "xa_post_g", "ffn_pre_g", "w_gate", "w_up", "w_down",
           "ffn_post_g")
LARGE = ("w_mix_in", "w_mix_out", "w_q", "w_k", "w_v", "w_o", "w_gate", "w_up", "w_down")
COLUMN_SHARDED = ("w_mix_in", "w_gate", "w_up")
GAINS = ("mix_pre_g", "mix_post_g", "xa_pre_g", "mem_norm_g", "xa_post_g", "ffn_pre_g", "ffn_post_g")
CHANNEL_VECTORS = ("conv_b_b", "ln_b_g", "ln_b_b")
CONV_TAPS = ("conv_a_w", "conv_b_w")
SMALL = GAINS + CHANNEL_VECTORS + CONV_TAPS
CONV_COLS_PER_DEVICE = D_A // N_DEV
TOKEN_TILE = 512
FFN_FWD_TOKEN_TILE = 512
FFN_TOKEN_TILE = 256
CONV_TOKEN_TILE = 256
WGRAD_TOKEN_TILE = 2048
ADAM_ROWS_PER_STEP = 64


def _lane_rows(v):
    flat = v.reshape(-1)
    tile = SUBLANES * LANES
    flat = jnp.pad(flat, (0, (-flat.shape[0]) % tile))
    return flat.reshape(-1, LANES)


def _pack_small(values, names):
    return jnp.concatenate([_lane_rows(values[n]) for n in names], axis=0)


def _unpack_small(packed, like, names):
    out, off = {}, 0
    for n in names:
        size = like[n].size
        rows = _lane_rows(like[n]).shape[0]
        out[n] = packed[off:off + rows, :].reshape(-1)[:size].reshape(like[n].shape)
        off += rows
    return out


def kernel(x, mem, mix_pre_g, w_mix_in, conv_a_w, conv_b_w, conv_b_b, ln_b_g, ln_b_b, w_mix_out, mix_post_g, xa_pre_g, mem_norm_g, w_q, w_k, w_v, w_o, xa_post_g, ffn_pre_g, w_gate, w_up, w_down, ffn_post_g, loss_target, m_mix_pre_g, m_w_mix_in, m_conv_a_w, m_conv_b_w, m_conv_b_b, m_ln_b_g, m_ln_b_b, m_w_mix_out, m_mix_post_g, m_xa_pre_g, m_mem_norm_g, m_w_q, m_w_k, m_w_v, m_w_o, m_xa_post_g, m_ffn_pre_g, m_w_gate, m_w_up, m_w_down, m_ffn_post_g, v_mix_pre_g, v_w_mix_in, v_conv_a_w, v_conv_b_w, v_conv_b_b, v_ln_b_g, v_ln_b_b, v_w_mix_out, v_mix_post_g, v_xa_pre_g, v_mem_norm_g, v_w_q, v_w_k, v_w_v, v_w_o, v_xa_post_g, v_ffn_pre_g, v_w_gate, v_w_up, v_w_down, v_ffn_post_g):
    given = dict(locals())
    w = {n: given[n] for n in WEIGHTS}
    m = {n: given["m_" + n] for n in WEIGHTS}
    v = {n: given["v_" + n] for n in WEIGHTS}
    xs, mems, target = x[0], mem[0], loss_target[0]
    t = xs.shape[0]
    tm, tm_ffn, tc, tk = min(TOKEN_TILE, t), min(FFN_TOKEN_TILE, t), min(CONV_TOKEN_TILE, t), min(WGRAD_TOKEN_TILE, t)
    g = {n: _row(w[n]) for n in GAINS}
    bb, lg, lb = (_row(w[n]) for n in CHANNEL_VECTORS)

    def shard_bf16(*names):
        return [w[n].astype(BF16) for n in names]

    def shard_bf16_t(*names):
        return [w[n].T.astype(BF16) for n in names]

    g_mix_in, g_taps = _comm_call(_Gather(shard_bf16_t("w_mix_in") + [_pack_small(w, CONV_TAPS)]), "gather_mixer")
    w_mix_in_t = g_mix_in.reshape(D_IN_ALL, D_MODEL)
    taps, off = {}, 0
    for n in CONV_TAPS:
        k, cols = w[n].shape
        rows = _lane_rows(w[n]).shape[0]
        blk = g_taps[:, off:off + rows, :].reshape(N_DEV, -1)[:, :k * cols].reshape(N_DEV, k, cols)
        taps[n] = blk.transpose(1, 0, 2).reshape(k, N_DEV * cols)
        off += rows
    wa, wb = taps["conv_a_w"], taps["conv_b_w"]

    (h1, u), gathered = _norm_matmul(xs, g["mix_pre_g"], w_mix_in_t.T, tm, "mix_in_fwd",
                                     _Gather(shard_bf16("w_mix_out", "w_q", "w_k", "w_v", "w_o")))
    w_mix_out, w_q, w_k, w_v, w_o = (a.reshape(D_MODEL, D_MODEL) for a in gathered)
    (ycat, zc, ca, x1, s1), (g_gate, g_up, g_down) = _conv_fwd(
        u, wa, wb, bb, lg, lb, xs, w_mix_out, g["mix_post_g"], tc, "conv_fwd",
        _Gather(shard_bf16_t("w_gate", "w_up") + shard_bf16("w_down")))
    w_gate_t, w_up_t, w_down = (a.reshape(D_FF, D_MODEL) for a in (g_gate, g_up, g_down))
    mem_n, kk, vv = _mem_fwd(mems, g["mem_norm_g"], w_k, w_v, "mem_fwd")
    x2, h2, q, o, a = _attn_fwd(x1, g["xa_pre_g"], w_q, kk, vv, w_o, g["xa_post_g"], tm, "attn_fwd")
    h3, gt, up, f, sq = _ffn_fwd(x2, g["ffn_pre_g"], w_gate_t, w_up_t, w_down, g["ffn_post_g"], target,
                                 min(FFN_FWD_TOKEN_TILE, t), "ffn_fwd")

    dx2, df, hd, dgt, dup, d_ffn_post, d_ffn_pre = _ffn_bwd(
        x2, f, target, gt, up, g["ffn_pre_g"], w_gate_t, w_up_t, w_down, g["ffn_post_g"], tm_ffn, "ffn_bwd")
    d_w_down = _wgrad(hd, df, tk, D_FF // 2, D_MODEL, "wgrad_down")
    d_w_gate_t = _wgrad(dgt, h3, tk, D_FF // 2, D_MODEL, "wgrad_gate")
    d_w_up_t = _wgrad(dup, h3, tk, D_FF // 2, D_MODEL, "wgrad_up")
    ffn_slabs = [_by_owner_rows(d) for d in (d_w_gate_t, d_w_up_t, d_w_down)]

    (dx1, da, dq, dk, dv, d_xa_post, d_xa_pre, ds1, dycat, d_mix_post), from_ffn = _attn_bwd(
        dx2, a, x1, q, kk, vv, g["xa_pre_g"], w_q, w_o, g["xa_post_g"], s1, w_mix_out, g["mix_post_g"], tm, "attn_bwd",
        _Exchange(ffn_slabs, [True] * 3))
    d_w_o = _wgrad(o, da, tk, D_MODEL, D_MODEL, "wgrad_o")
    d_w_q = _wgrad(h2, dq, tk, D_MODEL, D_MODEL, "wgrad_q")
    d_w_k, d_w_v, d_mem_norm = _mem_bwd(mems, mem_n, dk, dv, g["mem_norm_g"], w_k, w_v, "mem_bwd")
    d_w_mix_out = _wgrad(ycat, ds1, tk, D_MODEL, D_MODEL, "wgrad_mix_out")
    attn_slabs = [_by_owner_rows(d) for d in (d_w_mix_out, d_w_q, d_w_k, d_w_v, d_w_o)]

    (du, d_conv_a, d_conv_b, d_conv_bb, d_ln_g, d_ln_b), from_attn = _conv_bwd(
        dycat, u, zc, ca, wa, wb, lg, lb, tc, "conv_bwd", _Exchange(attn_slabs, [True] * 5))
    small_grads = dict(conv_a_w=d_conv_a, conv_b_w=d_conv_b, conv_b_b=d_conv_bb, ln_b_g=d_ln_g, ln_b_b=d_ln_b,
                       mix_post_g=d_mix_post, xa_pre_g=d_xa_pre, mem_norm_g=d_mem_norm, xa_post_g=d_xa_post,
                       ffn_pre_g=d_ffn_pre, ffn_post_g=d_ffn_post, loss=sq)
    names = tuple(n for n in SMALL if n != "mix_pre_g") + ("loss", "mix_pre_g")
    half = D_MODEL // 2
    d_in_lo = _wgrad(du, h1, tk, D_IN_ALL // 2, half, "wgrad_mix_in_lo", cols=(0, half))
    d_in_hi, (from_in_lo, early_small) = _wgrad(
        du, h1, tk, D_IN_ALL // 2, half, "wgrad_mix_in_hi",
        _Exchange([_by_owner_rows(d_in_lo), _pack_small(small_grads, names[:-1])], [True, False]), cols=(half, half))
    dx, d_mix_pre = _in_bwd(du, w_mix_in_t, xs, dx1, g["mix_pre_g"], tm, "mix_in_bwd")
    small_grads["mix_pre_g"] = d_mix_pre
    from_in_hi, late_small = _comm_call(
        _Exchange([_by_owner_rows(d_in_hi), _lane_rows(d_mix_pre)], [True, False]), "reduce_tail")
    all_small = jnp.concatenate([early_small, late_small], axis=1)

    received = dict(zip(("w_gate", "w_up", "w_down"), from_ffn))
    received.update(zip(("w_mix_out", "w_q", "w_k", "w_v", "w_o"), from_attn))
    received["w_mix_in"] = jnp.concatenate([from_in_lo, from_in_hi], axis=2)
    grad, delta, new_m, new_v = {}, {}, {}, {}
    for group in (("w_mix_in",), ("w_gate",), ("w_up",), ("w_down",), ("w_mix_out", "w_q", "w_k", "w_v", "w_o")):
        items = [[received[n]] + [a[n].T if n in COLUMN_SHARDED else a[n] for a in (w, m, v)] for n in group]
        rows = items[0][1].shape[0]
        results = _adamw(items, min(rows, ADAM_ROWS_PER_STEP) if len(group) > 1 else rows, "adamw_" + group[0])
        for n, result in zip(group, results):
            grad[n], delta[n], new_m[n], new_v[n] = [r.T if n in COLUMN_SHARDED else r for r in result]

    total = _unpack_small(_sum_parts(all_small, "sum_small"), small_grads, names)
    loss = jnp.sum(total.pop("loss")) * (0.5 / D_MODEL)
    first_col = _device_index() * CONV_COLS_PER_DEVICE
    for n in CONV_TAPS:
        total[n] = lax.dynamic_slice_in_dim(total[n], first_col, CONV_COLS_PER_DEVICE, axis=1)
    total = {n: total[n].reshape(w[n].shape) for n in SMALL}
    packed_small = [_pack_small(values, SMALL) for values in (total, w, m, v)]
    ((g_s, d_s, nm_s, nv_s),) = _adamw([[packed_small[0][None]] + packed_small[1:]], packed_small[0].shape[0],
                                       "adamw_small")
    for out, p in ((grad, g_s), (delta, d_s), (new_m, nm_s), (new_v, nv_s)):
        out.update(_unpack_small(p, w, SMALL))

    return (loss, dx[None], *[grad[n] for n in WEIGHTS], *[delta[n] for n in WEIGHTS], *[new_m[n] for n in WEIGHTS],
            *[new_v[n] for n in WEIGHTS])
```

```python
import jax
import jax.numpy as jnp
from jax import lax
from jax.experimental import pallas as pl
from jax.experimental.pallas import tpu as pltpu

F32 = jnp.float32
BF16 = jnp.bfloat16

D_MODEL = 1024
D_A = 512
D_B = 512
D_IN_ALL = 2560
CONV_A_W = 3
CONV_B_W = 31
XA_HEADS = 4
XA_HEAD_DIM = 256
D_FF = 2816
N_DEV = 8
RMS_EPS = 1e-6
LN_EPS = 1e-5
ADAM_LR = 0.001
ADAM_B1 = 0.9
ADAM_B2 = 0.999
ADAM_EPS = 1e-08
ADAM_WD = 0.01
ADAM_STEP = 10

VMEM_LIMIT_BYTES = 56 * 1024 * 1024
SUBLANES = 8
LANES = 128
HALO_B = 32
HALO_A = 8
CONV_FWD_CHUNK = 16
CONV_CHUNK = 32
GATHER_FORWARD_STEPS_BEFORE_END = 8

MESH = pl.DeviceIdType.MESH


def _params(n_grid_axes=1):
    return pltpu.CompilerParams(dimension_semantics=("arbitrary",) * n_grid_axes, vmem_limit_bytes=VMEM_LIMIT_BYTES)


def _sds(shape, dtype):
    return jax.ShapeDtypeStruct(shape, dtype)


def _tile(rows, cols):
    return pl.BlockSpec((rows, cols), lambda i: (i, 0))


def _rtile(rows, cols, n):
    return pl.BlockSpec((rows, cols), lambda i: (n - 1 - i, 0))


def _whole(shape):
    zeros = (0,) * len(shape)
    return pl.BlockSpec(shape, lambda i: zeros)


def _resident(shape):
    zeros = (0,) * len(shape)
    return pl.BlockSpec(shape, lambda i: zeros, pipeline_mode=pl.Buffered(1))


def _dot(a, b):
    return jnp.dot(a, b, preferred_element_type=F32)


def _dot_nt(a, b):
    return lax.dot_general(a, b, (((1,), (1,)), ((), ())), preferred_element_type=F32)


def _dot_tn(a, b):
    return lax.dot_general(a, b, (((0,), (0,)), ((), ())), preferred_element_type=F32)


def _sigmoid(x):
    return 1.0 / (1.0 + jnp.exp(-x))


def _rms_fwd(x, g):
    r = lax.rsqrt(jnp.mean(x * x, axis=-1, keepdims=True) + RMS_EPS)
    return x * r * g


def _rms_bwd(dy, xin, g):
    r = lax.rsqrt(jnp.mean(xin * xin, axis=-1, keepdims=True) + RMS_EPS)
    n = xin * r
    dg = jnp.sum(dy * n, axis=0, keepdims=True)
    dn = dy * g
    dx = r * (dn - n * jnp.mean(dn * n, axis=-1, keepdims=True))
    return dx, dg


def _zero_at_first_step(*refs):
    @pl.when(pl.program_id(0) == 0)
    def _():
        for ref in refs:
            ref[...] = jnp.zeros(ref.shape, ref.dtype)


def _place():
    return lax.axis_index("x"), lax.axis_index("y"), lax.axis_index("c")


def _device_index():
    x, y, c = _place()
    return 4 * x + 2 * y + c


class _Gather:
    def __init__(self, arrays):
        self.arrays = list(arrays)
        self.out_shape = [_sds((N_DEV, *a.shape), a.dtype) for a in self.arrays]
        n = len(self.arrays)
        self.scratch_shapes = [pltpu.SemaphoreType.DMA((n, 7)), pltpu.SemaphoreType.DMA((n, 7)),
                               pltpu.SemaphoreType.DMA((n,))]

    def forward_step(self, n_steps):
        return max(0, n_steps - 1 - GATHER_FORWARD_STEPS_BEFORE_END)

    def bind(self, srcs, dsts, send_sems, recv_sems, local_sems):
        x, y, cc = _place()
        me, sibling = (x, y, cc), (x, y, 1 - cc)
        chips = [(1 - x, y), (x, 1 - y), (1 - x, 1 - y)]

        def copy(a, k, owner, to, src=None):
            slot = dsts[a].at[4 * owner[0] + 2 * owner[1] + owner[2]]
            return pltpu.make_async_remote_copy(
                src_ref=slot if src is None else src, dst_ref=slot, send_sem=send_sems.at[a, k],
                recv_sem=recv_sems.at[a, k], device_id=to, device_id_type=MESH)

        def first(a):
            return [copy(a, 0, me, sibling, src=srcs[a])] + [
                copy(a, 1 + j, me, (*chip, cc), src=srcs[a]) for j, chip in enumerate(chips)]

        def passed(a, j):
            return copy(a, 4 + j, (*chips[j], cc), sibling)

        def mine(a):
            return pltpu.make_async_copy(srcs[a], dsts[a].at[4 * x + 2 * y + cc], local_sems.at[a])

        def start():
            for a in range(len(srcs)):
                mine(a).start()
                for cp in first(a):
                    cp.start()

        def forward():
            for a in range(len(srcs)):
                for j, chip in enumerate(chips):
                    copy(a, 1 + j, (*chip, cc), me).wait_recv()
                    passed(a, j).start()

        def finish():
            for a in range(len(srcs)):
                copy(a, 0, sibling, me).wait_recv()
                for j, chip in enumerate(chips):
                    copy(a, 4 + j, (*chip, 1 - cc), me).wait_recv()
                for cp in first(a) + [passed(a, j) for j in range(len(chips))]:
                    cp.wait_send()
                mine(a).wait()

        return start, forward, finish


class _Exchange:
    def __init__(self, arrays, scatter):
        self.arrays = list(arrays)
        self.scatter = list(scatter)
        self.out_shape = [_sds(a.shape if s else (N_DEV, *a.shape), a.dtype) for a, s in zip(self.arrays, self.scatter)]
        n = len(self.arrays)
        self.scratch_shapes = [pltpu.SemaphoreType.DMA((n, 7)), pltpu.SemaphoreType.DMA((n, 7)),
                               pltpu.SemaphoreType.DMA((n,))]

    def forward_step(self, n_steps):
        return n_steps - 1

    def bind(self, srcs, dsts, send_sems, recv_sems, local_sems):
        me = _device_index()

        def copies(a):
            out = []
            for k in range(1, N_DEV):
                p = me ^ k
                out.append(pltpu.make_async_remote_copy(
                    src_ref=srcs[a].at[p] if self.scatter[a] else srcs[a], dst_ref=dsts[a].at[me],
                    send_sem=send_sems.at[a, k - 1], recv_sem=recv_sems.at[a, k - 1],
                    device_id=(p >> 2, (p >> 1) & 1, p & 1), device_id_type=MESH))
            return out

        def mine(a):
            return pltpu.make_async_copy(srcs[a].at[me] if self.scatter[a] else srcs[a], dsts[a].at[me], local_sems.at[a])

        def start():
            for a in range(len(srcs)):
                mine(a).start()
                for cp in copies(a):
                    cp.start()

        def forward():
            pass

        def finish():
            for a in range(len(srcs)):
                for cp in copies(a):
                    cp.wait()
                mine(a).wait()

        return start, forward, finish


def _hosted_call(core, comm, name, grid, in_specs, out_specs, out_shape, scratch_shapes, operands):
    grid = (grid,) if isinstance(grid, int) else tuple(grid)
    n_steps = 1
    for extent in grid:
        n_steps *= extent
    n_in, n_out, n_scr, n_arr = len(in_specs), len(out_specs), len(scratch_shapes), len(comm.arrays)
    any_spec = pl.BlockSpec(memory_space=pl.ANY)

    def body(*refs):
        ins, refs = refs[:n_in], refs[n_in:]
        srcs, refs = refs[:n_arr], refs[n_arr:]
        outs, refs = refs[:n_out], refs[n_out:]
        dsts, refs = refs[:n_arr], refs[n_arr:]
        scratch, sems = refs[:n_scr], refs[n_scr:]
        start, forward, finish = comm.bind(srcs, dsts, *sems)
        step = pl.program_id(0)
        for axis in range(1, len(grid)):
            step = step * grid[axis] + pl.program_id(axis)
        pl.when(step == 0)(start)
        core(*ins, *outs, *scratch)
        pl.when(step == comm.forward_step(n_steps))(forward)
        pl.when(step == n_steps - 1)(finish)

    results = pl.pallas_call(
        body, name=name, grid=grid,
        in_specs=list(in_specs) + [any_spec] * n_arr,
        out_specs=list(out_specs) + [any_spec] * n_arr,
        out_shape=list(out_shape) + comm.out_shape,
        scratch_shapes=list(scratch_shapes) + comm.scratch_shapes,
        compiler_params=_params(len(grid)),
    )(*operands, *comm.arrays)
    return results[:n_out], results[n_out:]


def _comm_call(comm, name):
    return _hosted_call(lambda: None, comm, name, 1, [], [], [], [], [])[1]


def _norm_matmul(x, g, w, tm, name, comm):
    t, d = x.shape
    n = w.shape[1]

    def core(x_ref, g_ref, w_ref, h_ref, o_ref):
        h = _rms_fwd(x_ref[...], g_ref[...]).astype(BF16)
        h_ref[...] = h
        o_ref[...] = _dot(h, w_ref[...]).astype(BF16)

    return _hosted_call(
        core, comm, name, t // tm,
        in_specs=[_tile(tm, d), _whole((1, d)), _resident((d, n))],
        out_specs=[_tile(tm, d), _tile(tm, n)],
        out_shape=[_sds((t, d), BF16), _sds((t, n), BF16)],
        scratch_shapes=[], operands=(x, g, w))


def _conv_fwd(u, wa, wb, bb, lg, lb, xres, w_out, g_post, tc, name, comm):
    t = u.shape[0]
    d = w_out.shape[1]
    chunk = CONV_FWD_CHUNK
    n_chunks = tc // chunk
    piece = 2 * LANES

    def core(u_ref, wa_ref, wb_ref, bb_ref, lg_ref, lb_ref, x_ref, wo_ref, go_ref,
             y_ref, zc_ref, ca_ref, x1_ref, s_ref, zbuf, cvbuf, zcbuf):
        @pl.when(pl.program_id(0) == 0)
        def _():
            zbuf[:, 0:HALO_B, :] = jnp.zeros((D_B // LANES, HALO_B, LANES), F32)
            cvbuf[:, 0:HALO_A, :] = jnp.zeros((D_A // LANES, HALO_A, LANES), F32)

        s = None
        for lb_i in range(D_A // LANES):
            if lb_i > 0 and lb_i % 2 == 0:
                cols = slice((lb_i - 2) * LANES, lb_i * LANES)
                part = _dot(y_ref[:, cols], wo_ref[cols, :])
                s = part if s is None else s + part
            lanes = slice(lb_i * LANES, (lb_i + 1) * LANES)
            c_a = u_ref[:, D_A + lb_i * LANES:D_A + (lb_i + 1) * LANES].astype(F32)
            v_a = u_ref[:, 2 * D_A + lb_i * LANES:2 * D_A + (lb_i + 1) * LANES].astype(F32)
            cvbuf[lb_i, HALO_A:HALO_A + tc, :] = c_a * v_a
            glu_v = u_ref[:, 3 * D_A + lb_i * LANES:3 * D_A + (lb_i + 1) * LANES].astype(F32)
            glu_g = u_ref[:, 3 * D_A + D_B + lb_i * LANES:3 * D_A + D_B + (lb_i + 1) * LANES].astype(F32)
            zbuf[lb_i, HALO_B:HALO_B + tc, :] = glu_v * _sigmoid(glu_g)

            for c in range(n_chunks):
                r0 = c * chunk
                rows = slice(r0, r0 + chunk)
                acc = jnp.zeros((chunk, LANES), F32)
                for k in range(CONV_A_W):
                    off = r0 + HALO_A - (CONV_A_W - 1) + k
                    acc = acc + wa_ref[k:k + 1, lanes] * cvbuf[lb_i, off:off + chunk, :]
                ca_ref[rows, lanes] = acc.astype(BF16)
                y_ref[rows, lanes] = (u_ref[rows, lanes].astype(F32) * acc).astype(BF16)

                accb = jnp.zeros((chunk, LANES), F32)
                for k in range(CONV_B_W):
                    off = r0 + HALO_B - (CONV_B_W - 1) + k
                    accb = accb + wb_ref[k:k + 1, lanes] * zbuf[lb_i, off:off + chunk, :]
                zcbuf[rows, lanes] = accb + bb_ref[:, lanes]

            zbuf[lb_i, 0:HALO_B, :] = zbuf[lb_i, tc:tc + HALO_B, :]
            cvbuf[lb_i, 0:HALO_A, :] = cvbuf[lb_i, tc:tc + HALO_A, :]

        cols = slice(D_A - piece, D_A)
        s = s + _dot(y_ref[:, cols], wo_ref[cols, :])

        for c in range(n_chunks):
            rows = slice(c * chunk, (c + 1) * chunk)
            zc = zcbuf[rows, :]
            zc_ref[rows, :] = zc.astype(BF16)
            mu = jnp.mean(zc, axis=-1, keepdims=True)
            xc = zc - mu
            var = jnp.mean(xc * xc, axis=-1, keepdims=True)
            ln = xc * lax.rsqrt(var + LN_EPS) * lg_ref[...] + lb_ref[...]
            y_ref[rows, D_A:D_A + D_B] = (ln * _sigmoid(ln)).astype(BF16)

        s = s + _dot(y_ref[:, D_A:D_A + D_B], wo_ref[D_A:D_A + D_B, :])
        s_ref[...] = s.astype(BF16)
        x1_ref[...] = x_ref[...] + _rms_fwd(s, go_ref[...])

    return _hosted_call(
        core, comm, name, t // tc,
        in_specs=[_tile(tc, D_IN_ALL), _whole((CONV_A_W, D_A)), _whole((CONV_B_W, D_B)), _whole((1, D_B)),
                  _whole((1, D_B)), _whole((1, D_B)), _tile(tc, d), _resident((D_A + D_B, d)), _whole((1, d))],
        out_specs=[_tile(tc, D_A + D_B), _tile(tc, D_B), _tile(tc, D_A), _tile(tc, d), _tile(tc, d)],
        out_shape=[_sds((t, D_A + D_B), BF16), _sds((t, D_B), BF16), _sds((t, D_A), BF16), _sds((t, d), F32),
                   _sds((t, d), BF16)],
        scratch_shapes=[pltpu.VMEM((D_B // LANES, HALO_B + tc, LANES), F32),
                        pltpu.VMEM((D_A // LANES, HALO_A + tc, LANES), F32), pltpu.VMEM((tc, D_B), F32)],
        operands=(u, wa, wb, bb, lg, lb, xres, w_out, g_post))


def _mem_fwd(mem, g, wk, wv, name):
    m, d = mem.shape

    def body(mem_ref, g_ref, wk_ref, wv_ref, n_ref, k_ref, v_ref):
        n = _rms_fwd(mem_ref[...], g_ref[...]).astype(BF16)
        n_ref[...] = n
        k_ref[...] = _dot(n, wk_ref[...]).astype(BF16)
        v_ref[...] = _dot(n, wv_ref[...]).astype(BF16)

    return pl.pallas_call(
        body, name=name, grid=(1,),
        in_specs=[_whole((m, d)), _whole((1, d)), _whole((d, d)), _whole((d, d))],
        out_specs=[_whole((m, d))] * 3,
        out_shape=[_sds((m, d), BF16)] * 3,
        compiler_params=_params(),
    )(mem, g, wk, wv)


def _softmax_rows(s):
    e = jnp.exp(s - jnp.max(s, axis=-1, keepdims=True))
    return e / jnp.sum(e, axis=-1, keepdims=True)


def _attn_fwd(x1, g_pre, wq, k, v, wo, g_post, tm, name):
    t, d = x1.shape
    m = k.shape[0]
    scale = XA_HEAD_DIM ** -0.5

    def body(x_ref, gp_ref, wq_ref, k_ref, v_ref, wo_ref, go_ref, x2_ref, h_ref, q_ref, o_ref, a_ref):
        x = x_ref[...]
        h = _rms_fwd(x, gp_ref[...]).astype(BF16)
        h_ref[...] = h
        q_ref[...] = _dot(h, wq_ref[...]).astype(BF16)
        for hd in range(XA_HEADS):
            cols = slice(hd * XA_HEAD_DIM, (hd + 1) * XA_HEAD_DIM)
            p = _softmax_rows(_dot_nt(q_ref[:, cols], k_ref[:, cols]) * scale)
            o_ref[:, cols] = _dot(p.astype(BF16), v_ref[:, cols]).astype(BF16)
        a = _dot(o_ref[...], wo_ref[...])
        a_ref[...] = a.astype(BF16)
        x2_ref[...] = x + _rms_fwd(a, go_ref[...])

    return pl.pallas_call(
        body, name=name, grid=(t // tm,),
        in_specs=[_tile(tm, d), _whole((1, d)), _resident((d, d)), _whole((m, d)), _whole((m, d)), _resident((d, d)),
                  _whole((1, d))],
        out_specs=[_tile(tm, d)] * 5,
        out_shape=[_sds((t, d), F32)] + [_sds((t, d), BF16)] * 4,
        compiler_params=_params(),
    )(x1, g_pre, wq, k, v, wo, g_post)


def _ffn_fwd(x2, g_pre, wg_t, wu_t, wd, g_post, target, tm, name):
    t, d = x2.shape
    f = wg_t.shape[0]

    def body(x_ref, gp_ref, wg_ref, wu_ref, wd_ref, go_ref, tgt_ref, h_ref, gt_ref, up_ref, f_ref, sq_ref):
        _zero_at_first_step(sq_ref)
        x = x_ref[...]
        h = _rms_fwd(x, gp_ref[...]).astype(BF16)
        h_ref[...] = h
        gt = _dot_nt(h, wg_ref[...])
        up = _dot_nt(h, wu_ref[...])
        gt_ref[...] = gt.astype(BF16)
        up_ref[...] = up.astype(BF16)
        hd = (gt * _sigmoid(gt) * up).astype(BF16)
        ff = _dot(hd, wd_ref[...])
        f_ref[...] = ff.astype(BF16)
        err = x + _rms_fwd(ff, go_ref[...]) - tgt_ref[...]
        sq_ref[...] += jnp.sum(err * err, axis=0, keepdims=True)

    return pl.pallas_call(
        body, name=name, grid=(t // tm,),
        in_specs=[_tile(tm, d), _whole((1, d)), _resident((f, d)), _resident((f, d)), _resident((f, d)), _whole((1, d)),
                  _tile(tm, d)],
        out_specs=[_tile(tm, d), _tile(tm, f), _tile(tm, f), _tile(tm, d), _whole((1, d))],
        out_shape=[_sds((t, d), BF16), _sds((t, f), BF16), _sds((t, f), BF16), _sds((t, d), BF16), _sds((1, d), F32)],
        compiler_params=_params(),
    )(x2, g_pre, wg_t, wu_t, wd, g_post, target)


def _ffn_bwd(x2, f, target, gt, up, g_pre, wg_t, wu_t, wd, g_post, tm, name):
    t, d = x2.shape
    ff = wg_t.shape[0]

    def body(x_ref, f_ref, tgt_ref, gt_ref, up_ref, gp_ref, wg_ref, wu_ref, wd_ref, go_ref,
             dx_ref, df_ref, hd_ref, dgt_ref, dup_ref, dgo_ref, dgp_ref):
        _zero_at_first_step(dgo_ref, dgp_ref)
        x = x_ref[...]
        fo = f_ref[...].astype(F32)
        dx3 = (x + _rms_fwd(fo, go_ref[...]) - tgt_ref[...]) * (1.0 / d)
        df, dgo = _rms_bwd(dx3, fo, go_ref[...])
        dgo_ref[...] += dgo
        df = df.astype(BF16)
        df_ref[...] = df
        dhd = _dot_nt(df, wd_ref[...])
        gt = gt_ref[...].astype(F32)
        up = up_ref[...].astype(F32)
        sg = _sigmoid(gt)
        si = gt * sg
        hd_ref[...] = (si * up).astype(BF16)
        dup = (dhd * si).astype(BF16)
        dgt = (dhd * up * (sg * (1.0 + gt * (1.0 - sg)))).astype(BF16)
        dup_ref[...] = dup
        dgt_ref[...] = dgt
        dh = _dot(dgt, wg_ref[...]) + _dot(dup, wu_ref[...])
        dxn, dgp = _rms_bwd(dh, x, gp_ref[...])
        dgp_ref[...] += dgp
        dx_ref[...] = dx3 + dxn

    return pl.pallas_call(
        body, name=name, grid=(t // tm,),
        in_specs=[_tile(tm, d), _tile(tm, d), _tile(tm, d), _tile(tm, ff), _tile(tm, ff), _whole((1, d)),
                  _resident((ff, d)), _resident((ff, d)), _resident((ff, d)), _whole((1, d))],
        out_specs=[_tile(tm, d), _tile(tm, d), _tile(tm, ff), _tile(tm, ff), _tile(tm, ff), _whole((1, d)), _whole((1, d))],
        out_shape=[_sds((t, d), F32), _sds((t, d), BF16), _sds((t, ff), BF16), _sds((t, ff), BF16), _sds((t, ff), BF16),
                   _sds((1, d), F32), _sds((1, d), F32)],
        compiler_params=_params(),
    )(x2, f, target, gt, up, g_pre, wg_t, wu_t, wd, g_post)


def _attn_bwd(dx2, a, x1, q, k, v, g_pre, wq, wo, g_post, s_mix, w_mix_out, g_mix_post, tm, name, comm):
    t, d = x1.shape
    m = k.shape[0]
    kdim = w_mix_out.shape[0]
    scale = XA_HEAD_DIM ** -0.5

    def core(dx2_ref, a_ref, x_ref, q_ref, k_ref, v_ref, gp_ref, wq_ref, wo_ref, go_ref, s_ref, wm_ref, gm_ref,
             dx1_ref, da_ref, dq_ref, dk_ref, dv_ref, dgo_ref, dgp_ref, ds_ref, dy_ref, dgm_ref, do_buf):
        _zero_at_first_step(dgo_ref, dgp_ref, dk_ref, dv_ref, dgm_ref)
        dx2 = dx2_ref[...]
        da, dgo = _rms_bwd(dx2, a_ref[...].astype(F32), go_ref[...])
        dgo_ref[...] += dgo
        da = da.astype(BF16)
        da_ref[...] = da
        do_buf[...] = _dot_nt(da, wo_ref[...]).astype(BF16)
        for hd in range(XA_HEADS):
            cols = slice(hd * XA_HEAD_DIM, (hd + 1) * XA_HEAD_DIM)
            qh = q_ref[:, cols]
            p = _softmax_rows(_dot_nt(qh, k_ref[:, cols]) * scale)
            do_h = do_buf[:, cols]
            dp = _dot_nt(do_h, v_ref[:, cols])
            dv_ref[:, cols] += _dot_tn(p.astype(BF16), do_h)
            ds = (p * (dp - jnp.sum(dp * p, axis=-1, keepdims=True)) * scale).astype(BF16)
            dq_ref[:, cols] = _dot(ds, k_ref[:, cols]).astype(BF16)
            dk_ref[:, cols] += _dot_tn(ds, qh)
        dh = _dot_nt(dq_ref[...], wq_ref[...])
        dxn, dgp = _rms_bwd(dh, x_ref[...], gp_ref[...])
        dgp_ref[...] += dgp
        dx1 = dx2 + dxn
        dx1_ref[...] = dx1
        ds, dgm = _rms_bwd(dx1, s_ref[...].astype(F32), gm_ref[...])
        dgm_ref[...] += dgm
        ds = ds.astype(BF16)
        ds_ref[...] = ds
        dy_ref[...] = _dot_nt(ds, wm_ref[...]).astype(BF16)

    return _hosted_call(
        core, comm, name, t // tm,
        in_specs=[_tile(tm, d), _tile(tm, d), _tile(tm, d), _tile(tm, d), _whole((m, d)), _whole((m, d)), _whole((1, d)),
                  _resident((d, d)), _resident((d, d)), _whole((1, d)), _tile(tm, d), _resident((kdim, d)), _whole((1, d))],
        out_specs=[_tile(tm, d), _tile(tm, d), _tile(tm, d), _whole((m, d)), _whole((m, d)), _whole((1, d)), _whole((1, d)),
                   _tile(tm, d), _tile(tm, kdim), _whole((1, d))],
        out_shape=[_sds((t, d), F32), _sds((t, d), BF16), _sds((t, d), BF16), _sds((m, d), F32), _sds((m, d), F32),
                   _sds((1, d), F32), _sds((1, d), F32), _sds((t, d), BF16), _sds((t, kdim), BF16), _sds((1, d), F32)],
        scratch_shapes=[pltpu.VMEM((tm, d), BF16)],
        operands=(dx2, a, x1, q, k, v, g_pre, wq, wo, g_post, s_mix, w_mix_out, g_mix_post))


def _mem_bwd(mem, mem_n, dk, dv, g, wk, wv, name):
    m, d = mem.shape

    def body(mem_ref, n_ref, dk_ref, dv_ref, g_ref, wk_ref, wv_ref, dwk_ref, dwv_ref, dg_ref):
        dk = dk_ref[...].astype(BF16)
        dv = dv_ref[...].astype(BF16)
        n = n_ref[...]
        dwk_ref[...] = _dot_tn(n, dk).astype(BF16)
        dwv_ref[...] = _dot_tn(n, dv).astype(BF16)
        dn = _dot_nt(dk, wk_ref[...]) + _dot_nt(dv, wv_ref[...])
        _, dg = _rms_bwd(dn, mem_ref[...], g_ref[...])
        dg_ref[...] = dg

    return pl.pallas_call(
        body, name=name, grid=(1,),
        in_specs=[_whole((m, d)), _whole((m, d)), _whole((m, d)), _whole((m, d)), _whole((1, d)), _whole((d, d)),
                  _whole((d, d))],
        out_specs=[_whole((d, d)), _whole((d, d)), _whole((1, d))],
        out_shape=[_sds((d, d), BF16), _sds((d, d), BF16), _sds((1, d), F32)],
        compiler_params=_params(),
    )(mem, mem_n, dk, dv, g, wk, wv)


def _conv_bwd(dy, u, zc, ca, wa, wb, lg, lb, tc, name, comm):
    t = u.shape[0]
    n_tiles = t // tc
    n_chunks = tc // CONV_CHUNK

    def core(dy_ref, u_ref, zc_ref, ca_ref, wa_ref, wb_ref, lg_ref, lb_ref,
             du_ref, dwa_ref, dwb_ref, dbb_ref, dlg_ref, dlb_ref, ebuf, eabuf, zbuf, cvbuf, gatebuf, wacc, aacc, vacc):
        step = pl.program_id(0)

        @pl.when(step == 0)
        def _():
            ebuf[:, tc:tc + HALO_B, :] = jnp.zeros((D_B // LANES, HALO_B, LANES), F32)
            eabuf[:, tc:tc + HALO_A, :] = jnp.zeros((D_A // LANES, HALO_A, LANES), F32)
            wacc[...] = jnp.zeros_like(wacc)
            aacc[...] = jnp.zeros_like(aacc)
            vacc[...] = jnp.zeros_like(vacc)

        dbb = jnp.zeros((SUBLANES, D_B), F32)
        dlg = jnp.zeros((SUBLANES, D_B), F32)
        dlb = jnp.zeros((SUBLANES, D_B), F32)
        for c in range(n_chunks):
            rows = slice(c * CONV_CHUNK, (c + 1) * CONV_CHUNK)
            dy_a = dy_ref[rows, 0:D_A].astype(F32)
            b_a = u_ref[rows, 0:D_A].astype(F32)
            du_ref[rows, 0:D_A] = (dy_a * ca_ref[rows, :].astype(F32)).astype(BF16)
            dca = dy_a * b_a
            cv = u_ref[rows, D_A:2 * D_A].astype(F32) * u_ref[rows, 2 * D_A:3 * D_A].astype(F32)
            gate = _sigmoid(u_ref[rows, 3 * D_A + D_B:3 * D_A + 2 * D_B].astype(F32))
            z = u_ref[rows, 3 * D_A:3 * D_A + D_B].astype(F32) * gate
            for lb_i in range(D_A // LANES):
                lanes = slice(lb_i * LANES, (lb_i + 1) * LANES)
                eabuf[lb_i, rows, :] = dca[:, lanes]
                cvbuf[lb_i, rows, :] = cv[:, lanes]
                zbuf[lb_i, rows, :] = z[:, lanes]
                gatebuf[lb_i, rows, :] = gate[:, lanes]

            zcv = zc_ref[rows, :].astype(F32)
            mu = jnp.mean(zcv, axis=-1, keepdims=True)
            xc = zcv - mu
            rstd = lax.rsqrt(jnp.mean(xc * xc, axis=-1, keepdims=True) + LN_EPS)
            xhat = xc * rstd
            ln = xhat * lg_ref[...] + lb_ref[...]
            sg = _sigmoid(ln)
            dln = dy_ref[rows, D_A:D_A + D_B].astype(F32) * (sg * (1.0 + ln * (1.0 - sg)))
            dlg = dlg + jnp.sum((dln * xhat).reshape(CONV_CHUNK // SUBLANES, SUBLANES, D_B), axis=0)
            dlb = dlb + jnp.sum(dln.reshape(CONV_CHUNK // SUBLANES, SUBLANES, D_B), axis=0)
            dxh = dln * lg_ref[...]
            dzc = rstd * (dxh - jnp.mean(dxh, axis=-1, keepdims=True) - xhat * jnp.mean(dxh * xhat, axis=-1, keepdims=True))
            dbb = dbb + jnp.sum(dzc.reshape(CONV_CHUNK // SUBLANES, SUBLANES, D_B), axis=0)
            for lb_i in range(D_B // LANES):
                ebuf[lb_i, rows, :] = dzc[:, lb_i * LANES:(lb_i + 1) * LANES]
        vacc[0] += dbb
        vacc[1] += dlg
        vacc[2] += dlb

        for lb_i in range(D_A // LANES):
            lanes = slice(lb_i * LANES, (lb_i + 1) * LANES)

            def cols(first):
                return slice(first + lb_i * LANES, first + (lb_i + 1) * LANES)

            for c in range(n_chunks):
                r0 = c * CONV_CHUNK
                rows = slice(r0, r0 + CONV_CHUNK)
                cv = cvbuf[lb_i, rows, :]
                dcv = jnp.zeros((CONV_CHUNK, LANES), F32)
                for k in range(CONV_A_W):
                    off = r0 + (CONV_A_W - 1) - k
                    e = eabuf[lb_i, off:off + CONV_CHUNK, :]
                    dcv = dcv + wa_ref[k:k + 1, lanes] * e
                    aacc[k, :, lanes] += jnp.sum((cv * e).reshape(CONV_CHUNK // SUBLANES, SUBLANES, LANES), axis=0)
                du_ref[rows, cols(D_A)] = (dcv * u_ref[rows, cols(2 * D_A)].astype(F32)).astype(BF16)
                du_ref[rows, cols(2 * D_A)] = (dcv * u_ref[rows, cols(D_A)].astype(F32)).astype(BF16)

                z = zbuf[lb_i, rows, :]
                dz = jnp.zeros((CONV_CHUNK, LANES), F32)
                for k in range(CONV_B_W):
                    off = r0 + (CONV_B_W - 1) - k
                    e = ebuf[lb_i, off:off + CONV_CHUNK, :]
                    dz = dz + wb_ref[k:k + 1, lanes] * e
                    wacc[k, :, lanes] += jnp.sum((z * e).reshape(CONV_CHUNK // SUBLANES, SUBLANES, LANES), axis=0)
                glu_v = u_ref[rows, cols(3 * D_A)].astype(F32)
                sgg = gatebuf[lb_i, rows, :]
                du_ref[rows, cols(3 * D_A)] = (dz * sgg).astype(BF16)
                du_ref[rows, cols(3 * D_A + D_B)] = (dz * glu_v * sgg * (1.0 - sgg)).astype(BF16)

            ebuf[lb_i, tc:tc + HALO_B, :] = ebuf[lb_i, 0:HALO_B, :]
            eabuf[lb_i, tc:tc + HALO_A, :] = eabuf[lb_i, 0:HALO_A, :]

        @pl.when(step == n_tiles - 1)
        def _():
            for k in range(CONV_B_W):
                dwb_ref[k:k + 1, :] = jnp.sum(wacc[k], axis=0, keepdims=True)
            for k in range(CONV_A_W):
                dwa_ref[k:k + 1, :] = jnp.sum(aacc[k], axis=0, keepdims=True)
            dbb_ref[...] = jnp.sum(vacc[0], axis=0, keepdims=True)
            dlg_ref[...] = jnp.sum(vacc[1], axis=0, keepdims=True)
            dlb_ref[...] = jnp.sum(vacc[2], axis=0, keepdims=True)

    return _hosted_call(
        core, comm, name, n_tiles,
        in_specs=[_rtile(tc, D_A + D_B, n_tiles), _rtile(tc, D_IN_ALL, n_tiles), _rtile(tc, D_B, n_tiles),
                  _rtile(tc, D_A, n_tiles), _whole((CONV_A_W, D_A)), _whole((CONV_B_W, D_B)), _whole((1, D_B)),
                  _whole((1, D_B))],
        out_specs=[_rtile(tc, D_IN_ALL, n_tiles), _whole((CONV_A_W, D_A)), _whole((CONV_B_W, D_B)), _whole((1, D_B)),
                   _whole((1, D_B)), _whole((1, D_B))],
        out_shape=[_sds((t, D_IN_ALL), BF16), _sds((CONV_A_W, D_A), F32), _sds((CONV_B_W, D_B), F32), _sds((1, D_B), F32),
                   _sds((1, D_B), F32), _sds((1, D_B), F32)],
        scratch_shapes=[pltpu.VMEM((D_B // LANES, tc + HALO_B, LANES), F32), pltpu.VMEM((D_A // LANES, tc + HALO_A, LANES), F32),
                        pltpu.VMEM((D_B // LANES, tc, LANES), F32), pltpu.VMEM((D_A // LANES, tc, LANES), F32),
                        pltpu.VMEM((D_B // LANES, tc, LANES), F32),
                        pltpu.VMEM((CONV_B_W, SUBLANES, D_B), F32), pltpu.VMEM((CONV_A_W, SUBLANES, D_A), F32),
                        pltpu.VMEM((3, SUBLANES, D_B), F32)],
        operands=(dy, u, zc, ca, wa, wb, lg, lb))


def _in_bwd(du, w_t, x, dx1, g, tm, name):
    t, d = x.shape
    n = w_t.shape[0]

    def body(du_ref, w_ref, x_ref, dx1_ref, g_ref, dx_ref, dg_ref):
        _zero_at_first_step(dg_ref)
        dh = _dot(du_ref[...], w_ref[...])
        dxn, dg = _rms_bwd(dh, x_ref[...], g_ref[...])
        dg_ref[...] += dg
        dx_ref[...] = dx1_ref[...] + dxn

    return pl.pallas_call(
        body, name=name, grid=(t // tm,),
        in_specs=[_tile(tm, n), _resident((n, d)), _tile(tm, d), _tile(tm, d), _whole((1, d))],
        out_specs=[_tile(tm, d), _whole((1, d))],
        out_shape=[_sds((t, d), F32), _sds((1, d), F32)],
        compiler_params=_params(),
    )(du, w_t, x, dx1, g)


def _wgrad(a, b, tk, bm, bn, name, comm=None, cols=None):
    t, m = a.shape
    first, n = (0, b.shape[1]) if cols is None else cols
    first_block = first // bn
    n_k = t // tk

    def body(a_ref, b_ref, o_ref, acc):
        @pl.when(pl.program_id(2) == 0)
        def _():
            acc[...] = jnp.zeros_like(acc)

        acc[...] += _dot_tn(a_ref[...], b_ref[...])

        @pl.when(pl.program_id(2) == n_k - 1)
        def _():
            o_ref[...] = acc[...].astype(BF16)

    call = dict(
        grid=(m // bm, n // bn, n_k),
        in_specs=[pl.BlockSpec((tk, bm), lambda i, j, k: (k, i)),
                  pl.BlockSpec((tk, bn), lambda i, j, k: (k, first_block + j))],
        out_specs=[pl.BlockSpec((bm, bn), lambda i, j, k: (i, j))],
        out_shape=[_sds((m, n), BF16)],
        scratch_shapes=[pltpu.VMEM((bm, bn), F32)])
    if comm is None:
        return pl.pallas_call(body, name=name, compiler_params=_params(3), **call)(a, b)[0]
    (out,), received = _hosted_call(body, comm, name, operands=(a, b), **call)
    return out, received


def _adamw(items, tr, name):
    n_parts, r, c = items[0][0].shape
    n_items = len(items)

    def body(*refs):
        ins, outs = refs[:4 * n_items], refs[4 * n_items:]
        for i in range(n_items):
            p_ref, w_ref, m_ref, v_ref = ins[4 * i:4 * i + 4]
            g_ref, d_ref, nm_ref, nv_ref = outs[4 * i:4 * i + 4]
            g = p_ref[0].astype(F32)
            for j in range(1, n_parts):
                g = g + p_ref[j].astype(F32)
            g_ref[...] = g
            nm = ADAM_B1 * m_ref[...] + (1.0 - ADAM_B1) * g
            nv = ADAM_B2 * v_ref[...] + (1.0 - ADAM_B2) * (g * g)
            nm_ref[...] = nm
            nv_ref[...] = nv
            m_hat = nm / (1.0 - ADAM_B1 ** ADAM_STEP)
            v_hat = nv / (1.0 - ADAM_B2 ** ADAM_STEP)
            d_ref[...] = -ADAM_LR * (m_hat / (jnp.sqrt(v_hat) + ADAM_EPS) + ADAM_WD * w_ref[...])

    results = pl.pallas_call(
        body, name=name, grid=(r // tr,),
        in_specs=([pl.BlockSpec((n_parts, tr, c), lambda i: (0, i, 0))] + [_tile(tr, c)] * 3) * n_items,
        out_specs=[_tile(tr, c)] * (4 * n_items),
        out_shape=[_sds((r, c), F32)] * (4 * n_items),
        compiler_params=_params(),
    )(*[a for item in items for a in item])
    return [results[4 * i:4 * i + 4] for i in range(n_items)]


def _sum_parts(parts, name):
    n_parts, r, c = parts.shape

    def body(p_ref, o_ref):
        acc = p_ref[0]
        for j in range(1, n_parts):
            acc = acc + p_ref[j]
        o_ref[...] = acc

    return pl.pallas_call(
        body, name=name, grid=(1,),
        in_specs=[_whole((n_parts, r, c))], out_specs=_whole((r, c)), out_shape=_sds((r, c), F32),
        compiler_params=_params(),
    )(parts)


def _row(v):
    return v.reshape(1, -1)


def _by_owner_rows(g):
    return g.reshape(N_DEV, g.shape[0] // N_DEV, g.shape[1])


WEIGHTS = ("mix_pre_g", "w_mix_in", "conv_a_w", "conv_b_w", "conv_b_b", "ln_b_g", "ln_b_b", "w_mix_out", "mix_post_g",
           "xa_pre_g", "mem_norm_g", "w_q", "w_k", "w_v", "w_o", "xa_post_g", "ffn_pre_g", "w_gate", "w_up", "w_down",
           "ffn_post_g")
LARGE = ("w_mix_in", "w_mix_out", "w_q", "w_k", "w_v", "w_o", "w_gate", "w_up", "w_down")
COLUMN_SHARDED = ("w_mix_in", "w_gate", "w_up")
GAINS = ("mix_pre_g", "mix_post_g", "xa_pre_g", "mem_norm_g", "xa_post_g", "ffn_pre_g", "ffn_post_g")
CHANNEL_VECTORS = ("conv_b_b", "ln_b_g", "ln_b_b")
CONV_TAPS = ("conv_a_w", "conv_b_w")
SMALL = GAINS + CHANNEL_VECTORS + CONV_TAPS
CONV_COLS_PER_DEVICE = D_A // N_DEV
TOKEN_TILE = 512
FFN_FWD_TOKEN_TILE = 512
FFN_TOKEN_TILE = 256
CONV_TOKEN_TILE = 256
WGRAD_TOKEN_TILE = 2048
SMALL_WGRAD_TOKEN_TILE = 4096
IN_BWD_TOKEN_TILE = 1024
ADAM_ROWS_PER_STEP = 64


def _lane_rows(v):
    flat = v.reshape(-1)
    tile = SUBLANES * LANES
    flat = jnp.pad(flat, (0, (-flat.shape[0]) % tile))
    return flat.reshape(-1, LANES)


def _pack_small(values, names):
    return jnp.concatenate([_lane_rows(values[n]) for n in names], axis=0)


def _unpack_small(packed, like, names):
    out, off = {}, 0
    for n in names:
        size = like[n].size
        rows = _lane_rows(like[n]).shape[0]
        out[n] = packed[off:off + rows, :].reshape(-1)[:size].reshape(like[n].shape)
        off += rows
    return out


def kernel(x, mem, mix_pre_g, w_mix_in, conv_a_w, conv_b_w, conv_b_b, ln_b_g, ln_b_b, w_mix_out, mix_post_g, xa_pre_g, mem_norm_g, w_q, w_k, w_v, w_o, xa_post_g, ffn_pre_g, w_gate, w_up, w_down, ffn_post_g, loss_target, m_mix_pre_g, m_w_mix_in, m_conv_a_w, m_conv_b_w, m_conv_b_b, m_ln_b_g, m_ln_b_b, m_w_mix_out, m_mix_post_g, m_xa_pre_g, m_mem_norm_g, m_w_q, m_w_k, m_w_v, m_w_o, m_xa_post_g, m_ffn_pre_g, m_w_gate, m_w_up, m_w_down, m_ffn_post_g, v_mix_pre_g, v_w_mix_in, v_conv_a_w, v_conv_b_w, v_conv_b_b, v_ln_b_g, v_ln_b_b, v_w_mix_out, v_mix_post_g, v_xa_pre_g, v_mem_norm_g, v_w_q, v_w_k, v_w_v, v_w_o, v_xa_post_g, v_ffn_pre_g, v_w_gate, v_w_up, v_w_down, v_ffn_post_g):
    given = dict(locals())
    w = {n: given[n] for n in WEIGHTS}
    m = {n: given["m_" + n] for n in WEIGHTS}
    v = {n: given["v_" + n] for n in WEIGHTS}
    xs, mems, target = x[0], mem[0], loss_target[0]
    t = xs.shape[0]
    tm, tm_ffn, tc, tk = min(TOKEN_TILE, t), min(FFN_TOKEN_TILE, t), min(CONV_TOKEN_TILE, t), min(WGRAD_TOKEN_TILE, t)
    tk_small = min(SMALL_WGRAD_TOKEN_TILE, t)
    g = {n: _row(w[n]) for n in GAINS}
    bb, lg, lb = (_row(w[n]) for n in CHANNEL_VECTORS)

    def shard_bf16(*names):
        return [w[n].astype(BF16) for n in names]

    def shard_bf16_t(*names):
        return [w[n].T.astype(BF16) for n in names]

    g_mix_in, g_taps = _comm_call(_Gather(shard_bf16_t("w_mix_in") + [_pack_small(w, CONV_TAPS)]), "gather_mixer")
    w_mix_in_t = g_mix_in.reshape(D_IN_ALL, D_MODEL)
    taps, off = {}, 0
    for n in CONV_TAPS:
        k, cols = w[n].shape
        rows = _lane_rows(w[n]).shape[0]
        blk = g_taps[:, off:off + rows, :].reshape(N_DEV, -1)[:, :k * cols].reshape(N_DEV, k, cols)
        taps[n] = blk.transpose(1, 0, 2).reshape(k, N_DEV * cols)
        off += rows
    wa, wb = taps["conv_a_w"], taps["conv_b_w"]

    (h1, u), gathered = _norm_matmul(xs, g["mix_pre_g"], w_mix_in_t.T, tm, "mix_in_fwd",
                                     _Gather(shard_bf16("w_mix_out", "w_q", "w_k", "w_v", "w_o")))
    w_mix_out, w_q, w_k, w_v, w_o = (a.reshape(D_MODEL, D_MODEL) for a in gathered)
    (ycat, zc, ca, x1, s1), (g_gate, g_up, g_down) = _conv_fwd(
        u, wa, wb, bb, lg, lb, xs, w_mix_out, g["mix_post_g"], tc, "conv_fwd",
        _Gather(shard_bf16_t("w_gate", "w_up") + shard_bf16("w_down")))
    w_gate_t, w_up_t, w_down = (a.reshape(D_FF, D_MODEL) for a in (g_gate, g_up, g_down))
    mem_n, kk, vv = _mem_fwd(mems, g["mem_norm_g"], w_k, w_v, "mem_fwd")
    x2, h2, q, o, a = _attn_fwd(x1, g["xa_pre_g"], w_q, kk, vv, w_o, g["xa_post_g"], tm, "attn_fwd")
    h3, gt, up, f, sq = _ffn_fwd(x2, g["ffn_pre_g"], w_gate_t, w_up_t, w_down, g["ffn_post_g"], target,
                                 min(FFN_FWD_TOKEN_TILE, t), "ffn_fwd")

    dx2, df, hd, dgt, dup, d_ffn_post, d_ffn_pre = _ffn_bwd(
        x2, f, target, gt, up, g["ffn_pre_g"], w_gate_t, w_up_t, w_down, g["ffn_post_g"], tm_ffn, "ffn_bwd")
    d_w_down = _wgrad(hd, df, tk, D_FF // 2, D_MODEL, "wgrad_down")
    d_w_gate_t = _wgrad(dgt, h3, tk, D_FF // 2, D_MODEL, "wgrad_gate")
    d_w_up_t = _wgrad(dup, h3, tk, D_FF // 2, D_MODEL, "wgrad_up")
    ffn_slabs = [_by_owner_rows(d) for d in (d_w_gate_t, d_w_up_t, d_w_down)]

    (dx1, da, dq, dk, dv, d_xa_post, d_xa_pre, ds1, dycat, d_mix_post), from_ffn = _attn_bwd(
        dx2, a, x1, q, kk, vv, g["xa_pre_g"], w_q, w_o, g["xa_post_g"], s1, w_mix_out, g["mix_post_g"], tm, "attn_bwd",
        _Exchange(ffn_slabs, [True] * 3))
    d_w_o = _wgrad(o, da, tk_small, D_MODEL, D_MODEL, "wgrad_o")
    d_w_q = _wgrad(h2, dq, tk_small, D_MODEL, D_MODEL, "wgrad_q")
    d_w_k, d_w_v, d_mem_norm = _mem_bwd(mems, mem_n, dk, dv, g["mem_norm_g"], w_k, w_v, "mem_bwd")
    d_w_mix_out = _wgrad(ycat, ds1, tk_small, D_MODEL, D_MODEL, "wgrad_mix_out")
    attn_slabs = [_by_owner_rows(d) for d in (d_w_mix_out, d_w_q, d_w_k, d_w_v, d_w_o)]

    (du, d_conv_a, d_conv_b, d_conv_bb, d_ln_g, d_ln_b), from_attn = _conv_bwd(
        dycat, u, zc, ca, wa, wb, lg, lb, tc, "conv_bwd", _Exchange(attn_slabs, [True] * 5))
    small_grads = dict(conv_a_w=d_conv_a, conv_b_w=d_conv_b, conv_b_b=d_conv_bb, ln_b_g=d_ln_g, ln_b_b=d_ln_b,
                       mix_post_g=d_mix_post, xa_pre_g=d_xa_pre, mem_norm_g=d_mem_norm, xa_post_g=d_xa_post,
                       ffn_pre_g=d_ffn_pre, ffn_post_g=d_ffn_post, loss=sq)
    names = tuple(n for n in SMALL if n != "mix_pre_g") + ("loss", "mix_pre_g")
    half = D_MODEL // 2
    d_in_lo = _wgrad(du, h1, tk_small, D_IN_ALL // 2, half, "wgrad_mix_in_lo", cols=(0, half))
    d_in_hi, (from_in_lo, early_small) = _wgrad(
        du, h1, tk_small, D_IN_ALL // 2, half, "wgrad_mix_in_hi",
        _Exchange([_by_owner_rows(d_in_lo), _pack_small(small_grads, names[:-1])], [True, False]), cols=(half, half))
    dx, d_mix_pre = _in_bwd(du, w_mix_in_t, xs, dx1, g["mix_pre_g"], min(IN_BWD_TOKEN_TILE, t), "mix_in_bwd")
    small_grads["mix_pre_g"] = d_mix_pre
    from_in_hi, late_small = _comm_call(
        _Exchange([_by_owner_rows(d_in_hi), _lane_rows(d_mix_pre)], [True, False]), "reduce_tail")
    all_small = jnp.concatenate([early_small, late_small], axis=1)

    received = dict(zip(("w_gate", "w_up", "w_down"), from_ffn))
    received.update(zip(("w_mix_out", "w_q", "w_k", "w_v", "w_o"), from_attn))
    received["w_mix_in"] = jnp.concatenate([from_in_lo, from_in_hi], axis=2)
    grad, delta, new_m, new_v = {}, {}, {}, {}
    for group in (("w_mix_in",), ("w_gate",), ("w_up",), ("w_down",), ("w_mix_out", "w_q", "w_k", "w_v", "w_o")):
        items = [[received[n]] + [a[n].T if n in COLUMN_SHARDED else a[n] for a in (w, m, v)] for n in group]
        rows = items[0][1].shape[0]
        results = _adamw(items, min(rows, ADAM_ROWS_PER_STEP) if len(group) > 1 else rows, "adamw_" + group[0])
        for n, result in zip(group, results):
            grad[n], delta[n], new_m[n], new_v[n] = [r.T if n in COLUMN_SHARDED else r for r in result]

    total = _unpack_small(_sum_parts(all_small, "sum_small"), small_grads, names)
    loss = jnp.sum(total.pop("loss")) * (0.5 / D_MODEL)
    first_col = _device_index() * CONV_COLS_PER_DEVICE
    for n in CONV_TAPS:
        total[n] = lax.dynamic_slice_in_dim(total[n], first_col, CONV_COLS_PER_DEVICE, axis=1)
    total = {n: total[n].reshape(w[n].shape) for n in SMALL}
    packed_small = [_pack_small(values, SMALL) for values in (total, w, m, v)]
    ((g_s, d_s, nm_s, nv_s),) = _adamw([[packed_small[0][None]] + packed_small[1:]], packed_small[0].shape[0],
                                       "adamw_small")
    for out, p in ((grad, g_s), (delta, d_s), (new_m, nm_s), (new_v, nv_s)):
        out.update(_unpack_small(p, w, SMALL))

    return (loss, dx[None], *[grad[n] for n in WEIGHTS], *[delta[n] for n in WEIGHTS], *[new_m[n] for n in WEIGHTS],
            *[new_v[n] for n in WEIGHTS])
```

```python
import jax
import jax.numpy as jnp
from jax import lax
from jax.experimental import pallas as pl
from jax.experimental.pallas import tpu as pltpu

F32 = jnp.float32
BF16 = jnp.bfloat16

D_MODEL = 1024
D_A = 512
D_B = 512
D_IN_ALL = 2560
CONV_A_W = 3
CONV_B_W = 31
XA_HEADS = 4
XA_HEAD_DIM = 256
D_FF = 2816
N_DEV = 8
RMS_EPS = 1e-6
LN_EPS = 1e-5
ADAM_LR = 0.001
ADAM_B1 = 0.9
ADAM_B2 = 0.999
ADAM_EPS = 1e-08
ADAM_WD = 0.01
ADAM_STEP = 10

VMEM_LIMIT_BYTES = 56 * 1024 * 1024
SUBLANES = 8
LANES = 128
HALO_B = 32
HALO_A = 8
CONV_FWD_CHUNK = 16
CONV_CHUNK = 32
GATHER_FORWARD_STEPS_BEFORE_END = 8

MESH = pl.DeviceIdType.MESH


def _params(n_grid_axes=1):
    return pltpu.CompilerParams(dimension_semantics=("arbitrary",) * n_grid_axes, vmem_limit_bytes=VMEM_LIMIT_BYTES)


def _sds(shape, dtype):
    return jax.ShapeDtypeStruct(shape, dtype)


def _tile(rows, cols):
    return pl.BlockSpec((rows, cols), lambda i: (i, 0))


def _rtile(rows, cols, n):
    return pl.BlockSpec((rows, cols), lambda i: (n - 1 - i, 0))


def _whole(shape):
    zeros = (0,) * len(shape)
    return pl.BlockSpec(shape, lambda i: zeros)


def _resident(shape):
    zeros = (0,) * len(shape)
    return pl.BlockSpec(shape, lambda i: zeros, pipeline_mode=pl.Buffered(1))


def _dot(a, b):
    return jnp.dot(a, b, preferred_element_type=F32)


def _dot_nt(a, b):
    return lax.dot_general(a, b, (((1,), (1,)), ((), ())), preferred_element_type=F32)


def _dot_tn(a, b):
    return lax.dot_general(a, b, (((0,), (0,)), ((), ())), preferred_element_type=F32)


def _sigmoid(x):
    return 1.0 / (1.0 + jnp.exp(-x))


def _rms_fwd(x, g):
    r = lax.rsqrt(jnp.mean(x * x, axis=-1, keepdims=True) + RMS_EPS)
    return x * r * g


def _rms_bwd(dy, xin, g):
    r = lax.rsqrt(jnp.mean(xin * xin, axis=-1, keepdims=True) + RMS_EPS)
    n = xin * r
    dg = jnp.sum(dy * n, axis=0, keepdims=True)
    dn = dy * g
    dx = r * (dn - n * jnp.mean(dn * n, axis=-1, keepdims=True))
    return dx, dg


def _zero_at_first_step(*refs):
    @pl.when(pl.program_id(0) == 0)
    def _():
        for ref in refs:
            ref[...] = jnp.zeros(ref.shape, ref.dtype)


def _place():
    return lax.axis_index("x"), lax.axis_index("y"), lax.axis_index("c")


def _device_index():
    x, y, c = _place()
    return 4 * x + 2 * y + c


class _Gather:
    def __init__(self, arrays):
        self.arrays = list(arrays)
        self.out_shape = [_sds((N_DEV, *a.shape), a.dtype) for a in self.arrays]
        n = len(self.arrays)
        self.scratch_shapes = [pltpu.SemaphoreType.DMA((n, 7)), pltpu.SemaphoreType.DMA((n, 7)),
                               pltpu.SemaphoreType.DMA((n,))]

    def forward_step(self, n_steps):
        return max(0, n_steps - 1 - GATHER_FORWARD_STEPS_BEFORE_END)

    def bind(self, srcs, dsts, send_sems, recv_sems, local_sems):
        x, y, cc = _place()
        me, sibling = (x, y, cc), (x, y, 1 - cc)
        chips = [(1 - x, y), (x, 1 - y), (1 - x, 1 - y)]

        def copy(a, k, owner, to, src=None):
            slot = dsts[a].at[4 * owner[0] + 2 * owner[1] + owner[2]]
            return pltpu.make_async_remote_copy(
                src_ref=slot if src is None else src, dst_ref=slot, send_sem=send_sems.at[a, k],
                recv_sem=recv_sems.at[a, k], device_id=to, device_id_type=MESH)

        def first(a):
            return [copy(a, 0, me, sibling, src=srcs[a])] + [
                copy(a, 1 + j, me, (*chip, cc), src=srcs[a]) for j, chip in enumerate(chips)]

        def passed(a, j):
            return copy(a, 4 + j, (*chips[j], cc), sibling)

        def mine(a):
            return pltpu.make_async_copy(srcs[a], dsts[a].at[4 * x + 2 * y + cc], local_sems.at[a])

        def start():
            for a in range(len(srcs)):
                mine(a).start()
                for cp in first(a):
                    cp.start()

        def forward():
            for a in range(len(srcs)):
                for j, chip in enumerate(chips):
                    copy(a, 1 + j, (*chip, cc), me).wait_recv()
                    passed(a, j).start()

        def finish():
            for a in range(len(srcs)):
                copy(a, 0, sibling, me).wait_recv()
                for j, chip in enumerate(chips):
                    copy(a, 4 + j, (*chip, 1 - cc), me).wait_recv()
                for cp in first(a) + [passed(a, j) for j in range(len(chips))]:
                    cp.wait_send()
                mine(a).wait()

        return start, forward, finish


class _Exchange:
    def __init__(self, arrays, scatter):
        self.arrays = list(arrays)
        self.scatter = list(scatter)
        self.out_shape = [_sds(a.shape if s else (N_DEV, *a.shape), a.dtype) for a, s in zip(self.arrays, self.scatter)]
        n = len(self.arrays)
        self.scratch_shapes = [pltpu.SemaphoreType.DMA((n, 7)), pltpu.SemaphoreType.DMA((n, 7)),
                               pltpu.SemaphoreType.DMA((n,))]

    def forward_step(self, n_steps):
        return n_steps - 1

    def bind(self, srcs, dsts, send_sems, recv_sems, local_sems):
        me = _device_index()

        def copies(a):
            out = []
            for k in range(1, N_DEV):
                p = me ^ k
                out.append(pltpu.make_async_remote_copy(
                    src_ref=srcs[a].at[p] if self.scatter[a] else srcs[a], dst_ref=dsts[a].at[me],
                    send_sem=send_sems.at[a, k - 1], recv_sem=recv_sems.at[a, k - 1],
                    device_id=(p >> 2, (p >> 1) & 1, p & 1), device_id_type=MESH))
            return out

        def mine(a):
            return pltpu.make_async_copy(srcs[a].at[me] if self.scatter[a] else srcs[a], dsts[a].at[me], local_sems.at[a])

        def start():
            for a in range(len(srcs)):
                mine(a).start()
                for cp in copies(a):
                    cp.start()

        def forward():
            pass

        def finish():
            for a in range(len(srcs)):
                for cp in copies(a):
                    cp.wait()
                mine(a).wait()

        return start, forward, finish


def _hosted_call(core, comm, name, grid, in_specs, out_specs, out_shape, scratch_shapes, operands):
    grid = (grid,) if isinstance(grid, int) else tuple(grid)
    n_steps = 1
    for extent in grid:
        n_steps *= extent
    n_in, n_out, n_scr, n_arr = len(in_specs), len(out_specs), len(scratch_shapes), len(comm.arrays)
    any_spec = pl.BlockSpec(memory_space=pl.ANY)

    def body(*refs):
        ins, refs = refs[:n_in], refs[n_in:]
        srcs, refs = refs[:n_arr], refs[n_arr:]
        outs, refs = refs[:n_out], refs[n_out:]
        dsts, refs = refs[:n_arr], refs[n_arr:]
        scratch, sems = refs[:n_scr], refs[n_scr:]
        start, forward, finish = comm.bind(srcs, dsts, *sems)
        step = pl.program_id(0)
        for axis in range(1, len(grid)):
            step = step * grid[axis] + pl.program_id(axis)
        pl.when(step == 0)(start)
        core(*ins, *outs, *scratch)
        pl.when(step == comm.forward_step(n_steps))(forward)
        pl.when(step == n_steps - 1)(finish)

    results = pl.pallas_call(
        body, name=name, grid=grid,
        in_specs=list(in_specs) + [any_spec] * n_arr,
        out_specs=list(out_specs) + [any_spec] * n_arr,
        out_shape=list(out_shape) + comm.out_shape,
        scratch_shapes=list(scratch_shapes) + comm.scratch_shapes,
        compiler_params=_params(len(grid)),
    )(*operands, *comm.arrays)
    return results[:n_out], results[n_out:]


def _comm_call(comm, name):
    return _hosted_call(lambda: None, comm, name, 1, [], [], [], [], [])[1]


def _norm_matmul(x, g, w, tm, name, comm):
    t, d = x.shape
    n = w.shape[1]

    def core(x_ref, g_ref, w_ref, h_ref, o_ref):
        h = _rms_fwd(x_ref[...], g_ref[...]).astype(BF16)
        h_ref[...] = h
        o_ref[...] = _dot(h, w_ref[...]).astype(BF16)

    return _hosted_call(
        core, comm, name, t // tm,
        in_specs=[_tile(tm, d), _whole((1, d)), _resident((d, n))],
        out_specs=[_tile(tm, d), _tile(tm, n)],
        out_shape=[_sds((t, d), BF16), _sds((t, n), BF16)],
        scratch_shapes=[], operands=(x, g, w))


def _conv_fwd(u, wa, wb, bb, lg, lb, xres, w_out, g_post, tc, name, comm):
    t = u.shape[0]
    d = w_out.shape[1]
    chunk = CONV_FWD_CHUNK
    n_chunks = tc // chunk
    piece = 2 * LANES

    def core(u_ref, wa_ref, wb_ref, bb_ref, lg_ref, lb_ref, x_ref, wo_ref, go_ref,
             y_ref, zc_ref, ca_ref, x1_ref, s_ref, zbuf, cvbuf, zcbuf):
        @pl.when(pl.program_id(0) == 0)
        def _():
            zbuf[:, 0:HALO_B, :] = jnp.zeros((D_B // LANES, HALO_B, LANES), F32)
            cvbuf[:, 0:HALO_A, :] = jnp.zeros((D_A // LANES, HALO_A, LANES), F32)

        s = None
        for lb_i in range(D_A // LANES):
            if lb_i > 0 and lb_i % 2 == 0:
                cols = slice((lb_i - 2) * LANES, lb_i * LANES)
                part = _dot(y_ref[:, cols], wo_ref[cols, :])
                s = part if s is None else s + part
            lanes = slice(lb_i * LANES, (lb_i + 1) * LANES)
            c_a = u_ref[:, D_A + lb_i * LANES:D_A + (lb_i + 1) * LANES].astype(F32)
            v_a = u_ref[:, 2 * D_A + lb_i * LANES:2 * D_A + (lb_i + 1) * LANES].astype(F32)
            cvbuf[lb_i, HALO_A:HALO_A + tc, :] = c_a * v_a
            glu_v = u_ref[:, 3 * D_A + lb_i * LANES:3 * D_A + (lb_i + 1) * LANES].astype(F32)
            glu_g = u_ref[:, 3 * D_A + D_B + lb_i * LANES:3 * D_A + D_B + (lb_i + 1) * LANES].astype(F32)
            zbuf[lb_i, HALO_B:HALO_B + tc, :] = glu_v * _sigmoid(glu_g)

            for c in range(n_chunks):
                r0 = c * chunk
                rows = slice(r0, r0 + chunk)
                acc = jnp.zeros((chunk, LANES), F32)
                for k in range(CONV_A_W):
                    off = r0 + HALO_A - (CONV_A_W - 1) + k
                    acc = acc + wa_ref[k:k + 1, lanes] * cvbuf[lb_i, off:off + chunk, :]
                ca_ref[rows, lanes] = acc.astype(BF16)
                y_ref[rows, lanes] = (u_ref[rows, lanes].astype(F32) * acc).astype(BF16)

                accb = jnp.zeros((chunk, LANES), F32)
                for k in range(CONV_B_W):
                    off = r0 + HALO_B - (CONV_B_W - 1) + k
                    accb = accb + wb_ref[k:k + 1, lanes] * zbuf[lb_i, off:off + chunk, :]
                zcbuf[rows, lanes] = accb + bb_ref[:, lanes]

            zbuf[lb_i, 0:HALO_B, :] = zbuf[lb_i, tc:tc + HALO_B, :]
            cvbuf[lb_i, 0:HALO_A, :] = cvbuf[lb_i, tc:tc + HALO_A, :]

        cols = slice(D_A - piece, D_A)
        s = s + _dot(y_ref[:, cols], wo_ref[cols, :])

        for c in range(n_chunks):
            rows = slice(c * chunk, (c + 1) * chunk)
            zc = zcbuf[rows, :]
            zc_ref[rows, :] = zc.astype(BF16)
            mu = jnp.mean(zc, axis=-1, keepdims=True)
            xc = zc - mu
            var = jnp.mean(xc * xc, axis=-1, keepdims=True)
            ln = xc * lax.rsqrt(var + LN_EPS) * lg_ref[...] + lb_ref[...]
            y_ref[rows, D_A:D_A + D_B] = (ln * _sigmoid(ln)).astype(BF16)

        s = s + _dot(y_ref[:, D_A:D_A + D_B], wo_ref[D_A:D_A + D_B, :])
        s_ref[...] = s.astype(BF16)
        x1_ref[...] = x_ref[...] + _rms_fwd(s, go_ref[...])

    return _hosted_call(
        core, comm, name, t // tc,
        in_specs=[_tile(tc, D_IN_ALL), _whole((CONV_A_W, D_A)), _whole((CONV_B_W, D_B)), _whole((1, D_B)),
                  _whole((1, D_B)), _whole((1, D_B)), _tile(tc, d), _resident((D_A + D_B, d)), _whole((1, d))],
        out_specs=[_tile(tc, D_A + D_B), _tile(tc, D_B), _tile(tc, D_A), _tile(tc, d), _tile(tc, d)],
        out_shape=[_sds((t, D_A + D_B), BF16), _sds((t, D_B), BF16), _sds((t, D_A), BF16), _sds((t, d), F32),
                   _sds((t, d), BF16)],
        scratch_shapes=[pltpu.VMEM((D_B // LANES, HALO_B + tc, LANES), F32),
                        pltpu.VMEM((D_A // LANES, HALO_A + tc, LANES), F32), pltpu.VMEM((tc, D_B), F32)],
        operands=(u, wa, wb, bb, lg, lb, xres, w_out, g_post))


def _mem_fwd(mem, g, wk, wv, name):
    m, d = mem.shape

    def body(mem_ref, g_ref, wk_ref, wv_ref, n_ref, k_ref, v_ref):
        n = _rms_fwd(mem_ref[...], g_ref[...]).astype(BF16)
        n_ref[...] = n
        k_ref[...] = _dot(n, wk_ref[...]).astype(BF16)
        v_ref[...] = _dot(n, wv_ref[...]).astype(BF16)

    return pl.pallas_call(
        body, name=name, grid=(1,),
        in_specs=[_whole((m, d)), _whole((1, d)), _whole((d, d)), _whole((d, d))],
        out_specs=[_whole((m, d))] * 3,
        out_shape=[_sds((m, d), BF16)] * 3,
        compiler_params=_params(),
    )(mem, g, wk, wv)


def _softmax_rows(s):
    e = jnp.exp(s - jnp.max(s, axis=-1, keepdims=True))
    return e / jnp.sum(e, axis=-1, keepdims=True)


def _attn_fwd(x1, g_pre, wq, k, v, wo, g_post, tm, name):
    t, d = x1.shape
    m = k.shape[0]
    scale = XA_HEAD_DIM ** -0.5

    def body(x_ref, gp_ref, wq_ref, k_ref, v_ref, wo_ref, go_ref, x2_ref, h_ref, q_ref, o_ref, a_ref):
        x = x_ref[...]
        h = _rms_fwd(x, gp_ref[...]).astype(BF16)
        h_ref[...] = h
        q_ref[...] = _dot(h, wq_ref[...]).astype(BF16)
        for hd in range(XA_HEADS):
            cols = slice(hd * XA_HEAD_DIM, (hd + 1) * XA_HEAD_DIM)
            p = _softmax_rows(_dot_nt(q_ref[:, cols], k_ref[:, cols]) * scale)
            o_ref[:, cols] = _dot(p.astype(BF16), v_ref[:, cols]).astype(BF16)
        a = _dot(o_ref[...], wo_ref[...])
        a_ref[...] = a.astype(BF16)
        x2_ref[...] = x + _rms_fwd(a, go_ref[...])

    return pl.pallas_call(
        body, name=name, grid=(t // tm,),
        in_specs=[_tile(tm, d), _whole((1, d)), _resident((d, d)), _whole((m, d)), _whole((m, d)), _resident((d, d)),
                  _whole((1, d))],
        out_specs=[_tile(tm, d)] * 5,
        out_shape=[_sds((t, d), F32)] + [_sds((t, d), BF16)] * 4,
        compiler_params=_params(),
    )(x1, g_pre, wq, k, v, wo, g_post)


def _ffn_fwd(x2, g_pre, wg_t, wu_t, wd, g_post, target, tm, name):
    t, d = x2.shape
    f = wg_t.shape[0]

    def body(x_ref, gp_ref, wg_ref, wu_ref, wd_ref, go_ref, tgt_ref, h_ref, gt_ref, up_ref, f_ref, sq_ref):
        _zero_at_first_step(sq_ref)
        x = x_ref[...]
        h = _rms_fwd(x, gp_ref[...]).astype(BF16)
        h_ref[...] = h
        gt = _dot_nt(h, wg_ref[...])
        up = _dot_nt(h, wu_ref[...])
        gt_ref[...] = gt.astype(BF16)
        up_ref[...] = up.astype(BF16)
        hd = (gt * _sigmoid(gt) * up).astype(BF16)
        ff = _dot(hd, wd_ref[...])
        f_ref[...] = ff.astype(BF16)
        err = x + _rms_fwd(ff, go_ref[...]) - tgt_ref[...]
        sq_ref[...] += jnp.sum(err * err, axis=0, keepdims=True)

    return pl.pallas_call(
        body, name=name, grid=(t // tm,),
        in_specs=[_tile(tm, d), _whole((1, d)), _resident((f, d)), _resident((f, d)), _resident((f, d)), _whole((1, d)),
                  _tile(tm, d)],
        out_specs=[_tile(tm, d), _tile(tm, f), _tile(tm, f), _tile(tm, d), _whole((1, d))],
        out_shape=[_sds((t, d), BF16), _sds((t, f), BF16), _sds((t, f), BF16), _sds((t, d), BF16), _sds((1, d), F32)],
        compiler_params=_params(),
    )(x2, g_pre, wg_t, wu_t, wd, g_post, target)


def _ffn_bwd(x2, f, target, gt, up, g_pre, wg_t, wu_t, wd, g_post, tm, name):
    t, d = x2.shape
    ff = wg_t.shape[0]

    def body(x_ref, f_ref, tgt_ref, gt_ref, up_ref, gp_ref, wg_ref, wu_ref, wd_ref, go_ref,
             dx_ref, df_ref, hd_ref, dgt_ref, dup_ref, dgo_ref, dgp_ref):
        _zero_at_first_step(dgo_ref, dgp_ref)
        x = x_ref[...]
        fo = f_ref[...].astype(F32)
        dx3 = (x + _rms_fwd(fo, go_ref[...]) - tgt_ref[...]) * (1.0 / d)
        df, dgo = _rms_bwd(dx3, fo, go_ref[...])
        dgo_ref[...] += dgo
        df = df.astype(BF16)
        df_ref[...] = df
        dhd = _dot_nt(df, wd_ref[...])
        gt = gt_ref[...].astype(F32)
        up = up_ref[...].astype(F32)
        sg = _sigmoid(gt)
        si = gt * sg
        hd_ref[...] = (si * up).astype(BF16)
        dup = (dhd * si).astype(BF16)
        dgt = (dhd * up * (sg * (1.0 + gt * (1.0 - sg)))).astype(BF16)
        dup_ref[...] = dup
        dgt_ref[...] = dgt
        dh = _dot(dgt, wg_ref[...]) + _dot(dup, wu_ref[...])
        dxn, dgp = _rms_bwd(dh, x, gp_ref[...])
        dgp_ref[...] += dgp
        dx_ref[...] = dx3 + dxn

    return pl.pallas_call(
        body, name=name, grid=(t // tm,),
        in_specs=[_tile(tm, d), _tile(tm, d), _tile(tm, d), _tile(tm, ff), _tile(tm, ff), _whole((1, d)),
                  _resident((ff, d)), _resident((ff, d)), _resident((ff, d)), _whole((1, d))],
        out_specs=[_tile(tm, d), _tile(tm, d), _tile(tm, ff), _tile(tm, ff), _tile(tm, ff), _whole((1, d)), _whole((1, d))],
        out_shape=[_sds((t, d), F32), _sds((t, d), BF16), _sds((t, ff), BF16), _sds((t, ff), BF16), _sds((t, ff), BF16),
                   _sds((1, d), F32), _sds((1, d), F32)],
        compiler_params=_params(),
    )(x2, f, target, gt, up, g_pre, wg_t, wu_t, wd, g_post)


def _attn_bwd(dx2, a, x1, q, k, v, g_pre, wq, wo, g_post, s_mix, w_mix_out, g_mix_post, tm, name, comm):
    t, d = x1.shape
    m = k.shape[0]
    kdim = w_mix_out.shape[0]
    scale = XA_HEAD_DIM ** -0.5

    def core(dx2_ref, a_ref, x_ref, q_ref, k_ref, v_ref, gp_ref, wq_ref, wo_ref, go_ref, s_ref, wm_ref, gm_ref,
             dx1_ref, da_ref, dq_ref, dk_ref, dv_ref, dgo_ref, dgp_ref, ds_ref, dy_ref, dgm_ref, do_buf):
        _zero_at_first_step(dgo_ref, dgp_ref, dk_ref, dv_ref, dgm_ref)
        dx2 = dx2_ref[...]
        da, dgo = _rms_bwd(dx2, a_ref[...].astype(F32), go_ref[...])
        dgo_ref[...] += dgo
        da = da.astype(BF16)
        da_ref[...] = da
        do_buf[...] = _dot_nt(da, wo_ref[...]).astype(BF16)
        for hd in range(XA_HEADS):
            cols = slice(hd * XA_HEAD_DIM, (hd + 1) * XA_HEAD_DIM)
            qh = q_ref[:, cols]
            p = _softmax_rows(_dot_nt(qh, k_ref[:, cols]) * scale)
            do_h = do_buf[:, cols]
            dp = _dot_nt(do_h, v_ref[:, cols])
            dv_ref[:, cols] += _dot_tn(p.astype(BF16), do_h)
            ds = (p * (dp - jnp.sum(dp * p, axis=-1, keepdims=True)) * scale).astype(BF16)
            dq_ref[:, cols] = _dot(ds, k_ref[:, cols]).astype(BF16)
            dk_ref[:, cols] += _dot_tn(ds, qh)
        dh = _dot_nt(dq_ref[...], wq_ref[...])
        dxn, dgp = _rms_bwd(dh, x_ref[...], gp_ref[...])
        dgp_ref[...] += dgp
        dx1 = dx2 + dxn
        dx1_ref[...] = dx1
        ds, dgm = _rms_bwd(dx1, s_ref[...].astype(F32), gm_ref[...])
        dgm_ref[...] += dgm
        ds = ds.astype(BF16)
        ds_ref[...] = ds
        dy_ref[...] = _dot_nt(ds, wm_ref[...]).astype(BF16)

    return _hosted_call(
        core, comm, name, t // tm,
        in_specs=[_tile(tm, d), _tile(tm, d), _tile(tm, d), _tile(tm, d), _whole((m, d)), _whole((m, d)), _whole((1, d)),
                  _resident((d, d)), _resident((d, d)), _whole((1, d)), _tile(tm, d), _resident((kdim, d)), _whole((1, d))],
        out_specs=[_tile(tm, d), _tile(tm, d), _tile(tm, d), _whole((m, d)), _whole((m, d)), _whole((1, d)), _whole((1, d)),
                   _tile(tm, d), _tile(tm, kdim), _whole((1, d))],
        out_shape=[_sds((t, d), F32), _sds((t, d), BF16), _sds((t, d), BF16), _sds((m, d), F32), _sds((m, d), F32),
                   _sds((1, d), F32), _sds((1, d), F32), _sds((t, d), BF16), _sds((t, kdim), BF16), _sds((1, d), F32)],
        scratch_shapes=[pltpu.VMEM((tm, d), BF16)],
        operands=(dx2, a, x1, q, k, v, g_pre, wq, wo, g_post, s_mix, w_mix_out, g_mix_post))


def _mem_bwd(mem, mem_n, dk, dv, g, wk, wv, name):
    m, d = mem.shape

    def body(mem_ref, n_ref, dk_ref, dv_ref, g_ref, wk_ref, wv_ref, dwk_ref, dwv_ref, dg_ref):
        dk = dk_ref[...].astype(BF16)
        dv = dv_ref[...].astype(BF16)
        n = n_ref[...]
        dwk_ref[...] = _dot_tn(n, dk).astype(BF16)
        dwv_ref[...] = _dot_tn(n, dv).astype(BF16)
        dn = _dot_nt(dk, wk_ref[...]) + _dot_nt(dv, wv_ref[...])
        _, dg = _rms_bwd(dn, mem_ref[...], g_ref[...])
        dg_ref[...] = dg

    return pl.pallas_call(
        body, name=name, grid=(1,),
        in_specs=[_whole((m, d)), _whole((m, d)), _whole((m, d)), _whole((m, d)), _whole((1, d)), _whole((d, d)),
                  _whole((d, d))],
        out_specs=[_whole((d, d)), _whole((d, d)), _whole((1, d))],
        out_shape=[_sds((d, d), BF16), _sds((d, d), BF16), _sds((1, d), F32)],
        compiler_params=_params(),
    )(mem, mem_n, dk, dv, g, wk, wv)


def _conv_bwd(dy, u, zc, ca, wa, wb, lg, lb, tc, name, comm):
    t = u.shape[0]
    n_tiles = t // tc
    n_chunks = tc // CONV_CHUNK

    def core(dy_ref, u_ref, zc_ref, ca_ref, wa_ref, wb_ref, lg_ref, lb_ref,
             du_ref, dwa_ref, dwb_ref, dbb_ref, dlg_ref, dlb_ref, ebuf, eabuf, zbuf, cvbuf, gatebuf, wacc, aacc, vacc):
        step = pl.program_id(0)

        @pl.when(step == 0)
        def _():
            ebuf[:, tc:tc + HALO_B, :] = jnp.zeros((D_B // LANES, HALO_B, LANES), F32)
            eabuf[:, tc:tc + HALO_A, :] = jnp.zeros((D_A // LANES, HALO_A, LANES), F32)
            wacc[...] = jnp.zeros_like(wacc)
            aacc[...] = jnp.zeros_like(aacc)
            vacc[...] = jnp.zeros_like(vacc)

        dbb = jnp.zeros((SUBLANES, D_B), F32)
        dlg = jnp.zeros((SUBLANES, D_B), F32)
        dlb = jnp.zeros((SUBLANES, D_B), F32)
        for c in range(n_chunks):
            rows = slice(c * CONV_CHUNK, (c + 1) * CONV_CHUNK)
            dy_a = dy_ref[rows, 0:D_A].astype(F32)
            b_a = u_ref[rows, 0:D_A].astype(F32)
            du_ref[rows, 0:D_A] = (dy_a * ca_ref[rows, :].astype(F32)).astype(BF16)
            dca = dy_a * b_a
            cv = u_ref[rows, D_A:2 * D_A].astype(F32) * u_ref[rows, 2 * D_A:3 * D_A].astype(F32)
            gate = _sigmoid(u_ref[rows, 3 * D_A + D_B:3 * D_A + 2 * D_B].astype(F32))
            z = u_ref[rows, 3 * D_A:3 * D_A + D_B].astype(F32) * gate
            for lb_i in range(D_A // LANES):
                lanes = slice(lb_i * LANES, (lb_i + 1) * LANES)
                eabuf[lb_i, rows, :] = dca[:, lanes]
                cvbuf[lb_i, rows, :] = cv[:, lanes]
                zbuf[lb_i, rows, :] = z[:, lanes]
                gatebuf[lb_i, rows, :] = gate[:, lanes]

            zcv = zc_ref[rows, :].astype(F32)
            mu = jnp.mean(zcv, axis=-1, keepdims=True)
            xc = zcv - mu
            rstd = lax.rsqrt(jnp.mean(xc * xc, axis=-1, keepdims=True) + LN_EPS)
            xhat = xc * rstd
            ln = xhat * lg_ref[...] + lb_ref[...]
            sg = _sigmoid(ln)
            dln = dy_ref[rows, D_A:D_A + D_B].astype(F32) * (sg * (1.0 + ln * (1.0 - sg)))
            dlg = dlg + jnp.sum((dln * xhat).reshape(CONV_CHUNK // SUBLANES, SUBLANES, D_B), axis=0)
            dlb = dlb + jnp.sum(dln.reshape(CONV_CHUNK // SUBLANES, SUBLANES, D_B), axis=0)
            dxh = dln * lg_ref[...]
            dzc = rstd * (dxh - jnp.mean(dxh, axis=-1, keepdims=True) - xhat * jnp.mean(dxh * xhat, axis=-1, keepdims=True))
            dbb = dbb + jnp.sum(dzc.reshape(CONV_CHUNK // SUBLANES, SUBLANES, D_B), axis=0)
            for lb_i in range(D_B // LANES):
                ebuf[lb_i, rows, :] = dzc[:, lb_i * LANES:(lb_i + 1) * LANES]
        vacc[0] += dbb
        vacc[1] += dlg
        vacc[2] += dlb

        for lb_i in range(D_A // LANES):
            lanes = slice(lb_i * LANES, (lb_i + 1) * LANES)

            def cols(first):
                return slice(first + lb_i * LANES, first + (lb_i + 1) * LANES)

            for c in range(n_chunks):
                r0 = c * CONV_CHUNK
                rows = slice(r0, r0 + CONV_CHUNK)
                cv = cvbuf[lb_i, rows, :]
                dcv = jnp.zeros((CONV_CHUNK, LANES), F32)
                for k in range(CONV_A_W):
                    off = r0 + (CONV_A_W - 1) - k
                    e = eabuf[lb_i, off:off + CONV_CHUNK, :]
                    dcv = dcv + wa_ref[k:k + 1, lanes] * e
                    aacc[k, :, lanes] += jnp.sum((cv * e).reshape(CONV_CHUNK // SUBLANES, SUBLANES, LANES), axis=0)
                du_ref[rows, cols(D_A)] = (dcv * u_ref[rows, cols(2 * D_A)].astype(F32)).astype(BF16)
                du_ref[rows, cols(2 * D_A)] = (dcv * u_ref[rows, cols(D_A)].astype(F32)).astype(BF16)

                z = zbuf[lb_i, rows, :]
                dz = jnp.zeros((CONV_CHUNK, LANES), F32)
                for k in range(CONV_B_W):
                    off = r0 + (CONV_B_W - 1) - k
                    e = ebuf[lb_i, off:off + CONV_CHUNK, :]
                    dz = dz + wb_ref[k:k + 1, lanes] * e
                    wacc[k, :, lanes] += jnp.sum((z * e).reshape(CONV_CHUNK // SUBLANES, SUBLANES, LANES), axis=0)
                glu_v = u_ref[rows, cols(3 * D_A)].astype(F32)
                sgg = gatebuf[lb_i, rows, :]
                du_ref[rows, cols(3 * D_A)] = (dz * sgg).astype(BF16)
                du_ref[rows, cols(3 * D_A + D_B)] = (dz * glu_v * sgg * (1.0 - sgg)).astype(BF16)

            ebuf[lb_i, tc:tc + HALO_B, :] = ebuf[lb_i, 0:HALO_B, :]
            eabuf[lb_i, tc:tc + HALO_A, :] = eabuf[lb_i, 0:HALO_A, :]

        @pl.when(step == n_tiles - 1)
        def _():
            for k in range(CONV_B_W):
                dwb_ref[k:k + 1, :] = jnp.sum(wacc[k], axis=0, keepdims=True)
            for k in range(CONV_A_W):
                dwa_ref[k:k + 1, :] = jnp.sum(aacc[k], axis=0, keepdims=True)
            dbb_ref[...] = jnp.sum(vacc[0], axis=0, keepdims=True)
            dlg_ref[...] = jnp.sum(vacc[1], axis=0, keepdims=True)
            dlb_ref[...] = jnp.sum(vacc[2], axis=0, keepdims=True)

    return _hosted_call(
        core, comm, name, n_tiles,
        in_specs=[_rtile(tc, D_A + D_B, n_tiles), _rtile(tc, D_IN_ALL, n_tiles), _rtile(tc, D_B, n_tiles),
                  _rtile(tc, D_A, n_tiles), _whole((CONV_A_W, D_A)), _whole((CONV_B_W, D_B)), _whole((1, D_B)),
                  _whole((1, D_B))],
        out_specs=[_rtile(tc, D_IN_ALL, n_tiles), _whole((CONV_A_W, D_A)), _whole((CONV_B_W, D_B)), _whole((1, D_B)),
                   _whole((1, D_B)), _whole((1, D_B))],
        out_shape=[_sds((t, D_IN_ALL), BF16), _sds((CONV_A_W, D_A), F32), _sds((CONV_B_W, D_B), F32), _sds((1, D_B), F32),
                   _sds((1, D_B), F32), _sds((1, D_B), F32)],
        scratch_shapes=[pltpu.VMEM((D_B // LANES, tc + HALO_B, LANES), F32), pltpu.VMEM((D_A // LANES, tc + HALO_A, LANES), F32),
                        pltpu.VMEM((D_B // LANES, tc, LANES), F32), pltpu.VMEM((D_A // LANES, tc, LANES), F32),
                        pltpu.VMEM((D_B // LANES, tc, LANES), F32),
                        pltpu.VMEM((CONV_B_W, SUBLANES, D_B), F32), pltpu.VMEM((CONV_A_W, SUBLANES, D_A), F32),
                        pltpu.VMEM((3, SUBLANES, D_B), F32)],
        operands=(dy, u, zc, ca, wa, wb, lg, lb))


def _in_bwd(du, w_t, x, dx1, g, tm, name):
    t, d = x.shape
    n = w_t.shape[0]

    def body(du_ref, w_ref, x_ref, dx1_ref, g_ref, dx_ref, dg_ref):
        _zero_at_first_step(dg_ref)
        dh = _dot(du_ref[...], w_ref[...])
        dxn, dg = _rms_bwd(dh, x_ref[...], g_ref[...])
        dg_ref[...] += dg
        dx_ref[...] = dx1_ref[...] + dxn

    return pl.pallas_call(
        body, name=name, grid=(t // tm,),
        in_specs=[_tile(tm, n), _resident((n, d)), _tile(tm, d), _tile(tm, d), _whole((1, d))],
        out_specs=[_tile(tm, d), _whole((1, d))],
        out_shape=[_sds((t, d), F32), _sds((1, d), F32)],
        compiler_params=_params(),
    )(du, w_t, x, dx1, g)


def _wgrad(a, b, tk, bm, bn, name, comm=None, cols=None):
    t, m = a.shape
    first, n = (0, b.shape[1]) if cols is None else cols
    first_block = first // bn
    n_k = t // tk

    def body(a_ref, b_ref, o_ref, acc):
        @pl.when(pl.program_id(2) == 0)
        def _():
            acc[...] = jnp.zeros_like(acc)

        acc[...] += _dot_tn(a_ref[...], b_ref[...])

        @pl.when(pl.program_id(2) == n_k - 1)
        def _():
            o_ref[...] = acc[...].astype(BF16)

    call = dict(
        grid=(m // bm, n // bn, n_k),
        in_specs=[pl.BlockSpec((tk, bm), lambda i, j, k: (k, i)),
                  pl.BlockSpec((tk, bn), lambda i, j, k: (k, first_block + j))],
        out_specs=[pl.BlockSpec((bm, bn), lambda i, j, k: (i, j))],
        out_shape=[_sds((m, n), BF16)],
        scratch_shapes=[pltpu.VMEM((bm, bn), F32)])
    if comm is None:
        return pl.pallas_call(body, name=name, compiler_params=_params(3), **call)(a, b)[0]
    (out,), received = _hosted_call(body, comm, name, operands=(a, b), **call)
    return out, received


def _adamw(items, tr, name):
    n_parts, r, c = items[0][0].shape
    n_items = len(items)

    def body(*refs):
        ins, outs = refs[:4 * n_items], refs[4 * n_items:]
        for i in range(n_items):
            p_ref, w_ref, m_ref, v_ref = ins[4 * i:4 * i + 4]
            g_ref, d_ref, nm_ref, nv_ref = outs[4 * i:4 * i + 4]
            g = p_ref[0].astype(F32)
            for j in range(1, n_parts):
                g = g + p_ref[j].astype(F32)
            g_ref[...] = g
            nm = ADAM_B1 * m_ref[...] + (1.0 - ADAM_B1) * g
            nv = ADAM_B2 * v_ref[...] + (1.0 - ADAM_B2) * (g * g)
            nm_ref[...] = nm
            nv_ref[...] = nv
            m_hat = nm / (1.0 - ADAM_B1 ** ADAM_STEP)
            v_hat = nv / (1.0 - ADAM_B2 ** ADAM_STEP)
            d_ref[...] = -ADAM_LR * (m_hat / (jnp.sqrt(v_hat) + ADAM_EPS) + ADAM_WD * w_ref[...])

    results = pl.pallas_call(
        body, name=name, grid=(r // tr,),
        in_specs=([pl.BlockSpec((n_parts, tr, c), lambda i: (0, i, 0))] + [_tile(tr, c)] * 3) * n_items,
        out_specs=[_tile(tr, c)] * (4 * n_items),
        out_shape=[_sds((r, c), F32)] * (4 * n_items),
        compiler_params=_params(),
    )(*[a for item in items for a in item])
    return [results[4 * i:4 * i + 4] for i in range(n_items)]


def _sum_parts(parts, name):
    n_parts, r, c = parts.shape

    def body(p_ref, o_ref):
        acc = p_ref[0]
        for j in range(1, n_parts):
            acc = acc + p_ref[j]
        o_ref[...] = acc

    return pl.pallas_call(
        body, name=name, grid=(1,),
        in_specs=[_whole((n_parts, r, c))], out_specs=_whole((r, c)), out_shape=_sds((r, c), F32),
        compiler_params=_params(),
    )(parts)


def _row(v):
    return v.reshape(1, -1)


def _by_owner_rows(g):
    return g.reshape(N_DEV, g.shape[0] // N_DEV, g.shape[1])


WEIGHTS = ("mix_pre_g", "w_mix_in", "conv_a_w", "conv_b_w", "conv_b_b", "ln_b_g", "ln_b_b", "w_mix_out", "mix_post_g",
           "xa_pre_g", "mem_norm_g", "w_q", "w_k", "w_v", "w_o", "xa_post_g", "ffn_pre_g", "w_gate", "w_up", "w_down",
           "ffn_post_g")
LARGE = ("w_mix_in", "w_mix_out", "w_q", "w_k", "w_v", "w_o", "w_gate", "w_up", "w_down")
COLUMN_SHARDED = ("w_mix_in", "w_gate", "w_up")
GAINS = ("mix_pre_g", "mix_post_g", "xa_pre_g", "mem_norm_g", "xa_post_g", "ffn_pre_g", "ffn_post_g")
CHANNEL_VECTORS = ("conv_b_b", "ln_b_g", "ln_b_b")
CONV_TAPS = ("conv_a_w", "conv_b_w")
SMALL = GAINS + CHANNEL_VECTORS + CONV_TAPS
CONV_COLS_PER_DEVICE = D_A // N_DEV
TOKEN_TILE = 512
FFN_FWD_TOKEN_TILE = 512
FFN_TOKEN_TILE = 256
CONV_TOKEN_TILE = 256
WGRAD_TOKEN_TILE = 2048
SMALL_WGRAD_TOKEN_TILE = 4096
IN_BWD_TOKEN_TILE = 1024
ADAM_ROWS_PER_STEP = 64


def _lane_rows(v):
    flat = v.reshape(-1)
    tile = SUBLANES * LANES
    flat = jnp.pad(flat, (0, (-flat.shape[0]) % tile))
    return flat.reshape(-1, LANES)


def _pack_small(values, names):
    return jnp.concatenate([_lane_rows(values[n]) for n in names], axis=0)


def _unpack_small(packed, like, names):
    out, off = {}, 0
    for n in names:
        size = like[n].size
        rows = _lane_rows(like[n]).shape[0]
        out[n] = packed[off:off + rows, :].reshape(-1)[:size].reshape(like[n].shape)
        off += rows
    return out


def kernel(x, mem, mix_pre_g, w_mix_in, conv_a_w, conv_b_w, conv_b_b, ln_b_g, ln_b_b, w_mix_out, mix_post_g, xa_pre_g, mem_norm_g, w_q, w_k, w_v, w_o, xa_post_g, ffn_pre_g, w_gate, w_up, w_down, ffn_post_g, loss_target, m_mix_pre_g, m_w_mix_in, m_conv_a_w, m_conv_b_w, m_conv_b_b, m_ln_b_g, m_ln_b_b, m_w_mix_out, m_mix_post_g, m_xa_pre_g, m_mem_norm_g, m_w_q, m_w_k, m_w_v, m_w_o, m_xa_post_g, m_ffn_pre_g, m_w_gate, m_w_up, m_w_down, m_ffn_post_g, v_mix_pre_g, v_w_mix_in, v_conv_a_w, v_conv_b_w, v_conv_b_b, v_ln_b_g, v_ln_b_b, v_w_mix_out, v_mix_post_g, v_xa_pre_g, v_mem_norm_g, v_w_q, v_w_k, v_w_v, v_w_o, v_xa_post_g, v_ffn_pre_g, v_w_gate, v_w_up, v_w_down, v_ffn_post_g):
    given = dict(locals())
    w = {n: given[n] for n in WEIGHTS}
    m = {n: given["m_" + n] for n in WEIGHTS}
    v = {n: given["v_" + n] for n in WEIGHTS}
    xs, mems, target = x[0], mem[0], loss_target[0]
    t = xs.shape[0]
    tm, tm_ffn, tc, tk = min(TOKEN_TILE, t), min(FFN_TOKEN_TILE, t), min(CONV_TOKEN_TILE, t), min(WGRAD_TOKEN_TILE, t)
    tk_small = min(SMALL_WGRAD_TOKEN_TILE, t)
    g = {n: _row(w[n]) for n in GAINS}
    bb, lg, lb = (_row(w[n]) for n in CHANNEL_VECTORS)

    def shard_bf16(*names):
        return [w[n].astype(BF16) for n in names]

    def shard_bf16_t(*names):
        return [w[n].T.astype(BF16) for n in names]

    g_mix_in, g_taps = _comm_call(_Gather(shard_bf16_t("w_mix_in") + [_pack_small(w, CONV_TAPS)]), "gather_mixer")
    w_mix_in_t = g_mix_in.reshape(D_IN_ALL, D_MODEL)
    taps, off = {}, 0
    for n in CONV_TAPS:
        k, cols = w[n].shape
        rows = _lane_rows(w[n]).shape[0]
        blk = g_taps[:, off:off + rows, :].reshape(N_DEV, -1)[:, :k * cols].reshape(N_DEV, k, cols)
        taps[n] = blk.transpose(1, 0, 2).reshape(k, N_DEV * cols)
        off += rows
    wa, wb = taps["conv_a_w"], taps["conv_b_w"]

    (h1, u), gathered = _norm_matmul(xs, g["mix_pre_g"], w_mix_in_t.T, tm, "mix_in_fwd",
                                     _Gather(shard_bf16("w_mix_out", "w_q", "w_k", "w_v", "w_o")))
    w_mix_out, w_q, w_k, w_v, w_o = (a.reshape(D_MODEL, D_MODEL) for a in gathered)
    (ycat, zc, ca, x1, s1), (g_gate, g_up, g_down) = _conv_fwd(
        u, wa, wb, bb, lg, lb, xs, w_mix_out, g["mix_post_g"], tc, "conv_fwd",
        _Gather(shard_bf16_t("w_gate", "w_up") + shard_bf16("w_down")))
    w_gate_t, w_up_t, w_down = (a.reshape(D_FF, D_MODEL) for a in (g_gate, g_up, g_down))
    mem_n, kk, vv = _mem_fwd(mems, g["mem_norm_g"], w_k, w_v, "mem_fwd")
    x2, h2, q, o, a = _attn_fwd(x1, g["xa_pre_g"], w_q, kk, vv, w_o, g["xa_post_g"], tm, "attn_fwd")
    h3, gt, up, f, sq = _ffn_fwd(x2, g["ffn_pre_g"], w_gate_t, w_up_t, w_down, g["ffn_post_g"], target,
                                 min(FFN_FWD_TOKEN_TILE, t), "ffn_fwd")

    dx2, df, hd, dgt, dup, d_ffn_post, d_ffn_pre = _ffn_bwd(
        x2, f, target, gt, up, g["ffn_pre_g"], w_gate_t, w_up_t, w_down, g["ffn_post_g"], tm_ffn, "ffn_bwd")
    d_w_down = _wgrad(hd, df, tk, D_FF // 2, D_MODEL, "wgrad_down")
    d_w_gate_t = _wgrad(dgt, h3, tk, D_FF // 2, D_MODEL, "wgrad_gate")
    d_w_up_t = _wgrad(dup, h3, tk, D_FF // 2, D_MODEL, "wgrad_up")
    ffn_slabs = [_by_owner_rows(d) for d in (d_w_gate_t, d_w_up_t, d_w_down)]

    (dx1, da, dq, dk, dv, d_xa_post, d_xa_pre, ds1, dycat, d_mix_post), from_ffn = _attn_bwd(
        dx2, a, x1, q, kk, vv, g["xa_pre_g"], w_q, w_o, g["xa_post_g"], s1, w_mix_out, g["mix_post_g"], tm, "attn_bwd",
        _Exchange(ffn_slabs, [True] * 3))
    d_w_o = _wgrad(o, da, tk, D_MODEL, D_MODEL, "wgrad_o")
    d_w_q = _wgrad(h2, dq, tk, D_MODEL, D_MODEL, "wgrad_q")
    d_w_k, d_w_v, d_mem_norm = _mem_bwd(mems, mem_n, dk, dv, g["mem_norm_g"], w_k, w_v, "mem_bwd")
    d_w_mix_out = _wgrad(ycat, ds1, tk, D_MODEL, D_MODEL, "wgrad_mix_out")
    attn_slabs = [_by_owner_rows(d) for d in (d_w_mix_out, d_w_q, d_w_k, d_w_v, d_w_o)]

    (du, d_conv_a, d_conv_b, d_conv_bb, d_ln_g, d_ln_b), from_attn = _conv_bwd(
        dycat, u, zc, ca, wa, wb, lg, lb, tc, "conv_bwd", _Exchange(attn_slabs, [True] * 5))
    small_grads = dict(conv_a_w=d_conv_a, conv_b_w=d_conv_b, conv_b_b=d_conv_bb, ln_b_g=d_ln_g, ln_b_b=d_ln_b,
                       mix_post_g=d_mix_post, xa_pre_g=d_xa_pre, mem_norm_g=d_mem_norm, xa_post_g=d_xa_post,
                       ffn_pre_g=d_ffn_pre, ffn_post_g=d_ffn_post, loss=sq)
    names = tuple(n for n in SMALL if n != "mix_pre_g") + ("loss", "mix_pre_g")
    half = D_MODEL // 2
    d_in_lo = _wgrad(du, h1, tk_small, D_IN_ALL // 2, half, "wgrad_mix_in_lo", cols=(0, half))
    d_in_hi, (from_in_lo, early_small) = _wgrad(
        du, h1, tk_small, D_IN_ALL // 2, half, "wgrad_mix_in_hi",
        _Exchange([_by_owner_rows(d_in_lo), _pack_small(small_grads, names[:-1])], [True, False]), cols=(half, half))
    dx, d_mix_pre = _in_bwd(du, w_mix_in_t, xs, dx1, g["mix_pre_g"], min(IN_BWD_TOKEN_TILE, t), "mix_in_bwd")
    small_grads["mix_pre_g"] = d_mix_pre
    from_in_hi, late_small = _comm_call(
        _Exchange([_by_owner_rows(d_in_hi), _lane_rows(d_mix_pre)], [True, False]), "reduce_tail")
    all_small = jnp.concatenate([early_small, late_small], axis=1)

    received = dict(zip(("w_gate", "w_up", "w_down"), from_ffn))
    received.update(zip(("w_mix_out", "w_q", "w_k", "w_v", "w_o"), from_attn))
    received["w_mix_in"] = jnp.concatenate([from_in_lo, from_in_hi], axis=2)
    grad, delta, new_m, new_v = {}, {}, {}, {}
    for group in (("w_mix_in",), ("w_gate",), ("w_up",), ("w_down",), ("w_mix_out", "w_q", "w_k", "w_v", "w_o")):
        items = [[received[n]] + [a[n].T if n in COLUMN_SHARDED else a[n] for a in (w, m, v)] for n in group]
        rows = items[0][1].shape[0]
        results = _adamw(items, min(rows, ADAM_ROWS_PER_STEP) if len(group) > 1 else rows, "adamw_" + group[0])
        for n, result in zip(group, results):
            grad[n], delta[n], new_m[n], new_v[n] = [r.T if n in COLUMN_SHARDED else r for r in result]

    total = _unpack_small(_sum_parts(all_small, "sum_small"), small_grads, names)
    loss = jnp.sum(total.pop("loss")) * (0.5 / D_MODEL)
    first_col = _device_index() * CONV_COLS_PER_DEVICE
    for n in CONV_TAPS:
        total[n] = lax.dynamic_slice_in_dim(total[n], first_col, CONV_COLS_PER_DEVICE, axis=1)
    total = {n: total[n].reshape(w[n].shape) for n in SMALL}
    packed_small = [_pack_small(values, SMALL) for values in (total, w, m, v)]
    ((g_s, d_s, nm_s, nv_s),) = _adamw([[packed_small[0][None]] + packed_small[1:]], packed_small[0].shape[0],
                                       "adamw_small")
    for out, p in ((grad, g_s), (delta, d_s), (new_m, nm_s), (new_v, nv_s)):
        out.update(_unpack_small(p, w, SMALL))

    return (loss, dx[None], *[grad[n] for n in WEIGHTS], *[delta[n] for n in WEIGHTS], *[new_m[n] for n in WEIGHTS],
            *[new_v[n] for n in WEIGHTS])
```

```python
import jax
import jax.numpy as jnp
from jax import lax
from jax.experimental import pallas as pl
from jax.experimental.pallas import tpu as pltpu

F32 = jnp.float32
BF16 = jnp.bfloat16

D_MODEL = 1024
D_A = 512
D_B = 512
D_IN_ALL = 2560
CONV_A_W = 3
CONV_B_W = 31
XA_HEADS = 4
XA_HEAD_DIM = 256
D_FF = 2816
N_DEV = 8
RMS_EPS = 1e-6
LN_EPS = 1e-5
ADAM_LR = 0.001
ADAM_B1 = 0.9
ADAM_B2 = 0.999
ADAM_EPS = 1e-08
ADAM_WD = 0.01
ADAM_STEP = 10

VMEM_LIMIT_BYTES = 56 * 1024 * 1024
SUBLANES = 8
LANES = 128
HALO_B = 32
HALO_A = 8
CONV_FWD_CHUNK = 16
CONV_CHUNK = 32
GATHER_FORWARD_STEPS_BEFORE_END = 8

MESH = pl.DeviceIdType.MESH


def _params(n_grid_axes=1):
    return pltpu.CompilerParams(dimension_semantics=("arbitrary",) * n_grid_axes, vmem_limit_bytes=VMEM_LIMIT_BYTES)


def _sds(shape, dtype):
    return jax.ShapeDtypeStruct(shape, dtype)


def _tile(rows, cols):
    return pl.BlockSpec((rows, cols), lambda i: (i, 0))


def _rtile(rows, cols, n):
    return pl.BlockSpec((rows, cols), lambda i: (n - 1 - i, 0))


def _whole(shape):
    zeros = (0,) * len(shape)
    return pl.BlockSpec(shape, lambda i: zeros)


def _resident(shape):
    zeros = (0,) * len(shape)
    return pl.BlockSpec(shape, lambda i: zeros, pipeline_mode=pl.Buffered(1))


def _dot(a, b):
    return jnp.dot(a, b, preferred_element_type=F32)


def _dot_nt(a, b):
    return lax.dot_general(a, b, (((1,), (1,)), ((), ())), preferred_element_type=F32)


def _dot_tn(a, b):
    return lax.dot_general(a, b, (((0,), (0,)), ((), ())), preferred_element_type=F32)


def _sigmoid(x):
    return 1.0 / (1.0 + jnp.exp(-x))


def _rms_fwd(x, g):
    r = lax.rsqrt(jnp.mean(x * x, axis=-1, keepdims=True) + RMS_EPS)
    return x * r * g


def _rms_bwd(dy, xin, g):
    r = lax.rsqrt(jnp.mean(xin * xin, axis=-1, keepdims=True) + RMS_EPS)
    n = xin * r
    dg = jnp.sum(dy * n, axis=0, keepdims=True)
    dn = dy * g
    dx = r * (dn - n * jnp.mean(dn * n, axis=-1, keepdims=True))
    return dx, dg


def _zero_at_first_step(*refs):
    @pl.when(pl.program_id(0) == 0)
    def _():
        for ref in refs:
            ref[...] = jnp.zeros(ref.shape, ref.dtype)


def _place():
    return lax.axis_index("x"), lax.axis_index("y"), lax.axis_index("c")


def _device_index():
    x, y, c = _place()
    return 4 * x + 2 * y + c


class _Gather:
    def __init__(self, arrays):
        self.arrays = list(arrays)
        self.out_shape = [_sds((N_DEV, *a.shape), a.dtype) for a in self.arrays]
        n = len(self.arrays)
        self.scratch_shapes = [pltpu.SemaphoreType.DMA((n, 7)), pltpu.SemaphoreType.DMA((n, 7)),
                               pltpu.SemaphoreType.DMA((n,))]

    def forward_step(self, n_steps):
        return max(0, n_steps - 1 - GATHER_FORWARD_STEPS_BEFORE_END)

    def bind(self, srcs, dsts, send_sems, recv_sems, local_sems):
        x, y, cc = _place()
        me, sibling = (x, y, cc), (x, y, 1 - cc)
        chips = [(1 - x, y), (x, 1 - y), (1 - x, 1 - y)]

        def copy(a, k, owner, to, src=None):
            slot = dsts[a].at[4 * owner[0] + 2 * owner[1] + owner[2]]
            return pltpu.make_async_remote_copy(
                src_ref=slot if src is None else src, dst_ref=slot, send_sem=send_sems.at[a, k],
                recv_sem=recv_sems.at[a, k], device_id=to, device_id_type=MESH)

        def first(a):
            return [copy(a, 0, me, sibling, src=srcs[a])] + [
                copy(a, 1 + j, me, (*chip, cc), src=srcs[a]) for j, chip in enumerate(chips)]

        def passed(a, j):
            return copy(a, 4 + j, (*chips[j], cc), sibling)

        def mine(a):
            return pltpu.make_async_copy(srcs[a], dsts[a].at[4 * x + 2 * y + cc], local_sems.at[a])

        def start():
            for a in range(len(srcs)):
                mine(a).start()
                for cp in first(a):
                    cp.start()

        def forward():
            for a in range(len(srcs)):
                for j, chip in enumerate(chips):
                    copy(a, 1 + j, (*chip, cc), me).wait_recv()
                    passed(a, j).start()

        def finish():
            for a in range(len(srcs)):
                copy(a, 0, sibling, me).wait_recv()
                for j, chip in enumerate(chips):
                    copy(a, 4 + j, (*chip, 1 - cc), me).wait_recv()
                for cp in first(a) + [passed(a, j) for j in range(len(chips))]:
                    cp.wait_send()
                mine(a).wait()

        return start, forward, finish


class _Exchange:
    def __init__(self, arrays, scatter):
        self.arrays = list(arrays)
        self.scatter = list(scatter)
        self.out_shape = [_sds(a.shape if s else (N_DEV, *a.shape), a.dtype) for a, s in zip(self.arrays, self.scatter)]
        n = len(self.arrays)
        self.scratch_shapes = [pltpu.SemaphoreType.DMA((n, 7)), pltpu.SemaphoreType.DMA((n, 7)),
                               pltpu.SemaphoreType.DMA((n,))]

    def forward_step(self, n_steps):
        return n_steps - 1

    def bind(self, srcs, dsts, send_sems, recv_sems, local_sems):
        me = _device_index()

        def copies(a):
            out = []
            for k in range(1, N_DEV):
                p = me ^ k
                out.append(pltpu.make_async_remote_copy(
                    src_ref=srcs[a].at[p] if self.scatter[a] else srcs[a], dst_ref=dsts[a].at[me],
                    send_sem=send_sems.at[a, k - 1], recv_sem=recv_sems.at[a, k - 1],
                    device_id=(p >> 2, (p >> 1) & 1, p & 1), device_id_type=MESH))
            return out

        def mine(a):
            return pltpu.make_async_copy(srcs[a].at[me] if self.scatter[a] else srcs[a], dsts[a].at[me], local_sems.at[a])

        def start():
            for a in range(len(srcs)):
                mine(a).start()
                for cp in copies(a):
                    cp.start()

        def forward():
            pass

        def finish():
            for a in range(len(srcs)):
                for cp in copies(a):
                    cp.wait()
                mine(a).wait()

        return start, forward, finish


def _hosted_call(core, comm, name, grid, in_specs, out_specs, out_shape, scratch_shapes, operands):
    grid = (grid,) if isinstance(grid, int) else tuple(grid)
    n_steps = 1
    for extent in grid:
        n_steps *= extent
    n_in, n_out, n_scr, n_arr = len(in_specs), len(out_specs), len(scratch_shapes), len(comm.arrays)
    any_spec = pl.BlockSpec(memory_space=pl.ANY)

    def body(*refs):
        ins, refs = refs[:n_in], refs[n_in:]
        srcs, refs = refs[:n_arr], refs[n_arr:]
        outs, refs = refs[:n_out], refs[n_out:]
        dsts, refs = refs[:n_arr], refs[n_arr:]
        scratch, sems = refs[:n_scr], refs[n_scr:]
        start, forward, finish = comm.bind(srcs, dsts, *sems)
        step = pl.program_id(0)
        for axis in range(1, len(grid)):
            step = step * grid[axis] + pl.program_id(axis)
        pl.when(step == 0)(start)
        core(*ins, *outs, *scratch)
        pl.when(step == comm.forward_step(n_steps))(forward)
        pl.when(step == n_steps - 1)(finish)

    results = pl.pallas_call(
        body, name=name, grid=grid,
        in_specs=list(in_specs) + [any_spec] * n_arr,
        out_specs=list(out_specs) + [any_spec] * n_arr,
        out_shape=list(out_shape) + comm.out_shape,
        scratch_shapes=list(scratch_shapes) + comm.scratch_shapes,
        compiler_params=_params(len(grid)),
    )(*operands, *comm.arrays)
    return results[:n_out], results[n_out:]


def _comm_call(comm, name):
    return _hosted_call(lambda: None, comm, name, 1, [], [], [], [], [])[1]


def _norm_matmul(x, g, w, tm, name, comm):
    t, d = x.shape
    n = w.shape[1]

    def core(x_ref, g_ref, w_ref, h_ref, o_ref):
        h = _rms_fwd(x_ref[...], g_ref[...]).astype(BF16)
        h_ref[...] = h
        o_ref[...] = _dot(h, w_ref[...]).astype(BF16)

    return _hosted_call(
        core, comm, name, t // tm,
        in_specs=[_tile(tm, d), _whole((1, d)), _resident((d, n))],
        out_specs=[_tile(tm, d), _tile(tm, n)],
        out_shape=[_sds((t, d), BF16), _sds((t, n), BF16)],
        scratch_shapes=[], operands=(x, g, w))


def _conv_fwd(u, wa, wb, bb, lg, lb, xres, w_out, g_post, tc, name, comm):
    t = u.shape[0]
    d = w_out.shape[1]
    chunk = CONV_FWD_CHUNK
    n_chunks = tc // chunk
    piece = 2 * LANES

    def core(u_ref, wa_ref, wb_ref, bb_ref, lg_ref, lb_ref, x_ref, wo_ref, go_ref,
             y_ref, zc_ref, ca_ref, x1_ref, s_ref, zbuf, cvbuf, zcbuf):
        @pl.when(pl.program_id(0) == 0)
        def _():
            zbuf[:, 0:HALO_B, :] = jnp.zeros((D_B // LANES, HALO_B, LANES), F32)
            cvbuf[:, 0:HALO_A, :] = jnp.zeros((D_A // LANES, HALO_A, LANES), F32)

        s = None
        for lb_i in range(D_A // LANES):
            if lb_i > 0 and lb_i % 2 == 0:
                cols = slice((lb_i - 2) * LANES, lb_i * LANES)
                part = _dot(y_ref[:, cols], wo_ref[cols, :])
                s = part if s is None else s + part
            lanes = slice(lb_i * LANES, (lb_i + 1) * LANES)
            c_a = u_ref[:, D_A + lb_i * LANES:D_A + (lb_i + 1) * LANES].astype(F32)
            v_a = u_ref[:, 2 * D_A + lb_i * LANES:2 * D_A + (lb_i + 1) * LANES].astype(F32)
            cvbuf[lb_i, HALO_A:HALO_A + tc, :] = c_a * v_a
            glu_v = u_ref[:, 3 * D_A + lb_i * LANES:3 * D_A + (lb_i + 1) * LANES].astype(F32)
            glu_g = u_ref[:, 3 * D_A + D_B + lb_i * LANES:3 * D_A + D_B + (lb_i + 1) * LANES].astype(F32)
            zbuf[lb_i, HALO_B:HALO_B + tc, :] = glu_v * _sigmoid(glu_g)

            for c in range(n_chunks):
                r0 = c * chunk
                rows = slice(r0, r0 + chunk)
                acc = jnp.zeros((chunk, LANES), F32)
                for k in range(CONV_A_W):
                    off = r0 + HALO_A - (CONV_A_W - 1) + k
                    acc = acc + wa_ref[k:k + 1, lanes] * cvbuf[lb_i, off:off + chunk, :]
                ca_ref[rows, lanes] = acc.astype(BF16)
                y_ref[rows, lanes] = (u_ref[rows, lanes].astype(F32) * acc).astype(BF16)

                accb = jnp.zeros((chunk, LANES), F32)
                for k in range(CONV_B_W):
                    off = r0 + HALO_B - (CONV_B_W - 1) + k
                    accb = accb + wb_ref[k:k + 1, lanes] * zbuf[lb_i, off:off + chunk, :]
                zcbuf[rows, lanes] = accb + bb_ref[:, lanes]

            zbuf[lb_i, 0:HALO_B, :] = zbuf[lb_i, tc:tc + HALO_B, :]
            cvbuf[lb_i, 0:HALO_A, :] = cvbuf[lb_i, tc:tc + HALO_A, :]

        cols = slice(D_A - piece, D_A)
        s = s + _dot(y_ref[:, cols], wo_ref[cols, :])

        for c in range(n_chunks):
            rows = slice(c * chunk, (c + 1) * chunk)
            zc = zcbuf[rows, :]
            zc_ref[rows, :] = zc.astype(BF16)
            mu = jnp.mean(zc, axis=-1, keepdims=True)
            xc = zc - mu
            var = jnp.mean(xc * xc, axis=-1, keepdims=True)
            ln = xc * lax.rsqrt(var + LN_EPS) * lg_ref[...] + lb_ref[...]
            y_ref[rows, D_A:D_A + D_B] = (ln * _sigmoid(ln)).astype(BF16)

        s = s + _dot(y_ref[:, D_A:D_A + D_B], wo_ref[D_A:D_A + D_B, :])
        s_ref[...] = s.astype(BF16)
        x1_ref[...] = x_ref[...] + _rms_fwd(s, go_ref[...])

    return _hosted_call(
        core, comm, name, t // tc,
        in_specs=[_tile(tc, D_IN_ALL), _whole((CONV_A_W, D_A)), _whole((CONV_B_W, D_B)), _whole((1, D_B)),
                  _whole((1, D_B)), _whole((1, D_B)), _tile(tc, d), _resident((D_A + D_B, d)), _whole((1, d))],
        out_specs=[_tile(tc, D_A + D_B), _tile(tc, D_B), _tile(tc, D_A), _tile(tc, d), _tile(tc, d)],
        out_shape=[_sds((t, D_A + D_B), BF16), _sds((t, D_B), BF16), _sds((t, D_A), BF16), _sds((t, d), F32),
                   _sds((t, d), BF16)],
        scratch_shapes=[pltpu.VMEM((D_B // LANES, HALO_B + tc, LANES), F32),
                        pltpu.VMEM((D_A // LANES, HALO_A + tc, LANES), F32), pltpu.VMEM((tc, D_B), F32)],
        operands=(u, wa, wb, bb, lg, lb, xres, w_out, g_post))


def _mem_fwd(mem, g, wk, wv, name):
    m, d = mem.shape

    def body(mem_ref, g_ref, wk_ref, wv_ref, n_ref, k_ref, v_ref):
        n = _rms_fwd(mem_ref[...], g_ref[...]).astype(BF16)
        n_ref[...] = n
        k_ref[...] = _dot(n, wk_ref[...]).astype(BF16)
        v_ref[...] = _dot(n, wv_ref[...]).astype(BF16)

    return pl.pallas_call(
        body, name=name, grid=(1,),
        in_specs=[_whole((m, d)), _whole((1, d)), _whole((d, d)), _whole((d, d))],
        out_specs=[_whole((m, d))] * 3,
        out_shape=[_sds((m, d), BF16)] * 3,
        compiler_params=_params(),
    )(mem, g, wk, wv)


def _softmax_rows(s):
    e = jnp.exp(s - jnp.max(s, axis=-1, keepdims=True))
    return e / jnp.sum(e, axis=-1, keepdims=True)


def _attn_fwd(x1, g_pre, wq, k, v, wo, g_post, tm, name):
    t, d = x1.shape
    m = k.shape[0]
    scale = XA_HEAD_DIM ** -0.5

    def body(x_ref, gp_ref, wq_ref, k_ref, v_ref, wo_ref, go_ref, x2_ref, h_ref, q_ref, o_ref, a_ref):
        x = x_ref[...]
        h = _rms_fwd(x, gp_ref[...]).astype(BF16)
        h_ref[...] = h
        q_ref[...] = _dot(h, wq_ref[...]).astype(BF16)
        for hd in range(XA_HEADS):
            cols = slice(hd * XA_HEAD_DIM, (hd + 1) * XA_HEAD_DIM)
            p = _softmax_rows(_dot_nt(q_ref[:, cols], k_ref[:, cols]) * scale)
            o_ref[:, cols] = _dot(p.astype(BF16), v_ref[:, cols]).astype(BF16)
        a = _dot(o_ref[...], wo_ref[...])
        a_ref[...] = a.astype(BF16)
        x2_ref[...] = x + _rms_fwd(a, go_ref[...])

    return pl.pallas_call(
        body, name=name, grid=(t // tm,),
        in_specs=[_tile(tm, d), _whole((1, d)), _resident((d, d)), _whole((m, d)), _whole((m, d)), _resident((d, d)),
                  _whole((1, d))],
        out_specs=[_tile(tm, d)] * 5,
        out_shape=[_sds((t, d), F32)] + [_sds((t, d), BF16)] * 4,
        compiler_params=_params(),
    )(x1, g_pre, wq, k, v, wo, g_post)


def _ffn_fwd(x2, g_pre, wg_t, wu_t, wd, g_post, target, tm, name):
    t, d = x2.shape
    f = wg_t.shape[0]

    def body(x_ref, gp_ref, wg_ref, wu_ref, wd_ref, go_ref, tgt_ref, h_ref, gt_ref, up_ref, f_ref, sq_ref):
        _zero_at_first_step(sq_ref)
        x = x_ref[...]
        h = _rms_fwd(x, gp_ref[...]).astype(BF16)
        h_ref[...] = h
        gt = _dot_nt(h, wg_ref[...])
        up = _dot_nt(h, wu_ref[...])
        gt_ref[...] = gt.astype(BF16)
        up_ref[...] = up.astype(BF16)
        hd = (gt * _sigmoid(gt) * up).astype(BF16)
        ff = _dot(hd, wd_ref[...])
        f_ref[...] = ff.astype(BF16)
        err = x + _rms_fwd(ff, go_ref[...]) - tgt_ref[...]
        sq_ref[...] += jnp.sum(err * err, axis=0, keepdims=True)

    return pl.pallas_call(
        body, name=name, grid=(t // tm,),
        in_specs=[_tile(tm, d), _whole((1, d)), _resident((f, d)), _resident((f, d)), _resident((f, d)), _whole((1, d)),
                  _tile(tm, d)],
        out_specs=[_tile(tm, d), _tile(tm, f), _tile(tm, f), _tile(tm, d), _whole((1, d))],
        out_shape=[_sds((t, d), BF16), _sds((t, f), BF16), _sds((t, f), BF16), _sds((t, d), BF16), _sds((1, d), F32)],
        compiler_params=_params(),
    )(x2, g_pre, wg_t, wu_t, wd, g_post, target)


def _ffn_bwd(x2, f, target, gt, up, g_pre, wg_t, wu_t, wd, g_post, tm, name):
    t, d = x2.shape
    ff = wg_t.shape[0]

    def body(x_ref, f_ref, tgt_ref, gt_ref, up_ref, gp_ref, wg_ref, wu_ref, wd_ref, go_ref,
             dx_ref, df_ref, hd_ref, dgt_ref, dup_ref, dgo_ref, dgp_ref):
        _zero_at_first_step(dgo_ref, dgp_ref)
        x = x_ref[...]
        fo = f_ref[...].astype(F32)
        dx3 = (x + _rms_fwd(fo, go_ref[...]) - tgt_ref[...]) * (1.0 / d)
        df, dgo = _rms_bwd(dx3, fo, go_ref[...])
        dgo_ref[...] += dgo
        df = df.astype(BF16)
        df_ref[...] = df
        dhd = _dot_nt(df, wd_ref[...])
        gt = gt_ref[...].astype(F32)
        up = up_ref[...].astype(F32)
        sg = _sigmoid(gt)
        si = gt * sg
        hd_ref[...] = (si * up).astype(BF16)
        dup = (dhd * si).astype(BF16)
        dgt = (dhd * up * (sg * (1.0 + gt * (1.0 - sg)))).astype(BF16)
        dup_ref[...] = dup
        dgt_ref[...] = dgt
        dh = _dot(dgt, wg_ref[...]) + _dot(dup, wu_ref[...])
        dxn, dgp = _rms_bwd(dh, x, gp_ref[...])
        dgp_ref[...] += dgp
        dx_ref[...] = dx3 + dxn

    return pl.pallas_call(
        body, name=name, grid=(t // tm,),
        in_specs=[_tile(tm, d), _tile(tm, d), _tile(tm, d), _tile(tm, ff), _tile(tm, ff), _whole((1, d)),
                  _resident((ff, d)), _resident((ff, d)), _resident((ff, d)), _whole((1, d))],
        out_specs=[_tile(tm, d), _tile(tm, d), _tile(tm, ff), _tile(tm, ff), _tile(tm, ff), _whole((1, d)), _whole((1, d))],
        out_shape=[_sds((t, d), F32), _sds((t, d), BF16), _sds((t, ff), BF16), _sds((t, ff), BF16), _sds((t, ff), BF16),
                   _sds((1, d), F32), _sds((1, d), F32)],
        compiler_params=_params(),
    )(x2, f, target, gt, up, g_pre, wg_t, wu_t, wd, g_post)


def _attn_bwd(dx2, a, x1, q, k, v, g_pre, wq, wo, g_post, s_mix, w_mix_out, g_mix_post, tm, name, comm):
    t, d = x1.shape
    m = k.shape[0]
    kdim = w_mix_out.shape[0]
    scale = XA_HEAD_DIM ** -0.5

    def core(dx2_ref, a_ref, x_ref, q_ref, k_ref, v_ref, gp_ref, wq_ref, wo_ref, go_ref, s_ref, wm_ref, gm_ref,
             dx1_ref, da_ref, dq_ref, dk_ref, dv_ref, dgo_ref, dgp_ref, ds_ref, dy_ref, dgm_ref, do_buf):
        _zero_at_first_step(dgo_ref, dgp_ref, dk_ref, dv_ref, dgm_ref)
        dx2 = dx2_ref[...]
        da, dgo = _rms_bwd(dx2, a_ref[...].astype(F32), go_ref[...])
        dgo_ref[...] += dgo
        da = da.astype(BF16)
        da_ref[...] = da
        do_buf[...] = _dot_nt(da, wo_ref[...]).astype(BF16)
        for hd in range(XA_HEADS):
            cols = slice(hd * XA_HEAD_DIM, (hd + 1) * XA_HEAD_DIM)
            qh = q_ref[:, cols]
            p = _softmax_rows(_dot_nt(qh, k_ref[:, cols]) * scale)
            do_h = do_buf[:, cols]
            dp = _dot_nt(do_h, v_ref[:, cols])
            dv_ref[:, cols] += _dot_tn(p.astype(BF16), do_h)
            ds = (p * (dp - jnp.sum(dp * p, axis=-1, keepdims=True)) * scale).astype(BF16)
            dq_ref[:, cols] = _dot(ds, k_ref[:, cols]).astype(BF16)
            dk_ref[:, cols] += _dot_tn(ds, qh)
        dh = _dot_nt(dq_ref[...], wq_ref[...])
        dxn, dgp = _rms_bwd(dh, x_ref[...], gp_ref[...])
        dgp_ref[...] += dgp
        dx1 = dx2 + dxn
        dx1_ref[...] = dx1
        ds, dgm = _rms_bwd(dx1, s_ref[...].astype(F32), gm_ref[...])
        dgm_ref[...] += dgm
        ds = ds.astype(BF16)
        ds_ref[...] = ds
        dy_ref[...] = _dot_nt(ds, wm_ref[...]).astype(BF16)

    return _hosted_call(
        core, comm, name, t // tm,
        in_specs=[_tile(tm, d), _tile(tm, d), _tile(tm, d), _tile(tm, d), _whole((m, d)), _whole((m, d)), _whole((1, d)),
                  _resident((d, d)), _resident((d, d)), _whole((1, d)), _tile(tm, d), _resident((kdim, d)), _whole((1, d))],
        out_specs=[_tile(tm, d), _tile(tm, d), _tile(tm, d), _whole((m, d)), _whole((m, d)), _whole((1, d)), _whole((1, d)),
                   _tile(tm, d), _tile(tm, kdim), _whole((1, d))],
        out_shape=[_sds((t, d), F32), _sds((t, d), BF16), _sds((t, d), BF16), _sds((m, d), F32), _sds((m, d), F32),
                   _sds((1, d), F32), _sds((1, d), F32), _sds((t, d), BF16), _sds((t, kdim), BF16), _sds((1, d), F32)],
        scratch_shapes=[pltpu.VMEM((tm, d), BF16)],
        operands=(dx2, a, x1, q, k, v, g_pre, wq, wo, g_post, s_mix, w_mix_out, g_mix_post))


def _mem_bwd(mem, mem_n, dk, dv, g, wk, wv, name):
    m, d = mem.shape

    def body(mem_ref, n_ref, dk_ref, dv_ref, g_ref, wk_ref, wv_ref, dwk_ref, dwv_ref, dg_ref):
        dk = dk_ref[...].astype(BF16)
        dv = dv_ref[...].astype(BF16)
        n = n_ref[...]
        dwk_ref[...] = _dot_tn(n, dk).astype(BF16)
        dwv_ref[...] = _dot_tn(n, dv).astype(BF16)
        dn = _dot_nt(dk, wk_ref[...]) + _dot_nt(dv, wv_ref[...])
        _, dg = _rms_bwd(dn, mem_ref[...], g_ref[...])
        dg_ref[...] = dg

    return pl.pallas_call(
        body, name=name, grid=(1,),
        in_specs=[_whole((m, d)), _whole((m, d)), _whole((m, d)), _whole((m, d)), _whole((1, d)), _whole((d, d)),
                  _whole((d, d))],
        out_specs=[_whole((d, d)), _whole((d, d)), _whole((1, d))],
        out_shape=[_sds((d, d), BF16), _sds((d, d), BF16), _sds((1, d), F32)],
        compiler_params=_params(),
    )(mem, mem_n, dk, dv, g, wk, wv)


def _conv_bwd(dy, u, zc, ca, wa, wb, lg, lb, tc, name, comm):
    t = u.shape[0]
    n_tiles = t // tc
    n_chunks = tc // CONV_CHUNK

    def core(dy_ref, u_ref, zc_ref, ca_ref, wa_ref, wb_ref, lg_ref, lb_ref,
             du_ref, dwa_ref, dwb_ref, dbb_ref, dlg_ref, dlb_ref, ebuf, eabuf, zbuf, cvbuf, gatebuf, wacc, aacc, vacc):
        step = pl.program_id(0)

        @pl.when(step == 0)
        def _():
            ebuf[:, tc:tc + HALO_B, :] = jnp.zeros((D_B // LANES, HALO_B, LANES), F32)
            eabuf[:, tc:tc + HALO_A, :] = jnp.zeros((D_A // LANES, HALO_A, LANES), F32)
            wacc[...] = jnp.zeros_like(wacc)
            aacc[...] = jnp.zeros_like(aacc)
            vacc[...] = jnp.zeros_like(vacc)

        dbb = jnp.zeros((SUBLANES, D_B), F32)
        dlg = jnp.zeros((SUBLANES, D_B), F32)
        dlb = jnp.zeros((SUBLANES, D_B), F32)
        for c in range(n_chunks):
            rows = slice(c * CONV_CHUNK, (c + 1) * CONV_CHUNK)
            dy_a = dy_ref[rows, 0:D_A].astype(F32)
            b_a = u_ref[rows, 0:D_A].astype(F32)
            du_ref[rows, 0:D_A] = (dy_a * ca_ref[rows, :].astype(F32)).astype(BF16)
            dca = dy_a * b_a
            cv = u_ref[rows, D_A:2 * D_A].astype(F32) * u_ref[rows, 2 * D_A:3 * D_A].astype(F32)
            gate = _sigmoid(u_ref[rows, 3 * D_A + D_B:3 * D_A + 2 * D_B].astype(F32))
            z = u_ref[rows, 3 * D_A:3 * D_A + D_B].astype(F32) * gate
            for lb_i in range(D_A // LANES):
                lanes = slice(lb_i * LANES, (lb_i + 1) * LANES)
                eabuf[lb_i, rows, :] = dca[:, lanes]
                cvbuf[lb_i, rows, :] = cv[:, lanes]
                zbuf[lb_i, rows, :] = z[:, lanes]
                gatebuf[lb_i, rows, :] = gate[:, lanes]

            zcv = zc_ref[rows, :].astype(F32)
            mu = jnp.mean(zcv, axis=-1, keepdims=True)
            xc = zcv - mu
            rstd = lax.rsqrt(jnp.mean(xc * xc, axis=-1, keepdims=True) + LN_EPS)
            xhat = xc * rstd
            ln = xhat * lg_ref[...] + lb_ref[...]
            sg = _sigmoid(ln)
            dln = dy_ref[rows, D_A:D_A + D_B].astype(F32) * (sg * (1.0 + ln * (1.0 - sg)))
            dlg = dlg + jnp.sum((dln * xhat).reshape(CONV_CHUNK // SUBLANES, SUBLANES, D_B), axis=0)
            dlb = dlb + jnp.sum(dln.reshape(CONV_CHUNK // SUBLANES, SUBLANES, D_B), axis=0)
            dxh = dln * lg_ref[...]
            dzc = rstd * (dxh - jnp.mean(dxh, axis=-1, keepdims=True) - xhat * jnp.mean(dxh * xhat, axis=-1, keepdims=True))
            dbb = dbb + jnp.sum(dzc.reshape(CONV_CHUNK // SUBLANES, SUBLANES, D_B), axis=0)
            for lb_i in range(D_B // LANES):
                ebuf[lb_i, rows, :] = dzc[:, lb_i * LANES:(lb_i + 1) * LANES]
        vacc[0] += dbb
        vacc[1] += dlg
        vacc[2] += dlb

        for lb_i in range(D_A // LANES):
            lanes = slice(lb_i * LANES, (lb_i + 1) * LANES)

            def cols(first):
                return slice(first + lb_i * LANES, first + (lb_i + 1) * LANES)

            for c in range(n_chunks):
                r0 = c * CONV_CHUNK
                rows = slice(r0, r0 + CONV_CHUNK)
                cv = cvbuf[lb_i, rows, :]
                dcv = jnp.zeros((CONV_CHUNK, LANES), F32)
                for k in range(CONV_A_W):
                    off = r0 + (CONV_A_W - 1) - k
                    e = eabuf[lb_i, off:off + CONV_CHUNK, :]
                    dcv = dcv + wa_ref[k:k + 1, lanes] * e
                    aacc[k, :, lanes] += jnp.sum((cv * e).reshape(CONV_CHUNK // SUBLANES, SUBLANES, LANES), axis=0)
                du_ref[rows, cols(D_A)] = (dcv * u_ref[rows, cols(2 * D_A)].astype(F32)).astype(BF16)
                du_ref[rows, cols(2 * D_A)] = (dcv * u_ref[rows, cols(D_A)].astype(F32)).astype(BF16)

                z = zbuf[lb_i, rows, :]
                dz = jnp.zeros((CONV_CHUNK, LANES), F32)
                for k in range(CONV_B_W):
                    off = r0 + (CONV_B_W - 1) - k
                    e = ebuf[lb_i, off:off + CONV_CHUNK, :]
                    dz = dz + wb_ref[k:k + 1, lanes] * e
                    wacc[k, :, lanes] += jnp.sum((z * e).reshape(CONV_CHUNK // SUBLANES, SUBLANES, LANES), axis=0)
                glu_v = u_ref[rows, cols(3 * D_A)].astype(F32)
                sgg = gatebuf[lb_i, rows, :]
                du_ref[rows, cols(3 * D_A)] = (dz * sgg).astype(BF16)
                du_ref[rows, cols(3 * D_A + D_B)] = (dz * glu_v * sgg * (1.0 - sgg)).astype(BF16)

            ebuf[lb_i, tc:tc + HALO_B, :] = ebuf[lb_i, 0:HALO_B, :]
            eabuf[lb_i, tc:tc + HALO_A, :] = eabuf[lb_i, 0:HALO_A, :]

        @pl.when(step == n_tiles - 1)
        def _():
            for k in range(CONV_B_W):
                dwb_ref[k:k + 1, :] = jnp.sum(wacc[k], axis=0, keepdims=True)
            for k in range(CONV_A_W):
                dwa_ref[k:k + 1, :] = jnp.sum(aacc[k], axis=0, keepdims=True)
            dbb_ref[...] = jnp.sum(vacc[0], axis=0, keepdims=True)
            dlg_ref[...] = jnp.sum(vacc[1], axis=0, keepdims=True)
            dlb_ref[...] = jnp.sum(vacc[2], axis=0, keepdims=True)

    return _hosted_call(
        core, comm, name, n_tiles,
        in_specs=[_rtile(tc, D_A + D_B, n_tiles), _rtile(tc, D_IN_ALL, n_tiles), _rtile(tc, D_B, n_tiles),
                  _rtile(tc, D_A, n_tiles), _whole((CONV_A_W, D_A)), _whole((CONV_B_W, D_B)), _whole((1, D_B)),
                  _whole((1, D_B))],
        out_specs=[_rtile(tc, D_IN_ALL, n_tiles), _whole((CONV_A_W, D_A)), _whole((CONV_B_W, D_B)), _whole((1, D_B)),
                   _whole((1, D_B)), _whole((1, D_B))],
        out_shape=[_sds((t, D_IN_ALL), BF16), _sds((CONV_A_W, D_A), F32), _sds((CONV_B_W, D_B), F32), _sds((1, D_B), F32),
                   _sds((1, D_B), F32), _sds((1, D_B), F32)],
        scratch_shapes=[pltpu.VMEM((D_B // LANES, tc + HALO_B, LANES), F32), pltpu.VMEM((D_A // LANES, tc + HALO_A, LANES), F32),
                        pltpu.VMEM((D_B // LANES, tc, LANES), F32), pltpu.VMEM((D_A // LANES, tc, LANES), F32),
                        pltpu.VMEM((D_B // LANES, tc, LANES), F32),
                        pltpu.VMEM((CONV_B_W, SUBLANES, D_B), F32), pltpu.VMEM((CONV_A_W, SUBLANES, D_A), F32),
                        pltpu.VMEM((3, SUBLANES, D_B), F32)],
        operands=(dy, u, zc, ca, wa, wb, lg, lb))


def _in_bwd(du, w_t, x, dx1, g, tm, name):
    t, d = x.shape
    n = w_t.shape[0]

    def body(du_ref, w_ref, x_ref, dx1_ref, g_ref, dx_ref, dg_ref):
        _zero_at_first_step(dg_ref)
        dh = _dot(du_ref[...], w_ref[...])
        dxn, dg = _rms_bwd(dh, x_ref[...], g_ref[...])
        dg_ref[...] += dg
        dx_ref[...] = dx1_ref[...] + dxn

    return pl.pallas_call(
        body, name=name, grid=(t // tm,),
        in_specs=[_tile(tm, n), _resident((n, d)), _tile(tm, d), _tile(tm, d), _whole((1, d))],
        out_specs=[_tile(tm, d), _whole((1, d))],
        out_shape=[_sds((t, d), F32), _sds((1, d), F32)],
        compiler_params=_params(),
    )(du, w_t, x, dx1, g)


def _wgrad(a, b, tk, bm, bn, name, comm=None, cols=None):
    t, m = a.shape
    first, n = (0, b.shape[1]) if cols is None else cols
    first_block = first // bn
    n_k = t // tk

    def body(a_ref, b_ref, o_ref, acc):
        @pl.when(pl.program_id(2) == 0)
        def _():
            acc[...] = jnp.zeros_like(acc)

        acc[...] += _dot_tn(a_ref[...], b_ref[...])

        @pl.when(pl.program_id(2) == n_k - 1)
        def _():
            o_ref[...] = acc[...].astype(BF16)

    call = dict(
        grid=(m // bm, n // bn, n_k),
        in_specs=[pl.BlockSpec((tk, bm), lambda i, j, k: (k, i)),
                  pl.BlockSpec((tk, bn), lambda i, j, k: (k, first_block + j))],
        out_specs=[pl.BlockSpec((bm, bn), lambda i, j, k: (i, j))],
        out_shape=[_sds((m, n), BF16)],
        scratch_shapes=[pltpu.VMEM((bm, bn), F32)])
    if comm is None:
        return pl.pallas_call(body, name=name, compiler_params=_params(3), **call)(a, b)[0]
    (out,), received = _hosted_call(body, comm, name, operands=(a, b), **call)
    return out, received


def _adamw(items, tr, name):
    n_parts, r, c = items[0][0].shape
    n_items = len(items)

    def body(*refs):
        ins, outs = refs[:4 * n_items], refs[4 * n_items:]
        for i in range(n_items):
            p_ref, w_ref, m_ref, v_ref = ins[4 * i:4 * i + 4]
            g_ref, d_ref, nm_ref, nv_ref = outs[4 * i:4 * i + 4]
            g = p_ref[0].astype(F32)
            for j in range(1, n_parts):
                g = g + p_ref[j].astype(F32)
            g_ref[...] = g
            nm = ADAM_B1 * m_ref[...] + (1.0 - ADAM_B1) * g
            nv = ADAM_B2 * v_ref[...] + (1.0 - ADAM_B2) * (g * g)
            nm_ref[...] = nm
            nv_ref[...] = nv
            m_hat = nm / (1.0 - ADAM_B1 ** ADAM_STEP)
            v_hat = nv / (1.0 - ADAM_B2 ** ADAM_STEP)
            d_ref[...] = -ADAM_LR * (m_hat / (jnp.sqrt(v_hat) + ADAM_EPS) + ADAM_WD * w_ref[...])

    results = pl.pallas_call(
        body, name=name, grid=(r // tr,),
        in_specs=([pl.BlockSpec((n_parts, tr, c), lambda i: (0, i, 0))] + [_tile(tr, c)] * 3) * n_items,
        out_specs=[_tile(tr, c)] * (4 * n_items),
        out_shape=[_sds((r, c), F32)] * (4 * n_items),
        compiler_params=_params(),
    )(*[a for item in items for a in item])
    return [results[4 * i:4 * i + 4] for i in range(n_items)]


def _sum_parts(parts, name):
    n_parts, r, c = parts.shape

    def body(p_ref, o_ref):
        acc = p_ref[0]
        for j in range(1, n_parts):
            acc = acc + p_ref[j]
        o_ref[...] = acc

    return pl.pallas_call(
        body, name=name, grid=(1,),
        in_specs=[_whole((n_parts, r, c))], out_specs=_whole((r, c)), out_shape=_sds((r, c), F32),
        compiler_params=_params(),
    )(parts)


def _row(v):
    return v.reshape(1, -1)


def _by_owner_rows(g):
    return g.reshape(N_DEV, g.shape[0] // N_DEV, g.shape[1])


WEIGHTS = ("mix_pre_g", "w_mix_in", "conv_a_w", "conv_b_w", "conv_b_b", "ln_b_g", "ln_b_b", "w_mix_out", "mix_post_g",
           "xa_pre_g", "mem_norm_g", "w_q", "w_k", "w_v", "w_o", "xa_post_g", "ffn_pre_g", "w_gate", "w_up", "w_down",
           "ffn_post_g")
LARGE = ("w_mix_in", "w_mix_out", "w_q", "w_k", "w_v", "w_o", "w_gate", "w_up", "w_down")
COLUMN_SHARDED = ("w_mix_in", "w_gate", "w_up")
GAINS = ("mix_pre_g", "mix_post_g", "xa_pre_g", "mem_norm_g", "xa_post_g", "ffn_pre_g", "ffn_post_g")
CHANNEL_VECTORS = ("conv_b_b", "ln_b_g", "ln_b_b")
CONV_TAPS = ("conv_a_w", "conv_b_w")
SMALL = GAINS + CHANNEL_VECTORS + CONV_TAPS
CONV_COLS_PER_DEVICE = D_A // N_DEV
TOKEN_TILE = 512
FFN_FWD_TOKEN_TILE = 512
FFN_TOKEN_TILE = 256
CONV_TOKEN_TILE = 256
WGRAD_TOKEN_TILE = 2048
SMALL_WGRAD_TOKEN_TILE = 4096
IN_BWD_TOKEN_TILE = 1024
ADAM_ROWS_PER_STEP = 64


def _lane_rows(v):
    flat = v.reshape(-1)
    tile = SUBLANES * LANES
    flat = jnp.pad(flat, (0, (-flat.shape[0]) % tile))
    return flat.reshape(-1, LANES)


def _pack_small(values, names):
    return jnp.concatenate([_lane_rows(values[n]) for n in names], axis=0)


def _unpack_small(packed, like, names):
    out, off = {}, 0
    for n in names:
        size = like[n].size
        rows = _lane_rows(like[n]).shape[0]
        out[n] = packed[off:off + rows, :].reshape(-1)[:size].reshape(like[n].shape)
        off += rows
    return out


def kernel(x, mem, mix_pre_g, w_mix_in, conv_a_w, conv_b_w, conv_b_b, ln_b_g, ln_b_b, w_mix_out, mix_post_g, xa_pre_g, mem_norm_g, w_q, w_k, w_v, w_o, xa_post_g, ffn_pre_g, w_gate, w_up, w_down, ffn_post_g, loss_target, m_mix_pre_g, m_w_mix_in, m_conv_a_w, m_conv_b_w, m_conv_b_b, m_ln_b_g, m_ln_b_b, m_w_mix_out, m_mix_post_g, m_xa_pre_g, m_mem_norm_g, m_w_q, m_w_k, m_w_v, m_w_o, m_xa_post_g, m_ffn_pre_g, m_w_gate, m_w_up, m_w_down, m_ffn_post_g, v_mix_pre_g, v_w_mix_in, v_conv_a_w, v_conv_b_w, v_conv_b_b, v_ln_b_g, v_ln_b_b, v_w_mix_out, v_mix_post_g, v_xa_pre_g, v_mem_norm_g, v_w_q, v_w_k, v_w_v, v_w_o, v_xa_post_g, v_ffn_pre_g, v_w_gate, v_w_up, v_w_down, v_ffn_post_g):
    given = dict(locals())
    w = {n: given[n] for n in WEIGHTS}
    m = {n: given["m_" + n] for n in WEIGHTS}
    v = {n: given["v_" + n] for n in WEIGHTS}
    xs, mems, target = x[0], mem[0], loss_target[0]
    t = xs.shape[0]
    tm, tm_ffn, tc, tk = min(TOKEN_TILE, t), min(FFN_TOKEN_TILE, t), min(CONV_TOKEN_TILE, t), min(WGRAD_TOKEN_TILE, t)
    tk_small = min(SMALL_WGRAD_TOKEN_TILE, t)
    g = {n: _row(w[n]) for n in GAINS}
    bb, lg, lb = (_row(w[n]) for n in CHANNEL_VECTORS)

    def shard_bf16(*names):
        return [w[n].astype(BF16) for n in names]

    def shard_bf16_t(*names):
        return [w[n].T.astype(BF16) for n in names]

    g_mix_in, g_taps = _comm_call(_Gather(shard_bf16_t("w_mix_in") + [_pack_small(w, CONV_TAPS)]), "gather_mixer")
    w_mix_in_t = g_mix_in.reshape(D_IN_ALL, D_MODEL)
    taps, off = {}, 0
    for n in CONV_TAPS:
        k, cols = w[n].shape
        rows = _lane_rows(w[n]).shape[0]
        blk = g_taps[:, off:off + rows, :].reshape(N_DEV, -1)[:, :k * cols].reshape(N_DEV, k, cols)
        taps[n] = blk.transpose(1, 0, 2).reshape(k, N_DEV * cols)
        off += rows
    wa, wb = taps["conv_a_w"], taps["conv_b_w"]

    (h1, u), gathered = _norm_matmul(xs, g["mix_pre_g"], w_mix_in_t.T, tm, "mix_in_fwd",
                                     _Gather(shard_bf16("w_mix_out", "w_q", "w_k", "w_v", "w_o")))
    w_mix_out, w_q, w_k, w_v, w_o = (a.reshape(D_MODEL, D_MODEL) for a in gathered)
    (ycat, zc, ca, x1, s1), (g_gate, g_up, g_down) = _conv_fwd(
        u, wa, wb, bb, lg, lb, xs, w_mix_out, g["mix_post_g"], tc, "conv_fwd",
        _Gather(shard_bf16_t("w_gate", "w_up") + shard_bf16("w_down")))
    w_gate_t, w_up_t, w_down = (a.reshape(D_FF, D_MODEL) for a in (g_gate, g_up, g_down))
    mem_n, kk, vv = _mem_fwd(mems, g["mem_norm_g"], w_k, w_v, "mem_fwd")
    x2, h2, q, o, a = _attn_fwd(x1, g["xa_pre_g"], w_q, kk, vv, w_o, g["xa_post_g"], tm, "attn_fwd")
    h3, gt, up, f, sq = _ffn_fwd(x2, g["ffn_pre_g"], w_gate_t, w_up_t, w_down, g["ffn_post_g"], target,
                                 min(FFN_FWD_TOKEN_TILE, t), "ffn_fwd")

    dx2, df, hd, dgt, dup, d_ffn_post, d_ffn_pre = _ffn_bwd(
        x2, f, target, gt, up, g["ffn_pre_g"], w_gate_t, w_up_t, w_down, g["ffn_post_g"], tm_ffn, "ffn_bwd")
    d_w_down = _wgrad(hd, df, tk_small, D_FF // 2, D_MODEL // 2, "wgrad_down")
    d_w_gate_t = _wgrad(dgt, h3, tk_small, D_FF // 2, D_MODEL // 2, "wgrad_gate")
    d_w_up_t = _wgrad(dup, h3, tk_small, D_FF // 2, D_MODEL // 2, "wgrad_up")
    ffn_slabs = [_by_owner_rows(d) for d in (d_w_gate_t, d_w_up_t, d_w_down)]

    (dx1, da, dq, dk, dv, d_xa_post, d_xa_pre, ds1, dycat, d_mix_post), from_ffn = _attn_bwd(
        dx2, a, x1, q, kk, vv, g["xa_pre_g"], w_q, w_o, g["xa_post_g"], s1, w_mix_out, g["mix_post_g"], tm, "attn_bwd",
        _Exchange(ffn_slabs, [True] * 3))
    d_w_o = _wgrad(o, da, tk, D_MODEL, D_MODEL, "wgrad_o")
    d_w_q = _wgrad(h2, dq, tk, D_MODEL, D_MODEL, "wgrad_q")
    d_w_k, d_w_v, d_mem_norm = _mem_bwd(mems, mem_n, dk, dv, g["mem_norm_g"], w_k, w_v, "mem_bwd")
    d_w_mix_out = _wgrad(ycat, ds1, tk, D_MODEL, D_MODEL, "wgrad_mix_out")
    attn_slabs = [_by_owner_rows(d) for d in (d_w_mix_out, d_w_q, d_w_k, d_w_v, d_w_o)]

    (du, d_conv_a, d_conv_b, d_conv_bb, d_ln_g, d_ln_b), from_attn = _conv_bwd(
        dycat, u, zc, ca, wa, wb, lg, lb, tc, "conv_bwd", _Exchange(attn_slabs, [True] * 5))
    small_grads = dict(conv_a_w=d_conv_a, conv_b_w=d_conv_b, conv_b_b=d_conv_bb, ln_b_g=d_ln_g, ln_b_b=d_ln_b,
                       mix_post_g=d_mix_post, xa_pre_g=d_xa_pre, mem_norm_g=d_mem_norm, xa_post_g=d_xa_post,
                       ffn_pre_g=d_ffn_pre, ffn_post_g=d_ffn_post, loss=sq)
    names = tuple(n for n in SMALL if n != "mix_pre_g") + ("loss", "mix_pre_g")
    half = D_MODEL // 2
    d_in_lo = _wgrad(du, h1, tk_small, D_IN_ALL // 2, half, "wgrad_mix_in_lo", cols=(0, half))
    d_in_hi, (from_in_lo, early_small) = _wgrad(
        du, h1, tk_small, D_IN_ALL // 2, half, "wgrad_mix_in_hi",
        _Exchange([_by_owner_rows(d_in_lo), _pack_small(small_grads, names[:-1])], [True, False]), cols=(half, half))
    dx, d_mix_pre = _in_bwd(du, w_mix_in_t, xs, dx1, g["mix_pre_g"], min(IN_BWD_TOKEN_TILE, t), "mix_in_bwd")
    small_grads["mix_pre_g"] = d_mix_pre
    from_in_hi, late_small = _comm_call(
        _Exchange([_by_owner_rows(d_in_hi), _lane_rows(d_mix_pre)], [True, False]), "reduce_tail")
    all_small = jnp.concatenate([early_small, late_small], axis=1)

    received = dict(zip(("w_gate", "w_up", "w_down"), from_ffn))
    received.update(zip(("w_mix_out", "w_q", "w_k", "w_v", "w_o"), from_attn))
    received["w_mix_in"] = jnp.concatenate([from_in_lo, from_in_hi], axis=2)
    grad, delta, new_m, new_v = {}, {}, {}, {}
    for group in (("w_mix_in",), ("w_gate",), ("w_up",), ("w_down",), ("w_mix_out", "w_q", "w_k", "w_v", "w_o")):
        items = [[received[n]] + [a[n].T if n in COLUMN_SHARDED else a[n] for a in (w, m, v)] for n in group]
        rows = items[0][1].shape[0]
        results = _adamw(items, min(rows, ADAM_ROWS_PER_STEP) if len(group) > 1 else rows, "adamw_" + group[0])
        for n, result in zip(group, results):
            grad[n], delta[n], new_m[n], new_v[n] = [r.T if n in COLUMN_SHARDED else r for r in result]

    total = _unpack_small(_sum_parts(all_small, "sum_small"), small_grads, names)
    loss = jnp.sum(total.pop("loss")) * (0.5 / D_MODEL)
    first_col = _device_index() * CONV_COLS_PER_DEVICE
    for n in CONV_TAPS:
        total[n] = lax.dynamic_slice_in_dim(total[n], first_col, CONV_COLS_PER_DEVICE, axis=1)
    total = {n: total[n].reshape(w[n].shape) for n in SMALL}
    packed_small = [_pack_small(values, SMALL) for values in (total, w, m, v)]
    ((g_s, d_s, nm_s, nv_s),) = _adamw([[packed_small[0][None]] + packed_small[1:]], packed_small[0].shape[0],
                                       "adamw_small")
    for out, p in ((grad, g_s), (delta, d_s), (new_m, nm_s), (new_v, nv_s)):
        out.update(_unpack_small(p, w, SMALL))

    return (loss, dx[None], *[grad[n] for n in WEIGHTS], *[delta[n] for n in WEIGHTS], *[new_m[n] for n in WEIGHTS],
            *[new_v[n] for n in WEIGHTS])
```

```python
import jax
import jax.numpy as jnp
from jax import lax
from jax.experimental import pallas as pl
from jax.experimental.pallas import tpu as pltpu

F32 = jnp.float32
BF16 = jnp.bfloat16

D_MODEL = 1024
D_A = 512
D_B = 512
D_IN_ALL = 2560
CONV_A_W = 3
CONV_B_W = 31
XA_HEADS = 4
XA_HEAD_DIM = 256
D_FF = 2816
N_DEV = 8
RMS_EPS = 1e-6
LN_EPS = 1e-5
ADAM_LR = 0.001
ADAM_B1 = 0.9
ADAM_B2 = 0.999
ADAM_EPS = 1e-08
ADAM_WD = 0.01
ADAM_STEP = 10

VMEM_LIMIT_BYTES = 56 * 1024 * 1024
SUBLANES = 8
LANES = 128
HALO_B = 32
HALO_A = 8
CONV_FWD_CHUNK = 16
CONV_CHUNK = 32
GATHER_FORWARD_STEPS_BEFORE_END = 8

MESH = pl.DeviceIdType.MESH


def _params(n_grid_axes=1):
    return pltpu.CompilerParams(dimension_semantics=("arbitrary",) * n_grid_axes, vmem_limit_bytes=VMEM_LIMIT_BYTES)


def _sds(shape, dtype):
    return jax.ShapeDtypeStruct(shape, dtype)


def _tile(rows, cols):
    return pl.BlockSpec((rows, cols), lambda i: (i, 0))


def _rtile(rows, cols, n):
    return pl.BlockSpec((rows, cols), lambda i: (n - 1 - i, 0))


def _whole(shape):
    zeros = (0,) * len(shape)
    return pl.BlockSpec(shape, lambda i: zeros)


def _resident(shape):
    zeros = (0,) * len(shape)
    return pl.BlockSpec(shape, lambda i: zeros, pipeline_mode=pl.Buffered(1))


def _dot(a, b):
    return jnp.dot(a, b, preferred_element_type=F32)


def _dot_nt(a, b):
    return lax.dot_general(a, b, (((1,), (1,)), ((), ())), preferred_element_type=F32)


def _dot_tn(a, b):
    return lax.dot_general(a, b, (((0,), (0,)), ((), ())), preferred_element_type=F32)


def _sigmoid(x):
    return 1.0 / (1.0 + jnp.exp(-x))


def _rms_fwd(x, g):
    r = lax.rsqrt(jnp.mean(x * x, axis=-1, keepdims=True) + RMS_EPS)
    return x * r * g


def _rms_bwd(dy, xin, g):
    r = lax.rsqrt(jnp.mean(xin * xin, axis=-1, keepdims=True) + RMS_EPS)
    n = xin * r
    dg = jnp.sum(dy * n, axis=0, keepdims=True)
    dn = dy * g
    dx = r * (dn - n * jnp.mean(dn * n, axis=-1, keepdims=True))
    return dx, dg


def _zero_at_first_step(*refs):
    @pl.when(pl.program_id(0) == 0)
    def _():
        for ref in refs:
            ref[...] = jnp.zeros(ref.shape, ref.dtype)


def _place():
    return lax.axis_index("x"), lax.axis_index("y"), lax.axis_index("c")


def _device_index():
    x, y, c = _place()
    return 4 * x + 2 * y + c


class _Gather:
    def __init__(self, arrays):
        self.arrays = list(arrays)
        self.out_shape = [_sds((N_DEV, *a.shape), a.dtype) for a in self.arrays]
        n = len(self.arrays)
        self.scratch_shapes = [pltpu.SemaphoreType.DMA((n, 7)), pltpu.SemaphoreType.DMA((n, 7)),
                               pltpu.SemaphoreType.DMA((n,))]

    def forward_step(self, n_steps):
        return max(0, n_steps - 1 - GATHER_FORWARD_STEPS_BEFORE_END)

    def bind(self, srcs, dsts, send_sems, recv_sems, local_sems):
        x, y, cc = _place()
        me, sibling = (x, y, cc), (x, y, 1 - cc)
        chips = [(1 - x, y), (x, 1 - y), (1 - x, 1 - y)]

        def copy(a, k, owner, to, src=None):
            slot = dsts[a].at[4 * owner[0] + 2 * owner[1] + owner[2]]
            return pltpu.make_async_remote_copy(
                src_ref=slot if src is None else src, dst_ref=slot, send_sem=send_sems.at[a, k],
                recv_sem=recv_sems.at[a, k], device_id=to, device_id_type=MESH)

        def first(a):
            return [copy(a, 0, me, sibling, src=srcs[a])] + [
                copy(a, 1 + j, me, (*chip, cc), src=srcs[a]) for j, chip in enumerate(chips)]

        def passed(a, j):
            return copy(a, 4 + j, (*chips[j], cc), sibling)

        def mine(a):
            return pltpu.make_async_copy(srcs[a], dsts[a].at[4 * x + 2 * y + cc], local_sems.at[a])

        def start():
            for a in range(len(srcs)):
                mine(a).start()
                for cp in first(a):
                    cp.start()

        def forward():
            for a in range(len(srcs)):
                for j, chip in enumerate(chips):
                    copy(a, 1 + j, (*chip, cc), me).wait_recv()
                    passed(a, j).start()

        def finish():
            for a in range(len(srcs)):
                copy(a, 0, sibling, me).wait_recv()
                for j, chip in enumerate(chips):
                    copy(a, 4 + j, (*chip, 1 - cc), me).wait_recv()
                for cp in first(a) + [passed(a, j) for j in range(len(chips))]:
                    cp.wait_send()
                mine(a).wait()

        return start, forward, finish


class _Exchange:
    def __init__(self, arrays, scatter):
        self.arrays = list(arrays)
        self.scatter = list(scatter)
        self.out_shape = [_sds(a.shape if s else (N_DEV, *a.shape), a.dtype) for a, s in zip(self.arrays, self.scatter)]
        n = len(self.arrays)
        self.scratch_shapes = [pltpu.SemaphoreType.DMA((n, 7)), pltpu.SemaphoreType.DMA((n, 7)),
                               pltpu.SemaphoreType.DMA((n,))]

    def forward_step(self, n_steps):
        return n_steps - 1

    def bind(self, srcs, dsts, send_sems, recv_sems, local_sems):
        me = _device_index()

        def copies(a):
            out = []
            for k in range(1, N_DEV):
                p = me ^ k
                out.append(pltpu.make_async_remote_copy(
                    src_ref=srcs[a].at[p] if self.scatter[a] else srcs[a], dst_ref=dsts[a].at[me],
                    send_sem=send_sems.at[a, k - 1], recv_sem=recv_sems.at[a, k - 1],
                    device_id=(p >> 2, (p >> 1) & 1, p & 1), device_id_type=MESH))
            return out

        def mine(a):
            return pltpu.make_async_copy(srcs[a].at[me] if self.scatter[a] else srcs[a], dsts[a].at[me], local_sems.at[a])

        def start():
            for a in range(len(srcs)):
                mine(a).start()
                for cp in copies(a):
                    cp.start()

        def forward():
            pass

        def finish():
            for a in range(len(srcs)):
                for cp in copies(a):
                    cp.wait()
                mine(a).wait()

        return start, forward, finish


def _hosted_call(core, comm, name, grid, in_specs, out_specs, out_shape, scratch_shapes, operands):
    grid = (grid,) if isinstance(grid, int) else tuple(grid)
    n_steps = 1
    for extent in grid:
        n_steps *= extent
    n_in, n_out, n_scr, n_arr = len(in_specs), len(out_specs), len(scratch_shapes), len(comm.arrays)
    any_spec = pl.BlockSpec(memory_space=pl.ANY)

    def body(*refs):
        ins, refs = refs[:n_in], refs[n_in:]
        srcs, refs = refs[:n_arr], refs[n_arr:]
        outs, refs = refs[:n_out], refs[n_out:]
        dsts, refs = refs[:n_arr], refs[n_arr:]
        scratch, sems = refs[:n_scr], refs[n_scr:]
        start, forward, finish = comm.bind(srcs, dsts, *sems)
        step = pl.program_id(0)
        for axis in range(1, len(grid)):
            step = step * grid[axis] + pl.program_id(axis)
        pl.when(step == 0)(start)
        core(*ins, *outs, *scratch)
        pl.when(step == comm.forward_step(n_steps))(forward)
        pl.when(step == n_steps - 1)(finish)

    results = pl.pallas_call(
        body, name=name, grid=grid,
        in_specs=list(in_specs) + [any_spec] * n_arr,
        out_specs=list(out_specs) + [any_spec] * n_arr,
        out_shape=list(out_shape) + comm.out_shape,
        scratch_shapes=list(scratch_shapes) + comm.scratch_shapes,
        compiler_params=_params(len(grid)),
    )(*operands, *comm.arrays)
    return results[:n_out], results[n_out:]


def _comm_call(comm, name):
    return _hosted_call(lambda: None, comm, name, 1, [], [], [], [], [])[1]


def _split_exchange_copies(srcs, lands, scatter, send_sems, recv_sems):
    me = _device_index()
    out = []
    for a in range(len(srcs)):
        for k in range(1, N_DEV):
            p = me ^ k
            out.append(pltpu.make_async_remote_copy(
                src_ref=srcs[a].at[p] if scatter[a] else srcs[a], dst_ref=lands[a].at[me],
                send_sem=send_sems[a].at[k - 1], recv_sem=recv_sems[a].at[k - 1],
                device_id=(p >> 2, (p >> 1) & 1, p & 1), device_id_type=MESH))
    return out


def _exchange_start(arrays, scatter, name):
    n = len(arrays)
    me = _device_index()
    lands = []
    for a, s in zip(arrays, scatter):
        own = lax.dynamic_slice_in_dim(a, me, 1, axis=0) if s else a[None]
        empty = jnp.zeros(a.shape if s else (N_DEV, *a.shape), a.dtype)
        lands.append(lax.dynamic_update_slice_in_dim(empty, own, me, axis=0))

    def body(*refs):
        srcs, zones = refs[:n], refs[n:2 * n]
        send_sems, recv_sems = refs[2 * n:3 * n], refs[3 * n:4 * n]
        for cp in _split_exchange_copies(srcs, zones, scatter, send_sems, recv_sems):
            cp.start()
        refs[-1][...] = jnp.zeros(refs[-1].shape, F32)

    hbm = pl.BlockSpec(memory_space=pltpu.HBM)
    sem = pl.BlockSpec(memory_space=pltpu.SEMAPHORE)
    operands = [pltpu.with_memory_space_constraint(a, pltpu.HBM) for a in list(arrays) + lands]
    results = pl.pallas_call(
        body, name=name,
        out_shape=(*[pltpu.SemaphoreType.DMA((N_DEV - 1,))] * (2 * n), *[pltpu.HBM(a.shape, a.dtype) for a in operands],
                   _sds((SUBLANES, LANES), F32)),
        in_specs=[hbm] * (2 * n),
        out_specs=(*[sem] * (2 * n), *[hbm] * (2 * n), pl.BlockSpec(memory_space=pltpu.VMEM)),
        input_output_aliases={i: 2 * n + i for i in range(2 * n)},
        compiler_params=pltpu.CompilerParams(has_side_effects=pltpu.SideEffectType.DATAFLOW_SIDE_EFFECTING),
    )(*operands)
    return results[:n], results[n:2 * n], results[2 * n:3 * n], results[3 * n:4 * n]


def _exchange_wait(started, scatter, after, name):
    send_sems, recv_sems, srcs, lands = started
    n = len(srcs)

    def body(*refs):
        src_refs, zones = refs[:n], refs[n:2 * n]
        send, recv = refs[2 * n:3 * n], refs[3 * n:4 * n]
        for cp in _split_exchange_copies(src_refs, zones, scatter, send, recv):
            cp.wait_send()
            cp.wait_recv()

    hbm = pl.BlockSpec(memory_space=pltpu.HBM)
    sem = pl.BlockSpec(memory_space=pltpu.SEMAPHORE)
    results = pl.pallas_call(
        body, name=name,
        out_shape=tuple(pltpu.HBM(a.shape, a.dtype) for a in list(srcs) + list(lands)),
        in_specs=[hbm] * (2 * n) + [sem] * (2 * n) + [pl.BlockSpec(memory_space=pl.ANY)],
        out_specs=tuple([hbm] * (2 * n)),
        input_output_aliases={i: i for i in range(2 * n)},
        compiler_params=pltpu.CompilerParams(has_side_effects=pltpu.SideEffectType.DATAFLOW_SIDE_EFFECTING),
    )(*srcs, *lands, *send_sems, *recv_sems, after)
    return results[n:]


def _norm_matmul(x, g, w, tm, name, comm):
    t, d = x.shape
    n = w.shape[1]

    def core(x_ref, g_ref, w_ref, h_ref, o_ref):
        h = _rms_fwd(x_ref[...], g_ref[...]).astype(BF16)
        h_ref[...] = h
        o_ref[...] = _dot(h, w_ref[...]).astype(BF16)

    return _hosted_call(
        core, comm, name, t // tm,
        in_specs=[_tile(tm, d), _whole((1, d)), _resident((d, n))],
        out_specs=[_tile(tm, d), _tile(tm, n)],
        out_shape=[_sds((t, d), BF16), _sds((t, n), BF16)],
        scratch_shapes=[], operands=(x, g, w))


def _conv_fwd(u, wa, wb, bb, lg, lb, xres, w_out, g_post, tc, name, comm):
    t = u.shape[0]
    d = w_out.shape[1]
    chunk = CONV_FWD_CHUNK
    n_chunks = tc // chunk
    piece = 2 * LANES

    def core(u_ref, wa_ref, wb_ref, bb_ref, lg_ref, lb_ref, x_ref, wo_ref, go_ref,
             y_ref, zc_ref, ca_ref, x1_ref, s_ref, zbuf, cvbuf, zcbuf):
        @pl.when(pl.program_id(0) == 0)
        def _():
            zbuf[:, 0:HALO_B, :] = jnp.zeros((D_B // LANES, HALO_B, LANES), F32)
            cvbuf[:, 0:HALO_A, :] = jnp.zeros((D_A // LANES, HALO_A, LANES), F32)

        s = None
        for lb_i in range(D_A // LANES):
            if lb_i > 0 and lb_i % 2 == 0:
                cols = slice((lb_i - 2) * LANES, lb_i * LANES)
                part = _dot(y_ref[:, cols], wo_ref[cols, :])
                s = part if s is None else s + part
            lanes = slice(lb_i * LANES, (lb_i + 1) * LANES)
            c_a = u_ref[:, D_A + lb_i * LANES:D_A + (lb_i + 1) * LANES].astype(F32)
            v_a = u_ref[:, 2 * D_A + lb_i * LANES:2 * D_A + (lb_i + 1) * LANES].astype(F32)
            cvbuf[lb_i, HALO_A:HALO_A + tc, :] = c_a * v_a
            glu_v = u_ref[:, 3 * D_A + lb_i * LANES:3 * D_A + (lb_i + 1) * LANES].astype(F32)
            glu_g = u_ref[:, 3 * D_A + D_B + lb_i * LANES:3 * D_A + D_B + (lb_i + 1) * LANES].astype(F32)
            zbuf[lb_i, HALO_B:HALO_B + tc, :] = glu_v * _sigmoid(glu_g)

            for c in range(n_chunks):
                r0 = c * chunk
                rows = slice(r0, r0 + chunk)
                acc = jnp.zeros((chunk, LANES), F32)
                for k in range(CONV_A_W):
                    off = r0 + HALO_A - (CONV_A_W - 1) + k
                    acc = acc + wa_ref[k:k + 1, lanes] * cvbuf[lb_i, off:off + chunk, :]
                ca_ref[rows, lanes] = acc.astype(BF16)
                y_ref[rows, lanes] = (u_ref[rows, lanes].astype(F32) * acc).astype(BF16)

                accb = jnp.zeros((chunk, LANES), F32)
                for k in range(CONV_B_W):
                    off = r0 + HALO_B - (CONV_B_W - 1) + k
                    accb = accb + wb_ref[k:k + 1, lanes] * zbuf[lb_i, off:off + chunk, :]
                zcbuf[rows, lanes] = accb + bb_ref[:, lanes]

            zbuf[lb_i, 0:HALO_B, :] = zbuf[lb_i, tc:tc + HALO_B, :]
            cvbuf[lb_i, 0:HALO_A, :] = cvbuf[lb_i, tc:tc + HALO_A, :]

        cols = slice(D_A - piece, D_A)
        s = s + _dot(y_ref[:, cols], wo_ref[cols, :])

        for c in range(n_chunks):
            rows = slice(c * chunk, (c + 1) * chunk)
            zc = zcbuf[rows, :]
            zc_ref[rows, :] = zc.astype(BF16)
            mu = jnp.mean(zc, axis=-1, keepdims=True)
            xc = zc - mu
            var = jnp.mean(xc * xc, axis=-1, keepdims=True)
            ln = xc * lax.rsqrt(var + LN_EPS) * lg_ref[...] + lb_ref[...]
            y_ref[rows, D_A:D_A + D_B] = (ln * _sigmoid(ln)).astype(BF16)

        s = s + _dot(y_ref[:, D_A:D_A + D_B], wo_ref[D_A:D_A + D_B, :])
        s_ref[...] = s.astype(BF16)
        x1_ref[...] = x_ref[...] + _rms_fwd(s, go_ref[...])

    return _hosted_call(
        core, comm, name, t // tc,
        in_specs=[_tile(tc, D_IN_ALL), _whole((CONV_A_W, D_A)), _whole((CONV_B_W, D_B)), _whole((1, D_B)),
                  _whole((1, D_B)), _whole((1, D_B)), _tile(tc, d), _resident((D_A + D_B, d)), _whole((1, d))],
        out_specs=[_tile(tc, D_A + D_B), _tile(tc, D_B), _tile(tc, D_A), _tile(tc, d), _tile(tc, d)],
        out_shape=[_sds((t, D_A + D_B), BF16), _sds((t, D_B), BF16), _sds((t, D_A), BF16), _sds((t, d), F32),
                   _sds((t, d), BF16)],
        scratch_shapes=[pltpu.VMEM((D_B // LANES, HALO_B + tc, LANES), F32),
                        pltpu.VMEM((D_A // LANES, HALO_A + tc, LANES), F32), pltpu.VMEM((tc, D_B), F32)],
        operands=(u, wa, wb, bb, lg, lb, xres, w_out, g_post))


def _mem_fwd(mem, g, wk, wv, name):
    m, d = mem.shape

    def body(mem_ref, g_ref, wk_ref, wv_ref, n_ref, k_ref, v_ref):
        n = _rms_fwd(mem_ref[...], g_ref[...]).astype(BF16)
        n_ref[...] = n
        k_ref[...] = _dot(n, wk_ref[...]).astype(BF16)
        v_ref[...] = _dot(n, wv_ref[...]).astype(BF16)

    return pl.pallas_call(
        body, name=name, grid=(1,),
        in_specs=[_whole((m, d)), _whole((1, d)), _whole((d, d)), _whole((d, d))],
        out_specs=[_whole((m, d))] * 3,
        out_shape=[_sds((m, d), BF16)] * 3,
        compiler_params=_params(),
    )(mem, g, wk, wv)


def _softmax_rows(s):
    e = jnp.exp(s - jnp.max(s, axis=-1, keepdims=True))
    return e / jnp.sum(e, axis=-1, keepdims=True)


def _attn_fwd(x1, g_pre, wq, k, v, wo, g_post, tm, name):
    t, d = x1.shape
    m = k.shape[0]
    scale = XA_HEAD_DIM ** -0.5

    def body(x_ref, gp_ref, wq_ref, k_ref, v_ref, wo_ref, go_ref, x2_ref, h_ref, q_ref, o_ref, a_ref):
        x = x_ref[...]
        h = _rms_fwd(x, gp_ref[...]).astype(BF16)
        h_ref[...] = h
        q_ref[...] = _dot(h, wq_ref[...]).astype(BF16)
        for hd in range(XA_HEADS):
            cols = slice(hd * XA_HEAD_DIM, (hd + 1) * XA_HEAD_DIM)
            p = _softmax_rows(_dot_nt(q_ref[:, cols], k_ref[:, cols]) * scale)
            o_ref[:, cols] = _dot(p.astype(BF16), v_ref[:, cols]).astype(BF16)
        a = _dot(o_ref[...], wo_ref[...])
        a_ref[...] = a.astype(BF16)
        x2_ref[...] = x + _rms_fwd(a, go_ref[...])

    return pl.pallas_call(
        body, name=name, grid=(t // tm,),
        in_specs=[_tile(tm, d), _whole((1, d)), _resident((d, d)), _whole((m, d)), _whole((m, d)), _resident((d, d)),
                  _whole((1, d))],
        out_specs=[_tile(tm, d)] * 5,
        out_shape=[_sds((t, d), F32)] + [_sds((t, d), BF16)] * 4,
        compiler_params=_params(),
    )(x1, g_pre, wq, k, v, wo, g_post)


def _ffn_fwd(x2, g_pre, wg_t, wu_t, wd, g_post, target, tm, name):
    t, d = x2.shape
    f = wg_t.shape[0]

    def body(x_ref, gp_ref, wg_ref, wu_ref, wd_ref, go_ref, tgt_ref, h_ref, gt_ref, up_ref, f_ref, sq_ref):
        _zero_at_first_step(sq_ref)
        x = x_ref[...]
        h = _rms_fwd(x, gp_ref[...]).astype(BF16)
        h_ref[...] = h
        gt = _dot_nt(h, wg_ref[...])
        up = _dot_nt(h, wu_ref[...])
        gt_ref[...] = gt.astype(BF16)
        up_ref[...] = up.astype(BF16)
        hd = (gt * _sigmoid(gt) * up).astype(BF16)
        ff = _dot(hd, wd_ref[...])
        f_ref[...] = ff.astype(BF16)
        err = x + _rms_fwd(ff, go_ref[...]) - tgt_ref[...]
        sq_ref[...] += jnp.sum(err * err, axis=0, keepdims=True)

    return pl.pallas_call(
        body, name=name, grid=(t // tm,),
        in_specs=[_tile(tm, d), _whole((1, d)), _resident((f, d)), _resident((f, d)), _resident((f, d)), _whole((1, d)),
                  _tile(tm, d)],
        out_specs=[_tile(tm, d), _tile(tm, f), _tile(tm, f), _tile(tm, d), _whole((1, d))],
        out_shape=[_sds((t, d), BF16), _sds((t, f), BF16), _sds((t, f), BF16), _sds((t, d), BF16), _sds((1, d), F32)],
        compiler_params=_params(),
    )(x2, g_pre, wg_t, wu_t, wd, g_post, target)


def _ffn_bwd(x2, f, target, gt, up, g_pre, wg_t, wu_t, wd, g_post, tm, name):
    t, d = x2.shape
    ff = wg_t.shape[0]

    def body(x_ref, f_ref, tgt_ref, gt_ref, up_ref, gp_ref, wg_ref, wu_ref, wd_ref, go_ref,
             dx_ref, df_ref, hd_ref, dgt_ref, dup_ref, dgo_ref, dgp_ref):
        _zero_at_first_step(dgo_ref, dgp_ref)
        x = x_ref[...]
        fo = f_ref[...].astype(F32)
        dx3 = (x + _rms_fwd(fo, go_ref[...]) - tgt_ref[...]) * (1.0 / d)
        df, dgo = _rms_bwd(dx3, fo, go_ref[...])
        dgo_ref[...] += dgo
        df = df.astype(BF16)
        df_ref[...] = df
        dhd = _dot_nt(df, wd_ref[...])
        gt = gt_ref[...].astype(F32)
        up = up_ref[...].astype(F32)
        sg = _sigmoid(gt)
        si = gt * sg
        hd_ref[...] = (si * up).astype(BF16)
        dup = (dhd * si).astype(BF16)
        dgt = (dhd * up * (sg * (1.0 + gt * (1.0 - sg)))).astype(BF16)
        dup_ref[...] = dup
        dgt_ref[...] = dgt
        dh = _dot(dgt, wg_ref[...]) + _dot(dup, wu_ref[...])
        dxn, dgp = _rms_bwd(dh, x, gp_ref[...])
        dgp_ref[...] += dgp
        dx_ref[...] = dx3 + dxn

    return pl.pallas_call(
        body, name=name, grid=(t // tm,),
        in_specs=[_tile(tm, d), _tile(tm, d), _tile(tm, d), _tile(tm, ff), _tile(tm, ff), _whole((1, d)),
                  _resident((ff, d)), _resident((ff, d)), _resident((ff, d)), _whole((1, d))],
        out_specs=[_tile(tm, d), _tile(tm, d), _tile(tm, ff), _tile(tm, ff), _tile(tm, ff), _whole((1, d)), _whole((1, d))],
        out_shape=[_sds((t, d), F32), _sds((t, d), BF16), _sds((t, ff), BF16), _sds((t, ff), BF16), _sds((t, ff), BF16),
                   _sds((1, d), F32), _sds((1, d), F32)],
        compiler_params=_params(),
    )(x2, f, target, gt, up, g_pre, wg_t, wu_t, wd, g_post)


def _attn_bwd(dx2, a, x1, q, k, v, g_pre, wq, wo, g_post, s_mix, w_mix_out, g_mix_post, tm, name, comm):
    t, d = x1.shape
    m = k.shape[0]
    kdim = w_mix_out.shape[0]
    scale = XA_HEAD_DIM ** -0.5

    def core(dx2_ref, a_ref, x_ref, q_ref, k_ref, v_ref, gp_ref, wq_ref, wo_ref, go_ref, s_ref, wm_ref, gm_ref,
             dx1_ref, da_ref, dq_ref, dk_ref, dv_ref, dgo_ref, dgp_ref, ds_ref, dy_ref, dgm_ref, do_buf):
        _zero_at_first_step(dgo_ref, dgp_ref, dk_ref, dv_ref, dgm_ref)
        dx2 = dx2_ref[...]
        da, dgo = _rms_bwd(dx2, a_ref[...].astype(F32), go_ref[...])
        dgo_ref[...] += dgo
        da = da.astype(BF16)
        da_ref[...] = da
        do_buf[...] = _dot_nt(da, wo_ref[...]).astype(BF16)
        for hd in range(XA_HEADS):
            cols = slice(hd * XA_HEAD_DIM, (hd + 1) * XA_HEAD_DIM)
            qh = q_ref[:, cols]
            p = _softmax_rows(_dot_nt(qh, k_ref[:, cols]) * scale)
            do_h = do_buf[:, cols]
            dp = _dot_nt(do_h, v_ref[:, cols])
            dv_ref[:, cols] += _dot_tn(p.astype(BF16), do_h)
            ds = (p * (dp - jnp.sum(dp * p, axis=-1, keepdims=True)) * scale).astype(BF16)
            dq_ref[:, cols] = _dot(ds, k_ref[:, cols]).astype(BF16)
            dk_ref[:, cols] += _dot_tn(ds, qh)
        dh = _dot_nt(dq_ref[...], wq_ref[...])
        dxn, dgp = _rms_bwd(dh, x_ref[...], gp_ref[...])
        dgp_ref[...] += dgp
        dx1 = dx2 + dxn
        dx1_ref[...] = dx1
        ds, dgm = _rms_bwd(dx1, s_ref[...].astype(F32), gm_ref[...])
        dgm_ref[...] += dgm
        ds = ds.astype(BF16)
        ds_ref[...] = ds
        dy_ref[...] = _dot_nt(ds, wm_ref[...]).astype(BF16)

    return _hosted_call(
        core, comm, name, t // tm,
        in_specs=[_tile(tm, d), _tile(tm, d), _tile(tm, d), _tile(tm, d), _whole((m, d)), _whole((m, d)), _whole((1, d)),
                  _resident((d, d)), _resident((d, d)), _whole((1, d)), _tile(tm, d), _resident((kdim, d)), _whole((1, d))],
        out_specs=[_tile(tm, d), _tile(tm, d), _tile(tm, d), _whole((m, d)), _whole((m, d)), _whole((1, d)), _whole((1, d)),
                   _tile(tm, d), _tile(tm, kdim), _whole((1, d))],
        out_shape=[_sds((t, d), F32), _sds((t, d), BF16), _sds((t, d), BF16), _sds((m, d), F32), _sds((m, d), F32),
                   _sds((1, d), F32), _sds((1, d), F32), _sds((t, d), BF16), _sds((t, kdim), BF16), _sds((1, d), F32)],
        scratch_shapes=[pltpu.VMEM((tm, d), BF16)],
        operands=(dx2, a, x1, q, k, v, g_pre, wq, wo, g_post, s_mix, w_mix_out, g_mix_post))


def _mem_bwd(mem, mem_n, dk, dv, g, wk, wv, name):
    m, d = mem.shape

    def body(mem_ref, n_ref, dk_ref, dv_ref, g_ref, wk_ref, wv_ref, dwk_ref, dwv_ref, dg_ref):
        dk = dk_ref[...].astype(BF16)
        dv = dv_ref[...].astype(BF16)
        n = n_ref[...]
        dwk_ref[...] = _dot_tn(n, dk).astype(BF16)
        dwv_ref[...] = _dot_tn(n, dv).astype(BF16)
        dn = _dot_nt(dk, wk_ref[...]) + _dot_nt(dv, wv_ref[...])
        _, dg = _rms_bwd(dn, mem_ref[...], g_ref[...])
        dg_ref[...] = dg

    return pl.pallas_call(
        body, name=name, grid=(1,),
        in_specs=[_whole((m, d)), _whole((m, d)), _whole((m, d)), _whole((m, d)), _whole((1, d)), _whole((d, d)),
                  _whole((d, d))],
        out_specs=[_whole((d, d)), _whole((d, d)), _whole((1, d))],
        out_shape=[_sds((d, d), BF16), _sds((d, d), BF16), _sds((1, d), F32)],
        compiler_params=_params(),
    )(mem, mem_n, dk, dv, g, wk, wv)


def _conv_bwd(dy, u, zc, ca, wa, wb, lg, lb, tc, name, comm):
    t = u.shape[0]
    n_tiles = t // tc
    n_chunks = tc // CONV_CHUNK

    def core(dy_ref, u_ref, zc_ref, ca_ref, wa_ref, wb_ref, lg_ref, lb_ref,
             du_ref, dwa_ref, dwb_ref, dbb_ref, dlg_ref, dlb_ref, ebuf, eabuf, zbuf, cvbuf, gatebuf, wacc, aacc, vacc):
        step = pl.program_id(0)

        @pl.when(step == 0)
        def _():
            ebuf[:, tc:tc + HALO_B, :] = jnp.zeros((D_B // LANES, HALO_B, LANES), F32)
            eabuf[:, tc:tc + HALO_A, :] = jnp.zeros((D_A // LANES, HALO_A, LANES), F32)
            wacc[...] = jnp.zeros_like(wacc)
            aacc[...] = jnp.zeros_like(aacc)
            vacc[...] = jnp.zeros_like(vacc)

        dbb = jnp.zeros((SUBLANES, D_B), F32)
        dlg = jnp.zeros((SUBLANES, D_B), F32)
        dlb = jnp.zeros((SUBLANES, D_B), F32)
        for c in range(n_chunks):
            rows = slice(c * CONV_CHUNK, (c + 1) * CONV_CHUNK)
            dy_a = dy_ref[rows, 0:D_A].astype(F32)
            b_a = u_ref[rows, 0:D_A].astype(F32)
            du_ref[rows, 0:D_A] = (dy_a * ca_ref[rows, :].astype(F32)).astype(BF16)
            dca = dy_a * b_a
            cv = u_ref[rows, D_A:2 * D_A].astype(F32) * u_ref[rows, 2 * D_A:3 * D_A].astype(F32)
            gate = _sigmoid(u_ref[rows, 3 * D_A + D_B:3 * D_A + 2 * D_B].astype(F32))
            z = u_ref[rows, 3 * D_A:3 * D_A + D_B].astype(F32) * gate
            for lb_i in range(D_A // LANES):
                lanes = slice(lb_i * LANES, (lb_i + 1) * LANES)
                eabuf[lb_i, rows, :] = dca[:, lanes]
                cvbuf[lb_i, rows, :] = cv[:, lanes]
                zbuf[lb_i, rows, :] = z[:, lanes]
                gatebuf[lb_i, rows, :] = gate[:, lanes]

            zcv = zc_ref[rows, :].astype(F32)
            mu = jnp.mean(zcv, axis=-1, keepdims=True)
            xc = zcv - mu
            rstd = lax.rsqrt(jnp.mean(xc * xc, axis=-1, keepdims=True) + LN_EPS)
            xhat = xc * rstd
            ln = xhat * lg_ref[...] + lb_ref[...]
            sg = _sigmoid(ln)
            dln = dy_ref[rows, D_A:D_A + D_B].astype(F32) * (sg * (1.0 + ln * (1.0 - sg)))
            dlg = dlg + jnp.sum((dln * xhat).reshape(CONV_CHUNK // SUBLANES, SUBLANES, D_B), axis=0)
            dlb = dlb + jnp.sum(dln.reshape(CONV_CHUNK // SUBLANES, SUBLANES, D_B), axis=0)
            dxh = dln * lg_ref[...]
            dzc = rstd * (dxh - jnp.mean(dxh, axis=-1, keepdims=True) - xhat * jnp.mean(dxh * xhat, axis=-1, keepdims=True))
            dbb = dbb + jnp.sum(dzc.reshape(CONV_CHUNK // SUBLANES, SUBLANES, D_B), axis=0)
            for lb_i in range(D_B // LANES):
                ebuf[lb_i, rows, :] = dzc[:, lb_i * LANES:(lb_i + 1) * LANES]
        vacc[0] += dbb
        vacc[1] += dlg
        vacc[2] += dlb

        for lb_i in range(D_A // LANES):
            lanes = slice(lb_i * LANES, (lb_i + 1) * LANES)

            def cols(first):
                return slice(first + lb_i * LANES, first + (lb_i + 1) * LANES)

            for c in range(n_chunks):
                r0 = c * CONV_CHUNK
                rows = slice(r0, r0 + CONV_CHUNK)
                cv = cvbuf[lb_i, rows, :]
                dcv = jnp.zeros((CONV_CHUNK, LANES), F32)
                for k in range(CONV_A_W):
                    off = r0 + (CONV_A_W - 1) - k
                    e = eabuf[lb_i, off:off + CONV_CHUNK, :]
                    dcv = dcv + wa_ref[k:k + 1, lanes] * e
                    aacc[k, :, lanes] += jnp.sum((cv * e).reshape(CONV_CHUNK // SUBLANES, SUBLANES, LANES), axis=0)
                du_ref[rows, cols(D_A)] = (dcv * u_ref[rows, cols(2 * D_A)].astype(F32)).astype(BF16)
                du_ref[rows, cols(2 * D_A)] = (dcv * u_ref[rows, cols(D_A)].astype(F32)).astype(BF16)

                z = zbuf[lb_i, rows, :]
                dz = jnp.zeros((CONV_CHUNK, LANES), F32)
                for k in range(CONV_B_W):
                    off = r0 + (CONV_B_W - 1) - k
                    e = ebuf[lb_i, off:off + CONV_CHUNK, :]
                    dz = dz + wb_ref[k:k + 1, lanes] * e
                    wacc[k, :, lanes] += jnp.sum((z * e).reshape(CONV_CHUNK // SUBLANES, SUBLANES, LANES), axis=0)
                glu_v = u_ref[rows, cols(3 * D_A)].astype(F32)
                sgg = gatebuf[lb_i, rows, :]
                du_ref[rows, cols(3 * D_A)] = (dz * sgg).astype(BF16)
                du_ref[rows, cols(3 * D_A + D_B)] = (dz * glu_v * sgg * (1.0 - sgg)).astype(BF16)

            ebuf[lb_i, tc:tc + HALO_B, :] = ebuf[lb_i, 0:HALO_B, :]
            eabuf[lb_i, tc:tc + HALO_A, :] = eabuf[lb_i, 0:HALO_A, :]

        @pl.when(step == n_tiles - 1)
        def _():
            for k in range(CONV_B_W):
                dwb_ref[k:k + 1, :] = jnp.sum(wacc[k], axis=0, keepdims=True)
            for k in range(CONV_A_W):
                dwa_ref[k:k + 1, :] = jnp.sum(aacc[k], axis=0, keepdims=True)
            dbb_ref[...] = jnp.sum(vacc[0], axis=0, keepdims=True)
            dlg_ref[...] = jnp.sum(vacc[1], axis=0, keepdims=True)
            dlb_ref[...] = jnp.sum(vacc[2], axis=0, keepdims=True)

    return _hosted_call(
        core, comm, name, n_tiles,
        in_specs=[_rtile(tc, D_A + D_B, n_tiles), _rtile(tc, D_IN_ALL, n_tiles), _rtile(tc, D_B, n_tiles),
                  _rtile(tc, D_A, n_tiles), _whole((CONV_A_W, D_A)), _whole((CONV_B_W, D_B)), _whole((1, D_B)),
                  _whole((1, D_B))],
        out_specs=[_rtile(tc, D_IN_ALL, n_tiles), _whole((CONV_A_W, D_A)), _whole((CONV_B_W, D_B)), _whole((1, D_B)),
                   _whole((1, D_B)), _whole((1, D_B))],
        out_shape=[_sds((t, D_IN_ALL), BF16), _sds((CONV_A_W, D_A), F32), _sds((CONV_B_W, D_B), F32), _sds((1, D_B), F32),
                   _sds((1, D_B), F32), _sds((1, D_B), F32)],
        scratch_shapes=[pltpu.VMEM((D_B // LANES, tc + HALO_B, LANES), F32), pltpu.VMEM((D_A // LANES, tc + HALO_A, LANES), F32),
                        pltpu.VMEM((D_B // LANES, tc, LANES), F32), pltpu.VMEM((D_A // LANES, tc, LANES), F32),
                        pltpu.VMEM((D_B // LANES, tc, LANES), F32),
                        pltpu.VMEM((CONV_B_W, SUBLANES, D_B), F32), pltpu.VMEM((CONV_A_W, SUBLANES, D_A), F32),
                        pltpu.VMEM((3, SUBLANES, D_B), F32)],
        operands=(dy, u, zc, ca, wa, wb, lg, lb))


def _in_bwd(du, w_t, x, dx1, g, tm, name):
    t, d = x.shape
    n = w_t.shape[0]

    def body(du_ref, w_ref, x_ref, dx1_ref, g_ref, dx_ref, dg_ref):
        _zero_at_first_step(dg_ref)
        dh = _dot(du_ref[...], w_ref[...])
        dxn, dg = _rms_bwd(dh, x_ref[...], g_ref[...])
        dg_ref[...] += dg
        dx_ref[...] = dx1_ref[...] + dxn

    return pl.pallas_call(
        body, name=name, grid=(t // tm,),
        in_specs=[_tile(tm, n), _resident((n, d)), _tile(tm, d), _tile(tm, d), _whole((1, d))],
        out_specs=[_tile(tm, d), _whole((1, d))],
        out_shape=[_sds((t, d), F32), _sds((1, d), F32)],
        compiler_params=_params(),
    )(du, w_t, x, dx1, g)


def _wgrad(a, b, tk, bm, bn, name, comm=None, cols=None):
    t, m = a.shape
    first, n = (0, b.shape[1]) if cols is None else cols
    first_block = first // bn
    n_k = t // tk

    def body(a_ref, b_ref, o_ref, acc):
        @pl.when(pl.program_id(2) == 0)
        def _():
            acc[...] = jnp.zeros_like(acc)

        acc[...] += _dot_tn(a_ref[...], b_ref[...])

        @pl.when(pl.program_id(2) == n_k - 1)
        def _():
            o_ref[...] = acc[...].astype(BF16)

    call = dict(
        grid=(m // bm, n // bn, n_k),
        in_specs=[pl.BlockSpec((tk, bm), lambda i, j, k: (k, i)),
                  pl.BlockSpec((tk, bn), lambda i, j, k: (k, first_block + j))],
        out_specs=[pl.BlockSpec((bm, bn), lambda i, j, k: (i, j))],
        out_shape=[_sds((m, n), BF16)],
        scratch_shapes=[pltpu.VMEM((bm, bn), F32)])
    if comm is None:
        return pl.pallas_call(body, name=name, compiler_params=_params(3), **call)(a, b)[0]
    (out,), received = _hosted_call(body, comm, name, operands=(a, b), **call)
    return out, received


def _adamw(items, tr, name):
    n_parts, r, c = items[0][0].shape
    n_items = len(items)

    def body(*refs):
        ins, outs = refs[:4 * n_items], refs[4 * n_items:]
        for i in range(n_items):
            p_ref, w_ref, m_ref, v_ref = ins[4 * i:4 * i + 4]
            g_ref, d_ref, nm_ref, nv_ref = outs[4 * i:4 * i + 4]
            g = p_ref[0].astype(F32)
            for j in range(1, n_parts):
                g = g + p_ref[j].astype(F32)
            g_ref[...] = g
            nm = ADAM_B1 * m_ref[...] + (1.0 - ADAM_B1) * g
            nv = ADAM_B2 * v_ref[...] + (1.0 - ADAM_B2) * (g * g)
            nm_ref[...] = nm
            nv_ref[...] = nv
            m_hat = nm / (1.0 - ADAM_B1 ** ADAM_STEP)
            v_hat = nv / (1.0 - ADAM_B2 ** ADAM_STEP)
            d_ref[...] = -ADAM_LR * (m_hat / (jnp.sqrt(v_hat) + ADAM_EPS) + ADAM_WD * w_ref[...])

    results = pl.pallas_call(
        body, name=name, grid=(r // tr,),
        in_specs=([pl.BlockSpec((n_parts, tr, c), lambda i: (0, i, 0))] + [_tile(tr, c)] * 3) * n_items,
        out_specs=[_tile(tr, c)] * (4 * n_items),
        out_shape=[_sds((r, c), F32)] * (4 * n_items),
        compiler_params=_params(),
    )(*[a for item in items for a in item])
    return [results[4 * i:4 * i + 4] for i in range(n_items)]


def _sum_parts(parts, name):
    n_parts, r, c = parts.shape

    def body(p_ref, o_ref):
        acc = p_ref[0]
        for j in range(1, n_parts):
            acc = acc + p_ref[j]
        o_ref[...] = acc

    return pl.pallas_call(
        body, name=name, grid=(1,),
        in_specs=[_whole((n_parts, r, c))], out_specs=_whole((r, c)), out_shape=_sds((r, c), F32),
        compiler_params=_params(),
    )(parts)


def _row(v):
    return v.reshape(1, -1)


def _by_owner_rows(g):
    return g.reshape(N_DEV, g.shape[0] // N_DEV, g.shape[1])


WEIGHTS = ("mix_pre_g", "w_mix_in", "conv_a_w", "conv_b_w", "conv_b_b", "ln_b_g", "ln_b_b", "w_mix_out", "mix_post_g",
           "xa_pre_g", "mem_norm_g", "w_q", "w_k", "w_v", "w_o", "xa_post_g", "ffn_pre_g", "w_gate", "w_up", "w_down",
           "ffn_post_g")
LARGE = ("w_mix_in", "w_mix_out", "w_q", "w_k", "w_v", "w_o", "w_gate", "w_up", "w_down")
COLUMN_SHARDED = ("w_mix_in", "w_gate", "w_up")
GAINS = ("mix_pre_g", "mix_post_g", "xa_pre_g", "mem_norm_g", "xa_post_g", "ffn_pre_g", "ffn_post_g")
CHANNEL_VECTORS = ("conv_b_b", "ln_b_g", "ln_b_b")
CONV_TAPS = ("conv_a_w", "conv_b_w")
SMALL = GAINS + CHANNEL_VECTORS + CONV_TAPS
CONV_COLS_PER_DEVICE = D_A // N_DEV
TOKEN_TILE = 512
FFN_FWD_TOKEN_TILE = 512
FFN_TOKEN_TILE = 256
CONV_TOKEN_TILE = 256
WGRAD_TOKEN_TILE = 2048
SMALL_WGRAD_TOKEN_TILE = 4096
IN_BWD_TOKEN_TILE = 1024
ADAM_ROWS_PER_STEP = 64


def _lane_rows(v):
    flat = v.reshape(-1)
    tile = SUBLANES * LANES
    flat = jnp.pad(flat, (0, (-flat.shape[0]) % tile))
    return flat.reshape(-1, LANES)


def _pack_small(values, names):
    return jnp.concatenate([_lane_rows(values[n]) for n in names], axis=0)


def _unpack_small(packed, like, names):
    out, off = {}, 0
    for n in names:
        size = like[n].size
        rows = _lane_rows(like[n]).shape[0]
        out[n] = packed[off:off + rows, :].reshape(-1)[:size].reshape(like[n].shape)
        off += rows
    return out


def kernel(x, mem, mix_pre_g, w_mix_in, conv_a_w, conv_b_w, conv_b_b, ln_b_g, ln_b_b, w_mix_out, mix_post_g, xa_pre_g, mem_norm_g, w_q, w_k, w_v, w_o, xa_post_g, ffn_pre_g, w_gate, w_up, w_down, ffn_post_g, loss_target, m_mix_pre_g, m_w_mix_in, m_conv_a_w, m_conv_b_w, m_conv_b_b, m_ln_b_g, m_ln_b_b, m_w_mix_out, m_mix_post_g, m_xa_pre_g, m_mem_norm_g, m_w_q, m_w_k, m_w_v, m_w_o, m_xa_post_g, m_ffn_pre_g, m_w_gate, m_w_up, m_w_down, m_ffn_post_g, v_mix_pre_g, v_w_mix_in, v_conv_a_w, v_conv_b_w, v_conv_b_b, v_ln_b_g, v_ln_b_b, v_w_mix_out, v_mix_post_g, v_xa_pre_g, v_mem_norm_g, v_w_q, v_w_k, v_w_v, v_w_o, v_xa_post_g, v_ffn_pre_g, v_w_gate, v_w_up, v_w_down, v_ffn_post_g):
    given = dict(locals())
    w = {n: given[n] for n in WEIGHTS}
    m = {n: given["m_" + n] for n in WEIGHTS}
    v = {n: given["v_" + n] for n in WEIGHTS}
    xs, mems, target = x[0], mem[0], loss_target[0]
    t = xs.shape[0]
    tm, tm_ffn, tc, tk = min(TOKEN_TILE, t), min(FFN_TOKEN_TILE, t), min(CONV_TOKEN_TILE, t), min(WGRAD_TOKEN_TILE, t)
    tk_small = min(SMALL_WGRAD_TOKEN_TILE, t)
    g = {n: _row(w[n]) for n in GAINS}
    bb, lg, lb = (_row(w[n]) for n in CHANNEL_VECTORS)

    def shard_bf16(*names):
        return [w[n].astype(BF16) for n in names]

    def shard_bf16_t(*names):
        return [w[n].T.astype(BF16) for n in names]

    g_mix_in, g_taps = _comm_call(_Gather(shard_bf16_t("w_mix_in") + [_pack_small(w, CONV_TAPS)]), "gather_mixer")
    w_mix_in_t = g_mix_in.reshape(D_IN_ALL, D_MODEL)
    taps, off = {}, 0
    for n in CONV_TAPS:
        k, cols = w[n].shape
        rows = _lane_rows(w[n]).shape[0]
        blk = g_taps[:, off:off + rows, :].reshape(N_DEV, -1)[:, :k * cols].reshape(N_DEV, k, cols)
        taps[n] = blk.transpose(1, 0, 2).reshape(k, N_DEV * cols)
        off += rows
    wa, wb = taps["conv_a_w"], taps["conv_b_w"]

    (h1, u), gathered = _norm_matmul(xs, g["mix_pre_g"], w_mix_in_t.T, tm, "mix_in_fwd",
                                     _Gather(shard_bf16("w_mix_out", "w_q", "w_k", "w_v", "w_o")))
    w_mix_out, w_q, w_k, w_v, w_o = (a.reshape(D_MODEL, D_MODEL) for a in gathered)
    (ycat, zc, ca, x1, s1), (g_gate, g_up, g_down) = _conv_fwd(
        u, wa, wb, bb, lg, lb, xs, w_mix_out, g["mix_post_g"], tc, "conv_fwd",
        _Gather(shard_bf16_t("w_gate", "w_up") + shard_bf16("w_down")))
    w_gate_t, w_up_t, w_down = (a.reshape(D_FF, D_MODEL) for a in (g_gate, g_up, g_down))
    mem_n, kk, vv = _mem_fwd(mems, g["mem_norm_g"], w_k, w_v, "mem_fwd")
    x2, h2, q, o, a = _attn_fwd(x1, g["xa_pre_g"], w_q, kk, vv, w_o, g["xa_post_g"], tm, "attn_fwd")
    h3, gt, up, f, sq = _ffn_fwd(x2, g["ffn_pre_g"], w_gate_t, w_up_t, w_down, g["ffn_post_g"], target,
                                 min(FFN_FWD_TOKEN_TILE, t), "ffn_fwd")

    dx2, df, hd, dgt, dup, d_ffn_post, d_ffn_pre = _ffn_bwd(
        x2, f, target, gt, up, g["ffn_pre_g"], w_gate_t, w_up_t, w_down, g["ffn_post_g"], tm_ffn, "ffn_bwd")
    d_w_down = _wgrad(hd, df, tk, D_FF // 2, D_MODEL, "wgrad_down")
    d_w_gate_t = _wgrad(dgt, h3, tk, D_FF // 2, D_MODEL, "wgrad_gate")
    d_w_up_t = _wgrad(dup, h3, tk, D_FF // 2, D_MODEL, "wgrad_up")
    ffn_slabs = [_by_owner_rows(d) for d in (d_w_gate_t, d_w_up_t, d_w_down)]

    (dx1, da, dq, dk, dv, d_xa_post, d_xa_pre, ds1, dycat, d_mix_post), from_ffn = _attn_bwd(
        dx2, a, x1, q, kk, vv, g["xa_pre_g"], w_q, w_o, g["xa_post_g"], s1, w_mix_out, g["mix_post_g"], tm, "attn_bwd",
        _Exchange(ffn_slabs, [True] * 3))
    d_w_o = _wgrad(o, da, tk, D_MODEL, D_MODEL, "wgrad_o")
    d_w_q = _wgrad(h2, dq, tk, D_MODEL, D_MODEL, "wgrad_q")
    d_w_k, d_w_v, d_mem_norm = _mem_bwd(mems, mem_n, dk, dv, g["mem_norm_g"], w_k, w_v, "mem_bwd")
    d_w_mix_out = _wgrad(ycat, ds1, tk, D_MODEL, D_MODEL, "wgrad_mix_out")
    attn_slabs = [_by_owner_rows(d) for d in (d_w_mix_out, d_w_q, d_w_k, d_w_v, d_w_o)]

    (du, d_conv_a, d_conv_b, d_conv_bb, d_ln_g, d_ln_b), from_attn = _conv_bwd(
        dycat, u, zc, ca, wa, wb, lg, lb, tc, "conv_bwd", _Exchange(attn_slabs, [True] * 5))
    small_grads = dict(conv_a_w=d_conv_a, conv_b_w=d_conv_b, conv_b_b=d_conv_bb, ln_b_g=d_ln_g, ln_b_b=d_ln_b,
                       mix_post_g=d_mix_post, xa_pre_g=d_xa_pre, mem_norm_g=d_mem_norm, xa_post_g=d_xa_post,
                       ffn_pre_g=d_ffn_pre, ffn_post_g=d_ffn_post, loss=sq)
    names = tuple(n for n in SMALL if n != "mix_pre_g") + ("loss", "mix_pre_g")
    half = D_MODEL // 2
    d_in_lo = _wgrad(du, h1, tk_small, D_IN_ALL // 2, half, "wgrad_mix_in_lo", cols=(0, half))
    d_in_hi, (from_in_lo, early_small) = _wgrad(
        du, h1, tk_small, D_IN_ALL // 2, half, "wgrad_mix_in_hi",
        _Exchange([_by_owner_rows(d_in_lo), _pack_small(small_grads, names[:-1])], [True, False]), cols=(half, half))
    dx, d_mix_pre = _in_bwd(du, w_mix_in_t, xs, dx1, g["mix_pre_g"], min(IN_BWD_TOKEN_TILE, t), "mix_in_bwd")
    small_grads["mix_pre_g"] = d_mix_pre
    tail_scatter = [True, False]
    tail = _exchange_start([_by_owner_rows(d_in_hi), _lane_rows(d_mix_pre)], tail_scatter, "tail_start")

    received = dict(zip(("w_gate", "w_up", "w_down"), from_ffn))
    received.update(zip(("w_mix_out", "w_q", "w_k", "w_v", "w_o"), from_attn))
    grad, delta, new_m, new_v = {}, {}, {}, {}

    def adamw_group(group):
        items = [[received[n]] + [a[n].T if n in COLUMN_SHARDED else a[n] for a in (w, m, v)] for n in group]
        rows = items[0][1].shape[0]
        results = _adamw(items, min(rows, ADAM_ROWS_PER_STEP) if len(group) > 1 else rows, "adamw_" + group[0])
        for n, result in zip(group, results):
            grad[n], delta[n], new_m[n], new_v[n] = [r.T if n in COLUMN_SHARDED else r for r in result]

    for group in (("w_gate",), ("w_up",), ("w_down",), ("w_mix_out", "w_q", "w_k", "w_v", "w_o")):
        adamw_group(group)
    from_in_hi, late_small = _exchange_wait(tail, tail_scatter, new_v["w_o"], "tail_wait")
    all_small = jnp.concatenate([early_small, late_small], axis=1)
    received["w_mix_in"] = jnp.concatenate([from_in_lo, from_in_hi], axis=2)
    adamw_group(("w_mix_in",))

    total = _unpack_small(_sum_parts(all_small, "sum_small"), small_grads, names)
    loss = jnp.sum(total.pop("loss")) * (0.5 / D_MODEL)
    first_col = _device_index() * CONV_COLS_PER_DEVICE
    for n in CONV_TAPS:
        total[n] = lax.dynamic_slice_in_dim(total[n], first_col, CONV_COLS_PER_DEVICE, axis=1)
    total = {n: total[n].reshape(w[n].shape) for n in SMALL}
    packed_small = [_pack_small(values, SMALL) for values in (total, w, m, v)]
    ((g_s, d_s, nm_s, nv_s),) = _adamw([[packed_small[0][None]] + packed_small[1:]], packed_small[0].shape[0],
                                       "adamw_small")
    for out, p in ((grad, g_s), (delta, d_s), (new_m, nm_s), (new_v, nv_s)):
        out.update(_unpack_small(p, w, SMALL))

    return (loss, dx[None], *[grad[n] for n in WEIGHTS], *[delta[n] for n in WEIGHTS], *[new_m[n] for n in WEIGHTS],
            *[new_v[n] for n in WEIGHTS])
```

```python
import jax
import jax.numpy as jnp
from jax import lax
from jax.experimental import pallas as pl
from jax.experimental.pallas import tpu as pltpu

F32 = jnp.float32
BF16 = jnp.bfloat16

D_MODEL = 1024
D_A = 512
D_B = 512
D_IN_ALL = 2560
CONV_A_W = 3
CONV_B_W = 31
XA_HEADS = 4
XA_HEAD_DIM = 256
D_FF = 2816
N_DEV = 8
RMS_EPS = 1e-6
LN_EPS = 1e-5
ADAM_LR = 0.001
ADAM_B1 = 0.9
ADAM_B2 = 0.999
ADAM_EPS = 1e-08
ADAM_WD = 0.01
ADAM_STEP = 10

VMEM_LIMIT_BYTES = 56 * 1024 * 1024
SUBLANES = 8
LANES = 128
HALO_B = 32
HALO_A = 8
CONV_FWD_CHUNK = 16
CONV_CHUNK = 32
GATHER_FORWARD_STEPS_BEFORE_END = 8

MESH = pl.DeviceIdType.MESH


def _params(n_grid_axes=1):
    return pltpu.CompilerParams(dimension_semantics=("arbitrary",) * n_grid_axes, vmem_limit_bytes=VMEM_LIMIT_BYTES)


def _sds(shape, dtype):
    return jax.ShapeDtypeStruct(shape, dtype)


def _tile(rows, cols):
    return pl.BlockSpec((rows, cols), lambda i: (i, 0))


def _rtile(rows, cols, n):
    return pl.BlockSpec((rows, cols), lambda i: (n - 1 - i, 0))


def _whole(shape):
    zeros = (0,) * len(shape)
    return pl.BlockSpec(shape, lambda i: zeros)


def _resident(shape):
    zeros = (0,) * len(shape)
    return pl.BlockSpec(shape, lambda i: zeros, pipeline_mode=pl.Buffered(1))


def _dot(a, b):
    return jnp.dot(a, b, preferred_element_type=F32)


def _dot_nt(a, b):
    return lax.dot_general(a, b, (((1,), (1,)), ((), ())), preferred_element_type=F32)


def _dot_tn(a, b):
    return lax.dot_general(a, b, (((0,), (0,)), ((), ())), preferred_element_type=F32)


def _sigmoid(x):
    return 1.0 / (1.0 + jnp.exp(-x))


def _rms_fwd(x, g):
    r = lax.rsqrt(jnp.mean(x * x, axis=-1, keepdims=True) + RMS_EPS)
    return x * r * g


def _rms_bwd(dy, xin, g):
    r = lax.rsqrt(jnp.mean(xin * xin, axis=-1, keepdims=True) + RMS_EPS)
    n = xin * r
    dg = jnp.sum(dy * n, axis=0, keepdims=True)
    dn = dy * g
    dx = r * (dn - n * jnp.mean(dn * n, axis=-1, keepdims=True))
    return dx, dg


def _zero_at_first_step(*refs):
    @pl.when(pl.program_id(0) == 0)
    def _():
        for ref in refs:
            ref[...] = jnp.zeros(ref.shape, ref.dtype)


def _place():
    return lax.axis_index("x"), lax.axis_index("y"), lax.axis_index("c")


def _device_index():
    x, y, c = _place()
    return 4 * x + 2 * y + c


class _Gather:
    def __init__(self, arrays):
        self.arrays = list(arrays)
        self.out_shape = [_sds((N_DEV, *a.shape), a.dtype) for a in self.arrays]
        n = len(self.arrays)
        self.scratch_shapes = [pltpu.SemaphoreType.DMA((n, 7)), pltpu.SemaphoreType.DMA((n, 7)),
                               pltpu.SemaphoreType.DMA((n,))]

    def forward_step(self, n_steps):
        return max(0, n_steps - 1 - GATHER_FORWARD_STEPS_BEFORE_END)

    def bind(self, srcs, dsts, send_sems, recv_sems, local_sems):
        x, y, cc = _place()
        me, sibling = (x, y, cc), (x, y, 1 - cc)
        chips = [(1 - x, y), (x, 1 - y), (1 - x, 1 - y)]

        def copy(a, k, owner, to, src=None):
            slot = dsts[a].at[4 * owner[0] + 2 * owner[1] + owner[2]]
            return pltpu.make_async_remote_copy(
                src_ref=slot if src is None else src, dst_ref=slot, send_sem=send_sems.at[a, k],
                recv_sem=recv_sems.at[a, k], device_id=to, device_id_type=MESH)

        def first(a):
            return [copy(a, 0, me, sibling, src=srcs[a])] + [
                copy(a, 1 + j, me, (*chip, cc), src=srcs[a]) for j, chip in enumerate(chips)]

        def passed(a, j):
            return copy(a, 4 + j, (*chips[j], cc), sibling)

        def mine(a):
            return pltpu.make_async_copy(srcs[a], dsts[a].at[4 * x + 2 * y + cc], local_sems.at[a])

        def start():
            for a in range(len(srcs)):
                mine(a).start()
                for cp in first(a):
                    cp.start()

        def forward():
            for a in range(len(srcs)):
                for j, chip in enumerate(chips):
                    copy(a, 1 + j, (*chip, cc), me).wait_recv()
                    passed(a, j).start()

        def finish():
            for a in range(len(srcs)):
                copy(a, 0, sibling, me).wait_recv()
                for j, chip in enumerate(chips):
                    copy(a, 4 + j, (*chip, 1 - cc), me).wait_recv()
                for cp in first(a) + [passed(a, j) for j in range(len(chips))]:
                    cp.wait_send()
                mine(a).wait()

        return start, forward, finish


class _Exchange:
    def __init__(self, arrays, scatter):
        self.arrays = list(arrays)
        self.scatter = list(scatter)
        self.out_shape = [_sds(a.shape if s else (N_DEV, *a.shape), a.dtype) for a, s in zip(self.arrays, self.scatter)]
        n = len(self.arrays)
        self.scratch_shapes = [pltpu.SemaphoreType.DMA((n, 7)), pltpu.SemaphoreType.DMA((n, 7)),
                               pltpu.SemaphoreType.DMA((n,))]

    def forward_step(self, n_steps):
        return n_steps - 1

    def bind(self, srcs, dsts, send_sems, recv_sems, local_sems):
        me = _device_index()

        def copies(a):
            out = []
            for k in range(1, N_DEV):
                p = me ^ k
                out.append(pltpu.make_async_remote_copy(
                    src_ref=srcs[a].at[p] if self.scatter[a] else srcs[a], dst_ref=dsts[a].at[me],
                    send_sem=send_sems.at[a, k - 1], recv_sem=recv_sems.at[a, k - 1],
                    device_id=(p >> 2, (p >> 1) & 1, p & 1), device_id_type=MESH))
            return out

        def mine(a):
            return pltpu.make_async_copy(srcs[a].at[me] if self.scatter[a] else srcs[a], dsts[a].at[me], local_sems.at[a])

        def start():
            for a in range(len(srcs)):
                mine(a).start()
                for cp in copies(a):
                    cp.start()

        def forward():
            pass

        def finish():
            for a in range(len(srcs)):
                for cp in copies(a):
                    cp.wait()
                mine(a).wait()

        return start, forward, finish


def _hosted_call(core, comm, name, grid, in_specs, out_specs, out_shape, scratch_shapes, operands):
    grid = (grid,) if isinstance(grid, int) else tuple(grid)
    n_steps = 1
    for extent in grid:
        n_steps *= extent
    n_in, n_out, n_scr, n_arr = len(in_specs), len(out_specs), len(scratch_shapes), len(comm.arrays)
    any_spec = pl.BlockSpec(memory_space=pl.ANY)

    def body(*refs):
        ins, refs = refs[:n_in], refs[n_in:]
        srcs, refs = refs[:n_arr], refs[n_arr:]
        outs, refs = refs[:n_out], refs[n_out:]
        dsts, refs = refs[:n_arr], refs[n_arr:]
        scratch, sems = refs[:n_scr], refs[n_scr:]
        start, forward, finish = comm.bind(srcs, dsts, *sems)
        step = pl.program_id(0)
        for axis in range(1, len(grid)):
            step = step * grid[axis] + pl.program_id(axis)
        pl.when(step == 0)(start)
        core(*ins, *outs, *scratch)
        pl.when(step == comm.forward_step(n_steps))(forward)
        pl.when(step == n_steps - 1)(finish)

    results = pl.pallas_call(
        body, name=name, grid=grid,
        in_specs=list(in_specs) + [any_spec] * n_arr,
        out_specs=list(out_specs) + [any_spec] * n_arr,
        out_shape=list(out_shape) + comm.out_shape,
        scratch_shapes=list(scratch_shapes) + comm.scratch_shapes,
        compiler_params=_params(len(grid)),
    )(*operands, *comm.arrays)
    return results[:n_out], results[n_out:]


def _comm_call(comm, name):
    return _hosted_call(lambda: None, comm, name, 1, [], [], [], [], [])[1]


def _split_exchange_copies(srcs, lands, scatter, send_sems, recv_sems):
    me = _device_index()
    out = []
    for a in range(len(srcs)):
        for k in range(1, N_DEV):
            p = me ^ k
            out.append(pltpu.make_async_remote_copy(
                src_ref=srcs[a].at[p] if scatter[a] else srcs[a], dst_ref=lands[a].at[me],
                send_sem=send_sems[a].at[k - 1], recv_sem=recv_sems[a].at[k - 1],
                device_id=(p >> 2, (p >> 1) & 1, p & 1), device_id_type=MESH))
    return out


def _exchange_start(arrays, scatter, name):
    n = len(arrays)
    me = _device_index()
    lands = []
    for a, s in zip(arrays, scatter):
        own = lax.dynamic_slice_in_dim(a, me, 1, axis=0) if s else a[None]
        empty = jnp.zeros(a.shape if s else (N_DEV, *a.shape), a.dtype)
        lands.append(lax.dynamic_update_slice_in_dim(empty, own, me, axis=0))

    def body(*refs):
        srcs, zones = refs[:n], refs[n:2 * n]
        send_sems, recv_sems = refs[2 * n:3 * n], refs[3 * n:4 * n]
        for cp in _split_exchange_copies(srcs, zones, scatter, send_sems, recv_sems):
            cp.start()
        refs[-1][...] = jnp.zeros(refs[-1].shape, F32)

    hbm = pl.BlockSpec(memory_space=pltpu.HBM)
    sem = pl.BlockSpec(memory_space=pltpu.SEMAPHORE)
    operands = [pltpu.with_memory_space_constraint(a, pltpu.HBM) for a in list(arrays) + lands]
    results = pl.pallas_call(
        body, name=name,
        out_shape=(*[pltpu.SemaphoreType.DMA((N_DEV - 1,))] * (2 * n), *[pltpu.HBM(a.shape, a.dtype) for a in operands],
                   _sds((SUBLANES, LANES), F32)),
        in_specs=[hbm] * (2 * n),
        out_specs=(*[sem] * (2 * n), *[hbm] * (2 * n), pl.BlockSpec(memory_space=pltpu.VMEM)),
        input_output_aliases={i: 2 * n + i for i in range(2 * n)},
        compiler_params=pltpu.CompilerParams(has_side_effects=pltpu.SideEffectType.DATAFLOW_SIDE_EFFECTING),
    )(*operands)
    return (results[:n], results[n:2 * n], results[2 * n:3 * n], results[3 * n:4 * n]), results[-1]


def _exchange_wait(started, scatter, after, name):
    send_sems, recv_sems, srcs, lands = started
    n = len(srcs)

    def body(*refs):
        src_refs, zones = refs[:n], refs[n:2 * n]
        send, recv = refs[2 * n:3 * n], refs[3 * n:4 * n]
        for cp in _split_exchange_copies(src_refs, zones, scatter, send, recv):
            cp.wait_send()
            cp.wait_recv()

    hbm = pl.BlockSpec(memory_space=pltpu.HBM)
    sem = pl.BlockSpec(memory_space=pltpu.SEMAPHORE)
    results = pl.pallas_call(
        body, name=name,
        out_shape=tuple(pltpu.HBM(a.shape, a.dtype) for a in list(srcs) + list(lands)),
        in_specs=[hbm] * (2 * n) + [sem] * (2 * n) + [pl.BlockSpec(memory_space=pl.ANY)],
        out_specs=tuple([hbm] * (2 * n)),
        input_output_aliases={i: i for i in range(2 * n)},
        compiler_params=pltpu.CompilerParams(has_side_effects=pltpu.SideEffectType.DATAFLOW_SIDE_EFFECTING),
    )(*srcs, *lands, *send_sems, *recv_sems, after)
    return results[n:]


def _norm_matmul(x, g, w, tm, name, comm):
    t, d = x.shape
    n = w.shape[1]

    def core(x_ref, g_ref, w_ref, h_ref, o_ref):
        h = _rms_fwd(x_ref[...], g_ref[...]).astype(BF16)
        h_ref[...] = h
        o_ref[...] = _dot(h, w_ref[...]).astype(BF16)

    return _hosted_call(
        core, comm, name, t // tm,
        in_specs=[_tile(tm, d), _whole((1, d)), _resident((d, n))],
        out_specs=[_tile(tm, d), _tile(tm, n)],
        out_shape=[_sds((t, d), BF16), _sds((t, n), BF16)],
        scratch_shapes=[], operands=(x, g, w))


def _conv_fwd(u, wa, wb, bb, lg, lb, xres, w_out, g_post, tc, name, comm):
    t = u.shape[0]
    d = w_out.shape[1]
    chunk = CONV_FWD_CHUNK
    n_chunks = tc // chunk
    piece = 2 * LANES

    def core(u_ref, wa_ref, wb_ref, bb_ref, lg_ref, lb_ref, x_ref, wo_ref, go_ref,
             y_ref, zc_ref, ca_ref, x1_ref, s_ref, zbuf, cvbuf, zcbuf):
        @pl.when(pl.program_id(0) == 0)
        def _():
            zbuf[:, 0:HALO_B, :] = jnp.zeros((D_B // LANES, HALO_B, LANES), F32)
            cvbuf[:, 0:HALO_A, :] = jnp.zeros((D_A // LANES, HALO_A, LANES), F32)

        s = None
        for lb_i in range(D_A // LANES):
            if lb_i > 0 and lb_i % 2 == 0:
                cols = slice((lb_i - 2) * LANES, lb_i * LANES)
                part = _dot(y_ref[:, cols], wo_ref[cols, :])
                s = part if s is None else s + part
            lanes = slice(lb_i * LANES, (lb_i + 1) * LANES)
            c_a = u_ref[:, D_A + lb_i * LANES:D_A + (lb_i + 1) * LANES].astype(F32)
            v_a = u_ref[:, 2 * D_A + lb_i * LANES:2 * D_A + (lb_i + 1) * LANES].astype(F32)
            cvbuf[lb_i, HALO_A:HALO_A + tc, :] = c_a * v_a
            glu_v = u_ref[:, 3 * D_A + lb_i * LANES:3 * D_A + (lb_i + 1) * LANES].astype(F32)
            glu_g = u_ref[:, 3 * D_A + D_B + lb_i * LANES:3 * D_A + D_B + (lb_i + 1) * LANES].astype(F32)
            zbuf[lb_i, HALO_B:HALO_B + tc, :] = glu_v * _sigmoid(glu_g)

            for c in range(n_chunks):
                r0 = c * chunk
                rows = slice(r0, r0 + chunk)
                acc = jnp.zeros((chunk, LANES), F32)
                for k in range(CONV_A_W):
                    off = r0 + HALO_A - (CONV_A_W - 1) + k
                    acc = acc + wa_ref[k:k + 1, lanes] * cvbuf[lb_i, off:off + chunk, :]
                ca_ref[rows, lanes] = acc.astype(BF16)
                y_ref[rows, lanes] = (u_ref[rows, lanes].astype(F32) * acc).astype(BF16)

                accb = jnp.zeros((chunk, LANES), F32)
                for k in range(CONV_B_W):
                    off = r0 + HALO_B - (CONV_B_W - 1) + k
                    accb = accb + wb_ref[k:k + 1, lanes] * zbuf[lb_i, off:off + chunk, :]
                zcbuf[rows, lanes] = accb + bb_ref[:, lanes]

            zbuf[lb_i, 0:HALO_B, :] = zbuf[lb_i, tc:tc + HALO_B, :]
            cvbuf[lb_i, 0:HALO_A, :] = cvbuf[lb_i, tc:tc + HALO_A, :]

        cols = slice(D_A - piece, D_A)
        s = s + _dot(y_ref[:, cols], wo_ref[cols, :])

        for c in range(n_chunks):
            rows = slice(c * chunk, (c + 1) * chunk)
            zc = zcbuf[rows, :]
            zc_ref[rows, :] = zc.astype(BF16)
            mu = jnp.mean(zc, axis=-1, keepdims=True)
            xc = zc - mu
            var = jnp.mean(xc * xc, axis=-1, keepdims=True)
            ln = xc * lax.rsqrt(var + LN_EPS) * lg_ref[...] + lb_ref[...]
            y_ref[rows, D_A:D_A + D_B] = (ln * _sigmoid(ln)).astype(BF16)

        s = s + _dot(y_ref[:, D_A:D_A + D_B], wo_ref[D_A:D_A + D_B, :])
        s_ref[...] = s.astype(BF16)
        x1_ref[...] = x_ref[...] + _rms_fwd(s, go_ref[...])

    return _hosted_call(
        core, comm, name, t // tc,
        in_specs=[_tile(tc, D_IN_ALL), _whole((CONV_A_W, D_A)), _whole((CONV_B_W, D_B)), _whole((1, D_B)),
                  _whole((1, D_B)), _whole((1, D_B)), _tile(tc, d), _resident((D_A + D_B, d)), _whole((1, d))],
        out_specs=[_tile(tc, D_A + D_B), _tile(tc, D_B), _tile(tc, D_A), _tile(tc, d), _tile(tc, d)],
        out_shape=[_sds((t, D_A + D_B), BF16), _sds((t, D_B), BF16), _sds((t, D_A), BF16), _sds((t, d), F32),
                   _sds((t, d), BF16)],
        scratch_shapes=[pltpu.VMEM((D_B // LANES, HALO_B + tc, LANES), F32),
                        pltpu.VMEM((D_A // LANES, HALO_A + tc, LANES), F32), pltpu.VMEM((tc, D_B), F32)],
        operands=(u, wa, wb, bb, lg, lb, xres, w_out, g_post))


def _mem_fwd(mem, g, wk, wv, name):
    m, d = mem.shape

    def body(mem_ref, g_ref, wk_ref, wv_ref, n_ref, k_ref, v_ref):
        n = _rms_fwd(mem_ref[...], g_ref[...]).astype(BF16)
        n_ref[...] = n
        k_ref[...] = _dot(n, wk_ref[...]).astype(BF16)
        v_ref[...] = _dot(n, wv_ref[...]).astype(BF16)

    return pl.pallas_call(
        body, name=name, grid=(1,),
        in_specs=[_whole((m, d)), _whole((1, d)), _whole((d, d)), _whole((d, d))],
        out_specs=[_whole((m, d))] * 3,
        out_shape=[_sds((m, d), BF16)] * 3,
        compiler_params=_params(),
    )(mem, g, wk, wv)


def _softmax_rows(s):
    e = jnp.exp(s - jnp.max(s, axis=-1, keepdims=True))
    return e / jnp.sum(e, axis=-1, keepdims=True)


def _attn_fwd(x1, g_pre, wq, k, v, wo, g_post, tm, name):
    t, d = x1.shape
    m = k.shape[0]
    scale = XA_HEAD_DIM ** -0.5

    def body(x_ref, gp_ref, wq_ref, k_ref, v_ref, wo_ref, go_ref, x2_ref, h_ref, q_ref, o_ref, a_ref):
        x = x_ref[...]
        h = _rms_fwd(x, gp_ref[...]).astype(BF16)
        h_ref[...] = h
        q_ref[...] = _dot(h, wq_ref[...]).astype(BF16)
        for hd in range(XA_HEADS):
            cols = slice(hd * XA_HEAD_DIM, (hd + 1) * XA_HEAD_DIM)
            p = _softmax_rows(_dot_nt(q_ref[:, cols], k_ref[:, cols]) * scale)
            o_ref[:, cols] = _dot(p.astype(BF16), v_ref[:, cols]).astype(BF16)
        a = _dot(o_ref[...], wo_ref[...])
        a_ref[...] = a.astype(BF16)
        x2_ref[...] = x + _rms_fwd(a, go_ref[...])

    return pl.pallas_call(
        body, name=name, grid=(t // tm,),
        in_specs=[_tile(tm, d), _whole((1, d)), _resident((d, d)), _whole((m, d)), _whole((m, d)), _resident((d, d)),
                  _whole((1, d))],
        out_specs=[_tile(tm, d)] * 5,
        out_shape=[_sds((t, d), F32)] + [_sds((t, d), BF16)] * 4,
        compiler_params=_params(),
    )(x1, g_pre, wq, k, v, wo, g_post)


def _ffn_fwd(x2, g_pre, wg_t, wu_t, wd, g_post, target, tm, name):
    t, d = x2.shape
    f = wg_t.shape[0]

    def body(x_ref, gp_ref, wg_ref, wu_ref, wd_ref, go_ref, tgt_ref, h_ref, gt_ref, up_ref, f_ref, sq_ref):
        _zero_at_first_step(sq_ref)
        x = x_ref[...]
        h = _rms_fwd(x, gp_ref[...]).astype(BF16)
        h_ref[...] = h
        gt = _dot_nt(h, wg_ref[...])
        up = _dot_nt(h, wu_ref[...])
        gt_ref[...] = gt.astype(BF16)
        up_ref[...] = up.astype(BF16)
        hd = (gt * _sigmoid(gt) * up).astype(BF16)
        ff = _dot(hd, wd_ref[...])
        f_ref[...] = ff.astype(BF16)
        err = x + _rms_fwd(ff, go_ref[...]) - tgt_ref[...]
        sq_ref[...] += jnp.sum(err * err, axis=0, keepdims=True)

    return pl.pallas_call(
        body, name=name, grid=(t // tm,),
        in_specs=[_tile(tm, d), _whole((1, d)), _resident((f, d)), _resident((f, d)), _resident((f, d)), _whole((1, d)),
                  _tile(tm, d)],
        out_specs=[_tile(tm, d), _tile(tm, f), _tile(tm, f), _tile(tm, d), _whole((1, d))],
        out_shape=[_sds((t, d), BF16), _sds((t, f), BF16), _sds((t, f), BF16), _sds((t, d), BF16), _sds((1, d), F32)],
        compiler_params=_params(),
    )(x2, g_pre, wg_t, wu_t, wd, g_post, target)


def _ffn_bwd(x2, f, target, gt, up, g_pre, wg_t, wu_t, wd, g_post, tm, name):
    t, d = x2.shape
    ff = wg_t.shape[0]

    def body(x_ref, f_ref, tgt_ref, gt_ref, up_ref, gp_ref, wg_ref, wu_ref, wd_ref, go_ref,
             dx_ref, df_ref, hd_ref, dgt_ref, dup_ref, dgo_ref, dgp_ref):
        _zero_at_first_step(dgo_ref, dgp_ref)
        x = x_ref[...]
        fo = f_ref[...].astype(F32)
        dx3 = (x + _rms_fwd(fo, go_ref[...]) - tgt_ref[...]) * (1.0 / d)
        df, dgo = _rms_bwd(dx3, fo, go_ref[...])
        dgo_ref[...] += dgo
        df = df.astype(BF16)
        df_ref[...] = df
        dhd = _dot_nt(df, wd_ref[...])
        gt = gt_ref[...].astype(F32)
        up = up_ref[...].astype(F32)
        sg = _sigmoid(gt)
        si = gt * sg
        hd_ref[...] = (si * up).astype(BF16)
        dup = (dhd * si).astype(BF16)
        dgt = (dhd * up * (sg * (1.0 + gt * (1.0 - sg)))).astype(BF16)
        dup_ref[...] = dup
        dgt_ref[...] = dgt
        dh = _dot(dgt, wg_ref[...]) + _dot(dup, wu_ref[...])
        dxn, dgp = _rms_bwd(dh, x, gp_ref[...])
        dgp_ref[...] += dgp
        dx_ref[...] = dx3 + dxn

    return pl.pallas_call(
        body, name=name, grid=(t // tm,),
        in_specs=[_tile(tm, d), _tile(tm, d), _tile(tm, d), _tile(tm, ff), _tile(tm, ff), _whole((1, d)),
                  _resident((ff, d)), _resident((ff, d)), _resident((ff, d)), _whole((1, d))],
        out_specs=[_tile(tm, d), _tile(tm, d), _tile(tm, ff), _tile(tm, ff), _tile(tm, ff), _whole((1, d)), _whole((1, d))],
        out_shape=[_sds((t, d), F32), _sds((t, d), BF16), _sds((t, ff), BF16), _sds((t, ff), BF16), _sds((t, ff), BF16),
                   _sds((1, d), F32), _sds((1, d), F32)],
        compiler_params=_params(),
    )(x2, f, target, gt, up, g_pre, wg_t, wu_t, wd, g_post)


def _attn_bwd(dx2, a, x1, q, k, v, g_pre, wq, wo, g_post, s_mix, w_mix_out, g_mix_post, tm, name, comm):
    t, d = x1.shape
    m = k.shape[0]
    kdim = w_mix_out.shape[0]
    scale = XA_HEAD_DIM ** -0.5

    def core(dx2_ref, a_ref, x_ref, q_ref, k_ref, v_ref, gp_ref, wq_ref, wo_ref, go_ref, s_ref, wm_ref, gm_ref,
             dx1_ref, da_ref, dq_ref, dk_ref, dv_ref, dgo_ref, dgp_ref, ds_ref, dy_ref, dgm_ref, do_buf):
        _zero_at_first_step(dgo_ref, dgp_ref, dk_ref, dv_ref, dgm_ref)
        dx2 = dx2_ref[...]
        da, dgo = _rms_bwd(dx2, a_ref[...].astype(F32), go_ref[...])
        dgo_ref[...] += dgo
        da = da.astype(BF16)
        da_ref[...] = da
        do_buf[...] = _dot_nt(da, wo_ref[...]).astype(BF16)
        for hd in range(XA_HEADS):
            cols = slice(hd * XA_HEAD_DIM, (hd + 1) * XA_HEAD_DIM)
            qh = q_ref[:, cols]
            p = _softmax_rows(_dot_nt(qh, k_ref[:, cols]) * scale)
            do_h = do_buf[:, cols]
            dp = _dot_nt(do_h, v_ref[:, cols])
            dv_ref[:, cols] += _dot_tn(p.astype(BF16), do_h)
            ds = (p * (dp - jnp.sum(dp * p, axis=-1, keepdims=True)) * scale).astype(BF16)
            dq_ref[:, cols] = _dot(ds, k_ref[:, cols]).astype(BF16)
            dk_ref[:, cols] += _dot_tn(ds, qh)
        dh = _dot_nt(dq_ref[...], wq_ref[...])
        dxn, dgp = _rms_bwd(dh, x_ref[...], gp_ref[...])
        dgp_ref[...] += dgp
        dx1 = dx2 + dxn
        dx1_ref[...] = dx1
        ds, dgm = _rms_bwd(dx1, s_ref[...].astype(F32), gm_ref[...])
        dgm_ref[...] += dgm
        ds = ds.astype(BF16)
        ds_ref[...] = ds
        dy_ref[...] = _dot_nt(ds, wm_ref[...]).astype(BF16)

    return _hosted_call(
        core, comm, name, t // tm,
        in_specs=[_tile(tm, d), _tile(tm, d), _tile(tm, d), _tile(tm, d), _whole((m, d)), _whole((m, d)), _whole((1, d)),
                  _resident((d, d)), _resident((d, d)), _whole((1, d)), _tile(tm, d), _resident((kdim, d)), _whole((1, d))],
        out_specs=[_tile(tm, d), _tile(tm, d), _tile(tm, d), _whole((m, d)), _whole((m, d)), _whole((1, d)), _whole((1, d)),
                   _tile(tm, d), _tile(tm, kdim), _whole((1, d))],
        out_shape=[_sds((t, d), F32), _sds((t, d), BF16), _sds((t, d), BF16), _sds((m, d), F32), _sds((m, d), F32),
                   _sds((1, d), F32), _sds((1, d), F32), _sds((t, d), BF16), _sds((t, kdim), BF16), _sds((1, d), F32)],
        scratch_shapes=[pltpu.VMEM((tm, d), BF16)],
        operands=(dx2, a, x1, q, k, v, g_pre, wq, wo, g_post, s_mix, w_mix_out, g_mix_post))


def _mem_bwd(mem, mem_n, dk, dv, g, wk, wv, name):
    m, d = mem.shape

    def body(mem_ref, n_ref, dk_ref, dv_ref, g_ref, wk_ref, wv_ref, dwk_ref, dwv_ref, dg_ref):
        dk = dk_ref[...].astype(BF16)
        dv = dv_ref[...].astype(BF16)
        n = n_ref[...]
        dwk_ref[...] = _dot_tn(n, dk).astype(BF16)
        dwv_ref[...] = _dot_tn(n, dv).astype(BF16)
        dn = _dot_nt(dk, wk_ref[...]) + _dot_nt(dv, wv_ref[...])
        _, dg = _rms_bwd(dn, mem_ref[...], g_ref[...])
        dg_ref[...] = dg

    return pl.pallas_call(
        body, name=name, grid=(1,),
        in_specs=[_whole((m, d)), _whole((m, d)), _whole((m, d)), _whole((m, d)), _whole((1, d)), _whole((d, d)),
                  _whole((d, d))],
        out_specs=[_whole((d, d)), _whole((d, d)), _whole((1, d))],
        out_shape=[_sds((d, d), BF16), _sds((d, d), BF16), _sds((1, d), F32)],
        compiler_params=_params(),
    )(mem, mem_n, dk, dv, g, wk, wv)


def _conv_bwd(dy, u, zc, ca, wa, wb, lg, lb, tc, name, comm):
    t = u.shape[0]
    n_tiles = t // tc
    n_chunks = tc // CONV_CHUNK

    def core(dy_ref, u_ref, zc_ref, ca_ref, wa_ref, wb_ref, lg_ref, lb_ref,
             du_ref, dwa_ref, dwb_ref, dbb_ref, dlg_ref, dlb_ref, ebuf, eabuf, zbuf, cvbuf, gatebuf, wacc, aacc, vacc):
        step = pl.program_id(0)

        @pl.when(step == 0)
        def _():
            ebuf[:, tc:tc + HALO_B, :] = jnp.zeros((D_B // LANES, HALO_B, LANES), F32)
            eabuf[:, tc:tc + HALO_A, :] = jnp.zeros((D_A // LANES, HALO_A, LANES), F32)
            wacc[...] = jnp.zeros_like(wacc)
            aacc[...] = jnp.zeros_like(aacc)
            vacc[...] = jnp.zeros_like(vacc)

        dbb = jnp.zeros((SUBLANES, D_B), F32)
        dlg = jnp.zeros((SUBLANES, D_B), F32)
        dlb = jnp.zeros((SUBLANES, D_B), F32)
        for c in range(n_chunks):
            rows = slice(c * CONV_CHUNK, (c + 1) * CONV_CHUNK)
            dy_a = dy_ref[rows, 0:D_A].astype(F32)
            b_a = u_ref[rows, 0:D_A].astype(F32)
            du_ref[rows, 0:D_A] = (dy_a * ca_ref[rows, :].astype(F32)).astype(BF16)
            dca = dy_a * b_a
            cv = u_ref[rows, D_A:2 * D_A].astype(F32) * u_ref[rows, 2 * D_A:3 * D_A].astype(F32)
            gate = _sigmoid(u_ref[rows, 3 * D_A + D_B:3 * D_A + 2 * D_B].astype(F32))
            z = u_ref[rows, 3 * D_A:3 * D_A + D_B].astype(F32) * gate
            for lb_i in range(D_A // LANES):
                lanes = slice(lb_i * LANES, (lb_i + 1) * LANES)
                eabuf[lb_i, rows, :] = dca[:, lanes]
                cvbuf[lb_i, rows, :] = cv[:, lanes]
                zbuf[lb_i, rows, :] = z[:, lanes]
                gatebuf[lb_i, rows, :] = gate[:, lanes]

            zcv = zc_ref[rows, :].astype(F32)
            mu = jnp.mean(zcv, axis=-1, keepdims=True)
            xc = zcv - mu
            rstd = lax.rsqrt(jnp.mean(xc * xc, axis=-1, keepdims=True) + LN_EPS)
            xhat = xc * rstd
            ln = xhat * lg_ref[...] + lb_ref[...]
            sg = _sigmoid(ln)
            dln = dy_ref[rows, D_A:D_A + D_B].astype(F32) * (sg * (1.0 + ln * (1.0 - sg)))
            dlg = dlg + jnp.sum((dln * xhat).reshape(CONV_CHUNK // SUBLANES, SUBLANES, D_B), axis=0)
            dlb = dlb + jnp.sum(dln.reshape(CONV_CHUNK // SUBLANES, SUBLANES, D_B), axis=0)
            dxh = dln * lg_ref[...]
            dzc = rstd * (dxh - jnp.mean(dxh, axis=-1, keepdims=True) - xhat * jnp.mean(dxh * xhat, axis=-1, keepdims=True))
            dbb = dbb + jnp.sum(dzc.reshape(CONV_CHUNK // SUBLANES, SUBLANES, D_B), axis=0)
            for lb_i in range(D_B // LANES):
                ebuf[lb_i, rows, :] = dzc[:, lb_i * LANES:(lb_i + 1) * LANES]
        vacc[0] += dbb
        vacc[1] += dlg
        vacc[2] += dlb

        for lb_i in range(D_A // LANES):
            lanes = slice(lb_i * LANES, (lb_i + 1) * LANES)

            def cols(first):
                return slice(first + lb_i * LANES, first + (lb_i + 1) * LANES)

            for c in range(n_chunks):
                r0 = c * CONV_CHUNK
                rows = slice(r0, r0 + CONV_CHUNK)
                cv = cvbuf[lb_i, rows, :]
                dcv = jnp.zeros((CONV_CHUNK, LANES), F32)
                for k in range(CONV_A_W):
                    off = r0 + (CONV_A_W - 1) - k
                    e = eabuf[lb_i, off:off + CONV_CHUNK, :]
                    dcv = dcv + wa_ref[k:k + 1, lanes] * e
                    aacc[k, :, lanes] += jnp.sum((cv * e).reshape(CONV_CHUNK // SUBLANES, SUBLANES, LANES), axis=0)
                du_ref[rows, cols(D_A)] = (dcv * u_ref[rows, cols(2 * D_A)].astype(F32)).astype(BF16)
                du_ref[rows, cols(2 * D_A)] = (dcv * u_ref[rows, cols(D_A)].astype(F32)).astype(BF16)

                z = zbuf[lb_i, rows, :]
                dz = jnp.zeros((CONV_CHUNK, LANES), F32)
                for k in range(CONV_B_W):
                    off = r0 + (CONV_B_W - 1) - k
                    e = ebuf[lb_i, off:off + CONV_CHUNK, :]
                    dz = dz + wb_ref[k:k + 1, lanes] * e
                    wacc[k, :, lanes] += jnp.sum((z * e).reshape(CONV_CHUNK // SUBLANES, SUBLANES, LANES), axis=0)
                glu_v = u_ref[rows, cols(3 * D_A)].astype(F32)
                sgg = gatebuf[lb_i, rows, :]
                du_ref[rows, cols(3 * D_A)] = (dz * sgg).astype(BF16)
                du_ref[rows, cols(3 * D_A + D_B)] = (dz * glu_v * sgg * (1.0 - sgg)).astype(BF16)

            ebuf[lb_i, tc:tc + HALO_B, :] = ebuf[lb_i, 0:HALO_B, :]
            eabuf[lb_i, tc:tc + HALO_A, :] = eabuf[lb_i, 0:HALO_A, :]

        @pl.when(step == n_tiles - 1)
        def _():
            for k in range(CONV_B_W):
                dwb_ref[k:k + 1, :] = jnp.sum(wacc[k], axis=0, keepdims=True)
            for k in range(CONV_A_W):
                dwa_ref[k:k + 1, :] = jnp.sum(aacc[k], axis=0, keepdims=True)
            dbb_ref[...] = jnp.sum(vacc[0], axis=0, keepdims=True)
            dlg_ref[...] = jnp.sum(vacc[1], axis=0, keepdims=True)
            dlb_ref[...] = jnp.sum(vacc[2], axis=0, keepdims=True)

    return _hosted_call(
        core, comm, name, n_tiles,
        in_specs=[_rtile(tc, D_A + D_B, n_tiles), _rtile(tc, D_IN_ALL, n_tiles), _rtile(tc, D_B, n_tiles),
                  _rtile(tc, D_A, n_tiles), _whole((CONV_A_W, D_A)), _whole((CONV_B_W, D_B)), _whole((1, D_B)),
                  _whole((1, D_B))],
        out_specs=[_rtile(tc, D_IN_ALL, n_tiles), _whole((CONV_A_W, D_A)), _whole((CONV_B_W, D_B)), _whole((1, D_B)),
                   _whole((1, D_B)), _whole((1, D_B))],
        out_shape=[_sds((t, D_IN_ALL), BF16), _sds((CONV_A_W, D_A), F32), _sds((CONV_B_W, D_B), F32), _sds((1, D_B), F32),
                   _sds((1, D_B), F32), _sds((1, D_B), F32)],
        scratch_shapes=[pltpu.VMEM((D_B // LANES, tc + HALO_B, LANES), F32), pltpu.VMEM((D_A // LANES, tc + HALO_A, LANES), F32),
                        pltpu.VMEM((D_B // LANES, tc, LANES), F32), pltpu.VMEM((D_A // LANES, tc, LANES), F32),
                        pltpu.VMEM((D_B // LANES, tc, LANES), F32),
                        pltpu.VMEM((CONV_B_W, SUBLANES, D_B), F32), pltpu.VMEM((CONV_A_W, SUBLANES, D_A), F32),
                        pltpu.VMEM((3, SUBLANES, D_B), F32)],
        operands=(dy, u, zc, ca, wa, wb, lg, lb))


def _in_bwd(du, w_t, x, dx1, g, tm, name):
    t, d = x.shape
    n = w_t.shape[0]

    def body(du_ref, w_ref, x_ref, dx1_ref, g_ref, dx_ref, dg_ref):
        _zero_at_first_step(dg_ref)
        dh = _dot(du_ref[...], w_ref[...])
        dxn, dg = _rms_bwd(dh, x_ref[...], g_ref[...])
        dg_ref[...] += dg
        dx_ref[...] = dx1_ref[...] + dxn

    return pl.pallas_call(
        body, name=name, grid=(t // tm,),
        in_specs=[_tile(tm, n), _resident((n, d)), _tile(tm, d), _tile(tm, d), _whole((1, d))],
        out_specs=[_tile(tm, d), _whole((1, d))],
        out_shape=[_sds((t, d), F32), _sds((1, d), F32)],
        compiler_params=_params(),
    )(du, w_t, x, dx1, g)


def _wgrad(a, b, tk, bm, bn, name, comm=None, cols=None):
    t, m = a.shape
    first, n = (0, b.shape[1]) if cols is None else cols
    first_block = first // bn
    n_k = t // tk

    def body(a_ref, b_ref, o_ref, acc):
        @pl.when(pl.program_id(2) == 0)
        def _():
            acc[...] = jnp.zeros_like(acc)

        acc[...] += _dot_tn(a_ref[...], b_ref[...])

        @pl.when(pl.program_id(2) == n_k - 1)
        def _():
            o_ref[...] = acc[...].astype(BF16)

    call = dict(
        grid=(m // bm, n // bn, n_k),
        in_specs=[pl.BlockSpec((tk, bm), lambda i, j, k: (k, i)),
                  pl.BlockSpec((tk, bn), lambda i, j, k: (k, first_block + j))],
        out_specs=[pl.BlockSpec((bm, bn), lambda i, j, k: (i, j))],
        out_shape=[_sds((m, n), BF16)],
        scratch_shapes=[pltpu.VMEM((bm, bn), F32)])
    if comm is None:
        return pl.pallas_call(body, name=name, compiler_params=_params(3), **call)(a, b)[0]
    (out,), received = _hosted_call(body, comm, name, operands=(a, b), **call)
    return out, received


def _adamw(items, tr, name, after=None):
    n_parts, r, c = items[0][0].shape
    n_items = len(items)
    order = [] if after is None else [after]

    def body(*refs):
        refs = refs[len(order):]
        ins, outs = refs[:4 * n_items], refs[4 * n_items:]
        for i in range(n_items):
            p_ref, w_ref, m_ref, v_ref = ins[4 * i:4 * i + 4]
            g_ref, d_ref, nm_ref, nv_ref = outs[4 * i:4 * i + 4]
            g = p_ref[0].astype(F32)
            for j in range(1, n_parts):
                g = g + p_ref[j].astype(F32)
            g_ref[...] = g
            nm = ADAM_B1 * m_ref[...] + (1.0 - ADAM_B1) * g
            nv = ADAM_B2 * v_ref[...] + (1.0 - ADAM_B2) * (g * g)
            nm_ref[...] = nm
            nv_ref[...] = nv
            m_hat = nm / (1.0 - ADAM_B1 ** ADAM_STEP)
            v_hat = nv / (1.0 - ADAM_B2 ** ADAM_STEP)
            d_ref[...] = -ADAM_LR * (m_hat / (jnp.sqrt(v_hat) + ADAM_EPS) + ADAM_WD * w_ref[...])

    results = pl.pallas_call(
        body, name=name, grid=(r // tr,),
        in_specs=[pl.BlockSpec(memory_space=pl.ANY)] * len(order)
        + ([pl.BlockSpec((n_parts, tr, c), lambda i: (0, i, 0))] + [_tile(tr, c)] * 3) * n_items,
        out_specs=[_tile(tr, c)] * (4 * n_items),
        out_shape=[_sds((r, c), F32)] * (4 * n_items),
        compiler_params=_params(),
    )(*order, *[a for item in items for a in item])
    return [results[4 * i:4 * i + 4] for i in range(n_items)]


def _sum_parts(parts, name):
    n_parts, r, c = parts.shape

    def body(p_ref, o_ref):
        acc = p_ref[0]
        for j in range(1, n_parts):
            acc = acc + p_ref[j]
        o_ref[...] = acc

    return pl.pallas_call(
        body, name=name, grid=(1,),
        in_specs=[_whole((n_parts, r, c))], out_specs=_whole((r, c)), out_shape=_sds((r, c), F32),
        compiler_params=_params(),
    )(parts)


def _row(v):
    return v.reshape(1, -1)


def _by_owner_rows(g):
    return g.reshape(N_DEV, g.shape[0] // N_DEV, g.shape[1])


WEIGHTS = ("mix_pre_g", "w_mix_in", "conv_a_w", "conv_b_w", "conv_b_b", "ln_b_g", "ln_b_b", "w_mix_out", "mix_post_g",
           "xa_pre_g", "mem_norm_g", "w_q", "w_k", "w_v", "w_o", "xa_post_g", "ffn_pre_g", "w_gate", "w_up", "w_down",
           "ffn_post_g")
LARGE = ("w_mix_in", "w_mix_out", "w_q", "w_k", "w_v", "w_o", "w_gate", "w_up", "w_down")
COLUMN_SHARDED = ("w_mix_in", "w_gate", "w_up")
GAINS = ("mix_pre_g", "mix_post_g", "xa_pre_g", "mem_norm_g", "xa_post_g", "ffn_pre_g", "ffn_post_g")
CHANNEL_VECTORS = ("conv_b_b", "ln_b_g", "ln_b_b")
CONV_TAPS = ("conv_a_w", "conv_b_w")
SMALL = GAINS + CHANNEL_VECTORS + CONV_TAPS
CONV_COLS_PER_DEVICE = D_A // N_DEV
TOKEN_TILE = 512
FFN_FWD_TOKEN_TILE = 512
FFN_TOKEN_TILE = 256
CONV_TOKEN_TILE = 256
WGRAD_TOKEN_TILE = 2048
SMALL_WGRAD_TOKEN_TILE = 4096
IN_BWD_TOKEN_TILE = 1024
ADAM_ROWS_PER_STEP = 64


def _lane_rows(v):
    flat = v.reshape(-1)
    tile = SUBLANES * LANES
    flat = jnp.pad(flat, (0, (-flat.shape[0]) % tile))
    return flat.reshape(-1, LANES)


def _pack_small(values, names):
    return jnp.concatenate([_lane_rows(values[n]) for n in names], axis=0)


def _unpack_small(packed, like, names):
    out, off = {}, 0
    for n in names:
        size = like[n].size
        rows = _lane_rows(like[n]).shape[0]
        out[n] = packed[off:off + rows, :].reshape(-1)[:size].reshape(like[n].shape)
        off += rows
    return out


def kernel(x, mem, mix_pre_g, w_mix_in, conv_a_w, conv_b_w, conv_b_b, ln_b_g, ln_b_b, w_mix_out, mix_post_g, xa_pre_g, mem_norm_g, w_q, w_k, w_v, w_o, xa_post_g, ffn_pre_g, w_gate, w_up, w_down, ffn_post_g, loss_target, m_mix_pre_g, m_w_mix_in, m_conv_a_w, m_conv_b_w, m_conv_b_b, m_ln_b_g, m_ln_b_b, m_w_mix_out, m_mix_post_g, m_xa_pre_g, m_mem_norm_g, m_w_q, m_w_k, m_w_v, m_w_o, m_xa_post_g, m_ffn_pre_g, m_w_gate, m_w_up, m_w_down, m_ffn_post_g, v_mix_pre_g, v_w_mix_in, v_conv_a_w, v_conv_b_w, v_conv_b_b, v_ln_b_g, v_ln_b_b, v_w_mix_out, v_mix_post_g, v_xa_pre_g, v_mem_norm_g, v_w_q, v_w_k, v_w_v, v_w_o, v_xa_post_g, v_ffn_pre_g, v_w_gate, v_w_up, v_w_down, v_ffn_post_g):
    given = dict(locals())
    w = {n: given[n] for n in WEIGHTS}
    m = {n: given["m_" + n] for n in WEIGHTS}
    v = {n: given["v_" + n] for n in WEIGHTS}
    xs, mems, target = x[0], mem[0], loss_target[0]
    t = xs.shape[0]
    tm, tm_ffn, tc, tk = min(TOKEN_TILE, t), min(FFN_TOKEN_TILE, t), min(CONV_TOKEN_TILE, t), min(WGRAD_TOKEN_TILE, t)
    tk_small = min(SMALL_WGRAD_TOKEN_TILE, t)
    g = {n: _row(w[n]) for n in GAINS}
    bb, lg, lb = (_row(w[n]) for n in CHANNEL_VECTORS)

    def shard_bf16(*names):
        return [w[n].astype(BF16) for n in names]

    def shard_bf16_t(*names):
        return [w[n].T.astype(BF16) for n in names]

    g_mix_in, g_taps = _comm_call(_Gather(shard_bf16_t("w_mix_in") + [_pack_small(w, CONV_TAPS)]), "gather_mixer")
    w_mix_in_t = g_mix_in.reshape(D_IN_ALL, D_MODEL)
    taps, off = {}, 0
    for n in CONV_TAPS:
        k, cols = w[n].shape
        rows = _lane_rows(w[n]).shape[0]
        blk = g_taps[:, off:off + rows, :].reshape(N_DEV, -1)[:, :k * cols].reshape(N_DEV, k, cols)
        taps[n] = blk.transpose(1, 0, 2).reshape(k, N_DEV * cols)
        off += rows
    wa, wb = taps["conv_a_w"], taps["conv_b_w"]

    (h1, u), gathered = _norm_matmul(xs, g["mix_pre_g"], w_mix_in_t.T, tm, "mix_in_fwd",
                                     _Gather(shard_bf16("w_mix_out", "w_q", "w_k", "w_v", "w_o")))
    w_mix_out, w_q, w_k, w_v, w_o = (a.reshape(D_MODEL, D_MODEL) for a in gathered)
    (ycat, zc, ca, x1, s1), (g_gate, g_up, g_down) = _conv_fwd(
        u, wa, wb, bb, lg, lb, xs, w_mix_out, g["mix_post_g"], tc, "conv_fwd",
        _Gather(shard_bf16_t("w_gate", "w_up") + shard_bf16("w_down")))
    w_gate_t, w_up_t, w_down = (a.reshape(D_FF, D_MODEL) for a in (g_gate, g_up, g_down))
    mem_n, kk, vv = _mem_fwd(mems, g["mem_norm_g"], w_k, w_v, "mem_fwd")
    x2, h2, q, o, a = _attn_fwd(x1, g["xa_pre_g"], w_q, kk, vv, w_o, g["xa_post_g"], tm, "attn_fwd")
    h3, gt, up, f, sq = _ffn_fwd(x2, g["ffn_pre_g"], w_gate_t, w_up_t, w_down, g["ffn_post_g"], target,
                                 min(FFN_FWD_TOKEN_TILE, t), "ffn_fwd")

    dx2, df, hd, dgt, dup, d_ffn_post, d_ffn_pre = _ffn_bwd(
        x2, f, target, gt, up, g["ffn_pre_g"], w_gate_t, w_up_t, w_down, g["ffn_post_g"], tm_ffn, "ffn_bwd")
    d_w_down = _wgrad(hd, df, tk, D_FF // 2, D_MODEL, "wgrad_down")
    d_w_gate_t = _wgrad(dgt, h3, tk, D_FF // 2, D_MODEL, "wgrad_gate")
    d_w_up_t = _wgrad(dup, h3, tk, D_FF // 2, D_MODEL, "wgrad_up")
    ffn_slabs = [_by_owner_rows(d) for d in (d_w_gate_t, d_w_up_t, d_w_down)]

    (dx1, da, dq, dk, dv, d_xa_post, d_xa_pre, ds1, dycat, d_mix_post), from_ffn = _attn_bwd(
        dx2, a, x1, q, kk, vv, g["xa_pre_g"], w_q, w_o, g["xa_post_g"], s1, w_mix_out, g["mix_post_g"], tm, "attn_bwd",
        _Exchange(ffn_slabs, [True] * 3))
    d_w_o = _wgrad(o, da, tk, D_MODEL, D_MODEL, "wgrad_o")
    d_w_q = _wgrad(h2, dq, tk, D_MODEL, D_MODEL, "wgrad_q")
    d_w_k, d_w_v, d_mem_norm = _mem_bwd(mems, mem_n, dk, dv, g["mem_norm_g"], w_k, w_v, "mem_bwd")
    d_w_mix_out = _wgrad(ycat, ds1, tk, D_MODEL, D_MODEL, "wgrad_mix_out")
    attn_slabs = [_by_owner_rows(d) for d in (d_w_mix_out, d_w_q, d_w_k, d_w_v, d_w_o)]

    (du, d_conv_a, d_conv_b, d_conv_bb, d_ln_g, d_ln_b), from_attn = _conv_bwd(
        dycat, u, zc, ca, wa, wb, lg, lb, tc, "conv_bwd", _Exchange(attn_slabs, [True] * 5))
    small_grads = dict(conv_a_w=d_conv_a, conv_b_w=d_conv_b, conv_b_b=d_conv_bb, ln_b_g=d_ln_g, ln_b_b=d_ln_b,
                       mix_post_g=d_mix_post, xa_pre_g=d_xa_pre, mem_norm_g=d_mem_norm, xa_post_g=d_xa_post,
                       ffn_pre_g=d_ffn_pre, ffn_post_g=d_ffn_post, loss=sq)
    names = tuple(n for n in SMALL if n != "mix_pre_g") + ("loss", "mix_pre_g")
    half = D_MODEL // 2
    d_in_lo = _wgrad(du, h1, tk_small, D_IN_ALL // 2, half, "wgrad_mix_in_lo", cols=(0, half))
    d_in_hi, (from_in_lo, early_small) = _wgrad(
        du, h1, tk_small, D_IN_ALL // 2, half, "wgrad_mix_in_hi",
        _Exchange([_by_owner_rows(d_in_lo), _pack_small(small_grads, names[:-1])], [True, False]), cols=(half, half))
    dx, d_mix_pre = _in_bwd(du, w_mix_in_t, xs, dx1, g["mix_pre_g"], min(IN_BWD_TOKEN_TILE, t), "mix_in_bwd")
    small_grads["mix_pre_g"] = d_mix_pre
    tail_scatter = [True, False]
    tail, started = _exchange_start([_by_owner_rows(d_in_hi), _lane_rows(d_mix_pre)], tail_scatter, "tail_start")

    received = dict(zip(("w_gate", "w_up", "w_down"), from_ffn))
    received.update(zip(("w_mix_out", "w_q", "w_k", "w_v", "w_o"), from_attn))
    grad, delta, new_m, new_v = {}, {}, {}, {}

    def adamw_group(group, after=None):
        items = [[received[n]] + [a[n].T if n in COLUMN_SHARDED else a[n] for a in (w, m, v)] for n in group]
        rows = items[0][1].shape[0]
        results = _adamw(items, min(rows, ADAM_ROWS_PER_STEP) if len(group) > 1 else rows, "adamw_" + group[0], after)
        for n, result in zip(group, results):
            grad[n], delta[n], new_m[n], new_v[n] = [r.T if n in COLUMN_SHARDED else r for r in result]
        return results[-1][-1]

    behind = started
    for group in (("w_gate",), ("w_up",), ("w_down",), ("w_mix_out", "w_q", "w_k", "w_v", "w_o")):
        behind = adamw_group(group, behind)
    from_in_hi, late_small = _exchange_wait(tail, tail_scatter, behind, "tail_wait")
    all_small = jnp.concatenate([early_small, late_small], axis=1)
    received["w_mix_in"] = jnp.concatenate([from_in_lo, from_in_hi], axis=2)
    adamw_group(("w_mix_in",))

    total = _unpack_small(_sum_parts(all_small, "sum_small"), small_grads, names)
    loss = jnp.sum(total.pop("loss")) * (0.5 / D_MODEL)
    first_col = _device_index() * CONV_COLS_PER_DEVICE
    for n in CONV_TAPS:
        total[n] = lax.dynamic_slice_in_dim(total[n], first_col, CONV_COLS_PER_DEVICE, axis=1)
    total = {n: total[n].reshape(w[n].shape) for n in SMALL}
    packed_small = [_pack_small(values, SMALL) for values in (total, w, m, v)]
    ((g_s, d_s, nm_s, nv_s),) = _adamw([[packed_small[0][None]] + packed_small[1:]], packed_small[0].shape[0],
                                       "adamw_small")
    for out, p in ((grad, g_s), (delta, d_s), (new_m, nm_s), (new_v, nv_s)):
        out.update(_unpack_small(p, w, SMALL))

    return (loss, dx[None], *[grad[n] for n in WEIGHTS], *[delta[n] for n in WEIGHTS], *[new_m[n] for n in WEIGHTS],
            *[new_v[n] for n in WEIGHTS])
```

```python
import jax
import jax.numpy as jnp
from jax import lax
from jax.experimental import pallas as pl
from jax.experimental.pallas import tpu as pltpu

F32 = jnp.float32
BF16 = jnp.bfloat16

D_MODEL = 1024
D_A = 512
D_B = 512
D_IN_ALL = 2560
CONV_A_W = 3
CONV_B_W = 31
XA_HEADS = 4
XA_HEAD_DIM = 256
D_FF = 2816
N_DEV = 8
RMS_EPS = 1e-6
LN_EPS = 1e-5
ADAM_LR = 0.001
ADAM_B1 = 0.9
ADAM_B2 = 0.999
ADAM_EPS = 1e-08
ADAM_WD = 0.01
ADAM_STEP = 10

VMEM_LIMIT_BYTES = 56 * 1024 * 1024
SUBLANES = 8
LANES = 128
HALO_B = 32
HALO_A = 8
CONV_FWD_CHUNK = 16
CONV_CHUNK = 32
GATHER_FORWARD_STEPS_BEFORE_END = 8

MESH = pl.DeviceIdType.MESH


def _params(n_grid_axes=1):
    return pltpu.CompilerParams(dimension_semantics=("arbitrary",) * n_grid_axes, vmem_limit_bytes=VMEM_LIMIT_BYTES)


def _sds(shape, dtype):
    return jax.ShapeDtypeStruct(shape, dtype)


def _tile(rows, cols):
    return pl.BlockSpec((rows, cols), lambda i: (i, 0))


def _rtile(rows, cols, n):
    return pl.BlockSpec((rows, cols), lambda i: (n - 1 - i, 0))


def _whole(shape):
    zeros = (0,) * len(shape)
    return pl.BlockSpec(shape, lambda i: zeros)


def _resident(shape):
    zeros = (0,) * len(shape)
    return pl.BlockSpec(shape, lambda i: zeros, pipeline_mode=pl.Buffered(1))


def _dot(a, b):
    return jnp.dot(a, b, preferred_element_type=F32)


def _dot_nt(a, b):
    return lax.dot_general(a, b, (((1,), (1,)), ((), ())), preferred_element_type=F32)


def _dot_tn(a, b):
    return lax.dot_general(a, b, (((0,), (0,)), ((), ())), preferred_element_type=F32)


def _sigmoid(x):
    return 1.0 / (1.0 + jnp.exp(-x))


def _rms_fwd(x, g):
    r = lax.rsqrt(jnp.mean(x * x, axis=-1, keepdims=True) + RMS_EPS)
    return x * r * g


def _rms_bwd(dy, xin, g):
    r = lax.rsqrt(jnp.mean(xin * xin, axis=-1, keepdims=True) + RMS_EPS)
    n = xin * r
    dg = jnp.sum(dy * n, axis=0, keepdims=True)
    dn = dy * g
    dx = r * (dn - n * jnp.mean(dn * n, axis=-1, keepdims=True))
    return dx, dg


def _zero_at_first_step(*refs):
    @pl.when(pl.program_id(0) == 0)
    def _():
        for ref in refs:
            ref[...] = jnp.zeros(ref.shape, ref.dtype)


def _place():
    return lax.axis_index("x"), lax.axis_index("y"), lax.axis_index("c")


def _device_index():
    x, y, c = _place()
    return 4 * x + 2 * y + c


class _Gather:
    def __init__(self, arrays):
        self.arrays = list(arrays)
        self.out_shape = [_sds((N_DEV, *a.shape), a.dtype) for a in self.arrays]
        n = len(self.arrays)
        self.scratch_shapes = [pltpu.SemaphoreType.DMA((n, 7)), pltpu.SemaphoreType.DMA((n, 7)),
                               pltpu.SemaphoreType.DMA((n,))]

    def forward_step(self, n_steps):
        return max(0, n_steps - 1 - GATHER_FORWARD_STEPS_BEFORE_END)

    def bind(self, srcs, dsts, send_sems, recv_sems, local_sems):
        x, y, cc = _place()
        me, sibling = (x, y, cc), (x, y, 1 - cc)
        chips = [(1 - x, y), (x, 1 - y), (1 - x, 1 - y)]

        def copy(a, k, owner, to, src=None):
            slot = dsts[a].at[4 * owner[0] + 2 * owner[1] + owner[2]]
            return pltpu.make_async_remote_copy(
                src_ref=slot if src is None else src, dst_ref=slot, send_sem=send_sems.at[a, k],
                recv_sem=recv_sems.at[a, k], device_id=to, device_id_type=MESH)

        def first(a):
            return [copy(a, 0, me, sibling, src=srcs[a])] + [
                copy(a, 1 + j, me, (*chip, cc), src=srcs[a]) for j, chip in enumerate(chips)]

        def passed(a, j):
            return copy(a, 4 + j, (*chips[j], cc), sibling)

        def mine(a):
            return pltpu.make_async_copy(srcs[a], dsts[a].at[4 * x + 2 * y + cc], local_sems.at[a])

        def start():
            for a in range(len(srcs)):
                mine(a).start()
                for cp in first(a):
                    cp.start()

        def forward():
            for a in range(len(srcs)):
                for j, chip in enumerate(chips):
                    copy(a, 1 + j, (*chip, cc), me).wait_recv()
                    passed(a, j).start()

        def finish():
            for a in range(len(srcs)):
                copy(a, 0, sibling, me).wait_recv()
                for j, chip in enumerate(chips):
                    copy(a, 4 + j, (*chip, 1 - cc), me).wait_recv()
                for cp in first(a) + [passed(a, j) for j in range(len(chips))]:
                    cp.wait_send()
                mine(a).wait()

        return start, forward, finish


class _Exchange:
    def __init__(self, arrays, scatter):
        self.arrays = list(arrays)
        self.scatter = list(scatter)
        self.out_shape = [_sds(a.shape if s else (N_DEV, *a.shape), a.dtype) for a, s in zip(self.arrays, self.scatter)]
        n = len(self.arrays)
        self.scratch_shapes = [pltpu.SemaphoreType.DMA((n, 7)), pltpu.SemaphoreType.DMA((n, 7)),
                               pltpu.SemaphoreType.DMA((n,))]

    def forward_step(self, n_steps):
        return n_steps - 1

    def bind(self, srcs, dsts, send_sems, recv_sems, local_sems):
        me = _device_index()

        def copies(a):
            out = []
            for k in range(1, N_DEV):
                p = me ^ k
                out.append(pltpu.make_async_remote_copy(
                    src_ref=srcs[a].at[p] if self.scatter[a] else srcs[a], dst_ref=dsts[a].at[me],
                    send_sem=send_sems.at[a, k - 1], recv_sem=recv_sems.at[a, k - 1],
                    device_id=(p >> 2, (p >> 1) & 1, p & 1), device_id_type=MESH))
            return out

        def mine(a):
            return pltpu.make_async_copy(srcs[a].at[me] if self.scatter[a] else srcs[a], dsts[a].at[me], local_sems.at[a])

        def start():
            for a in range(len(srcs)):
                mine(a).start()
                for cp in copies(a):
                    cp.start()

        def forward():
            pass

        def finish():
            for a in range(len(srcs)):
                for cp in copies(a):
                    cp.wait()
                mine(a).wait()

        return start, forward, finish


def _hosted_call(core, comm, name, grid, in_specs, out_specs, out_shape, scratch_shapes, operands):
    grid = (grid,) if isinstance(grid, int) else tuple(grid)
    n_steps = 1
    for extent in grid:
        n_steps *= extent
    n_in, n_out, n_scr, n_arr = len(in_specs), len(out_specs), len(scratch_shapes), len(comm.arrays)
    any_spec = pl.BlockSpec(memory_space=pl.ANY)

    def body(*refs):
        ins, refs = refs[:n_in], refs[n_in:]
        srcs, refs = refs[:n_arr], refs[n_arr:]
        outs, refs = refs[:n_out], refs[n_out:]
        dsts, refs = refs[:n_arr], refs[n_arr:]
        scratch, sems = refs[:n_scr], refs[n_scr:]
        start, forward, finish = comm.bind(srcs, dsts, *sems)
        step = pl.program_id(0)
        for axis in range(1, len(grid)):
            step = step * grid[axis] + pl.program_id(axis)
        pl.when(step == 0)(start)
        core(*ins, *outs, *scratch)
        pl.when(step == comm.forward_step(n_steps))(forward)
        pl.when(step == n_steps - 1)(finish)

    results = pl.pallas_call(
        body, name=name, grid=grid,
        in_specs=list(in_specs) + [any_spec] * n_arr,
        out_specs=list(out_specs) + [any_spec] * n_arr,
        out_shape=list(out_shape) + comm.out_shape,
        scratch_shapes=list(scratch_shapes) + comm.scratch_shapes,
        compiler_params=_params(len(grid)),
    )(*operands, *comm.arrays)
    return results[:n_out], results[n_out:]


def _comm_call(comm, name):
    return _hosted_call(lambda: None, comm, name, 1, [], [], [], [], [])[1]


def _split_exchange_copies(srcs, lands, scatter, send_sems, recv_sems):
    me = _device_index()
    out = []
    for a in range(len(srcs)):
        for k in range(1, N_DEV):
            p = me ^ k
            out.append(pltpu.make_async_remote_copy(
                src_ref=srcs[a].at[p] if scatter[a] else srcs[a], dst_ref=lands[a].at[me],
                send_sem=send_sems[a].at[k - 1], recv_sem=recv_sems[a].at[k - 1],
                device_id=(p >> 2, (p >> 1) & 1, p & 1), device_id_type=MESH))
    return out


def _exchange_start(arrays, scatter, name):
    n = len(arrays)
    me = _device_index()
    lands = []
    for a, s in zip(arrays, scatter):
        own = lax.dynamic_slice_in_dim(a, me, 1, axis=0) if s else a[None]
        empty = jnp.zeros(a.shape if s else (N_DEV, *a.shape), a.dtype)
        lands.append(lax.dynamic_update_slice_in_dim(empty, own, me, axis=0))

    def body(*refs):
        srcs, zones = refs[:n], refs[n:2 * n]
        send_sems, recv_sems = refs[2 * n:3 * n], refs[3 * n:4 * n]
        for cp in _split_exchange_copies(srcs, zones, scatter, send_sems, recv_sems):
            cp.start()
        refs[-1][...] = jnp.zeros(refs[-1].shape, F32)

    hbm = pl.BlockSpec(memory_space=pltpu.HBM)
    sem = pl.BlockSpec(memory_space=pltpu.SEMAPHORE)
    operands = [pltpu.with_memory_space_constraint(a, pltpu.HBM) for a in list(arrays) + lands]
    results = pl.pallas_call(
        body, name=name,
        out_shape=(*[pltpu.SemaphoreType.DMA((N_DEV - 1,))] * (2 * n), *[pltpu.HBM(a.shape, a.dtype) for a in operands],
                   _sds((SUBLANES, LANES), F32)),
        in_specs=[hbm] * (2 * n),
        out_specs=(*[sem] * (2 * n), *[hbm] * (2 * n), pl.BlockSpec(memory_space=pltpu.VMEM)),
        input_output_aliases={i: 2 * n + i for i in range(2 * n)},
        compiler_params=pltpu.CompilerParams(has_side_effects=pltpu.SideEffectType.DATAFLOW_SIDE_EFFECTING),
    )(*operands)
    return (results[:n], results[n:2 * n], results[2 * n:3 * n], results[3 * n:4 * n]), results[-1]


def _exchange_wait(started, scatter, after, name):
    send_sems, recv_sems, srcs, lands = started
    n = len(srcs)

    def body(*refs):
        src_refs, zones = refs[:n], refs[n:2 * n]
        send, recv = refs[2 * n:3 * n], refs[3 * n:4 * n]
        for cp in _split_exchange_copies(src_refs, zones, scatter, send, recv):
            cp.wait_send()
            cp.wait_recv()

    hbm = pl.BlockSpec(memory_space=pltpu.HBM)
    sem = pl.BlockSpec(memory_space=pltpu.SEMAPHORE)
    results = pl.pallas_call(
        body, name=name,
        out_shape=tuple(pltpu.HBM(a.shape, a.dtype) for a in list(srcs) + list(lands)),
        in_specs=[hbm] * (2 * n) + [sem] * (2 * n) + [pl.BlockSpec(memory_space=pl.ANY)],
        out_specs=tuple([hbm] * (2 * n)),
        input_output_aliases={i: i for i in range(2 * n)},
        compiler_params=pltpu.CompilerParams(has_side_effects=pltpu.SideEffectType.DATAFLOW_SIDE_EFFECTING),
    )(*srcs, *lands, *send_sems, *recv_sems, after)
    return results[n:]


def _norm_matmul(x, g, w, tm, name, comm):
    t, d = x.shape
    n = w.shape[1]

    def core(x_ref, g_ref, w_ref, h_ref, o_ref):
        h = _rms_fwd(x_ref[...], g_ref[...]).astype(BF16)
        h_ref[...] = h
        o_ref[...] = _dot(h, w_ref[...]).astype(BF16)

    return _hosted_call(
        core, comm, name, t // tm,
        in_specs=[_tile(tm, d), _whole((1, d)), _resident((d, n))],
        out_specs=[_tile(tm, d), _tile(tm, n)],
        out_shape=[_sds((t, d), BF16), _sds((t, n), BF16)],
        scratch_shapes=[], operands=(x, g, w))


def _conv_fwd(u, wa, wb, bb, lg, lb, xres, w_out, g_post, tc, name, comm):
    t = u.shape[0]
    d = w_out.shape[1]
    chunk = CONV_FWD_CHUNK
    n_chunks = tc // chunk
    piece = 2 * LANES

    def core(u_ref, wa_ref, wb_ref, bb_ref, lg_ref, lb_ref, x_ref, wo_ref, go_ref,
             y_ref, zc_ref, ca_ref, x1_ref, s_ref, zbuf, cvbuf, zcbuf):
        @pl.when(pl.program_id(0) == 0)
        def _():
            zbuf[:, 0:HALO_B, :] = jnp.zeros((D_B // LANES, HALO_B, LANES), F32)
            cvbuf[:, 0:HALO_A, :] = jnp.zeros((D_A // LANES, HALO_A, LANES), F32)

        s = None
        for lb_i in range(D_A // LANES):
            if lb_i > 0 and lb_i % 2 == 0:
                cols = slice((lb_i - 2) * LANES, lb_i * LANES)
                part = _dot(y_ref[:, cols], wo_ref[cols, :])
                s = part if s is None else s + part
            lanes = slice(lb_i * LANES, (lb_i + 1) * LANES)
            c_a = u_ref[:, D_A + lb_i * LANES:D_A + (lb_i + 1) * LANES].astype(F32)
            v_a = u_ref[:, 2 * D_A + lb_i * LANES:2 * D_A + (lb_i + 1) * LANES].astype(F32)
            cvbuf[lb_i, HALO_A:HALO_A + tc, :] = c_a * v_a
            glu_v = u_ref[:, 3 * D_A + lb_i * LANES:3 * D_A + (lb_i + 1) * LANES].astype(F32)
            glu_g = u_ref[:, 3 * D_A + D_B + lb_i * LANES:3 * D_A + D_B + (lb_i + 1) * LANES].astype(F32)
            zbuf[lb_i, HALO_B:HALO_B + tc, :] = glu_v * _sigmoid(glu_g)

            for c in range(n_chunks):
                r0 = c * chunk
                rows = slice(r0, r0 + chunk)
                acc = jnp.zeros((chunk, LANES), F32)
                for k in range(CONV_A_W):
                    off = r0 + HALO_A - (CONV_A_W - 1) + k
                    acc = acc + wa_ref[k:k + 1, lanes] * cvbuf[lb_i, off:off + chunk, :]
                ca_ref[rows, lanes] = acc.astype(BF16)
                y_ref[rows, lanes] = (u_ref[rows, lanes].astype(F32) * acc).astype(BF16)

                accb = jnp.zeros((chunk, LANES), F32)
                for k in range(CONV_B_W):
                    off = r0 + HALO_B - (CONV_B_W - 1) + k
                    accb = accb + wb_ref[k:k + 1, lanes] * zbuf[lb_i, off:off + chunk, :]
                zcbuf[rows, lanes] = accb + bb_ref[:, lanes]

            zbuf[lb_i, 0:HALO_B, :] = zbuf[lb_i, tc:tc + HALO_B, :]
            cvbuf[lb_i, 0:HALO_A, :] = cvbuf[lb_i, tc:tc + HALO_A, :]

        cols = slice(D_A - piece, D_A)
        s = s + _dot(y_ref[:, cols], wo_ref[cols, :])

        for c in range(n_chunks):
            rows = slice(c * chunk, (c + 1) * chunk)
            zc = zcbuf[rows, :]
            zc_ref[rows, :] = zc.astype(BF16)
            mu = jnp.mean(zc, axis=-1, keepdims=True)
            xc = zc - mu
            var = jnp.mean(xc * xc, axis=-1, keepdims=True)
            ln = xc * lax.rsqrt(var + LN_EPS) * lg_ref[...] + lb_ref[...]
            y_ref[rows, D_A:D_A + D_B] = (ln * _sigmoid(ln)).astype(BF16)

        s = s + _dot(y_ref[:, D_A:D_A + D_B], wo_ref[D_A:D_A + D_B, :])
        s_ref[...] = s.astype(BF16)
        x1_ref[...] = x_ref[...] + _rms_fwd(s, go_ref[...])

    return _hosted_call(
        core, comm, name, t // tc,
        in_specs=[_tile(tc, D_IN_ALL), _whole((CONV_A_W, D_A)), _whole((CONV_B_W, D_B)), _whole((1, D_B)),
                  _whole((1, D_B)), _whole((1, D_B)), _tile(tc, d), _resident((D_A + D_B, d)), _whole((1, d))],
        out_specs=[_tile(tc, D_A + D_B), _tile(tc, D_B), _tile(tc, D_A), _tile(tc, d), _tile(tc, d)],
        out_shape=[_sds((t, D_A + D_B), BF16), _sds((t, D_B), BF16), _sds((t, D_A), BF16), _sds((t, d), F32),
                   _sds((t, d), BF16)],
        scratch_shapes=[pltpu.VMEM((D_B // LANES, HALO_B + tc, LANES), F32),
                        pltpu.VMEM((D_A // LANES, HALO_A + tc, LANES), F32), pltpu.VMEM((tc, D_B), F32)],
        operands=(u, wa, wb, bb, lg, lb, xres, w_out, g_post))


def _mem_fwd(mem, g, wk, wv, name):
    m, d = mem.shape

    def body(mem_ref, g_ref, wk_ref, wv_ref, n_ref, k_ref, v_ref):
        n = _rms_fwd(mem_ref[...], g_ref[...]).astype(BF16)
        n_ref[...] = n
        k_ref[...] = _dot(n, wk_ref[...]).astype(BF16)
        v_ref[...] = _dot(n, wv_ref[...]).astype(BF16)

    return pl.pallas_call(
        body, name=name, grid=(1,),
        in_specs=[_whole((m, d)), _whole((1, d)), _whole((d, d)), _whole((d, d))],
        out_specs=[_whole((m, d))] * 3,
        out_shape=[_sds((m, d), BF16)] * 3,
        compiler_params=_params(),
    )(mem, g, wk, wv)


def _softmax_rows(s):
    e = jnp.exp(s - jnp.max(s, axis=-1, keepdims=True))
    return e / jnp.sum(e, axis=-1, keepdims=True)


def _attn_fwd(x1, g_pre, wq, k, v, wo, g_post, tm, name):
    t, d = x1.shape
    m = k.shape[0]
    scale = XA_HEAD_DIM ** -0.5

    def body(x_ref, gp_ref, wq_ref, k_ref, v_ref, wo_ref, go_ref, x2_ref, h_ref, q_ref, o_ref, a_ref):
        x = x_ref[...]
        h = _rms_fwd(x, gp_ref[...]).astype(BF16)
        h_ref[...] = h
        q_ref[...] = _dot(h, wq_ref[...]).astype(BF16)
        for hd in range(XA_HEADS):
            cols = slice(hd * XA_HEAD_DIM, (hd + 1) * XA_HEAD_DIM)
            p = _softmax_rows(_dot_nt(q_ref[:, cols], k_ref[:, cols]) * scale)
            o_ref[:, cols] = _dot(p.astype(BF16), v_ref[:, cols]).astype(BF16)
        a = _dot(o_ref[...], wo_ref[...])
        a_ref[...] = a.astype(BF16)
        x2_ref[...] = x + _rms_fwd(a, go_ref[...])

    return pl.pallas_call(
        body, name=name, grid=(t // tm,),
        in_specs=[_tile(tm, d), _whole((1, d)), _resident((d, d)), _whole((m, d)), _whole((m, d)), _resident((d, d)),
                  _whole((1, d))],
        out_specs=[_tile(tm, d)] * 5,
        out_shape=[_sds((t, d), F32)] + [_sds((t, d), BF16)] * 4,
        compiler_params=_params(),
    )(x1, g_pre, wq, k, v, wo, g_post)


def _ffn_fwd(x2, g_pre, wg_t, wu_t, wd, g_post, target, tm, name):
    t, d = x2.shape
    f = wg_t.shape[0]

    def body(x_ref, gp_ref, wg_ref, wu_ref, wd_ref, go_ref, tgt_ref, h_ref, gt_ref, up_ref, f_ref, sq_ref):
        _zero_at_first_step(sq_ref)
        x = x_ref[...]
        h = _rms_fwd(x, gp_ref[...]).astype(BF16)
        h_ref[...] = h
        gt = _dot_nt(h, wg_ref[...])
        up = _dot_nt(h, wu_ref[...])
        gt_ref[...] = gt.astype(BF16)
        up_ref[...] = up.astype(BF16)
        hd = (gt * _sigmoid(gt) * up).astype(BF16)
        ff = _dot(hd, wd_ref[...])
        f_ref[...] = ff.astype(BF16)
        err = x + _rms_fwd(ff, go_ref[...]) - tgt_ref[...]
        sq_ref[...] += jnp.sum(err * err, axis=0, keepdims=True)

    return pl.pallas_call(
        body, name=name, grid=(t // tm,),
        in_specs=[_tile(tm, d), _whole((1, d)), _resident((f, d)), _resident((f, d)), _resident((f, d)), _whole((1, d)),
                  _tile(tm, d)],
        out_specs=[_tile(tm, d), _tile(tm, f), _tile(tm, f), _tile(tm, d), _whole((1, d))],
        out_shape=[_sds((t, d), BF16), _sds((t, f), BF16), _sds((t, f), BF16), _sds((t, d), BF16), _sds((1, d), F32)],
        compiler_params=_params(),
    )(x2, g_pre, wg_t, wu_t, wd, g_post, target)


def _ffn_bwd(x2, f, target, gt, up, g_pre, wg_t, wu_t, wd, g_post, tm, name):
    t, d = x2.shape
    ff = wg_t.shape[0]

    def body(x_ref, f_ref, tgt_ref, gt_ref, up_ref, gp_ref, wg_ref, wu_ref, wd_ref, go_ref,
             dx_ref, df_ref, hd_ref, dgt_ref, dup_ref, dgo_ref, dgp_ref):
        _zero_at_first_step(dgo_ref, dgp_ref)
        x = x_ref[...]
        fo = f_ref[...].astype(F32)
        dx3 = (x + _rms_fwd(fo, go_ref[...]) - tgt_ref[...]) * (1.0 / d)
        df, dgo = _rms_bwd(dx3, fo, go_ref[...])
        dgo_ref[...] += dgo
        df = df.astype(BF16)
        df_ref[...] = df
        dhd = _dot_nt(df, wd_ref[...])
        gt = gt_ref[...].astype(F32)
        up = up_ref[...].astype(F32)
        sg = _sigmoid(gt)
        si = gt * sg
        hd_ref[...] = (si * up).astype(BF16)
        dup = (dhd * si).astype(BF16)
        dgt = (dhd * up * (sg * (1.0 + gt * (1.0 - sg)))).astype(BF16)
        dup_ref[...] = dup
        dgt_ref[...] = dgt
        dh = _dot(dgt, wg_ref[...]) + _dot(dup, wu_ref[...])
        dxn, dgp = _rms_bwd(dh, x, gp_ref[...])
        dgp_ref[...] += dgp
        dx_ref[...] = dx3 + dxn

    return pl.pallas_call(
        body, name=name, grid=(t // tm,),
        in_specs=[_tile(tm, d), _tile(tm, d), _tile(tm, d), _tile(tm, ff), _tile(tm, ff), _whole((1, d)),
                  _resident((ff, d)), _resident((ff, d)), _resident((ff, d)), _whole((1, d))],
        out_specs=[_tile(tm, d), _tile(tm, d), _tile(tm, ff), _tile(tm, ff), _tile(tm, ff), _whole((1, d)), _whole((1, d))],
        out_shape=[_sds((t, d), F32), _sds((t, d), BF16), _sds((t, ff), BF16), _sds((t, ff), BF16), _sds((t, ff), BF16),
                   _sds((1, d), F32), _sds((1, d), F32)],
        compiler_params=_params(),
    )(x2, f, target, gt, up, g_pre, wg_t, wu_t, wd, g_post)


def _attn_bwd(dx2, a, x1, q, k, v, g_pre, wq, wo, g_post, s_mix, w_mix_out, g_mix_post, tm, name, comm):
    t, d = x1.shape
    m = k.shape[0]
    kdim = w_mix_out.shape[0]
    scale = XA_HEAD_DIM ** -0.5

    def core(dx2_ref, a_ref, x_ref, q_ref, k_ref, v_ref, gp_ref, wq_ref, wo_ref, go_ref, s_ref, wm_ref, gm_ref,
             dx1_ref, da_ref, dq_ref, dk_ref, dv_ref, dgo_ref, dgp_ref, ds_ref, dy_ref, dgm_ref, do_buf):
        _zero_at_first_step(dgo_ref, dgp_ref, dk_ref, dv_ref, dgm_ref)
        dx2 = dx2_ref[...]
        da, dgo = _rms_bwd(dx2, a_ref[...].astype(F32), go_ref[...])
        dgo_ref[...] += dgo
        da = da.astype(BF16)
        da_ref[...] = da
        do_buf[...] = _dot_nt(da, wo_ref[...]).astype(BF16)
        for hd in range(XA_HEADS):
            cols = slice(hd * XA_HEAD_DIM, (hd + 1) * XA_HEAD_DIM)
            qh = q_ref[:, cols]
            p = _softmax_rows(_dot_nt(qh, k_ref[:, cols]) * scale)
            do_h = do_buf[:, cols]
            dp = _dot_nt(do_h, v_ref[:, cols])
            dv_ref[:, cols] += _dot_tn(p.astype(BF16), do_h)
            ds = (p * (dp - jnp.sum(dp * p, axis=-1, keepdims=True)) * scale).astype(BF16)
            dq_ref[:, cols] = _dot(ds, k_ref[:, cols]).astype(BF16)
            dk_ref[:, cols] += _dot_tn(ds, qh)
        dh = _dot_nt(dq_ref[...], wq_ref[...])
        dxn, dgp = _rms_bwd(dh, x_ref[...], gp_ref[...])
        dgp_ref[...] += dgp
        dx1 = dx2 + dxn
        dx1_ref[...] = dx1
        ds, dgm = _rms_bwd(dx1, s_ref[...].astype(F32), gm_ref[...])
        dgm_ref[...] += dgm
        ds = ds.astype(BF16)
        ds_ref[...] = ds
        dy_ref[...] = _dot_nt(ds, wm_ref[...]).astype(BF16)

    return _hosted_call(
        core, comm, name, t // tm,
        in_specs=[_tile(tm, d), _tile(tm, d), _tile(tm, d), _tile(tm, d), _whole((m, d)), _whole((m, d)), _whole((1, d)),
                  _resident((d, d)), _resident((d, d)), _whole((1, d)), _tile(tm, d), _resident((kdim, d)), _whole((1, d))],
        out_specs=[_tile(tm, d), _tile(tm, d), _tile(tm, d), _whole((m, d)), _whole((m, d)), _whole((1, d)), _whole((1, d)),
                   _tile(tm, d), _tile(tm, kdim), _whole((1, d))],
        out_shape=[_sds((t, d), F32), _sds((t, d), BF16), _sds((t, d), BF16), _sds((m, d), F32), _sds((m, d), F32),
                   _sds((1, d), F32), _sds((1, d), F32), _sds((t, d), BF16), _sds((t, kdim), BF16), _sds((1, d), F32)],
        scratch_shapes=[pltpu.VMEM((tm, d), BF16)],
        operands=(dx2, a, x1, q, k, v, g_pre, wq, wo, g_post, s_mix, w_mix_out, g_mix_post))


def _mem_bwd(mem, mem_n, dk, dv, g, wk, wv, name):
    m, d = mem.shape

    def body(mem_ref, n_ref, dk_ref, dv_ref, g_ref, wk_ref, wv_ref, dwk_ref, dwv_ref, dg_ref):
        dk = dk_ref[...].astype(BF16)
        dv = dv_ref[...].astype(BF16)
        n = n_ref[...]
        dwk_ref[...] = _dot_tn(n, dk).astype(BF16)
        dwv_ref[...] = _dot_tn(n, dv).astype(BF16)
        dn = _dot_nt(dk, wk_ref[...]) + _dot_nt(dv, wv_ref[...])
        _, dg = _rms_bwd(dn, mem_ref[...], g_ref[...])
        dg_ref[...] = dg

    return pl.pallas_call(
        body, name=name, grid=(1,),
        in_specs=[_whole((m, d)), _whole((m, d)), _whole((m, d)), _whole((m, d)), _whole((1, d)), _whole((d, d)),
                  _whole((d, d))],
        out_specs=[_whole((d, d)), _whole((d, d)), _whole((1, d))],
        out_shape=[_sds((d, d), BF16), _sds((d, d), BF16), _sds((1, d), F32)],
        compiler_params=_params(),
    )(mem, mem_n, dk, dv, g, wk, wv)


def _conv_bwd(dy, u, zc, ca, wa, wb, lg, lb, tc, name, comm):
    t = u.shape[0]
    n_tiles = t // tc
    n_chunks = tc // CONV_CHUNK

    def core(dy_ref, u_ref, zc_ref, ca_ref, wa_ref, wb_ref, lg_ref, lb_ref,
             du_ref, dwa_ref, dwb_ref, dbb_ref, dlg_ref, dlb_ref, ebuf, eabuf, zbuf, cvbuf, gatebuf, wacc, aacc, vacc):
        step = pl.program_id(0)

        @pl.when(step == 0)
        def _():
            ebuf[:, tc:tc + HALO_B, :] = jnp.zeros((D_B // LANES, HALO_B, LANES), F32)
            eabuf[:, tc:tc + HALO_A, :] = jnp.zeros((D_A // LANES, HALO_A, LANES), F32)
            wacc[...] = jnp.zeros_like(wacc)
            aacc[...] = jnp.zeros_like(aacc)
            vacc[...] = jnp.zeros_like(vacc)

        dbb = jnp.zeros((SUBLANES, D_B), F32)
        dlg = jnp.zeros((SUBLANES, D_B), F32)
        dlb = jnp.zeros((SUBLANES, D_B), F32)
        for c in range(n_chunks):
            rows = slice(c * CONV_CHUNK, (c + 1) * CONV_CHUNK)
            dy_a = dy_ref[rows, 0:D_A].astype(F32)
            b_a = u_ref[rows, 0:D_A].astype(F32)
            du_ref[rows, 0:D_A] = (dy_a * ca_ref[rows, :].astype(F32)).astype(BF16)
            dca = dy_a * b_a
            cv = u_ref[rows, D_A:2 * D_A].astype(F32) * u_ref[rows, 2 * D_A:3 * D_A].astype(F32)
            gate = _sigmoid(u_ref[rows, 3 * D_A + D_B:3 * D_A + 2 * D_B].astype(F32))
            z = u_ref[rows, 3 * D_A:3 * D_A + D_B].astype(F32) * gate
            for lb_i in range(D_A // LANES):
                lanes = slice(lb_i * LANES, (lb_i + 1) * LANES)
                eabuf[lb_i, rows, :] = dca[:, lanes]
                cvbuf[lb_i, rows, :] = cv[:, lanes]
                zbuf[lb_i, rows, :] = z[:, lanes]
                gatebuf[lb_i, rows, :] = gate[:, lanes]

            zcv = zc_ref[rows, :].astype(F32)
            mu = jnp.mean(zcv, axis=-1, keepdims=True)
            xc = zcv - mu
            rstd = lax.rsqrt(jnp.mean(xc * xc, axis=-1, keepdims=True) + LN_EPS)
            xhat = xc * rstd
            ln = xhat * lg_ref[...] + lb_ref[...]
            sg = _sigmoid(ln)
            dln = dy_ref[rows, D_A:D_A + D_B].astype(F32) * (sg * (1.0 + ln * (1.0 - sg)))
            dlg = dlg + jnp.sum((dln * xhat).reshape(CONV_CHUNK // SUBLANES, SUBLANES, D_B), axis=0)
            dlb = dlb + jnp.sum(dln.reshape(CONV_CHUNK // SUBLANES, SUBLANES, D_B), axis=0)
            dxh = dln * lg_ref[...]
            dzc = rstd * (dxh - jnp.mean(dxh, axis=-1, keepdims=True) - xhat * jnp.mean(dxh * xhat, axis=-1, keepdims=True))
            dbb = dbb + jnp.sum(dzc.reshape(CONV_CHUNK // SUBLANES, SUBLANES, D_B), axis=0)
            for lb_i in range(D_B // LANES):
                ebuf[lb_i, rows, :] = dzc[:, lb_i * LANES:(lb_i + 1) * LANES]
        vacc[0] += dbb
        vacc[1] += dlg
        vacc[2] += dlb

        for lb_i in range(D_A // LANES):
            lanes = slice(lb_i * LANES, (lb_i + 1) * LANES)

            def cols(first):
                return slice(first + lb_i * LANES, first + (lb_i + 1) * LANES)

            for c in range(n_chunks):
                r0 = c * CONV_CHUNK
                rows = slice(r0, r0 + CONV_CHUNK)
                cv = cvbuf[lb_i, rows, :]
                dcv = jnp.zeros((CONV_CHUNK, LANES), F32)
                for k in range(CONV_A_W):
                    off = r0 + (CONV_A_W - 1) - k
                    e = eabuf[lb_i, off:off + CONV_CHUNK, :]
                    dcv = dcv + wa_ref[k:k + 1, lanes] * e
                    aacc[k, :, lanes] += jnp.sum((cv * e).reshape(CONV_CHUNK // SUBLANES, SUBLANES, LANES), axis=0)
                du_ref[rows, cols(D_A)] = (dcv * u_ref[rows, cols(2 * D_A)].astype(F32)).astype(BF16)
                du_ref[rows, cols(2 * D_A)] = (dcv * u_ref[rows, cols(D_A)].astype(F32)).astype(BF16)

                z = zbuf[lb_i, rows, :]
                dz = jnp.zeros((CONV_CHUNK, LANES), F32)
                for k in range(CONV_B_W):
                    off = r0 + (CONV_B_W - 1) - k
                    e = ebuf[lb_i, off:off + CONV_CHUNK, :]
                    dz = dz + wb_ref[k:k + 1, lanes] * e
                    wacc[k, :, lanes] += jnp.sum((z * e).reshape(CONV_CHUNK // SUBLANES, SUBLANES, LANES), axis=0)
                glu_v = u_ref[rows, cols(3 * D_A)].astype(F32)
                sgg = gatebuf[lb_i, rows, :]
                du_ref[rows, cols(3 * D_A)] = (dz * sgg).astype(BF16)
                du_ref[rows, cols(3 * D_A + D_B)] = (dz * glu_v * sgg * (1.0 - sgg)).astype(BF16)

            ebuf[lb_i, tc:tc + HALO_B, :] = ebuf[lb_i, 0:HALO_B, :]
            eabuf[lb_i, tc:tc + HALO_A, :] = eabuf[lb_i, 0:HALO_A, :]

        @pl.when(step == n_tiles - 1)
        def _():
            for k in range(CONV_B_W):
                dwb_ref[k:k + 1, :] = jnp.sum(wacc[k], axis=0, keepdims=True)
            for k in range(CONV_A_W):
                dwa_ref[k:k + 1, :] = jnp.sum(aacc[k], axis=0, keepdims=True)
            dbb_ref[...] = jnp.sum(vacc[0], axis=0, keepdims=True)
            dlg_ref[...] = jnp.sum(vacc[1], axis=0, keepdims=True)
            dlb_ref[...] = jnp.sum(vacc[2], axis=0, keepdims=True)

    return _hosted_call(
        core, comm, name, n_tiles,
        in_specs=[_rtile(tc, D_A + D_B, n_tiles), _rtile(tc, D_IN_ALL, n_tiles), _rtile(tc, D_B, n_tiles),
                  _rtile(tc, D_A, n_tiles), _whole((CONV_A_W, D_A)), _whole((CONV_B_W, D_B)), _whole((1, D_B)),
                  _whole((1, D_B))],
        out_specs=[_rtile(tc, D_IN_ALL, n_tiles), _whole((CONV_A_W, D_A)), _whole((CONV_B_W, D_B)), _whole((1, D_B)),
                   _whole((1, D_B)), _whole((1, D_B))],
        out_shape=[_sds((t, D_IN_ALL), BF16), _sds((CONV_A_W, D_A), F32), _sds((CONV_B_W, D_B), F32), _sds((1, D_B), F32),
                   _sds((1, D_B), F32), _sds((1, D_B), F32)],
        scratch_shapes=[pltpu.VMEM((D_B // LANES, tc + HALO_B, LANES), F32), pltpu.VMEM((D_A // LANES, tc + HALO_A, LANES), F32),
                        pltpu.VMEM((D_B // LANES, tc, LANES), F32), pltpu.VMEM((D_A // LANES, tc, LANES), F32),
                        pltpu.VMEM((D_B // LANES, tc, LANES), F32),
                        pltpu.VMEM((CONV_B_W, SUBLANES, D_B), F32), pltpu.VMEM((CONV_A_W, SUBLANES, D_A), F32),
                        pltpu.VMEM((3, SUBLANES, D_B), F32)],
        operands=(dy, u, zc, ca, wa, wb, lg, lb))


def _in_bwd(du, w_t, x, dx1, g, tm, name, after):
    t, d = x.shape
    n = w_t.shape[0]

    def body(after_ref, du_ref, w_ref, x_ref, dx1_ref, g_ref, dx_ref, dg_ref):
        del after_ref
        _zero_at_first_step(dg_ref)
        dh = _dot(du_ref[...], w_ref[...])
        dxn, dg = _rms_bwd(dh, x_ref[...], g_ref[...])
        dg_ref[...] += dg
        dx_ref[...] = dx1_ref[...] + dxn

    return pl.pallas_call(
        body, name=name, grid=(t // tm,),
        in_specs=[pl.BlockSpec(memory_space=pl.ANY), _tile(tm, n), _resident((n, d)), _tile(tm, d), _tile(tm, d),
                  _whole((1, d))],
        out_specs=[_tile(tm, d), _whole((1, d))],
        out_shape=[_sds((t, d), F32), _sds((1, d), F32)],
        compiler_params=_params(),
    )(after, du, w_t, x, dx1, g)


def _wgrad(a, b, tk, bm, bn, name, comm=None, cols=None):
    t, m = a.shape
    first, n = (0, b.shape[1]) if cols is None else cols
    first_block = first // bn
    n_k = t // tk

    def body(a_ref, b_ref, o_ref, acc):
        @pl.when(pl.program_id(2) == 0)
        def _():
            acc[...] = jnp.zeros_like(acc)

        acc[...] += _dot_tn(a_ref[...], b_ref[...])

        @pl.when(pl.program_id(2) == n_k - 1)
        def _():
            o_ref[...] = acc[...].astype(BF16)

    call = dict(
        grid=(m // bm, n // bn, n_k),
        in_specs=[pl.BlockSpec((tk, bm), lambda i, j, k: (k, i)),
                  pl.BlockSpec((tk, bn), lambda i, j, k: (k, first_block + j))],
        out_specs=[pl.BlockSpec((bm, bn), lambda i, j, k: (i, j))],
        out_shape=[_sds((m, n), BF16)],
        scratch_shapes=[pltpu.VMEM((bm, bn), F32)])
    if comm is None:
        return pl.pallas_call(body, name=name, compiler_params=_params(3), **call)(a, b)[0]
    (out,), received = _hosted_call(body, comm, name, operands=(a, b), **call)
    return out, received


def _adamw(items, tr, name, after=None):
    n_parts, r, c = items[0][0].shape
    n_items = len(items)
    order = [] if after is None else [after]

    def body(*refs):
        refs = refs[len(order):]
        ins, outs = refs[:4 * n_items], refs[4 * n_items:]
        for i in range(n_items):
            p_ref, w_ref, m_ref, v_ref = ins[4 * i:4 * i + 4]
            g_ref, d_ref, nm_ref, nv_ref = outs[4 * i:4 * i + 4]
            g = p_ref[0].astype(F32)
            for j in range(1, n_parts):
                g = g + p_ref[j].astype(F32)
            g_ref[...] = g
            nm = ADAM_B1 * m_ref[...] + (1.0 - ADAM_B1) * g
            nv = ADAM_B2 * v_ref[...] + (1.0 - ADAM_B2) * (g * g)
            nm_ref[...] = nm
            nv_ref[...] = nv
            m_hat = nm / (1.0 - ADAM_B1 ** ADAM_STEP)
            v_hat = nv / (1.0 - ADAM_B2 ** ADAM_STEP)
            d_ref[...] = -ADAM_LR * (m_hat / (jnp.sqrt(v_hat) + ADAM_EPS) + ADAM_WD * w_ref[...])

    results = pl.pallas_call(
        body, name=name, grid=(r // tr,),
        in_specs=[pl.BlockSpec(memory_space=pl.ANY)] * len(order)
        + ([pl.BlockSpec((n_parts, tr, c), lambda i: (0, i, 0))] + [_tile(tr, c)] * 3) * n_items,
        out_specs=[_tile(tr, c)] * (4 * n_items),
        out_shape=[_sds((r, c), F32)] * (4 * n_items),
        compiler_params=_params(),
    )(*order, *[a for item in items for a in item])
    return [results[4 * i:4 * i + 4] for i in range(n_items)]


def _sum_parts(parts, name):
    n_parts, r, c = parts.shape

    def body(p_ref, o_ref):
        acc = p_ref[0]
        for j in range(1, n_parts):
            acc = acc + p_ref[j]
        o_ref[...] = acc

    return pl.pallas_call(
        body, name=name, grid=(1,),
        in_specs=[_whole((n_parts, r, c))], out_specs=_whole((r, c)), out_shape=_sds((r, c), F32),
        compiler_params=_params(),
    )(parts)


def _row(v):
    return v.reshape(1, -1)


def _by_owner_rows(g):
    return g.reshape(N_DEV, g.shape[0] // N_DEV, g.shape[1])


WEIGHTS = ("mix_pre_g", "w_mix_in", "conv_a_w", "conv_b_w", "conv_b_b", "ln_b_g", "ln_b_b", "w_mix_out", "mix_post_g",
           "xa_pre_g", "mem_norm_g", "w_q", "w_k", "w_v", "w_o", "xa_post_g", "ffn_pre_g", "w_gate", "w_up", "w_down",
           "ffn_post_g")
LARGE = ("w_mix_in", "w_mix_out", "w_q", "w_k", "w_v", "w_o", "w_gate", "w_up", "w_down")
COLUMN_SHARDED = ("w_mix_in", "w_gate", "w_up")
GAINS = ("mix_pre_g", "mix_post_g", "xa_pre_g", "mem_norm_g", "xa_post_g", "ffn_pre_g", "ffn_post_g")
CHANNEL_VECTORS = ("conv_b_b", "ln_b_g", "ln_b_b")
CONV_TAPS = ("conv_a_w", "conv_b_w")
SMALL = GAINS + CHANNEL_VECTORS + CONV_TAPS
CONV_COLS_PER_DEVICE = D_A // N_DEV
TOKEN_TILE = 512
FFN_FWD_TOKEN_TILE = 512
FFN_TOKEN_TILE = 256
CONV_TOKEN_TILE = 256
WGRAD_TOKEN_TILE = 2048
SMALL_WGRAD_TOKEN_TILE = 4096
IN_BWD_TOKEN_TILE = 1024
ADAM_ROWS_PER_STEP = 64


def _lane_rows(v):
    flat = v.reshape(-1)
    tile = SUBLANES * LANES
    flat = jnp.pad(flat, (0, (-flat.shape[0]) % tile))
    return flat.reshape(-1, LANES)


def _pack_small(values, names):
    return jnp.concatenate([_lane_rows(values[n]) for n in names], axis=0)


def _unpack_small(packed, like, names):
    out, off = {}, 0
    for n in names:
        size = like[n].size
        rows = _lane_rows(like[n]).shape[0]
        out[n] = packed[off:off + rows, :].reshape(-1)[:size].reshape(like[n].shape)
        off += rows
    return out


def kernel(x, mem, mix_pre_g, w_mix_in, conv_a_w, conv_b_w, conv_b_b, ln_b_g, ln_b_b, w_mix_out, mix_post_g, xa_pre_g, mem_norm_g, w_q, w_k, w_v, w_o, xa_post_g, ffn_pre_g, w_gate, w_up, w_down, ffn_post_g, loss_target, m_mix_pre_g, m_w_mix_in, m_conv_a_w, m_conv_b_w, m_conv_b_b, m_ln_b_g, m_ln_b_b, m_w_mix_out, m_mix_post_g, m_xa_pre_g, m_mem_norm_g, m_w_q, m_w_k, m_w_v, m_w_o, m_xa_post_g, m_ffn_pre_g, m_w_gate, m_w_up, m_w_down, m_ffn_post_g, v_mix_pre_g, v_w_mix_in, v_conv_a_w, v_conv_b_w, v_conv_b_b, v_ln_b_g, v_ln_b_b, v_w_mix_out, v_mix_post_g, v_xa_pre_g, v_mem_norm_g, v_w_q, v_w_k, v_w_v, v_w_o, v_xa_post_g, v_ffn_pre_g, v_w_gate, v_w_up, v_w_down, v_ffn_post_g):
    given = dict(locals())
    w = {n: given[n] for n in WEIGHTS}
    m = {n: given["m_" + n] for n in WEIGHTS}
    v = {n: given["v_" + n] for n in WEIGHTS}
    xs, mems, target = x[0], mem[0], loss_target[0]
    t = xs.shape[0]
    tm, tm_ffn, tc, tk = min(TOKEN_TILE, t), min(FFN_TOKEN_TILE, t), min(CONV_TOKEN_TILE, t), min(WGRAD_TOKEN_TILE, t)
    tk_small = min(SMALL_WGRAD_TOKEN_TILE, t)
    g = {n: _row(w[n]) for n in GAINS}
    bb, lg, lb = (_row(w[n]) for n in CHANNEL_VECTORS)

    def shard_bf16(*names):
        return [w[n].astype(BF16) for n in names]

    def shard_bf16_t(*names):
        return [w[n].T.astype(BF16) for n in names]

    g_mix_in, g_taps = _comm_call(_Gather(shard_bf16_t("w_mix_in") + [_pack_small(w, CONV_TAPS)]), "gather_mixer")
    w_mix_in_t = g_mix_in.reshape(D_IN_ALL, D_MODEL)
    taps, off = {}, 0
    for n in CONV_TAPS:
        k, cols = w[n].shape
        rows = _lane_rows(w[n]).shape[0]
        blk = g_taps[:, off:off + rows, :].reshape(N_DEV, -1)[:, :k * cols].reshape(N_DEV, k, cols)
        taps[n] = blk.transpose(1, 0, 2).reshape(k, N_DEV * cols)
        off += rows
    wa, wb = taps["conv_a_w"], taps["conv_b_w"]

    (h1, u), gathered = _norm_matmul(xs, g["mix_pre_g"], w_mix_in_t.T, tm, "mix_in_fwd",
                                     _Gather(shard_bf16("w_mix_out", "w_q", "w_k", "w_v", "w_o")))
    w_mix_out, w_q, w_k, w_v, w_o = (a.reshape(D_MODEL, D_MODEL) for a in gathered)
    (ycat, zc, ca, x1, s1), (g_gate, g_up, g_down) = _conv_fwd(
        u, wa, wb, bb, lg, lb, xs, w_mix_out, g["mix_post_g"], tc, "conv_fwd",
        _Gather(shard_bf16_t("w_gate", "w_up") + shard_bf16("w_down")))
    w_gate_t, w_up_t, w_down = (a.reshape(D_FF, D_MODEL) for a in (g_gate, g_up, g_down))
    mem_n, kk, vv = _mem_fwd(mems, g["mem_norm_g"], w_k, w_v, "mem_fwd")
    x2, h2, q, o, a = _attn_fwd(x1, g["xa_pre_g"], w_q, kk, vv, w_o, g["xa_post_g"], tm, "attn_fwd")
    h3, gt, up, f, sq = _ffn_fwd(x2, g["ffn_pre_g"], w_gate_t, w_up_t, w_down, g["ffn_post_g"], target,
                                 min(FFN_FWD_TOKEN_TILE, t), "ffn_fwd")

    dx2, df, hd, dgt, dup, d_ffn_post, d_ffn_pre = _ffn_bwd(
        x2, f, target, gt, up, g["ffn_pre_g"], w_gate_t, w_up_t, w_down, g["ffn_post_g"], tm_ffn, "ffn_bwd")
    d_w_down = _wgrad(hd, df, tk, D_FF // 2, D_MODEL, "wgrad_down")
    d_w_gate_t = _wgrad(dgt, h3, tk, D_FF // 2, D_MODEL, "wgrad_gate")
    d_w_up_t = _wgrad(dup, h3, tk, D_FF // 2, D_MODEL, "wgrad_up")
    ffn_slabs = [_by_owner_rows(d) for d in (d_w_gate_t, d_w_up_t, d_w_down)]

    (dx1, da, dq, dk, dv, d_xa_post, d_xa_pre, ds1, dycat, d_mix_post), from_ffn = _attn_bwd(
        dx2, a, x1, q, kk, vv, g["xa_pre_g"], w_q, w_o, g["xa_post_g"], s1, w_mix_out, g["mix_post_g"], tm, "attn_bwd",
        _Exchange(ffn_slabs, [True] * 3))
    d_w_o = _wgrad(o, da, tk, D_MODEL, D_MODEL, "wgrad_o")
    d_w_q = _wgrad(h2, dq, tk, D_MODEL, D_MODEL, "wgrad_q")
    d_w_k, d_w_v, d_mem_norm = _mem_bwd(mems, mem_n, dk, dv, g["mem_norm_g"], w_k, w_v, "mem_bwd")
    d_w_mix_out = _wgrad(ycat, ds1, tk, D_MODEL, D_MODEL, "wgrad_mix_out")
    attn_slabs = [_by_owner_rows(d) for d in (d_w_mix_out, d_w_q, d_w_k, d_w_v, d_w_o)]

    (du, d_conv_a, d_conv_b, d_conv_bb, d_ln_g, d_ln_b), from_attn = _conv_bwd(
        dycat, u, zc, ca, wa, wb, lg, lb, tc, "conv_bwd", _Exchange(attn_slabs, [True] * 5))
    d_w_in_t = _wgrad(du, h1, tk, D_IN_ALL // 2, D_MODEL, "wgrad_mix_in")
    in_scatter = [True]
    in_copy, in_started = _exchange_start([_by_owner_rows(d_w_in_t)], in_scatter, "mix_in_start")
    dx, d_mix_pre = _in_bwd(du, w_mix_in_t, xs, dx1, g["mix_pre_g"], min(IN_BWD_TOKEN_TILE, t), "mix_in_bwd",
                            in_started)
    (from_in,) = _exchange_wait(in_copy, in_scatter, d_mix_pre, "mix_in_wait")
    small_grads = dict(mix_pre_g=d_mix_pre, conv_a_w=d_conv_a, conv_b_w=d_conv_b, conv_b_b=d_conv_bb, ln_b_g=d_ln_g,
                       ln_b_b=d_ln_b, mix_post_g=d_mix_post, xa_pre_g=d_xa_pre, mem_norm_g=d_mem_norm, xa_post_g=d_xa_post,
                       ffn_pre_g=d_ffn_pre, ffn_post_g=d_ffn_post, loss=sq)
    names = SMALL + ("loss",)
    tail_scatter = [False]
    tail, started = _exchange_start([_pack_small(small_grads, names)], tail_scatter, "tail_start")

    received = dict(zip(("w_gate", "w_up", "w_down"), from_ffn))
    received.update(zip(("w_mix_out", "w_q", "w_k", "w_v", "w_o"), from_attn))
    grad, delta, new_m, new_v = {}, {}, {}, {}

    def adamw_group(group, after=None):
        items = [[received[n]] + [a[n].T if n in COLUMN_SHARDED else a[n] for a in (w, m, v)] for n in group]
        rows = items[0][1].shape[0]
        results = _adamw(items, min(rows, ADAM_ROWS_PER_STEP) if len(group) > 1 else rows, "adamw_" + group[0], after)
        for n, result in zip(group, results):
            grad[n], delta[n], new_m[n], new_v[n] = [r.T if n in COLUMN_SHARDED else r for r in result]
        return results[-1][-1]

    behind = started
    for group in (("w_gate",), ("w_up",), ("w_down",), ("w_mix_out", "w_q", "w_k", "w_v", "w_o")):
        behind = adamw_group(group, behind)
    received["w_mix_in"] = from_in
    behind = adamw_group(("w_mix_in",), behind)
    (all_small,) = _exchange_wait(tail, tail_scatter, behind, "tail_wait")

    total = _unpack_small(_sum_parts(all_small, "sum_small"), small_grads, names)
    loss = jnp.sum(total.pop("loss")) * (0.5 / D_MODEL)
    first_col = _device_index() * CONV_COLS_PER_DEVICE
    for n in CONV_TAPS:
        total[n] = lax.dynamic_slice_in_dim(total[n], first_col, CONV_COLS_PER_DEVICE, axis=1)
    total = {n: total[n].reshape(w[n].shape) for n in SMALL}
    packed_small = [_pack_small(values, SMALL) for values in (total, w, m, v)]
    ((g_s, d_s, nm_s, nv_s),) = _adamw([[packed_small[0][None]] + packed_small[1:]], packed_small[0].shape[0],
                                       "adamw_small")
    for out, p in ((grad, g_s), (delta, d_s), (new_m, nm_s), (new_v, nv_s)):
        out.update(_unpack_small(p, w, SMALL))

    return (loss, dx[None], *[grad[n] for n in WEIGHTS], *[delta[n] for n in WEIGHTS], *[new_m[n] for n in WEIGHTS],
            *[new_v[n] for n in WEIGHTS])
```

```python
import jax
import jax.numpy as jnp
from jax import lax
from jax.experimental import pallas as pl
from jax.experimental.pallas import tpu as pltpu

F32 = jnp.float32
BF16 = jnp.bfloat16

D_MODEL = 1024
D_A = 512
D_B = 512
D_IN_ALL = 2560
CONV_A_W = 3
CONV_B_W = 31
XA_HEADS = 4
XA_HEAD_DIM = 256
D_FF = 2816
N_DEV = 8
RMS_EPS = 1e-6
LN_EPS = 1e-5
ADAM_LR = 0.001
ADAM_B1 = 0.9
ADAM_B2 = 0.999
ADAM_EPS = 1e-08
ADAM_WD = 0.01
ADAM_STEP = 10

VMEM_LIMIT_BYTES = 56 * 1024 * 1024
SUBLANES = 8
LANES = 128
HALO_B = 32
HALO_A = 8
CONV_FWD_CHUNK = 16
CONV_CHUNK = 32
GATHER_FORWARD_STEPS_BEFORE_END = 8

MESH = pl.DeviceIdType.MESH


def _params(n_grid_axes=1):
    return pltpu.CompilerParams(dimension_semantics=("arbitrary",) * n_grid_axes, vmem_limit_bytes=VMEM_LIMIT_BYTES)


def _sds(shape, dtype):
    return jax.ShapeDtypeStruct(shape, dtype)


def _tile(rows, cols):
    return pl.BlockSpec((rows, cols), lambda i: (i, 0))


def _rtile(rows, cols, n):
    return pl.BlockSpec((rows, cols), lambda i: (n - 1 - i, 0))


def _whole(shape):
    zeros = (0,) * len(shape)
    return pl.BlockSpec(shape, lambda i: zeros)


def _resident(shape):
    zeros = (0,) * len(shape)
    return pl.BlockSpec(shape, lambda i: zeros, pipeline_mode=pl.Buffered(1))


def _dot(a, b):
    return jnp.dot(a, b, preferred_element_type=F32)


def _dot_nt(a, b):
    return lax.dot_general(a, b, (((1,), (1,)), ((), ())), preferred_element_type=F32)


def _dot_tn(a, b):
    return lax.dot_general(a, b, (((0,), (0,)), ((), ())), preferred_element_type=F32)


def _sigmoid(x):
    return 1.0 / (1.0 + jnp.exp(-x))


def _rms_fwd(x, g):
    r = lax.rsqrt(jnp.mean(x * x, axis=-1, keepdims=True) + RMS_EPS)
    return x * r * g


def _rms_bwd(dy, xin, g):
    r = lax.rsqrt(jnp.mean(xin * xin, axis=-1, keepdims=True) + RMS_EPS)
    n = xin * r
    dg = jnp.sum(dy * n, axis=0, keepdims=True)
    dn = dy * g
    dx = r * (dn - n * jnp.mean(dn * n, axis=-1, keepdims=True))
    return dx, dg


def _zero_at_first_step(*refs):
    @pl.when(pl.program_id(0) == 0)
    def _():
        for ref in refs:
            ref[...] = jnp.zeros(ref.shape, ref.dtype)


def _place():
    return lax.axis_index("x"), lax.axis_index("y"), lax.axis_index("c")


def _device_index():
    x, y, c = _place()
    return 4 * x + 2 * y + c


class _Gather:
    def __init__(self, arrays):
        self.arrays = list(arrays)
        self.out_shape = [_sds((N_DEV, *a.shape), a.dtype) for a in self.arrays]
        n = len(self.arrays)
        self.scratch_shapes = [pltpu.SemaphoreType.DMA((n, 7)), pltpu.SemaphoreType.DMA((n, 7)),
                               pltpu.SemaphoreType.DMA((n,))]

    def forward_step(self, n_steps):
        return max(0, n_steps - 1 - GATHER_FORWARD_STEPS_BEFORE_END)

    def bind(self, srcs, dsts, send_sems, recv_sems, local_sems):
        x, y, cc = _place()
        me, sibling = (x, y, cc), (x, y, 1 - cc)
        chips = [(1 - x, y), (x, 1 - y), (1 - x, 1 - y)]

        def copy(a, k, owner, to, src=None):
            slot = dsts[a].at[4 * owner[0] + 2 * owner[1] + owner[2]]
            return pltpu.make_async_remote_copy(
                src_ref=slot if src is None else src, dst_ref=slot, send_sem=send_sems.at[a, k],
                recv_sem=recv_sems.at[a, k], device_id=to, device_id_type=MESH)

        def first(a):
            return [copy(a, 0, me, sibling, src=srcs[a])] + [
                copy(a, 1 + j, me, (*chip, cc), src=srcs[a]) for j, chip in enumerate(chips)]

        def passed(a, j):
            return copy(a, 4 + j, (*chips[j], cc), sibling)

        def mine(a):
            return pltpu.make_async_copy(srcs[a], dsts[a].at[4 * x + 2 * y + cc], local_sems.at[a])

        def start():
            for a in range(len(srcs)):
                mine(a).start()
                for cp in first(a):
                    cp.start()

        def forward():
            for a in range(len(srcs)):
                for j, chip in enumerate(chips):
                    copy(a, 1 + j, (*chip, cc), me).wait_recv()
                    passed(a, j).start()

        def finish():
            for a in range(len(srcs)):
                copy(a, 0, sibling, me).wait_recv()
                for j, chip in enumerate(chips):
                    copy(a, 4 + j, (*chip, 1 - cc), me).wait_recv()
                for cp in first(a) + [passed(a, j) for j in range(len(chips))]:
                    cp.wait_send()
                mine(a).wait()

        return start, forward, finish


class _Exchange:
    def __init__(self, arrays, scatter):
        self.arrays = list(arrays)
        self.scatter = list(scatter)
        self.out_shape = [_sds(a.shape if s else (N_DEV, *a.shape), a.dtype) for a, s in zip(self.arrays, self.scatter)]
        n = len(self.arrays)
        self.scratch_shapes = [pltpu.SemaphoreType.DMA((n, 7)), pltpu.SemaphoreType.DMA((n, 7)),
                               pltpu.SemaphoreType.DMA((n,))]

    def forward_step(self, n_steps):
        return n_steps - 1

    def bind(self, srcs, dsts, send_sems, recv_sems, local_sems):
        me = _device_index()

        def copies(a):
            out = []
            for k in range(1, N_DEV):
                p = me ^ k
                out.append(pltpu.make_async_remote_copy(
                    src_ref=srcs[a].at[p] if self.scatter[a] else srcs[a], dst_ref=dsts[a].at[me],
                    send_sem=send_sems.at[a, k - 1], recv_sem=recv_sems.at[a, k - 1],
                    device_id=(p >> 2, (p >> 1) & 1, p & 1), device_id_type=MESH))
            return out

        def mine(a):
            return pltpu.make_async_copy(srcs[a].at[me] if self.scatter[a] else srcs[a], dsts[a].at[me], local_sems.at[a])

        def start():
            for a in range(len(srcs)):
                mine(a).start()
                for cp in copies(a):
                    cp.start()

        def forward():
            pass

        def finish():
            for a in range(len(srcs)):
                for cp in copies(a):
                    cp.wait()
                mine(a).wait()

        return start, forward, finish


def _hosted_call(core, comm, name, grid, in_specs, out_specs, out_shape, scratch_shapes, operands):
    grid = (grid,) if isinstance(grid, int) else tuple(grid)
    n_steps = 1
    for extent in grid:
        n_steps *= extent
    n_in, n_out, n_scr, n_arr = len(in_specs), len(out_specs), len(scratch_shapes), len(comm.arrays)
    any_spec = pl.BlockSpec(memory_space=pl.ANY)

    def body(*refs):
        ins, refs = refs[:n_in], refs[n_in:]
        srcs, refs = refs[:n_arr], refs[n_arr:]
        outs, refs = refs[:n_out], refs[n_out:]
        dsts, refs = refs[:n_arr], refs[n_arr:]
        scratch, sems = refs[:n_scr], refs[n_scr:]
        start, forward, finish = comm.bind(srcs, dsts, *sems)
        step = pl.program_id(0)
        for axis in range(1, len(grid)):
            step = step * grid[axis] + pl.program_id(axis)
        pl.when(step == 0)(start)
        core(*ins, *outs, *scratch)
        pl.when(step == comm.forward_step(n_steps))(forward)
        pl.when(step == n_steps - 1)(finish)

    results = pl.pallas_call(
        body, name=name, grid=grid,
        in_specs=list(in_specs) + [any_spec] * n_arr,
        out_specs=list(out_specs) + [any_spec] * n_arr,
        out_shape=list(out_shape) + comm.out_shape,
        scratch_shapes=list(scratch_shapes) + comm.scratch_shapes,
        compiler_params=_params(len(grid)),
    )(*operands, *comm.arrays)
    return results[:n_out], results[n_out:]


def _comm_call(comm, name):
    return _hosted_call(lambda: None, comm, name, 1, [], [], [], [], [])[1]


def _split_exchange_copies(srcs, lands, scatter, send_sems, recv_sems):
    me = _device_index()
    out = []
    for a in range(len(srcs)):
        for k in range(1, N_DEV):
            p = me ^ k
            out.append(pltpu.make_async_remote_copy(
                src_ref=srcs[a].at[p] if scatter[a] else srcs[a], dst_ref=lands[a].at[me],
                send_sem=send_sems[a].at[k - 1], recv_sem=recv_sems[a].at[k - 1],
                device_id=(p >> 2, (p >> 1) & 1, p & 1), device_id_type=MESH))
    return out


def _exchange_start(arrays, scatter, name):
    n = len(arrays)
    me = _device_index()
    lands = []
    for a, s in zip(arrays, scatter):
        own = lax.dynamic_slice_in_dim(a, me, 1, axis=0) if s else a[None]
        empty = jnp.zeros(a.shape if s else (N_DEV, *a.shape), a.dtype)
        lands.append(lax.dynamic_update_slice_in_dim(empty, own, me, axis=0))

    def body(*refs):
        srcs, zones = refs[:n], refs[n:2 * n]
        send_sems, recv_sems = refs[2 * n:3 * n], refs[3 * n:4 * n]
        for cp in _split_exchange_copies(srcs, zones, scatter, send_sems, recv_sems):
            cp.start()
        refs[-1][...] = jnp.zeros(refs[-1].shape, F32)

    hbm = pl.BlockSpec(memory_space=pltpu.HBM)
    sem = pl.BlockSpec(memory_space=pltpu.SEMAPHORE)
    operands = [pltpu.with_memory_space_constraint(a, pltpu.HBM) for a in list(arrays) + lands]
    results = pl.pallas_call(
        body, name=name,
        out_shape=(*[pltpu.SemaphoreType.DMA((N_DEV - 1,))] * (2 * n), *[pltpu.HBM(a.shape, a.dtype) for a in operands],
                   _sds((SUBLANES, LANES), F32)),
        in_specs=[hbm] * (2 * n),
        out_specs=(*[sem] * (2 * n), *[hbm] * (2 * n), pl.BlockSpec(memory_space=pltpu.VMEM)),
        input_output_aliases={i: 2 * n + i for i in range(2 * n)},
        compiler_params=pltpu.CompilerParams(has_side_effects=pltpu.SideEffectType.DATAFLOW_SIDE_EFFECTING),
    )(*operands)
    return (results[:n], results[n:2 * n], results[2 * n:3 * n], results[3 * n:4 * n]), results[-1]


def _exchange_wait(started, scatter, after, name):
    send_sems, recv_sems, srcs, lands = started
    n = len(srcs)

    def body(*refs):
        src_refs, zones = refs[:n], refs[n:2 * n]
        send, recv = refs[2 * n:3 * n], refs[3 * n:4 * n]
        for cp in _split_exchange_copies(src_refs, zones, scatter, send, recv):
            cp.wait_send()
            cp.wait_recv()

    hbm = pl.BlockSpec(memory_space=pltpu.HBM)
    sem = pl.BlockSpec(memory_space=pltpu.SEMAPHORE)
    results = pl.pallas_call(
        body, name=name,
        out_shape=tuple(pltpu.HBM(a.shape, a.dtype) for a in list(srcs) + list(lands)),
        in_specs=[hbm] * (2 * n) + [sem] * (2 * n) + [pl.BlockSpec(memory_space=pl.ANY)],
        out_specs=tuple([hbm] * (2 * n)),
        input_output_aliases={i: i for i in range(2 * n)},
        compiler_params=pltpu.CompilerParams(has_side_effects=pltpu.SideEffectType.DATAFLOW_SIDE_EFFECTING),
    )(*srcs, *lands, *send_sems, *recv_sems, after)
    return results[n:]


def _norm_matmul(x, g, w, tm, name, comm):
    t, d = x.shape
    n = w.shape[1]

    def core(x_ref, g_ref, w_ref, h_ref, o_ref):
        h = _rms_fwd(x_ref[...], g_ref[...]).astype(BF16)
        h_ref[...] = h
        o_ref[...] = _dot(h, w_ref[...]).astype(BF16)

    return _hosted_call(
        core, comm, name, t // tm,
        in_specs=[_tile(tm, d), _whole((1, d)), _resident((d, n))],
        out_specs=[_tile(tm, d), _tile(tm, n)],
        out_shape=[_sds((t, d), BF16), _sds((t, n), BF16)],
        scratch_shapes=[], operands=(x, g, w))


def _conv_fwd(u, wa, wb, bb, lg, lb, xres, w_out, g_post, tc, name, comm):
    t = u.shape[0]
    d = w_out.shape[1]
    chunk = CONV_FWD_CHUNK
    n_chunks = tc // chunk
    piece = 2 * LANES

    def core(u_ref, wa_ref, wb_ref, bb_ref, lg_ref, lb_ref, x_ref, wo_ref, go_ref,
             y_ref, zc_ref, ca_ref, x1_ref, s_ref, zbuf, cvbuf, zcbuf):
        @pl.when(pl.program_id(0) == 0)
        def _():
            zbuf[:, 0:HALO_B, :] = jnp.zeros((D_B // LANES, HALO_B, LANES), F32)
            cvbuf[:, 0:HALO_A, :] = jnp.zeros((D_A // LANES, HALO_A, LANES), F32)

        s = None
        for lb_i in range(D_A // LANES):
            if lb_i > 0 and lb_i % 2 == 0:
                cols = slice((lb_i - 2) * LANES, lb_i * LANES)
                part = _dot(y_ref[:, cols], wo_ref[cols, :])
                s = part if s is None else s + part
            lanes = slice(lb_i * LANES, (lb_i + 1) * LANES)
            c_a = u_ref[:, D_A + lb_i * LANES:D_A + (lb_i + 1) * LANES].astype(F32)
            v_a = u_ref[:, 2 * D_A + lb_i * LANES:2 * D_A + (lb_i + 1) * LANES].astype(F32)
            cvbuf[lb_i, HALO_A:HALO_A + tc, :] = c_a * v_a
            glu_v = u_ref[:, 3 * D_A + lb_i * LANES:3 * D_A + (lb_i + 1) * LANES].astype(F32)
            glu_g = u_ref[:, 3 * D_A + D_B + lb_i * LANES:3 * D_A + D_B + (lb_i + 1) * LANES].astype(F32)
            zbuf[lb_i, HALO_B:HALO_B + tc, :] = glu_v * _sigmoid(glu_g)

            for c in range(n_chunks):
                r0 = c * chunk
                rows = slice(r0, r0 + chunk)
                acc = jnp.zeros((chunk, LANES), F32)
                for k in range(CONV_A_W):
                    off = r0 + HALO_A - (CONV_A_W - 1) + k
                    acc = acc + wa_ref[k:k + 1, lanes] * cvbuf[lb_i, off:off + chunk, :]
                ca_ref[rows, lanes] = acc.astype(BF16)
                y_ref[rows, lanes] = (u_ref[rows, lanes].astype(F32) * acc).astype(BF16)

                accb = jnp.zeros((chunk, LANES), F32)
                for k in range(CONV_B_W):
                    off = r0 + HALO_B - (CONV_B_W - 1) + k
                    accb = accb + wb_ref[k:k + 1, lanes] * zbuf[lb_i, off:off + chunk, :]
                zcbuf[rows, lanes] = accb + bb_ref[:, lanes]

            zbuf[lb_i, 0:HALO_B, :] = zbuf[lb_i, tc:tc + HALO_B, :]
            cvbuf[lb_i, 0:HALO_A, :] = cvbuf[lb_i, tc:tc + HALO_A, :]

        cols = slice(D_A - piece, D_A)
        s = s + _dot(y_ref[:, cols], wo_ref[cols, :])

        for c in range(n_chunks):
            rows = slice(c * chunk, (c + 1) * chunk)
            zc = zcbuf[rows, :]
            zc_ref[rows, :] = zc.astype(BF16)
            mu = jnp.mean(zc, axis=-1, keepdims=True)
            xc = zc - mu
            var = jnp.mean(xc * xc, axis=-1, keepdims=True)
            ln = xc * lax.rsqrt(var + LN_EPS) * lg_ref[...] + lb_ref[...]
            y_ref[rows, D_A:D_A + D_B] = (ln * _sigmoid(ln)).astype(BF16)

        s = s + _dot(y_ref[:, D_A:D_A + D_B], wo_ref[D_A:D_A + D_B, :])
        s_ref[...] = s.astype(BF16)
        x1_ref[...] = x_ref[...] + _rms_fwd(s, go_ref[...])

    return _hosted_call(
        core, comm, name, t // tc,
        in_specs=[_tile(tc, D_IN_ALL), _whole((CONV_A_W, D_A)), _whole((CONV_B_W, D_B)), _whole((1, D_B)),
                  _whole((1, D_B)), _whole((1, D_B)), _tile(tc, d), _resident((D_A + D_B, d)), _whole((1, d))],
        out_specs=[_tile(tc, D_A + D_B), _tile(tc, D_B), _tile(tc, D_A), _tile(tc, d), _tile(tc, d)],
        out_shape=[_sds((t, D_A + D_B), BF16), _sds((t, D_B), BF16), _sds((t, D_A), BF16), _sds((t, d), F32),
                   _sds((t, d), BF16)],
        scratch_shapes=[pltpu.VMEM((D_B // LANES, HALO_B + tc, LANES), F32),
                        pltpu.VMEM((D_A // LANES, HALO_A + tc, LANES), F32), pltpu.VMEM((tc, D_B), F32)],
        operands=(u, wa, wb, bb, lg, lb, xres, w_out, g_post))


def _mem_fwd(mem, g, wk, wv, name):
    m, d = mem.shape

    def body(mem_ref, g_ref, wk_ref, wv_ref, n_ref, k_ref, v_ref):
        n = _rms_fwd(mem_ref[...], g_ref[...]).astype(BF16)
        n_ref[...] = n
        k_ref[...] = _dot(n, wk_ref[...]).astype(BF16)
        v_ref[...] = _dot(n, wv_ref[...]).astype(BF16)

    return pl.pallas_call(
        body, name=name, grid=(1,),
        in_specs=[_whole((m, d)), _whole((1, d)), _whole((d, d)), _whole((d, d))],
        out_specs=[_whole((m, d))] * 3,
        out_shape=[_sds((m, d), BF16)] * 3,
        compiler_params=_params(),
    )(mem, g, wk, wv)


def _softmax_rows(s):
    e = jnp.exp(s - jnp.max(s, axis=-1, keepdims=True))
    return e / jnp.sum(e, axis=-1, keepdims=True)


def _attn_fwd(x1, g_pre, wq, k, v, wo, g_post, tm, name):
    t, d = x1.shape
    m = k.shape[0]
    scale = XA_HEAD_DIM ** -0.5

    def body(x_ref, gp_ref, wq_ref, k_ref, v_ref, wo_ref, go_ref, x2_ref, h_ref, q_ref, o_ref, a_ref):
        x = x_ref[...]
        h = _rms_fwd(x, gp_ref[...]).astype(BF16)
        h_ref[...] = h
        q_ref[...] = _dot(h, wq_ref[...]).astype(BF16)
        for hd in range(XA_HEADS):
            cols = slice(hd * XA_HEAD_DIM, (hd + 1) * XA_HEAD_DIM)
            p = _softmax_rows(_dot_nt(q_ref[:, cols], k_ref[:, cols]) * scale)
            o_ref[:, cols] = _dot(p.astype(BF16), v_ref[:, cols]).astype(BF16)
        a = _dot(o_ref[...], wo_ref[...])
        a_ref[...] = a.astype(BF16)
        x2_ref[...] = x + _rms_fwd(a, go_ref[...])

    return pl.pallas_call(
        body, name=name, grid=(t // tm,),
        in_specs=[_tile(tm, d), _whole((1, d)), _resident((d, d)), _whole((m, d)), _whole((m, d)), _resident((d, d)),
                  _whole((1, d))],
        out_specs=[_tile(tm, d)] * 5,
        out_shape=[_sds((t, d), F32)] + [_sds((t, d), BF16)] * 4,
        compiler_params=_params(),
    )(x1, g_pre, wq, k, v, wo, g_post)


def _ffn_fwd(x2, g_pre, wg_t, wu_t, wd, g_post, target, tm, name):
    t, d = x2.shape
    f = wg_t.shape[0]

    def body(x_ref, gp_ref, wg_ref, wu_ref, wd_ref, go_ref, tgt_ref, h_ref, gt_ref, up_ref, f_ref, sq_ref):
        _zero_at_first_step(sq_ref)
        x = x_ref[...]
        h = _rms_fwd(x, gp_ref[...]).astype(BF16)
        h_ref[...] = h
        gt = _dot_nt(h, wg_ref[...])
        up = _dot_nt(h, wu_ref[...])
        gt_ref[...] = gt.astype(BF16)
        up_ref[...] = up.astype(BF16)
        hd = (gt * _sigmoid(gt) * up).astype(BF16)
        ff = _dot(hd, wd_ref[...])
        f_ref[...] = ff.astype(BF16)
        err = x + _rms_fwd(ff, go_ref[...]) - tgt_ref[...]
        sq_ref[...] += jnp.sum(err * err, axis=0, keepdims=True)

    return pl.pallas_call(
        body, name=name, grid=(t // tm,),
        in_specs=[_tile(tm, d), _whole((1, d)), _resident((f, d)), _resident((f, d)), _resident((f, d)), _whole((1, d)),
                  _tile(tm, d)],
        out_specs=[_tile(tm, d), _tile(tm, f), _tile(tm, f), _tile(tm, d), _whole((1, d))],
        out_shape=[_sds((t, d), BF16), _sds((t, f), BF16), _sds((t, f), BF16), _sds((t, d), BF16), _sds((1, d), F32)],
        compiler_params=_params(),
    )(x2, g_pre, wg_t, wu_t, wd, g_post, target)


def _ffn_bwd(x2, f, target, gt, up, g_pre, wg_t, wu_t, wd, g_post, tm, name):
    t, d = x2.shape
    ff = wg_t.shape[0]

    def body(x_ref, f_ref, tgt_ref, gt_ref, up_ref, gp_ref, wg_ref, wu_ref, wd_ref, go_ref,
             dx_ref, df_ref, hd_ref, dgt_ref, dup_ref, dgo_ref, dgp_ref):
        _zero_at_first_step(dgo_ref, dgp_ref)
        x = x_ref[...]
        fo = f_ref[...].astype(F32)
        dx3 = (x + _rms_fwd(fo, go_ref[...]) - tgt_ref[...]) * (1.0 / d)
        df, dgo = _rms_bwd(dx3, fo, go_ref[...])
        dgo_ref[...] += dgo
        df = df.astype(BF16)
        df_ref[...] = df
        dhd = _dot_nt(df, wd_ref[...])
        gt = gt_ref[...].astype(F32)
        up = up_ref[...].astype(F32)
        sg = _sigmoid(gt)
        si = gt * sg
        hd_ref[...] = (si * up).astype(BF16)
        dup = (dhd * si).astype(BF16)
        dgt = (dhd * up * (sg * (1.0 + gt * (1.0 - sg)))).astype(BF16)
        dup_ref[...] = dup
        dgt_ref[...] = dgt
        dh = _dot(dgt, wg_ref[...]) + _dot(dup, wu_ref[...])
        dxn, dgp = _rms_bwd(dh, x, gp_ref[...])
        dgp_ref[...] += dgp
        dx_ref[...] = dx3 + dxn

    return pl.pallas_call(
        body, name=name, grid=(t // tm,),
        in_specs=[_tile(tm, d), _tile(tm, d), _tile(tm, d), _tile(tm, ff), _tile(tm, ff), _whole((1, d)),
                  _resident((ff, d)), _resident((ff, d)), _resident((ff, d)), _whole((1, d))],
        out_specs=[_tile(tm, d), _tile(tm, d), _tile(tm, ff), _tile(tm, ff), _tile(tm, ff), _whole((1, d)), _whole((1, d))],
        out_shape=[_sds((t, d), F32), _sds((t, d), BF16), _sds((t, ff), BF16), _sds((t, ff), BF16), _sds((t, ff), BF16),
                   _sds((1, d), F32), _sds((1, d), F32)],
        compiler_params=_params(),
    )(x2, f, target, gt, up, g_pre, wg_t, wu_t, wd, g_post)


def _attn_bwd(dx2, a, x1, q, k, v, g_pre, wq, wo, g_post, s_mix, w_mix_out, g_mix_post, tm, name, after):
    t, d = x1.shape
    m = k.shape[0]
    kdim = w_mix_out.shape[0]
    scale = XA_HEAD_DIM ** -0.5

    def core(after_ref, dx2_ref, a_ref, x_ref, q_ref, k_ref, v_ref, gp_ref, wq_ref, wo_ref, go_ref, s_ref, wm_ref,
             gm_ref, dx1_ref, da_ref, dq_ref, dk_ref, dv_ref, dgo_ref, dgp_ref, ds_ref, dy_ref, dgm_ref, do_buf):
        del after_ref
        _zero_at_first_step(dgo_ref, dgp_ref, dk_ref, dv_ref, dgm_ref)
        dx2 = dx2_ref[...]
        da, dgo = _rms_bwd(dx2, a_ref[...].astype(F32), go_ref[...])
        dgo_ref[...] += dgo
        da = da.astype(BF16)
        da_ref[...] = da
        do_buf[...] = _dot_nt(da, wo_ref[...]).astype(BF16)
        for hd in range(XA_HEADS):
            cols = slice(hd * XA_HEAD_DIM, (hd + 1) * XA_HEAD_DIM)
            qh = q_ref[:, cols]
            p = _softmax_rows(_dot_nt(qh, k_ref[:, cols]) * scale)
            do_h = do_buf[:, cols]
            dp = _dot_nt(do_h, v_ref[:, cols])
            dv_ref[:, cols] += _dot_tn(p.astype(BF16), do_h)
            ds = (p * (dp - jnp.sum(dp * p, axis=-1, keepdims=True)) * scale).astype(BF16)
            dq_ref[:, cols] = _dot(ds, k_ref[:, cols]).astype(BF16)
            dk_ref[:, cols] += _dot_tn(ds, qh)
        dh = _dot_nt(dq_ref[...], wq_ref[...])
        dxn, dgp = _rms_bwd(dh, x_ref[...], gp_ref[...])
        dgp_ref[...] += dgp
        dx1 = dx2 + dxn
        dx1_ref[...] = dx1
        ds, dgm = _rms_bwd(dx1, s_ref[...].astype(F32), gm_ref[...])
        dgm_ref[...] += dgm
        ds = ds.astype(BF16)
        ds_ref[...] = ds
        dy_ref[...] = _dot_nt(ds, wm_ref[...]).astype(BF16)

    return pl.pallas_call(
        core, name=name, grid=(t // tm,), compiler_params=_params(),
        in_specs=[pl.BlockSpec(memory_space=pl.ANY),
                  _tile(tm, d), _tile(tm, d), _tile(tm, d), _tile(tm, d), _whole((m, d)), _whole((m, d)), _whole((1, d)),
                  _resident((d, d)), _resident((d, d)), _whole((1, d)), _tile(tm, d), _resident((kdim, d)), _whole((1, d))],
        out_specs=[_tile(tm, d), _tile(tm, d), _tile(tm, d), _whole((m, d)), _whole((m, d)), _whole((1, d)), _whole((1, d)),
                   _tile(tm, d), _tile(tm, kdim), _whole((1, d))],
        out_shape=[_sds((t, d), F32), _sds((t, d), BF16), _sds((t, d), BF16), _sds((m, d), F32), _sds((m, d), F32),
                   _sds((1, d), F32), _sds((1, d), F32), _sds((t, d), BF16), _sds((t, kdim), BF16), _sds((1, d), F32)],
        scratch_shapes=[pltpu.VMEM((tm, d), BF16)],
    )(after, dx2, a, x1, q, k, v, g_pre, wq, wo, g_post, s_mix, w_mix_out, g_mix_post)


def _mem_bwd(mem, mem_n, dk, dv, g, wk, wv, name):
    m, d = mem.shape

    def body(mem_ref, n_ref, dk_ref, dv_ref, g_ref, wk_ref, wv_ref, dwk_ref, dwv_ref, dg_ref):
        dk = dk_ref[...].astype(BF16)
        dv = dv_ref[...].astype(BF16)
        n = n_ref[...]
        dwk_ref[...] = _dot_tn(n, dk).astype(BF16)
        dwv_ref[...] = _dot_tn(n, dv).astype(BF16)
        dn = _dot_nt(dk, wk_ref[...]) + _dot_nt(dv, wv_ref[...])
        _, dg = _rms_bwd(dn, mem_ref[...], g_ref[...])
        dg_ref[...] = dg

    return pl.pallas_call(
        body, name=name, grid=(1,),
        in_specs=[_whole((m, d)), _whole((m, d)), _whole((m, d)), _whole((m, d)), _whole((1, d)), _whole((d, d)),
                  _whole((d, d))],
        out_specs=[_whole((d, d)), _whole((d, d)), _whole((1, d))],
        out_shape=[_sds((d, d), BF16), _sds((d, d), BF16), _sds((1, d), F32)],
        compiler_params=_params(),
    )(mem, mem_n, dk, dv, g, wk, wv)


def _conv_bwd(dy, u, zc, ca, wa, wb, lg, lb, tc, name, comm):
    t = u.shape[0]
    n_tiles = t // tc
    n_chunks = tc // CONV_CHUNK

    def core(dy_ref, u_ref, zc_ref, ca_ref, wa_ref, wb_ref, lg_ref, lb_ref,
             du_ref, dwa_ref, dwb_ref, dbb_ref, dlg_ref, dlb_ref, ebuf, eabuf, zbuf, cvbuf, gatebuf, wacc, aacc, vacc):
        step = pl.program_id(0)

        @pl.when(step == 0)
        def _():
            ebuf[:, tc:tc + HALO_B, :] = jnp.zeros((D_B // LANES, HALO_B, LANES), F32)
            eabuf[:, tc:tc + HALO_A, :] = jnp.zeros((D_A // LANES, HALO_A, LANES), F32)
            wacc[...] = jnp.zeros_like(wacc)
            aacc[...] = jnp.zeros_like(aacc)
            vacc[...] = jnp.zeros_like(vacc)

        dbb = jnp.zeros((SUBLANES, D_B), F32)
        dlg = jnp.zeros((SUBLANES, D_B), F32)
        dlb = jnp.zeros((SUBLANES, D_B), F32)
        for c in range(n_chunks):
            rows = slice(c * CONV_CHUNK, (c + 1) * CONV_CHUNK)
            dy_a = dy_ref[rows, 0:D_A].astype(F32)
            b_a = u_ref[rows, 0:D_A].astype(F32)
            du_ref[rows, 0:D_A] = (dy_a * ca_ref[rows, :].astype(F32)).astype(BF16)
            dca = dy_a * b_a
            cv = u_ref[rows, D_A:2 * D_A].astype(F32) * u_ref[rows, 2 * D_A:3 * D_A].astype(F32)
            gate = _sigmoid(u_ref[rows, 3 * D_A + D_B:3 * D_A + 2 * D_B].astype(F32))
            z = u_ref[rows, 3 * D_A:3 * D_A + D_B].astype(F32) * gate
            for lb_i in range(D_A // LANES):
                lanes = slice(lb_i * LANES, (lb_i + 1) * LANES)
                eabuf[lb_i, rows, :] = dca[:, lanes]
                cvbuf[lb_i, rows, :] = cv[:, lanes]
                zbuf[lb_i, rows, :] = z[:, lanes]
                gatebuf[lb_i, rows, :] = gate[:, lanes]

            zcv = zc_ref[rows, :].astype(F32)
            mu = jnp.mean(zcv, axis=-1, keepdims=True)
            xc = zcv - mu
            rstd = lax.rsqrt(jnp.mean(xc * xc, axis=-1, keepdims=True) + LN_EPS)
            xhat = xc * rstd
            ln = xhat * lg_ref[...] + lb_ref[...]
            sg = _sigmoid(ln)
            dln = dy_ref[rows, D_A:D_A + D_B].astype(F32) * (sg * (1.0 + ln * (1.0 - sg)))
            dlg = dlg + jnp.sum((dln * xhat).reshape(CONV_CHUNK // SUBLANES, SUBLANES, D_B), axis=0)
            dlb = dlb + jnp.sum(dln.reshape(CONV_CHUNK // SUBLANES, SUBLANES, D_B), axis=0)
            dxh = dln * lg_ref[...]
            dzc = rstd * (dxh - jnp.mean(dxh, axis=-1, keepdims=True) - xhat * jnp.mean(dxh * xhat, axis=-1, keepdims=True))
            dbb = dbb + jnp.sum(dzc.reshape(CONV_CHUNK // SUBLANES, SUBLANES, D_B), axis=0)
            for lb_i in range(D_B // LANES):
                ebuf[lb_i, rows, :] = dzc[:, lb_i * LANES:(lb_i + 1) * LANES]
        vacc[0] += dbb
        vacc[1] += dlg
        vacc[2] += dlb

        for lb_i in range(D_A // LANES):
            lanes = slice(lb_i * LANES, (lb_i + 1) * LANES)

            def cols(first):
                return slice(first + lb_i * LANES, first + (lb_i + 1) * LANES)

            for c in range(n_chunks):
                r0 = c * CONV_CHUNK
                rows = slice(r0, r0 + CONV_CHUNK)
                cv = cvbuf[lb_i, rows, :]
                dcv = jnp.zeros((CONV_CHUNK, LANES), F32)
                for k in range(CONV_A_W):
                    off = r0 + (CONV_A_W - 1) - k
                    e = eabuf[lb_i, off:off + CONV_CHUNK, :]
                    dcv = dcv + wa_ref[k:k + 1, lanes] * e
                    aacc[k, :, lanes] += jnp.sum((cv * e).reshape(CONV_CHUNK // SUBLANES, SUBLANES, LANES), axis=0)
                du_ref[rows, cols(D_A)] = (dcv * u_ref[rows, cols(2 * D_A)].astype(F32)).astype(BF16)
                du_ref[rows, cols(2 * D_A)] = (dcv * u_ref[rows, cols(D_A)].astype(F32)).astype(BF16)

                z = zbuf[lb_i, rows, :]
                dz = jnp.zeros((CONV_CHUNK, LANES), F32)
                for k in range(CONV_B_W):
                    off = r0 + (CONV_B_W - 1) - k
                    e = ebuf[lb_i, off:off + CONV_CHUNK, :]
                    dz = dz + wb_ref[k:k + 1, lanes] * e
                    wacc[k, :, lanes] += jnp.sum((z * e).reshape(CONV_CHUNK // SUBLANES, SUBLANES, LANES), axis=0)
                glu_v = u_ref[rows, cols(3 * D_A)].astype(F32)
                sgg = gatebuf[lb_i, rows, :]
                du_ref[rows, cols(3 * D_A)] = (dz * sgg).astype(BF16)
                du_ref[rows, cols(3 * D_A + D_B)] = (dz * glu_v * sgg * (1.0 - sgg)).astype(BF16)

            ebuf[lb_i, tc:tc + HALO_B, :] = ebuf[lb_i, 0:HALO_B, :]
            eabuf[lb_i, tc:tc + HALO_A, :] = eabuf[lb_i, 0:HALO_A, :]

        @pl.when(step == n_tiles - 1)
        def _():
            for k in range(CONV_B_W):
                dwb_ref[k:k + 1, :] = jnp.sum(wacc[k], axis=0, keepdims=True)
            for k in range(CONV_A_W):
                dwa_ref[k:k + 1, :] = jnp.sum(aacc[k], axis=0, keepdims=True)
            dbb_ref[...] = jnp.sum(vacc[0], axis=0, keepdims=True)
            dlg_ref[...] = jnp.sum(vacc[1], axis=0, keepdims=True)
            dlb_ref[...] = jnp.sum(vacc[2], axis=0, keepdims=True)

    return _hosted_call(
        core, comm, name, n_tiles,
        in_specs=[_rtile(tc, D_A + D_B, n_tiles), _rtile(tc, D_IN_ALL, n_tiles), _rtile(tc, D_B, n_tiles),
                  _rtile(tc, D_A, n_tiles), _whole((CONV_A_W, D_A)), _whole((CONV_B_W, D_B)), _whole((1, D_B)),
                  _whole((1, D_B))],
        out_specs=[_rtile(tc, D_IN_ALL, n_tiles), _whole((CONV_A_W, D_A)), _whole((CONV_B_W, D_B)), _whole((1, D_B)),
                   _whole((1, D_B)), _whole((1, D_B))],
        out_shape=[_sds((t, D_IN_ALL), BF16), _sds((CONV_A_W, D_A), F32), _sds((CONV_B_W, D_B), F32), _sds((1, D_B), F32),
                   _sds((1, D_B), F32), _sds((1, D_B), F32)],
        scratch_shapes=[pltpu.VMEM((D_B // LANES, tc + HALO_B, LANES), F32), pltpu.VMEM((D_A // LANES, tc + HALO_A, LANES), F32),
                        pltpu.VMEM((D_B // LANES, tc, LANES), F32), pltpu.VMEM((D_A // LANES, tc, LANES), F32),
                        pltpu.VMEM((D_B // LANES, tc, LANES), F32),
                        pltpu.VMEM((CONV_B_W, SUBLANES, D_B), F32), pltpu.VMEM((CONV_A_W, SUBLANES, D_A), F32),
                        pltpu.VMEM((3, SUBLANES, D_B), F32)],
        operands=(dy, u, zc, ca, wa, wb, lg, lb))


def _in_bwd(du, w_t, x, dx1, g, tm, name, after):
    t, d = x.shape
    n = w_t.shape[0]

    def body(after_ref, du_ref, w_ref, x_ref, dx1_ref, g_ref, dx_ref, dg_ref):
        del after_ref
        _zero_at_first_step(dg_ref)
        dh = _dot(du_ref[...], w_ref[...])
        dxn, dg = _rms_bwd(dh, x_ref[...], g_ref[...])
        dg_ref[...] += dg
        dx_ref[...] = dx1_ref[...] + dxn

    return pl.pallas_call(
        body, name=name, grid=(t // tm,),
        in_specs=[pl.BlockSpec(memory_space=pl.ANY), _tile(tm, n), _resident((n, d)), _tile(tm, d), _tile(tm, d),
                  _whole((1, d))],
        out_specs=[_tile(tm, d), _whole((1, d))],
        out_shape=[_sds((t, d), F32), _sds((1, d), F32)],
        compiler_params=_params(),
    )(after, du, w_t, x, dx1, g)


def _wgrad(a, b, tk, bm, bn, name, comm=None, cols=None):
    t, m = a.shape
    first, n = (0, b.shape[1]) if cols is None else cols
    first_block = first // bn
    n_k = t // tk

    def body(a_ref, b_ref, o_ref, acc):
        @pl.when(pl.program_id(2) == 0)
        def _():
            acc[...] = jnp.zeros_like(acc)

        acc[...] += _dot_tn(a_ref[...], b_ref[...])

        @pl.when(pl.program_id(2) == n_k - 1)
        def _():
            o_ref[...] = acc[...].astype(BF16)

    call = dict(
        grid=(m // bm, n // bn, n_k),
        in_specs=[pl.BlockSpec((tk, bm), lambda i, j, k: (k, i)),
                  pl.BlockSpec((tk, bn), lambda i, j, k: (k, first_block + j))],
        out_specs=[pl.BlockSpec((bm, bn), lambda i, j, k: (i, j))],
        out_shape=[_sds((m, n), BF16)],
        scratch_shapes=[pltpu.VMEM((bm, bn), F32)])
    if comm is None:
        return pl.pallas_call(body, name=name, compiler_params=_params(3), **call)(a, b)[0]
    (out,), received = _hosted_call(body, comm, name, operands=(a, b), **call)
    return out, received


def _adamw(items, tr, name, after=None):
    n_parts, r, c = items[0][0].shape
    n_items = len(items)
    order = [] if after is None else [after]

    def body(*refs):
        refs = refs[len(order):]
        ins, outs = refs[:4 * n_items], refs[4 * n_items:]
        for i in range(n_items):
            p_ref, w_ref, m_ref, v_ref = ins[4 * i:4 * i + 4]
            g_ref, d_ref, nm_ref, nv_ref = outs[4 * i:4 * i + 4]
            g = p_ref[0].astype(F32)
            for j in range(1, n_parts):
                g = g + p_ref[j].astype(F32)
            g_ref[...] = g
            nm = ADAM_B1 * m_ref[...] + (1.0 - ADAM_B1) * g
            nv = ADAM_B2 * v_ref[...] + (1.0 - ADAM_B2) * (g * g)
            nm_ref[...] = nm
            nv_ref[...] = nv
            m_hat = nm / (1.0 - ADAM_B1 ** ADAM_STEP)
            v_hat = nv / (1.0 - ADAM_B2 ** ADAM_STEP)
            d_ref[...] = -ADAM_LR * (m_hat / (jnp.sqrt(v_hat) + ADAM_EPS) + ADAM_WD * w_ref[...])

    results = pl.pallas_call(
        body, name=name, grid=(r // tr,),
        in_specs=[pl.BlockSpec(memory_space=pl.ANY)] * len(order)
        + ([pl.BlockSpec((n_parts, tr, c), lambda i: (0, i, 0))] + [_tile(tr, c)] * 3) * n_items,
        out_specs=[_tile(tr, c)] * (4 * n_items),
        out_shape=[_sds((r, c), F32)] * (4 * n_items),
        compiler_params=_params(),
    )(*order, *[a for item in items for a in item])
    return [results[4 * i:4 * i + 4] for i in range(n_items)]


def _sum_parts(parts, name):
    n_parts, r, c = parts.shape

    def body(p_ref, o_ref):
        acc = p_ref[0]
        for j in range(1, n_parts):
            acc = acc + p_ref[j]
        o_ref[...] = acc

    return pl.pallas_call(
        body, name=name, grid=(1,),
        in_specs=[_whole((n_parts, r, c))], out_specs=_whole((r, c)), out_shape=_sds((r, c), F32),
        compiler_params=_params(),
    )(parts)


def _row(v):
    return v.reshape(1, -1)


def _by_owner_rows(g):
    return g.reshape(N_DEV, g.shape[0] // N_DEV, g.shape[1])


WEIGHTS = ("mix_pre_g", "w_mix_in", "conv_a_w", "conv_b_w", "conv_b_b", "ln_b_g", "ln_b_b", "w_mix_out", "mix_post_g",
           "xa_pre_g", "mem_norm_g", "w_q", "w_k", "w_v", "w_o", "xa_post_g", "ffn_pre_g", "w_gate", "w_up", "w_down",
           "ffn_post_g")
LARGE = ("w_mix_in", "w_mix_out", "w_q", "w_k", "w_v", "w_o", "w_gate", "w_up", "w_down")
COLUMN_SHARDED = ("w_mix_in", "w_gate", "w_up")
GAINS = ("mix_pre_g", "mix_post_g", "xa_pre_g", "mem_norm_g", "xa_post_g", "ffn_pre_g", "ffn_post_g")
CHANNEL_VECTORS = ("conv_b_b", "ln_b_g", "ln_b_b")
CONV_TAPS = ("conv_a_w", "conv_b_w")
SMALL = GAINS + CHANNEL_VECTORS + CONV_TAPS
CONV_COLS_PER_DEVICE = D_A // N_DEV
TOKEN_TILE = 512
FFN_FWD_TOKEN_TILE = 512
FFN_TOKEN_TILE = 256
CONV_TOKEN_TILE = 256
WGRAD_TOKEN_TILE = 2048
SMALL_WGRAD_TOKEN_TILE = 4096
IN_BWD_TOKEN_TILE = 1024
ADAM_ROWS_PER_STEP = 64


def _lane_rows(v):
    flat = v.reshape(-1)
    tile = SUBLANES * LANES
    flat = jnp.pad(flat, (0, (-flat.shape[0]) % tile))
    return flat.reshape(-1, LANES)


def _pack_small(values, names):
    return jnp.concatenate([_lane_rows(values[n]) for n in names], axis=0)


def _unpack_small(packed, like, names):
    out, off = {}, 0
    for n in names:
        size = like[n].size
        rows = _lane_rows(like[n]).shape[0]
        out[n] = packed[off:off + rows, :].reshape(-1)[:size].reshape(like[n].shape)
        off += rows
    return out


def kernel(x, mem, mix_pre_g, w_mix_in, conv_a_w, conv_b_w, conv_b_b, ln_b_g, ln_b_b, w_mix_out, mix_post_g, xa_pre_g, mem_norm_g, w_q, w_k, w_v, w_o, xa_post_g, ffn_pre_g, w_gate, w_up, w_down, ffn_post_g, loss_target, m_mix_pre_g, m_w_mix_in, m_conv_a_w, m_conv_b_w, m_conv_b_b, m_ln_b_g, m_ln_b_b, m_w_mix_out, m_mix_post_g, m_xa_pre_g, m_mem_norm_g, m_w_q, m_w_k, m_w_v, m_w_o, m_xa_post_g, m_ffn_pre_g, m_w_gate, m_w_up, m_w_down, m_ffn_post_g, v_mix_pre_g, v_w_mix_in, v_conv_a_w, v_conv_b_w, v_conv_b_b, v_ln_b_g, v_ln_b_b, v_w_mix_out, v_mix_post_g, v_xa_pre_g, v_mem_norm_g, v_w_q, v_w_k, v_w_v, v_w_o, v_xa_post_g, v_ffn_pre_g, v_w_gate, v_w_up, v_w_down, v_ffn_post_g):
    given = dict(locals())
    w = {n: given[n] for n in WEIGHTS}
    m = {n: given["m_" + n] for n in WEIGHTS}
    v = {n: given["v_" + n] for n in WEIGHTS}
    xs, mems, target = x[0], mem[0], loss_target[0]
    t = xs.shape[0]
    tm, tm_ffn, tc, tk = min(TOKEN_TILE, t), min(FFN_TOKEN_TILE, t), min(CONV_TOKEN_TILE, t), min(WGRAD_TOKEN_TILE, t)
    tk_small = min(SMALL_WGRAD_TOKEN_TILE, t)
    g = {n: _row(w[n]) for n in GAINS}
    bb, lg, lb = (_row(w[n]) for n in CHANNEL_VECTORS)

    def shard_bf16(*names):
        return [w[n].astype(BF16) for n in names]

    def shard_bf16_t(*names):
        return [w[n].T.astype(BF16) for n in names]

    g_mix_in, g_taps = _comm_call(_Gather(shard_bf16_t("w_mix_in") + [_pack_small(w, CONV_TAPS)]), "gather_mixer")
    w_mix_in_t = g_mix_in.reshape(D_IN_ALL, D_MODEL)
    taps, off = {}, 0
    for n in CONV_TAPS:
        k, cols = w[n].shape
        rows = _lane_rows(w[n]).shape[0]
        blk = g_taps[:, off:off + rows, :].reshape(N_DEV, -1)[:, :k * cols].reshape(N_DEV, k, cols)
        taps[n] = blk.transpose(1, 0, 2).reshape(k, N_DEV * cols)
        off += rows
    wa, wb = taps["conv_a_w"], taps["conv_b_w"]

    (h1, u), gathered = _norm_matmul(xs, g["mix_pre_g"], w_mix_in_t.T, tm, "mix_in_fwd",
                                     _Gather(shard_bf16("w_mix_out", "w_q", "w_k", "w_v", "w_o")))
    w_mix_out, w_q, w_k, w_v, w_o = (a.reshape(D_MODEL, D_MODEL) for a in gathered)
    (ycat, zc, ca, x1, s1), (g_gate, g_up, g_down) = _conv_fwd(
        u, wa, wb, bb, lg, lb, xs, w_mix_out, g["mix_post_g"], tc, "conv_fwd",
        _Gather(shard_bf16_t("w_gate", "w_up") + shard_bf16("w_down")))
    w_gate_t, w_up_t, w_down = (a.reshape(D_FF, D_MODEL) for a in (g_gate, g_up, g_down))
    mem_n, kk, vv = _mem_fwd(mems, g["mem_norm_g"], w_k, w_v, "mem_fwd")
    x2, h2, q, o, a = _attn_fwd(x1, g["xa_pre_g"], w_q, kk, vv, w_o, g["xa_post_g"], tm, "attn_fwd")
    h3, gt, up, f, sq = _ffn_fwd(x2, g["ffn_pre_g"], w_gate_t, w_up_t, w_down, g["ffn_post_g"], target,
                                 min(FFN_FWD_TOKEN_TILE, t), "ffn_fwd")

    dx2, df, hd, dgt, dup, d_ffn_post, d_ffn_pre = _ffn_bwd(
        x2, f, target, gt, up, g["ffn_pre_g"], w_gate_t, w_up_t, w_down, g["ffn_post_g"], tm_ffn, "ffn_bwd")
    d_w_down = _wgrad(hd, df, tk, D_FF // 2, D_MODEL, "wgrad_down")
    d_w_gate_t = _wgrad(dgt, h3, tk, D_FF // 2, D_MODEL, "wgrad_gate")
    d_w_up_t = _wgrad(dup, h3, tk, D_FF // 2, D_MODEL, "wgrad_up")
    ffn_slabs = [_by_owner_rows(d) for d in (d_w_gate_t, d_w_up_t, d_w_down)]

    ffn_copy, ffn_started = _exchange_start(ffn_slabs, [True] * 3, "ffn_start")
    dx1, da, dq, dk, dv, d_xa_post, d_xa_pre, ds1, dycat, d_mix_post = _attn_bwd(
        dx2, a, x1, q, kk, vv, g["xa_pre_g"], w_q, w_o, g["xa_post_g"], s1, w_mix_out, g["mix_post_g"], tm, "attn_bwd",
        ffn_started)
    from_ffn = _exchange_wait(ffn_copy, [True] * 3, d_mix_post, "ffn_wait")
    d_w_o = _wgrad(o, da, tk, D_MODEL, D_MODEL, "wgrad_o")
    d_w_q = _wgrad(h2, dq, tk, D_MODEL, D_MODEL, "wgrad_q")
    d_w_k, d_w_v, d_mem_norm = _mem_bwd(mems, mem_n, dk, dv, g["mem_norm_g"], w_k, w_v, "mem_bwd")
    d_w_mix_out = _wgrad(ycat, ds1, tk, D_MODEL, D_MODEL, "wgrad_mix_out")
    attn_slabs = [_by_owner_rows(d) for d in (d_w_mix_out, d_w_q, d_w_k, d_w_v, d_w_o)]

    (du, d_conv_a, d_conv_b, d_conv_bb, d_ln_g, d_ln_b), from_attn = _conv_bwd(
        dycat, u, zc, ca, wa, wb, lg, lb, tc, "conv_bwd", _Exchange(attn_slabs, [True] * 5))
    d_w_in_t = _wgrad(du, h1, tk, D_IN_ALL // 2, D_MODEL, "wgrad_mix_in")
    in_scatter = [True]
    in_copy, in_started = _exchange_start([_by_owner_rows(d_w_in_t)], in_scatter, "mix_in_start")
    dx, d_mix_pre = _in_bwd(du, w_mix_in_t, xs, dx1, g["mix_pre_g"], min(IN_BWD_TOKEN_TILE, t), "mix_in_bwd",
                            in_started)
    (from_in,) = _exchange_wait(in_copy, in_scatter, d_mix_pre, "mix_in_wait")
    small_grads = dict(mix_pre_g=d_mix_pre, conv_a_w=d_conv_a, conv_b_w=d_conv_b, conv_b_b=d_conv_bb, ln_b_g=d_ln_g,
                       ln_b_b=d_ln_b, mix_post_g=d_mix_post, xa_pre_g=d_xa_pre, mem_norm_g=d_mem_norm, xa_post_g=d_xa_post,
                       ffn_pre_g=d_ffn_pre, ffn_post_g=d_ffn_post, loss=sq)
    names = SMALL + ("loss",)
    tail_scatter = [False]
    tail, started = _exchange_start([_pack_small(small_grads, names)], tail_scatter, "tail_start")

    received = dict(zip(("w_gate", "w_up", "w_down"), from_ffn))
    received.update(zip(("w_mix_out", "w_q", "w_k", "w_v", "w_o"), from_attn))
    grad, delta, new_m, new_v = {}, {}, {}, {}

    def adamw_group(group, after=None):
        items = [[received[n]] + [a[n].T if n in COLUMN_SHARDED else a[n] for a in (w, m, v)] for n in group]
        rows = items[0][1].shape[0]
        results = _adamw(items, min(rows, ADAM_ROWS_PER_STEP) if len(group) > 1 else rows, "adamw_" + group[0], after)
        for n, result in zip(group, results):
            grad[n], delta[n], new_m[n], new_v[n] = [r.T if n in COLUMN_SHARDED else r for r in result]
        return results[-1][-1]

    behind = started
    for group in (("w_gate",), ("w_up",), ("w_down",), ("w_mix_out", "w_q", "w_k", "w_v", "w_o")):
        behind = adamw_group(group, behind)
    received["w_mix_in"] = from_in
    behind = adamw_group(("w_mix_in",), behind)
    (all_small,) = _exchange_wait(tail, tail_scatter, behind, "tail_wait")

    total = _unpack_small(_sum_parts(all_small, "sum_small"), small_grads, names)
    loss = jnp.sum(total.pop("loss")) * (0.5 / D_MODEL)
    first_col = _device_index() * CONV_COLS_PER_DEVICE
    for n in CONV_TAPS:
        total[n] = lax.dynamic_slice_in_dim(total[n], first_col, CONV_COLS_PER_DEVICE, axis=1)
    total = {n: total[n].reshape(w[n].shape) for n in SMALL}
    packed_small = [_pack_small(values, SMALL) for values in (total, w, m, v)]
    ((g_s, d_s, nm_s, nv_s),) = _adamw([[packed_small[0][None]] + packed_small[1:]], packed_small[0].shape[0],
                                       "adamw_small")
    for out, p in ((grad, g_s), (delta, d_s), (new_m, nm_s), (new_v, nv_s)):
        out.update(_unpack_small(p, w, SMALL))

    return (loss, dx[None], *[grad[n] for n in WEIGHTS], *[delta[n] for n in WEIGHTS], *[new_m[n] for n in WEIGHTS],
            *[new_v[n] for n in WEIGHTS])
```

```python
import jax
import jax.numpy as jnp
from jax import lax
from jax.experimental import pallas as pl
from jax.experimental.pallas import tpu as pltpu

F32 = jnp.float32
BF16 = jnp.bfloat16

D_MODEL = 1024
D_A = 512
D_B = 512
D_IN_ALL = 2560
CONV_A_W = 3
CONV_B_W = 31
XA_HEADS = 4
XA_HEAD_DIM = 256
D_FF = 2816
N_DEV = 8
RMS_EPS = 1e-6
LN_EPS = 1e-5
ADAM_LR = 0.001
ADAM_B1 = 0.9
ADAM_B2 = 0.999
ADAM_EPS = 1e-08
ADAM_WD = 0.01
ADAM_STEP = 10

VMEM_LIMIT_BYTES = 56 * 1024 * 1024
SUBLANES = 8
LANES = 128
HALO_B = 32
HALO_A = 8
CONV_FWD_CHUNK = 16
CONV_CHUNK = 32
GATHER_FORWARD_STEPS_BEFORE_END = 8

MESH = pl.DeviceIdType.MESH


def _params(n_grid_axes=1):
    return pltpu.CompilerParams(dimension_semantics=("arbitrary",) * n_grid_axes, vmem_limit_bytes=VMEM_LIMIT_BYTES)


def _sds(shape, dtype):
    return jax.ShapeDtypeStruct(shape, dtype)


def _tile(rows, cols):
    return pl.BlockSpec((rows, cols), lambda i: (i, 0))


def _rtile(rows, cols, n):
    return pl.BlockSpec((rows, cols), lambda i: (n - 1 - i, 0))


def _whole(shape):
    zeros = (0,) * len(shape)
    return pl.BlockSpec(shape, lambda i: zeros)


def _resident(shape):
    zeros = (0,) * len(shape)
    return pl.BlockSpec(shape, lambda i: zeros, pipeline_mode=pl.Buffered(1))


def _dot(a, b):
    return jnp.dot(a, b, preferred_element_type=F32)


def _dot_nt(a, b):
    return lax.dot_general(a, b, (((1,), (1,)), ((), ())), preferred_element_type=F32)


def _dot_tn(a, b):
    return lax.dot_general(a, b, (((0,), (0,)), ((), ())), preferred_element_type=F32)


def _sigmoid(x):
    return 1.0 / (1.0 + jnp.exp(-x))


def _rms_fwd(x, g):
    r = lax.rsqrt(jnp.mean(x * x, axis=-1, keepdims=True) + RMS_EPS)
    return x * r * g


def _rms_bwd(dy, xin, g):
    r = lax.rsqrt(jnp.mean(xin * xin, axis=-1, keepdims=True) + RMS_EPS)
    n = xin * r
    dg = jnp.sum(dy * n, axis=0, keepdims=True)
    dn = dy * g
    dx = r * (dn - n * jnp.mean(dn * n, axis=-1, keepdims=True))
    return dx, dg


def _zero_at_first_step(*refs):
    @pl.when(pl.program_id(0) == 0)
    def _():
        for ref in refs:
            ref[...] = jnp.zeros(ref.shape, ref.dtype)


def _place():
    return lax.axis_index("x"), lax.axis_index("y"), lax.axis_index("c")


def _device_index():
    x, y, c = _place()
    return 4 * x + 2 * y + c


class _Gather:
    def __init__(self, arrays):
        self.arrays = list(arrays)
        self.out_shape = [_sds((N_DEV, *a.shape), a.dtype) for a in self.arrays]
        n = len(self.arrays)
        self.scratch_shapes = [pltpu.SemaphoreType.DMA((n, 7)), pltpu.SemaphoreType.DMA((n, 7)),
                               pltpu.SemaphoreType.DMA((n,))]

    def forward_step(self, n_steps):
        return max(0, n_steps - 1 - GATHER_FORWARD_STEPS_BEFORE_END)

    def bind(self, srcs, dsts, send_sems, recv_sems, local_sems):
        x, y, cc = _place()
        me, sibling = (x, y, cc), (x, y, 1 - cc)
        chips = [(1 - x, y), (x, 1 - y), (1 - x, 1 - y)]

        def copy(a, k, owner, to, src=None):
            slot = dsts[a].at[4 * owner[0] + 2 * owner[1] + owner[2]]
            return pltpu.make_async_remote_copy(
                src_ref=slot if src is None else src, dst_ref=slot, send_sem=send_sems.at[a, k],
                recv_sem=recv_sems.at[a, k], device_id=to, device_id_type=MESH)

        def first(a):
            return [copy(a, 0, me, sibling, src=srcs[a])] + [
                copy(a, 1 + j, me, (*chip, cc), src=srcs[a]) for j, chip in enumerate(chips)]

        def passed(a, j):
            return copy(a, 4 + j, (*chips[j], cc), sibling)

        def mine(a):
            return pltpu.make_async_copy(srcs[a], dsts[a].at[4 * x + 2 * y + cc], local_sems.at[a])

        def start():
            for a in range(len(srcs)):
                mine(a).start()
                for cp in first(a):
                    cp.start()

        def forward():
            for a in range(len(srcs)):
                for j, chip in enumerate(chips):
                    copy(a, 1 + j, (*chip, cc), me).wait_recv()
                    passed(a, j).start()

        def finish():
            for a in range(len(srcs)):
                copy(a, 0, sibling, me).wait_recv()
                for j, chip in enumerate(chips):
                    copy(a, 4 + j, (*chip, 1 - cc), me).wait_recv()
                for cp in first(a) + [passed(a, j) for j in range(len(chips))]:
                    cp.wait_send()
                mine(a).wait()

        return start, forward, finish


class _Exchange:
    def __init__(self, arrays, scatter):
        self.arrays = list(arrays)
        self.scatter = list(scatter)
        self.out_shape = [_sds(a.shape if s else (N_DEV, *a.shape), a.dtype) for a, s in zip(self.arrays, self.scatter)]
        n = len(self.arrays)
        self.scratch_shapes = [pltpu.SemaphoreType.DMA((n, 7)), pltpu.SemaphoreType.DMA((n, 7)),
                               pltpu.SemaphoreType.DMA((n,))]

    def forward_step(self, n_steps):
        return n_steps - 1

    def bind(self, srcs, dsts, send_sems, recv_sems, local_sems):
        me = _device_index()

        def copies(a):
            out = []
            for k in range(1, N_DEV):
                p = me ^ k
                out.append(pltpu.make_async_remote_copy(
                    src_ref=srcs[a].at[p] if self.scatter[a] else srcs[a], dst_ref=dsts[a].at[me],
                    send_sem=send_sems.at[a, k - 1], recv_sem=recv_sems.at[a, k - 1],
                    device_id=(p >> 2, (p >> 1) & 1, p & 1), device_id_type=MESH))
            return out

        def mine(a):
            return pltpu.make_async_copy(srcs[a].at[me] if self.scatter[a] else srcs[a], dsts[a].at[me], local_sems.at[a])

        def start():
            for a in range(len(srcs)):
                mine(a).start()
                for cp in copies(a):
                    cp.start()

        def forward():
            pass

        def finish():
            for a in range(len(srcs)):
                for cp in copies(a):
                    cp.wait()
                mine(a).wait()

        return start, forward, finish


def _hosted_call(core, comm, name, grid, in_specs, out_specs, out_shape, scratch_shapes, operands):
    grid = (grid,) if isinstance(grid, int) else tuple(grid)
    n_steps = 1
    for extent in grid:
        n_steps *= extent
    n_in, n_out, n_scr, n_arr = len(in_specs), len(out_specs), len(scratch_shapes), len(comm.arrays)
    any_spec = pl.BlockSpec(memory_space=pl.ANY)

    def body(*refs):
        ins, refs = refs[:n_in], refs[n_in:]
        srcs, refs = refs[:n_arr], refs[n_arr:]
        outs, refs = refs[:n_out], refs[n_out:]
        dsts, refs = refs[:n_arr], refs[n_arr:]
        scratch, sems = refs[:n_scr], refs[n_scr:]
        start, forward, finish = comm.bind(srcs, dsts, *sems)
        step = pl.program_id(0)
        for axis in range(1, len(grid)):
            step = step * grid[axis] + pl.program_id(axis)
        pl.when(step == 0)(start)
        core(*ins, *outs, *scratch)
        pl.when(step == comm.forward_step(n_steps))(forward)
        pl.when(step == n_steps - 1)(finish)

    results = pl.pallas_call(
        body, name=name, grid=grid,
        in_specs=list(in_specs) + [any_spec] * n_arr,
        out_specs=list(out_specs) + [any_spec] * n_arr,
        out_shape=list(out_shape) + comm.out_shape,
        scratch_shapes=list(scratch_shapes) + comm.scratch_shapes,
        compiler_params=_params(len(grid)),
    )(*operands, *comm.arrays)
    return results[:n_out], results[n_out:]


def _comm_call(comm, name):
    return _hosted_call(lambda: None, comm, name, 1, [], [], [], [], [])[1]


def _split_exchange_copies(srcs, lands, scatter, send_sems, recv_sems):
    me = _device_index()
    out = []
    for a in range(len(srcs)):
        for k in range(1, N_DEV):
            p = me ^ k
            out.append(pltpu.make_async_remote_copy(
                src_ref=srcs[a].at[p] if scatter[a] else srcs[a], dst_ref=lands[a].at[me],
                send_sem=send_sems[a].at[k - 1], recv_sem=recv_sems[a].at[k - 1],
                device_id=(p >> 2, (p >> 1) & 1, p & 1), device_id_type=MESH))
    return out


def _exchange_start(arrays, scatter, name):
    n = len(arrays)
    me = _device_index()
    lands = []
    for a, s in zip(arrays, scatter):
        own = lax.dynamic_slice_in_dim(a, me, 1, axis=0) if s else a[None]
        empty = jnp.zeros(a.shape if s else (N_DEV, *a.shape), a.dtype)
        lands.append(lax.dynamic_update_slice_in_dim(empty, own, me, axis=0))

    def body(*refs):
        srcs, zones = refs[:n], refs[n:2 * n]
        send_sems, recv_sems = refs[2 * n:3 * n], refs[3 * n:4 * n]
        for cp in _split_exchange_copies(srcs, zones, scatter, send_sems, recv_sems):
            cp.start()
        refs[-1][...] = jnp.zeros(refs[-1].shape, F32)

    hbm = pl.BlockSpec(memory_space=pltpu.HBM)
    sem = pl.BlockSpec(memory_space=pltpu.SEMAPHORE)
    operands = [pltpu.with_memory_space_constraint(a, pltpu.HBM) for a in list(arrays) + lands]
    results = pl.pallas_call(
        body, name=name,
        out_shape=(*[pltpu.SemaphoreType.DMA((N_DEV - 1,))] * (2 * n), *[pltpu.HBM(a.shape, a.dtype) for a in operands],
                   _sds((SUBLANES, LANES), F32)),
        in_specs=[hbm] * (2 * n),
        out_specs=(*[sem] * (2 * n), *[hbm] * (2 * n), pl.BlockSpec(memory_space=pltpu.VMEM)),
        input_output_aliases={i: 2 * n + i for i in range(2 * n)},
        compiler_params=pltpu.CompilerParams(has_side_effects=pltpu.SideEffectType.DATAFLOW_SIDE_EFFECTING),
    )(*operands)
    return (results[:n], results[n:2 * n], results[2 * n:3 * n], results[3 * n:4 * n]), results[-1]


def _exchange_wait(started, scatter, after, name):
    send_sems, recv_sems, srcs, lands = started
    n = len(srcs)

    def body(*refs):
        src_refs, zones = refs[:n], refs[n:2 * n]
        send, recv = refs[2 * n:3 * n], refs[3 * n:4 * n]
        for cp in _split_exchange_copies(src_refs, zones, scatter, send, recv):
            cp.wait_send()
            cp.wait_recv()

    hbm = pl.BlockSpec(memory_space=pltpu.HBM)
    sem = pl.BlockSpec(memory_space=pltpu.SEMAPHORE)
    results = pl.pallas_call(
        body, name=name,
        out_shape=tuple(pltpu.HBM(a.shape, a.dtype) for a in list(srcs) + list(lands)),
        in_specs=[hbm] * (2 * n) + [sem] * (2 * n) + [pl.BlockSpec(memory_space=pl.ANY)],
        out_specs=tuple([hbm] * (2 * n)),
        input_output_aliases={i: i for i in range(2 * n)},
        compiler_params=pltpu.CompilerParams(has_side_effects=pltpu.SideEffectType.DATAFLOW_SIDE_EFFECTING),
    )(*srcs, *lands, *send_sems, *recv_sems, after)
    return results[n:]


def _norm_matmul(x, g, w_t, tm, name, comm):
    t, d = x.shape
    n = w_t.shape[0]

    def core(x_ref, g_ref, wt_ref, h_ref, o_ref, w_buf):
        @pl.when(pl.program_id(0) == 0)
        def _():
            w_buf[...] = wt_ref[...].T

        h = _rms_fwd(x_ref[...], g_ref[...]).astype(BF16)
        h_ref[...] = h
        o_ref[...] = _dot(h, w_buf[...]).astype(BF16)

    return _hosted_call(
        core, comm, name, t // tm,
        in_specs=[_tile(tm, d), _whole((1, d)), _resident((n, d))],
        out_specs=[_tile(tm, d), _tile(tm, n)],
        out_shape=[_sds((t, d), BF16), _sds((t, n), BF16)],
        scratch_shapes=[pltpu.VMEM((d, n), BF16)], operands=(x, g, w_t))


def _conv_fwd(u, wa, wb, bb, lg, lb, xres, w_out, g_post, tc, name, comm):
    t = u.shape[0]
    d = w_out.shape[1]
    chunk = CONV_FWD_CHUNK
    n_chunks = tc // chunk
    piece = 2 * LANES

    def core(u_ref, wa_ref, wb_ref, bb_ref, lg_ref, lb_ref, x_ref, wo_ref, go_ref,
             y_ref, zc_ref, ca_ref, x1_ref, s_ref, zbuf, cvbuf, zcbuf):
        @pl.when(pl.program_id(0) == 0)
        def _():
            zbuf[:, 0:HALO_B, :] = jnp.zeros((D_B // LANES, HALO_B, LANES), F32)
            cvbuf[:, 0:HALO_A, :] = jnp.zeros((D_A // LANES, HALO_A, LANES), F32)

        s = None
        for lb_i in range(D_A // LANES):
            if lb_i > 0 and lb_i % 2 == 0:
                cols = slice((lb_i - 2) * LANES, lb_i * LANES)
                part = _dot(y_ref[:, cols], wo_ref[cols, :])
                s = part if s is None else s + part
            lanes = slice(lb_i * LANES, (lb_i + 1) * LANES)
            c_a = u_ref[:, D_A + lb_i * LANES:D_A + (lb_i + 1) * LANES].astype(F32)
            v_a = u_ref[:, 2 * D_A + lb_i * LANES:2 * D_A + (lb_i + 1) * LANES].astype(F32)
            cvbuf[lb_i, HALO_A:HALO_A + tc, :] = c_a * v_a
            glu_v = u_ref[:, 3 * D_A + lb_i * LANES:3 * D_A + (lb_i + 1) * LANES].astype(F32)
            glu_g = u_ref[:, 3 * D_A + D_B + lb_i * LANES:3 * D_A + D_B + (lb_i + 1) * LANES].astype(F32)
            zbuf[lb_i, HALO_B:HALO_B + tc, :] = glu_v * _sigmoid(glu_g)

            for c in range(n_chunks):
                r0 = c * chunk
                rows = slice(r0, r0 + chunk)
                acc = jnp.zeros((chunk, LANES), F32)
                for k in range(CONV_A_W):
                    off = r0 + HALO_A - (CONV_A_W - 1) + k
                    acc = acc + wa_ref[k:k + 1, lanes] * cvbuf[lb_i, off:off + chunk, :]
                ca_ref[rows, lanes] = acc.astype(BF16)
                y_ref[rows, lanes] = (u_ref[rows, lanes].astype(F32) * acc).astype(BF16)

                accb = jnp.zeros((chunk, LANES), F32)
                for k in range(CONV_B_W):
                    off = r0 + HALO_B - (CONV_B_W - 1) + k
                    accb = accb + wb_ref[k:k + 1, lanes] * zbuf[lb_i, off:off + chunk, :]
                zcbuf[rows, lanes] = accb + bb_ref[:, lanes]

            zbuf[lb_i, 0:HALO_B, :] = zbuf[lb_i, tc:tc + HALO_B, :]
            cvbuf[lb_i, 0:HALO_A, :] = cvbuf[lb_i, tc:tc + HALO_A, :]

        cols = slice(D_A - piece, D_A)
        s = s + _dot(y_ref[:, cols], wo_ref[cols, :])

        for c in range(n_chunks):
            rows = slice(c * chunk, (c + 1) * chunk)
            zc = zcbuf[rows, :]
            zc_ref[rows, :] = zc.astype(BF16)
            mu = jnp.mean(zc, axis=-1, keepdims=True)
            xc = zc - mu
            var = jnp.mean(xc * xc, axis=-1, keepdims=True)
            ln = xc * lax.rsqrt(var + LN_EPS) * lg_ref[...] + lb_ref[...]
            y_ref[rows, D_A:D_A + D_B] = (ln * _sigmoid(ln)).astype(BF16)

        s = s + _dot(y_ref[:, D_A:D_A + D_B], wo_ref[D_A:D_A + D_B, :])
        s_ref[...] = s.astype(BF16)
        x1_ref[...] = x_ref[...] + _rms_fwd(s, go_ref[...])

    return _hosted_call(
        core, comm, name, t // tc,
        in_specs=[_tile(tc, D_IN_ALL), _whole((CONV_A_W, D_A)), _whole((CONV_B_W, D_B)), _whole((1, D_B)),
                  _whole((1, D_B)), _whole((1, D_B)), _tile(tc, d), _resident((D_A + D_B, d)), _whole((1, d))],
        out_specs=[_tile(tc, D_A + D_B), _tile(tc, D_B), _tile(tc, D_A), _tile(tc, d), _tile(tc, d)],
        out_shape=[_sds((t, D_A + D_B), BF16), _sds((t, D_B), BF16), _sds((t, D_A), BF16), _sds((t, d), F32),
                   _sds((t, d), BF16)],
        scratch_shapes=[pltpu.VMEM((D_B // LANES, HALO_B + tc, LANES), F32),
                        pltpu.VMEM((D_A // LANES, HALO_A + tc, LANES), F32), pltpu.VMEM((tc, D_B), F32)],
        operands=(u, wa, wb, bb, lg, lb, xres, w_out, g_post))


def _mem_fwd(mem, g, wk, wv, name):
    m, d = mem.shape

    def body(mem_ref, g_ref, wk_ref, wv_ref, n_ref, k_ref, v_ref):
        n = _rms_fwd(mem_ref[...], g_ref[...]).astype(BF16)
        n_ref[...] = n
        k_ref[...] = _dot(n, wk_ref[...]).astype(BF16)
        v_ref[...] = _dot(n, wv_ref[...]).astype(BF16)

    return pl.pallas_call(
        body, name=name, grid=(1,),
        in_specs=[_whole((m, d)), _whole((1, d)), _whole((d, d)), _whole((d, d))],
        out_specs=[_whole((m, d))] * 3,
        out_shape=[_sds((m, d), BF16)] * 3,
        compiler_params=_params(),
    )(mem, g, wk, wv)


def _softmax_rows(s):
    e = jnp.exp(s - jnp.max(s, axis=-1, keepdims=True))
    return e / jnp.sum(e, axis=-1, keepdims=True)


def _attn_fwd(x1, g_pre, wq, k, v, wo, g_post, tm, name):
    t, d = x1.shape
    m = k.shape[0]
    scale = XA_HEAD_DIM ** -0.5

    def body(x_ref, gp_ref, wq_ref, k_ref, v_ref, wo_ref, go_ref, x2_ref, h_ref, q_ref, o_ref, a_ref):
        x = x_ref[...]
        h = _rms_fwd(x, gp_ref[...]).astype(BF16)
        h_ref[...] = h
        q_ref[...] = _dot(h, wq_ref[...]).astype(BF16)
        for hd in range(XA_HEADS):
            cols = slice(hd * XA_HEAD_DIM, (hd + 1) * XA_HEAD_DIM)
            p = _softmax_rows(_dot_nt(q_ref[:, cols], k_ref[:, cols]) * scale)
            o_ref[:, cols] = _dot(p.astype(BF16), v_ref[:, cols]).astype(BF16)
        a = _dot(o_ref[...], wo_ref[...])
        a_ref[...] = a.astype(BF16)
        x2_ref[...] = x + _rms_fwd(a, go_ref[...])

    return pl.pallas_call(
        body, name=name, grid=(t // tm,),
        in_specs=[_tile(tm, d), _whole((1, d)), _resident((d, d)), _whole((m, d)), _whole((m, d)), _resident((d, d)),
                  _whole((1, d))],
        out_specs=[_tile(tm, d)] * 5,
        out_shape=[_sds((t, d), F32)] + [_sds((t, d), BF16)] * 4,
        compiler_params=_params(),
    )(x1, g_pre, wq, k, v, wo, g_post)


def _ffn_fwd(x2, g_pre, wg_t, wu_t, wd, g_post, target, tm, name):
    t, d = x2.shape
    f = wg_t.shape[0]

    def body(x_ref, gp_ref, wg_ref, wu_ref, wd_ref, go_ref, tgt_ref, h_ref, gt_ref, up_ref, f_ref, sq_ref):
        _zero_at_first_step(sq_ref)
        x = x_ref[...]
        h = _rms_fwd(x, gp_ref[...]).astype(BF16)
        h_ref[...] = h
        gt = _dot_nt(h, wg_ref[...])
        up = _dot_nt(h, wu_ref[...])
        gt_ref[...] = gt.astype(BF16)
        up_ref[...] = up.astype(BF16)
        hd = (gt * _sigmoid(gt) * up).astype(BF16)
        ff = _dot(hd, wd_ref[...])
        f_ref[...] = ff.astype(BF16)
        err = x + _rms_fwd(ff, go_ref[...]) - tgt_ref[...]
        sq_ref[...] += jnp.sum(err * err, axis=0, keepdims=True)

    return pl.pallas_call(
        body, name=name, grid=(t // tm,),
        in_specs=[_tile(tm, d), _whole((1, d)), _resident((f, d)), _resident((f, d)), _resident((f, d)), _whole((1, d)),
                  _tile(tm, d)],
        out_specs=[_tile(tm, d), _tile(tm, f), _tile(tm, f), _tile(tm, d), _whole((1, d))],
        out_shape=[_sds((t, d), BF16), _sds((t, f), BF16), _sds((t, f), BF16), _sds((t, d), BF16), _sds((1, d), F32)],
        compiler_params=_params(),
    )(x2, g_pre, wg_t, wu_t, wd, g_post, target)


def _ffn_bwd(x2, f, target, gt, up, g_pre, wg_t, wu_t, wd, g_post, tm, name):
    t, d = x2.shape
    ff = wg_t.shape[0]

    def body(x_ref, f_ref, tgt_ref, gt_ref, up_ref, gp_ref, wg_ref, wu_ref, wd_ref, go_ref,
             dx_ref, df_ref, hd_ref, dgt_ref, dup_ref, dgo_ref, dgp_ref):
        _zero_at_first_step(dgo_ref, dgp_ref)
        x = x_ref[...]
        fo = f_ref[...].astype(F32)
        dx3 = (x + _rms_fwd(fo, go_ref[...]) - tgt_ref[...]) * (1.0 / d)
        df, dgo = _rms_bwd(dx3, fo, go_ref[...])
        dgo_ref[...] += dgo
        df = df.astype(BF16)
        df_ref[...] = df
        dhd = _dot_nt(df, wd_ref[...])
        gt = gt_ref[...].astype(F32)
        up = up_ref[...].astype(F32)
        sg = _sigmoid(gt)
        si = gt * sg
        hd_ref[...] = (si * up).astype(BF16)
        dup = (dhd * si).astype(BF16)
        dgt = (dhd * up * (sg * (1.0 + gt * (1.0 - sg)))).astype(BF16)
        dup_ref[...] = dup
        dgt_ref[...] = dgt
        dh = _dot(dgt, wg_ref[...]) + _dot(dup, wu_ref[...])
        dxn, dgp = _rms_bwd(dh, x, gp_ref[...])
        dgp_ref[...] += dgp
        dx_ref[...] = dx3 + dxn

    return pl.pallas_call(
        body, name=name, grid=(t // tm,),
        in_specs=[_tile(tm, d), _tile(tm, d), _tile(tm, d), _tile(tm, ff), _tile(tm, ff), _whole((1, d)),
                  _resident((ff, d)), _resident((ff, d)), _resident((ff, d)), _whole((1, d))],
        out_specs=[_tile(tm, d), _tile(tm, d), _tile(tm, ff), _tile(tm, ff), _tile(tm, ff), _whole((1, d)), _whole((1, d))],
        out_shape=[_sds((t, d), F32), _sds((t, d), BF16), _sds((t, ff), BF16), _sds((t, ff), BF16), _sds((t, ff), BF16),
                   _sds((1, d), F32), _sds((1, d), F32)],
        compiler_params=_params(),
    )(x2, f, target, gt, up, g_pre, wg_t, wu_t, wd, g_post)


def _attn_bwd(dx2, a, x1, q, k, v, g_pre, wq, wo, g_post, s_mix, w_mix_out, g_mix_post, tm, name, comm):
    t, d = x1.shape
    m = k.shape[0]
    kdim = w_mix_out.shape[0]
    scale = XA_HEAD_DIM ** -0.5

    def core(dx2_ref, a_ref, x_ref, q_ref, k_ref, v_ref, gp_ref, wq_ref, wo_ref, go_ref, s_ref, wm_ref, gm_ref,
             dx1_ref, da_ref, dq_ref, dk_ref, dv_ref, dgo_ref, dgp_ref, ds_ref, dy_ref, dgm_ref, do_buf):
        _zero_at_first_step(dgo_ref, dgp_ref, dk_ref, dv_ref, dgm_ref)
        dx2 = dx2_ref[...]
        da, dgo = _rms_bwd(dx2, a_ref[...].astype(F32), go_ref[...])
        dgo_ref[...] += dgo
        da = da.astype(BF16)
        da_ref[...] = da
        do_buf[...] = _dot_nt(da, wo_ref[...]).astype(BF16)
        for hd in range(XA_HEADS):
            cols = slice(hd * XA_HEAD_DIM, (hd + 1) * XA_HEAD_DIM)
            qh = q_ref[:, cols]
            p = _softmax_rows(_dot_nt(qh, k_ref[:, cols]) * scale)
            do_h = do_buf[:, cols]
            dp = _dot_nt(do_h, v_ref[:, cols])
            dv_ref[:, cols] += _dot_tn(p.astype(BF16), do_h)
            ds = (p * (dp - jnp.sum(dp * p, axis=-1, keepdims=True)) * scale).astype(BF16)
            dq_ref[:, cols] = _dot(ds, k_ref[:, cols]).astype(BF16)
            dk_ref[:, cols] += _dot_tn(ds, qh)
        dh = _dot_nt(dq_ref[...], wq_ref[...])
        dxn, dgp = _rms_bwd(dh, x_ref[...], gp_ref[...])
        dgp_ref[...] += dgp
        dx1 = dx2 + dxn
        dx1_ref[...] = dx1
        ds, dgm = _rms_bwd(dx1, s_ref[...].astype(F32), gm_ref[...])
        dgm_ref[...] += dgm
        ds = ds.astype(BF16)
        ds_ref[...] = ds
        dy_ref[...] = _dot_nt(ds, wm_ref[...]).astype(BF16)

    return _hosted_call(
        core, comm, name, t // tm,
        in_specs=[_tile(tm, d), _tile(tm, d), _tile(tm, d), _tile(tm, d), _whole((m, d)), _whole((m, d)), _whole((1, d)),
                  _resident((d, d)), _resident((d, d)), _whole((1, d)), _tile(tm, d), _resident((kdim, d)), _whole((1, d))],
        out_specs=[_tile(tm, d), _tile(tm, d), _tile(tm, d), _whole((m, d)), _whole((m, d)), _whole((1, d)), _whole((1, d)),
                   _tile(tm, d), _tile(tm, kdim), _whole((1, d))],
        out_shape=[_sds((t, d), F32), _sds((t, d), BF16), _sds((t, d), BF16), _sds((m, d), F32), _sds((m, d), F32),
                   _sds((1, d), F32), _sds((1, d), F32), _sds((t, d), BF16), _sds((t, kdim), BF16), _sds((1, d), F32)],
        scratch_shapes=[pltpu.VMEM((tm, d), BF16)],
        operands=(dx2, a, x1, q, k, v, g_pre, wq, wo, g_post, s_mix, w_mix_out, g_mix_post))


def _mem_bwd(mem, mem_n, dk, dv, g, wk, wv, name):
    m, d = mem.shape

    def body(mem_ref, n_ref, dk_ref, dv_ref, g_ref, wk_ref, wv_ref, dwk_ref, dwv_ref, dg_ref):
        dk = dk_ref[...].astype(BF16)
        dv = dv_ref[...].astype(BF16)
        n = n_ref[...]
        dwk_ref[...] = _dot_tn(n, dk).astype(BF16)
        dwv_ref[...] = _dot_tn(n, dv).astype(BF16)
        dn = _dot_nt(dk, wk_ref[...]) + _dot_nt(dv, wv_ref[...])
        _, dg = _rms_bwd(dn, mem_ref[...], g_ref[...])
        dg_ref[...] = dg

    return pl.pallas_call(
        body, name=name, grid=(1,),
        in_specs=[_whole((m, d)), _whole((m, d)), _whole((m, d)), _whole((m, d)), _whole((1, d)), _whole((d, d)),
                  _whole((d, d))],
        out_specs=[_whole((d, d)), _whole((d, d)), _whole((1, d))],
        out_shape=[_sds((d, d), BF16), _sds((d, d), BF16), _sds((1, d), F32)],
        compiler_params=_params(),
    )(mem, mem_n, dk, dv, g, wk, wv)


def _conv_bwd(dy, u, zc, ca, wa, wb, lg, lb, tc, name, comm):
    t = u.shape[0]
    n_tiles = t // tc
    n_chunks = tc // CONV_CHUNK

    def core(dy_ref, u_ref, zc_ref, ca_ref, wa_ref, wb_ref, lg_ref, lb_ref,
             du_ref, dwa_ref, dwb_ref, dbb_ref, dlg_ref, dlb_ref, ebuf, eabuf, zbuf, cvbuf, gatebuf, wacc, aacc, vacc):
        step = pl.program_id(0)

        @pl.when(step == 0)
        def _():
            ebuf[:, tc:tc + HALO_B, :] = jnp.zeros((D_B // LANES, HALO_B, LANES), F32)
            eabuf[:, tc:tc + HALO_A, :] = jnp.zeros((D_A // LANES, HALO_A, LANES), F32)
            wacc[...] = jnp.zeros_like(wacc)
            aacc[...] = jnp.zeros_like(aacc)
            vacc[...] = jnp.zeros_like(vacc)

        dbb = jnp.zeros((SUBLANES, D_B), F32)
        dlg = jnp.zeros((SUBLANES, D_B), F32)
        dlb = jnp.zeros((SUBLANES, D_B), F32)
        for c in range(n_chunks):
            rows = slice(c * CONV_CHUNK, (c + 1) * CONV_CHUNK)
            dy_a = dy_ref[rows, 0:D_A].astype(F32)
            b_a = u_ref[rows, 0:D_A].astype(F32)
            du_ref[rows, 0:D_A] = (dy_a * ca_ref[rows, :].astype(F32)).astype(BF16)
            dca = dy_a * b_a
            cv = u_ref[rows, D_A:2 * D_A].astype(F32) * u_ref[rows, 2 * D_A:3 * D_A].astype(F32)
            gate = _sigmoid(u_ref[rows, 3 * D_A + D_B:3 * D_A + 2 * D_B].astype(F32))
            z = u_ref[rows, 3 * D_A:3 * D_A + D_B].astype(F32) * gate
            for lb_i in range(D_A // LANES):
                lanes = slice(lb_i * LANES, (lb_i + 1) * LANES)
                eabuf[lb_i, rows, :] = dca[:, lanes]
                cvbuf[lb_i, rows, :] = cv[:, lanes]
                zbuf[lb_i, rows, :] = z[:, lanes]
                gatebuf[lb_i, rows, :] = gate[:, lanes]

            zcv = zc_ref[rows, :].astype(F32)
            mu = jnp.mean(zcv, axis=-1, keepdims=True)
            xc = zcv - mu
            rstd = lax.rsqrt(jnp.mean(xc * xc, axis=-1, keepdims=True) + LN_EPS)
            xhat = xc * rstd
            ln = xhat * lg_ref[...] + lb_ref[...]
            sg = _sigmoid(ln)
            dln = dy_ref[rows, D_A:D_A + D_B].astype(F32) * (sg * (1.0 + ln * (1.0 - sg)))
            dlg = dlg + jnp.sum((dln * xhat).reshape(CONV_CHUNK // SUBLANES, SUBLANES, D_B), axis=0)
            dlb = dlb + jnp.sum(dln.reshape(CONV_CHUNK // SUBLANES, SUBLANES, D_B), axis=0)
            dxh = dln * lg_ref[...]
            dzc = rstd * (dxh - jnp.mean(dxh, axis=-1, keepdims=True) - xhat * jnp.mean(dxh * xhat, axis=-1, keepdims=True))
            dbb = dbb + jnp.sum(dzc.reshape(CONV_CHUNK // SUBLANES, SUBLANES, D_B), axis=0)
            for lb_i in range(D_B // LANES):
                ebuf[lb_i, rows, :] = dzc[:, lb_i * LANES:(lb_i + 1) * LANES]
        vacc[0] += dbb
        vacc[1] += dlg
        vacc[2] += dlb

        for lb_i in range(D_A // LANES):
            lanes = slice(lb_i * LANES, (lb_i + 1) * LANES)

            def cols(first):
                return slice(first + lb_i * LANES, first + (lb_i + 1) * LANES)

            for c in range(n_chunks):
                r0 = c * CONV_CHUNK
                rows = slice(r0, r0 + CONV_CHUNK)
                cv = cvbuf[lb_i, rows, :]
                dcv = jnp.zeros((CONV_CHUNK, LANES), F32)
                for k in range(CONV_A_W):
                    off = r0 + (CONV_A_W - 1) - k
                    e = eabuf[lb_i, off:off + CONV_CHUNK, :]
                    dcv = dcv + wa_ref[k:k + 1, lanes] * e
                    aacc[k, :, lanes] += jnp.sum((cv * e).reshape(CONV_CHUNK // SUBLANES, SUBLANES, LANES), axis=0)
                du_ref[rows, cols(D_A)] = (dcv * u_ref[rows, cols(2 * D_A)].astype(F32)).astype(BF16)
                du_ref[rows, cols(2 * D_A)] = (dcv * u_ref[rows, cols(D_A)].astype(F32)).astype(BF16)

                z = zbuf[lb_i, rows, :]
                dz = jnp.zeros((CONV_CHUNK, LANES), F32)
                for k in range(CONV_B_W):
                    off = r0 + (CONV_B_W - 1) - k
                    e = ebuf[lb_i, off:off + CONV_CHUNK, :]
                    dz = dz + wb_ref[k:k + 1, lanes] * e
                    wacc[k, :, lanes] += jnp.sum((z * e).reshape(CONV_CHUNK // SUBLANES, SUBLANES, LANES), axis=0)
                glu_v = u_ref[rows, cols(3 * D_A)].astype(F32)
                sgg = gatebuf[lb_i, rows, :]
                du_ref[rows, cols(3 * D_A)] = (dz * sgg).astype(BF16)
                du_ref[rows, cols(3 * D_A + D_B)] = (dz * glu_v * sgg * (1.0 - sgg)).astype(BF16)

            ebuf[lb_i, tc:tc + HALO_B, :] = ebuf[lb_i, 0:HALO_B, :]
            eabuf[lb_i, tc:tc + HALO_A, :] = eabuf[lb_i, 0:HALO_A, :]

        @pl.when(step == n_tiles - 1)
        def _():
            for k in range(CONV_B_W):
                dwb_ref[k:k + 1, :] = jnp.sum(wacc[k], axis=0, keepdims=True)
            for k in range(CONV_A_W):
                dwa_ref[k:k + 1, :] = jnp.sum(aacc[k], axis=0, keepdims=True)
            dbb_ref[...] = jnp.sum(vacc[0], axis=0, keepdims=True)
            dlg_ref[...] = jnp.sum(vacc[1], axis=0, keepdims=True)
            dlb_ref[...] = jnp.sum(vacc[2], axis=0, keepdims=True)

    return _hosted_call(
        core, comm, name, n_tiles,
        in_specs=[_rtile(tc, D_A + D_B, n_tiles), _rtile(tc, D_IN_ALL, n_tiles), _rtile(tc, D_B, n_tiles),
                  _rtile(tc, D_A, n_tiles), _whole((CONV_A_W, D_A)), _whole((CONV_B_W, D_B)), _whole((1, D_B)),
                  _whole((1, D_B))],
        out_specs=[_rtile(tc, D_IN_ALL, n_tiles), _whole((CONV_A_W, D_A)), _whole((CONV_B_W, D_B)), _whole((1, D_B)),
                   _whole((1, D_B)), _whole((1, D_B))],
        out_shape=[_sds((t, D_IN_ALL), BF16), _sds((CONV_A_W, D_A), F32), _sds((CONV_B_W, D_B), F32), _sds((1, D_B), F32),
                   _sds((1, D_B), F32), _sds((1, D_B), F32)],
        scratch_shapes=[pltpu.VMEM((D_B // LANES, tc + HALO_B, LANES), F32), pltpu.VMEM((D_A // LANES, tc + HALO_A, LANES), F32),
                        pltpu.VMEM((D_B // LANES, tc, LANES), F32), pltpu.VMEM((D_A // LANES, tc, LANES), F32),
                        pltpu.VMEM((D_B // LANES, tc, LANES), F32),
                        pltpu.VMEM((CONV_B_W, SUBLANES, D_B), F32), pltpu.VMEM((CONV_A_W, SUBLANES, D_A), F32),
                        pltpu.VMEM((3, SUBLANES, D_B), F32)],
        operands=(dy, u, zc, ca, wa, wb, lg, lb))


def _in_bwd(du, w_t, x, dx1, g, tm, name, after):
    t, d = x.shape
    n = w_t.shape[0]

    def body(after_ref, du_ref, w_ref, x_ref, dx1_ref, g_ref, dx_ref, dg_ref):
        del after_ref
        _zero_at_first_step(dg_ref)
        dh = _dot(du_ref[...], w_ref[...])
        dxn, dg = _rms_bwd(dh, x_ref[...], g_ref[...])
        dg_ref[...] += dg
        dx_ref[...] = dx1_ref[...] + dxn

    return pl.pallas_call(
        body, name=name, grid=(t // tm,),
        in_specs=[pl.BlockSpec(memory_space=pl.ANY), _tile(tm, n), _resident((n, d)), _tile(tm, d), _tile(tm, d),
                  _whole((1, d))],
        out_specs=[_tile(tm, d), _whole((1, d))],
        out_shape=[_sds((t, d), F32), _sds((1, d), F32)],
        compiler_params=_params(),
    )(after, du, w_t, x, dx1, g)


def _wgrad(a, b, tk, bm, bn, name, comm=None, cols=None):
    t, m = a.shape
    first, n = (0, b.shape[1]) if cols is None else cols
    first_block = first // bn
    n_k = t // tk

    def body(a_ref, b_ref, o_ref, acc):
        @pl.when(pl.program_id(2) == 0)
        def _():
            acc[...] = jnp.zeros_like(acc)

        acc[...] += _dot_tn(a_ref[...], b_ref[...])

        @pl.when(pl.program_id(2) == n_k - 1)
        def _():
            o_ref[...] = acc[...].astype(BF16)

    call = dict(
        grid=(m // bm, n // bn, n_k),
        in_specs=[pl.BlockSpec((tk, bm), lambda i, j, k: (k, i)),
                  pl.BlockSpec((tk, bn), lambda i, j, k: (k, first_block + j))],
        out_specs=[pl.BlockSpec((bm, bn), lambda i, j, k: (i, j))],
        out_shape=[_sds((m, n), BF16)],
        scratch_shapes=[pltpu.VMEM((bm, bn), F32)])
    if comm is None:
        return pl.pallas_call(body, name=name, compiler_params=_params(3), **call)(a, b)[0]
    (out,), received = _hosted_call(body, comm, name, operands=(a, b), **call)
    return out, received


def _adamw(items, tr, name, after=None):
    n_parts, r, c = items[0][0].shape
    n_items = len(items)
    order = [] if after is None else [after]

    def body(*refs):
        refs = refs[len(order):]
        ins, outs = refs[:4 * n_items], refs[4 * n_items:]
        for i in range(n_items):
            p_ref, w_ref, m_ref, v_ref = ins[4 * i:4 * i + 4]
            g_ref, d_ref, nm_ref, nv_ref = outs[4 * i:4 * i + 4]
            g = p_ref[0].astype(F32)
            for j in range(1, n_parts):
                g = g + p_ref[j].astype(F32)
            g_ref[...] = g
            nm = ADAM_B1 * m_ref[...] + (1.0 - ADAM_B1) * g
            nv = ADAM_B2 * v_ref[...] + (1.0 - ADAM_B2) * (g * g)
            nm_ref[...] = nm
            nv_ref[...] = nv
            m_hat = nm / (1.0 - ADAM_B1 ** ADAM_STEP)
            v_hat = nv / (1.0 - ADAM_B2 ** ADAM_STEP)
            d_ref[...] = -ADAM_LR * (m_hat / (jnp.sqrt(v_hat) + ADAM_EPS) + ADAM_WD * w_ref[...])

    results = pl.pallas_call(
        body, name=name, grid=(r // tr,),
        in_specs=[pl.BlockSpec(memory_space=pl.ANY)] * len(order)
        + ([pl.BlockSpec((n_parts, tr, c), lambda i: (0, i, 0))] + [_tile(tr, c)] * 3) * n_items,
        out_specs=[_tile(tr, c)] * (4 * n_items),
        out_shape=[_sds((r, c), F32)] * (4 * n_items),
        compiler_params=_params(),
    )(*order, *[a for item in items for a in item])
    return [results[4 * i:4 * i + 4] for i in range(n_items)]


def _sum_parts(parts, name):
    n_parts, r, c = parts.shape

    def body(p_ref, o_ref):
        acc = p_ref[0]
        for j in range(1, n_parts):
            acc = acc + p_ref[j]
        o_ref[...] = acc

    return pl.pallas_call(
        body, name=name, grid=(1,),
        in_specs=[_whole((n_parts, r, c))], out_specs=_whole((r, c)), out_shape=_sds((r, c), F32),
        compiler_params=_params(),
    )(parts)


def _row(v):
    return v.reshape(1, -1)


def _by_owner_rows(g):
    return g.reshape(N_DEV, g.shape[0] // N_DEV, g.shape[1])


WEIGHTS = ("mix_pre_g", "w_mix_in", "conv_a_w", "conv_b_w", "conv_b_b", "ln_b_g", "ln_b_b", "w_mix_out", "mix_post_g",
           "xa_pre_g", "mem_norm_g", "w_q", "w_k", "w_v", "w_o", "xa_post_g", "ffn_pre_g", "w_gate", "w_up", "w_down",
           "ffn_post_g")
LARGE = ("w_mix_in", "w_mix_out", "w_q", "w_k", "w_v", "w_o", "w_gate", "w_up", "w_down")
COLUMN_SHARDED = ("w_mix_in", "w_gate", "w_up")
GAINS = ("mix_pre_g", "mix_post_g", "xa_pre_g", "mem_norm_g", "xa_post_g", "ffn_pre_g", "ffn_post_g")
CHANNEL_VECTORS = ("conv_b_b", "ln_b_g", "ln_b_b")
CONV_TAPS = ("conv_a_w", "conv_b_w")
SMALL = GAINS + CHANNEL_VECTORS + CONV_TAPS
CONV_COLS_PER_DEVICE = D_A // N_DEV
TOKEN_TILE = 512
FFN_FWD_TOKEN_TILE = 512
FFN_TOKEN_TILE = 256
CONV_TOKEN_TILE = 256
WGRAD_TOKEN_TILE = 2048
SMALL_WGRAD_TOKEN_TILE = 4096
IN_BWD_TOKEN_TILE = 1024
ADAM_ROWS_PER_STEP = 64


def _lane_rows(v):
    flat = v.reshape(-1)
    tile = SUBLANES * LANES
    flat = jnp.pad(flat, (0, (-flat.shape[0]) % tile))
    return flat.reshape(-1, LANES)


def _pack_small(values, names):
    return jnp.concatenate([_lane_rows(values[n]) for n in names], axis=0)


def _unpack_small(packed, like, names):
    out, off = {}, 0
    for n in names:
        size = like[n].size
        rows = _lane_rows(like[n]).shape[0]
        out[n] = packed[off:off + rows, :].reshape(-1)[:size].reshape(like[n].shape)
        off += rows
    return out


def kernel(x, mem, mix_pre_g, w_mix_in, conv_a_w, conv_b_w, conv_b_b, ln_b_g, ln_b_b, w_mix_out, mix_post_g, xa_pre_g, mem_norm_g, w_q, w_k, w_v, w_o, xa_post_g, ffn_pre_g, w_gate, w_up, w_down, ffn_post_g, loss_target, m_mix_pre_g, m_w_mix_in, m_conv_a_w, m_conv_b_w, m_conv_b_b, m_ln_b_g, m_ln_b_b, m_w_mix_out, m_mix_post_g, m_xa_pre_g, m_mem_norm_g, m_w_q, m_w_k, m_w_v, m_w_o, m_xa_post_g, m_ffn_pre_g, m_w_gate, m_w_up, m_w_down, m_ffn_post_g, v_mix_pre_g, v_w_mix_in, v_conv_a_w, v_conv_b_w, v_conv_b_b, v_ln_b_g, v_ln_b_b, v_w_mix_out, v_mix_post_g, v_xa_pre_g, v_mem_norm_g, v_w_q, v_w_k, v_w_v, v_w_o, v_xa_post_g, v_ffn_pre_g, v_w_gate, v_w_up, v_w_down, v_ffn_post_g):
    given = dict(locals())
    w = {n: given[n] for n in WEIGHTS}
    m = {n: given["m_" + n] for n in WEIGHTS}
    v = {n: given["v_" + n] for n in WEIGHTS}
    xs, mems, target = x[0], mem[0], loss_target[0]
    t = xs.shape[0]
    tm, tm_ffn, tc, tk = min(TOKEN_TILE, t), min(FFN_TOKEN_TILE, t), min(CONV_TOKEN_TILE, t), min(WGRAD_TOKEN_TILE, t)
    tk_small = min(SMALL_WGRAD_TOKEN_TILE, t)
    g = {n: _row(w[n]) for n in GAINS}
    bb, lg, lb = (_row(w[n]) for n in CHANNEL_VECTORS)

    def shard_bf16(*names):
        return [w[n].astype(BF16) for n in names]

    def shard_bf16_t(*names):
        return [w[n].T.astype(BF16) for n in names]

    g_mix_in, g_taps = _comm_call(_Gather(shard_bf16_t("w_mix_in") + [_pack_small(w, CONV_TAPS)]), "gather_mixer")
    w_mix_in_t = g_mix_in.reshape(D_IN_ALL, D_MODEL)
    taps, off = {}, 0
    for n in CONV_TAPS:
        k, cols = w[n].shape
        rows = _lane_rows(w[n]).shape[0]
        blk = g_taps[:, off:off + rows, :].reshape(N_DEV, -1)[:, :k * cols].reshape(N_DEV, k, cols)
        taps[n] = blk.transpose(1, 0, 2).reshape(k, N_DEV * cols)
        off += rows
    wa, wb = taps["conv_a_w"], taps["conv_b_w"]

    (h1, u), gathered = _norm_matmul(xs, g["mix_pre_g"], w_mix_in_t, tm, "mix_in_fwd",
                                     _Gather(shard_bf16("w_mix_out", "w_q", "w_k", "w_v", "w_o")))
    w_mix_out, w_q, w_k, w_v, w_o = (a.reshape(D_MODEL, D_MODEL) for a in gathered)
    (ycat, zc, ca, x1, s1), (g_gate, g_up, g_down) = _conv_fwd(
        u, wa, wb, bb, lg, lb, xs, w_mix_out, g["mix_post_g"], tc, "conv_fwd",
        _Gather(shard_bf16_t("w_gate", "w_up") + shard_bf16("w_down")))
    w_gate_t, w_up_t, w_down = (a.reshape(D_FF, D_MODEL) for a in (g_gate, g_up, g_down))
    mem_n, kk, vv = _mem_fwd(mems, g["mem_norm_g"], w_k, w_v, "mem_fwd")
    x2, h2, q, o, a = _attn_fwd(x1, g["xa_pre_g"], w_q, kk, vv, w_o, g["xa_post_g"], tm, "attn_fwd")
    h3, gt, up, f, sq = _ffn_fwd(x2, g["ffn_pre_g"], w_gate_t, w_up_t, w_down, g["ffn_post_g"], target,
                                 min(FFN_FWD_TOKEN_TILE, t), "ffn_fwd")

    dx2, df, hd, dgt, dup, d_ffn_post, d_ffn_pre = _ffn_bwd(
        x2, f, target, gt, up, g["ffn_pre_g"], w_gate_t, w_up_t, w_down, g["ffn_post_g"], tm_ffn, "ffn_bwd")
    d_w_down = _wgrad(hd, df, tk, D_FF // 2, D_MODEL, "wgrad_down")
    d_w_gate_t = _wgrad(dgt, h3, tk, D_FF // 2, D_MODEL, "wgrad_gate")
    d_w_up_t = _wgrad(dup, h3, tk, D_FF // 2, D_MODEL, "wgrad_up")
    ffn_slabs = [_by_owner_rows(d) for d in (d_w_gate_t, d_w_up_t, d_w_down)]

    (dx1, da, dq, dk, dv, d_xa_post, d_xa_pre, ds1, dycat, d_mix_post), from_ffn = _attn_bwd(
        dx2, a, x1, q, kk, vv, g["xa_pre_g"], w_q, w_o, g["xa_post_g"], s1, w_mix_out, g["mix_post_g"], tm, "attn_bwd",
        _Exchange(ffn_slabs, [True] * 3))
    d_w_o = _wgrad(o, da, tk, D_MODEL, D_MODEL, "wgrad_o")
    d_w_q = _wgrad(h2, dq, tk, D_MODEL, D_MODEL, "wgrad_q")
    d_w_k, d_w_v, d_mem_norm = _mem_bwd(mems, mem_n, dk, dv, g["mem_norm_g"], w_k, w_v, "mem_bwd")
    d_w_mix_out = _wgrad(ycat, ds1, tk, D_MODEL, D_MODEL, "wgrad_mix_out")
    attn_slabs = [_by_owner_rows(d) for d in (d_w_mix_out, d_w_q, d_w_k, d_w_v, d_w_o)]

    (du, d_conv_a, d_conv_b, d_conv_bb, d_ln_g, d_ln_b), from_attn = _conv_bwd(
        dycat, u, zc, ca, wa, wb, lg, lb, tc, "conv_bwd", _Exchange(attn_slabs, [True] * 5))
    d_w_in_t = _wgrad(du, h1, tk, D_IN_ALL // 2, D_MODEL, "wgrad_mix_in")
    in_scatter = [True]
    in_copy, in_started = _exchange_start([_by_owner_rows(d_w_in_t)], in_scatter, "mix_in_start")
    dx, d_mix_pre = _in_bwd(du, w_mix_in_t, xs, dx1, g["mix_pre_g"], min(IN_BWD_TOKEN_TILE, t), "mix_in_bwd",
                            in_started)
    (from_in,) = _exchange_wait(in_copy, in_scatter, d_mix_pre, "mix_in_wait")
    small_grads = dict(mix_pre_g=d_mix_pre, conv_a_w=d_conv_a, conv_b_w=d_conv_b, conv_b_b=d_conv_bb, ln_b_g=d_ln_g,
                       ln_b_b=d_ln_b, mix_post_g=d_mix_post, xa_pre_g=d_xa_pre, mem_norm_g=d_mem_norm, xa_post_g=d_xa_post,
                       ffn_pre_g=d_ffn_pre, ffn_post_g=d_ffn_post, loss=sq)
    names = SMALL + ("loss",)
    tail_scatter = [False]
    tail, started = _exchange_start([_pack_small(small_grads, names)], tail_scatter, "tail_start")

    received = dict(zip(("w_gate", "w_up", "w_down"), from_ffn))
    received.update(zip(("w_mix_out", "w_q", "w_k", "w_v", "w_o"), from_attn))
    grad, delta, new_m, new_v = {}, {}, {}, {}

    def adamw_group(group, after=None):
        items = [[received[n]] + [a[n].T if n in COLUMN_SHARDED else a[n] for a in (w, m, v)] for n in group]
        rows = items[0][1].shape[0]
        results = _adamw(items, min(rows, ADAM_ROWS_PER_STEP) if len(group) > 1 else rows, "adamw_" + group[0], after)
        for n, result in zip(group, results):
            grad[n], delta[n], new_m[n], new_v[n] = [r.T if n in COLUMN_SHARDED else r for r in result]
        return results[-1][-1]

    behind = started
    for group in (("w_gate",), ("w_up",), ("w_down",), ("w_mix_out", "w_q", "w_k", "w_v", "w_o")):
        behind = adamw_group(group, behind)
    received["w_mix_in"] = from_in
    behind = adamw_group(("w_mix_in",), behind)
    (all_small,) = _exchange_wait(tail, tail_scatter, behind, "tail_wait")

    total = _unpack_small(_sum_parts(all_small, "sum_small"), small_grads, names)
    loss = jnp.sum(total.pop("loss")) * (0.5 / D_MODEL)
    first_col = _device_index() * CONV_COLS_PER_DEVICE
    for n in CONV_TAPS:
        total[n] = lax.dynamic_slice_in_dim(total[n], first_col, CONV_COLS_PER_DEVICE, axis=1)
    total = {n: total[n].reshape(w[n].shape) for n in SMALL}
    packed_small = [_pack_small(values, SMALL) for values in (total, w, m, v)]
    ((g_s, d_s, nm_s, nv_s),) = _adamw([[packed_small[0][None]] + packed_small[1:]], packed_small[0].shape[0],
                                       "adamw_small")
    for out, p in ((grad, g_s), (delta, d_s), (new_m, nm_s), (new_v, nv_s)):
        out.update(_unpack_small(p, w, SMALL))

    return (loss, dx[None], *[grad[n] for n in WEIGHTS], *[delta[n] for n in WEIGHTS], *[new_m[n] for n in WEIGHTS],
            *[new_v[n] for n in WEIGHTS])
```

```python
import jax
import jax.numpy as jnp
from jax import lax
from jax.experimental import pallas as pl
from jax.experimental.pallas import tpu as pltpu

F32 = jnp.float32
BF16 = jnp.bfloat16

D_MODEL = 1024
D_A = 512
D_B = 512
D_IN_ALL = 2560
CONV_A_W = 3
CONV_B_W = 31
XA_HEADS = 4
XA_HEAD_DIM = 256
D_FF = 2816
N_DEV = 8
RMS_EPS = 1e-6
LN_EPS = 1e-5
ADAM_LR = 0.001
ADAM_B1 = 0.9
ADAM_B2 = 0.999
ADAM_EPS = 1e-08
ADAM_WD = 0.01
ADAM_STEP = 10

VMEM_LIMIT_BYTES = 56 * 1024 * 1024
SUBLANES = 8
LANES = 128
HALO_B = 32
HALO_A = 8
CONV_FWD_CHUNK = 16
CONV_CHUNK = 32
GATHER_FORWARD_STEPS_BEFORE_END = 8

MESH = pl.DeviceIdType.MESH


def _params(n_grid_axes=1):
    return pltpu.CompilerParams(dimension_semantics=("arbitrary",) * n_grid_axes, vmem_limit_bytes=VMEM_LIMIT_BYTES)


def _sds(shape, dtype):
    return jax.ShapeDtypeStruct(shape, dtype)


def _tile(rows, cols):
    return pl.BlockSpec((rows, cols), lambda i: (i, 0))


def _rtile(rows, cols, n):
    return pl.BlockSpec((rows, cols), lambda i: (n - 1 - i, 0))


def _whole(shape):
    zeros = (0,) * len(shape)
    return pl.BlockSpec(shape, lambda i: zeros)


def _resident(shape):
    zeros = (0,) * len(shape)
    return pl.BlockSpec(shape, lambda i: zeros, pipeline_mode=pl.Buffered(1))


def _dot(a, b):
    return jnp.dot(a, b, preferred_element_type=F32)


def _dot_nt(a, b):
    return lax.dot_general(a, b, (((1,), (1,)), ((), ())), preferred_element_type=F32)


def _dot_tn(a, b):
    return lax.dot_general(a, b, (((0,), (0,)), ((), ())), preferred_element_type=F32)


def _sigmoid(x):
    return 1.0 / (1.0 + jnp.exp(-x))


def _rms_fwd(x, g):
    r = lax.rsqrt(jnp.mean(x * x, axis=-1, keepdims=True) + RMS_EPS)
    return x * r * g


def _rms_bwd(dy, xin, g):
    r = lax.rsqrt(jnp.mean(xin * xin, axis=-1, keepdims=True) + RMS_EPS)
    n = xin * r
    dg = jnp.sum(dy * n, axis=0, keepdims=True)
    dn = dy * g
    dx = r * (dn - n * jnp.mean(dn * n, axis=-1, keepdims=True))
    return dx, dg


def _zero_at_first_step(*refs):
    @pl.when(pl.program_id(0) == 0)
    def _():
        for ref in refs:
            ref[...] = jnp.zeros(ref.shape, ref.dtype)


def _place():
    return lax.axis_index("x"), lax.axis_index("y"), lax.axis_index("c")


def _device_index():
    x, y, c = _place()
    return 4 * x + 2 * y + c


class _Gather:
    def __init__(self, arrays):
        self.arrays = list(arrays)
        self.out_shape = [_sds((N_DEV, *a.shape), a.dtype) for a in self.arrays]
        n = len(self.arrays)
        self.scratch_shapes = [pltpu.SemaphoreType.DMA((n, 7)), pltpu.SemaphoreType.DMA((n, 7)),
                               pltpu.SemaphoreType.DMA((n,))]

    def forward_step(self, n_steps):
        return max(0, n_steps - 1 - GATHER_FORWARD_STEPS_BEFORE_END)

    def bind(self, srcs, dsts, send_sems, recv_sems, local_sems):
        x, y, cc = _place()
        me, sibling = (x, y, cc), (x, y, 1 - cc)
        chips = [(1 - x, y), (x, 1 - y), (1 - x, 1 - y)]

        def copy(a, k, owner, to, src=None):
            slot = dsts[a].at[4 * owner[0] + 2 * owner[1] + owner[2]]
            return pltpu.make_async_remote_copy(
                src_ref=slot if src is None else src, dst_ref=slot, send_sem=send_sems.at[a, k],
                recv_sem=recv_sems.at[a, k], device_id=to, device_id_type=MESH)

        def first(a):
            return [copy(a, 0, me, sibling, src=srcs[a])] + [
                copy(a, 1 + j, me, (*chip, cc), src=srcs[a]) for j, chip in enumerate(chips)]

        def passed(a, j):
            return copy(a, 4 + j, (*chips[j], cc), sibling)

        def mine(a):
            return pltpu.make_async_copy(srcs[a], dsts[a].at[4 * x + 2 * y + cc], local_sems.at[a])

        def start():
            for a in range(len(srcs)):
                mine(a).start()
                for cp in first(a):
                    cp.start()

        def forward():
            for a in range(len(srcs)):
                for j, chip in enumerate(chips):
                    copy(a, 1 + j, (*chip, cc), me).wait_recv()
                    passed(a, j).start()

        def finish():
            for a in range(len(srcs)):
                copy(a, 0, sibling, me).wait_recv()
                for j, chip in enumerate(chips):
                    copy(a, 4 + j, (*chip, 1 - cc), me).wait_recv()
                for cp in first(a) + [passed(a, j) for j in range(len(chips))]:
                    cp.wait_send()
                mine(a).wait()

        return start, forward, finish


class _Exchange:
    def __init__(self, arrays, scatter):
        self.arrays = list(arrays)
        self.scatter = list(scatter)
        self.out_shape = [_sds(a.shape if s else (N_DEV, *a.shape), a.dtype) for a, s in zip(self.arrays, self.scatter)]
        n = len(self.arrays)
        self.scratch_shapes = [pltpu.SemaphoreType.DMA((n, 7)), pltpu.SemaphoreType.DMA((n, 7)),
                               pltpu.SemaphoreType.DMA((n,))]

    def forward_step(self, n_steps):
        return n_steps - 1

    def bind(self, srcs, dsts, send_sems, recv_sems, local_sems):
        me = _device_index()

        def copies(a):
            out = []
            for k in range(1, N_DEV):
                p = me ^ k
                out.append(pltpu.make_async_remote_copy(
                    src_ref=srcs[a].at[p] if self.scatter[a] else srcs[a], dst_ref=dsts[a].at[me],
                    send_sem=send_sems.at[a, k - 1], recv_sem=recv_sems.at[a, k - 1],
                    device_id=(p >> 2, (p >> 1) & 1, p & 1), device_id_type=MESH))
            return out

        def mine(a):
            return pltpu.make_async_copy(srcs[a].at[me] if self.scatter[a] else srcs[a], dsts[a].at[me], local_sems.at[a])

        def start():
            for a in range(len(srcs)):
                mine(a).start()
                for cp in copies(a):
                    cp.start()

        def forward():
            pass

        def finish():
            for a in range(len(srcs)):
                for cp in copies(a):
                    cp.wait()
                mine(a).wait()

        return start, forward, finish


def _hosted_call(core, comm, name, grid, in_specs, out_specs, out_shape, scratch_shapes, operands):
    grid = (grid,) if isinstance(grid, int) else tuple(grid)
    n_steps = 1
    for extent in grid:
        n_steps *= extent
    n_in, n_out, n_scr, n_arr = len(in_specs), len(out_specs), len(scratch_shapes), len(comm.arrays)
    any_spec = pl.BlockSpec(memory_space=pl.ANY)

    def body(*refs):
        ins, refs = refs[:n_in], refs[n_in:]
        srcs, refs = refs[:n_arr], refs[n_arr:]
        outs, refs = refs[:n_out], refs[n_out:]
        dsts, refs = refs[:n_arr], refs[n_arr:]
        scratch, sems = refs[:n_scr], refs[n_scr:]
        start, forward, finish = comm.bind(srcs, dsts, *sems)
        step = pl.program_id(0)
        for axis in range(1, len(grid)):
            step = step * grid[axis] + pl.program_id(axis)
        pl.when(step == 0)(start)
        core(*ins, *outs, *scratch)
        pl.when(step == comm.forward_step(n_steps))(forward)
        pl.when(step == n_steps - 1)(finish)

    results = pl.pallas_call(
        body, name=name, grid=grid,
        in_specs=list(in_specs) + [any_spec] * n_arr,
        out_specs=list(out_specs) + [any_spec] * n_arr,
        out_shape=list(out_shape) + comm.out_shape,
        scratch_shapes=list(scratch_shapes) + comm.scratch_shapes,
        compiler_params=_params(len(grid)),
    )(*operands, *comm.arrays)
    return results[:n_out], results[n_out:]


def _comm_call(comm, name):
    return _hosted_call(lambda: None, comm, name, 1, [], [], [], [], [])[1]


def _split_exchange_copies(srcs, lands, scatter, send_sems, recv_sems):
    me = _device_index()
    out = []
    for a in range(len(srcs)):
        for k in range(1, N_DEV):
            p = me ^ k
            out.append(pltpu.make_async_remote_copy(
                src_ref=srcs[a].at[p] if scatter[a] else srcs[a], dst_ref=lands[a].at[me],
                send_sem=send_sems[a].at[k - 1], recv_sem=recv_sems[a].at[k - 1],
                device_id=(p >> 2, (p >> 1) & 1, p & 1), device_id_type=MESH))
    return out


def _exchange_start(arrays, scatter, name):
    n = len(arrays)
    me = _device_index()
    lands = []
    for a, s in zip(arrays, scatter):
        own = lax.dynamic_slice_in_dim(a, me, 1, axis=0) if s else a[None]
        empty = jnp.zeros(a.shape if s else (N_DEV, *a.shape), a.dtype)
        lands.append(lax.dynamic_update_slice_in_dim(empty, own, me, axis=0))

    def body(*refs):
        srcs, zones = refs[:n], refs[n:2 * n]
        send_sems, recv_sems = refs[2 * n:3 * n], refs[3 * n:4 * n]
        for cp in _split_exchange_copies(srcs, zones, scatter, send_sems, recv_sems):
            cp.start()
        refs[-1][...] = jnp.zeros(refs[-1].shape, F32)

    hbm = pl.BlockSpec(memory_space=pltpu.HBM)
    sem = pl.BlockSpec(memory_space=pltpu.SEMAPHORE)
    operands = [pltpu.with_memory_space_constraint(a, pltpu.HBM) for a in list(arrays) + lands]
    results = pl.pallas_call(
        body, name=name,
        out_shape=(*[pltpu.SemaphoreType.DMA((N_DEV - 1,))] * (2 * n), *[pltpu.HBM(a.shape, a.dtype) for a in operands],
                   _sds((SUBLANES, LANES), F32)),
        in_specs=[hbm] * (2 * n),
        out_specs=(*[sem] * (2 * n), *[hbm] * (2 * n), pl.BlockSpec(memory_space=pltpu.VMEM)),
        input_output_aliases={i: 2 * n + i for i in range(2 * n)},
        compiler_params=pltpu.CompilerParams(has_side_effects=pltpu.SideEffectType.DATAFLOW_SIDE_EFFECTING),
    )(*operands)
    return (results[:n], results[n:2 * n], results[2 * n:3 * n], results[3 * n:4 * n]), results[-1]


def _exchange_wait(started, scatter, after, name):
    send_sems, recv_sems, srcs, lands = started
    n = len(srcs)

    def body(*refs):
        src_refs, zones = refs[:n], refs[n:2 * n]
        send, recv = refs[2 * n:3 * n], refs[3 * n:4 * n]
        for cp in _split_exchange_copies(src_refs, zones, scatter, send, recv):
            cp.wait_send()
            cp.wait_recv()

    hbm = pl.BlockSpec(memory_space=pltpu.HBM)
    sem = pl.BlockSpec(memory_space=pltpu.SEMAPHORE)
    results = pl.pallas_call(
        body, name=name,
        out_shape=tuple(pltpu.HBM(a.shape, a.dtype) for a in list(srcs) + list(lands)),
        in_specs=[hbm] * (2 * n) + [sem] * (2 * n) + [pl.BlockSpec(memory_space=pl.ANY)],
        out_specs=tuple([hbm] * (2 * n)),
        input_output_aliases={i: i for i in range(2 * n)},
        compiler_params=pltpu.CompilerParams(has_side_effects=pltpu.SideEffectType.DATAFLOW_SIDE_EFFECTING),
    )(*srcs, *lands, *send_sems, *recv_sems, after)
    return results[n:]


def _norm_matmul(x, g, w_t, tm, name, comm):
    t, d = x.shape
    n = w_t.shape[0]

    def core(x_ref, g_ref, wt_ref, h_ref, o_ref, w_buf):
        @pl.when(pl.program_id(0) == 0)
        def _():
            w_buf[...] = wt_ref[...].T

        h = _rms_fwd(x_ref[...], g_ref[...]).astype(BF16)
        h_ref[...] = h
        o_ref[...] = _dot(h, w_buf[...]).astype(BF16)

    return _hosted_call(
        core, comm, name, t // tm,
        in_specs=[_tile(tm, d), _whole((1, d)), _resident((n, d))],
        out_specs=[_tile(tm, d), _tile(tm, n)],
        out_shape=[_sds((t, d), BF16), _sds((t, n), BF16)],
        scratch_shapes=[pltpu.VMEM((d, n), BF16)], operands=(x, g, w_t))


def _conv_fwd(u, wa, wb, bb, lg, lb, xres, w_out, g_post, tc, name, comm):
    t = u.shape[0]
    d = w_out.shape[1]
    chunk = CONV_FWD_CHUNK
    n_chunks = tc // chunk
    piece = 2 * LANES

    def core(u_ref, wa_ref, wb_ref, bb_ref, lg_ref, lb_ref, x_ref, wo_ref, go_ref,
             y_ref, zc_ref, ca_ref, x1_ref, s_ref, zbuf, cvbuf, zcbuf):
        @pl.when(pl.program_id(0) == 0)
        def _():
            zbuf[:, 0:HALO_B, :] = jnp.zeros((D_B // LANES, HALO_B, LANES), F32)
            cvbuf[:, 0:HALO_A, :] = jnp.zeros((D_A // LANES, HALO_A, LANES), F32)

        s = None
        for lb_i in range(D_A // LANES):
            if lb_i > 0 and lb_i % 2 == 0:
                cols = slice((lb_i - 2) * LANES, lb_i * LANES)
                part = _dot(y_ref[:, cols], wo_ref[cols, :])
                s = part if s is None else s + part
            lanes = slice(lb_i * LANES, (lb_i + 1) * LANES)
            c_a = u_ref[:, D_A + lb_i * LANES:D_A + (lb_i + 1) * LANES].astype(F32)
            v_a = u_ref[:, 2 * D_A + lb_i * LANES:2 * D_A + (lb_i + 1) * LANES].astype(F32)
            cvbuf[lb_i, HALO_A:HALO_A + tc, :] = c_a * v_a
            glu_v = u_ref[:, 3 * D_A + lb_i * LANES:3 * D_A + (lb_i + 1) * LANES].astype(F32)
            glu_g = u_ref[:, 3 * D_A + D_B + lb_i * LANES:3 * D_A + D_B + (lb_i + 1) * LANES].astype(F32)
            zbuf[lb_i, HALO_B:HALO_B + tc, :] = glu_v * _sigmoid(glu_g)

            for c in range(n_chunks):
                r0 = c * chunk
                rows = slice(r0, r0 + chunk)
                acc = jnp.zeros((chunk, LANES), F32)
                for k in range(CONV_A_W):
                    off = r0 + HALO_A - (CONV_A_W - 1) + k
                    acc = acc + wa_ref[k:k + 1, lanes] * cvbuf[lb_i, off:off + chunk, :]
                ca_ref[rows, lanes] = acc.astype(BF16)
                y_ref[rows, lanes] = (u_ref[rows, lanes].astype(F32) * acc).astype(BF16)

                accb = jnp.zeros((chunk, LANES), F32)
                for k in range(CONV_B_W):
                    off = r0 + HALO_B - (CONV_B_W - 1) + k
                    accb = accb + wb_ref[k:k + 1, lanes] * zbuf[lb_i, off:off + chunk, :]
                zcbuf[rows, lanes] = accb + bb_ref[:, lanes]

            zbuf[lb_i, 0:HALO_B, :] = zbuf[lb_i, tc:tc + HALO_B, :]
            cvbuf[lb_i, 0:HALO_A, :] = cvbuf[lb_i, tc:tc + HALO_A, :]

        cols = slice(D_A - piece, D_A)
        s = s + _dot(y_ref[:, cols], wo_ref[cols, :])

        for c in range(n_chunks):
            rows = slice(c * chunk, (c + 1) * chunk)
            zc = zcbuf[rows, :]
            zc_ref[rows, :] = zc.astype(BF16)
            mu = jnp.mean(zc, axis=-1, keepdims=True)
            xc = zc - mu
            var = jnp.mean(xc * xc, axis=-1, keepdims=True)
            ln = xc * lax.rsqrt(var + LN_EPS) * lg_ref[...] + lb_ref[...]
            y_ref[rows, D_A:D_A + D_B] = (ln * _sigmoid(ln)).astype(BF16)

        s = s + _dot(y_ref[:, D_A:D_A + D_B], wo_ref[D_A:D_A + D_B, :])
        s_ref[...] = s.astype(BF16)
        x1_ref[...] = x_ref[...] + _rms_fwd(s, go_ref[...])

    return _hosted_call(
        core, comm, name, t // tc,
        in_specs=[_tile(tc, D_IN_ALL), _whole((CONV_A_W, D_A)), _whole((CONV_B_W, D_B)), _whole((1, D_B)),
                  _whole((1, D_B)), _whole((1, D_B)), _tile(tc, d), _resident((D_A + D_B, d)), _whole((1, d))],
        out_specs=[_tile(tc, D_A + D_B), _tile(tc, D_B), _tile(tc, D_A), _tile(tc, d), _tile(tc, d)],
        out_shape=[_sds((t, D_A + D_B), BF16), _sds((t, D_B), BF16), _sds((t, D_A), BF16), _sds((t, d), F32),
                   _sds((t, d), BF16)],
        scratch_shapes=[pltpu.VMEM((D_B // LANES, HALO_B + tc, LANES), F32),
                        pltpu.VMEM((D_A // LANES, HALO_A + tc, LANES), F32), pltpu.VMEM((tc, D_B), F32)],
        operands=(u, wa, wb, bb, lg, lb, xres, w_out, g_post))


def _mem_fwd(mem, g, wk, wv, name):
    m, d = mem.shape

    def body(mem_ref, g_ref, wk_ref, wv_ref, n_ref, k_ref, v_ref):
        n = _rms_fwd(mem_ref[...], g_ref[...]).astype(BF16)
        n_ref[...] = n
        k_ref[...] = _dot(n, wk_ref[...]).astype(BF16)
        v_ref[...] = _dot(n, wv_ref[...]).astype(BF16)

    return pl.pallas_call(
        body, name=name, grid=(1,),
        in_specs=[_whole((m, d)), _whole((1, d)), _whole((d, d)), _whole((d, d))],
        out_specs=[_whole((m, d))] * 3,
        out_shape=[_sds((m, d), BF16)] * 3,
        compiler_params=_params(),
    )(mem, g, wk, wv)


def _softmax_rows(s):
    e = jnp.exp(s - jnp.max(s, axis=-1, keepdims=True))
    return e / jnp.sum(e, axis=-1, keepdims=True)


def _attn_fwd(x1, g_pre, wq, k, v, wo, g_post, tm, name):
    t, d = x1.shape
    m = k.shape[0]
    scale = XA_HEAD_DIM ** -0.5

    def body(x_ref, gp_ref, wq_ref, k_ref, v_ref, wo_ref, go_ref, x2_ref, h_ref, q_ref, o_ref, a_ref):
        x = x_ref[...]
        h = _rms_fwd(x, gp_ref[...]).astype(BF16)
        h_ref[...] = h
        q_ref[...] = _dot(h, wq_ref[...]).astype(BF16)
        for hd in range(XA_HEADS):
            cols = slice(hd * XA_HEAD_DIM, (hd + 1) * XA_HEAD_DIM)
            p = _softmax_rows(_dot_nt(q_ref[:, cols], k_ref[:, cols]) * scale)
            o_ref[:, cols] = _dot(p.astype(BF16), v_ref[:, cols]).astype(BF16)
        a = _dot(o_ref[...], wo_ref[...])
        a_ref[...] = a.astype(BF16)
        x2_ref[...] = x + _rms_fwd(a, go_ref[...])

    return pl.pallas_call(
        body, name=name, grid=(t // tm,),
        in_specs=[_tile(tm, d), _whole((1, d)), _resident((d, d)), _whole((m, d)), _whole((m, d)), _resident((d, d)),
                  _whole((1, d))],
        out_specs=[_tile(tm, d)] * 5,
        out_shape=[_sds((t, d), F32)] + [_sds((t, d), BF16)] * 4,
        compiler_params=_params(),
    )(x1, g_pre, wq, k, v, wo, g_post)


def _ffn_fwd(x2, g_pre, wg_t, wu_t, wd, g_post, target, tm, name):
    t, d = x2.shape
    f = wg_t.shape[0]

    def body(x_ref, gp_ref, wg_ref, wu_ref, wd_ref, go_ref, tgt_ref, h_ref, gt_ref, up_ref, f_ref, sq_ref):
        _zero_at_first_step(sq_ref)
        x = x_ref[...]
        h = _rms_fwd(x, gp_ref[...]).astype(BF16)
        h_ref[...] = h
        gt = _dot_nt(h, wg_ref[...])
        up = _dot_nt(h, wu_ref[...])
        gt_ref[...] = gt.astype(BF16)
        up_ref[...] = up.astype(BF16)
        hd = (gt * _sigmoid(gt) * up).astype(BF16)
        ff = _dot(hd, wd_ref[...])
        f_ref[...] = ff.astype(BF16)
        err = x + _rms_fwd(ff, go_ref[...]) - tgt_ref[...]
        sq_ref[...] += jnp.sum(err * err, axis=0, keepdims=True)

    return pl.pallas_call(
        body, name=name, grid=(t // tm,),
        in_specs=[_tile(tm, d), _whole((1, d)), _resident((f, d)), _resident((f, d)), _resident((f, d)), _whole((1, d)),
                  _tile(tm, d)],
        out_specs=[_tile(tm, d), _tile(tm, f), _tile(tm, f), _tile(tm, d), _whole((1, d))],
        out_shape=[_sds((t, d), BF16), _sds((t, f), BF16), _sds((t, f), BF16), _sds((t, d), BF16), _sds((1, d), F32)],
        compiler_params=_params(),
    )(x2, g_pre, wg_t, wu_t, wd, g_post, target)


def _ffn_bwd(x2, f, target, gt, up, g_pre, wg_t, wu_t, wd, g_post, tm, name):
    t, d = x2.shape
    ff = wg_t.shape[0]

    def body(x_ref, f_ref, tgt_ref, gt_ref, up_ref, gp_ref, wg_ref, wu_ref, wd_ref, go_ref,
             dx_ref, df_ref, hd_ref, dgt_ref, dup_ref, dgo_ref, dgp_ref):
        _zero_at_first_step(dgo_ref, dgp_ref)
        x = x_ref[...]
        fo = f_ref[...].astype(F32)
        dx3 = (x + _rms_fwd(fo, go_ref[...]) - tgt_ref[...]) * (1.0 / d)
        df, dgo = _rms_bwd(dx3, fo, go_ref[...])
        dgo_ref[...] += dgo
        df = df.astype(BF16)
        df_ref[...] = df
        dhd = _dot_nt(df, wd_ref[...])
        gt = gt_ref[...].astype(F32)
        up = up_ref[...].astype(F32)
        sg = _sigmoid(gt)
        si = gt * sg
        hd_ref[...] = (si * up).astype(BF16)
        dup = (dhd * si).astype(BF16)
        dgt = (dhd * up * (sg * (1.0 + gt * (1.0 - sg)))).astype(BF16)
        dup_ref[...] = dup
        dgt_ref[...] = dgt
        dh = _dot(dgt, wg_ref[...]) + _dot(dup, wu_ref[...])
        dxn, dgp = _rms_bwd(dh, x, gp_ref[...])
        dgp_ref[...] += dgp
        dx_ref[...] = dx3 + dxn

    return pl.pallas_call(
        body, name=name, grid=(t // tm,),
        in_specs=[_tile(tm, d), _tile(tm, d), _tile(tm, d), _tile(tm, ff), _tile(tm, ff), _whole((1, d)),
                  _resident((ff, d)), _resident((ff, d)), _resident((ff, d)), _whole((1, d))],
        out_specs=[_tile(tm, d), _tile(tm, d), _tile(tm, ff), _tile(tm, ff), _tile(tm, ff), _whole((1, d)), _whole((1, d))],
        out_shape=[_sds((t, d), F32), _sds((t, d), BF16), _sds((t, ff), BF16), _sds((t, ff), BF16), _sds((t, ff), BF16),
                   _sds((1, d), F32), _sds((1, d), F32)],
        compiler_params=_params(),
    )(x2, f, target, gt, up, g_pre, wg_t, wu_t, wd, g_post)


def _attn_bwd(dx2, a, x1, q, k, v, g_pre, wq, wo, g_post, s_mix, w_mix_out, g_mix_post, tm, name, comm):
    t, d = x1.shape
    m = k.shape[0]
    kdim = w_mix_out.shape[0]
    scale = XA_HEAD_DIM ** -0.5

    def core(dx2_ref, a_ref, x_ref, q_ref, k_ref, v_ref, gp_ref, wq_ref, wo_ref, go_ref, s_ref, wm_ref, gm_ref,
             dx1_ref, da_ref, dq_ref, dk_ref, dv_ref, dgo_ref, dgp_ref, ds_ref, dy_ref, dgm_ref, do_buf):
        _zero_at_first_step(dgo_ref, dgp_ref, dk_ref, dv_ref, dgm_ref)
        dx2 = dx2_ref[...]
        da, dgo = _rms_bwd(dx2, a_ref[...].astype(F32), go_ref[...])
        dgo_ref[...] += dgo
        da = da.astype(BF16)
        da_ref[...] = da
        do_buf[...] = _dot_nt(da, wo_ref[...]).astype(BF16)
        for hd in range(XA_HEADS):
            cols = slice(hd * XA_HEAD_DIM, (hd + 1) * XA_HEAD_DIM)
            qh = q_ref[:, cols]
            p = _softmax_rows(_dot_nt(qh, k_ref[:, cols]) * scale)
            do_h = do_buf[:, cols]
            dp = _dot_nt(do_h, v_ref[:, cols])
            dv_ref[:, cols] += _dot_tn(p.astype(BF16), do_h)
            ds = (p * (dp - jnp.sum(dp * p, axis=-1, keepdims=True)) * scale).astype(BF16)
            dq_ref[:, cols] = _dot(ds, k_ref[:, cols]).astype(BF16)
            dk_ref[:, cols] += _dot_tn(ds, qh)
        dh = _dot_nt(dq_ref[...], wq_ref[...])
        dxn, dgp = _rms_bwd(dh, x_ref[...], gp_ref[...])
        dgp_ref[...] += dgp
        dx1 = dx2 + dxn
        dx1_ref[...] = dx1
        ds, dgm = _rms_bwd(dx1, s_ref[...].astype(F32), gm_ref[...])
        dgm_ref[...] += dgm
        ds = ds.astype(BF16)
        ds_ref[...] = ds
        dy_ref[...] = _dot_nt(ds, wm_ref[...]).astype(BF16)

    return _hosted_call(
        core, comm, name, t // tm,
        in_specs=[_tile(tm, d), _tile(tm, d), _tile(tm, d), _tile(tm, d), _whole((m, d)), _whole((m, d)), _whole((1, d)),
                  _resident((d, d)), _resident((d, d)), _whole((1, d)), _tile(tm, d), _resident((kdim, d)), _whole((1, d))],
        out_specs=[_tile(tm, d), _tile(tm, d), _tile(tm, d), _whole((m, d)), _whole((m, d)), _whole((1, d)), _whole((1, d)),
                   _tile(tm, d), _tile(tm, kdim), _whole((1, d))],
        out_shape=[_sds((t, d), F32), _sds((t, d), BF16), _sds((t, d), BF16), _sds((m, d), F32), _sds((m, d), F32),
                   _sds((1, d), F32), _sds((1, d), F32), _sds((t, d), BF16), _sds((t, kdim), BF16), _sds((1, d), F32)],
        scratch_shapes=[pltpu.VMEM((tm, d), BF16)],
        operands=(dx2, a, x1, q, k, v, g_pre, wq, wo, g_post, s_mix, w_mix_out, g_mix_post))


def _mem_bwd(mem, mem_n, dk, dv, g, wk, wv, name):
    m, d = mem.shape

    def body(mem_ref, n_ref, dk_ref, dv_ref, g_ref, wk_ref, wv_ref, dwk_ref, dwv_ref, dg_ref):
        dk = dk_ref[...].astype(BF16)
        dv = dv_ref[...].astype(BF16)
        n = n_ref[...]
        dwk_ref[...] = _dot_tn(n, dk).astype(BF16)
        dwv_ref[...] = _dot_tn(n, dv).astype(BF16)
        dn = _dot_nt(dk, wk_ref[...]) + _dot_nt(dv, wv_ref[...])
        _, dg = _rms_bwd(dn, mem_ref[...], g_ref[...])
        dg_ref[...] = dg

    return pl.pallas_call(
        body, name=name, grid=(1,),
        in_specs=[_whole((m, d)), _whole((m, d)), _whole((m, d)), _whole((m, d)), _whole((1, d)), _whole((d, d)),
                  _whole((d, d))],
        out_specs=[_whole((d, d)), _whole((d, d)), _whole((1, d))],
        out_shape=[_sds((d, d), BF16), _sds((d, d), BF16), _sds((1, d), F32)],
        compiler_params=_params(),
    )(mem, mem_n, dk, dv, g, wk, wv)


def _conv_bwd(dy, u, zc, ca, wa, wb, lg, lb, tc, name, comm):
    t = u.shape[0]
    n_tiles = t // tc
    n_chunks = tc // CONV_CHUNK

    def core(dy_ref, u_ref, zc_ref, ca_ref, wa_ref, wb_ref, lg_ref, lb_ref,
             du_ref, dwa_ref, dwb_ref, dbb_ref, dlg_ref, dlb_ref, ebuf, eabuf, zbuf, cvbuf, gatebuf, wacc, aacc, vacc):
        step = pl.program_id(0)

        @pl.when(step == 0)
        def _():
            ebuf[:, tc:tc + HALO_B, :] = jnp.zeros((D_B // LANES, HALO_B, LANES), F32)
            eabuf[:, tc:tc + HALO_A, :] = jnp.zeros((D_A // LANES, HALO_A, LANES), F32)
            wacc[...] = jnp.zeros_like(wacc)
            aacc[...] = jnp.zeros_like(aacc)
            vacc[...] = jnp.zeros_like(vacc)

        dbb = jnp.zeros((SUBLANES, D_B), F32)
        dlg = jnp.zeros((SUBLANES, D_B), F32)
        dlb = jnp.zeros((SUBLANES, D_B), F32)
        for c in range(n_chunks):
            rows = slice(c * CONV_CHUNK, (c + 1) * CONV_CHUNK)
            dy_a = dy_ref[rows, 0:D_A].astype(F32)
            b_a = u_ref[rows, 0:D_A].astype(F32)
            du_ref[rows, 0:D_A] = (dy_a * ca_ref[rows, :].astype(F32)).astype(BF16)
            dca = dy_a * b_a
            cv = u_ref[rows, D_A:2 * D_A].astype(F32) * u_ref[rows, 2 * D_A:3 * D_A].astype(F32)
            gate = _sigmoid(u_ref[rows, 3 * D_A + D_B:3 * D_A + 2 * D_B].astype(F32))
            z = u_ref[rows, 3 * D_A:3 * D_A + D_B].astype(F32) * gate
            for lb_i in range(D_A // LANES):
                lanes = slice(lb_i * LANES, (lb_i + 1) * LANES)
                eabuf[lb_i, rows, :] = dca[:, lanes]
                cvbuf[lb_i, rows, :] = cv[:, lanes]
                zbuf[lb_i, rows, :] = z[:, lanes]
                gatebuf[lb_i, rows, :] = gate[:, lanes]

            zcv = zc_ref[rows, :].astype(F32)
            mu = jnp.mean(zcv, axis=-1, keepdims=True)
            xc = zcv - mu
            rstd = lax.rsqrt(jnp.mean(xc * xc, axis=-1, keepdims=True) + LN_EPS)
            xhat = xc * rstd
            ln = xhat * lg_ref[...] + lb_ref[...]
            sg = _sigmoid(ln)
            dln = dy_ref[rows, D_A:D_A + D_B].astype(F32) * (sg * (1.0 + ln * (1.0 - sg)))
            dlg = dlg + jnp.sum((dln * xhat).reshape(CONV_CHUNK // SUBLANES, SUBLANES, D_B), axis=0)
            dlb = dlb + jnp.sum(dln.reshape(CONV_CHUNK // SUBLANES, SUBLANES, D_B), axis=0)
            dxh = dln * lg_ref[...]
            dzc = rstd * (dxh - jnp.mean(dxh, axis=-1, keepdims=True) - xhat * jnp.mean(dxh * xhat, axis=-1, keepdims=True))
            dbb = dbb + jnp.sum(dzc.reshape(CONV_CHUNK // SUBLANES, SUBLANES, D_B), axis=0)
            for lb_i in range(D_B // LANES):
                ebuf[lb_i, rows, :] = dzc[:, lb_i * LANES:(lb_i + 1) * LANES]
        vacc[0] += dbb
        vacc[1] += dlg
        vacc[2] += dlb

        for lb_i in range(D_A // LANES):
            lanes = slice(lb_i * LANES, (lb_i + 1) * LANES)

            def cols(first):
                return slice(first + lb_i * LANES, first + (lb_i + 1) * LANES)

            for c in range(n_chunks):
                r0 = c * CONV_CHUNK
                rows = slice(r0, r0 + CONV_CHUNK)
                cv = cvbuf[lb_i, rows, :]
                dcv = jnp.zeros((CONV_CHUNK, LANES), F32)
                for k in range(CONV_A_W):
                    off = r0 + (CONV_A_W - 1) - k
                    e = eabuf[lb_i, off:off + CONV_CHUNK, :]
                    dcv = dcv + wa_ref[k:k + 1, lanes] * e
                    aacc[k, :, lanes] += jnp.sum((cv * e).reshape(CONV_CHUNK // SUBLANES, SUBLANES, LANES), axis=0)
                du_ref[rows, cols(D_A)] = (dcv * u_ref[rows, cols(2 * D_A)].astype(F32)).astype(BF16)
                du_ref[rows, cols(2 * D_A)] = (dcv * u_ref[rows, cols(D_A)].astype(F32)).astype(BF16)

                z = zbuf[lb_i, rows, :]
                dz = jnp.zeros((CONV_CHUNK, LANES), F32)
                for k in range(CONV_B_W):
                    off = r0 + (CONV_B_W - 1) - k
                    e = ebuf[lb_i, off:off + CONV_CHUNK, :]
                    dz = dz + wb_ref[k:k + 1, lanes] * e
                    wacc[k, :, lanes] += jnp.sum((z * e).reshape(CONV_CHUNK // SUBLANES, SUBLANES, LANES), axis=0)
                glu_v = u_ref[rows, cols(3 * D_A)].astype(F32)
                sgg = gatebuf[lb_i, rows, :]
                du_ref[rows, cols(3 * D_A)] = (dz * sgg).astype(BF16)
                du_ref[rows, cols(3 * D_A + D_B)] = (dz * glu_v * sgg * (1.0 - sgg)).astype(BF16)

            ebuf[lb_i, tc:tc + HALO_B, :] = ebuf[lb_i, 0:HALO_B, :]
            eabuf[lb_i, tc:tc + HALO_A, :] = eabuf[lb_i, 0:HALO_A, :]

        @pl.when(step == n_tiles - 1)
        def _():
            for k in range(CONV_B_W):
                dwb_ref[k:k + 1, :] = jnp.sum(wacc[k], axis=0, keepdims=True)
            for k in range(CONV_A_W):
                dwa_ref[k:k + 1, :] = jnp.sum(aacc[k], axis=0, keepdims=True)
            dbb_ref[...] = jnp.sum(vacc[0], axis=0, keepdims=True)
            dlg_ref[...] = jnp.sum(vacc[1], axis=0, keepdims=True)
            dlb_ref[...] = jnp.sum(vacc[2], axis=0, keepdims=True)

    return _hosted_call(
        core, comm, name, n_tiles,
        in_specs=[_rtile(tc, D_A + D_B, n_tiles), _rtile(tc, D_IN_ALL, n_tiles), _rtile(tc, D_B, n_tiles),
                  _rtile(tc, D_A, n_tiles), _whole((CONV_A_W, D_A)), _whole((CONV_B_W, D_B)), _whole((1, D_B)),
                  _whole((1, D_B))],
        out_specs=[_rtile(tc, D_IN_ALL, n_tiles), _whole((CONV_A_W, D_A)), _whole((CONV_B_W, D_B)), _whole((1, D_B)),
                   _whole((1, D_B)), _whole((1, D_B))],
        out_shape=[_sds((t, D_IN_ALL), BF16), _sds((CONV_A_W, D_A), F32), _sds((CONV_B_W, D_B), F32), _sds((1, D_B), F32),
                   _sds((1, D_B), F32), _sds((1, D_B), F32)],
        scratch_shapes=[pltpu.VMEM((D_B // LANES, tc + HALO_B, LANES), F32), pltpu.VMEM((D_A // LANES, tc + HALO_A, LANES), F32),
                        pltpu.VMEM((D_B // LANES, tc, LANES), F32), pltpu.VMEM((D_A // LANES, tc, LANES), F32),
                        pltpu.VMEM((D_B // LANES, tc, LANES), F32),
                        pltpu.VMEM((CONV_B_W, SUBLANES, D_B), F32), pltpu.VMEM((CONV_A_W, SUBLANES, D_A), F32),
                        pltpu.VMEM((3, SUBLANES, D_B), F32)],
        operands=(dy, u, zc, ca, wa, wb, lg, lb))


def _in_bwd(du, w_t, x, dx1, g, tm, name, after):
    t, d = x.shape
    n = w_t.shape[0]

    def body(after_ref, du_ref, w_ref, x_ref, dx1_ref, g_ref, dx_ref, dg_ref):
        del after_ref
        _zero_at_first_step(dg_ref)
        dh = _dot(du_ref[...], w_ref[...])
        dxn, dg = _rms_bwd(dh, x_ref[...], g_ref[...])
        dg_ref[...] += dg
        dx_ref[...] = dx1_ref[...] + dxn

    return pl.pallas_call(
        body, name=name, grid=(t // tm,),
        in_specs=[pl.BlockSpec(memory_space=pl.ANY), _tile(tm, n), _resident((n, d)), _tile(tm, d), _tile(tm, d),
                  _whole((1, d))],
        out_specs=[_tile(tm, d), _whole((1, d))],
        out_shape=[_sds((t, d), F32), _sds((1, d), F32)],
        compiler_params=_params(),
    )(after, du, w_t, x, dx1, g)


def _wgrad(a, b, tk, bm, bn, name, comm=None, cols=None):
    t, m = a.shape
    first, n = (0, b.shape[1]) if cols is None else cols
    first_block = first // bn
    n_k = t // tk

    def body(a_ref, b_ref, o_ref, acc):
        @pl.when(pl.program_id(2) == 0)
        def _():
            acc[...] = jnp.zeros_like(acc)

        acc[...] += _dot_tn(a_ref[...], b_ref[...])

        @pl.when(pl.program_id(2) == n_k - 1)
        def _():
            o_ref[...] = acc[...].astype(BF16)

    call = dict(
        grid=(m // bm, n // bn, n_k),
        in_specs=[pl.BlockSpec((tk, bm), lambda i, j, k: (k, i)),
                  pl.BlockSpec((tk, bn), lambda i, j, k: (k, first_block + j))],
        out_specs=[pl.BlockSpec((bm, bn), lambda i, j, k: (i, j))],
        out_shape=[_sds((m, n), BF16)],
        scratch_shapes=[pltpu.VMEM((bm, bn), F32)])
    if comm is None:
        return pl.pallas_call(body, name=name, compiler_params=_params(3), **call)(a, b)[0]
    (out,), received = _hosted_call(body, comm, name, operands=(a, b), **call)
    return out, received


def _adamw(items, tr, name, after=None):
    n_parts, r, c = items[0][0].shape
    n_items = len(items)
    order = [] if after is None else [after]

    def body(*refs):
        refs = refs[len(order):]
        ins, outs = refs[:4 * n_items], refs[4 * n_items:]
        for i in range(n_items):
            p_ref, w_ref, m_ref, v_ref = ins[4 * i:4 * i + 4]
            g_ref, d_ref, nm_ref, nv_ref = outs[4 * i:4 * i + 4]
            g = p_ref[0].astype(F32)
            for j in range(1, n_parts):
                g = g + p_ref[j].astype(F32)
            g_ref[...] = g
            nm = ADAM_B1 * m_ref[...] + (1.0 - ADAM_B1) * g
            nv = ADAM_B2 * v_ref[...] + (1.0 - ADAM_B2) * (g * g)
            nm_ref[...] = nm
            nv_ref[...] = nv
            m_hat = nm / (1.0 - ADAM_B1 ** ADAM_STEP)
            v_hat = nv / (1.0 - ADAM_B2 ** ADAM_STEP)
            d_ref[...] = -ADAM_LR * (m_hat / (jnp.sqrt(v_hat) + ADAM_EPS) + ADAM_WD * w_ref[...])

    results = pl.pallas_call(
        body, name=name, grid=(r // tr,),
        in_specs=[pl.BlockSpec(memory_space=pl.ANY)] * len(order)
        + ([pl.BlockSpec((n_parts, tr, c), lambda i: (0, i, 0))] + [_tile(tr, c)] * 3) * n_items,
        out_specs=[_tile(tr, c)] * (4 * n_items),
        out_shape=[_sds((r, c), F32)] * (4 * n_items),
        compiler_params=_params(),
    )(*order, *[a for item in items for a in item])
    return [results[4 * i:4 * i + 4] for i in range(n_items)]


def _sum_parts(parts, name):
    n_parts, r, c = parts.shape

    def body(p_ref, o_ref):
        acc = p_ref[0]
        for j in range(1, n_parts):
            acc = acc + p_ref[j]
        o_ref[...] = acc

    return pl.pallas_call(
        body, name=name, grid=(1,),
        in_specs=[_whole((n_parts, r, c))], out_specs=_whole((r, c)), out_shape=_sds((r, c), F32),
        compiler_params=_params(),
    )(parts)


def _row(v):
    return v.reshape(1, -1)


def _by_owner_rows(g):
    return g.reshape(N_DEV, g.shape[0] // N_DEV, g.shape[1])


WEIGHTS = ("mix_pre_g", "w_mix_in", "conv_a_w", "conv_b_w", "conv_b_b", "ln_b_g", "ln_b_b", "w_mix_out", "mix_post_g",
           "xa_pre_g", "mem_norm_g", "w_q", "w_k", "w_v", "w_o", "xa_post_g", "ffn_pre_g", "w_gate", "w_up", "w_down",
           "ffn_post_g")
LARGE = ("w_mix_in", "w_mix_out", "w_q", "w_k", "w_v", "w_o", "w_gate", "w_up", "w_down")
COLUMN_SHARDED = ("w_mix_in", "w_gate", "w_up")
GAINS = ("mix_pre_g", "mix_post_g", "xa_pre_g", "mem_norm_g", "xa_post_g", "ffn_pre_g", "ffn_post_g")
CHANNEL_VECTORS = ("conv_b_b", "ln_b_g", "ln_b_b")
CONV_TAPS = ("conv_a_w", "conv_b_w")
SMALL = GAINS + CHANNEL_VECTORS + CONV_TAPS
CONV_COLS_PER_DEVICE = D_A // N_DEV
TOKEN_TILE = 512
FFN_FWD_TOKEN_TILE = 512
FFN_TOKEN_TILE = 256
CONV_TOKEN_TILE = 256
WGRAD_TOKEN_TILE = 2048
SMALL_WGRAD_TOKEN_TILE = 4096
IN_BWD_TOKEN_TILE = 1024
ADAM_ROWS_PER_STEP = 64


def _lane_rows(v):
    flat = v.reshape(-1)
    tile = SUBLANES * LANES
    flat = jnp.pad(flat, (0, (-flat.shape[0]) % tile))
    return flat.reshape(-1, LANES)


def _pack_small(values, names):
    return jnp.concatenate([_lane_rows(values[n]) for n in names], axis=0)


def _unpack_small(packed, like, names):
    out, off = {}, 0
    for n in names:
        size = like[n].size
        rows = _lane_rows(like[n]).shape[0]
        out[n] = packed[off:off + rows, :].reshape(-1)[:size].reshape(like[n].shape)
        off += rows
    return out


def kernel(x, mem, mix_pre_g, w_mix_in, conv_a_w, conv_b_w, conv_b_b, ln_b_g, ln_b_b, w_mix_out, mix_post_g, xa_pre_g, mem_norm_g, w_q, w_k, w_v, w_o, xa_post_g, ffn_pre_g, w_gate, w_up, w_down, ffn_post_g, loss_target, m_mix_pre_g, m_w_mix_in, m_conv_a_w, m_conv_b_w, m_conv_b_b, m_ln_b_g, m_ln_b_b, m_w_mix_out, m_mix_post_g, m_xa_pre_g, m_mem_norm_g, m_w_q, m_w_k, m_w_v, m_w_o, m_xa_post_g, m_ffn_pre_g, m_w_gate, m_w_up, m_w_down, m_ffn_post_g, v_mix_pre_g, v_w_mix_in, v_conv_a_w, v_conv_b_w, v_conv_b_b, v_ln_b_g, v_ln_b_b, v_w_mix_out, v_mix_post_g, v_xa_pre_g, v_mem_norm_g, v_w_q, v_w_k, v_w_v, v_w_o, v_xa_post_g, v_ffn_pre_g, v_w_gate, v_w_up, v_w_down, v_ffn_post_g):
    given = dict(locals())
    w = {n: given[n] for n in WEIGHTS}
    m = {n: given["m_" + n] for n in WEIGHTS}
    v = {n: given["v_" + n] for n in WEIGHTS}
    xs, mems, target = x[0], mem[0], loss_target[0]
    t = xs.shape[0]
    tm, tm_ffn, tc, tk = min(TOKEN_TILE, t), min(FFN_TOKEN_TILE, t), min(CONV_TOKEN_TILE, t), min(WGRAD_TOKEN_TILE, t)
    tk_small = min(SMALL_WGRAD_TOKEN_TILE, t)
    g = {n: _row(w[n]) for n in GAINS}
    bb, lg, lb = (_row(w[n]) for n in CHANNEL_VECTORS)

    def shard_bf16(*names):
        return [w[n].astype(BF16) for n in names]

    def shard_bf16_t(*names):
        return [w[n].T.astype(BF16) for n in names]

    (g_mix_in,) = _comm_call(_Gather(shard_bf16_t("w_mix_in")), "gather_mixer")
    w_mix_in_t = g_mix_in.reshape(D_IN_ALL, D_MODEL)

    (h1, u), gathered = _norm_matmul(
        xs, g["mix_pre_g"], w_mix_in_t, tm, "mix_in_fwd",
        _Gather(shard_bf16("w_mix_out", "w_q", "w_k", "w_v", "w_o") + [_pack_small(w, CONV_TAPS)]))
    w_mix_out, w_q, w_k, w_v, w_o = (a.reshape(D_MODEL, D_MODEL) for a in gathered[:5])
    g_taps = gathered[5]
    taps, off = {}, 0
    for n in CONV_TAPS:
        k, cols = w[n].shape
        rows = _lane_rows(w[n]).shape[0]
        blk = g_taps[:, off:off + rows, :].reshape(N_DEV, -1)[:, :k * cols].reshape(N_DEV, k, cols)
        taps[n] = blk.transpose(1, 0, 2).reshape(k, N_DEV * cols)
        off += rows
    wa, wb = taps["conv_a_w"], taps["conv_b_w"]
    (ycat, zc, ca, x1, s1), (g_gate, g_up, g_down) = _conv_fwd(
        u, wa, wb, bb, lg, lb, xs, w_mix_out, g["mix_post_g"], tc, "conv_fwd",
        _Gather(shard_bf16_t("w_gate", "w_up") + shard_bf16("w_down")))
    w_gate_t, w_up_t, w_down = (a.reshape(D_FF, D_MODEL) for a in (g_gate, g_up, g_down))
    mem_n, kk, vv = _mem_fwd(mems, g["mem_norm_g"], w_k, w_v, "mem_fwd")
    x2, h2, q, o, a = _attn_fwd(x1, g["xa_pre_g"], w_q, kk, vv, w_o, g["xa_post_g"], tm, "attn_fwd")
    h3, gt, up, f, sq = _ffn_fwd(x2, g["ffn_pre_g"], w_gate_t, w_up_t, w_down, g["ffn_post_g"], target,
                                 min(FFN_FWD_TOKEN_TILE, t), "ffn_fwd")

    dx2, df, hd, dgt, dup, d_ffn_post, d_ffn_pre = _ffn_bwd(
        x2, f, target, gt, up, g["ffn_pre_g"], w_gate_t, w_up_t, w_down, g["ffn_post_g"], tm_ffn, "ffn_bwd")
    d_w_down = _wgrad(hd, df, tk, D_FF // 2, D_MODEL, "wgrad_down")
    d_w_gate_t = _wgrad(dgt, h3, tk, D_FF // 2, D_MODEL, "wgrad_gate")
    d_w_up_t = _wgrad(dup, h3, tk, D_FF // 2, D_MODEL, "wgrad_up")
    ffn_slabs = [_by_owner_rows(d) for d in (d_w_gate_t, d_w_up_t, d_w_down)]

    (dx1, da, dq, dk, dv, d_xa_post, d_xa_pre, ds1, dycat, d_mix_post), from_ffn = _attn_bwd(
        dx2, a, x1, q, kk, vv, g["xa_pre_g"], w_q, w_o, g["xa_post_g"], s1, w_mix_out, g["mix_post_g"], tm, "attn_bwd",
        _Exchange(ffn_slabs, [True] * 3))
    d_w_o = _wgrad(o, da, tk, D_MODEL, D_MODEL, "wgrad_o")
    d_w_q = _wgrad(h2, dq, tk, D_MODEL, D_MODEL, "wgrad_q")
    d_w_k, d_w_v, d_mem_norm = _mem_bwd(mems, mem_n, dk, dv, g["mem_norm_g"], w_k, w_v, "mem_bwd")
    d_w_mix_out = _wgrad(ycat, ds1, tk, D_MODEL, D_MODEL, "wgrad_mix_out")
    attn_slabs = [_by_owner_rows(d) for d in (d_w_mix_out, d_w_q, d_w_k, d_w_v, d_w_o)]

    (du, d_conv_a, d_conv_b, d_conv_bb, d_ln_g, d_ln_b), from_attn = _conv_bwd(
        dycat, u, zc, ca, wa, wb, lg, lb, tc, "conv_bwd", _Exchange(attn_slabs, [True] * 5))
    d_w_in_t = _wgrad(du, h1, tk, D_IN_ALL // 2, D_MODEL, "wgrad_mix_in")
    in_scatter = [True]
    in_copy, in_started = _exchange_start([_by_owner_rows(d_w_in_t)], in_scatter, "mix_in_start")
    dx, d_mix_pre = _in_bwd(du, w_mix_in_t, xs, dx1, g["mix_pre_g"], min(IN_BWD_TOKEN_TILE, t), "mix_in_bwd",
                            in_started)
    (from_in,) = _exchange_wait(in_copy, in_scatter, d_mix_pre, "mix_in_wait")
    small_grads = dict(mix_pre_g=d_mix_pre, conv_a_w=d_conv_a, conv_b_w=d_conv_b, conv_b_b=d_conv_bb, ln_b_g=d_ln_g,
                       ln_b_b=d_ln_b, mix_post_g=d_mix_post, xa_pre_g=d_xa_pre, mem_norm_g=d_mem_norm, xa_post_g=d_xa_post,
                       ffn_pre_g=d_ffn_pre, ffn_post_g=d_ffn_post, loss=sq)
    names = SMALL + ("loss",)
    tail_scatter = [False]
    tail, started = _exchange_start([_pack_small(small_grads, names)], tail_scatter, "tail_start")

    received = dict(zip(("w_gate", "w_up", "w_down"), from_ffn))
    received.update(zip(("w_mix_out", "w_q", "w_k", "w_v", "w_o"), from_attn))
    grad, delta, new_m, new_v = {}, {}, {}, {}

    def adamw_group(group, after=None):
        items = [[received[n]] + [a[n].T if n in COLUMN_SHARDED else a[n] for a in (w, m, v)] for n in group]
        rows = items[0][1].shape[0]
        results = _adamw(items, min(rows, ADAM_ROWS_PER_STEP) if len(group) > 1 else rows, "adamw_" + group[0], after)
        for n, result in zip(group, results):
            grad[n], delta[n], new_m[n], new_v[n] = [r.T if n in COLUMN_SHARDED else r for r in result]
        return results[-1][-1]

    behind = started
    for group in (("w_gate",), ("w_up",), ("w_down",), ("w_mix_out", "w_q", "w_k", "w_v", "w_o")):
        behind = adamw_group(group, behind)
    received["w_mix_in"] = from_in
    behind = adamw_group(("w_mix_in",), behind)
    (all_small,) = _exchange_wait(tail, tail_scatter, behind, "tail_wait")

    total = _unpack_small(_sum_parts(all_small, "sum_small"), small_grads, names)
    loss = jnp.sum(total.pop("loss")) * (0.5 / D_MODEL)
    first_col = _device_index() * CONV_COLS_PER_DEVICE
    for n in CONV_TAPS:
        total[n] = lax.dynamic_slice_in_dim(total[n], first_col, CONV_COLS_PER_DEVICE, axis=1)
    total = {n: total[n].reshape(w[n].shape) for n in SMALL}
    packed_small = [_pack_small(values, SMALL) for values in (total, w, m, v)]
    ((g_s, d_s, nm_s, nv_s),) = _adamw([[packed_small[0][None]] + packed_small[1:]], packed_small[0].shape[0],
                                       "adamw_small")
    for out, p in ((grad, g_s), (delta, d_s), (new_m, nm_s), (new_v, nv_s)):
        out.update(_unpack_small(p, w, SMALL))

    return (loss, dx[None], *[grad[n] for n in WEIGHTS], *[delta[n] for n in WEIGHTS], *[new_m[n] for n in WEIGHTS],
            *[new_v[n] for n in WEIGHTS])
```
